```python
import math
import jax, jax.numpy as jnp
from jax import lax
import numpy as np

D_MODEL = 1024
BATCH = 4
SEQ = 8192
DEPTH = 1
DEC_BATCH = 16
DEC_SEQ = 2048
PAST_LEN = 128

HEAD_DIM = 64
A_Q_HEADS = 8
A_KV_HEADS = 2
A_HALF_WINDOW = 128
B_GROUPS = ((128, 1), (512, 4), (2048, 16))
B_HEADS_PER_GROUP = 4
N_BUCKETS = 32
MAX_DISTANCE = 1024
N_EXPERTS = 64
TOP_K = 6
N_EXPERT_GROUPS = 8
TOPK_GROUPS = 4
D_EXPERT = 256
D_SHARED = 256
ROUTED_SCALE = 2.5
MOE_BLOCK = 128
RMS_EPS = 1e-6
NEG_INF = -1e30

A_Q_W = A_Q_HEADS * HEAD_DIM
A_KV_W = A_KV_HEADS * HEAD_DIM
B_HEADS = len(B_GROUPS) * B_HEADS_PER_GROUP
B_W = B_HEADS * HEAD_DIM
B_OUT_W = B_HEADS_PER_GROUP * HEAD_DIM
N_BIAS_HEADS = A_Q_HEADS + B_HEADS
IN_WIDTHS = (A_Q_W, A_KV_W, A_KV_W, B_W, B_W, B_W, D_MODEL, D_MODEL)
D_IN = A_Q_W + 2 * A_KV_W + 3 * B_W + 2 * D_MODEL

kernel_name = "hybrid_gated_window_dilated_moe_encoder"


def _rmsnorm(x, g):
    xf = x.astype(jnp.float32)
    y = xf * lax.rsqrt(jnp.mean(xf * xf, axis=-1, keepdims=True) + RMS_EPS)
    return (y * g.astype(jnp.float32)).astype(x.dtype)


def _rel_bucket(rel):
    half = N_BUCKETS // 2
    max_exact = half // 2
    n = np.abs(rel)
    large = max_exact + (np.log(np.maximum(n, 1) / max_exact) / math.log(MAX_DISTANCE / max_exact)
                         * (half - max_exact)).astype(np.int32)
    large = np.minimum(large, half - 1)
    return ((rel > 0).astype(np.int32) * half + np.where(n < max_exact, n, large)).astype(np.int32)


def _banded_attention(q, k, v, half, dist_scale, bias_tab, sink=None):
    b, L, kv, g, dh = q.shape
    nb = -(-L // half)
    pad = nb * half - L
    qb = jnp.pad(q, ((0, 0), (0, pad), (0, 0), (0, 0), (0, 0))).reshape(b, nb, half, kv, g, dh)
    kp = jnp.pad(k, ((0, 0), (half, half + pad), (0, 0), (0, 0))).reshape(b, nb + 2, half, kv, dh)
    vp = jnp.pad(v, ((0, 0), (half, half + pad), (0, 0), (0, 0))).reshape(b, nb + 2, half, kv, dh)
    kw = jnp.concatenate([kp[:, :-2], kp[:, 1:-1], kp[:, 2:]], axis=2)
    vw = jnp.concatenate([vp[:, :-2], vp[:, 1:-1], vp[:, 2:]], axis=2)
    s = jnp.einsum('bnqkgd,bnskd->bnkgqs', qb, kw, preferred_element_type=jnp.float32) * (dh ** -0.5)
    rel = np.arange(3 * half)[None, :] - half - np.arange(half)[:, None]
    bucket = _rel_bucket(rel * dist_scale)
    bias = jnp.transpose(bias_tab[bucket].astype(jnp.float32), (2, 0, 1)).reshape(kv, g, half, 3 * half)
    key_pos = np.arange(nb)[:, None] * half + np.arange(3 * half)[None, :] - half
    mask = (np.abs(rel) <= half)[None] & ((key_pos >= 0) & (key_pos < L))[:, None, :]
    logits = jnp.where(mask[None, :, None, None], s + bias, NEG_INF)
    m = jnp.max(logits, axis=-1)
    if sink is not None:
        sink_b = sink.astype(jnp.float32).reshape(kv, g)[..., None]
        m = jnp.maximum(m, sink_b)
    p = jnp.exp(logits - m[..., None])
    den = jnp.sum(p, axis=-1)
    if sink is not None:
        den = den + jnp.exp(sink_b - m)
    o = jnp.einsum('bnkgqs,bnskd->bnqkgd', p.astype(v.dtype), vw, preferred_element_type=jnp.float32)
    den_t = jnp.transpose(den, (0, 1, 4, 2, 3))
    o = (o / den_t[..., None]).astype(q.dtype).reshape(b, nb * half, kv, g, dh)[:, :L]
    lse = (jnp.transpose(m, (0, 1, 4, 2, 3)) + jnp.log(den_t)).reshape(b, nb * half, kv, g)[:, :L]
    return o, lse


def _dilated_branch(q, k, v, rel_bias):
    bsz, S, _ = q.shape
    outs, lses = [], []
    for gi, (w, d) in enumerate(B_GROUPS):
        sl = slice(gi * B_OUT_W, (gi + 1) * B_OUT_W)

        def to_res(t):
            t = t[..., sl].reshape(bsz, S // d, d, B_HEADS_PER_GROUP, HEAD_DIM)
            return jnp.swapaxes(t, 1, 2).reshape(bsz * d, S // d, B_HEADS_PER_GROUP, HEAD_DIM)

        qg, kg, vg = to_res(q), to_res(k), to_res(v)
        h0 = A_Q_HEADS + gi * B_HEADS_PER_GROUP
        o, lse = _banded_attention(qg[:, :, :, None, :], kg, vg, w // (2 * d), d,
                                   rel_bias[:, h0:h0 + B_HEADS_PER_GROUP])
        o = jnp.swapaxes(o.reshape(bsz, d, S // d, B_HEADS_PER_GROUP, HEAD_DIM), 1, 2)
        lse = jnp.swapaxes(lse.reshape(bsz, d, S // d, B_HEADS_PER_GROUP), 1, 2)
        outs.append(o.reshape(bsz, S, B_HEADS_PER_GROUP, HEAD_DIM))
        lses.append(lse.reshape(bsz, S, B_HEADS_PER_GROUP))
    wts = jax.nn.softmax(jnp.stack(lses), axis=0)
    ob = jnp.einsum('gbsh,gbshd->bshd', wts.astype(q.dtype), jnp.stack(outs))
    return ob.reshape(bsz, S, B_OUT_W)


def _token_mixers(h, rel_bias, w_in, sink, w_pa, w_pb, w_o):
    bsz, S, _ = h.shape
    proj = h @ w_in
    offs = np.cumsum(np.array(IN_WIDTHS))[:-1]
    qa, ka, va, qb, kb, vb, ga, gb = jnp.split(proj, offs, axis=-1)
    G = A_Q_HEADS // A_KV_HEADS
    ya, _ = _banded_attention(qa.reshape(bsz, S, A_KV_HEADS, G, HEAD_DIM),
                              ka.reshape(bsz, S, A_KV_HEADS, HEAD_DIM),
                              va.reshape(bsz, S, A_KV_HEADS, HEAD_DIM),
                              A_HALF_WINDOW, 1, rel_bias[:, :A_Q_HEADS], sink)
    ya = ya.reshape(bsz, S, A_Q_W)
    yb = _dilated_branch(qb, kb, vb, rel_bias)
    merged = jax.nn.sigmoid(ga) * (ya @ w_pa) + jax.nn.sigmoid(gb) * (yb @ w_pb)
    return merged @ w_o


def _moe(h, w_router, router_bias, w_gate, w_up, w_down, ws_gate, ws_up, ws_down):
    shape = h.shape
    x = h.reshape(-1, D_MODEL)
    n = x.shape[0]
    scores = jax.nn.sigmoid((x @ w_router).astype(jnp.float32))
    sel = scores + router_bias.astype(jnp.float32)
    grp_score = jnp.sum(lax.top_k(sel.reshape(n, N_EXPERT_GROUPS, -1), 2)[0], axis=-1)
    _, gidx = lax.top_k(grp_score, TOPK_GROUPS)
    gmask = jnp.sum(jax.nn.one_hot(gidx, N_EXPERT_GROUPS, dtype=jnp.float32), axis=1) > 0
    emask = jnp.repeat(gmask, N_EXPERTS // N_EXPERT_GROUPS, axis=1)
    _, eidx = lax.top_k(jnp.where(emask, sel, NEG_INF), TOP_K)
    wts = jnp.take_along_axis(scores, eidx, axis=1)
    wts = wts / jnp.sum(wts, axis=-1, keepdims=True) * ROUTED_SCALE
    nk = n * TOP_K
    M = MOE_BLOCK
    flat_e = eidx.reshape(-1).astype(jnp.int32)
    flat_t = jnp.repeat(jnp.arange(n, dtype=jnp.int32), TOP_K)
    flat_w = wts.reshape(-1)
    order = jnp.argsort(flat_e)
    se = flat_e[order]
    counts = jnp.bincount(flat_e, length=N_EXPERTS).astype(jnp.int32)
    padded = (counts + M - 1) // M * M
    start = jnp.cumsum(counts) - counts
    pstart = jnp.cumsum(padded) - padded
    dest = pstart[se] + jnp.arange(nk, dtype=jnp.int32) - start[se]
    nblk = -(-nk // M) + N_EXPERTS
    R = nblk * M
    row_tok = jnp.full((R,), n, jnp.int32).at[dest].set(flat_t[order])
    row_w = jnp.zeros((R,), jnp.float32).at[dest].set(flat_w[order])
    blk_end = jnp.cumsum(padded) // M
    blk_e = jnp.minimum(jnp.searchsorted(blk_end, jnp.arange(nblk), side='right'), N_EXPERTS - 1)
    x_pad = jnp.concatenate([x, jnp.zeros((1, D_MODEL), x.dtype)], axis=0)

    def expert_block(args):
        rows, e = args
        xb = x_pad[rows]
        hb = jax.nn.silu(xb @ w_gate[e]) * (xb @ w_up[e])
        return hb @ w_down[e]

    out = lax.map(expert_block, (row_tok.reshape(nblk, M), blk_e))
    out = out.reshape(R, D_MODEL) * row_w[:, None].astype(out.dtype)
    routed = jax.ops.segment_sum(out, row_tok, num_segments=n + 1)[:n]
    shared = (jax.nn.silu(x @ ws_gate) * (x @ ws_up)) @ ws_down
    return (routed + shared).reshape(shape)


def _layer(x, c, rel_bias, w_ada, b_ada, norm1, w_in, sink, w_pa, w_pb, w_o, norm2,
           w_router, router_bias, w_gate, w_up, w_down, ws_gate, ws_up, ws_down):
    mod = (jax.nn.silu(c) @ w_ada + b_ada)[:, None, :]
    sh1, sc1, g1, sh2, sc2, g2 = jnp.split(mod, 6, axis=-1)
    h = _rmsnorm(x, norm1) * (1 + sc1) + sh1
    x = x + g1 * _token_mixers(h, rel_bias, w_in, sink, w_pa, w_pb, w_o)
    h = _rmsnorm(x, norm2) * (1 + sc2) + sh2
    x = x + g2 * _moe(h, w_router, router_bias, w_gate, w_up, w_down, ws_gate, ws_up, ws_down)
    return x


def setup_inputs(seed: int = 0) -> dict:
    key = jax.random.key(seed)
    ks = jax.random.split(key, 24)
    f32 = jnp.float32
    nrm = lambda k, shape, s: jax.random.normal(k, shape, f32) * s
    L = DEPTH
    return {
        "x_prompt": nrm(ks[0], (BATCH, SEQ, D_MODEL), 1.0),
        "x_sample": nrm(ks[1], (DEC_BATCH, DEC_SEQ, D_MODEL), 1.0),
        "c_prompt": nrm(ks[2], (BATCH, D_MODEL), 1.0),
        "c_sample": nrm(ks[3], (DEC_BATCH, D_MODEL), 1.0),
        "rel_bias": nrm(ks[4], (N_BUCKETS, N_BIAS_HEADS), 0.5),
        "w_ada": nrm(ks[5], (L, D_MODEL, 6 * D_MODEL), 0.5 * D_MODEL ** -0.5),
        "b_ada": nrm(ks[6], (L, 6 * D_MODEL), 0.01),
        "norm1": 1.0 + nrm(ks[7], (L, D_MODEL), 0.02),
        "w_in": nrm(ks[8], (L, D_MODEL, D_IN), D_MODEL ** -0.5),
        "sink": nrm(ks[9], (L, A_Q_HEADS), 0.5),
        "w_pa": nrm(ks[10], (L, A_Q_W, D_MODEL), A_Q_W ** -0.5),
        "w_pb": nrm(ks[11], (L, B_OUT_W, D_MODEL), B_OUT_W ** -0.5),
        "w_o": nrm(ks[12], (L, D_MODEL, D_MODEL), D_MODEL ** -0.5),
        "norm2": 1.0 + nrm(ks[13], (L, D_MODEL), 0.02),
        "w_router": nrm(ks[14], (L, D_MODEL, N_EXPERTS), D_MODEL ** -0.5),
        "router_bias": nrm(ks[15], (L, N_EXPERTS), 0.01),
        "w_gate": nrm(ks[16], (L, N_EXPERTS, D_MODEL, D_EXPERT), D_MODEL ** -0.5),
        "w_up": nrm(ks[17], (L, N_EXPERTS, D_MODEL, D_EXPERT), D_MODEL ** -0.5),
        "w_down": nrm(ks[18], (L, N_EXPERTS, D_EXPERT, D_MODEL), D_EXPERT ** -0.5),
        "ws_gate": nrm(ks[19], (L, D_MODEL, D_SHARED), D_MODEL ** -0.5),
        "ws_up": nrm(ks[20], (L, D_MODEL, D_SHARED), D_MODEL ** -0.5),
        "ws_down": nrm(ks[21], (L, D_SHARED, D_MODEL), D_SHARED ** -0.5),
        "final_norm": 1.0 + nrm(ks[22], (D_MODEL,), 0.02),
    }


def reference(x_prompt, x_sample, c_prompt, c_sample, rel_bias, w_ada, b_ada, norm1, w_in, sink,
              w_pa, w_pb, w_o, norm2, w_router, router_bias, w_gate, w_up, w_down,
              ws_gate, ws_up, ws_down, final_norm):
    def trunk(x, c):
        for l in range(DEPTH):
            x = _layer(x, c, rel_bias, w_ada[l], b_ada[l], norm1[l], w_in[l], sink[l], w_pa[l],
                       w_pb[l], w_o[l], norm2[l], w_router[l], router_bias[l], w_gate[l], w_up[l],
                       w_down[l], ws_gate[l], ws_up[l], ws_down[l])
        return _rmsnorm(x, final_norm)

    y_prompt = trunk(x_prompt, c_prompt)
    y_sample = trunk(x_sample, c_sample)
    return (y_prompt, y_sample)
```

```python
import functools
import math

import jax
import jax.numpy as jnp
import numpy as np
from jax import lax
from jax.experimental import pallas as pl
from jax.experimental.pallas import tpu as pltpu

F32 = jnp.float32
BF16 = jnp.bfloat16
U32 = jnp.uint32
I32 = jnp.int32

D_MODEL = 1024
HEAD_DIM = 64
A_Q_HEADS = 8
A_KV_HEADS = 2
A_HALF_WINDOW = 128
B_GROUPS = ((128, 1), (512, 4), (2048, 16))
B_HEADS_PER_GROUP = 4
N_BUCKETS = 32
MAX_DISTANCE = 1024
N_EXPERTS = 64
TOP_K = 6
N_EXPERT_GROUPS = 8
TOPK_GROUPS = 4
D_EXPERT = 256
D_SHARED = 256
ROUTED_SCALE = 2.5
RMS_EPS = 1e-6
NEG_INF = -1e30
REMOVED = -3e38

A_Q_W = A_Q_HEADS * HEAD_DIM
A_KV_W = A_KV_HEADS * HEAD_DIM
B_W = len(B_GROUPS) * B_HEADS_PER_GROUP * HEAD_DIM
B_OUT_W = B_HEADS_PER_GROUP * HEAD_DIM
IN_WIDTHS = (A_Q_W, A_KV_W, A_KV_W, B_W, B_W, B_W, D_MODEL, D_MODEL)
D_IN = sum(IN_WIDTHS)
HALF_D = D_MODEL // 2

TM_INPROJ = 512
TM_MERGE = 256
ATTN_QB = 128
MOE_BM = 256
TM_DISPATCH = 256
TM_COMBINE = 256
VMEM_LIMIT = 48 * 1024 * 1024


def _cparams(n_axes):
    return pltpu.CompilerParams(
        dimension_semantics=("arbitrary",) * n_axes, vmem_limit_bytes=VMEM_LIMIT)


def _ada_kernel(c_ref, w_ref, b_ref, o_ref):
    c = c_ref[...]
    s = c * jax.nn.sigmoid(c)
    o_ref[...] = jnp.dot(s, w_ref[...], preferred_element_type=F32,
                         precision=lax.Precision.HIGHEST) + b_ref[...]


def _ada(c_all, w_ada, b_ada):
    nb = c_all.shape[0]
    return pl.pallas_call(
        _ada_kernel,
        out_shape=jax.ShapeDtypeStruct((nb, 6 * D_MODEL), F32),
        grid=(6,),
        in_specs=[pl.BlockSpec((nb, D_MODEL), lambda j: (0, 0)),
                  pl.BlockSpec((D_MODEL, D_MODEL), lambda j: (0, j)),
                  pl.BlockSpec((1, D_MODEL), lambda j: (0, j))],
        out_specs=pl.BlockSpec((nb, D_MODEL), lambda j: (0, j)),
        compiler_params=_cparams(1),
        name="ada",
    )(c_all, w_ada, b_ada.reshape(1, 6 * D_MODEL))


def _rms(x, g):
    return x * lax.rsqrt(jnp.mean(x * x, axis=-1, keepdims=True) + RMS_EPS) * g


def _inproj_kernel(x_ref, mod_ref, n1_ref, w_ref, qa, ka, va, qb, kb, vb, sga, sgb):
    mod = mod_ref[0]
    h = _rms(x_ref[...], n1_ref[...]) * (1.0 + mod[1:2]) + mod[0:1]
    hb = h.astype(BF16)
    scale = HEAD_DIM ** -0.5
    off = 0
    for ref, width, kind in ((qa, A_Q_W, "q"), (ka, A_KV_W, ""), (va, A_KV_W, ""),
                             (qb, B_W, "q"), (kb, B_W, ""), (vb, B_W, ""),
                             (sga, D_MODEL, "g"), (sgb, D_MODEL, "g")):
        r = jnp.dot(hb, w_ref[:, off:off + width], preferred_element_type=F32)
        if kind == "q":
            r = r * scale
        elif kind == "g":
            r = jax.nn.sigmoid(r)
        ref[...] = r.astype(ref.dtype)
        off += width


def _inproj(x2, mod3, norm1, w_in_bf, seq):
    n = x2.shape[0]
    tm = TM_INPROJ
    assert seq % tm == 0 and n % tm == 0
    row = lambda i: (i, 0)
    return pl.pallas_call(
        _inproj_kernel,
        out_shape=[jax.ShapeDtypeStruct((n, w), BF16) for w in IN_WIDTHS],
        grid=(n // tm,),
        in_specs=[pl.BlockSpec((tm, D_MODEL), row),
                  pl.BlockSpec((1, 6, D_MODEL), lambda i: (i * tm // seq, 0, 0)),
                  pl.BlockSpec((1, D_MODEL), lambda i: (0, 0)),
                  pl.BlockSpec((D_MODEL, D_IN), lambda i: (0, 0))],
        out_specs=[pl.BlockSpec((tm, w), row) for w in IN_WIDTHS],
        compiler_params=_cparams(1),
        name="inproj",
    )(x2, mod3, norm1.reshape(1, D_MODEL), w_in_bf)


def _rel_bucket_np(rel):
    half = N_BUCKETS // 2
    max_exact = half // 2
    n = np.abs(rel)
    large = max_exact + (np.log(np.maximum(n, 1) / max_exact) / math.log(MAX_DISTANCE / max_exact)
                         * (half - max_exact)).astype(np.int32)
    large = np.minimum(large, half - 1)
    return ((rel > 0).astype(np.int32) * half + np.where(n < max_exact, n, large)).astype(np.int32)


def _bias_table(rel_bias_heads, half, dist_scale):
    rel = np.arange(3 * half)[None, :] - half - np.arange(half)[:, None]
    bucket = _rel_bucket_np(rel * dist_scale)
    band = np.abs(rel) <= half
    tab = jnp.transpose(rel_bias_heads[bucket].astype(F32), (2, 0, 1))
    return jnp.where(band[None], tab, NEG_INF)


def _attn_kernel(*refs, half, nsub, n_q_heads, n_kv_heads, has_sink, want_lse, n_qblocks):
    it = iter(refs)
    q_ref = next(it)
    kp, kc, kn, vp, vc, vn = (next(it) for _ in range(6))
    bias_ref = next(it)
    sink_ref = next(it) if has_sink else None
    o_ref = next(it)
    lse_ref = next(it) if want_lse else None
    grp = n_q_heads // n_kv_heads
    hd = HEAD_DIM
    step = pl.program_id(2)

    kcat = jnp.concatenate([kp[...], kc[...], kn[...]], axis=0)
    vcat = jnp.concatenate([vp[...], vc[...], vn[...]], axis=0)
    col = lax.broadcasted_iota(I32, (1, 3 * half), 1)
    for j in range(nsub):
        qblk = step * nsub + j
        lo = jnp.where(qblk == 0, half, 0)
        hi = jnp.where(qblk == n_qblocks - 1, 2 * half, 3 * half)
        pen = jnp.where((col >= lo) & (col < hi), 0.0, NEG_INF)
        rows = slice(j * half, (j + 1) * half)
        for kv in range(n_kv_heads):
            kk = kcat[j * half:(j + 3) * half, kv * hd:(kv + 1) * hd]
            vv = vcat[j * half:(j + 3) * half, kv * hd:(kv + 1) * hd]
            heads = [kv * grp + g for g in range(grp)]
            qs = jnp.concatenate([q_ref[rows, h * hd:(h + 1) * hd] for h in heads], axis=0)
            s = lax.dot_general(qs, kk, (((1,), (1,)), ((), ())), preferred_element_type=F32)
            s = s + bias_ref[kv] + pen
            m = jnp.max(s, axis=-1, keepdims=True)
            if has_sink:
                sk = sink_ref[kv]
                m = jnp.maximum(m, sk)
            p = jnp.exp(s - m)
            den = jnp.sum(p, axis=-1, keepdims=True)
            if has_sink:
                den = den + jnp.exp(sk - m)
            o = jnp.dot(p.astype(BF16), vv, preferred_element_type=F32) / den
            if want_lse:
                lse = m + jnp.log(den)
            for g, h in enumerate(heads):
                o_ref[rows, h * hd:(h + 1) * hd] = o[g * half:(g + 1) * half].astype(o_ref.dtype)
                if want_lse:
                    lse_ref[rows, h * hd:(h + 1) * hd] = jnp.broadcast_to(
                        lse[g * half:(g + 1) * half], (half, hd))


def _band_attention(q, k, v, bias, sink, *, nbatch, seq, dil, half, n_q_heads, n_kv_heads,
                    q_col, kv_col, want_lse):
    n = nbatch * seq
    sub_len = seq // dil
    qb = ATTN_QB
    assert sub_len % qb == 0 and qb % half == 0
    nsub = qb // half
    nq = sub_len // qb
    qw = n_q_heads * HEAD_DIM
    kw = n_kv_heads * HEAD_DIM
    total_half_blocks = n // dil // half
    view = lambda a: a.reshape(n // dil, dil * a.shape[1])

    def cur(col):
        return lambda b, r, i: (b * nq + i, col(r))

    def prev(col):
        return lambda b, r, i: (jnp.maximum((b * nq + i) * nsub - 1, 0), col(r))

    def nxt(col):
        return lambda b, r, i: (jnp.minimum((b * nq + i + 1) * nsub, total_half_blocks - 1), col(r))

    in_specs = [pl.BlockSpec((qb, qw), cur(q_col))]
    for _ in range(2):
        in_specs += [pl.BlockSpec((half, kw), prev(kv_col)),
                     pl.BlockSpec((qb, kw), cur(kv_col)),
                     pl.BlockSpec((half, kw), nxt(kv_col))]
    in_specs.append(pl.BlockSpec(bias.shape, lambda b, r, i: (0, 0, 0)))
    args = [view(q), view(k), view(k), view(k), view(v), view(v), view(v), bias]
    if sink is not None:
        in_specs.append(pl.BlockSpec(sink.shape, lambda b, r, i: (0, 0, 0)))
        args.append(sink)
    out_col = lambda r: r
    out_shape = [jax.ShapeDtypeStruct((n // dil, dil * qw), BF16)]
    out_specs = [pl.BlockSpec((qb, qw), cur(out_col))]
    if want_lse:
        out_shape.append(jax.ShapeDtypeStruct((n // dil, dil * qw), F32))
        out_specs.append(pl.BlockSpec((qb, qw), cur(out_col)))
    outs = pl.pallas_call(
        functools.partial(_attn_kernel, half=half, nsub=nsub, n_q_heads=n_q_heads,
                          n_kv_heads=n_kv_heads, has_sink=sink is not None, want_lse=want_lse,
                          n_qblocks=sub_len // half),
        out_shape=out_shape,
        grid=(nbatch, dil, nq),
        in_specs=in_specs,
        out_specs=out_specs,
        compiler_params=_cparams(3),
        name=f"band_attn_d{dil}",
    )(*args)
    return [o.reshape(n, qw) for o in outs]


def _pack_rows(y):
    lo = pltpu.bitcast(y[:, :HALF_D].astype(BF16).astype(F32), U32)
    hi = pltpu.bitcast(y[:, HALF_D:].astype(BF16).astype(F32), U32)
    return (hi & jnp.uint32(0xFFFF0000)) | (lo >> 16)


def _unpack_rows(p):
    lo = pltpu.bitcast(p << 16, F32)
    hi = pltpu.bitcast(p & jnp.uint32(0xFFFF0000), F32)
    return lo, hi


def _route(sel, scores):
    t = sel.shape[-1]
    per = N_EXPERTS // N_EXPERT_GROUPS
    shape3 = (N_EXPERT_GROUPS, per, t)
    sel3 = sel.reshape(shape3)
    sc3 = scores.reshape(shape3)
    iota_g = lax.broadcasted_iota(I32, shape3, 0)
    iota_m = lax.broadcasted_iota(I32, shape3, 1)
    iota_e = iota_g * per + iota_m
    m1 = jnp.max(sel3, axis=1, keepdims=True)
    i1 = jnp.min(jnp.where(sel3 == m1, iota_m, per), axis=1, keepdims=True)
    m2 = jnp.max(jnp.where(iota_m == i1, REMOVED, sel3), axis=1, keepdims=True)
    gscore = m1 + m2
    iota_g1 = lax.broadcasted_iota(I32, gscore.shape, 0)
    gmask = jnp.zeros(gscore.shape, jnp.bool_)
    for _ in range(TOPK_GROUPS):
        mx = jnp.max(gscore, axis=0, keepdims=True)
        ix = jnp.min(jnp.where(gscore == mx, iota_g1, N_EXPERT_GROUPS), axis=0, keepdims=True)
        hit = iota_g1 == ix
        gmask = gmask | hit
        gscore = jnp.where(hit, REMOVED, gscore)
    cur = jnp.where(gmask, sel3, NEG_INF)
    idxs, wts = [], []
    for _ in range(TOP_K):
        mx = jnp.max(jnp.max(cur, axis=1, keepdims=True), axis=0, keepdims=True)
        cand = jnp.where(cur == mx, iota_e, N_EXPERTS)
        ix = jnp.min(jnp.min(cand, axis=1, keepdims=True), axis=0, keepdims=True)
        hit = iota_e == ix
        w = jnp.sum(jnp.sum(jnp.where(hit, sc3, 0.0), axis=1, keepdims=True), axis=0, keepdims=True)
        cur = jnp.where(hit, REMOVED, cur)
        idxs.append(ix.reshape(1, t))
        wts.append(w.reshape(1, t))
    wsum = wts[0]
    for w in wts[1:]:
        wsum = wsum + w
    wts = [w / wsum * ROUTED_SCALE for w in wts]
    return idxs, wts


def _merge_kernel(x_ref, ya_ref, o1, o2, o3, l1, l2, l3, sga_ref, sgb_ref, mod_ref, n2_ref,
                  wpa, wpb, wo, wrt, rbias, wsg, wsu, wsd,
                  base_ref, hp_ref, ridx_ref, rw_ref):
    mod = mod_ref[0]
    ls = [l1[...], l2[...], l3[...]]
    mx = jnp.maximum(jnp.maximum(ls[0], ls[1]), ls[2])
    es = [jnp.exp(l - mx) for l in ls]
    den = es[0] + es[1] + es[2]
    ob = (es[0] / den) * o1[...].astype(F32) + (es[1] / den) * o2[...].astype(F32) \
        + (es[2] / den) * o3[...].astype(F32)
    pa = jnp.dot(ya_ref[...], wpa[...], preferred_element_type=F32)
    pb = jnp.dot(ob.astype(BF16), wpb[...], preferred_element_type=F32)
    merged = sga_ref[...].astype(F32) * pa + sgb_ref[...].astype(F32) * pb
    mix = jnp.dot(merged.astype(BF16), wo[...], preferred_element_type=F32)
    x1 = x_ref[...] + mod[2:3] * mix
    h2 = _rms(x1, n2_ref[...]) * (1.0 + mod[4:5]) + mod[3:4]
    logits = lax.dot_general(wrt[...], h2, (((1,), (1,)), ((), ())), preferred_element_type=F32,
                             precision=lax.Precision.HIGHEST)
    scores = jax.nn.sigmoid(logits)
    idxs, wts = _route(scores + rbias[...], scores)
    t = h2.shape[0]
    for k in range(TOP_K):
        ridx_ref[k:k + 1, :] = idxs[k]
        rw_ref[k:k + 1, :] = wts[k]
    ridx_ref[TOP_K:, :] = jnp.zeros((8 - TOP_K, t), I32)
    rw_ref[TOP_K:, :] = jnp.zeros((8 - TOP_K, t), F32)
    hb = h2.astype(BF16)
    g = jnp.dot(hb, wsg[...], preferred_element_type=F32)
    u = jnp.dot(hb, wsu[...], preferred_element_type=F32)
    act = (g * jax.nn.sigmoid(g) * u).astype(BF16)
    shared = jnp.dot(act, wsd[...], preferred_element_type=F32)
    base_ref[...] = x1 + mod[5:6] * shared
    hp_ref[...] = _pack_rows(h2)


def _merge(x2, ya, outs, lses, sga, sgb, mod3, norm2, wpa, wpb, wo, wrt, rbias, wsg, wsu, wsd, seq):
    n = x2.shape[0]
    tm = TM_MERGE
    assert seq % tm == 0
    row = lambda i: (i, 0)
    full = lambda a: pl.BlockSpec(a.shape, lambda i: (0,) * a.ndim)
    weights = [wpa, wpb, wo, wrt, rbias, wsg, wsu, wsd]
    return pl.pallas_call(
        _merge_kernel,
        out_shape=[jax.ShapeDtypeStruct((n, D_MODEL), F32),
                   jax.ShapeDtypeStruct((n, HALF_D), U32),
                   jax.ShapeDtypeStruct((8, n), I32),
                   jax.ShapeDtypeStruct((8, n), F32)],
        grid=(n // tm,),
        in_specs=[pl.BlockSpec((tm, D_MODEL), row), pl.BlockSpec((tm, A_Q_W), row)]
                 + [pl.BlockSpec((tm, B_OUT_W), row)] * 6
                 + [pl.BlockSpec((tm, D_MODEL), row)] * 2
                 + [pl.BlockSpec((1, 6, D_MODEL), lambda i: (i * tm // seq, 0, 0)),
                    pl.BlockSpec((1, D_MODEL), lambda i: (0, 0))]
                 + [full(w) for w in weights],
        out_specs=[pl.BlockSpec((tm, D_MODEL), row), pl.BlockSpec((tm, HALF_D), row),
                   pl.BlockSpec((8, tm), lambda i: (0, i)), pl.BlockSpec((8, tm), lambda i: (0, i))],
        compiler_params=_cparams(1),
        name="merge_route",
    )(x2, ya, *outs, *lses, sga, sgb, mod3, norm2.reshape(1, D_MODEL), *weights)


def _dispatch_kernel(dest_ref, h_ref, xs_in, xs_out, sem, *, tm):
    del xs_in

    def issue(i, carry):
        for k in range(TOP_K):
            d = dest_ref[0, 0, k * tm + i]
            pltpu.make_async_copy(h_ref.at[pl.ds(i, 1)], xs_out.at[pl.ds(d, 1)], sem).start()
        return carry

    lax.fori_loop(0, tm, issue, 0)

    def drain(i, carry):
        for k in range(TOP_K):
            pltpu.make_async_copy(h_ref.at[pl.ds(0, 1)], xs_out.at[pl.ds(0, 1)], sem).wait()
        return carry

    lax.fori_loop(0, tm, drain, 0)


def _dispatch(hp, dest_tiles, n_rows):
    n = hp.shape[0]
    tm = TM_DISPATCH
    zeros = jnp.zeros((n_rows, HALF_D), U32)
    return pl.pallas_call(
        functools.partial(_dispatch_kernel, tm=tm),
        out_shape=jax.ShapeDtypeStruct((n_rows, HALF_D), U32),
        grid=(n // tm,),
        in_specs=[pl.BlockSpec((1, 1, TOP_K * tm), lambda i: (i, 0, 0), memory_space=pltpu.SMEM),
                  pl.BlockSpec((tm, HALF_D), lambda i: (i, 0)),
                  pl.BlockSpec(memory_space=pl.ANY)],
        out_specs=pl.BlockSpec(memory_space=pl.ANY),
        scratch_shapes=[pltpu.SemaphoreType.DMA],
        input_output_aliases={2: 0},
        compiler_params=_cparams(1),
        name="moe_dispatch",
    )(dest_tiles, hp, zeros)


def _expert_kernel(blk_e_ref, nused_ref, xs_ref, wgu_ref, wd_ref, ys_ref):
    del blk_e_ref
    i = pl.program_id(0)

    @pl.when(i < nused_ref[0])
    def _():
        lo, hi = _unpack_rows(xs_ref[...])
        wgu = wgu_ref[0]
        gu = jnp.dot(lo.astype(BF16), wgu[:HALF_D], preferred_element_type=F32) \
            + jnp.dot(hi.astype(BF16), wgu[HALF_D:], preferred_element_type=F32)
        g = gu[:, :D_EXPERT]
        u = gu[:, D_EXPERT:]
        act = (g * jax.nn.sigmoid(g) * u).astype(BF16)
        ys_ref[...] = _pack_rows(jnp.dot(act, wd_ref[0], preferred_element_type=F32))

    @pl.when(i >= nused_ref[0])
    def _():
        ys_ref[...] = jnp.zeros(ys_ref.shape, U32)


def _experts(xs, blk_e, nused, wgu, wd):
    n_rows = xs.shape[0]
    bm = MOE_BM
    nblk = n_rows // bm
    return pl.pallas_call(
        _expert_kernel,
        out_shape=jax.ShapeDtypeStruct((n_rows, HALF_D), U32),
        grid_spec=pltpu.PrefetchScalarGridSpec(
            num_scalar_prefetch=2,
            grid=(nblk,),
            in_specs=[pl.BlockSpec((bm, HALF_D), lambda i, be, nu: (i, 0)),
                      pl.BlockSpec((1, D_MODEL, 2 * D_EXPERT), lambda i, be, nu: (be[i], 0, 0)),
                      pl.BlockSpec((1, D_EXPERT, D_MODEL), lambda i, be, nu: (be[i], 0, 0))],
            out_specs=pl.BlockSpec((bm, HALF_D), lambda i, be, nu: (i, 0))),
        compiler_params=_cparams(1),
        name="moe_experts",
    )(blk_e, nused, xs, wgu, wd)


def _combine_kernel(dest_ref, ys_ref, base_ref, rw_ref, mod_ref, fn_ref, y_ref, buf, sem, *, tm):
    def issue(i, carry):
        for k in range(TOP_K):
            d = dest_ref[0, 0, k * tm + i]
            pltpu.make_async_copy(ys_ref.at[pl.ds(d, 1)], buf.at[k, pl.ds(i, 1)], sem).start()
        return carry

    lax.fori_loop(0, tm, issue, 0)

    def drain(i, carry):
        for k in range(TOP_K):
            pltpu.make_async_copy(ys_ref.at[pl.ds(0, 1)], buf.at[k, pl.ds(0, 1)], sem).wait()
        return carry

    lax.fori_loop(0, tm, drain, 0)

    rw = rw_ref[...]
    acc_lo = jnp.zeros((tm, HALF_D), F32)
    acc_hi = jnp.zeros((tm, HALF_D), F32)
    for k in range(TOP_K):
        lo, hi = _unpack_rows(buf[k])
        w = rw[:, k:k + 1]
        acc_lo = acc_lo + w * lo
        acc_hi = acc_hi + w * hi
    routed = jnp.concatenate([acc_lo, acc_hi], axis=1)
    x2 = base_ref[...] + mod_ref[0][5:6] * routed
    y_ref[...] = _rms(x2, fn_ref[...])


def _combine(ys, dest_tiles, base, rw_t, mod3, final_norm, seq):
    n = base.shape[0]
    tm = TM_COMBINE
    assert seq % tm == 0
    return pl.pallas_call(
        functools.partial(_combine_kernel, tm=tm),
        out_shape=jax.ShapeDtypeStruct((n, D_MODEL), F32),
        grid=(n // tm,),
        in_specs=[pl.BlockSpec((1, 1, TOP_K * tm), lambda i: (i, 0, 0), memory_space=pltpu.SMEM),
                  pl.BlockSpec(memory_space=pl.ANY),
                  pl.BlockSpec((tm, D_MODEL), lambda i: (i, 0)),
                  pl.BlockSpec((tm, 8), lambda i: (i, 0)),
                  pl.BlockSpec((1, 6, D_MODEL), lambda i: (i * tm // seq, 0, 0)),
                  pl.BlockSpec((1, D_MODEL), lambda i: (0, 0))],
        out_specs=pl.BlockSpec((tm, D_MODEL), lambda i: (i, 0)),
        scratch_shapes=[pltpu.VMEM((TOP_K, tm, HALF_D), U32), pltpu.SemaphoreType.DMA],
        compiler_params=_cparams(1),
        name="moe_combine",
    )(dest_tiles, ys, base, rw_t, mod3, final_norm.reshape(1, D_MODEL))


def _dispatch_plan(ridx, n):
    bm = MOE_BM
    e = ridx[:TOP_K]
    onehot = jnp.sum((e[:, :, None] == jnp.arange(N_EXPERTS, dtype=I32)).astype(I32), axis=0)
    csum = jnp.cumsum(onehot, axis=0)
    counts = csum[-1]
    rank = csum - onehot
    padded = (counts + bm - 1) // bm * bm
    pend = jnp.cumsum(padded)
    pstart = pend - padded
    dest = pstart[e] + jnp.take_along_axis(rank, e.T, axis=1).T
    nblk = -(-n * TOP_K // bm) + N_EXPERTS
    blk_e = jnp.minimum(jnp.searchsorted(pend // bm, jnp.arange(nblk, dtype=I32), side="right"),
                        N_EXPERTS - 1).astype(I32)
    nused = (pend[-1] // bm).astype(I32).reshape(1)
    return dest.astype(I32), blk_e, nused, nblk * bm


def _tile_dest(dest, tm):
    n = dest.shape[1]
    return dest.reshape(TOP_K, n // tm, tm).transpose(1, 0, 2).reshape(n // tm, 1, TOP_K * tm)


def _trunk(x, mod3, p):
    nbatch, seq, _ = x.shape
    n = nbatch * seq
    x2 = x.reshape(n, D_MODEL)
    qa, ka, va, qb, kb, vb, sga, sgb = _inproj(x2, mod3, p["norm1"], p["w_in"], seq)
    (ya,) = _band_attention(qa, ka, va, p["bias_a"], p["sink_a"], nbatch=nbatch, seq=seq, dil=1,
                            half=A_HALF_WINDOW, n_q_heads=A_Q_HEADS, n_kv_heads=A_KV_HEADS,
                            q_col=lambda r: 0, kv_col=lambda r: 0, want_lse=False)
    outs, lses = [], []
    for gi, (w, d) in enumerate(B_GROUPS):
        col = lambda r, gi=gi: r * len(B_GROUPS) + gi
        o, lse = _band_attention(qb, kb, vb, p["bias_b"][gi], None, nbatch=nbatch, seq=seq, dil=d,
                                 half=w // (2 * d), n_q_heads=B_HEADS_PER_GROUP,
                                 n_kv_heads=B_HEADS_PER_GROUP, q_col=col, kv_col=col, want_lse=True)
        outs.append(o)
        lses.append(lse)
    base, hp, ridx, rw = _merge(x2, ya, outs, lses, sga, sgb, mod3, p["norm2"], p["w_pa"], p["w_pb"],
                                p["w_o"], p["w_rt"], p["rbias"], p["ws_gate"], p["ws_up"],
                                p["ws_down"], seq)
    dest, blk_e, nused, n_rows = _dispatch_plan(ridx, n)
    xs = _dispatch(hp, _tile_dest(dest, TM_DISPATCH), n_rows)
    ys = _experts(xs, blk_e, nused, p["w_gu"], p["w_down"])
    y = _combine(ys, _tile_dest(dest, TM_COMBINE), base, rw.T, mod3, p["final_norm"], seq)
    return y.reshape(nbatch, seq, D_MODEL)


def kernel(x_prompt, x_sample, c_prompt, c_sample, rel_bias, w_ada, b_ada, norm1, w_in, sink, w_pa, w_pb, w_o, norm2, w_router, router_bias, w_gate, w_up, w_down, ws_gate, ws_up, ws_down, final_norm):
    assert w_ada.shape[0] == 1
    nbp = x_prompt.shape[0]
    mod = _ada(jnp.concatenate([c_prompt, c_sample], axis=0), w_ada[0], b_ada[0])
    mod3 = mod.reshape(-1, 6, D_MODEL)
    grp = A_Q_HEADS // A_KV_HEADS
    bias_a = _bias_table(rel_bias[:, :A_Q_HEADS], A_HALF_WINDOW, 1)
    bias_b = []
    for gi, (w, d) in enumerate(B_GROUPS):
        h0 = A_Q_HEADS + gi * B_HEADS_PER_GROUP
        bias_b.append(_bias_table(rel_bias[:, h0:h0 + B_HEADS_PER_GROUP], w // (2 * d), d))
    p = {
        "norm1": norm1[0], "norm2": norm2[0], "final_norm": final_norm,
        "w_in": w_in[0].astype(BF16),
        "bias_a": bias_a.reshape(A_KV_HEADS, grp * A_HALF_WINDOW, 3 * A_HALF_WINDOW),
        "sink_a": jnp.repeat(sink[0].astype(F32), A_HALF_WINDOW).reshape(
            A_KV_HEADS, grp * A_HALF_WINDOW, 1),
        "bias_b": bias_b,
        "w_pa": w_pa[0].astype(BF16), "w_pb": w_pb[0].astype(BF16), "w_o": w_o[0].astype(BF16),
        "w_rt": w_router[0].T, "rbias": router_bias[0].reshape(N_EXPERTS, 1),
        "ws_gate": ws_gate[0].astype(BF16), "ws_up": ws_up[0].astype(BF16),
        "ws_down": ws_down[0].astype(BF16),
        "w_gu": jnp.concatenate([w_gate[0], w_up[0]], axis=-1).astype(BF16),
        "w_down": w_down[0].astype(BF16),
    }
    y_prompt = _trunk(x_prompt, mod3[:nbp], p)
    y_sample = _trunk(x_sample, mod3[nbp:], p)
    return (y_prompt, y_sample)
```

```python
import functools
import math

import jax
import jax.numpy as jnp
import numpy as np
from jax import lax
from jax.experimental import pallas as pl
from jax.experimental.pallas import tpu as pltpu

F32 = jnp.float32
BF16 = jnp.bfloat16
U32 = jnp.uint32
I32 = jnp.int32

D_MODEL = 1024
HEAD_DIM = 64
A_Q_HEADS = 8
A_KV_HEADS = 2
A_HALF_WINDOW = 128
B_GROUPS = ((128, 1), (512, 4), (2048, 16))
B_HEADS_PER_GROUP = 4
N_BUCKETS = 32
MAX_DISTANCE = 1024
N_EXPERTS = 64
TOP_K = 6
N_EXPERT_GROUPS = 8
TOPK_GROUPS = 4
D_EXPERT = 256
D_SHARED = 256
ROUTED_SCALE = 2.5
RMS_EPS = 1e-6
NEG_INF = -1e30
REMOVED = -3e38

A_Q_W = A_Q_HEADS * HEAD_DIM
A_KV_W = A_KV_HEADS * HEAD_DIM
B_W = len(B_GROUPS) * B_HEADS_PER_GROUP * HEAD_DIM
B_OUT_W = B_HEADS_PER_GROUP * HEAD_DIM
IN_WIDTHS = (A_Q_W, A_KV_W, A_KV_W, B_W, B_W, B_W, D_MODEL, D_MODEL)
D_IN = sum(IN_WIDTHS)
HALF_D = D_MODEL // 2
LANES = 128

TM_INPROJ = 512
TM_MERGE = 256
ATTN_QB = 128
MOE_BM = 256
TM_DISPATCH = 256
TM_COMBINE = 256
VMEM_LIMIT = 48 * 1024 * 1024


def _cparams(n_axes):
    return pltpu.CompilerParams(
        dimension_semantics=("arbitrary",) * n_axes, vmem_limit_bytes=VMEM_LIMIT)


def _ada_kernel(c_ref, w_ref, b_ref, o_ref):
    c = c_ref[...]
    s = c * jax.nn.sigmoid(c)
    o_ref[...] = jnp.dot(s, w_ref[...], preferred_element_type=F32,
                         precision=lax.Precision.HIGHEST) + b_ref[...]


def _ada(c_all, w_ada, b_ada):
    nb = c_all.shape[0]
    return pl.pallas_call(
        _ada_kernel,
        out_shape=jax.ShapeDtypeStruct((nb, 6 * D_MODEL), F32),
        grid=(6,),
        in_specs=[pl.BlockSpec((nb, D_MODEL), lambda j: (0, 0)),
                  pl.BlockSpec((D_MODEL, D_MODEL), lambda j: (0, j)),
                  pl.BlockSpec((1, D_MODEL), lambda j: (0, j))],
        out_specs=pl.BlockSpec((nb, D_MODEL), lambda j: (0, j)),
        compiler_params=_cparams(1),
        name="ada",
    )(c_all, w_ada, b_ada.reshape(1, 6 * D_MODEL))


def _rms(x, g):
    return x * lax.rsqrt(jnp.mean(x * x, axis=-1, keepdims=True) + RMS_EPS) * g


def _inproj_kernel(x_ref, mod_ref, n1_ref, w_ref, qa, ka, va, *rest):
    qkv_b = rest[:9]
    sga, sgb, scr = rest[9:]
    mod = mod_ref[0]
    h = _rms(x_ref[...], n1_ref[...]) * (1.0 + mod[1:2]) + mod[0:1]
    hb = h.astype(BF16)
    tm = hb.shape[0]
    scale = HEAD_DIM ** -0.5

    def proj(off, width):
        return jnp.dot(hb, w_ref[:, off:off + width], preferred_element_type=F32)

    qa[...] = (proj(0, A_Q_W) * scale).astype(BF16)
    ka[...] = proj(A_Q_W, A_KV_W).astype(BF16)
    va[...] = proj(A_Q_W + A_KV_W, A_KV_W).astype(BF16)
    off = A_Q_W + 2 * A_KV_W
    for t in range(3):
        for gi, (_, d) in enumerate(B_GROUPS):
            r = proj(off + t * B_W + gi * B_OUT_W, B_OUT_W)
            if t == 0:
                r = r * scale
            ref = qkv_b[gi * 3 + t]
            if d == 1:
                ref[...] = r.astype(BF16)
            else:
                for c in range(B_OUT_W // LANES):
                    slot = (t * 2 + gi - 1) * (B_OUT_W // LANES) + c
                    scr[slot] = r[:, c * LANES:(c + 1) * LANES]
                    for res in range(d):
                        col = res * B_OUT_W + c * LANES
                        ref[:, col:col + LANES] = scr[
                            slot, pl.ds(res, tm // d, stride=d), :].astype(BF16)
    off += 3 * B_W
    sga[...] = jax.nn.sigmoid(proj(off, D_MODEL)).astype(BF16)
    sgb[...] = jax.nn.sigmoid(proj(off + D_MODEL, D_MODEL)).astype(BF16)


def _inproj(x2, mod3, norm1, w_in_bf, seq):
    n = x2.shape[0]
    tm = TM_INPROJ
    assert seq % tm == 0 and n % tm == 0
    row = lambda i: (i, 0)
    shapes = [(n, A_Q_W, tm), (n, A_KV_W, tm), (n, A_KV_W, tm)]
    for _, d in B_GROUPS:
        shapes += [(n // d, d * B_OUT_W, tm // d)] * 3
    shapes += [(n, D_MODEL, tm)] * 2
    return pl.pallas_call(
        _inproj_kernel,
        out_shape=[jax.ShapeDtypeStruct((r, c), BF16) for r, c, _ in shapes],
        grid=(n // tm,),
        in_specs=[pl.BlockSpec((tm, D_MODEL), row),
                  pl.BlockSpec((1, 6, D_MODEL), lambda i: (i * tm // seq, 0, 0)),
                  pl.BlockSpec((1, D_MODEL), lambda i: (0, 0)),
                  pl.BlockSpec((D_MODEL, D_IN), lambda i: (0, 0))],
        out_specs=[pl.BlockSpec((b, c), row) for _, c, b in shapes],
        scratch_shapes=[pltpu.VMEM((6 * B_OUT_W // LANES, tm, LANES), F32)],
        compiler_params=_cparams(1),
        name="inproj",
    )(x2, mod3, norm1.reshape(1, D_MODEL), w_in_bf)


def _rel_bucket_np(rel):
    half = N_BUCKETS // 2
    max_exact = half // 2
    n = np.abs(rel)
    large = max_exact + (np.log(np.maximum(n, 1) / max_exact) / math.log(MAX_DISTANCE / max_exact)
                         * (half - max_exact)).astype(np.int32)
    large = np.minimum(large, half - 1)
    return ((rel > 0).astype(np.int32) * half + np.where(n < max_exact, n, large)).astype(np.int32)


def _bias_table(rel_bias_heads, n_q, n_k, key_off, band, dist_scale):
    p = n_q + n_k
    rel = np.arange(p) - (n_q - 1) - key_off
    bucket = _rel_bucket_np(rel * dist_scale)
    t = jnp.where((np.abs(rel) <= band)[None], rel_bias_heads.astype(F32)[bucket].T, NEG_INF)
    big = jnp.tile(t, (1, n_q + 1))
    tab = big[:, n_q - 1:n_q - 1 + n_q * (p - 1)].reshape(-1, n_q, p - 1)
    return tab[:, :, :n_k]


def _attn_kernel(*refs, half, nsub, n_q_heads, n_kv_heads, has_sink, want_lse, n_qblocks):
    it = iter(refs)
    q_ref = next(it)
    kp, kc, kn, vp, vc, vn = (next(it) for _ in range(6))
    bias_ref = next(it)
    sink_ref = next(it) if has_sink else None
    o_ref = next(it)
    lse_ref = next(it) if want_lse else None
    grp = n_q_heads // n_kv_heads
    hd = HEAD_DIM
    step = pl.program_id(2)

    kcat = jnp.concatenate([kp[...], kc[...], kn[...]], axis=0)
    vcat = jnp.concatenate([vp[...], vc[...], vn[...]], axis=0)
    col = lax.broadcasted_iota(I32, (1, 3 * half), 1)
    for j in range(nsub):
        qblk = step * nsub + j
        lo = jnp.where(qblk == 0, half, 0)
        hi = jnp.where(qblk == n_qblocks - 1, 2 * half, 3 * half)
        pen = jnp.where((col >= lo) & (col < hi), 0.0, NEG_INF)
        rows = slice(j * half, (j + 1) * half)
        for kv in range(n_kv_heads):
            kk = kcat[j * half:(j + 3) * half, kv * hd:(kv + 1) * hd]
            vv = vcat[j * half:(j + 3) * half, kv * hd:(kv + 1) * hd]
            heads = [kv * grp + g for g in range(grp)]
            qs = jnp.concatenate([q_ref[rows, h * hd:(h + 1) * hd] for h in heads], axis=0)
            s = lax.dot_general(qs, kk, (((1,), (1,)), ((), ())), preferred_element_type=F32)
            s = s + bias_ref[kv] + pen
            m = jnp.max(s, axis=-1, keepdims=True)
            if has_sink:
                sk = sink_ref[kv]
                m = jnp.maximum(m, sk)
            p = jnp.exp(s - m)
            den = jnp.sum(p, axis=-1, keepdims=True)
            if has_sink:
                den = den + jnp.exp(sk - m)
            o = jnp.dot(p.astype(BF16), vv, preferred_element_type=F32) / den
            if want_lse:
                lse = m + jnp.log(den)
            for g, h in enumerate(heads):
                o_ref[rows, h * hd:(h + 1) * hd] = o[g * half:(g + 1) * half].astype(o_ref.dtype)
                if want_lse:
                    lse_ref[rows, h * hd:(h + 1) * hd] = jnp.broadcast_to(
                        lse[g * half:(g + 1) * half], (half, hd))


def _band_attention(q, k, v, bias, sink, *, nbatch, seq, dil, half, n_q_heads, n_kv_heads,
                    q_col, kv_col, want_lse):
    n = nbatch * seq
    sub_len = seq // dil
    qb = ATTN_QB
    assert sub_len % qb == 0 and qb % half == 0
    nsub = qb // half
    nq = sub_len // qb
    qw = n_q_heads * HEAD_DIM
    kw = n_kv_heads * HEAD_DIM
    total_half_blocks = n // dil // half

    def cur(col):
        return lambda b, r, i: (b * nq + i, col(r))

    def prev(col):
        return lambda b, r, i: (jnp.maximum((b * nq + i) * nsub - 1, 0), col(r))

    def nxt(col):
        return lambda b, r, i: (jnp.minimum((b * nq + i + 1) * nsub, total_half_blocks - 1), col(r))

    in_specs = [pl.BlockSpec((qb, qw), cur(q_col))]
    for _ in range(2):
        in_specs += [pl.BlockSpec((half, kw), prev(kv_col)),
                     pl.BlockSpec((qb, kw), cur(kv_col)),
                     pl.BlockSpec((half, kw), nxt(kv_col))]
    in_specs.append(pl.BlockSpec(bias.shape, lambda b, r, i: (0, 0, 0)))
    args = [q, k, k, k, v, v, v, bias]
    if sink is not None:
        in_specs.append(pl.BlockSpec(sink.shape, lambda b, r, i: (0, 0, 0)))
        args.append(sink)
    out_col = lambda r: r
    out_shape = [jax.ShapeDtypeStruct((n // dil, dil * qw), BF16)]
    out_specs = [pl.BlockSpec((qb, qw), cur(out_col))]
    if want_lse:
        out_shape.append(jax.ShapeDtypeStruct((n // dil, dil * qw), F32))
        out_specs.append(pl.BlockSpec((qb, qw), cur(out_col)))
    return pl.pallas_call(
        functools.partial(_attn_kernel, half=half, nsub=nsub, n_q_heads=n_q_heads,
                          n_kv_heads=n_kv_heads, has_sink=sink is not None, want_lse=want_lse,
                          n_qblocks=sub_len // half),
        out_shape=out_shape,
        grid=(nbatch, dil, nq),
        in_specs=in_specs,
        out_specs=out_specs,
        compiler_params=_cparams(3),
        name=f"band_attn_d{dil}",
    )(*args)


def _pack_rows(y):
    lo = pltpu.bitcast(y[:, :HALF_D].astype(BF16).astype(F32), U32)
    hi = pltpu.bitcast(y[:, HALF_D:].astype(BF16).astype(F32), U32)
    return (hi & jnp.uint32(0xFFFF0000)) | (lo >> 16)


def _unpack_rows(p):
    lo = pltpu.bitcast(p << 16, F32)
    hi = pltpu.bitcast(p & jnp.uint32(0xFFFF0000), F32)
    return lo, hi


def _route(sel, scores):
    t = sel.shape[-1]
    per = N_EXPERTS // N_EXPERT_GROUPS
    shape3 = (N_EXPERT_GROUPS, per, t)
    sel3 = sel.reshape(shape3)
    sc3 = scores.reshape(shape3)
    iota_g = lax.broadcasted_iota(I32, shape3, 0)
    iota_m = lax.broadcasted_iota(I32, shape3, 1)
    iota_e = iota_g * per + iota_m
    m1 = jnp.max(sel3, axis=1, keepdims=True)
    i1 = jnp.min(jnp.where(sel3 == m1, iota_m, per), axis=1, keepdims=True)
    m2 = jnp.max(jnp.where(iota_m == i1, REMOVED, sel3), axis=1, keepdims=True)
    gscore = m1 + m2
    iota_g1 = lax.broadcasted_iota(I32, gscore.shape, 0)
    gmask = jnp.zeros(gscore.shape, jnp.bool_)
    for _ in range(TOPK_GROUPS):
        mx = jnp.max(gscore, axis=0, keepdims=True)
        ix = jnp.min(jnp.where(gscore == mx, iota_g1, N_EXPERT_GROUPS), axis=0, keepdims=True)
        hit = iota_g1 == ix
        gmask = gmask | hit
        gscore = jnp.where(hit, REMOVED, gscore)
    cur = jnp.where(gmask, sel3, NEG_INF)
    sum_all = lambda a: jnp.sum(jnp.sum(a, axis=1, keepdims=True), axis=0, keepdims=True)
    idxs, wts, hits = [], [], []
    for _ in range(TOP_K):
        mx = jnp.max(jnp.max(cur, axis=1, keepdims=True), axis=0, keepdims=True)
        cand = jnp.where(cur == mx, iota_e, N_EXPERTS)
        ix = jnp.min(jnp.min(cand, axis=1, keepdims=True), axis=0, keepdims=True)
        hit = iota_e == ix
        wts.append(sum_all(jnp.where(hit, sc3, 0.0)).reshape(1, t))
        cur = jnp.where(hit, REMOVED, cur)
        idxs.append(ix.reshape(1, t))
        hits.append(hit)
    wsum = wts[0]
    for w in wts[1:]:
        wsum = wsum + w
    wts = [w / wsum * ROUTED_SCALE for w in wts]
    onehot = hits[0].astype(F32)
    for hit in hits[1:]:
        onehot = onehot + hit.astype(F32)
    tri = (lax.broadcasted_iota(I32, (t, t), 0) <= lax.broadcasted_iota(I32, (t, t), 1))
    cum = jnp.dot(onehot.reshape(N_EXPERTS, t).astype(BF16), tri.astype(F32).astype(BF16),
                  preferred_element_type=F32)
    cum3 = cum.reshape(shape3) - 1.0
    ranks = [sum_all(jnp.where(hit, cum3, 0.0)).reshape(1, t).astype(I32) for hit in hits]
    counts = cum[:, t - 1:t].astype(I32)
    return idxs, wts, ranks, counts


def _merge_kernel(x_ref, ya_ref, o1, o2, o3, l1, l2, l3, sga_ref, sgb_ref, mod_ref, n2_ref,
                  wpa, wpb, wo, wrt, rbias, wsg, wsu, wsd,
                  base_ref, hp_ref, ridx_ref, rw_ref, rank_ref, cnt_ref, scr):
    mod = mod_ref[0]
    tm = x_ref.shape[0]

    def token_major(ref, slot, d):
        if d == 1:
            return ref[...].astype(F32)
        nc = B_OUT_W // LANES
        for c in range(nc):
            for res in range(d):
                col = res * B_OUT_W + c * LANES
                scr[slot * nc + c, pl.ds(res, tm // d, stride=d), :] = ref[
                    :, col:col + LANES].astype(F32)
        return jnp.concatenate([scr[slot * nc + c] for c in range(nc)], axis=1)

    dils = [d for _, d in B_GROUPS]
    os_ = [token_major(r, i, d) for i, (r, d) in enumerate(zip((o1, o2, o3), dils))]
    ls = [token_major(r, 3 + i, d) for i, (r, d) in enumerate(zip((l1, l2, l3), dils))]
    mx = jnp.maximum(jnp.maximum(ls[0], ls[1]), ls[2])
    es = [jnp.exp(l - mx) for l in ls]
    den = es[0] + es[1] + es[2]
    ob = (es[0] / den) * os_[0] + (es[1] / den) * os_[1] + (es[2] / den) * os_[2]
    pa = jnp.dot(ya_ref[...], wpa[...], preferred_element_type=F32)
    pb = jnp.dot(ob.astype(BF16), wpb[...], preferred_element_type=F32)
    merged = sga_ref[...].astype(F32) * pa + sgb_ref[...].astype(F32) * pb
    mix = jnp.dot(merged.astype(BF16), wo[...], preferred_element_type=F32)
    x1 = x_ref[...] + mod[2:3] * mix
    h2 = _rms(x1, n2_ref[...]) * (1.0 + mod[4:5]) + mod[3:4]
    logits = lax.dot_general(wrt[...], h2, (((1,), (1,)), ((), ())), preferred_element_type=F32,
                             precision=lax.Precision.HIGHEST)
    scores = jax.nn.sigmoid(logits)
    idxs, wts, ranks, counts = _route(scores + rbias[...], scores)
    for k in range(TOP_K):
        ridx_ref[k:k + 1, :] = idxs[k]
        rw_ref[k:k + 1, :] = wts[k]
        rank_ref[k:k + 1, :] = ranks[k]
    ridx_ref[TOP_K:, :] = jnp.zeros((8 - TOP_K, tm), I32)
    rw_ref[TOP_K:, :] = jnp.zeros((8 - TOP_K, tm), F32)
    rank_ref[TOP_K:, :] = jnp.zeros((8 - TOP_K, tm), I32)
    cnt_ref[0] = counts
    hb = h2.astype(BF16)
    g = jnp.dot(hb, wsg[...], preferred_element_type=F32)
    u = jnp.dot(hb, wsu[...], preferred_element_type=F32)
    act = (g * jax.nn.sigmoid(g) * u).astype(BF16)
    shared = jnp.dot(act, wsd[...], preferred_element_type=F32)
    base_ref[...] = x1 + mod[5:6] * shared
    hp_ref[...] = _pack_rows(h2)


def _merge(x2, ya, outs, lses, sga, sgb, mod3, norm2, wpa, wpb, wo, wrt, rbias, wsg, wsu, wsd, seq):
    n = x2.shape[0]
    tm = TM_MERGE
    assert seq % tm == 0
    row = lambda i: (i, 0)
    full = lambda a: pl.BlockSpec(a.shape, lambda i: (0,) * a.ndim)
    weights = [wpa, wpb, wo, wrt, rbias, wsg, wsu, wsd]
    group_specs = [pl.BlockSpec((tm // d, d * B_OUT_W), row) for _, d in B_GROUPS]
    lanes = lambda i: (0, i)
    return pl.pallas_call(
        _merge_kernel,
        out_shape=[jax.ShapeDtypeStruct((n, D_MODEL), F32),
                   jax.ShapeDtypeStruct((n, HALF_D), U32),
                   jax.ShapeDtypeStruct((8, n), I32),
                   jax.ShapeDtypeStruct((8, n), F32),
                   jax.ShapeDtypeStruct((8, n), I32),
                   jax.ShapeDtypeStruct((n // tm, N_EXPERTS, 1), I32)],
        grid=(n // tm,),
        in_specs=[pl.BlockSpec((tm, D_MODEL), row), pl.BlockSpec((tm, A_Q_W), row)]
                 + group_specs * 2
                 + [pl.BlockSpec((tm, D_MODEL), row)] * 2
                 + [pl.BlockSpec((1, 6, D_MODEL), lambda i: (i * tm // seq, 0, 0)),
                    pl.BlockSpec((1, D_MODEL), lambda i: (0, 0))]
                 + [full(w) for w in weights],
        out_specs=[pl.BlockSpec((tm, D_MODEL), row), pl.BlockSpec((tm, HALF_D), row),
                   pl.BlockSpec((8, tm), lanes), pl.BlockSpec((8, tm), lanes),
                   pl.BlockSpec((8, tm), lanes),
                   pl.BlockSpec((1, N_EXPERTS, 1), lambda i: (i, 0, 0))],
        scratch_shapes=[pltpu.VMEM((6 * B_OUT_W // LANES, tm, LANES), F32)],
        compiler_params=_cparams(1),
        name="merge_route",
    )(x2, ya, *outs, *lses, sga, sgb, mod3, norm2.reshape(1, D_MODEL), *weights)


def _plan_kernel(ridx_ref, rank_ref, tb_ref, dest_ref):
    tb = tb_ref[0]
    t = ridx_ref.shape[1]
    iota_e = lax.broadcasted_iota(I32, (N_EXPERTS, t), 0)
    for k in range(TOP_K):
        base = jnp.sum(jnp.where(iota_e == ridx_ref[k:k + 1, :], tb, 0), axis=0, keepdims=True)
        dest_ref[0, :, k * t:(k + 1) * t] = base + rank_ref[k:k + 1, :]


def _plan(ridx, rank, tile_base, tm):
    n = ridx.shape[1]
    lanes = lambda i: (0, i)
    return pl.pallas_call(
        _plan_kernel,
        out_shape=jax.ShapeDtypeStruct((n // tm, 1, TOP_K * tm), I32),
        grid=(n // tm,),
        in_specs=[pl.BlockSpec((8, tm), lanes), pl.BlockSpec((8, tm), lanes),
                  pl.BlockSpec((1, N_EXPERTS, 1), lambda i: (i, 0, 0))],
        out_specs=pl.BlockSpec((1, 1, TOP_K * tm), lambda i: (i, 0, 0)),
        compiler_params=_cparams(1),
        name="moe_plan",
    )(ridx, rank, tile_base)


def _dispatch_kernel(dest_ref, h_ref, xs_out, sem, *, tm):
    def issue(i, carry):
        for k in range(TOP_K):
            d = dest_ref[0, 0, k * tm + i]
            pltpu.make_async_copy(h_ref.at[pl.ds(i, 1)], xs_out.at[pl.ds(d, 1)], sem).start()
        return carry

    lax.fori_loop(0, tm, issue, 0)

    def drain(i, carry):
        for k in range(TOP_K):
            pltpu.make_async_copy(h_ref.at[pl.ds(0, 1)], xs_out.at[pl.ds(0, 1)], sem).wait()
        return carry

    lax.fori_loop(0, tm, drain, 0)


def _dispatch(hp, dest_tiles, n_rows):
    n = hp.shape[0]
    tm = TM_DISPATCH
    return pl.pallas_call(
        functools.partial(_dispatch_kernel, tm=tm),
        out_shape=jax.ShapeDtypeStruct((n_rows, HALF_D), U32),
        grid=(n // tm,),
        in_specs=[pl.BlockSpec((1, 1, TOP_K * tm), lambda i: (i, 0, 0), memory_space=pltpu.SMEM),
                  pl.BlockSpec((tm, HALF_D), lambda i: (i, 0))],
        out_specs=pl.BlockSpec(memory_space=pl.ANY),
        scratch_shapes=[pltpu.SemaphoreType.DMA],
        compiler_params=_cparams(1),
        name="moe_dispatch",
    )(dest_tiles, hp)


def _expert_kernel(blk_e_ref, valid_ref, xs_ref, wgu_ref, wd_ref, ys_ref):
    del blk_e_ref
    valid = valid_ref[pl.program_id(0)]

    @pl.when(valid > 0)
    def _():
        rows = lax.broadcasted_iota(I32, (xs_ref.shape[0], 1), 0)
        lo, hi = _unpack_rows(jnp.where(rows < valid, xs_ref[...], jnp.uint32(0)))
        wgu = wgu_ref[0]
        gu = jnp.dot(lo.astype(BF16), wgu[:HALF_D], preferred_element_type=F32) \
            + jnp.dot(hi.astype(BF16), wgu[HALF_D:], preferred_element_type=F32)
        g = gu[:, :D_EXPERT]
        u = gu[:, D_EXPERT:]
        act = (g * jax.nn.sigmoid(g) * u).astype(BF16)
        ys_ref[...] = _pack_rows(jnp.dot(act, wd_ref[0], preferred_element_type=F32))

    @pl.when(valid <= 0)
    def _():
        ys_ref[...] = jnp.zeros(ys_ref.shape, U32)


def _experts(xs, blk_e, blk_valid, wgu, wd):
    n_rows = xs.shape[0]
    bm = MOE_BM
    nblk = n_rows // bm
    return pl.pallas_call(
        _expert_kernel,
        out_shape=jax.ShapeDtypeStruct((n_rows, HALF_D), U32),
        grid_spec=pltpu.PrefetchScalarGridSpec(
            num_scalar_prefetch=2,
            grid=(nblk,),
            in_specs=[pl.BlockSpec((bm, HALF_D), lambda i, be, nu: (i, 0)),
                      pl.BlockSpec((1, D_MODEL, 2 * D_EXPERT), lambda i, be, nu: (be[i], 0, 0)),
                      pl.BlockSpec((1, D_EXPERT, D_MODEL), lambda i, be, nu: (be[i], 0, 0))],
            out_specs=pl.BlockSpec((bm, HALF_D), lambda i, be, nu: (i, 0))),
        compiler_params=_cparams(1),
        name="moe_experts",
    )(blk_e, blk_valid, xs, wgu, wd)


def _combine_kernel(dest_ref, ys_ref, base_ref, rw_ref, mod_ref, fn_ref, y_ref, buf, sem, *, tm):
    def issue(i, carry):
        for k in range(TOP_K):
            d = dest_ref[0, 0, k * tm + i]
            pltpu.make_async_copy(ys_ref.at[pl.ds(d, 1)], buf.at[k, pl.ds(i, 1)], sem).start()
        return carry

    lax.fori_loop(0, tm, issue, 0)

    def drain(i, carry):
        for k in range(TOP_K):
            pltpu.make_async_copy(ys_ref.at[pl.ds(0, 1)], buf.at[k, pl.ds(0, 1)], sem).wait()
        return carry

    lax.fori_loop(0, tm, drain, 0)

    rw = rw_ref[...]
    acc_lo = jnp.zeros((tm, HALF_D), F32)
    acc_hi = jnp.zeros((tm, HALF_D), F32)
    for k in range(TOP_K):
        lo, hi = _unpack_rows(buf[k])
        w = rw[:, k:k + 1]
        acc_lo = acc_lo + w * lo
        acc_hi = acc_hi + w * hi
    routed = jnp.concatenate([acc_lo, acc_hi], axis=1)
    x2 = base_ref[...] + mod_ref[0][5:6] * routed
    y_ref[...] = _rms(x2, fn_ref[...])


def _combine(ys, dest_tiles, base, rw_t, mod3, final_norm, seq):
    n = base.shape[0]
    tm = TM_COMBINE
    assert seq % tm == 0
    return pl.pallas_call(
        functools.partial(_combine_kernel, tm=tm),
        out_shape=jax.ShapeDtypeStruct((n, D_MODEL), F32),
        grid=(n // tm,),
        in_specs=[pl.BlockSpec((1, 1, TOP_K * tm), lambda i: (i, 0, 0), memory_space=pltpu.SMEM),
                  pl.BlockSpec(memory_space=pl.ANY),
                  pl.BlockSpec((tm, D_MODEL), lambda i: (i, 0)),
                  pl.BlockSpec((tm, 8), lambda i: (i, 0)),
                  pl.BlockSpec((1, 6, D_MODEL), lambda i: (i * tm // seq, 0, 0)),
                  pl.BlockSpec((1, D_MODEL), lambda i: (0, 0))],
        out_specs=pl.BlockSpec((tm, D_MODEL), lambda i: (i, 0)),
        scratch_shapes=[pltpu.VMEM((TOP_K, tm, HALF_D), U32), pltpu.SemaphoreType.DMA],
        compiler_params=_cparams(1),
        name="moe_combine",
    )(dest_tiles, ys, base, rw_t, mod3, final_norm.reshape(1, D_MODEL))


def _block_layout(tile_counts, n):
    bm = MOE_BM
    c = tile_counts[:, :, 0]
    counts = jnp.sum(c, axis=0)
    padded = (counts + bm - 1) // bm * bm
    pend = jnp.cumsum(padded)
    pstart = pend - padded
    tile_base = pstart[None, :] + jnp.cumsum(c, axis=0) - c
    nblk = -(-n * TOP_K // bm) + N_EXPERTS
    blk = jnp.arange(nblk, dtype=I32)
    blk_e = jnp.minimum(jnp.searchsorted(pend // bm, blk, side="right"), N_EXPERTS - 1).astype(I32)
    blk_valid = jnp.clip(pstart[blk_e] + counts[blk_e] - blk * bm, 0, bm).astype(I32)
    return tile_base.astype(I32)[:, :, None], blk_e, blk_valid, nblk * bm


def _trunk(x, mod3, p):
    nbatch, seq, _ = x.shape
    n = nbatch * seq
    x2 = x.reshape(n, D_MODEL)
    proj = _inproj(x2, mod3, p["norm1"], p["w_in"], seq)
    qa, ka, va = proj[:3]
    sga, sgb = proj[12:]
    ident = lambda r: r
    (ya,) = _band_attention(qa, ka, va, p["bias_a"], p["sink_a"], nbatch=nbatch, seq=seq, dil=1,
                            half=A_HALF_WINDOW, n_q_heads=A_Q_HEADS, n_kv_heads=A_KV_HEADS,
                            q_col=ident, kv_col=ident, want_lse=False)
    outs, lses = [], []
    for gi, (w, d) in enumerate(B_GROUPS):
        q, k, v = proj[3 + 3 * gi:6 + 3 * gi]
        o, lse = _band_attention(q, k, v, p["bias_b"][gi], None, nbatch=nbatch, seq=seq, dil=d,
                                 half=w // (2 * d), n_q_heads=B_HEADS_PER_GROUP,
                                 n_kv_heads=B_HEADS_PER_GROUP, q_col=ident, kv_col=ident,
                                 want_lse=True)
        outs.append(o)
        lses.append(lse)
    base, hp, ridx, rw, rank, tile_counts = _merge(
        x2, ya, outs, lses, sga, sgb, mod3, p["norm2"], p["w_pa"], p["w_pb"], p["w_o"], p["w_rt"],
        p["rbias"], p["ws_gate"], p["ws_up"], p["ws_down"], seq)
    assert TM_MERGE == TM_DISPATCH == TM_COMBINE
    tile_base, blk_e, blk_valid, n_rows = _block_layout(tile_counts, n)
    dest_tiles = _plan(ridx, rank, tile_base, TM_MERGE)
    xs = _dispatch(hp, dest_tiles, n_rows)
    ys = _experts(xs, blk_e, blk_valid, p["w_gu"], p["w_down"])
    y = _combine(ys, dest_tiles, base, rw.T, mod3, p["final_norm"], seq)
    return y.reshape(nbatch, seq, D_MODEL)


def kernel(x_prompt, x_sample, c_prompt, c_sample, rel_bias, w_ada, b_ada, norm1, w_in, sink, w_pa, w_pb, w_o, norm2, w_router, router_bias, w_gate, w_up, w_down, ws_gate, ws_up, ws_down, final_norm):
    assert w_ada.shape[0] == 1
    nbp = x_prompt.shape[0]
    mod = _ada(jnp.concatenate([c_prompt, c_sample], axis=0), w_ada[0], b_ada[0])
    mod3 = mod.reshape(-1, 6, D_MODEL)
    grp = A_Q_HEADS // A_KV_HEADS
    ha = A_HALF_WINDOW
    bias_a = _bias_table(rel_bias[:, :A_Q_HEADS], ha, 3 * ha, ha, ha, 1)
    bias_b = []
    for gi, (w, d) in enumerate(B_GROUPS):
        h0 = A_Q_HEADS + gi * B_HEADS_PER_GROUP
        hb = w // (2 * d)
        bias_b.append(_bias_table(rel_bias[:, h0:h0 + B_HEADS_PER_GROUP], hb, 3 * hb, hb, hb, d))
    p = {
        "norm1": norm1[0], "norm2": norm2[0], "final_norm": final_norm,
        "w_in": w_in[0].astype(BF16),
        "bias_a": bias_a.reshape(A_KV_HEADS, grp * A_HALF_WINDOW, 3 * A_HALF_WINDOW),
        "sink_a": jnp.repeat(sink[0].astype(F32), A_HALF_WINDOW).reshape(
            A_KV_HEADS, grp * A_HALF_WINDOW, 1),
        "bias_b": bias_b,
        "w_pa": w_pa[0].astype(BF16), "w_pb": w_pb[0].astype(BF16), "w_o": w_o[0].astype(BF16),
        "w_rt": w_router[0].T, "rbias": router_bias[0].reshape(N_EXPERTS, 1),
        "ws_gate": ws_gate[0].astype(BF16), "ws_up": ws_up[0].astype(BF16),
        "ws_down": ws_down[0].astype(BF16),
        "w_gu": jnp.concatenate([w_gate[0], w_up[0]], axis=-1).astype(BF16),
        "w_down": w_down[0].astype(BF16),
    }
    y_prompt = _trunk(x_prompt, mod3[:nbp], p)
    y_sample = _trunk(x_sample, mod3[nbp:], p)
    return (y_prompt, y_sample)
```

```python
import functools
import math

import jax
import jax.numpy as jnp
import numpy as np
from jax import lax
from jax.experimental import pallas as pl
from jax.experimental.pallas import tpu as pltpu

F32 = jnp.float32
BF16 = jnp.bfloat16
U32 = jnp.uint32
I32 = jnp.int32

D_MODEL = 1024
HEAD_DIM = 64
A_Q_HEADS = 8
A_KV_HEADS = 2
A_HALF_WINDOW = 128
B_GROUPS = ((128, 1), (512, 4), (2048, 16))
B_HEADS_PER_GROUP = 4
N_BUCKETS = 32
MAX_DISTANCE = 1024
N_EXPERTS = 64
TOP_K = 6
N_EXPERT_GROUPS = 8
TOPK_GROUPS = 4
D_EXPERT = 256
D_SHARED = 256
ROUTED_SCALE = 2.5
RMS_EPS = 1e-6
NEG_INF = -1e30
REMOVED = -3e38

A_Q_W = A_Q_HEADS * HEAD_DIM
A_KV_W = A_KV_HEADS * HEAD_DIM
B_W = len(B_GROUPS) * B_HEADS_PER_GROUP * HEAD_DIM
B_OUT_W = B_HEADS_PER_GROUP * HEAD_DIM
IN_WIDTHS = (A_Q_W, A_KV_W, A_KV_W, B_W, B_W, B_W, D_MODEL, D_MODEL)
D_IN = sum(IN_WIDTHS)
HALF_D = D_MODEL // 2
LANES = 128

TM_INPROJ = 512
TM_MERGE = 256
ATTN_BLOCK = 128
ATTN_ROWS = 512
MOE_BM = 256
TM_DISPATCH = 256
TM_COMBINE = 256
VMEM_LIMIT = 48 * 1024 * 1024


def _cparams(n_axes):
    return pltpu.CompilerParams(
        dimension_semantics=("arbitrary",) * n_axes, vmem_limit_bytes=VMEM_LIMIT)


def _ada_kernel(c_ref, w_ref, b_ref, o_ref):
    c = c_ref[...]
    s = c * jax.nn.sigmoid(c)
    o_ref[...] = jnp.dot(s, w_ref[...], preferred_element_type=F32,
                         precision=lax.Precision.HIGHEST) + b_ref[...]


def _ada(c_all, w_ada, b_ada):
    nb = c_all.shape[0]
    return pl.pallas_call(
        _ada_kernel,
        out_shape=jax.ShapeDtypeStruct((nb, 6 * D_MODEL), F32),
        grid=(6,),
        in_specs=[pl.BlockSpec((nb, D_MODEL), lambda j: (0, 0)),
                  pl.BlockSpec((D_MODEL, D_MODEL), lambda j: (0, j)),
                  pl.BlockSpec((1, D_MODEL), lambda j: (0, j))],
        out_specs=pl.BlockSpec((nb, D_MODEL), lambda j: (0, j)),
        compiler_params=_cparams(1),
        name="ada",
    )(c_all, w_ada, b_ada.reshape(1, 6 * D_MODEL))


def _rms(x, g):
    return x * lax.rsqrt(jnp.mean(x * x, axis=-1, keepdims=True) + RMS_EPS) * g


def _inproj_kernel(x_ref, mod_ref, n1_ref, w_ref, qa, ka, ka_sw, va, va_sw, *rest):
    qkv_b = rest[:9]
    sga, sgb, scr = rest[9:]
    mod = mod_ref[0]
    h = _rms(x_ref[...], n1_ref[...]) * (1.0 + mod[1:2]) + mod[0:1]
    hb = h.astype(BF16)
    tm = hb.shape[0]
    scale = HEAD_DIM ** -0.5

    def proj(off, width):
        return jnp.dot(hb, w_ref[:, off:off + width], preferred_element_type=F32)

    qa[...] = (proj(0, A_Q_W) * scale).astype(BF16)
    for ref, ref_sw, off in ((ka, ka_sw, A_Q_W), (va, va_sw, A_Q_W + A_KV_W)):
        r = proj(off, A_KV_W)
        ref[...] = r.astype(BF16)
        ref_sw[...] = pltpu.roll(r, HEAD_DIM, 1).astype(BF16)
    off = A_Q_W + 2 * A_KV_W
    for t in range(3):
        for gi, (_, d) in enumerate(B_GROUPS):
            r = proj(off + t * B_W + gi * B_OUT_W, B_OUT_W)
            if t == 0:
                r = r * scale
            ref = qkv_b[gi * 3 + t]
            if d == 1:
                ref[...] = r.astype(BF16)
            else:
                for c in range(B_OUT_W // LANES):
                    slot = (t * 2 + gi - 1) * (B_OUT_W // LANES) + c
                    scr[slot] = r[:, c * LANES:(c + 1) * LANES]
                    for res in range(d):
                        col = res * B_OUT_W + c * LANES
                        ref[:, col:col + LANES] = scr[
                            slot, pl.ds(res, tm // d, stride=d), :].astype(BF16)
    off += 3 * B_W
    sga[...] = jax.nn.sigmoid(proj(off, D_MODEL)).astype(BF16)
    sgb[...] = jax.nn.sigmoid(proj(off + D_MODEL, D_MODEL)).astype(BF16)


def _inproj(x2, mod3, norm1, w_in_bf, seq):
    n = x2.shape[0]
    tm = TM_INPROJ
    assert seq % tm == 0 and n % tm == 0
    row = lambda i: (i, 0)
    shapes = [(n, A_Q_W, tm)] + [(n, A_KV_W, tm)] * 4
    for _, d in B_GROUPS:
        shapes += [(n // d, d * B_OUT_W, tm // d)] * 3
    shapes += [(n, D_MODEL, tm)] * 2
    return pl.pallas_call(
        _inproj_kernel,
        out_shape=[jax.ShapeDtypeStruct((r, c), BF16) for r, c, _ in shapes],
        grid=(n // tm,),
        in_specs=[pl.BlockSpec((tm, D_MODEL), row),
                  pl.BlockSpec((1, 6, D_MODEL), lambda i: (i * tm // seq, 0, 0)),
                  pl.BlockSpec((1, D_MODEL), lambda i: (0, 0)),
                  pl.BlockSpec((D_MODEL, D_IN), lambda i: (0, 0))],
        out_specs=[pl.BlockSpec((b, c), row) for _, c, b in shapes],
        scratch_shapes=[pltpu.VMEM((6 * B_OUT_W // LANES, tm, LANES), F32)],
        compiler_params=_cparams(1),
        name="inproj",
    )(x2, mod3, norm1.reshape(1, D_MODEL), w_in_bf)


def _rel_bucket_np(rel):
    half = N_BUCKETS // 2
    max_exact = half // 2
    n = np.abs(rel)
    large = max_exact + (np.log(np.maximum(n, 1) / max_exact) / math.log(MAX_DISTANCE / max_exact)
                         * (half - max_exact)).astype(np.int32)
    large = np.minimum(large, half - 1)
    return ((rel > 0).astype(np.int32) * half + np.where(n < max_exact, n, large)).astype(np.int32)


def _bias_table(rel_bias_heads, n_q, n_k, key_off, band, dist_scale):
    p = n_q + n_k
    rel = np.arange(p) - (n_q - 1) - key_off
    bucket = _rel_bucket_np(rel * dist_scale)
    t = jnp.where((np.abs(rel) <= band)[None], rel_bias_heads.astype(F32)[bucket].T, NEG_INF)
    big = jnp.tile(t, (1, n_q + 1))
    tab = big[:, n_q - 1:n_q - 1 + n_q * (p - 1)].reshape(-1, n_q, p - 1)
    return tab[:, :, :n_k]


def _pair_rhs(k_top, k_bot, v_top, v_bot):
    kb = k_top.shape[0]
    low = jnp.where(lax.broadcasted_iota(I32, (kb, LANES), 1) < HEAD_DIM, 1.0, 0.0).astype(BF16)
    high = jnp.where(lax.broadcasted_iota(I32, (kb, LANES), 1) < HEAD_DIM, 0.0, 1.0).astype(BF16)
    rhs_k = jnp.concatenate([k_top * low, k_bot * high], axis=0)
    rhs_v = jnp.concatenate([jnp.concatenate([v_top * low, low], axis=1),
                             jnp.concatenate([v_bot * high, high], axis=1)], axis=0)
    return rhs_k, rhs_v


def _pair_attention(q_pair, rhs_k, rhs_v, bias_pair, pen, sink_pair, want_lse):
    kb = rhs_k.shape[0] // 2
    s = lax.dot_general(q_pair, rhs_k, (((1,), (1,)), ((), ())), preferred_element_type=F32)
    s = s + bias_pair
    if pen is not None:
        s = s + pen
    s0, s1 = s[:, :kb], s[:, kb:]
    m0 = jnp.max(s0, axis=-1, keepdims=True)
    m1 = jnp.max(s1, axis=-1, keepdims=True)
    if sink_pair is not None:
        m0 = jnp.maximum(m0, sink_pair[:, 0:1])
        m1 = jnp.maximum(m1, sink_pair[:, HEAD_DIM:HEAD_DIM + 1])
    p = jnp.concatenate([jnp.exp(s0 - m0), jnp.exp(s1 - m1)], axis=1).astype(BF16)
    od = jnp.dot(p, rhs_v, preferred_element_type=F32)
    o, den = od[:, :LANES], od[:, LANES:]
    low = lax.broadcasted_iota(I32, o.shape, 1) < HEAD_DIM
    m_full = jnp.where(low, m0, m1)
    if sink_pair is not None:
        den = den + jnp.exp(sink_pair - m_full)
    return o / den, (m_full + jnp.log(den)) if want_lse else None


def _attn_kernel(*refs, halo, nblk, n_pairs, shared_kv, has_sink, want_lse, n_steps):
    it = iter(refs)
    q_ref = next(it)
    n_kv = 4 if shared_kv else 2
    kv_refs = [[next(it) for _ in range(3)] for _ in range(n_kv)]
    bias_ref = next(it)
    sink_ref = next(it) if has_sink else None
    o_ref = next(it)
    lse_ref = next(it) if want_lse else None
    step = pl.program_id(2)
    blk = ATTN_BLOCK
    kb = blk + 2 * halo
    cats = [jnp.concatenate([r[...] for r in trio], axis=0) for trio in kv_refs]
    col = lax.broadcasted_iota(I32, (1, kb), 1)
    for b in range(nblk):
        lo = jnp.where(step == 0, halo, 0) if b == 0 else 0
        hi = jnp.where(step == n_steps - 1, kb - halo, kb) if b == nblk - 1 else kb
        pen = None
        if b == 0 or b == nblk - 1:
            pen1 = jnp.where((col >= lo) & (col < hi), 0.0, NEG_INF)
            pen = jnp.concatenate([pen1, pen1], axis=1)
        rows = slice(b * blk, (b + 1) * blk)
        krows = slice(b * blk, b * blk + kb)
        if shared_kv:
            k, k_sw, v, v_sw = (c[krows] for c in cats)
            rhs = [_pair_rhs(k, k_sw, v, v_sw), _pair_rhs(k_sw, k, v_sw, v)]
        for c in range(n_pairs):
            lanes = slice(c * LANES, (c + 1) * LANES)
            if shared_kv:
                rhs_k, rhs_v = rhs[c // (n_pairs // 2)]
            else:
                k, v = cats[0][krows, lanes], cats[1][krows, lanes]
                rhs_k, rhs_v = _pair_rhs(k, k, v, v)
            o, lse = _pair_attention(q_ref[rows, lanes], rhs_k, rhs_v, bias_ref[c], pen,
                                     sink_ref[c] if has_sink else None, want_lse)
            o_ref[rows, lanes] = o.astype(o_ref.dtype)
            if want_lse:
                lse_ref[rows, lanes] = lse


def _band_attention(q, kvs, bias, sink, *, nbatch, seq, dil, halo, kv_width, want_lse):
    n = nbatch * seq
    sub_len = seq // dil
    qw = q.shape[1] // dil
    rows = min(ATTN_ROWS, sub_len)
    assert sub_len % rows == 0 and rows % ATTN_BLOCK == 0 and rows % halo == 0
    nq = sub_len // rows
    per = rows // halo
    total_halos = n // dil // halo
    cur = lambda b, r, i: (b * nq + i, r)
    prev = lambda b, r, i: (jnp.maximum((b * nq + i) * per - 1, 0), r)
    nxt = lambda b, r, i: (jnp.minimum((b * nq + i + 1) * per, total_halos - 1), r)
    const = lambda a: pl.BlockSpec(a.shape, lambda b, r, i: (0,) * a.ndim)

    in_specs = [pl.BlockSpec((rows, qw), cur)]
    args = [q]
    for a in kvs:
        in_specs += [pl.BlockSpec((halo, kv_width), prev), pl.BlockSpec((rows, kv_width), cur),
                     pl.BlockSpec((halo, kv_width), nxt)]
        args += [a, a, a]
    in_specs.append(const(bias))
    args.append(bias)
    if sink is not None:
        in_specs.append(const(sink))
        args.append(sink)
    out_shape = [jax.ShapeDtypeStruct(q.shape, BF16)]
    out_specs = [pl.BlockSpec((rows, qw), cur)]
    if want_lse:
        out_shape.append(jax.ShapeDtypeStruct(q.shape, F32))
        out_specs.append(pl.BlockSpec((rows, qw), cur))
    return pl.pallas_call(
        functools.partial(_attn_kernel, halo=halo, nblk=rows // ATTN_BLOCK, n_pairs=qw // LANES,
                          shared_kv=len(kvs) == 4, has_sink=sink is not None, want_lse=want_lse,
                          n_steps=nq),
        out_shape=out_shape,
        grid=(nbatch, dil, nq),
        in_specs=in_specs,
        out_specs=out_specs,
        compiler_params=_cparams(3),
        name=f"band_attn_d{dil}",
    )(*args)


def _pack_rows(y):
    lo = pltpu.bitcast(y[:, :HALF_D].astype(BF16).astype(F32), U32)
    hi = pltpu.bitcast(y[:, HALF_D:].astype(BF16).astype(F32), U32)
    return (hi & jnp.uint32(0xFFFF0000)) | (lo >> 16)


def _unpack_rows(p):
    lo = pltpu.bitcast(p << 16, F32)
    hi = pltpu.bitcast(p & jnp.uint32(0xFFFF0000), F32)
    return lo, hi


def _route(sel, scores):
    t = sel.shape[-1]
    per = N_EXPERTS // N_EXPERT_GROUPS
    shape3 = (N_EXPERT_GROUPS, per, t)
    sel3 = sel.reshape(shape3)
    sc3 = scores.reshape(shape3)
    iota_g = lax.broadcasted_iota(I32, shape3, 0)
    iota_m = lax.broadcasted_iota(I32, shape3, 1)
    iota_e = iota_g * per + iota_m
    m1 = jnp.max(sel3, axis=1, keepdims=True)
    i1 = jnp.min(jnp.where(sel3 == m1, iota_m, per), axis=1, keepdims=True)
    m2 = jnp.max(jnp.where(iota_m == i1, REMOVED, sel3), axis=1, keepdims=True)
    gscore = m1 + m2
    iota_g1 = lax.broadcasted_iota(I32, gscore.shape, 0)
    gmask = jnp.zeros(gscore.shape, jnp.bool_)
    for _ in range(TOPK_GROUPS):
        mx = jnp.max(gscore, axis=0, keepdims=True)
        ix = jnp.min(jnp.where(gscore == mx, iota_g1, N_EXPERT_GROUPS), axis=0, keepdims=True)
        hit = iota_g1 == ix
        gmask = gmask | hit
        gscore = jnp.where(hit, REMOVED, gscore)
    cur = jnp.where(gmask, sel3, NEG_INF)
    sum_all = lambda a: jnp.sum(jnp.sum(a, axis=1, keepdims=True), axis=0, keepdims=True)
    idxs, wts, hits = [], [], []
    for _ in range(TOP_K):
        mx = jnp.max(jnp.max(cur, axis=1, keepdims=True), axis=0, keepdims=True)
        cand = jnp.where(cur == mx, iota_e, N_EXPERTS)
        ix = jnp.min(jnp.min(cand, axis=1, keepdims=True), axis=0, keepdims=True)
        hit = iota_e == ix
        wts.append(sum_all(jnp.where(hit, sc3, 0.0)).reshape(1, t))
        cur = jnp.where(hit, REMOVED, cur)
        idxs.append(ix.reshape(1, t))
        hits.append(hit)
    wsum = wts[0]
    for w in wts[1:]:
        wsum = wsum + w
    wts = [w / wsum * ROUTED_SCALE for w in wts]
    onehot = hits[0].astype(F32)
    for hit in hits[1:]:
        onehot = onehot + hit.astype(F32)
    tri = (lax.broadcasted_iota(I32, (t, t), 0) <= lax.broadcasted_iota(I32, (t, t), 1))
    cum = jnp.dot(onehot.reshape(N_EXPERTS, t).astype(BF16), tri.astype(F32).astype(BF16),
                  preferred_element_type=F32)
    cum3 = cum.reshape(shape3) - 1.0
    ranks = [sum_all(jnp.where(hit, cum3, 0.0)).reshape(1, t).astype(I32) for hit in hits]
    counts = cum[:, t - 1:t].astype(I32)
    return idxs, wts, ranks, counts


def _merge_kernel(x_ref, ya_ref, o1, o2, o3, l1, l2, l3, sga_ref, sgb_ref, mod_ref, n2_ref,
                  wpa, wpb, wo, wrt, rbias, wsg, wsu, wsd,
                  base_ref, hp_ref, ridx_ref, rw_ref, rank_ref, cnt_ref, scr):
    mod = mod_ref[0]
    tm = x_ref.shape[0]

    def token_major(ref, slot, d):
        if d == 1:
            return ref[...].astype(F32)
        nc = B_OUT_W // LANES
        for c in range(nc):
            for res in range(d):
                col = res * B_OUT_W + c * LANES
                scr[slot * nc + c, pl.ds(res, tm // d, stride=d), :] = ref[
                    :, col:col + LANES].astype(F32)
        return jnp.concatenate([scr[slot * nc + c] for c in range(nc)], axis=1)

    dils = [d for _, d in B_GROUPS]
    os_ = [token_major(r, i, d) for i, (r, d) in enumerate(zip((o1, o2, o3), dils))]
    ls = [token_major(r, 3 + i, d) for i, (r, d) in enumerate(zip((l1, l2, l3), dils))]
    mx = jnp.maximum(jnp.maximum(ls[0], ls[1]), ls[2])
    es = [jnp.exp(l - mx) for l in ls]
    den = es[0] + es[1] + es[2]
    ob = (es[0] / den) * os_[0] + (es[1] / den) * os_[1] + (es[2] / den) * os_[2]
    pa = jnp.dot(ya_ref[...], wpa[...], preferred_element_type=F32)
    pb = jnp.dot(ob.astype(BF16), wpb[...], preferred_element_type=F32)
    merged = sga_ref[...].astype(F32) * pa + sgb_ref[...].astype(F32) * pb
    mix = jnp.dot(merged.astype(BF16), wo[...], preferred_element_type=F32)
    x1 = x_ref[...] + mod[2:3] * mix
    h2 = _rms(x1, n2_ref[...]) * (1.0 + mod[4:5]) + mod[3:4]
    logits = lax.dot_general(wrt[...], h2, (((1,), (1,)), ((), ())), preferred_element_type=F32,
                             precision=lax.Precision.HIGHEST)
    scores = jax.nn.sigmoid(logits)
    idxs, wts, ranks, counts = _route(scores + rbias[...], scores)
    for k in range(TOP_K):
        ridx_ref[k:k + 1, :] = idxs[k]
        rw_ref[k:k + 1, :] = wts[k]
        rank_ref[k:k + 1, :] = ranks[k]
    ridx_ref[TOP_K:, :] = jnp.zeros((8 - TOP_K, tm), I32)
    rw_ref[TOP_K:, :] = jnp.zeros((8 - TOP_K, tm), F32)
    rank_ref[TOP_K:, :] = jnp.zeros((8 - TOP_K, tm), I32)
    cnt_ref[0] = counts
    hb = h2.astype(BF16)
    g = jnp.dot(hb, wsg[...], preferred_element_type=F32)
    u = jnp.dot(hb, wsu[...], preferred_element_type=F32)
    act = (g * jax.nn.sigmoid(g) * u).astype(BF16)
    shared = jnp.dot(act, wsd[...], preferred_element_type=F32)
    base_ref[...] = x1 + mod[5:6] * shared
    hp_ref[...] = _pack_rows(h2)


def _merge(x2, ya, outs, lses, sga, sgb, mod3, norm2, wpa, wpb, wo, wrt, rbias, wsg, wsu, wsd, seq):
    n = x2.shape[0]
    tm = TM_MERGE
    assert seq % tm == 0
    row = lambda i: (i, 0)
    full = lambda a: pl.BlockSpec(a.shape, lambda i: (0,) * a.ndim)
    weights = [wpa, wpb, wo, wrt, rbias, wsg, wsu, wsd]
    group_specs = [pl.BlockSpec((tm // d, d * B_OUT_W), row) for _, d in B_GROUPS]
    lanes = lambda i: (0, i)
    return pl.pallas_call(
        _merge_kernel,
        out_shape=[jax.ShapeDtypeStruct((n, D_MODEL), F32),
                   jax.ShapeDtypeStruct((n, HALF_D), U32),
                   jax.ShapeDtypeStruct((8, n), I32),
                   jax.ShapeDtypeStruct((8, n), F32),
                   jax.ShapeDtypeStruct((8, n), I32),
                   jax.ShapeDtypeStruct((n // tm, N_EXPERTS, 1), I32)],
        grid=(n // tm,),
        in_specs=[pl.BlockSpec((tm, D_MODEL), row), pl.BlockSpec((tm, A_Q_W), row)]
                 + group_specs * 2
                 + [pl.BlockSpec((tm, D_MODEL), row)] * 2
                 + [pl.BlockSpec((1, 6, D_MODEL), lambda i: (i * tm // seq, 0, 0)),
                    pl.BlockSpec((1, D_MODEL), lambda i: (0, 0))]
                 + [full(w) for w in weights],
        out_specs=[pl.BlockSpec((tm, D_MODEL), row), pl.BlockSpec((tm, HALF_D), row),
                   pl.BlockSpec((8, tm), lanes), pl.BlockSpec((8, tm), lanes),
                   pl.BlockSpec((8, tm), lanes),
                   pl.BlockSpec((1, N_EXPERTS, 1), lambda i: (i, 0, 0))],
        scratch_shapes=[pltpu.VMEM((6 * B_OUT_W // LANES, tm, LANES), F32)],
        compiler_params=_cparams(1),
        name="merge_route",
    )(x2, ya, *outs, *lses, sga, sgb, mod3, norm2.reshape(1, D_MODEL), *weights)


def _plan_kernel(ridx_ref, rank_ref, tb_ref, dest_ref):
    tb = tb_ref[0]
    t = ridx_ref.shape[1]
    iota_e = lax.broadcasted_iota(I32, (N_EXPERTS, t), 0)
    for k in range(TOP_K):
        base = jnp.sum(jnp.where(iota_e == ridx_ref[k:k + 1, :], tb, 0), axis=0, keepdims=True)
        dest_ref[0, :, k * t:(k + 1) * t] = base + rank_ref[k:k + 1, :]


def _plan(ridx, rank, tile_base, tm):
    n = ridx.shape[1]
    lanes = lambda i: (0, i)
    return pl.pallas_call(
        _plan_kernel,
        out_shape=jax.ShapeDtypeStruct((n // tm, 1, TOP_K * tm), I32),
        grid=(n // tm,),
        in_specs=[pl.BlockSpec((8, tm), lanes), pl.BlockSpec((8, tm), lanes),
                  pl.BlockSpec((1, N_EXPERTS, 1), lambda i: (i, 0, 0))],
        out_specs=pl.BlockSpec((1, 1, TOP_K * tm), lambda i: (i, 0, 0)),
        compiler_params=_cparams(1),
        name="moe_plan",
    )(ridx, rank, tile_base)


def _dispatch_kernel(dest_ref, h_ref, xs_out, sem, *, tm):
    def issue(i, carry):
        for k in range(TOP_K):
            d = dest_ref[0, 0, k * tm + i]
            pltpu.make_async_copy(h_ref.at[pl.ds(i, 1)], xs_out.at[pl.ds(d, 1)], sem).start()
        return carry

    lax.fori_loop(0, tm, issue, 0)

    def drain(i, carry):
        for k in range(TOP_K):
            pltpu.make_async_copy(h_ref.at[pl.ds(0, 1)], xs_out.at[pl.ds(0, 1)], sem).wait()
        return carry

    lax.fori_loop(0, tm, drain, 0)


def _dispatch(hp, dest_tiles, n_rows):
    n = hp.shape[0]
    tm = TM_DISPATCH
    return pl.pallas_call(
        functools.partial(_dispatch_kernel, tm=tm),
        out_shape=jax.ShapeDtypeStruct((n_rows, HALF_D), U32),
        grid=(n // tm,),
        in_specs=[pl.BlockSpec((1, 1, TOP_K * tm), lambda i: (i, 0, 0), memory_space=pltpu.SMEM),
                  pl.BlockSpec((tm, HALF_D), lambda i: (i, 0))],
        out_specs=pl.BlockSpec(memory_space=pl.ANY),
        scratch_shapes=[pltpu.SemaphoreType.DMA],
        compiler_params=_cparams(1),
        name="moe_dispatch",
    )(dest_tiles, hp)


def _expert_kernel(blk_e_ref, valid_ref, xs_ref, wgu_ref, wd_ref, ys_ref):
    del blk_e_ref
    valid = valid_ref[pl.program_id(0)]

    @pl.when(valid > 0)
    def _():
        rows = lax.broadcasted_iota(I32, (xs_ref.shape[0], 1), 0)
        lo, hi = _unpack_rows(jnp.where(rows < valid, xs_ref[...], jnp.uint32(0)))
        wgu = wgu_ref[0]
        gu = jnp.dot(lo.astype(BF16), wgu[:HALF_D], preferred_element_type=F32) \
            + jnp.dot(hi.astype(BF16), wgu[HALF_D:], preferred_element_type=F32)
        g = gu[:, :D_EXPERT]
        u = gu[:, D_EXPERT:]
        act = (g * jax.nn.sigmoid(g) * u).astype(BF16)
        ys_ref[...] = _pack_rows(jnp.dot(act, wd_ref[0], preferred_element_type=F32))

    @pl.when(valid <= 0)
    def _():
        ys_ref[...] = jnp.zeros(ys_ref.shape, U32)


def _experts(xs, blk_e, blk_valid, wgu, wd):
    n_rows = xs.shape[0]
    bm = MOE_BM
    nblk = n_rows // bm
    return pl.pallas_call(
        _expert_kernel,
        out_shape=jax.ShapeDtypeStruct((n_rows, HALF_D), U32),
        grid_spec=pltpu.PrefetchScalarGridSpec(
            num_scalar_prefetch=2,
            grid=(nblk,),
            in_specs=[pl.BlockSpec((bm, HALF_D), lambda i, be, nu: (i, 0)),
                      pl.BlockSpec((1, D_MODEL, 2 * D_EXPERT), lambda i, be, nu: (be[i], 0, 0)),
                      pl.BlockSpec((1, D_EXPERT, D_MODEL), lambda i, be, nu: (be[i], 0, 0))],
            out_specs=pl.BlockSpec((bm, HALF_D), lambda i, be, nu: (i, 0))),
        compiler_params=_cparams(1),
        name="moe_experts",
    )(blk_e, blk_valid, xs, wgu, wd)


def _combine_kernel(dest_ref, ys_ref, base_ref, rw_ref, mod_ref, fn_ref, y_ref, buf, sem, *, tm):
    def issue(i, carry):
        for k in range(TOP_K):
            d = dest_ref[0, 0, k * tm + i]
            pltpu.make_async_copy(ys_ref.at[pl.ds(d, 1)], buf.at[k, pl.ds(i, 1)], sem).start()
        return carry

    lax.fori_loop(0, tm, issue, 0)

    def drain(i, carry):
        for k in range(TOP_K):
            pltpu.make_async_copy(ys_ref.at[pl.ds(0, 1)], buf.at[k, pl.ds(0, 1)], sem).wait()
        return carry

    lax.fori_loop(0, tm, drain, 0)

    rw = rw_ref[...]
    acc_lo = jnp.zeros((tm, HALF_D), F32)
    acc_hi = jnp.zeros((tm, HALF_D), F32)
    for k in range(TOP_K):
        lo, hi = _unpack_rows(buf[k])
        w = rw[:, k:k + 1]
        acc_lo = acc_lo + w * lo
        acc_hi = acc_hi + w * hi
    routed = jnp.concatenate([acc_lo, acc_hi], axis=1)
    x2 = base_ref[...] + mod_ref[0][5:6] * routed
    y_ref[...] = _rms(x2, fn_ref[...])


def _combine(ys, dest_tiles, base, rw_t, mod3, final_norm, seq):
    n = base.shape[0]
    tm = TM_COMBINE
    assert seq % tm == 0
    return pl.pallas_call(
        functools.partial(_combine_kernel, tm=tm),
        out_shape=jax.ShapeDtypeStruct((n, D_MODEL), F32),
        grid=(n // tm,),
        in_specs=[pl.BlockSpec((1, 1, TOP_K * tm), lambda i: (i, 0, 0), memory_space=pltpu.SMEM),
                  pl.BlockSpec(memory_space=pl.ANY),
                  pl.BlockSpec((tm, D_MODEL), lambda i: (i, 0)),
                  pl.BlockSpec((tm, 8), lambda i: (i, 0)),
                  pl.BlockSpec((1, 6, D_MODEL), lambda i: (i * tm // seq, 0, 0)),
                  pl.BlockSpec((1, D_MODEL), lambda i: (0, 0))],
        out_specs=pl.BlockSpec((tm, D_MODEL), lambda i: (i, 0)),
        scratch_shapes=[pltpu.VMEM((TOP_K, tm, HALF_D), U32), pltpu.SemaphoreType.DMA],
        compiler_params=_cparams(1),
        name="moe_combine",
    )(dest_tiles, ys, base, rw_t, mod3, final_norm.reshape(1, D_MODEL))


def _block_layout(tile_counts, n):
    bm = MOE_BM
    c = tile_counts[:, :, 0]
    ntiles = c.shape[0]
    counts = jnp.sum(c, axis=0)
    padded = (counts + bm - 1) // bm * bm
    earlier_e = np.tri(N_EXPERTS, k=-1, dtype=bool)
    pstart = jnp.sum(jnp.where(earlier_e, padded[None, :], 0), axis=1)
    pend = pstart + padded
    earlier_t = np.tri(ntiles, k=-1, dtype=bool)
    tile_base = pstart[None, :] + jnp.sum(jnp.where(earlier_t[:, :, None], c[None], 0), axis=1)
    nblk = -(-n * TOP_K // bm) + N_EXPERTS
    blk = jnp.arange(nblk, dtype=I32)
    blk_e = jnp.minimum(jnp.sum((pend[None, :] <= blk[:, None] * bm).astype(I32), axis=1),
                        N_EXPERTS - 1)
    onehot_e = blk_e[:, None] == jnp.arange(N_EXPERTS, dtype=I32)[None, :]
    end_e = jnp.sum(jnp.where(onehot_e, (pstart + counts)[None, :], 0), axis=1)
    blk_valid = jnp.clip(end_e - blk * bm, 0, bm).astype(I32)
    return tile_base.astype(I32)[:, :, None], blk_e.astype(I32), blk_valid, nblk * bm


def _trunk(x, mod3, p):
    nbatch, seq, _ = x.shape
    n = nbatch * seq
    x2 = x.reshape(n, D_MODEL)
    proj = _inproj(x2, mod3, p["norm1"], p["w_in"], seq)
    qa = proj[0]
    sga, sgb = proj[14:]
    (ya,) = _band_attention(qa, proj[1:5], p["bias_a"], p["sink_a"], nbatch=nbatch, seq=seq, dil=1,
                            halo=A_HALF_WINDOW, kv_width=A_KV_W, want_lse=False)
    outs, lses = [], []
    for gi, (w, d) in enumerate(B_GROUPS):
        q, k, v = proj[5 + 3 * gi:8 + 3 * gi]
        o, lse = _band_attention(q, (k, v), p["bias_b"][gi], None, nbatch=nbatch, seq=seq, dil=d,
                                 halo=w // (2 * d), kv_width=B_OUT_W, want_lse=True)
        outs.append(o)
        lses.append(lse)
    base, hp, ridx, rw, rank, tile_counts = _merge(
        x2, ya, outs, lses, sga, sgb, mod3, p["norm2"], p["w_pa"], p["w_pb"], p["w_o"], p["w_rt"],
        p["rbias"], p["ws_gate"], p["ws_up"], p["ws_down"], seq)
    assert TM_MERGE == TM_DISPATCH == TM_COMBINE
    tile_base, blk_e, blk_valid, n_rows = _block_layout(tile_counts, n)
    dest_tiles = _plan(ridx, rank, tile_base, TM_MERGE)
    xs = _dispatch(hp, dest_tiles, n_rows)
    ys = _experts(xs, blk_e, blk_valid, p["w_gu"], p["w_down"])
    y = _combine(ys, dest_tiles, base, rw.T, mod3, p["final_norm"], seq)
    return y.reshape(nbatch, seq, D_MODEL)


def kernel(x_prompt, x_sample, c_prompt, c_sample, rel_bias, w_ada, b_ada, norm1, w_in, sink, w_pa, w_pb, w_o, norm2, w_router, router_bias, w_gate, w_up, w_down, ws_gate, ws_up, ws_down, final_norm):
    assert w_ada.shape[0] == 1
    nbp = x_prompt.shape[0]
    mod = _ada(jnp.concatenate([c_prompt, c_sample], axis=0), w_ada[0], b_ada[0])
    mod3 = mod.reshape(-1, 6, D_MODEL)

    def pair_bias(heads, halo, dist_scale):
        kb = ATTN_BLOCK + 2 * halo
        tab = _bias_table(heads, ATTN_BLOCK, kb, halo, halo, dist_scale)
        return tab.reshape(-1, 2, ATTN_BLOCK, kb).transpose(0, 2, 1, 3).reshape(-1, ATTN_BLOCK, 2 * kb)

    bias_a = pair_bias(rel_bias[:, :A_Q_HEADS], A_HALF_WINDOW, 1)
    bias_b = []
    for gi, (w, d) in enumerate(B_GROUPS):
        h0 = A_Q_HEADS + gi * B_HEADS_PER_GROUP
        bias_b.append(pair_bias(rel_bias[:, h0:h0 + B_HEADS_PER_GROUP], w // (2 * d), d))
    p = {
        "norm1": norm1[0], "norm2": norm2[0], "final_norm": final_norm,
        "w_in": w_in[0].astype(BF16),
        "bias_a": bias_a,
        "sink_a": jnp.repeat(sink[0].astype(F32), HEAD_DIM).reshape(A_Q_HEADS // 2, 1, LANES),
        "bias_b": bias_b,
        "w_pa": w_pa[0].astype(BF16), "w_pb": w_pb[0].astype(BF16), "w_o": w_o[0].astype(BF16),
        "w_rt": w_router[0].T, "rbias": router_bias[0].reshape(N_EXPERTS, 1),
        "ws_gate": ws_gate[0].astype(BF16), "ws_up": ws_up[0].astype(BF16),
        "ws_down": ws_down[0].astype(BF16),
        "w_gu": jnp.concatenate([w_gate[0], w_up[0]], axis=-1).astype(BF16),
        "w_down": w_down[0].astype(BF16),
    }
    y_prompt = _trunk(x_prompt, mod3[:nbp], p)
    y_sample = _trunk(x_sample, mod3[nbp:], p)
    return (y_prompt, y_sample)
```

```python
import functools
import math

import jax
import jax.numpy as jnp
import numpy as np
from jax import lax
from jax.experimental import pallas as pl
from jax.experimental.pallas import tpu as pltpu

F32 = jnp.float32
BF16 = jnp.bfloat16
U32 = jnp.uint32
I32 = jnp.int32

D_MODEL = 1024
HEAD_DIM = 64
A_Q_HEADS = 8
A_KV_HEADS = 2
A_HALF_WINDOW = 128
B_GROUPS = ((128, 1), (512, 4), (2048, 16))
B_HEADS_PER_GROUP = 4
N_BUCKETS = 32
MAX_DISTANCE = 1024
N_EXPERTS = 64
TOP_K = 6
N_EXPERT_GROUPS = 8
TOPK_GROUPS = 4
D_EXPERT = 256
D_SHARED = 256
ROUTED_SCALE = 2.5
RMS_EPS = 1e-6
NEG_INF = -1e30
REMOVED = -3e38

A_Q_W = A_Q_HEADS * HEAD_DIM
A_KV_W = A_KV_HEADS * HEAD_DIM
B_W = len(B_GROUPS) * B_HEADS_PER_GROUP * HEAD_DIM
B_OUT_W = B_HEADS_PER_GROUP * HEAD_DIM
IN_WIDTHS = (A_Q_W, A_KV_W, A_KV_W, B_W, B_W, B_W, D_MODEL, D_MODEL)
D_IN = sum(IN_WIDTHS)
HALF_D = D_MODEL // 2
LANES = 128

TM_INPROJ = 512
TM_MERGE = 256
ATTN_BLOCK = 128
ATTN_ROWS = 512
MOE_BM = 512
TM_DISPATCH = 256
TM_COMBINE = 256
VMEM_LIMIT = 48 * 1024 * 1024


def _cparams(n_axes):
    return pltpu.CompilerParams(
        dimension_semantics=("arbitrary",) * n_axes, vmem_limit_bytes=VMEM_LIMIT)


def _ada_kernel(c_ref, w_ref, b_ref, o_ref):
    c = c_ref[...]
    s = c * jax.nn.sigmoid(c)
    o_ref[...] = jnp.dot(s, w_ref[...], preferred_element_type=F32,
                         precision=lax.Precision.HIGHEST) + b_ref[...]


def _ada(c_all, w_ada, b_ada):
    nb = c_all.shape[0]
    return pl.pallas_call(
        _ada_kernel,
        out_shape=jax.ShapeDtypeStruct((nb, 6 * D_MODEL), F32),
        grid=(6,),
        in_specs=[pl.BlockSpec((nb, D_MODEL), lambda j: (0, 0)),
                  pl.BlockSpec((D_MODEL, D_MODEL), lambda j: (0, j)),
                  pl.BlockSpec((1, D_MODEL), lambda j: (0, j))],
        out_specs=pl.BlockSpec((nb, D_MODEL), lambda j: (0, j)),
        compiler_params=_cparams(1),
        name="ada",
    )(c_all, w_ada, b_ada.reshape(1, 6 * D_MODEL))


def _rms(x, g):
    return x * lax.rsqrt(jnp.mean(x * x, axis=-1, keepdims=True) + RMS_EPS) * g


def _inproj_kernel(x_ref, mod_ref, n1_ref, w_ref, qa, ka, ka_sw, va, va_sw, *rest):
    qkv_b = rest[:9]
    sga, sgb, scr = rest[9:]
    mod = mod_ref[0]
    h = _rms(x_ref[...], n1_ref[...]) * (1.0 + mod[1:2]) + mod[0:1]
    hb = h.astype(BF16)
    tm = hb.shape[0]
    scale = HEAD_DIM ** -0.5

    def proj(off, width):
        return jnp.dot(hb, w_ref[:, off:off + width], preferred_element_type=F32)

    qa[...] = (proj(0, A_Q_W) * scale).astype(BF16)
    for ref, ref_sw, off in ((ka, ka_sw, A_Q_W), (va, va_sw, A_Q_W + A_KV_W)):
        r = proj(off, A_KV_W)
        ref[...] = r.astype(BF16)
        ref_sw[...] = pltpu.roll(r, HEAD_DIM, 1).astype(BF16)
    off = A_Q_W + 2 * A_KV_W
    for t in range(3):
        for gi, (_, d) in enumerate(B_GROUPS):
            r = proj(off + t * B_W + gi * B_OUT_W, B_OUT_W)
            if t == 0:
                r = r * scale
            ref = qkv_b[gi * 3 + t]
            if d == 1:
                ref[...] = r.astype(BF16)
            else:
                for c in range(B_OUT_W // LANES):
                    slot = (t * 2 + gi - 1) * (B_OUT_W // LANES) + c
                    scr[slot] = r[:, c * LANES:(c + 1) * LANES]
                    for res in range(d):
                        col = res * B_OUT_W + c * LANES
                        ref[:, col:col + LANES] = scr[
                            slot, pl.ds(res, tm // d, stride=d), :].astype(BF16)
    off += 3 * B_W
    sga[...] = jax.nn.sigmoid(proj(off, D_MODEL)).astype(BF16)
    sgb[...] = jax.nn.sigmoid(proj(off + D_MODEL, D_MODEL)).astype(BF16)


def _inproj(x2, mod3, norm1, w_in_bf, seq):
    n = x2.shape[0]
    tm = TM_INPROJ
    assert seq % tm == 0 and n % tm == 0
    row = lambda i: (i, 0)
    shapes = [(n, A_Q_W, tm)] + [(n, A_KV_W, tm)] * 4
    for _, d in B_GROUPS:
        shapes += [(n // d, d * B_OUT_W, tm // d)] * 3
    shapes += [(n, D_MODEL, tm)] * 2
    return pl.pallas_call(
        _inproj_kernel,
        out_shape=[jax.ShapeDtypeStruct((r, c), BF16) for r, c, _ in shapes],
        grid=(n // tm,),
        in_specs=[pl.BlockSpec((tm, D_MODEL), row),
                  pl.BlockSpec((1, 6, D_MODEL), lambda i: (i * tm // seq, 0, 0)),
                  pl.BlockSpec((1, D_MODEL), lambda i: (0, 0)),
                  pl.BlockSpec((D_MODEL, D_IN), lambda i: (0, 0))],
        out_specs=[pl.BlockSpec((b, c), row) for _, c, b in shapes],
        scratch_shapes=[pltpu.VMEM((6 * B_OUT_W // LANES, tm, LANES), F32)],
        compiler_params=_cparams(1),
        name="inproj",
    )(x2, mod3, norm1.reshape(1, D_MODEL), w_in_bf)


def _rel_bucket_np(rel):
    half = N_BUCKETS // 2
    max_exact = half // 2
    n = np.abs(rel)
    large = max_exact + (np.log(np.maximum(n, 1) / max_exact) / math.log(MAX_DISTANCE / max_exact)
                         * (half - max_exact)).astype(np.int32)
    large = np.minimum(large, half - 1)
    return ((rel > 0).astype(np.int32) * half + np.where(n < max_exact, n, large)).astype(np.int32)


def _bias_table(rel_bias_heads, n_q, n_k, key_off, band, dist_scale):
    p = n_q + n_k
    rel = np.arange(p) - (n_q - 1) - key_off
    bucket = _rel_bucket_np(rel * dist_scale)
    t = jnp.where((np.abs(rel) <= band)[None], rel_bias_heads.astype(F32)[bucket].T, NEG_INF)
    big = jnp.tile(t, (1, n_q + 1))
    tab = big[:, n_q - 1:n_q - 1 + n_q * (p - 1)].reshape(-1, n_q, p - 1)
    return tab[:, :, :n_k]


def _pair_rhs(k_top, k_bot, v_top, v_bot):
    kb = k_top.shape[0]
    low = jnp.where(lax.broadcasted_iota(I32, (kb, LANES), 1) < HEAD_DIM, 1.0, 0.0).astype(BF16)
    high = jnp.where(lax.broadcasted_iota(I32, (kb, LANES), 1) < HEAD_DIM, 0.0, 1.0).astype(BF16)
    rhs_k = jnp.concatenate([k_top * low, k_bot * high], axis=0)
    rhs_v = jnp.concatenate([jnp.concatenate([v_top * low, low], axis=1),
                             jnp.concatenate([v_bot * high, high], axis=1)], axis=0)
    return rhs_k, rhs_v


def _pair_attention(q_pair, rhs_k, rhs_v, bias_pair, pen, sink_pair, want_lse):
    kb = rhs_k.shape[0] // 2
    s = lax.dot_general(q_pair, rhs_k, (((1,), (1,)), ((), ())), preferred_element_type=F32)
    s = s + bias_pair
    if pen is not None:
        s = s + pen
    s0, s1 = s[:, :kb], s[:, kb:]
    m0 = jnp.max(s0, axis=-1, keepdims=True)
    m1 = jnp.max(s1, axis=-1, keepdims=True)
    if sink_pair is not None:
        m0 = jnp.maximum(m0, sink_pair[:, 0:1])
        m1 = jnp.maximum(m1, sink_pair[:, HEAD_DIM:HEAD_DIM + 1])
    p = jnp.concatenate([jnp.exp(s0 - m0), jnp.exp(s1 - m1)], axis=1).astype(BF16)
    od = jnp.dot(p, rhs_v, preferred_element_type=F32)
    o, den = od[:, :LANES], od[:, LANES:]
    low = lax.broadcasted_iota(I32, o.shape, 1) < HEAD_DIM
    m_full = jnp.where(low, m0, m1)
    if sink_pair is not None:
        den = den + jnp.exp(sink_pair - m_full)
    return o / den, (m_full + jnp.log(den)) if want_lse else None


def _attn_kernel(*refs, halo, nblk, n_pairs, shared_kv, has_sink, want_lse, n_steps):
    it = iter(refs)
    q_ref = next(it)
    n_kv = 4 if shared_kv else 2
    kv_refs = [[next(it) for _ in range(3)] for _ in range(n_kv)]
    bias_ref = next(it)
    sink_ref = next(it) if has_sink else None
    o_ref = next(it)
    lse_ref = next(it) if want_lse else None
    step = pl.program_id(2)
    blk = ATTN_BLOCK
    kb = blk + 2 * halo
    cats = [jnp.concatenate([r[...] for r in trio], axis=0) for trio in kv_refs]
    col = lax.broadcasted_iota(I32, (1, kb), 1)
    for b in range(nblk):
        lo = jnp.where(step == 0, halo, 0) if b == 0 else 0
        hi = jnp.where(step == n_steps - 1, kb - halo, kb) if b == nblk - 1 else kb
        pen = None
        if b == 0 or b == nblk - 1:
            pen1 = jnp.where((col >= lo) & (col < hi), 0.0, NEG_INF)
            pen = jnp.concatenate([pen1, pen1], axis=1)
        rows = slice(b * blk, (b + 1) * blk)
        krows = slice(b * blk, b * blk + kb)
        if shared_kv:
            k, k_sw, v, v_sw = (c[krows] for c in cats)
            rhs = [_pair_rhs(k, k_sw, v, v_sw), _pair_rhs(k_sw, k, v_sw, v)]
        for c in range(n_pairs):
            lanes = slice(c * LANES, (c + 1) * LANES)
            if shared_kv:
                rhs_k, rhs_v = rhs[c // (n_pairs // 2)]
            else:
                k, v = cats[0][krows, lanes], cats[1][krows, lanes]
                rhs_k, rhs_v = _pair_rhs(k, k, v, v)
            o, lse = _pair_attention(q_ref[rows, lanes], rhs_k, rhs_v, bias_ref[c], pen,
                                     sink_ref[c] if has_sink else None, want_lse)
            o_ref[rows, lanes] = o.astype(o_ref.dtype)
            if want_lse:
                lse_ref[rows, lanes] = lse


def _band_attention(q, kvs, bias, sink, *, nbatch, seq, dil, halo, kv_width, want_lse):
    n = nbatch * seq
    sub_len = seq // dil
    qw = q.shape[1] // dil
    rows = min(ATTN_ROWS, sub_len)
    assert sub_len % rows == 0 and rows % ATTN_BLOCK == 0 and rows % halo == 0
    nq = sub_len // rows
    per = rows // halo
    total_halos = n // dil // halo
    cur = lambda b, r, i: (b * nq + i, r)
    prev = lambda b, r, i: (jnp.maximum((b * nq + i) * per - 1, 0), r)
    nxt = lambda b, r, i: (jnp.minimum((b * nq + i + 1) * per, total_halos - 1), r)
    const = lambda a: pl.BlockSpec(a.shape, lambda b, r, i: (0,) * a.ndim)

    in_specs = [pl.BlockSpec((rows, qw), cur)]
    args = [q]
    for a in kvs:
        in_specs += [pl.BlockSpec((halo, kv_width), prev), pl.BlockSpec((rows, kv_width), cur),
                     pl.BlockSpec((halo, kv_width), nxt)]
        args += [a, a, a]
    in_specs.append(const(bias))
    args.append(bias)
    if sink is not None:
        in_specs.append(const(sink))
        args.append(sink)
    out_shape = [jax.ShapeDtypeStruct(q.shape, BF16)]
    out_specs = [pl.BlockSpec((rows, qw), cur)]
    if want_lse:
        out_shape.append(jax.ShapeDtypeStruct(q.shape, F32))
        out_specs.append(pl.BlockSpec((rows, qw), cur))
    return pl.pallas_call(
        functools.partial(_attn_kernel, halo=halo, nblk=rows // ATTN_BLOCK, n_pairs=qw // LANES,
                          shared_kv=len(kvs) == 4, has_sink=sink is not None, want_lse=want_lse,
                          n_steps=nq),
        out_shape=out_shape,
        grid=(nbatch, dil, nq),
        in_specs=in_specs,
        out_specs=out_specs,
        compiler_params=_cparams(3),
        name=f"band_attn_d{dil}",
    )(*args)


def _pack_rows(y):
    lo = pltpu.bitcast(y[:, :HALF_D].astype(BF16).astype(F32), U32)
    hi = pltpu.bitcast(y[:, HALF_D:].astype(BF16).astype(F32), U32)
    return (hi & jnp.uint32(0xFFFF0000)) | (lo >> 16)


def _unpack_rows(p):
    lo = pltpu.bitcast(p << 16, F32)
    hi = pltpu.bitcast(p & jnp.uint32(0xFFFF0000), F32)
    return lo, hi


def _route(sel, scores):
    t = sel.shape[-1]
    per = N_EXPERTS // N_EXPERT_GROUPS
    shape3 = (N_EXPERT_GROUPS, per, t)
    sel3 = sel.reshape(shape3)
    sc3 = scores.reshape(shape3)
    iota_g = lax.broadcasted_iota(I32, shape3, 0)
    iota_m = lax.broadcasted_iota(I32, shape3, 1)
    iota_e = iota_g * per + iota_m
    m1 = jnp.max(sel3, axis=1, keepdims=True)
    i1 = jnp.min(jnp.where(sel3 == m1, iota_m, per), axis=1, keepdims=True)
    m2 = jnp.max(jnp.where(iota_m == i1, REMOVED, sel3), axis=1, keepdims=True)
    gscore = m1 + m2
    iota_g1 = lax.broadcasted_iota(I32, gscore.shape, 0)
    gmask = jnp.zeros(gscore.shape, jnp.bool_)
    for _ in range(TOPK_GROUPS):
        mx = jnp.max(gscore, axis=0, keepdims=True)
        ix = jnp.min(jnp.where(gscore == mx, iota_g1, N_EXPERT_GROUPS), axis=0, keepdims=True)
        hit = iota_g1 == ix
        gmask = gmask | hit
        gscore = jnp.where(hit, REMOVED, gscore)
    cur = jnp.where(gmask, sel3, NEG_INF)
    sum_all = lambda a: jnp.sum(jnp.sum(a, axis=1, keepdims=True), axis=0, keepdims=True)
    idxs, wts, hits = [], [], []
    for _ in range(TOP_K):
        mx = jnp.max(jnp.max(cur, axis=1, keepdims=True), axis=0, keepdims=True)
        cand = jnp.where(cur == mx, iota_e, N_EXPERTS)
        ix = jnp.min(jnp.min(cand, axis=1, keepdims=True), axis=0, keepdims=True)
        hit = iota_e == ix
        wts.append(sum_all(jnp.where(hit, sc3, 0.0)).reshape(1, t))
        cur = jnp.where(hit, REMOVED, cur)
        idxs.append(ix.reshape(1, t))
        hits.append(hit)
    wsum = wts[0]
    for w in wts[1:]:
        wsum = wsum + w
    wts = [w / wsum * ROUTED_SCALE for w in wts]
    onehot = hits[0].astype(F32)
    for hit in hits[1:]:
        onehot = onehot + hit.astype(F32)
    tri = (lax.broadcasted_iota(I32, (t, t), 0) <= lax.broadcasted_iota(I32, (t, t), 1))
    cum = jnp.dot(onehot.reshape(N_EXPERTS, t).astype(BF16), tri.astype(F32).astype(BF16),
                  preferred_element_type=F32)
    cum3 = cum.reshape(shape3) - 1.0
    ranks = [sum_all(jnp.where(hit, cum3, 0.0)).reshape(1, t).astype(I32) for hit in hits]
    counts = cum[:, t - 1:t].astype(I32)
    return idxs, wts, ranks, counts


def _merge_kernel(x_ref, ya_ref, o1, o2, o3, l1, l2, l3, sga_ref, sgb_ref, mod_ref, n2_ref,
                  wpa, wpb, wo, wrt, rbias, wsg, wsu, wsd,
                  base_ref, hp_ref, ridx_ref, rw_ref, rank_ref, cnt_ref, scr):
    mod = mod_ref[0]
    tm = x_ref.shape[0]

    def token_major(ref, slot, d):
        if d == 1:
            return ref[...].astype(F32)
        nc = B_OUT_W // LANES
        for c in range(nc):
            for res in range(d):
                col = res * B_OUT_W + c * LANES
                scr[slot * nc + c, pl.ds(res, tm // d, stride=d), :] = ref[
                    :, col:col + LANES].astype(F32)
        return jnp.concatenate([scr[slot * nc + c] for c in range(nc)], axis=1)

    dils = [d for _, d in B_GROUPS]
    os_ = [token_major(r, i, d) for i, (r, d) in enumerate(zip((o1, o2, o3), dils))]
    ls = [token_major(r, 3 + i, d) for i, (r, d) in enumerate(zip((l1, l2, l3), dils))]
    mx = jnp.maximum(jnp.maximum(ls[0], ls[1]), ls[2])
    es = [jnp.exp(l - mx) for l in ls]
    den = es[0] + es[1] + es[2]
    ob = (es[0] / den) * os_[0] + (es[1] / den) * os_[1] + (es[2] / den) * os_[2]
    pa = jnp.dot(ya_ref[...], wpa[...], preferred_element_type=F32)
    pb = jnp.dot(ob.astype(BF16), wpb[...], preferred_element_type=F32)
    merged = sga_ref[...].astype(F32) * pa + sgb_ref[...].astype(F32) * pb
    mix = jnp.dot(merged.astype(BF16), wo[...], preferred_element_type=F32)
    x1 = x_ref[...] + mod[2:3] * mix
    h2 = _rms(x1, n2_ref[...]) * (1.0 + mod[4:5]) + mod[3:4]
    logits = lax.dot_general(wrt[...], h2, (((1,), (1,)), ((), ())), preferred_element_type=F32,
                             precision=lax.Precision.HIGHEST)
    scores = jax.nn.sigmoid(logits)
    idxs, wts, ranks, counts = _route(scores + rbias[...], scores)
    for k in range(TOP_K):
        ridx_ref[k:k + 1, :] = idxs[k]
        rw_ref[k:k + 1, :] = wts[k]
        rank_ref[k:k + 1, :] = ranks[k]
    ridx_ref[TOP_K:, :] = jnp.zeros((8 - TOP_K, tm), I32)
    rw_ref[TOP_K:, :] = jnp.zeros((8 - TOP_K, tm), F32)
    rank_ref[TOP_K:, :] = jnp.zeros((8 - TOP_K, tm), I32)
    cnt_ref[0] = counts
    hb = h2.astype(BF16)
    g = jnp.dot(hb, wsg[...], preferred_element_type=F32)
    u = jnp.dot(hb, wsu[...], preferred_element_type=F32)
    act = (g * jax.nn.sigmoid(g) * u).astype(BF16)
    shared = jnp.dot(act, wsd[...], preferred_element_type=F32)
    base_ref[...] = x1 + mod[5:6] * shared
    hp_ref[...] = _pack_rows(h2)


def _merge(x2, ya, outs, lses, sga, sgb, mod3, norm2, wpa, wpb, wo, wrt, rbias, wsg, wsu, wsd, seq):
    n = x2.shape[0]
    tm = TM_MERGE
    assert seq % tm == 0
    row = lambda i: (i, 0)
    full = lambda a: pl.BlockSpec(a.shape, lambda i: (0,) * a.ndim)
    weights = [wpa, wpb, wo, wrt, rbias, wsg, wsu, wsd]
    group_specs = [pl.BlockSpec((tm // d, d * B_OUT_W), row) for _, d in B_GROUPS]
    lanes = lambda i: (0, i)
    return pl.pallas_call(
        _merge_kernel,
        out_shape=[jax.ShapeDtypeStruct((n, D_MODEL), F32),
                   jax.ShapeDtypeStruct((n, HALF_D), U32),
                   jax.ShapeDtypeStruct((8, n), I32),
                   jax.ShapeDtypeStruct((8, n), F32),
                   jax.ShapeDtypeStruct((8, n), I32),
                   jax.ShapeDtypeStruct((n // tm, N_EXPERTS, 1), I32)],
        grid=(n // tm,),
        in_specs=[pl.BlockSpec((tm, D_MODEL), row), pl.BlockSpec((tm, A_Q_W), row)]
                 + group_specs * 2
                 + [pl.BlockSpec((tm, D_MODEL), row)] * 2
                 + [pl.BlockSpec((1, 6, D_MODEL), lambda i: (i * tm // seq, 0, 0)),
                    pl.BlockSpec((1, D_MODEL), lambda i: (0, 0))]
                 + [full(w) for w in weights],
        out_specs=[pl.BlockSpec((tm, D_MODEL), row), pl.BlockSpec((tm, HALF_D), row),
                   pl.BlockSpec((8, tm), lanes), pl.BlockSpec((8, tm), lanes),
                   pl.BlockSpec((8, tm), lanes),
                   pl.BlockSpec((1, N_EXPERTS, 1), lambda i: (i, 0, 0))],
        scratch_shapes=[pltpu.VMEM((6 * B_OUT_W // LANES, tm, LANES), F32)],
        compiler_params=_cparams(1),
        name="merge_route",
    )(x2, ya, *outs, *lses, sga, sgb, mod3, norm2.reshape(1, D_MODEL), *weights)


def _plan_kernel(ridx_ref, rank_ref, tb_ref, dest_ref):
    tb = tb_ref[0]
    t = ridx_ref.shape[1]
    iota_e = lax.broadcasted_iota(I32, (N_EXPERTS, t), 0)
    for k in range(TOP_K):
        base = jnp.sum(jnp.where(iota_e == ridx_ref[k:k + 1, :], tb, 0), axis=0, keepdims=True)
        dest_ref[0, :, k * t:(k + 1) * t] = base + rank_ref[k:k + 1, :]


def _plan(ridx, rank, tile_base, tm):
    n = ridx.shape[1]
    lanes = lambda i: (0, i)
    return pl.pallas_call(
        _plan_kernel,
        out_shape=jax.ShapeDtypeStruct((n // tm, 1, TOP_K * tm), I32),
        grid=(n // tm,),
        in_specs=[pl.BlockSpec((8, tm), lanes), pl.BlockSpec((8, tm), lanes),
                  pl.BlockSpec((1, N_EXPERTS, 1), lambda i: (i, 0, 0))],
        out_specs=pl.BlockSpec((1, 1, TOP_K * tm), lambda i: (i, 0, 0)),
        compiler_params=_cparams(1),
        name="moe_plan",
    )(ridx, rank, tile_base)


def _dispatch_kernel(dest_ref, h_ref, xs_out, sem, *, tm):
    def issue(i, carry):
        for k in range(TOP_K):
            d = dest_ref[0, 0, k * tm + i]
            pltpu.make_async_copy(h_ref.at[pl.ds(i, 1)], xs_out.at[pl.ds(d, 1)], sem).start()
        return carry

    lax.fori_loop(0, tm, issue, 0)

    def drain(i, carry):
        for k in range(TOP_K):
            pltpu.make_async_copy(h_ref.at[pl.ds(0, 1)], xs_out.at[pl.ds(0, 1)], sem).wait()
        return carry

    lax.fori_loop(0, tm, drain, 0)


def _dispatch(hp, dest_tiles, n_rows):
    n = hp.shape[0]
    tm = TM_DISPATCH
    return pl.pallas_call(
        functools.partial(_dispatch_kernel, tm=tm),
        out_shape=jax.ShapeDtypeStruct((n_rows, HALF_D), U32),
        grid=(n // tm,),
        in_specs=[pl.BlockSpec((1, 1, TOP_K * tm), lambda i: (i, 0, 0), memory_space=pltpu.SMEM),
                  pl.BlockSpec((tm, HALF_D), lambda i: (i, 0))],
        out_specs=pl.BlockSpec(memory_space=pl.ANY),
        scratch_shapes=[pltpu.SemaphoreType.DMA],
        compiler_params=_cparams(1),
        name="moe_dispatch",
    )(dest_tiles, hp)


def _expert_kernel(blk_e_ref, valid_ref, xs_ref, wgu_ref, wd_ref, ys_ref):
    del blk_e_ref
    valid = valid_ref[pl.program_id(0)]

    @pl.when(valid > 0)
    def _():
        rows = lax.broadcasted_iota(I32, (xs_ref.shape[0], 1), 0)
        lo, hi = _unpack_rows(jnp.where(rows < valid, xs_ref[...], jnp.uint32(0)))
        wgu = wgu_ref[0]
        gu = jnp.dot(lo.astype(BF16), wgu[:HALF_D], preferred_element_type=F32) \
            + jnp.dot(hi.astype(BF16), wgu[HALF_D:], preferred_element_type=F32)
        g = gu[:, :D_EXPERT]
        u = gu[:, D_EXPERT:]
        act = (g * jax.nn.sigmoid(g) * u).astype(BF16)
        ys_ref[...] = _pack_rows(jnp.dot(act, wd_ref[0], preferred_element_type=F32))

    @pl.when(valid <= 0)
    def _():
        ys_ref[...] = jnp.zeros(ys_ref.shape, U32)


def _experts(xs, blk_e, blk_valid, wgu, wd):
    n_rows = xs.shape[0]
    bm = MOE_BM
    nblk = n_rows // bm
    return pl.pallas_call(
        _expert_kernel,
        out_shape=jax.ShapeDtypeStruct((n_rows, HALF_D), U32),
        grid_spec=pltpu.PrefetchScalarGridSpec(
            num_scalar_prefetch=2,
            grid=(nblk,),
            in_specs=[pl.BlockSpec((bm, HALF_D), lambda i, be, nu: (i, 0)),
                      pl.BlockSpec((1, D_MODEL, 2 * D_EXPERT), lambda i, be, nu: (be[i], 0, 0)),
                      pl.BlockSpec((1, D_EXPERT, D_MODEL), lambda i, be, nu: (be[i], 0, 0))],
            out_specs=pl.BlockSpec((bm, HALF_D), lambda i, be, nu: (i, 0))),
        compiler_params=_cparams(1),
        name="moe_experts",
    )(blk_e, blk_valid, xs, wgu, wd)


def _combine_kernel(dest_ref, dest_next_ref, ys_ref, base_ref, rw_ref, mod_ref, fn_ref, y_ref,
                    buf, sem, *, tm, n_steps):
    step = pl.program_id(0)
    slot = lax.rem(step, 2)

    def gather(dref, s):
        def issue(i, carry):
            for k in range(TOP_K):
                d = dref[0, 0, k * tm + i]
                pltpu.make_async_copy(ys_ref.at[pl.ds(d, 1)], buf.at[s, k, pl.ds(i, 1)],
                                      sem.at[s]).start()
            return carry

        lax.fori_loop(0, tm, issue, 0)

    @pl.when(step == 0)
    def _():
        gather(dest_ref, 0)

    @pl.when(step + 1 < n_steps)
    def _():
        gather(dest_next_ref, 1 - slot)

    def drain(i, carry):
        for k in range(TOP_K):
            pltpu.make_async_copy(ys_ref.at[pl.ds(0, 1)], buf.at[slot, k, pl.ds(0, 1)],
                                  sem.at[slot]).wait()
        return carry

    lax.fori_loop(0, tm, drain, 0)

    rw = rw_ref[...]
    acc_lo = jnp.zeros((tm, HALF_D), F32)
    acc_hi = jnp.zeros((tm, HALF_D), F32)
    for k in range(TOP_K):
        lo, hi = _unpack_rows(buf[slot, k])
        w = rw[:, k:k + 1]
        acc_lo = acc_lo + w * lo
        acc_hi = acc_hi + w * hi
    routed = jnp.concatenate([acc_lo, acc_hi], axis=1)
    x2 = base_ref[...] + mod_ref[0][5:6] * routed
    y_ref[...] = _rms(x2, fn_ref[...])


def _combine(ys, dest_tiles, base, rw_t, mod3, final_norm, seq):
    n = base.shape[0]
    tm = TM_COMBINE
    assert seq % tm == 0
    n_steps = n // tm
    dest_spec = lambda index: pl.BlockSpec((1, 1, TOP_K * tm), index, memory_space=pltpu.SMEM)
    return pl.pallas_call(
        functools.partial(_combine_kernel, tm=tm, n_steps=n_steps),
        out_shape=jax.ShapeDtypeStruct((n, D_MODEL), F32),
        grid=(n_steps,),
        in_specs=[dest_spec(lambda i: (i, 0, 0)),
                  dest_spec(lambda i: (jnp.minimum(i + 1, n_steps - 1), 0, 0)),
                  pl.BlockSpec(memory_space=pl.ANY),
                  pl.BlockSpec((tm, D_MODEL), lambda i: (i, 0)),
                  pl.BlockSpec((tm, 8), lambda i: (i, 0)),
                  pl.BlockSpec((1, 6, D_MODEL), lambda i: (i * tm // seq, 0, 0)),
                  pl.BlockSpec((1, D_MODEL), lambda i: (0, 0))],
        out_specs=pl.BlockSpec((tm, D_MODEL), lambda i: (i, 0)),
        scratch_shapes=[pltpu.VMEM((2, TOP_K, tm, HALF_D), U32), pltpu.SemaphoreType.DMA((2,))],
        compiler_params=_cparams(1),
        name="moe_combine",
    )(dest_tiles, dest_tiles, ys, base, rw_t, mod3, final_norm.reshape(1, D_MODEL))


def _block_layout(tile_counts, n):
    bm = MOE_BM
    c = tile_counts[:, :, 0]
    ntiles = c.shape[0]
    counts = jnp.sum(c, axis=0)
    padded = (counts + bm - 1) // bm * bm
    earlier_e = np.tri(N_EXPERTS, k=-1, dtype=bool)
    pstart = jnp.sum(jnp.where(earlier_e, padded[None, :], 0), axis=1)
    pend = pstart + padded
    earlier_t = np.tri(ntiles, k=-1, dtype=bool)
    tile_base = pstart[None, :] + jnp.sum(jnp.where(earlier_t[:, :, None], c[None], 0), axis=1)
    nblk = -(-n * TOP_K // bm) + N_EXPERTS
    blk = jnp.arange(nblk, dtype=I32)
    blk_e = jnp.minimum(jnp.sum((pend[None, :] <= blk[:, None] * bm).astype(I32), axis=1),
                        N_EXPERTS - 1)
    onehot_e = blk_e[:, None] == jnp.arange(N_EXPERTS, dtype=I32)[None, :]
    end_e = jnp.sum(jnp.where(onehot_e, (pstart + counts)[None, :], 0), axis=1)
    blk_valid = jnp.clip(end_e - blk * bm, 0, bm).astype(I32)
    return tile_base.astype(I32)[:, :, None], blk_e.astype(I32), blk_valid, nblk * bm


def _trunk(x, mod3, p):
    nbatch, seq, _ = x.shape
    n = nbatch * seq
    x2 = x.reshape(n, D_MODEL)
    proj = _inproj(x2, mod3, p["norm1"], p["w_in"], seq)
    qa = proj[0]
    sga, sgb = proj[14:]
    (ya,) = _band_attention(qa, proj[1:5], p["bias_a"], p["sink_a"], nbatch=nbatch, seq=seq, dil=1,
                            halo=A_HALF_WINDOW, kv_width=A_KV_W, want_lse=False)
    outs, lses = [], []
    for gi, (w, d) in enumerate(B_GROUPS):
        q, k, v = proj[5 + 3 * gi:8 + 3 * gi]
        o, lse = _band_attention(q, (k, v), p["bias_b"][gi], None, nbatch=nbatch, seq=seq, dil=d,
                                 halo=w // (2 * d), kv_width=B_OUT_W, want_lse=True)
        outs.append(o)
        lses.append(lse)
    base, hp, ridx, rw, rank, tile_counts = _merge(
        x2, ya, outs, lses, sga, sgb, mod3, p["norm2"], p["w_pa"], p["w_pb"], p["w_o"], p["w_rt"],
        p["rbias"], p["ws_gate"], p["ws_up"], p["ws_down"], seq)
    assert TM_MERGE == TM_DISPATCH == TM_COMBINE
    tile_base, blk_e, blk_valid, n_rows = _block_layout(tile_counts, n)
    dest_tiles = _plan(ridx, rank, tile_base, TM_MERGE)
    xs = _dispatch(hp, dest_tiles, n_rows)
    ys = _experts(xs, blk_e, blk_valid, p["w_gu"], p["w_down"])
    y = _combine(ys, dest_tiles, base, rw.T, mod3, p["final_norm"], seq)
    return y.reshape(nbatch, seq, D_MODEL)


def kernel(x_prompt, x_sample, c_prompt, c_sample, rel_bias, w_ada, b_ada, norm1, w_in, sink, w_pa, w_pb, w_o, norm2, w_router, router_bias, w_gate, w_up, w_down, ws_gate, ws_up, ws_down, final_norm):
    assert w_ada.shape[0] == 1
    nbp = x_prompt.shape[0]
    mod = _ada(jnp.concatenate([c_prompt, c_sample], axis=0), w_ada[0], b_ada[0])
    mod3 = mod.reshape(-1, 6, D_MODEL)

    def pair_bias(heads, halo, dist_scale):
        kb = ATTN_BLOCK + 2 * halo
        tab = _bias_table(heads, ATTN_BLOCK, kb, halo, halo, dist_scale)
        return tab.reshape(-1, 2, ATTN_BLOCK, kb).transpose(0, 2, 1, 3).reshape(-1, ATTN_BLOCK, 2 * kb)

    bias_a = pair_bias(rel_bias[:, :A_Q_HEADS], A_HALF_WINDOW, 1)
    bias_b = []
    for gi, (w, d) in enumerate(B_GROUPS):
        h0 = A_Q_HEADS + gi * B_HEADS_PER_GROUP
        bias_b.append(pair_bias(rel_bias[:, h0:h0 + B_HEADS_PER_GROUP], w // (2 * d), d))
    p = {
        "norm1": norm1[0], "norm2": norm2[0], "final_norm": final_norm,
        "w_in": w_in[0].astype(BF16),
        "bias_a": bias_a,
        "sink_a": jnp.repeat(sink[0].astype(F32), HEAD_DIM).reshape(A_Q_HEADS // 2, 1, LANES),
        "bias_b": bias_b,
        "w_pa": w_pa[0].astype(BF16), "w_pb": w_pb[0].astype(BF16), "w_o": w_o[0].astype(BF16),
        "w_rt": w_router[0].T, "rbias": router_bias[0].reshape(N_EXPERTS, 1),
        "ws_gate": ws_gate[0].astype(BF16), "ws_up": ws_up[0].astype(BF16),
        "ws_down": ws_down[0].astype(BF16),
        "w_gu": jnp.concatenate([w_gate[0], w_up[0]], axis=-1).astype(BF16),
        "w_down": w_down[0].astype(BF16),
    }
    y_prompt = _trunk(x_prompt, mod3[:nbp], p)
    y_sample = _trunk(x_sample, mod3[nbp:], p)
    return (y_prompt, y_sample)
```

```python
import functools
import math

import jax
import jax.numpy as jnp
import numpy as np
from jax import lax
from jax.experimental import pallas as pl
from jax.experimental.pallas import tpu as pltpu
from jax.experimental.pallas import tpu_sc as plsc

F32 = jnp.float32
BF16 = jnp.bfloat16
U32 = jnp.uint32
I32 = jnp.int32

D_MODEL = 1024
HEAD_DIM = 64
A_Q_HEADS = 8
A_KV_HEADS = 2
A_HALF_WINDOW = 128
B_GROUPS = ((128, 1), (512, 4), (2048, 16))
B_HEADS_PER_GROUP = 4
N_BUCKETS = 32
MAX_DISTANCE = 1024
N_EXPERTS = 64
TOP_K = 6
N_EXPERT_GROUPS = 8
TOPK_GROUPS = 4
D_EXPERT = 256
D_SHARED = 256
ROUTED_SCALE = 2.5
RMS_EPS = 1e-6
NEG_INF = -1e30
REMOVED = -3e38

A_Q_W = A_Q_HEADS * HEAD_DIM
A_KV_W = A_KV_HEADS * HEAD_DIM
B_W = len(B_GROUPS) * B_HEADS_PER_GROUP * HEAD_DIM
B_OUT_W = B_HEADS_PER_GROUP * HEAD_DIM
IN_WIDTHS = (A_Q_W, A_KV_W, A_KV_W, B_W, B_W, B_W, D_MODEL, D_MODEL)
D_IN = sum(IN_WIDTHS)
HALF_D = D_MODEL // 2
LANES = 128
ROW_SLABS = HALF_D // LANES
SC_CORES = 2
SC_SUBCORES = 16
SC_CHUNK = 128

TM_INPROJ = 512
TM_MERGE = 256
ATTN_BLOCK = 128
ATTN_ROWS = 512
MOE_BM = 512
TM_COMBINE = 256
VMEM_LIMIT = 48 * 1024 * 1024


def _cparams(n_axes):
    return pltpu.CompilerParams(
        dimension_semantics=("arbitrary",) * n_axes, vmem_limit_bytes=VMEM_LIMIT)


def _ada_kernel(c_ref, w_ref, b_ref, o_ref):
    c = c_ref[...]
    s = c * jax.nn.sigmoid(c)
    o_ref[...] = jnp.dot(s, w_ref[...], preferred_element_type=F32,
                         precision=lax.Precision.HIGHEST) + b_ref[...]


def _ada(c_all, w_ada, b_ada):
    nb = c_all.shape[0]
    return pl.pallas_call(
        _ada_kernel,
        out_shape=jax.ShapeDtypeStruct((nb, 6 * D_MODEL), F32),
        grid=(6,),
        in_specs=[pl.BlockSpec((nb, D_MODEL), lambda j: (0, 0)),
                  pl.BlockSpec((D_MODEL, D_MODEL), lambda j: (0, j)),
                  pl.BlockSpec((1, D_MODEL), lambda j: (0, j))],
        out_specs=pl.BlockSpec((nb, D_MODEL), lambda j: (0, j)),
        compiler_params=_cparams(1),
        name="ada",
    )(c_all, w_ada, b_ada.reshape(1, 6 * D_MODEL))


def _rms(x, g):
    return x * lax.rsqrt(jnp.mean(x * x, axis=-1, keepdims=True) + RMS_EPS) * g


def _inproj_kernel(x_ref, mod_ref, n1_ref, w_ref, qa, ka, ka_sw, va, va_sw, *rest):
    qkv_b = rest[:9]
    sga, sgb, scr = rest[9:]
    mod = mod_ref[0]
    h = _rms(x_ref[...], n1_ref[...]) * (1.0 + mod[1:2]) + mod[0:1]
    hb = h.astype(BF16)
    tm = hb.shape[0]
    scale = HEAD_DIM ** -0.5

    def proj(off, width):
        return jnp.dot(hb, w_ref[:, off:off + width], preferred_element_type=F32)

    qa[...] = (proj(0, A_Q_W) * scale).astype(BF16)
    for ref, ref_sw, off in ((ka, ka_sw, A_Q_W), (va, va_sw, A_Q_W + A_KV_W)):
        r = proj(off, A_KV_W)
        ref[...] = r.astype(BF16)
        ref_sw[...] = pltpu.roll(r, HEAD_DIM, 1).astype(BF16)
    off = A_Q_W + 2 * A_KV_W
    for t in range(3):
        for gi, (_, d) in enumerate(B_GROUPS):
            r = proj(off + t * B_W + gi * B_OUT_W, B_OUT_W)
            if t == 0:
                r = r * scale
            ref = qkv_b[gi * 3 + t]
            if d == 1:
                ref[...] = r.astype(BF16)
            else:
                for c in range(B_OUT_W // LANES):
                    slot = (t * 2 + gi - 1) * (B_OUT_W // LANES) + c
                    scr[slot] = r[:, c * LANES:(c + 1) * LANES]
                    for res in range(d):
                        col = res * B_OUT_W + c * LANES
                        ref[:, col:col + LANES] = scr[
                            slot, pl.ds(res, tm // d, stride=d), :].astype(BF16)
    off += 3 * B_W
    sga[...] = jax.nn.sigmoid(proj(off, D_MODEL)).astype(BF16)
    sgb[...] = jax.nn.sigmoid(proj(off + D_MODEL, D_MODEL)).astype(BF16)


def _inproj(x2, mod3, norm1, w_in_bf, seq):
    n = x2.shape[0]
    tm = TM_INPROJ
    assert seq % tm == 0 and n % tm == 0
    row = lambda i: (i, 0)
    shapes = [(n, A_Q_W, tm)] + [(n, A_KV_W, tm)] * 4
    for _, d in B_GROUPS:
        shapes += [(n // d, d * B_OUT_W, tm // d)] * 3
    shapes += [(n, D_MODEL, tm)] * 2
    return pl.pallas_call(
        _inproj_kernel,
        out_shape=[jax.ShapeDtypeStruct((r, c), BF16) for r, c, _ in shapes],
        grid=(n // tm,),
        in_specs=[pl.BlockSpec((tm, D_MODEL), row),
                  pl.BlockSpec((1, 6, D_MODEL), lambda i: (i * tm // seq, 0, 0)),
                  pl.BlockSpec((1, D_MODEL), lambda i: (0, 0)),
                  pl.BlockSpec((D_MODEL, D_IN), lambda i: (0, 0))],
        out_specs=[pl.BlockSpec((b, c), row) for _, c, b in shapes],
        scratch_shapes=[pltpu.VMEM((6 * B_OUT_W // LANES, tm, LANES), F32)],
        compiler_params=_cparams(1),
        name="inproj",
    )(x2, mod3, norm1.reshape(1, D_MODEL), w_in_bf)


def _rel_bucket_np(rel):
    half = N_BUCKETS // 2
    max_exact = half // 2
    n = np.abs(rel)
    large = max_exact + (np.log(np.maximum(n, 1) / max_exact) / math.log(MAX_DISTANCE / max_exact)
                         * (half - max_exact)).astype(np.int32)
    large = np.minimum(large, half - 1)
    return ((rel > 0).astype(np.int32) * half + np.where(n < max_exact, n, large)).astype(np.int32)


def _bias_table(rel_bias_heads, n_q, n_k, key_off, band, dist_scale):
    p = n_q + n_k
    rel = np.arange(p) - (n_q - 1) - key_off
    bucket = _rel_bucket_np(rel * dist_scale)
    t = jnp.where((np.abs(rel) <= band)[None], rel_bias_heads.astype(F32)[bucket].T, NEG_INF)
    big = jnp.tile(t, (1, n_q + 1))
    tab = big[:, n_q - 1:n_q - 1 + n_q * (p - 1)].reshape(-1, n_q, p - 1)
    return tab[:, :, :n_k]


def _pair_rhs(k_top, k_bot, v_top, v_bot):
    kb = k_top.shape[0]
    low = jnp.where(lax.broadcasted_iota(I32, (kb, LANES), 1) < HEAD_DIM, 1.0, 0.0).astype(BF16)
    high = jnp.where(lax.broadcasted_iota(I32, (kb, LANES), 1) < HEAD_DIM, 0.0, 1.0).astype(BF16)
    rhs_k = jnp.concatenate([k_top * low, k_bot * high], axis=0)
    rhs_v = jnp.concatenate([jnp.concatenate([v_top * low, low], axis=1),
                             jnp.concatenate([v_bot * high, high], axis=1)], axis=0)
    return rhs_k, rhs_v


def _pair_attention(q_pair, rhs_k, rhs_v, bias_pair, pen, sink_pair, want_lse):
    kb = rhs_k.shape[0] // 2
    s = lax.dot_general(q_pair, rhs_k, (((1,), (1,)), ((), ())), preferred_element_type=F32)
    s = s + bias_pair
    if pen is not None:
        s = s + pen
    s0, s1 = s[:, :kb], s[:, kb:]
    m0 = jnp.max(s0, axis=-1, keepdims=True)
    m1 = jnp.max(s1, axis=-1, keepdims=True)
    if sink_pair is not None:
        m0 = jnp.maximum(m0, sink_pair[:, 0:1])
        m1 = jnp.maximum(m1, sink_pair[:, HEAD_DIM:HEAD_DIM + 1])
    p = jnp.concatenate([jnp.exp(s0 - m0), jnp.exp(s1 - m1)], axis=1).astype(BF16)
    od = jnp.dot(p, rhs_v, preferred_element_type=F32)
    o, den = od[:, :LANES], od[:, LANES:]
    low = lax.broadcasted_iota(I32, o.shape, 1) < HEAD_DIM
    m_full = jnp.where(low, m0, m1)
    if sink_pair is not None:
        den = den + jnp.exp(sink_pair - m_full)
    return o / den, (m_full + jnp.log(den)) if want_lse else None


def _attn_kernel(*refs, halo, nblk, n_pairs, shared_kv, has_sink, want_lse, n_steps):
    it = iter(refs)
    q_ref = next(it)
    n_kv = 4 if shared_kv else 2
    kv_refs = [[next(it) for _ in range(3)] for _ in range(n_kv)]
    bias_ref = next(it)
    sink_ref = next(it) if has_sink else None
    o_ref = next(it)
    lse_ref = next(it) if want_lse else None
    step = pl.program_id(2)
    blk = ATTN_BLOCK
    kb = blk + 2 * halo
    cats = [jnp.concatenate([r[...] for r in trio], axis=0) for trio in kv_refs]
    col = lax.broadcasted_iota(I32, (1, kb), 1)
    for b in range(nblk):
        lo = jnp.where(step == 0, halo, 0) if b == 0 else 0
        hi = jnp.where(step == n_steps - 1, kb - halo, kb) if b == nblk - 1 else kb
        pen = None
        if b == 0 or b == nblk - 1:
            pen1 = jnp.where((col >= lo) & (col < hi), 0.0, NEG_INF)
            pen = jnp.concatenate([pen1, pen1], axis=1)
        rows = slice(b * blk, (b + 1) * blk)
        krows = slice(b * blk, b * blk + kb)
        if shared_kv:
            k, k_sw, v, v_sw = (c[krows] for c in cats)
            rhs = [_pair_rhs(k, k_sw, v, v_sw), _pair_rhs(k_sw, k, v_sw, v)]
        for c in range(n_pairs):
            lanes = slice(c * LANES, (c + 1) * LANES)
            if shared_kv:
                rhs_k, rhs_v = rhs[c // (n_pairs // 2)]
            else:
                k, v = cats[0][krows, lanes], cats[1][krows, lanes]
                rhs_k, rhs_v = _pair_rhs(k, k, v, v)
            o, lse = _pair_attention(q_ref[rows, lanes], rhs_k, rhs_v, bias_ref[c], pen,
                                     sink_ref[c] if has_sink else None, want_lse)
            o_ref[rows, lanes] = o.astype(o_ref.dtype)
            if want_lse:
                lse_ref[rows, lanes] = lse


def _band_attention(q, kvs, bias, sink, *, nbatch, seq, dil, halo, kv_width, want_lse):
    n = nbatch * seq
    sub_len = seq // dil
    qw = q.shape[1] // dil
    rows = min(ATTN_ROWS, sub_len)
    assert sub_len % rows == 0 and rows % ATTN_BLOCK == 0 and rows % halo == 0
    nq = sub_len // rows
    per = rows // halo
    total_halos = n // dil // halo
    cur = lambda b, r, i: (b * nq + i, r)
    prev = lambda b, r, i: (jnp.maximum((b * nq + i) * per - 1, 0), r)
    nxt = lambda b, r, i: (jnp.minimum((b * nq + i + 1) * per, total_halos - 1), r)
    const = lambda a: pl.BlockSpec(a.shape, lambda b, r, i: (0,) * a.ndim)

    in_specs = [pl.BlockSpec((rows, qw), cur)]
    args = [q]
    for a in kvs:
        in_specs += [pl.BlockSpec((halo, kv_width), prev), pl.BlockSpec((rows, kv_width), cur),
                     pl.BlockSpec((halo, kv_width), nxt)]
        args += [a, a, a]
    in_specs.append(const(bias))
    args.append(bias)
    if sink is not None:
        in_specs.append(const(sink))
        args.append(sink)
    out_shape = [jax.ShapeDtypeStruct(q.shape, BF16)]
    out_specs = [pl.BlockSpec((rows, qw), cur)]
    if want_lse:
        out_shape.append(jax.ShapeDtypeStruct(q.shape, F32))
        out_specs.append(pl.BlockSpec((rows, qw), cur))
    return pl.pallas_call(
        functools.partial(_attn_kernel, halo=halo, nblk=rows // ATTN_BLOCK, n_pairs=qw // LANES,
                          shared_kv=len(kvs) == 4, has_sink=sink is not None, want_lse=want_lse,
                          n_steps=nq),
        out_shape=out_shape,
        grid=(nbatch, dil, nq),
        in_specs=in_specs,
        out_specs=out_specs,
        compiler_params=_cparams(3),
        name=f"band_attn_d{dil}",
    )(*args)


def _pack_rows(y):
    lo = pltpu.bitcast(y[:, :HALF_D].astype(BF16).astype(F32), U32)
    hi = pltpu.bitcast(y[:, HALF_D:].astype(BF16).astype(F32), U32)
    return (hi & jnp.uint32(0xFFFF0000)) | (lo >> 16)


def _store_slabs(ref, packed):
    for c in range(ROW_SLABS):
        ref[:, c, :] = packed[:, c * LANES:(c + 1) * LANES]


def _load_slabs(ref):
    return jnp.concatenate([ref[:, c, :] for c in range(ROW_SLABS)], axis=1)


def _unpack_rows(p):
    lo = pltpu.bitcast(p << 16, F32)
    hi = pltpu.bitcast(p & jnp.uint32(0xFFFF0000), F32)
    return lo, hi


def _route(sel, scores):
    t = sel.shape[-1]
    per = N_EXPERTS // N_EXPERT_GROUPS
    shape3 = (N_EXPERT_GROUPS, per, t)
    sel3 = sel.reshape(shape3)
    sc3 = scores.reshape(shape3)
    iota_g = lax.broadcasted_iota(I32, shape3, 0)
    iota_m = lax.broadcasted_iota(I32, shape3, 1)
    iota_e = iota_g * per + iota_m
    m1 = jnp.max(sel3, axis=1, keepdims=True)
    i1 = jnp.min(jnp.where(sel3 == m1, iota_m, per), axis=1, keepdims=True)
    m2 = jnp.max(jnp.where(iota_m == i1, REMOVED, sel3), axis=1, keepdims=True)
    gscore = m1 + m2
    iota_g1 = lax.broadcasted_iota(I32, gscore.shape, 0)
    gmask = jnp.zeros(gscore.shape, jnp.bool_)
    for _ in range(TOPK_GROUPS):
        mx = jnp.max(gscore, axis=0, keepdims=True)
        ix = jnp.min(jnp.where(gscore == mx, iota_g1, N_EXPERT_GROUPS), axis=0, keepdims=True)
        hit = iota_g1 == ix
        gmask = gmask | hit
        gscore = jnp.where(hit, REMOVED, gscore)
    cur = jnp.where(gmask, sel3, NEG_INF)
    sum_all = lambda a: jnp.sum(jnp.sum(a, axis=1, keepdims=True), axis=0, keepdims=True)
    idxs, wts, hits = [], [], []
    for _ in range(TOP_K):
        mx = jnp.max(jnp.max(cur, axis=1, keepdims=True), axis=0, keepdims=True)
        cand = jnp.where(cur == mx, iota_e, N_EXPERTS)
        ix = jnp.min(jnp.min(cand, axis=1, keepdims=True), axis=0, keepdims=True)
        hit = iota_e == ix
        wts.append(sum_all(jnp.where(hit, sc3, 0.0)).reshape(1, t))
        cur = jnp.where(hit, REMOVED, cur)
        idxs.append(ix.reshape(1, t))
        hits.append(hit)
    wsum = wts[0]
    for w in wts[1:]:
        wsum = wsum + w
    wts = [w / wsum * ROUTED_SCALE for w in wts]
    onehot = hits[0].astype(F32)
    for hit in hits[1:]:
        onehot = onehot + hit.astype(F32)
    tri = (lax.broadcasted_iota(I32, (t, t), 0) <= lax.broadcasted_iota(I32, (t, t), 1))
    cum = jnp.dot(onehot.reshape(N_EXPERTS, t).astype(BF16), tri.astype(F32).astype(BF16),
                  preferred_element_type=F32)
    cum3 = cum.reshape(shape3) - 1.0
    ranks = [sum_all(jnp.where(hit, cum3, 0.0)).reshape(1, t).astype(I32) for hit in hits]
    counts = cum[:, t - 1:t].astype(I32)
    return idxs, wts, ranks, counts


def _merge_kernel(x_ref, ya_ref, o1, o2, o3, l1, l2, l3, sga_ref, sgb_ref, mod_ref, n2_ref,
                  wpa, wpb, wo, wrt, rbias, wsg, wsu, wsd,
                  base_ref, hp_ref, ridx_ref, rw_ref, rank_ref, cnt_ref, scr):
    mod = mod_ref[0]
    tm = x_ref.shape[0]

    def token_major(ref, slot, d):
        if d == 1:
            return ref[...].astype(F32)
        nc = B_OUT_W // LANES
        for c in range(nc):
            for res in range(d):
                col = res * B_OUT_W + c * LANES
                scr[slot * nc + c, pl.ds(res, tm // d, stride=d), :] = ref[
                    :, col:col + LANES].astype(F32)
        return jnp.concatenate([scr[slot * nc + c] for c in range(nc)], axis=1)

    dils = [d for _, d in B_GROUPS]
    os_ = [token_major(r, i, d) for i, (r, d) in enumerate(zip((o1, o2, o3), dils))]
    ls = [token_major(r, 3 + i, d) for i, (r, d) in enumerate(zip((l1, l2, l3), dils))]
    mx = jnp.maximum(jnp.maximum(ls[0], ls[1]), ls[2])
    es = [jnp.exp(l - mx) for l in ls]
    den = es[0] + es[1] + es[2]
    ob = (es[0] / den) * os_[0] + (es[1] / den) * os_[1] + (es[2] / den) * os_[2]
    pa = jnp.dot(ya_ref[...], wpa[...], preferred_element_type=F32)
    pb = jnp.dot(ob.astype(BF16), wpb[...], preferred_element_type=F32)
    merged = sga_ref[...].astype(F32) * pa + sgb_ref[...].astype(F32) * pb
    mix = jnp.dot(merged.astype(BF16), wo[...], preferred_element_type=F32)
    x1 = x_ref[...] + mod[2:3] * mix
    h2 = _rms(x1, n2_ref[...]) * (1.0 + mod[4:5]) + mod[3:4]
    logits = lax.dot_general(wrt[...], h2, (((1,), (1,)), ((), ())), preferred_element_type=F32,
                             precision=lax.Precision.HIGHEST)
    scores = jax.nn.sigmoid(logits)
    idxs, wts, ranks, counts = _route(scores + rbias[...], scores)
    for k in range(TOP_K):
        ridx_ref[k:k + 1, :] = idxs[k]
        rw_ref[k:k + 1, :] = wts[k]
        rank_ref[k:k + 1, :] = ranks[k]
    ridx_ref[TOP_K:, :] = jnp.zeros((8 - TOP_K, tm), I32)
    rw_ref[TOP_K:, :] = jnp.zeros((8 - TOP_K, tm), F32)
    rank_ref[TOP_K:, :] = jnp.zeros((8 - TOP_K, tm), I32)
    cnt_ref[0] = counts
    hb = h2.astype(BF16)
    g = jnp.dot(hb, wsg[...], preferred_element_type=F32)
    u = jnp.dot(hb, wsu[...], preferred_element_type=F32)
    act = (g * jax.nn.sigmoid(g) * u).astype(BF16)
    shared = jnp.dot(act, wsd[...], preferred_element_type=F32)
    base_ref[...] = x1 + mod[5:6] * shared
    _store_slabs(hp_ref, _pack_rows(h2))


def _merge(x2, ya, outs, lses, sga, sgb, mod3, norm2, wpa, wpb, wo, wrt, rbias, wsg, wsu, wsd, seq):
    n = x2.shape[0]
    tm = TM_MERGE
    assert seq % tm == 0
    row = lambda i: (i, 0)
    full = lambda a: pl.BlockSpec(a.shape, lambda i: (0,) * a.ndim)
    weights = [wpa, wpb, wo, wrt, rbias, wsg, wsu, wsd]
    group_specs = [pl.BlockSpec((tm // d, d * B_OUT_W), row) for _, d in B_GROUPS]
    lanes = lambda i: (0, i)
    return pl.pallas_call(
        _merge_kernel,
        out_shape=[jax.ShapeDtypeStruct((n, D_MODEL), F32),
                   jax.ShapeDtypeStruct((n, ROW_SLABS, LANES), U32),
                   jax.ShapeDtypeStruct((8, n), I32),
                   jax.ShapeDtypeStruct((8, n), F32),
                   jax.ShapeDtypeStruct((8, n), I32),
                   jax.ShapeDtypeStruct((n // tm, N_EXPERTS, 1), I32)],
        grid=(n // tm,),
        in_specs=[pl.BlockSpec((tm, D_MODEL), row), pl.BlockSpec((tm, A_Q_W), row)]
                 + group_specs * 2
                 + [pl.BlockSpec((tm, D_MODEL), row)] * 2
                 + [pl.BlockSpec((1, 6, D_MODEL), lambda i: (i * tm // seq, 0, 0)),
                    pl.BlockSpec((1, D_MODEL), lambda i: (0, 0))]
                 + [full(w) for w in weights],
        out_specs=[pl.BlockSpec((tm, D_MODEL), row),
                   pl.BlockSpec((tm, ROW_SLABS, LANES), lambda i: (i, 0, 0)),
                   pl.BlockSpec((8, tm), lanes), pl.BlockSpec((8, tm), lanes),
                   pl.BlockSpec((8, tm), lanes),
                   pl.BlockSpec((1, N_EXPERTS, 1), lambda i: (i, 0, 0))],
        scratch_shapes=[pltpu.VMEM((6 * B_OUT_W // LANES, tm, LANES), F32)],
        compiler_params=_cparams(1),
        name="merge_route",
    )(x2, ya, *outs, *lses, sga, sgb, mod3, norm2.reshape(1, D_MODEL), *weights)


def _plan_kernel(ridx_ref, rank_ref, tb_ref, dest_ref):
    tb = tb_ref[0]
    t = ridx_ref.shape[1]
    iota_e = lax.broadcasted_iota(I32, (N_EXPERTS, t), 0)
    for k in range(TOP_K):
        base = jnp.sum(jnp.where(iota_e == ridx_ref[k:k + 1, :], tb, 0), axis=0, keepdims=True)
        dest_ref[0, :, k * t:(k + 1) * t] = base + rank_ref[k:k + 1, :]


def _plan(ridx, rank, tile_base, tm):
    n = ridx.shape[1]
    lanes = lambda i: (0, i)
    return pl.pallas_call(
        _plan_kernel,
        out_shape=jax.ShapeDtypeStruct((n // tm, 1, TOP_K * tm), I32),
        grid=(n // tm,),
        in_specs=[pl.BlockSpec((8, tm), lanes), pl.BlockSpec((8, tm), lanes),
                  pl.BlockSpec((1, N_EXPERTS, 1), lambda i: (i, 0, 0))],
        out_specs=pl.BlockSpec((1, 1, TOP_K * tm), lambda i: (i, 0, 0)),
        compiler_params=_cparams(1),
        name="moe_plan",
    )(ridx, rank, tile_base)


def _sc_mesh():
    return plsc.VectorSubcoreMesh(core_axis_name="core", subcore_axis_name="subcore",
                                  num_cores=SC_CORES, num_subcores=SC_SUBCORES)


def _sc_worker():
    return lax.axis_index("subcore") * SC_CORES + lax.axis_index("core")


def _sc_dispatch(hp, dest_rows, n_rows):
    n = hp.shape[0]
    halves = TM_MERGE // SC_CHUNK
    workers = SC_CORES * SC_SUBCORES
    tiles_per_w = n // TM_MERGE // workers
    idx_per_w = tiles_per_w * TOP_K * halves
    assert tiles_per_w * workers * TM_MERGE == n

    @functools.partial(
        pl.kernel, mesh=_sc_mesh(),
        out_type=jax.ShapeDtypeStruct((n_rows, ROW_SLABS, LANES), U32),
        scratch_types=[pltpu.VMEM((idx_per_w, SC_CHUNK), I32),
                       pltpu.VMEM((SC_CHUNK, ROW_SLABS, LANES), U32),
                       pltpu.SemaphoreType.DMA],
        name="moe_dispatch_sc")
    def scatter(hp_hbm, idx_hbm, xs_hbm, idx_v, rows_v, sem):
        wid = _sc_worker()
        pltpu.sync_copy(idx_hbm.at[pl.ds(wid * idx_per_w, idx_per_w)], idx_v)

        @pl.loop(0, tiles_per_w * halves)
        def _(j):
            tile = j // halves
            h = j - tile * halves
            tok = (wid * tiles_per_w + tile) * TM_MERGE + h * SC_CHUNK
            pltpu.sync_copy(hp_hbm.at[pl.ds(tok, SC_CHUNK)], rows_v)
            for k in range(TOP_K):
                row = (tile * TOP_K + k) * halves + h
                pltpu.async_copy(rows_v, xs_hbm.at[idx_v.at[row]], sem).wait()

    return scatter(hp, dest_rows)


def _sc_collect(ys, dest_rows):
    n_chunks = dest_rows.shape[0]
    workers = SC_CORES * SC_SUBCORES
    per_w = n_chunks // workers
    assert per_w * workers == n_chunks

    @functools.partial(
        pl.kernel, mesh=_sc_mesh(),
        out_type=jax.ShapeDtypeStruct((n_chunks * SC_CHUNK, ROW_SLABS, LANES), U32),
        scratch_types=[pltpu.VMEM((per_w, SC_CHUNK), I32),
                       pltpu.VMEM((SC_CHUNK, ROW_SLABS, LANES), U32),
                       pltpu.SemaphoreType.DMA],
        name="moe_collect_sc")
    def gather(ys_hbm, idx_hbm, out_hbm, idx_v, rows_v, sem):
        wid = _sc_worker()
        base = wid * per_w
        pltpu.sync_copy(idx_hbm.at[pl.ds(base, per_w)], idx_v)

        @pl.loop(0, per_w)
        def _(j):
            pltpu.async_copy(ys_hbm.at[idx_v.at[j]], rows_v, sem).wait()
            pltpu.sync_copy(rows_v, out_hbm.at[pl.ds((base + j) * SC_CHUNK, SC_CHUNK)])

    return gather(ys, dest_rows)


def _expert_kernel(blk_e_ref, valid_ref, xs_ref, wgu_ref, wd_ref, ys_ref):
    del blk_e_ref
    valid = valid_ref[pl.program_id(0)]

    @pl.when(valid > 0)
    def _():
        rows = lax.broadcasted_iota(I32, (xs_ref.shape[0], 1), 0)
        lo, hi = _unpack_rows(jnp.where(rows < valid, _load_slabs(xs_ref), jnp.uint32(0)))
        wgu = wgu_ref[0]
        gu = jnp.dot(lo.astype(BF16), wgu[:HALF_D], preferred_element_type=F32) \
            + jnp.dot(hi.astype(BF16), wgu[HALF_D:], preferred_element_type=F32)
        g = gu[:, :D_EXPERT]
        u = gu[:, D_EXPERT:]
        act = (g * jax.nn.sigmoid(g) * u).astype(BF16)
        _store_slabs(ys_ref, _pack_rows(jnp.dot(act, wd_ref[0], preferred_element_type=F32)))

    @pl.when(valid <= 0)
    def _():
        ys_ref[...] = jnp.zeros(ys_ref.shape, U32)


def _experts(xs, blk_e, blk_valid, wgu, wd):
    n_rows = xs.shape[0]
    bm = MOE_BM
    nblk = n_rows // bm
    return pl.pallas_call(
        _expert_kernel,
        out_shape=jax.ShapeDtypeStruct((n_rows, ROW_SLABS, LANES), U32),
        grid_spec=pltpu.PrefetchScalarGridSpec(
            num_scalar_prefetch=2,
            grid=(nblk,),
            in_specs=[pl.BlockSpec((bm, ROW_SLABS, LANES), lambda i, be, nu: (i, 0, 0)),
                      pl.BlockSpec((1, D_MODEL, 2 * D_EXPERT), lambda i, be, nu: (be[i], 0, 0)),
                      pl.BlockSpec((1, D_EXPERT, D_MODEL), lambda i, be, nu: (be[i], 0, 0))],
            out_specs=pl.BlockSpec((bm, ROW_SLABS, LANES), lambda i, be, nu: (i, 0, 0))),
        compiler_params=_cparams(1),
        name="moe_experts",
    )(blk_e, blk_valid, xs, wgu, wd)


def _combine_kernel(g_ref, base_ref, rw_ref, mod_ref, fn_ref, y_ref):
    tm = base_ref.shape[0]
    rw = rw_ref[...]
    acc_lo = jnp.zeros((tm, HALF_D), F32)
    acc_hi = jnp.zeros((tm, HALF_D), F32)
    for k in range(TOP_K):
        lo, hi = _unpack_rows(_load_slabs(g_ref.at[0, k]))
        w = rw[:, k:k + 1]
        acc_lo = acc_lo + w * lo
        acc_hi = acc_hi + w * hi
    routed = jnp.concatenate([acc_lo, acc_hi], axis=1)
    x2 = base_ref[...] + mod_ref[0][5:6] * routed
    y_ref[...] = _rms(x2, fn_ref[...])


def _combine(g, base, rw_t, mod3, final_norm, seq):
    n = base.shape[0]
    tm = TM_COMBINE
    assert seq % tm == 0
    g5 = g.reshape(n // tm, TOP_K, tm, ROW_SLABS, LANES)
    return pl.pallas_call(
        _combine_kernel,
        out_shape=jax.ShapeDtypeStruct((n, D_MODEL), F32),
        grid=(n // tm,),
        in_specs=[pl.BlockSpec((1, TOP_K, tm, ROW_SLABS, LANES), lambda i: (i, 0, 0, 0, 0)),
                  pl.BlockSpec((tm, D_MODEL), lambda i: (i, 0)),
                  pl.BlockSpec((tm, 8), lambda i: (i, 0)),
                  pl.BlockSpec((1, 6, D_MODEL), lambda i: (i * tm // seq, 0, 0)),
                  pl.BlockSpec((1, D_MODEL), lambda i: (0, 0))],
        out_specs=pl.BlockSpec((tm, D_MODEL), lambda i: (i, 0)),
        compiler_params=_cparams(1),
        name="moe_combine",
    )(g5, base, rw_t, mod3, final_norm.reshape(1, D_MODEL))


def _block_layout(tile_counts, n):
    bm = MOE_BM
    c = tile_counts[:, :, 0]
    ntiles = c.shape[0]
    counts = jnp.sum(c, axis=0)
    padded = (counts + bm - 1) // bm * bm
    earlier_e = np.tri(N_EXPERTS, k=-1, dtype=bool)
    pstart = jnp.sum(jnp.where(earlier_e, padded[None, :], 0), axis=1)
    pend = pstart + padded
    earlier_t = np.tri(ntiles, k=-1, dtype=bool)
    tile_base = pstart[None, :] + jnp.sum(jnp.where(earlier_t[:, :, None], c[None], 0), axis=1)
    nblk = -(-n * TOP_K // bm) + N_EXPERTS
    blk = jnp.arange(nblk, dtype=I32)
    blk_e = jnp.minimum(jnp.sum((pend[None, :] <= blk[:, None] * bm).astype(I32), axis=1),
                        N_EXPERTS - 1)
    onehot_e = blk_e[:, None] == jnp.arange(N_EXPERTS, dtype=I32)[None, :]
    end_e = jnp.sum(jnp.where(onehot_e, (pstart + counts)[None, :], 0), axis=1)
    blk_valid = jnp.clip(end_e - blk * bm, 0, bm).astype(I32)
    return tile_base.astype(I32)[:, :, None], blk_e.astype(I32), blk_valid, nblk * bm


def _trunk(x, mod3, p):
    nbatch, seq, _ = x.shape
    n = nbatch * seq
    x2 = x.reshape(n, D_MODEL)
    proj = _inproj(x2, mod3, p["norm1"], p["w_in"], seq)
    qa = proj[0]
    sga, sgb = proj[14:]
    (ya,) = _band_attention(qa, proj[1:5], p["bias_a"], p["sink_a"], nbatch=nbatch, seq=seq, dil=1,
                            halo=A_HALF_WINDOW, kv_width=A_KV_W, want_lse=False)
    outs, lses = [], []
    for gi, (w, d) in enumerate(B_GROUPS):
        q, k, v = proj[5 + 3 * gi:8 + 3 * gi]
        o, lse = _band_attention(q, (k, v), p["bias_b"][gi], None, nbatch=nbatch, seq=seq, dil=d,
                                 halo=w // (2 * d), kv_width=B_OUT_W, want_lse=True)
        outs.append(o)
        lses.append(lse)
    base, hp, ridx, rw, rank, tile_counts = _merge(
        x2, ya, outs, lses, sga, sgb, mod3, p["norm2"], p["w_pa"], p["w_pb"], p["w_o"], p["w_rt"],
        p["rbias"], p["ws_gate"], p["ws_up"], p["ws_down"], seq)
    assert TM_MERGE == TM_COMBINE
    tile_base, blk_e, blk_valid, n_rows = _block_layout(tile_counts, n)
    dest_rows = _plan(ridx, rank, tile_base, TM_MERGE).reshape(-1, SC_CHUNK)
    xs = _sc_dispatch(hp, dest_rows, n_rows)
    ys = _experts(xs, blk_e, blk_valid, p["w_gu"], p["w_down"])
    y = _combine(_sc_collect(ys, dest_rows), base, rw.T, mod3, p["final_norm"], seq)
    return y.reshape(nbatch, seq, D_MODEL)


def kernel(x_prompt, x_sample, c_prompt, c_sample, rel_bias, w_ada, b_ada, norm1, w_in, sink, w_pa, w_pb, w_o, norm2, w_router, router_bias, w_gate, w_up, w_down, ws_gate, ws_up, ws_down, final_norm):
    assert w_ada.shape[0] == 1
    nbp = x_prompt.shape[0]
    mod = _ada(jnp.concatenate([c_prompt, c_sample], axis=0), w_ada[0], b_ada[0])
    mod3 = mod.reshape(-1, 6, D_MODEL)

    def pair_bias(heads, halo, dist_scale):
        kb = ATTN_BLOCK + 2 * halo
        tab = _bias_table(heads, ATTN_BLOCK, kb, halo, halo, dist_scale)
        return tab.reshape(-1, 2, ATTN_BLOCK, kb).transpose(0, 2, 1, 3).reshape(-1, ATTN_BLOCK, 2 * kb)

    bias_a = pair_bias(rel_bias[:, :A_Q_HEADS], A_HALF_WINDOW, 1)
    bias_b = []
    for gi, (w, d) in enumerate(B_GROUPS):
        h0 = A_Q_HEADS + gi * B_HEADS_PER_GROUP
        bias_b.append(pair_bias(rel_bias[:, h0:h0 + B_HEADS_PER_GROUP], w // (2 * d), d))
    p = {
        "norm1": norm1[0], "norm2": norm2[0], "final_norm": final_norm,
        "w_in": w_in[0].astype(BF16),
        "bias_a": bias_a,
        "sink_a": jnp.repeat(sink[0].astype(F32), HEAD_DIM).reshape(A_Q_HEADS // 2, 1, LANES),
        "bias_b": bias_b,
        "w_pa": w_pa[0].astype(BF16), "w_pb": w_pb[0].astype(BF16), "w_o": w_o[0].astype(BF16),
        "w_rt": w_router[0].T, "rbias": router_bias[0].reshape(N_EXPERTS, 1),
        "ws_gate": ws_gate[0].astype(BF16), "ws_up": ws_up[0].astype(BF16),
        "ws_down": ws_down[0].astype(BF16),
        "w_gu": jnp.concatenate([w_gate[0], w_up[0]], axis=-1).astype(BF16),
        "w_down": w_down[0].astype(BF16),
    }
    y_prompt = _trunk(x_prompt, mod3[:nbp], p)
    y_sample = _trunk(x_sample, mod3[nbp:], p)
    return (y_prompt, y_sample)
```

```python
import functools
import math

import jax
import jax.numpy as jnp
import numpy as np
from jax import lax
from jax.experimental import pallas as pl
from jax.experimental.pallas import tpu as pltpu
from jax.experimental.pallas import tpu_sc as plsc

F32 = jnp.float32
BF16 = jnp.bfloat16
U32 = jnp.uint32
I32 = jnp.int32

D_MODEL = 1024
HEAD_DIM = 64
A_Q_HEADS = 8
A_KV_HEADS = 2
A_HALF_WINDOW = 128
B_GROUPS = ((128, 1), (512, 4), (2048, 16))
B_HEADS_PER_GROUP = 4
N_BUCKETS = 32
MAX_DISTANCE = 1024
N_EXPERTS = 64
TOP_K = 6
N_EXPERT_GROUPS = 8
TOPK_GROUPS = 4
D_EXPERT = 256
D_SHARED = 256
ROUTED_SCALE = 2.5
RMS_EPS = 1e-6
NEG_INF = -1e30
REMOVED = -3e38

A_Q_W = A_Q_HEADS * HEAD_DIM
A_KV_W = A_KV_HEADS * HEAD_DIM
B_W = len(B_GROUPS) * B_HEADS_PER_GROUP * HEAD_DIM
B_OUT_W = B_HEADS_PER_GROUP * HEAD_DIM
IN_WIDTHS = (A_Q_W, A_KV_W, A_KV_W, B_W, B_W, B_W, D_MODEL, D_MODEL)
D_IN = sum(IN_WIDTHS)
HALF_D = D_MODEL // 2
LANES = 128
ROW_SLABS = HALF_D // LANES
SC_CORES = 2
SC_SUBCORES = 16
SC_CHUNK = 128

TM_INPROJ = 512
TM_MERGE = 256
ATTN_BLOCK = 128
ATTN_ROWS = 512
MOE_BM = 512
TM_COMBINE = 256
VMEM_LIMIT = 48 * 1024 * 1024


def _cparams(n_axes):
    return pltpu.CompilerParams(
        dimension_semantics=("arbitrary",) * n_axes, vmem_limit_bytes=VMEM_LIMIT)


def _ada_kernel(c_ref, w_ref, b_ref, o_ref):
    c = c_ref[...]
    s = c * jax.nn.sigmoid(c)
    o_ref[...] = jnp.dot(s, w_ref[...], preferred_element_type=F32,
                         precision=lax.Precision.HIGHEST) + b_ref[...]


def _ada(c_all, w_ada, b_ada):
    nb = c_all.shape[0]
    return pl.pallas_call(
        _ada_kernel,
        out_shape=jax.ShapeDtypeStruct((nb, 6 * D_MODEL), F32),
        grid=(6,),
        in_specs=[pl.BlockSpec((nb, D_MODEL), lambda j: (0, 0)),
                  pl.BlockSpec((D_MODEL, D_MODEL), lambda j: (0, j)),
                  pl.BlockSpec((1, D_MODEL), lambda j: (0, j))],
        out_specs=pl.BlockSpec((nb, D_MODEL), lambda j: (0, j)),
        compiler_params=_cparams(1),
        name="ada",
    )(c_all, w_ada, b_ada.reshape(1, 6 * D_MODEL))


def _rms(x, g):
    return x * lax.rsqrt(jnp.mean(x * x, axis=-1, keepdims=True) + RMS_EPS) * g


def _inproj_kernel(x_ref, mod_ref, n1_ref, w_ref, qa, ka, ka_sw, va, va_sw, *rest):
    qkv_b = rest[:9]
    sga, sgb, scr = rest[9:]
    mod = mod_ref[0]
    h = _rms(x_ref[...], n1_ref[...]) * (1.0 + mod[1:2]) + mod[0:1]
    hb = h.astype(BF16)
    tm = hb.shape[0]
    scale = HEAD_DIM ** -0.5

    def proj(off, width):
        return jnp.dot(hb, w_ref[:, off:off + width], preferred_element_type=F32)

    qa[...] = (proj(0, A_Q_W) * scale).astype(BF16)
    for ref, ref_sw, off in ((ka, ka_sw, A_Q_W), (va, va_sw, A_Q_W + A_KV_W)):
        r = proj(off, A_KV_W)
        ref[...] = r.astype(BF16)
        ref_sw[...] = pltpu.roll(r, HEAD_DIM, 1).astype(BF16)
    off = A_Q_W + 2 * A_KV_W
    for t in range(3):
        for gi, (_, d) in enumerate(B_GROUPS):
            r = proj(off + t * B_W + gi * B_OUT_W, B_OUT_W)
            if t == 0:
                r = r * scale
            ref = qkv_b[gi * 3 + t]
            if d == 1:
                ref[...] = r.astype(BF16)
            else:
                for c in range(B_OUT_W // LANES):
                    slot = (t * 2 + gi - 1) * (B_OUT_W // LANES) + c
                    scr[slot] = r[:, c * LANES:(c + 1) * LANES]
                    for res in range(d):
                        col = res * B_OUT_W + c * LANES
                        ref[:, col:col + LANES] = scr[
                            slot, pl.ds(res, tm // d, stride=d), :].astype(BF16)
    off += 3 * B_W
    sga[...] = jax.nn.sigmoid(proj(off, D_MODEL)).astype(BF16)
    sgb[...] = jax.nn.sigmoid(proj(off + D_MODEL, D_MODEL)).astype(BF16)


def _inproj(x2, mod3, norm1, w_in_bf, seq):
    n = x2.shape[0]
    tm = TM_INPROJ
    assert seq % tm == 0 and n % tm == 0
    row = lambda i: (i, 0)
    shapes = [(n, A_Q_W, tm)] + [(n, A_KV_W, tm)] * 4
    for _, d in B_GROUPS:
        shapes += [(n // d, d * B_OUT_W, tm // d)] * 3
    shapes += [(n, D_MODEL, tm)] * 2
    return pl.pallas_call(
        _inproj_kernel,
        out_shape=[jax.ShapeDtypeStruct((r, c), BF16) for r, c, _ in shapes],
        grid=(n // tm,),
        in_specs=[pl.BlockSpec((tm, D_MODEL), row),
                  pl.BlockSpec((1, 6, D_MODEL), lambda i: (i * tm // seq, 0, 0)),
                  pl.BlockSpec((1, D_MODEL), lambda i: (0, 0)),
                  pl.BlockSpec((D_MODEL, D_IN), lambda i: (0, 0))],
        out_specs=[pl.BlockSpec((b, c), row) for _, c, b in shapes],
        scratch_shapes=[pltpu.VMEM((6 * B_OUT_W // LANES, tm, LANES), F32)],
        compiler_params=_cparams(1),
        name="inproj",
    )(x2, mod3, norm1.reshape(1, D_MODEL), w_in_bf)


def _rel_bucket_np(rel):
    half = N_BUCKETS // 2
    max_exact = half // 2
    n = np.abs(rel)
    large = max_exact + (np.log(np.maximum(n, 1) / max_exact) / math.log(MAX_DISTANCE / max_exact)
                         * (half - max_exact)).astype(np.int32)
    large = np.minimum(large, half - 1)
    return ((rel > 0).astype(np.int32) * half + np.where(n < max_exact, n, large)).astype(np.int32)


def _bias_table(rel_bias_heads, n_q, n_k, key_off, band, dist_scale):
    p = n_q + n_k
    rel = np.arange(p) - (n_q - 1) - key_off
    bucket = _rel_bucket_np(rel * dist_scale)
    t = jnp.where((np.abs(rel) <= band)[None], rel_bias_heads.astype(F32)[bucket].T, NEG_INF)
    big = jnp.tile(t, (1, n_q + 1))
    tab = big[:, n_q - 1:n_q - 1 + n_q * (p - 1)].reshape(-1, n_q, p - 1)
    return tab[:, :, :n_k]


def _pair_rhs(k_top, k_bot, v_top, v_bot):
    kb = k_top.shape[0]
    low = jnp.where(lax.broadcasted_iota(I32, (kb, LANES), 1) < HEAD_DIM, 1.0, 0.0).astype(BF16)
    high = jnp.where(lax.broadcasted_iota(I32, (kb, LANES), 1) < HEAD_DIM, 0.0, 1.0).astype(BF16)
    rhs_k = jnp.concatenate([k_top * low, k_bot * high], axis=0)
    rhs_v = jnp.concatenate([jnp.concatenate([v_top * low, low], axis=1),
                             jnp.concatenate([v_bot * high, high], axis=1)], axis=0)
    return rhs_k, rhs_v


def _pair_attention(q_pair, rhs_k, rhs_v, bias_pair, pen, sink_pair, want_lse):
    kb = rhs_k.shape[0] // 2
    s = lax.dot_general(q_pair, rhs_k, (((1,), (1,)), ((), ())), preferred_element_type=F32)
    s = s + bias_pair
    if pen is not None:
        s = s + pen
    s0, s1 = s[:, :kb], s[:, kb:]
    m0 = jnp.max(s0, axis=-1, keepdims=True)
    m1 = jnp.max(s1, axis=-1, keepdims=True)
    if sink_pair is not None:
        m0 = jnp.maximum(m0, sink_pair[:, 0:1])
        m1 = jnp.maximum(m1, sink_pair[:, HEAD_DIM:HEAD_DIM + 1])
    p = jnp.concatenate([jnp.exp(s0 - m0), jnp.exp(s1 - m1)], axis=1).astype(BF16)
    od = jnp.dot(p, rhs_v, preferred_element_type=F32)
    o, den = od[:, :LANES], od[:, LANES:]
    low = lax.broadcasted_iota(I32, o.shape, 1) < HEAD_DIM
    m_full = jnp.where(low, m0, m1)
    if sink_pair is not None:
        den = den + jnp.exp(sink_pair - m_full)
    return o / den, (m_full + jnp.log(den)) if want_lse else None


def _attn_kernel(*refs, halo, nblk, n_pairs, n_seqs, shared_kv, has_sink, want_lse, n_steps):
    it = iter(refs)
    q_ref = next(it)
    n_kv = 4 if shared_kv else 2
    kv_refs = [[next(it) for _ in range(3)] for _ in range(n_kv)]
    bias_ref = next(it)
    sink_ref = next(it) if has_sink else None
    o_ref = next(it)
    lse_ref = next(it) if want_lse else None
    step = pl.program_id(2)
    blk = ATTN_BLOCK
    kb = blk + 2 * halo
    cats = [jnp.concatenate([r[...] for r in trio], axis=0) for trio in kv_refs]
    col = lax.broadcasted_iota(I32, (1, kb), 1)
    for b in range(nblk):
        lo = jnp.where(step == 0, halo, 0) if b == 0 else 0
        hi = jnp.where(step == n_steps - 1, kb - halo, kb) if b == nblk - 1 else kb
        pen = None
        if b == 0 or b == nblk - 1:
            pen1 = jnp.where((col >= lo) & (col < hi), 0.0, NEG_INF)
            pen = jnp.concatenate([pen1, pen1], axis=1)
        rows = slice(b * blk, (b + 1) * blk)
        krows = slice(b * blk, b * blk + kb)
        if shared_kv:
            k, k_sw, v, v_sw = (c[krows] for c in cats)
            rhs = [_pair_rhs(k, k_sw, v, v_sw), _pair_rhs(k_sw, k, v_sw, v)]
        for c in range(n_pairs * n_seqs):
            lanes = slice(c * LANES, (c + 1) * LANES)
            if shared_kv:
                rhs_k, rhs_v = rhs[c // (n_pairs // 2)]
            else:
                k, v = cats[0][krows, lanes], cats[1][krows, lanes]
                rhs_k, rhs_v = _pair_rhs(k, k, v, v)
            o, lse = _pair_attention(q_ref[rows, lanes], rhs_k, rhs_v, bias_ref[c % n_pairs], pen,
                                     sink_ref[c] if has_sink else None, want_lse)
            o_ref[rows, lanes] = o.astype(o_ref.dtype)
            if want_lse:
                lse_ref[rows, lanes] = lse


def _band_attention(q, kvs, bias, sink, *, nbatch, seq, dil, halo, kv_width, want_lse):
    n = nbatch * seq
    sub_len = seq // dil
    qw = q.shape[1] // dil
    rows = min(ATTN_ROWS, sub_len)
    n_seqs = min(dil, ATTN_ROWS // rows)
    assert sub_len % rows == 0 and rows % ATTN_BLOCK == 0 and rows % halo == 0 and dil % n_seqs == 0
    nq = sub_len // rows
    per = rows // halo
    total_halos = n // dil // halo
    cur = lambda b, r, i: (b * nq + i, r)
    prev = lambda b, r, i: (jnp.maximum((b * nq + i) * per - 1, 0), r)
    nxt = lambda b, r, i: (jnp.minimum((b * nq + i + 1) * per, total_halos - 1), r)
    const = lambda a: pl.BlockSpec(a.shape, lambda b, r, i: (0,) * a.ndim)
    qw_step, kvw_step = qw * n_seqs, kv_width * n_seqs

    in_specs = [pl.BlockSpec((rows, qw_step), cur)]
    args = [q]
    for a in kvs:
        in_specs += [pl.BlockSpec((halo, kvw_step), prev), pl.BlockSpec((rows, kvw_step), cur),
                     pl.BlockSpec((halo, kvw_step), nxt)]
        args += [a, a, a]
    in_specs.append(const(bias))
    args.append(bias)
    if sink is not None:
        in_specs.append(const(sink))
        args.append(sink)
    out_shape = [jax.ShapeDtypeStruct(q.shape, BF16)]
    out_specs = [pl.BlockSpec((rows, qw_step), cur)]
    if want_lse:
        out_shape.append(jax.ShapeDtypeStruct(q.shape, F32))
        out_specs.append(pl.BlockSpec((rows, qw_step), cur))
    return pl.pallas_call(
        functools.partial(_attn_kernel, halo=halo, nblk=rows // ATTN_BLOCK, n_pairs=qw // LANES,
                          n_seqs=n_seqs, shared_kv=len(kvs) == 4, has_sink=sink is not None,
                          want_lse=want_lse, n_steps=nq),
        out_shape=out_shape,
        grid=(nbatch, dil // n_seqs, nq),
        in_specs=in_specs,
        out_specs=out_specs,
        compiler_params=_cparams(3),
        name=f"band_attn_d{dil}",
    )(*args)


def _pack_rows(y):
    lo = pltpu.bitcast(y[:, :HALF_D].astype(BF16).astype(F32), U32)
    hi = pltpu.bitcast(y[:, HALF_D:].astype(BF16).astype(F32), U32)
    return (hi & jnp.uint32(0xFFFF0000)) | (lo >> 16)


def _store_slabs(ref, packed):
    t = packed.shape[0]
    for c in range(ROW_SLABS):
        ref[pl.ds(c, t, stride=ROW_SLABS), :] = packed[:, c * LANES:(c + 1) * LANES]


def _load_slabs(ref):
    t = ref.shape[0] // ROW_SLABS
    return jnp.concatenate([ref[pl.ds(c, t, stride=ROW_SLABS), :] for c in range(ROW_SLABS)], axis=1)


def _unpack_rows(p):
    lo = pltpu.bitcast(p << 16, F32)
    hi = pltpu.bitcast(p & jnp.uint32(0xFFFF0000), F32)
    return lo, hi


def _route(sel, scores):
    t = sel.shape[-1]
    per = N_EXPERTS // N_EXPERT_GROUPS
    shape3 = (N_EXPERT_GROUPS, per, t)
    sel3 = sel.reshape(shape3)
    sc3 = scores.reshape(shape3)
    iota_g = lax.broadcasted_iota(I32, shape3, 0)
    iota_m = lax.broadcasted_iota(I32, shape3, 1)
    iota_e = iota_g * per + iota_m
    m1 = jnp.max(sel3, axis=1, keepdims=True)
    i1 = jnp.min(jnp.where(sel3 == m1, iota_m, per), axis=1, keepdims=True)
    m2 = jnp.max(jnp.where(iota_m == i1, REMOVED, sel3), axis=1, keepdims=True)
    gscore = m1 + m2
    iota_g1 = lax.broadcasted_iota(I32, gscore.shape, 0)
    gmask = jnp.zeros(gscore.shape, jnp.bool_)
    for _ in range(TOPK_GROUPS):
        mx = jnp.max(gscore, axis=0, keepdims=True)
        ix = jnp.min(jnp.where(gscore == mx, iota_g1, N_EXPERT_GROUPS), axis=0, keepdims=True)
        hit = iota_g1 == ix
        gmask = gmask | hit
        gscore = jnp.where(hit, REMOVED, gscore)
    cur = jnp.where(gmask, sel3, NEG_INF)
    sum_all = lambda a: jnp.sum(jnp.sum(a, axis=1, keepdims=True), axis=0, keepdims=True)
    idxs, wts, hits = [], [], []
    for _ in range(TOP_K):
        mx = jnp.max(jnp.max(cur, axis=1, keepdims=True), axis=0, keepdims=True)
        cand = jnp.where(cur == mx, iota_e, N_EXPERTS)
        ix = jnp.min(jnp.min(cand, axis=1, keepdims=True), axis=0, keepdims=True)
        hit = iota_e == ix
        wts.append(sum_all(jnp.where(hit, sc3, 0.0)).reshape(1, t))
        cur = jnp.where(hit, REMOVED, cur)
        idxs.append(ix.reshape(1, t))
        hits.append(hit)
    wsum = wts[0]
    for w in wts[1:]:
        wsum = wsum + w
    wts = [w / wsum * ROUTED_SCALE for w in wts]
    onehot = hits[0].astype(F32)
    for hit in hits[1:]:
        onehot = onehot + hit.astype(F32)
    tri = (lax.broadcasted_iota(I32, (t, t), 0) <= lax.broadcasted_iota(I32, (t, t), 1))
    cum = jnp.dot(onehot.reshape(N_EXPERTS, t).astype(BF16), tri.astype(F32).astype(BF16),
                  preferred_element_type=F32)
    cum3 = cum.reshape(shape3) - 1.0
    ranks = [sum_all(jnp.where(hit, cum3, 0.0)).reshape(1, t).astype(I32) for hit in hits]
    counts = cum[:, t - 1:t].astype(I32)
    return idxs, wts, ranks, counts


def _merge_kernel(x_ref, ya_ref, o1, o2, o3, l1, l2, l3, sga_ref, sgb_ref, mod_ref, n2_ref,
                  wpa, wpb, wo, wrt, rbias, wsg, wsu, wsd,
                  base_ref, hp_ref, ridx_ref, rw_ref, rank_ref, cnt_ref, scr):
    mod = mod_ref[0]
    tm = x_ref.shape[0]

    def token_major(ref, slot, d):
        if d == 1:
            return ref[...].astype(F32)
        nc = B_OUT_W // LANES
        for c in range(nc):
            for res in range(d):
                col = res * B_OUT_W + c * LANES
                scr[slot * nc + c, pl.ds(res, tm // d, stride=d), :] = ref[
                    :, col:col + LANES].astype(F32)
        return jnp.concatenate([scr[slot * nc + c] for c in range(nc)], axis=1)

    dils = [d for _, d in B_GROUPS]
    os_ = [token_major(r, i, d) for i, (r, d) in enumerate(zip((o1, o2, o3), dils))]
    ls = [token_major(r, 3 + i, d) for i, (r, d) in enumerate(zip((l1, l2, l3), dils))]
    mx = jnp.maximum(jnp.maximum(ls[0], ls[1]), ls[2])
    es = [jnp.exp(l - mx) for l in ls]
    den = es[0] + es[1] + es[2]
    ob = (es[0] / den) * os_[0] + (es[1] / den) * os_[1] + (es[2] / den) * os_[2]
    pa = jnp.dot(ya_ref[...], wpa[...], preferred_element_type=F32)
    pb = jnp.dot(ob.astype(BF16), wpb[...], preferred_element_type=F32)
    merged = sga_ref[...].astype(F32) * pa + sgb_ref[...].astype(F32) * pb
    mix = jnp.dot(merged.astype(BF16), wo[...], preferred_element_type=F32)
    x1 = x_ref[...] + mod[2:3] * mix
    h2 = _rms(x1, n2_ref[...]) * (1.0 + mod[4:5]) + mod[3:4]
    logits = lax.dot_general(wrt[...], h2, (((1,), (1,)), ((), ())), preferred_element_type=F32,
                             precision=lax.Precision.HIGHEST)
    scores = jax.nn.sigmoid(logits)
    idxs, wts, ranks, counts = _route(scores + rbias[...], scores)
    for k in range(TOP_K):
        ridx_ref[k:k + 1, :] = idxs[k]
        rw_ref[k:k + 1, :] = wts[k]
        rank_ref[k:k + 1, :] = ranks[k]
    ridx_ref[TOP_K:, :] = jnp.zeros((8 - TOP_K, tm), I32)
    rw_ref[TOP_K:, :] = jnp.zeros((8 - TOP_K, tm), F32)
    rank_ref[TOP_K:, :] = jnp.zeros((8 - TOP_K, tm), I32)
    cnt_ref[0] = counts
    hb = h2.astype(BF16)
    g = jnp.dot(hb, wsg[...], preferred_element_type=F32)
    u = jnp.dot(hb, wsu[...], preferred_element_type=F32)
    act = (g * jax.nn.sigmoid(g) * u).astype(BF16)
    shared = jnp.dot(act, wsd[...], preferred_element_type=F32)
    base_ref[...] = x1 + mod[5:6] * shared
    _store_slabs(hp_ref, _pack_rows(h2))


def _merge(x2, ya, outs, lses, sga, sgb, mod3, norm2, wpa, wpb, wo, wrt, rbias, wsg, wsu, wsd, seq):
    n = x2.shape[0]
    tm = TM_MERGE
    assert seq % tm == 0
    row = lambda i: (i, 0)
    full = lambda a: pl.BlockSpec(a.shape, lambda i: (0,) * a.ndim)
    weights = [wpa, wpb, wo, wrt, rbias, wsg, wsu, wsd]
    group_specs = [pl.BlockSpec((tm // d, d * B_OUT_W), row) for _, d in B_GROUPS]
    lanes = lambda i: (0, i)
    return pl.pallas_call(
        _merge_kernel,
        out_shape=[jax.ShapeDtypeStruct((n, D_MODEL), F32),
                   jax.ShapeDtypeStruct((n * ROW_SLABS, LANES), U32),
                   jax.ShapeDtypeStruct((8, n), I32),
                   jax.ShapeDtypeStruct((8, n), F32),
                   jax.ShapeDtypeStruct((8, n), I32),
                   jax.ShapeDtypeStruct((n // tm, N_EXPERTS, 1), I32)],
        grid=(n // tm,),
        in_specs=[pl.BlockSpec((tm, D_MODEL), row), pl.BlockSpec((tm, A_Q_W), row)]
                 + group_specs * 2
                 + [pl.BlockSpec((tm, D_MODEL), row)] * 2
                 + [pl.BlockSpec((1, 6, D_MODEL), lambda i: (i * tm // seq, 0, 0)),
                    pl.BlockSpec((1, D_MODEL), lambda i: (0, 0))]
                 + [full(w) for w in weights],
        out_specs=[pl.BlockSpec((tm, D_MODEL), row),
                   pl.BlockSpec((tm * ROW_SLABS, LANES), row),
                   pl.BlockSpec((8, tm), lanes), pl.BlockSpec((8, tm), lanes),
                   pl.BlockSpec((8, tm), lanes),
                   pl.BlockSpec((1, N_EXPERTS, 1), lambda i: (i, 0, 0))],
        scratch_shapes=[pltpu.VMEM((6 * B_OUT_W // LANES, tm, LANES), F32)],
        compiler_params=_cparams(1),
        name="merge_route",
    )(x2, ya, *outs, *lses, sga, sgb, mod3, norm2.reshape(1, D_MODEL), *weights)


def _plan_kernel(ridx_ref, rank_ref, tb_ref, dest_ref):
    tb = tb_ref[0]
    t = ridx_ref.shape[1]
    iota_e = lax.broadcasted_iota(I32, (N_EXPERTS, t), 0)
    for k in range(TOP_K):
        base = jnp.sum(jnp.where(iota_e == ridx_ref[k:k + 1, :], tb, 0), axis=0, keepdims=True)
        dest_ref[0, :, k * t:(k + 1) * t] = base + rank_ref[k:k + 1, :]


def _plan(ridx, rank, tile_base, tm):
    n = ridx.shape[1]
    lanes = lambda i: (0, i)
    return pl.pallas_call(
        _plan_kernel,
        out_shape=jax.ShapeDtypeStruct((n // tm, 1, TOP_K * tm), I32),
        grid=(n // tm,),
        in_specs=[pl.BlockSpec((8, tm), lanes), pl.BlockSpec((8, tm), lanes),
                  pl.BlockSpec((1, N_EXPERTS, 1), lambda i: (i, 0, 0))],
        out_specs=pl.BlockSpec((1, 1, TOP_K * tm), lambda i: (i, 0, 0)),
        compiler_params=_cparams(1),
        name="moe_plan",
    )(ridx, rank, tile_base)


def _sc_mesh():
    return plsc.VectorSubcoreMesh(core_axis_name="core", subcore_axis_name="subcore",
                                  num_cores=SC_CORES, num_subcores=SC_SUBCORES)


def _sc_worker():
    return lax.axis_index("subcore") * SC_CORES + lax.axis_index("core")


def _sc_dispatch(hp, dest_rows, n_rows):
    n = hp.shape[0]
    halves = TM_MERGE // SC_CHUNK
    workers = SC_CORES * SC_SUBCORES
    tiles_per_w = n // TM_MERGE // workers
    idx_per_w = tiles_per_w * TOP_K * halves
    assert tiles_per_w * workers * TM_MERGE == n

    @functools.partial(
        pl.kernel, mesh=_sc_mesh(),
        out_type=jax.ShapeDtypeStruct((n_rows, ROW_SLABS, LANES), U32),
        scratch_types=[pltpu.VMEM((idx_per_w, SC_CHUNK), I32),
                       pltpu.VMEM((SC_CHUNK, ROW_SLABS, LANES), U32),
                       pltpu.SemaphoreType.DMA],
        name="moe_dispatch_sc")
    def scatter(hp_hbm, idx_hbm, xs_hbm, idx_v, rows_v, sem):
        wid = _sc_worker()
        pltpu.sync_copy(idx_hbm.at[pl.ds(wid * idx_per_w, idx_per_w)], idx_v)

        @pl.loop(0, tiles_per_w * halves)
        def _(j):
            tile = j // halves
            h = j - tile * halves
            tok = (wid * tiles_per_w + tile) * TM_MERGE + h * SC_CHUNK
            pltpu.sync_copy(hp_hbm.at[pl.ds(tok, SC_CHUNK)], rows_v)
            for k in range(TOP_K):
                row = (tile * TOP_K + k) * halves + h
                pltpu.async_copy(rows_v, xs_hbm.at[idx_v.at[row]], sem).wait()

    return scatter(hp, dest_rows)


def _sc_collect(ys, dest_rows):
    n_chunks = dest_rows.shape[0]
    workers = SC_CORES * SC_SUBCORES
    per_w = n_chunks // workers
    assert per_w * workers == n_chunks

    @functools.partial(
        pl.kernel, mesh=_sc_mesh(),
        out_type=jax.ShapeDtypeStruct((n_chunks * SC_CHUNK, ROW_SLABS, LANES), U32),
        scratch_types=[pltpu.VMEM((per_w, SC_CHUNK), I32),
                       pltpu.VMEM((SC_CHUNK, ROW_SLABS, LANES), U32),
                       pltpu.SemaphoreType.DMA],
        name="moe_collect_sc")
    def gather(ys_hbm, idx_hbm, out_hbm, idx_v, rows_v, sem):
        wid = _sc_worker()
        base = wid * per_w
        pltpu.sync_copy(idx_hbm.at[pl.ds(base, per_w)], idx_v)

        @pl.loop(0, per_w)
        def _(j):
            pltpu.async_copy(ys_hbm.at[idx_v.at[j]], rows_v, sem).wait()
            pltpu.sync_copy(rows_v, out_hbm.at[pl.ds((base + j) * SC_CHUNK, SC_CHUNK)])

    return gather(ys, dest_rows)


def _expert_kernel(blk_e_ref, valid_ref, xs_ref, wgu_ref, wd_ref, ys_ref):
    del blk_e_ref
    valid = valid_ref[pl.program_id(0)]

    @pl.when(valid > 0)
    def _():
        rows = lax.broadcasted_iota(I32, (xs_ref.shape[0] // ROW_SLABS, 1), 0)
        lo, hi = _unpack_rows(jnp.where(rows < valid, _load_slabs(xs_ref), jnp.uint32(0)))
        wgu = wgu_ref[0]
        gu = jnp.dot(lo.astype(BF16), wgu[:HALF_D], preferred_element_type=F32) \
            + jnp.dot(hi.astype(BF16), wgu[HALF_D:], preferred_element_type=F32)
        g = gu[:, :D_EXPERT]
        u = gu[:, D_EXPERT:]
        act = (g * jax.nn.sigmoid(g) * u).astype(BF16)
        _store_slabs(ys_ref, _pack_rows(jnp.dot(act, wd_ref[0], preferred_element_type=F32)))

    @pl.when(valid <= 0)
    def _():
        ys_ref[...] = jnp.zeros(ys_ref.shape, U32)


def _experts(xs, blk_e, blk_valid, wgu, wd):
    n_rows = xs.shape[0] // ROW_SLABS
    bm = MOE_BM
    nblk = n_rows // bm
    return pl.pallas_call(
        _expert_kernel,
        out_shape=jax.ShapeDtypeStruct((n_rows * ROW_SLABS, LANES), U32),
        grid_spec=pltpu.PrefetchScalarGridSpec(
            num_scalar_prefetch=2,
            grid=(nblk,),
            in_specs=[pl.BlockSpec((bm * ROW_SLABS, LANES), lambda i, be, nu: (i, 0)),
                      pl.BlockSpec((1, D_MODEL, 2 * D_EXPERT), lambda i, be, nu: (be[i], 0, 0)),
                      pl.BlockSpec((1, D_EXPERT, D_MODEL), lambda i, be, nu: (be[i], 0, 0))],
            out_specs=pl.BlockSpec((bm * ROW_SLABS, LANES), lambda i, be, nu: (i, 0))),
        compiler_params=_cparams(1),
        name="moe_experts",
    )(blk_e, blk_valid, xs, wgu, wd)


def _combine_kernel(g_ref, base_ref, rw_ref, mod_ref, fn_ref, y_ref):
    tm = base_ref.shape[0]
    rw = rw_ref[...]
    acc_lo = jnp.zeros((tm, HALF_D), F32)
    acc_hi = jnp.zeros((tm, HALF_D), F32)
    for k in range(TOP_K):
        lo, hi = _unpack_rows(_load_slabs(g_ref.at[0, k]))
        w = rw[:, k:k + 1]
        acc_lo = acc_lo + w * lo
        acc_hi = acc_hi + w * hi
    routed = jnp.concatenate([acc_lo, acc_hi], axis=1)
    x2 = base_ref[...] + mod_ref[0][5:6] * routed
    y_ref[...] = _rms(x2, fn_ref[...])


def _combine(g, base, rw_t, mod3, final_norm, seq):
    n = base.shape[0]
    tm = TM_COMBINE
    assert seq % tm == 0
    g4 = g.reshape(n // tm, TOP_K, tm * ROW_SLABS, LANES)
    return pl.pallas_call(
        _combine_kernel,
        out_shape=jax.ShapeDtypeStruct((n, D_MODEL), F32),
        grid=(n // tm,),
        in_specs=[pl.BlockSpec((1, TOP_K, tm * ROW_SLABS, LANES), lambda i: (i, 0, 0, 0)),
                  pl.BlockSpec((tm, D_MODEL), lambda i: (i, 0)),
                  pl.BlockSpec((tm, 8), lambda i: (i, 0)),
                  pl.BlockSpec((1, 6, D_MODEL), lambda i: (i * tm // seq, 0, 0)),
                  pl.BlockSpec((1, D_MODEL), lambda i: (0, 0))],
        out_specs=pl.BlockSpec((tm, D_MODEL), lambda i: (i, 0)),
        compiler_params=_cparams(1),
        name="moe_combine",
    )(g4, base, rw_t, mod3, final_norm.reshape(1, D_MODEL))


def _block_layout(tile_counts, n):
    bm = MOE_BM
    c = tile_counts[:, :, 0]
    ntiles = c.shape[0]
    counts = jnp.sum(c, axis=0)
    padded = (counts + bm - 1) // bm * bm
    earlier_e = np.tri(N_EXPERTS, k=-1, dtype=bool)
    pstart = jnp.sum(jnp.where(earlier_e, padded[None, :], 0), axis=1)
    pend = pstart + padded
    earlier_t = np.tri(ntiles, k=-1, dtype=bool)
    tile_base = pstart[None, :] + jnp.sum(jnp.where(earlier_t[:, :, None], c[None], 0), axis=1)
    nblk = -(-n * TOP_K // bm) + N_EXPERTS
    blk = jnp.arange(nblk, dtype=I32)
    blk_e = jnp.minimum(jnp.sum((pend[None, :] <= blk[:, None] * bm).astype(I32), axis=1),
                        N_EXPERTS - 1)
    onehot_e = blk_e[:, None] == jnp.arange(N_EXPERTS, dtype=I32)[None, :]
    end_e = jnp.sum(jnp.where(onehot_e, (pstart + counts)[None, :], 0), axis=1)
    blk_valid = jnp.clip(end_e - blk * bm, 0, bm).astype(I32)
    return tile_base.astype(I32)[:, :, None], blk_e.astype(I32), blk_valid, nblk * bm


def _trunk(x, mod3, p):
    nbatch, seq, _ = x.shape
    n = nbatch * seq
    x2 = x.reshape(n, D_MODEL)
    proj = _inproj(x2, mod3, p["norm1"], p["w_in"], seq)
    qa = proj[0]
    sga, sgb = proj[14:]
    (ya,) = _band_attention(qa, proj[1:5], p["bias_a"], p["sink_a"], nbatch=nbatch, seq=seq, dil=1,
                            halo=A_HALF_WINDOW, kv_width=A_KV_W, want_lse=False)
    outs, lses = [], []
    for gi, (w, d) in enumerate(B_GROUPS):
        q, k, v = proj[5 + 3 * gi:8 + 3 * gi]
        o, lse = _band_attention(q, (k, v), p["bias_b"][gi], None, nbatch=nbatch, seq=seq, dil=d,
                                 halo=w // (2 * d), kv_width=B_OUT_W, want_lse=True)
        outs.append(o)
        lses.append(lse)
    base, hp, ridx, rw, rank, tile_counts = _merge(
        x2, ya, outs, lses, sga, sgb, mod3, p["norm2"], p["w_pa"], p["w_pb"], p["w_o"], p["w_rt"],
        p["rbias"], p["ws_gate"], p["ws_up"], p["ws_down"], seq)
    assert TM_MERGE == TM_COMBINE
    tile_base, blk_e, blk_valid, n_rows = _block_layout(tile_counts, n)
    dest_rows = _plan(ridx, rank, tile_base, TM_MERGE).reshape(-1, SC_CHUNK)
    slabs = lambda a: a.reshape(-1, ROW_SLABS, LANES)
    flat = lambda a: a.reshape(-1, LANES)
    xs = _sc_dispatch(slabs(hp), dest_rows, n_rows)
    ys = _experts(flat(xs), blk_e, blk_valid, p["w_gu"], p["w_down"])
    g = _sc_collect(slabs(ys), dest_rows)
    y = _combine(g, base, rw.T, mod3, p["final_norm"], seq)
    return y.reshape(nbatch, seq, D_MODEL)


def kernel(x_prompt, x_sample, c_prompt, c_sample, rel_bias, w_ada, b_ada, norm1, w_in, sink, w_pa, w_pb, w_o, norm2, w_router, router_bias, w_gate, w_up, w_down, ws_gate, ws_up, ws_down, final_norm):
    assert w_ada.shape[0] == 1
    nbp = x_prompt.shape[0]
    mod = _ada(jnp.concatenate([c_prompt, c_sample], axis=0), w_ada[0], b_ada[0])
    mod3 = mod.reshape(-1, 6, D_MODEL)

    def pair_bias(heads, halo, dist_scale):
        kb = ATTN_BLOCK + 2 * halo
        tab = _bias_table(heads, ATTN_BLOCK, kb, halo, halo, dist_scale)
        return tab.reshape(-1, 2, ATTN_BLOCK, kb).transpose(0, 2, 1, 3).reshape(-1, ATTN_BLOCK, 2 * kb)

    bias_a = pair_bias(rel_bias[:, :A_Q_HEADS], A_HALF_WINDOW, 1)
    bias_b = []
    for gi, (w, d) in enumerate(B_GROUPS):
        h0 = A_Q_HEADS + gi * B_HEADS_PER_GROUP
        bias_b.append(pair_bias(rel_bias[:, h0:h0 + B_HEADS_PER_GROUP], w // (2 * d), d))
    p = {
        "norm1": norm1[0], "norm2": norm2[0], "final_norm": final_norm,
        "w_in": w_in[0].astype(BF16),
        "bias_a": bias_a,
        "sink_a": jnp.repeat(sink[0].astype(F32), HEAD_DIM).reshape(A_Q_HEADS // 2, 1, LANES),
        "bias_b": bias_b,
        "w_pa": w_pa[0].astype(BF16), "w_pb": w_pb[0].astype(BF16), "w_o": w_o[0].astype(BF16),
        "w_rt": w_router[0].T, "rbias": router_bias[0].reshape(N_EXPERTS, 1),
        "ws_gate": ws_gate[0].astype(BF16), "ws_up": ws_up[0].astype(BF16),
        "ws_down": ws_down[0].astype(BF16),
        "w_gu": jnp.concatenate([w_gate[0], w_up[0]], axis=-1).astype(BF16),
        "w_down": w_down[0].astype(BF16),
    }
    y_prompt = _trunk(x_prompt, mod3[:nbp], p)
    y_sample = _trunk(x_sample, mod3[nbp:], p)
    return (y_prompt, y_sample)
```

```python
import functools
import math

import jax
import jax.numpy as jnp
import numpy as np
from jax import lax
from jax.experimental import pallas as pl
from jax.experimental.pallas import tpu as pltpu
from jax.experimental.pallas import tpu_sc as plsc

F32 = jnp.float32
BF16 = jnp.bfloat16
U32 = jnp.uint32
I32 = jnp.int32

D_MODEL = 1024
HEAD_DIM = 64
A_Q_HEADS = 8
A_KV_HEADS = 2
A_HALF_WINDOW = 128
B_GROUPS = ((128, 1), (512, 4), (2048, 16))
B_HEADS_PER_GROUP = 4
N_BUCKETS = 32
MAX_DISTANCE = 1024
N_EXPERTS = 64
TOP_K = 6
N_EXPERT_GROUPS = 8
TOPK_GROUPS = 4
D_EXPERT = 256
D_SHARED = 256
ROUTED_SCALE = 2.5
RMS_EPS = 1e-6
NEG_INF = -1e30
REMOVED = -3e38

A_Q_W = A_Q_HEADS * HEAD_DIM
A_KV_W = A_KV_HEADS * HEAD_DIM
B_W = len(B_GROUPS) * B_HEADS_PER_GROUP * HEAD_DIM
B_OUT_W = B_HEADS_PER_GROUP * HEAD_DIM
IN_WIDTHS = (A_Q_W, A_KV_W, A_KV_W, B_W, B_W, B_W, D_MODEL, D_MODEL)
D_IN = sum(IN_WIDTHS)
HALF_D = D_MODEL // 2
LANES = 128
ROW_SLABS = HALF_D // LANES
SC_CORES = 2
SC_SUBCORES = 16
SC_CHUNK = 128

TM_INPROJ = 512
TM_MERGE = 512
ATTN_BLOCK = 128
ATTN_ROWS = 512
MOE_BM = 512
TM_COMBINE = 512
VMEM_LIMIT = 56 * 1024 * 1024


def _cparams(n_axes):
    return pltpu.CompilerParams(
        dimension_semantics=("arbitrary",) * n_axes, vmem_limit_bytes=VMEM_LIMIT)


def _ada_kernel(c_ref, w_ref, b_ref, o_ref):
    c = c_ref[...]
    s = c * jax.nn.sigmoid(c)
    o_ref[...] = jnp.dot(s, w_ref[...], preferred_element_type=F32,
                         precision=lax.Precision.HIGHEST) + b_ref[...]


def _ada(c_all, w_ada, b_ada):
    nb = c_all.shape[0]
    return pl.pallas_call(
        _ada_kernel,
        out_shape=jax.ShapeDtypeStruct((nb, 6 * D_MODEL), F32),
        grid=(6,),
        in_specs=[pl.BlockSpec((nb, D_MODEL), lambda j: (0, 0)),
                  pl.BlockSpec((D_MODEL, D_MODEL), lambda j: (0, j)),
                  pl.BlockSpec((1, D_MODEL), lambda j: (0, j))],
        out_specs=pl.BlockSpec((nb, D_MODEL), lambda j: (0, j)),
        compiler_params=_cparams(1),
        name="ada",
    )(c_all, w_ada, b_ada.reshape(1, 6 * D_MODEL))


def _rms(x, g):
    return x * lax.rsqrt(jnp.mean(x * x, axis=-1, keepdims=True) + RMS_EPS) * g


def _inproj_kernel(x_ref, mod_ref, n1_ref, w_ref, qa, ka, ka_sw, va, va_sw, *rest):
    qkv_b = rest[:9]
    sga, sgb, scr = rest[9:]
    mod = mod_ref[0]
    h = _rms(x_ref[...], n1_ref[...]) * (1.0 + mod[1:2]) + mod[0:1]
    hb = h.astype(BF16)
    tm = hb.shape[0]
    scale = HEAD_DIM ** -0.5

    def proj(off, width):
        return jnp.dot(hb, w_ref[:, off:off + width], preferred_element_type=F32)

    qa[...] = (proj(0, A_Q_W) * scale).astype(BF16)
    for ref, ref_sw, off in ((ka, ka_sw, A_Q_W), (va, va_sw, A_Q_W + A_KV_W)):
        r = proj(off, A_KV_W)
        ref[...] = r.astype(BF16)
        ref_sw[...] = pltpu.roll(r, HEAD_DIM, 1).astype(BF16)
    off = A_Q_W + 2 * A_KV_W
    for t in range(3):
        for gi, (_, d) in enumerate(B_GROUPS):
            r = proj(off + t * B_W + gi * B_OUT_W, B_OUT_W)
            if t == 0:
                r = r * scale
            ref = qkv_b[gi * 3 + t]
            if d == 1:
                ref[...] = r.astype(BF16)
            else:
                for c in range(B_OUT_W // LANES):
                    slot = (t * 2 + gi - 1) * (B_OUT_W // LANES) + c
                    scr[slot] = r[:, c * LANES:(c + 1) * LANES]
                    for res in range(d):
                        col = res * B_OUT_W + c * LANES
                        ref[:, col:col + LANES] = scr[
                            slot, pl.ds(res, tm // d, stride=d), :].astype(BF16)
    off += 3 * B_W
    sga[...] = jax.nn.sigmoid(proj(off, D_MODEL)).astype(BF16)
    sgb[...] = jax.nn.sigmoid(proj(off + D_MODEL, D_MODEL)).astype(BF16)


def _inproj(x2, mod3, norm1, w_in_bf, seq):
    n = x2.shape[0]
    tm = TM_INPROJ
    assert seq % tm == 0 and n % tm == 0
    row = lambda i: (i, 0)
    shapes = [(n, A_Q_W, tm)] + [(n, A_KV_W, tm)] * 4
    for _, d in B_GROUPS:
        shapes += [(n // d, d * B_OUT_W, tm // d)] * 3
    shapes += [(n, D_MODEL, tm)] * 2
    return pl.pallas_call(
        _inproj_kernel,
        out_shape=[jax.ShapeDtypeStruct((r, c), BF16) for r, c, _ in shapes],
        grid=(n // tm,),
        in_specs=[pl.BlockSpec((tm, D_MODEL), row),
                  pl.BlockSpec((1, 6, D_MODEL), lambda i: (i * tm // seq, 0, 0)),
                  pl.BlockSpec((1, D_MODEL), lambda i: (0, 0)),
                  pl.BlockSpec((D_MODEL, D_IN), lambda i: (0, 0))],
        out_specs=[pl.BlockSpec((b, c), row) for _, c, b in shapes],
        scratch_shapes=[pltpu.VMEM((6 * B_OUT_W // LANES, tm, LANES), F32)],
        compiler_params=_cparams(1),
        name="inproj",
    )(x2, mod3, norm1.reshape(1, D_MODEL), w_in_bf)


def _rel_bucket_np(rel):
    half = N_BUCKETS // 2
    max_exact = half // 2
    n = np.abs(rel)
    large = max_exact + (np.log(np.maximum(n, 1) / max_exact) / math.log(MAX_DISTANCE / max_exact)
                         * (half - max_exact)).astype(np.int32)
    large = np.minimum(large, half - 1)
    return ((rel > 0).astype(np.int32) * half + np.where(n < max_exact, n, large)).astype(np.int32)


def _bias_table(rel_bias_heads, n_q, n_k, key_off, band, dist_scale):
    p = n_q + n_k
    rel = np.arange(p) - (n_q - 1) - key_off
    bucket = _rel_bucket_np(rel * dist_scale)
    t = jnp.where((np.abs(rel) <= band)[None], rel_bias_heads.astype(F32)[bucket].T, NEG_INF)
    big = jnp.tile(t, (1, n_q + 1))
    tab = big[:, n_q - 1:n_q - 1 + n_q * (p - 1)].reshape(-1, n_q, p - 1)
    return tab[:, :, :n_k]


def _pair_rhs(k_top, k_bot, v_top, v_bot):
    kb = k_top.shape[0]
    low = jnp.where(lax.broadcasted_iota(I32, (kb, LANES), 1) < HEAD_DIM, 1.0, 0.0).astype(BF16)
    high = jnp.where(lax.broadcasted_iota(I32, (kb, LANES), 1) < HEAD_DIM, 0.0, 1.0).astype(BF16)
    rhs_k = jnp.concatenate([k_top * low, k_bot * high], axis=0)
    rhs_v = jnp.concatenate([jnp.concatenate([v_top * low, low], axis=1),
                             jnp.concatenate([v_bot * high, high], axis=1)], axis=0)
    return rhs_k, rhs_v


def _pair_attention(q_pair, rhs_k, rhs_v, bias_pair, pen, sink_pair, want_lse):
    kb = rhs_k.shape[0] // 2
    s = lax.dot_general(q_pair, rhs_k, (((1,), (1,)), ((), ())), preferred_element_type=F32)
    s = s + bias_pair
    if pen is not None:
        s = s + pen
    s0, s1 = s[:, :kb], s[:, kb:]
    m0 = jnp.max(s0, axis=-1, keepdims=True)
    m1 = jnp.max(s1, axis=-1, keepdims=True)
    if sink_pair is not None:
        m0 = jnp.maximum(m0, sink_pair[:, 0:1])
        m1 = jnp.maximum(m1, sink_pair[:, HEAD_DIM:HEAD_DIM + 1])
    p = jnp.concatenate([jnp.exp(s0 - m0), jnp.exp(s1 - m1)], axis=1).astype(BF16)
    od = jnp.dot(p, rhs_v, preferred_element_type=F32)
    o, den = od[:, :LANES], od[:, LANES:]
    low = lax.broadcasted_iota(I32, o.shape, 1) < HEAD_DIM
    m_full = jnp.where(low, m0, m1)
    if sink_pair is not None:
        den = den + jnp.exp(sink_pair - m_full)
    return o / den, (m_full + jnp.log(den)) if want_lse else None


def _attn_kernel(*refs, halo, nblk, n_pairs, n_seqs, shared_kv, has_sink, want_lse, n_steps):
    it = iter(refs)
    q_ref = next(it)
    n_kv = 4 if shared_kv else 2
    kv_refs = [[next(it) for _ in range(3)] for _ in range(n_kv)]
    bias_ref = next(it)
    sink_ref = next(it) if has_sink else None
    o_ref = next(it)
    lse_ref = next(it) if want_lse else None
    step = pl.program_id(2)
    blk = ATTN_BLOCK
    kb = blk + 2 * halo
    cats = [jnp.concatenate([r[...] for r in trio], axis=0) for trio in kv_refs]
    col = lax.broadcasted_iota(I32, (1, kb), 1)
    for b in range(nblk):
        lo = jnp.where(step == 0, halo, 0) if b == 0 else 0
        hi = jnp.where(step == n_steps - 1, kb - halo, kb) if b == nblk - 1 else kb
        pen = None
        if b == 0 or b == nblk - 1:
            pen1 = jnp.where((col >= lo) & (col < hi), 0.0, NEG_INF)
            pen = jnp.concatenate([pen1, pen1], axis=1)
        rows = slice(b * blk, (b + 1) * blk)
        krows = slice(b * blk, b * blk + kb)
        if shared_kv:
            k, k_sw, v, v_sw = (c[krows] for c in cats)
            rhs = [_pair_rhs(k, k_sw, v, v_sw), _pair_rhs(k_sw, k, v_sw, v)]
        for c in range(n_pairs * n_seqs):
            lanes = slice(c * LANES, (c + 1) * LANES)
            if shared_kv:
                rhs_k, rhs_v = rhs[c // (n_pairs // 2)]
            else:
                k, v = cats[0][krows, lanes], cats[1][krows, lanes]
                rhs_k, rhs_v = _pair_rhs(k, k, v, v)
            o, lse = _pair_attention(q_ref[rows, lanes], rhs_k, rhs_v, bias_ref[c % n_pairs], pen,
                                     sink_ref[c] if has_sink else None, want_lse)
            o_ref[rows, lanes] = o.astype(o_ref.dtype)
            if want_lse:
                lse_ref[rows, lanes] = lse


def _band_attention(q, kvs, bias, sink, *, nbatch, seq, dil, halo, kv_width, want_lse):
    n = nbatch * seq
    sub_len = seq // dil
    qw = q.shape[1] // dil
    rows = min(ATTN_ROWS, sub_len)
    n_seqs = min(dil, ATTN_ROWS // rows)
    assert sub_len % rows == 0 and rows % ATTN_BLOCK == 0 and rows % halo == 0 and dil % n_seqs == 0
    nq = sub_len // rows
    per = rows // halo
    total_halos = n // dil // halo
    cur = lambda b, r, i: (b * nq + i, r)
    prev = lambda b, r, i: (jnp.maximum((b * nq + i) * per - 1, 0), r)
    nxt = lambda b, r, i: (jnp.minimum((b * nq + i + 1) * per, total_halos - 1), r)
    const = lambda a: pl.BlockSpec(a.shape, lambda b, r, i: (0,) * a.ndim)
    qw_step, kvw_step = qw * n_seqs, kv_width * n_seqs

    in_specs = [pl.BlockSpec((rows, qw_step), cur)]
    args = [q]
    for a in kvs:
        in_specs += [pl.BlockSpec((halo, kvw_step), prev), pl.BlockSpec((rows, kvw_step), cur),
                     pl.BlockSpec((halo, kvw_step), nxt)]
        args += [a, a, a]
    in_specs.append(const(bias))
    args.append(bias)
    if sink is not None:
        in_specs.append(const(sink))
        args.append(sink)
    out_shape = [jax.ShapeDtypeStruct(q.shape, BF16)]
    out_specs = [pl.BlockSpec((rows, qw_step), cur)]
    if want_lse:
        out_shape.append(jax.ShapeDtypeStruct(q.shape, F32))
        out_specs.append(pl.BlockSpec((rows, qw_step), cur))
    return pl.pallas_call(
        functools.partial(_attn_kernel, halo=halo, nblk=rows // ATTN_BLOCK, n_pairs=qw // LANES,
                          n_seqs=n_seqs, shared_kv=len(kvs) == 4, has_sink=sink is not None,
                          want_lse=want_lse, n_steps=nq),
        out_shape=out_shape,
        grid=(nbatch, dil // n_seqs, nq),
        in_specs=in_specs,
        out_specs=out_specs,
        compiler_params=_cparams(3),
        name=f"band_attn_d{dil}",
    )(*args)


def _pack_rows(y):
    lo = pltpu.bitcast(y[:, :HALF_D].astype(BF16).astype(F32), U32)
    hi = pltpu.bitcast(y[:, HALF_D:].astype(BF16).astype(F32), U32)
    return (hi & jnp.uint32(0xFFFF0000)) | (lo >> 16)


def _store_slabs(ref, packed):
    t = packed.shape[0]
    for c in range(ROW_SLABS):
        ref[pl.ds(c, t, stride=ROW_SLABS), :] = packed[:, c * LANES:(c + 1) * LANES]


def _load_slabs(ref):
    t = ref.shape[0] // ROW_SLABS
    return jnp.concatenate([ref[pl.ds(c, t, stride=ROW_SLABS), :] for c in range(ROW_SLABS)], axis=1)


def _unpack_rows(p):
    lo = pltpu.bitcast(p << 16, F32)
    hi = pltpu.bitcast(p & jnp.uint32(0xFFFF0000), F32)
    return lo, hi


def _route(sel, scores):
    t = sel.shape[-1]
    per = N_EXPERTS // N_EXPERT_GROUPS
    shape3 = (N_EXPERT_GROUPS, per, t)
    sel3 = sel.reshape(shape3)
    sc3 = scores.reshape(shape3)
    iota_g = lax.broadcasted_iota(I32, shape3, 0)
    iota_m = lax.broadcasted_iota(I32, shape3, 1)
    iota_e = iota_g * per + iota_m
    m1 = jnp.max(sel3, axis=1, keepdims=True)
    i1 = jnp.min(jnp.where(sel3 == m1, iota_m, per), axis=1, keepdims=True)
    m2 = jnp.max(jnp.where(iota_m == i1, REMOVED, sel3), axis=1, keepdims=True)
    gscore = m1 + m2
    iota_g1 = lax.broadcasted_iota(I32, gscore.shape, 0)
    gmask = jnp.zeros(gscore.shape, jnp.bool_)
    for _ in range(TOPK_GROUPS):
        mx = jnp.max(gscore, axis=0, keepdims=True)
        ix = jnp.min(jnp.where(gscore == mx, iota_g1, N_EXPERT_GROUPS), axis=0, keepdims=True)
        hit = iota_g1 == ix
        gmask = gmask | hit
        gscore = jnp.where(hit, REMOVED, gscore)
    cur = jnp.where(gmask, sel3, NEG_INF)
    sum_all = lambda a: jnp.sum(jnp.sum(a, axis=1, keepdims=True), axis=0, keepdims=True)
    idxs, wts, hits = [], [], []
    for _ in range(TOP_K):
        mx = jnp.max(jnp.max(cur, axis=1, keepdims=True), axis=0, keepdims=True)
        cand = jnp.where(cur == mx, iota_e, N_EXPERTS)
        ix = jnp.min(jnp.min(cand, axis=1, keepdims=True), axis=0, keepdims=True)
        hit = iota_e == ix
        wts.append(sum_all(jnp.where(hit, sc3, 0.0)).reshape(1, t))
        cur = jnp.where(hit, REMOVED, cur)
        idxs.append(ix.reshape(1, t))
        hits.append(hit)
    wsum = wts[0]
    for w in wts[1:]:
        wsum = wsum + w
    wts = [w / wsum * ROUTED_SCALE for w in wts]
    onehot = hits[0].astype(F32)
    for hit in hits[1:]:
        onehot = onehot + hit.astype(F32)
    tri = (lax.broadcasted_iota(I32, (t, t), 0) <= lax.broadcasted_iota(I32, (t, t), 1))
    cum = jnp.dot(onehot.reshape(N_EXPERTS, t).astype(BF16), tri.astype(F32).astype(BF16),
                  preferred_element_type=F32)
    cum3 = cum.reshape(shape3) - 1.0
    ranks = [sum_all(jnp.where(hit, cum3, 0.0)).reshape(1, t).astype(I32) for hit in hits]
    counts = cum[:, t - 1:t].astype(I32)
    return idxs, wts, ranks, counts


def _merge_kernel(x_ref, ya_ref, o1, o2, o3, l1, l2, l3, sga_ref, sgb_ref, mod_ref, n2_ref,
                  wpa, wpb, wo, wrt, rbias, wsg, wsu, wsd,
                  base_ref, hp_ref, ridx_ref, rw_ref, rank_ref, cnt_ref, scr):
    mod = mod_ref[0]
    tm = x_ref.shape[0]

    def token_major(ref, slot, d):
        if d == 1:
            return ref[...].astype(F32)
        nc = B_OUT_W // LANES
        for c in range(nc):
            for res in range(d):
                col = res * B_OUT_W + c * LANES
                scr[slot * nc + c, pl.ds(res, tm // d, stride=d), :] = ref[
                    :, col:col + LANES].astype(F32)
        return jnp.concatenate([scr[slot * nc + c] for c in range(nc)], axis=1)

    dils = [d for _, d in B_GROUPS]
    os_ = [token_major(r, i, d) for i, (r, d) in enumerate(zip((o1, o2, o3), dils))]
    ls = [token_major(r, 3 + i, d) for i, (r, d) in enumerate(zip((l1, l2, l3), dils))]
    mx = jnp.maximum(jnp.maximum(ls[0], ls[1]), ls[2])
    es = [jnp.exp(l - mx) for l in ls]
    den = es[0] + es[1] + es[2]
    ob = (es[0] / den) * os_[0] + (es[1] / den) * os_[1] + (es[2] / den) * os_[2]
    pa = jnp.dot(ya_ref[...], wpa[...], preferred_element_type=F32)
    pb = jnp.dot(ob.astype(BF16), wpb[...], preferred_element_type=F32)
    merged = sga_ref[...].astype(F32) * pa + sgb_ref[...].astype(F32) * pb
    mix = jnp.dot(merged.astype(BF16), wo[...], preferred_element_type=F32)
    x1 = x_ref[...] + mod[2:3] * mix
    h2 = _rms(x1, n2_ref[...]) * (1.0 + mod[4:5]) + mod[3:4]
    logits = lax.dot_general(wrt[...], h2, (((1,), (1,)), ((), ())), preferred_element_type=F32,
                             precision=lax.Precision.HIGHEST)
    scores = jax.nn.sigmoid(logits)
    idxs, wts, ranks, counts = _route(scores + rbias[...], scores)
    for k in range(TOP_K):
        ridx_ref[k:k + 1, :] = idxs[k]
        rw_ref[k:k + 1, :] = wts[k]
        rank_ref[k:k + 1, :] = ranks[k]
    ridx_ref[TOP_K:, :] = jnp.zeros((8 - TOP_K, tm), I32)
    rw_ref[TOP_K:, :] = jnp.zeros((8 - TOP_K, tm), F32)
    rank_ref[TOP_K:, :] = jnp.zeros((8 - TOP_K, tm), I32)
    cnt_ref[0] = counts
    hb = h2.astype(BF16)
    g = jnp.dot(hb, wsg[...], preferred_element_type=F32)
    u = jnp.dot(hb, wsu[...], preferred_element_type=F32)
    act = (g * jax.nn.sigmoid(g) * u).astype(BF16)
    shared = jnp.dot(act, wsd[...], preferred_element_type=F32)
    base_ref[...] = x1 + mod[5:6] * shared
    _store_slabs(hp_ref, _pack_rows(h2))


def _merge(x2, ya, outs, lses, sga, sgb, mod3, norm2, wpa, wpb, wo, wrt, rbias, wsg, wsu, wsd, seq):
    n = x2.shape[0]
    tm = TM_MERGE
    assert seq % tm == 0
    row = lambda i: (i, 0)
    full = lambda a: pl.BlockSpec(a.shape, lambda i: (0,) * a.ndim)
    weights = [wpa, wpb, wo, wrt, rbias, wsg, wsu, wsd]
    group_specs = [pl.BlockSpec((tm // d, d * B_OUT_W), row) for _, d in B_GROUPS]
    lanes = lambda i: (0, i)
    return pl.pallas_call(
        _merge_kernel,
        out_shape=[jax.ShapeDtypeStruct((n, D_MODEL), F32),
                   jax.ShapeDtypeStruct((n * ROW_SLABS, LANES), U32),
                   jax.ShapeDtypeStruct((8, n), I32),
                   jax.ShapeDtypeStruct((8, n), F32),
                   jax.ShapeDtypeStruct((8, n), I32),
                   jax.ShapeDtypeStruct((n // tm, N_EXPERTS, 1), I32)],
        grid=(n // tm,),
        in_specs=[pl.BlockSpec((tm, D_MODEL), row), pl.BlockSpec((tm, A_Q_W), row)]
                 + group_specs * 2
                 + [pl.BlockSpec((tm, D_MODEL), row)] * 2
                 + [pl.BlockSpec((1, 6, D_MODEL), lambda i: (i * tm // seq, 0, 0)),
                    pl.BlockSpec((1, D_MODEL), lambda i: (0, 0))]
                 + [full(w) for w in weights],
        out_specs=[pl.BlockSpec((tm, D_MODEL), row),
                   pl.BlockSpec((tm * ROW_SLABS, LANES), row),
                   pl.BlockSpec((8, tm), lanes), pl.BlockSpec((8, tm), lanes),
                   pl.BlockSpec((8, tm), lanes),
                   pl.BlockSpec((1, N_EXPERTS, 1), lambda i: (i, 0, 0))],
        scratch_shapes=[pltpu.VMEM((6 * B_OUT_W // LANES, tm, LANES), F32)],
        compiler_params=_cparams(1),
        name="merge_route",
    )(x2, ya, *outs, *lses, sga, sgb, mod3, norm2.reshape(1, D_MODEL), *weights)


def _plan_kernel(ridx_ref, rank_ref, tb_ref, dest_ref):
    tb = tb_ref[0]
    t = ridx_ref.shape[1]
    iota_e = lax.broadcasted_iota(I32, (N_EXPERTS, t), 0)
    for k in range(TOP_K):
        base = jnp.sum(jnp.where(iota_e == ridx_ref[k:k + 1, :], tb, 0), axis=0, keepdims=True)
        dest_ref[0, :, k * t:(k + 1) * t] = base + rank_ref[k:k + 1, :]


def _plan(ridx, rank, tile_base, tm):
    n = ridx.shape[1]
    lanes = lambda i: (0, i)
    return pl.pallas_call(
        _plan_kernel,
        out_shape=jax.ShapeDtypeStruct((n // tm, 1, TOP_K * tm), I32),
        grid=(n // tm,),
        in_specs=[pl.BlockSpec((8, tm), lanes), pl.BlockSpec((8, tm), lanes),
                  pl.BlockSpec((1, N_EXPERTS, 1), lambda i: (i, 0, 0))],
        out_specs=pl.BlockSpec((1, 1, TOP_K * tm), lambda i: (i, 0, 0)),
        compiler_params=_cparams(1),
        name="moe_plan",
    )(ridx, rank, tile_base)


def _sc_mesh():
    return plsc.VectorSubcoreMesh(core_axis_name="core", subcore_axis_name="subcore",
                                  num_cores=SC_CORES, num_subcores=SC_SUBCORES)


def _sc_worker():
    return lax.axis_index("subcore") * SC_CORES + lax.axis_index("core")


def _sc_dispatch(hp, dest_rows, n_rows):
    n = hp.shape[0]
    halves = TM_MERGE // SC_CHUNK
    workers = SC_CORES * SC_SUBCORES
    tiles_per_w = n // TM_MERGE // workers
    idx_per_w = tiles_per_w * TOP_K * halves
    assert tiles_per_w * workers * TM_MERGE == n

    @functools.partial(
        pl.kernel, mesh=_sc_mesh(),
        out_type=jax.ShapeDtypeStruct((n_rows, ROW_SLABS, LANES), U32),
        scratch_types=[pltpu.VMEM((idx_per_w, SC_CHUNK), I32),
                       pltpu.VMEM((SC_CHUNK, ROW_SLABS, LANES), U32),
                       pltpu.SemaphoreType.DMA],
        name="moe_dispatch_sc")
    def scatter(hp_hbm, idx_hbm, xs_hbm, idx_v, rows_v, sem):
        wid = _sc_worker()
        pltpu.sync_copy(idx_hbm.at[pl.ds(wid * idx_per_w, idx_per_w)], idx_v)

        @pl.loop(0, tiles_per_w * halves)
        def _(j):
            tile = j // halves
            h = j - tile * halves
            tok = (wid * tiles_per_w + tile) * TM_MERGE + h * SC_CHUNK
            pltpu.sync_copy(hp_hbm.at[pl.ds(tok, SC_CHUNK)], rows_v)
            for k in range(TOP_K):
                row = (tile * TOP_K + k) * halves + h
                pltpu.async_copy(rows_v, xs_hbm.at[idx_v.at[row]], sem).wait()

    return scatter(hp, dest_rows)


def _sc_collect(ys, dest_rows):
    n_chunks = dest_rows.shape[0]
    workers = SC_CORES * SC_SUBCORES
    per_w = n_chunks // workers
    assert per_w * workers == n_chunks

    @functools.partial(
        pl.kernel, mesh=_sc_mesh(),
        out_type=jax.ShapeDtypeStruct((n_chunks * SC_CHUNK, ROW_SLABS, LANES), U32),
        scratch_types=[pltpu.VMEM((per_w, SC_CHUNK), I32),
                       pltpu.VMEM((SC_CHUNK, ROW_SLABS, LANES), U32),
                       pltpu.SemaphoreType.DMA],
        name="moe_collect_sc")
    def gather(ys_hbm, idx_hbm, out_hbm, idx_v, rows_v, sem):
        wid = _sc_worker()
        base = wid * per_w
        pltpu.sync_copy(idx_hbm.at[pl.ds(base, per_w)], idx_v)

        @pl.loop(0, per_w)
        def _(j):
            pltpu.async_copy(ys_hbm.at[idx_v.at[j]], rows_v, sem).wait()
            pltpu.sync_copy(rows_v, out_hbm.at[pl.ds((base + j) * SC_CHUNK, SC_CHUNK)])

    return gather(ys, dest_rows)


def _expert_kernel(blk_e_ref, valid_ref, xs_ref, wgu_ref, wd_ref, ys_ref):
    del blk_e_ref
    valid = valid_ref[pl.program_id(0)]

    @pl.when(valid > 0)
    def _():
        rows = lax.broadcasted_iota(I32, (xs_ref.shape[0] // ROW_SLABS, 1), 0)
        lo, hi = _unpack_rows(jnp.where(rows < valid, _load_slabs(xs_ref), jnp.uint32(0)))
        wgu = wgu_ref[0]
        gu = jnp.dot(lo.astype(BF16), wgu[:HALF_D], preferred_element_type=F32) \
            + jnp.dot(hi.astype(BF16), wgu[HALF_D:], preferred_element_type=F32)
        g = gu[:, :D_EXPERT]
        u = gu[:, D_EXPERT:]
        act = (g * jax.nn.sigmoid(g) * u).astype(BF16)
        _store_slabs(ys_ref, _pack_rows(jnp.dot(act, wd_ref[0], preferred_element_type=F32)))

    @pl.when(valid <= 0)
    def _():
        ys_ref[...] = jnp.zeros(ys_ref.shape, U32)


def _experts(xs, blk_e, blk_valid, wgu, wd):
    n_rows = xs.shape[0] // ROW_SLABS
    bm = MOE_BM
    nblk = n_rows // bm
    return pl.pallas_call(
        _expert_kernel,
        out_shape=jax.ShapeDtypeStruct((n_rows * ROW_SLABS, LANES), U32),
        grid_spec=pltpu.PrefetchScalarGridSpec(
            num_scalar_prefetch=2,
            grid=(nblk,),
            in_specs=[pl.BlockSpec((bm * ROW_SLABS, LANES), lambda i, be, nu: (i, 0)),
                      pl.BlockSpec((1, D_MODEL, 2 * D_EXPERT), lambda i, be, nu: (be[i], 0, 0)),
                      pl.BlockSpec((1, D_EXPERT, D_MODEL), lambda i, be, nu: (be[i], 0, 0))],
            out_specs=pl.BlockSpec((bm * ROW_SLABS, LANES), lambda i, be, nu: (i, 0))),
        compiler_params=_cparams(1),
        name="moe_experts",
    )(blk_e, blk_valid, xs, wgu, wd)


def _combine_kernel(g_ref, base_ref, rw_ref, mod_ref, fn_ref, y_ref):
    tm = base_ref.shape[0]
    rw = rw_ref[...]
    acc_lo = jnp.zeros((tm, HALF_D), F32)
    acc_hi = jnp.zeros((tm, HALF_D), F32)
    for k in range(TOP_K):
        lo, hi = _unpack_rows(_load_slabs(g_ref.at[0, k]))
        w = rw[:, k:k + 1]
        acc_lo = acc_lo + w * lo
        acc_hi = acc_hi + w * hi
    routed = jnp.concatenate([acc_lo, acc_hi], axis=1)
    x2 = base_ref[...] + mod_ref[0][5:6] * routed
    y_ref[...] = _rms(x2, fn_ref[...])


def _combine(g, base, rw_t, mod3, final_norm, seq):
    n = base.shape[0]
    tm = TM_COMBINE
    assert seq % tm == 0
    g4 = g.reshape(n // tm, TOP_K, tm * ROW_SLABS, LANES)
    return pl.pallas_call(
        _combine_kernel,
        out_shape=jax.ShapeDtypeStruct((n, D_MODEL), F32),
        grid=(n // tm,),
        in_specs=[pl.BlockSpec((1, TOP_K, tm * ROW_SLABS, LANES), lambda i: (i, 0, 0, 0)),
                  pl.BlockSpec((tm, D_MODEL), lambda i: (i, 0)),
                  pl.BlockSpec((tm, 8), lambda i: (i, 0)),
                  pl.BlockSpec((1, 6, D_MODEL), lambda i: (i * tm // seq, 0, 0)),
                  pl.BlockSpec((1, D_MODEL), lambda i: (0, 0))],
        out_specs=pl.BlockSpec((tm, D_MODEL), lambda i: (i, 0)),
        compiler_params=_cparams(1),
        name="moe_combine",
    )(g4, base, rw_t, mod3, final_norm.reshape(1, D_MODEL))


def _block_layout(tile_counts, n):
    bm = MOE_BM
    c = tile_counts[:, :, 0]
    ntiles = c.shape[0]
    counts = jnp.sum(c, axis=0)
    padded = (counts + bm - 1) // bm * bm
    earlier_e = np.tri(N_EXPERTS, k=-1, dtype=bool)
    pstart = jnp.sum(jnp.where(earlier_e, padded[None, :], 0), axis=1)
    pend = pstart + padded
    earlier_t = np.tri(ntiles, k=-1, dtype=bool)
    tile_base = pstart[None, :] + jnp.sum(jnp.where(earlier_t[:, :, None], c[None], 0), axis=1)
    nblk = -(-n * TOP_K // bm) + N_EXPERTS
    blk = jnp.arange(nblk, dtype=I32)
    blk_e = jnp.minimum(jnp.sum((pend[None, :] <= blk[:, None] * bm).astype(I32), axis=1),
                        N_EXPERTS - 1)
    onehot_e = blk_e[:, None] == jnp.arange(N_EXPERTS, dtype=I32)[None, :]
    end_e = jnp.sum(jnp.where(onehot_e, (pstart + counts)[None, :], 0), axis=1)
    blk_valid = jnp.clip(end_e - blk * bm, 0, bm).astype(I32)
    return tile_base.astype(I32)[:, :, None], blk_e.astype(I32), blk_valid, nblk * bm


def _trunk(x, mod3, p):
    nbatch, seq, _ = x.shape
    n = nbatch * seq
    x2 = x.reshape(n, D_MODEL)
    proj = _inproj(x2, mod3, p["norm1"], p["w_in"], seq)
    qa = proj[0]
    sga, sgb = proj[14:]
    (ya,) = _band_attention(qa, proj[1:5], p["bias_a"], p["sink_a"], nbatch=nbatch, seq=seq, dil=1,
                            halo=A_HALF_WINDOW, kv_width=A_KV_W, want_lse=False)
    outs, lses = [], []
    for gi, (w, d) in enumerate(B_GROUPS):
        q, k, v = proj[5 + 3 * gi:8 + 3 * gi]
        o, lse = _band_attention(q, (k, v), p["bias_b"][gi], None, nbatch=nbatch, seq=seq, dil=d,
                                 halo=w // (2 * d), kv_width=B_OUT_W, want_lse=True)
        outs.append(o)
        lses.append(lse)
    base, hp, ridx, rw, rank, tile_counts = _merge(
        x2, ya, outs, lses, sga, sgb, mod3, p["norm2"], p["w_pa"], p["w_pb"], p["w_o"], p["w_rt"],
        p["rbias"], p["ws_gate"], p["ws_up"], p["ws_down"], seq)
    assert TM_MERGE == TM_COMBINE
    tile_base, blk_e, blk_valid, n_rows = _block_layout(tile_counts, n)
    dest_rows = _plan(ridx, rank, tile_base, TM_MERGE).reshape(-1, SC_CHUNK)
    slabs = lambda a: a.reshape(-1, ROW_SLABS, LANES)
    flat = lambda a: a.reshape(-1, LANES)
    xs = _sc_dispatch(slabs(hp), dest_rows, n_rows)
    ys = _experts(flat(xs), blk_e, blk_valid, p["w_gu"], p["w_down"])
    g = _sc_collect(slabs(ys), dest_rows)
    y = _combine(g, base, rw.T, mod3, p["final_norm"], seq)
    return y.reshape(nbatch, seq, D_MODEL)


def kernel(x_prompt, x_sample, c_prompt, c_sample, rel_bias, w_ada, b_ada, norm1, w_in, sink, w_pa, w_pb, w_o, norm2, w_router, router_bias, w_gate, w_up, w_down, ws_gate, ws_up, ws_down, final_norm):
    assert w_ada.shape[0] == 1
    nbp = x_prompt.shape[0]
    mod = _ada(jnp.concatenate([c_prompt, c_sample], axis=0), w_ada[0], b_ada[0])
    mod3 = mod.reshape(-1, 6, D_MODEL)

    def pair_bias(heads, halo, dist_scale):
        kb = ATTN_BLOCK + 2 * halo
        tab = _bias_table(heads, ATTN_BLOCK, kb, halo, halo, dist_scale)
        return tab.reshape(-1, 2, ATTN_BLOCK, kb).transpose(0, 2, 1, 3).reshape(-1, ATTN_BLOCK, 2 * kb)

    bias_a = pair_bias(rel_bias[:, :A_Q_HEADS], A_HALF_WINDOW, 1)
    bias_b = []
    for gi, (w, d) in enumerate(B_GROUPS):
        h0 = A_Q_HEADS + gi * B_HEADS_PER_GROUP
        bias_b.append(pair_bias(rel_bias[:, h0:h0 + B_HEADS_PER_GROUP], w // (2 * d), d))
    p = {
        "norm1": norm1[0], "norm2": norm2[0], "final_norm": final_norm,
        "w_in": w_in[0].astype(BF16),
        "bias_a": bias_a,
        "sink_a": jnp.repeat(sink[0].astype(F32), HEAD_DIM).reshape(A_Q_HEADS // 2, 1, LANES),
        "bias_b": bias_b,
        "w_pa": w_pa[0].astype(BF16), "w_pb": w_pb[0].astype(BF16), "w_o": w_o[0].astype(BF16),
        "w_rt": w_router[0].T, "rbias": router_bias[0].reshape(N_EXPERTS, 1),
        "ws_gate": ws_gate[0].astype(BF16), "ws_up": ws_up[0].astype(BF16),
        "ws_down": ws_down[0].astype(BF16),
        "w_gu": jnp.concatenate([w_gate[0], w_up[0]], axis=-1).astype(BF16),
        "w_down": w_down[0].astype(BF16),
    }
    y_prompt = _trunk(x_prompt, mod3[:nbp], p)
    y_sample = _trunk(x_sample, mod3[nbp:], p)
    return (y_prompt, y_sample)
```

```python
import functools
import math

import jax
import jax.numpy as jnp
import numpy as np
from jax import lax
from jax.experimental import pallas as pl
from jax.experimental.pallas import tpu as pltpu
from jax.experimental.pallas import tpu_sc as plsc

F32 = jnp.float32
BF16 = jnp.bfloat16
U32 = jnp.uint32
I32 = jnp.int32

D_MODEL = 1024
HEAD_DIM = 64
A_Q_HEADS = 8
A_KV_HEADS = 2
A_HALF_WINDOW = 128
B_GROUPS = ((128, 1), (512, 4), (2048, 16))
B_HEADS_PER_GROUP = 4
N_BUCKETS = 32
MAX_DISTANCE = 1024
N_EXPERTS = 64
TOP_K = 6
N_EXPERT_GROUPS = 8
TOPK_GROUPS = 4
D_EXPERT = 256
D_SHARED = 256
ROUTED_SCALE = 2.5
RMS_EPS = 1e-6
NEG_INF = -1e30
REMOVED = -3e38

A_Q_W = A_Q_HEADS * HEAD_DIM
A_KV_W = A_KV_HEADS * HEAD_DIM
B_W = len(B_GROUPS) * B_HEADS_PER_GROUP * HEAD_DIM
B_OUT_W = B_HEADS_PER_GROUP * HEAD_DIM
IN_WIDTHS = (A_Q_W, A_KV_W, A_KV_W, B_W, B_W, B_W, D_MODEL, D_MODEL)
D_IN = sum(IN_WIDTHS)
HALF_D = D_MODEL // 2
LANES = 128
ROW_SLABS = HALF_D // LANES
SC_CORES = 2
SC_SUBCORES = 16
SC_CHUNK = 64

TM_INPROJ = 512
TM_MERGE = 512
ATTN_BLOCK = 128
ATTN_ROWS = 512
MOE_BM = 512
TM_COMBINE = 512
VMEM_LIMIT = 56 * 1024 * 1024


def _cparams(n_axes):
    return pltpu.CompilerParams(
        dimension_semantics=("arbitrary",) * n_axes, vmem_limit_bytes=VMEM_LIMIT)


def _ada_kernel(c_ref, w_ref, b_ref, o_ref):
    c = c_ref[...]
    s = c * jax.nn.sigmoid(c)
    o_ref[...] = jnp.dot(s, w_ref[...], preferred_element_type=F32,
                         precision=lax.Precision.HIGHEST) + b_ref[...]


def _ada(c_all, w_ada, b_ada):
    nb = c_all.shape[0]
    return pl.pallas_call(
        _ada_kernel,
        out_shape=jax.ShapeDtypeStruct((nb, 6 * D_MODEL), F32),
        grid=(6,),
        in_specs=[pl.BlockSpec((nb, D_MODEL), lambda j: (0, 0)),
                  pl.BlockSpec((D_MODEL, D_MODEL), lambda j: (0, j)),
                  pl.BlockSpec((1, D_MODEL), lambda j: (0, j))],
        out_specs=pl.BlockSpec((nb, D_MODEL), lambda j: (0, j)),
        compiler_params=_cparams(1),
        name="ada",
    )(c_all, w_ada, b_ada.reshape(1, 6 * D_MODEL))


def _rms(x, g):
    return x * lax.rsqrt(jnp.mean(x * x, axis=-1, keepdims=True) + RMS_EPS) * g


def _inproj_kernel(x_ref, mod_ref, n1_ref, w_ref, qa, ka, ka_sw, va, va_sw, *rest):
    qkv_b = rest[:9]
    sga, sgb, scr = rest[9:]
    mod = mod_ref[0]
    h = _rms(x_ref[...], n1_ref[...]) * (1.0 + mod[1:2]) + mod[0:1]
    hb = h.astype(BF16)
    tm = hb.shape[0]
    scale = HEAD_DIM ** -0.5

    def proj(off, width):
        return jnp.dot(hb, w_ref[:, off:off + width], preferred_element_type=F32)

    qa[...] = (proj(0, A_Q_W) * scale).astype(BF16)
    for ref, ref_sw, off in ((ka, ka_sw, A_Q_W), (va, va_sw, A_Q_W + A_KV_W)):
        r = proj(off, A_KV_W)
        ref[...] = r.astype(BF16)
        ref_sw[...] = pltpu.roll(r, HEAD_DIM, 1).astype(BF16)
    off = A_Q_W + 2 * A_KV_W
    for t in range(3):
        for gi, (_, d) in enumerate(B_GROUPS):
            r = proj(off + t * B_W + gi * B_OUT_W, B_OUT_W)
            if t == 0:
                r = r * scale
            ref = qkv_b[gi * 3 + t]
            if d == 1:
                ref[...] = r.astype(BF16)
            else:
                for c in range(B_OUT_W // LANES):
                    slot = (t * 2 + gi - 1) * (B_OUT_W // LANES) + c
                    scr[slot] = r[:, c * LANES:(c + 1) * LANES]
                    for res in range(d):
                        col = res * B_OUT_W + c * LANES
                        ref[:, col:col + LANES] = scr[
                            slot, pl.ds(res, tm // d, stride=d), :].astype(BF16)
    off += 3 * B_W
    sga[...] = jax.nn.sigmoid(proj(off, D_MODEL)).astype(BF16)
    sgb[...] = jax.nn.sigmoid(proj(off + D_MODEL, D_MODEL)).astype(BF16)


def _inproj(x2, mod3, norm1, w_in_bf, seq):
    n = x2.shape[0]
    tm = TM_INPROJ
    assert seq % tm == 0 and n % tm == 0
    row = lambda i: (i, 0)
    shapes = [(n, A_Q_W, tm)] + [(n, A_KV_W, tm)] * 4
    for _, d in B_GROUPS:
        shapes += [(n // d, d * B_OUT_W, tm // d)] * 3
    shapes += [(n, D_MODEL, tm)] * 2
    return pl.pallas_call(
        _inproj_kernel,
        out_shape=[jax.ShapeDtypeStruct((r, c), BF16) for r, c, _ in shapes],
        grid=(n // tm,),
        in_specs=[pl.BlockSpec((tm, D_MODEL), row),
                  pl.BlockSpec((1, 6, D_MODEL), lambda i: (i * tm // seq, 0, 0)),
                  pl.BlockSpec((1, D_MODEL), lambda i: (0, 0)),
                  pl.BlockSpec((D_MODEL, D_IN), lambda i: (0, 0))],
        out_specs=[pl.BlockSpec((b, c), row) for _, c, b in shapes],
        scratch_shapes=[pltpu.VMEM((6 * B_OUT_W // LANES, tm, LANES), F32)],
        compiler_params=_cparams(1),
        name="inproj",
    )(x2, mod3, norm1.reshape(1, D_MODEL), w_in_bf)


def _rel_bucket_np(rel):
    half = N_BUCKETS // 2
    max_exact = half // 2
    n = np.abs(rel)
    large = max_exact + (np.log(np.maximum(n, 1) / max_exact) / math.log(MAX_DISTANCE / max_exact)
                         * (half - max_exact)).astype(np.int32)
    large = np.minimum(large, half - 1)
    return ((rel > 0).astype(np.int32) * half + np.where(n < max_exact, n, large)).astype(np.int32)


def _bias_table(rel_bias_heads, n_q, n_k, key_off, band, dist_scale):
    p = n_q + n_k
    rel = np.arange(p) - (n_q - 1) - key_off
    bucket = _rel_bucket_np(rel * dist_scale)
    t = jnp.where((np.abs(rel) <= band)[None], rel_bias_heads.astype(F32)[bucket].T, NEG_INF)
    big = jnp.tile(t, (1, n_q + 1))
    tab = big[:, n_q - 1:n_q - 1 + n_q * (p - 1)].reshape(-1, n_q, p - 1)
    return tab[:, :, :n_k]


def _pair_rhs(k_top, k_bot, v_top, v_bot):
    kb = k_top.shape[0]
    low = jnp.where(lax.broadcasted_iota(I32, (kb, LANES), 1) < HEAD_DIM, 1.0, 0.0).astype(BF16)
    high = jnp.where(lax.broadcasted_iota(I32, (kb, LANES), 1) < HEAD_DIM, 0.0, 1.0).astype(BF16)
    rhs_k = jnp.concatenate([k_top * low, k_bot * high], axis=0)
    rhs_v = jnp.concatenate([jnp.concatenate([v_top * low, low], axis=1),
                             jnp.concatenate([v_bot * high, high], axis=1)], axis=0)
    return rhs_k, rhs_v


def _pair_attention(q_pair, rhs_k, rhs_v, bias_pair, pen, sink_pair, want_lse):
    kb = rhs_k.shape[0] // 2
    s = lax.dot_general(q_pair, rhs_k, (((1,), (1,)), ((), ())), preferred_element_type=F32)
    s = s + bias_pair
    if pen is not None:
        s = s + pen
    s0, s1 = s[:, :kb], s[:, kb:]
    m0 = jnp.max(s0, axis=-1, keepdims=True)
    m1 = jnp.max(s1, axis=-1, keepdims=True)
    if sink_pair is not None:
        m0 = jnp.maximum(m0, sink_pair[:, 0:1])
        m1 = jnp.maximum(m1, sink_pair[:, HEAD_DIM:HEAD_DIM + 1])
    p = jnp.concatenate([jnp.exp(s0 - m0), jnp.exp(s1 - m1)], axis=1).astype(BF16)
    od = jnp.dot(p, rhs_v, preferred_element_type=F32)
    o, den = od[:, :LANES], od[:, LANES:]
    low = lax.broadcasted_iota(I32, o.shape, 1) < HEAD_DIM
    m_full = jnp.where(low, m0, m1)
    if sink_pair is not None:
        den = den + jnp.exp(sink_pair - m_full)
    return o / den, (m_full + jnp.log(den)) if want_lse else None


def _attn_kernel(*refs, halo, nblk, n_pairs, n_seqs, shared_kv, has_sink, want_lse, n_steps):
    it = iter(refs)
    q_ref = next(it)
    n_kv = 4 if shared_kv else 2
    kv_refs = [[next(it) for _ in range(3)] for _ in range(n_kv)]
    bias_ref = next(it)
    sink_ref = next(it) if has_sink else None
    o_ref = next(it)
    lse_ref = next(it) if want_lse else None
    step = pl.program_id(2)
    blk = ATTN_BLOCK
    kb = blk + 2 * halo
    cats = [jnp.concatenate([r[...] for r in trio], axis=0) for trio in kv_refs]
    col = lax.broadcasted_iota(I32, (1, kb), 1)
    for b in range(nblk):
        lo = jnp.where(step == 0, halo, 0) if b == 0 else 0
        hi = jnp.where(step == n_steps - 1, kb - halo, kb) if b == nblk - 1 else kb
        pen = None
        if b == 0 or b == nblk - 1:
            pen1 = jnp.where((col >= lo) & (col < hi), 0.0, NEG_INF)
            pen = jnp.concatenate([pen1, pen1], axis=1)
        rows = slice(b * blk, (b + 1) * blk)
        krows = slice(b * blk, b * blk + kb)
        if shared_kv:
            k, k_sw, v, v_sw = (c[krows] for c in cats)
            rhs = [_pair_rhs(k, k_sw, v, v_sw), _pair_rhs(k_sw, k, v_sw, v)]
        for c in range(n_pairs * n_seqs):
            lanes = slice(c * LANES, (c + 1) * LANES)
            if shared_kv:
                rhs_k, rhs_v = rhs[c // (n_pairs // 2)]
            else:
                k, v = cats[0][krows, lanes], cats[1][krows, lanes]
                rhs_k, rhs_v = _pair_rhs(k, k, v, v)
            o, lse = _pair_attention(q_ref[rows, lanes], rhs_k, rhs_v, bias_ref[c % n_pairs], pen,
                                     sink_ref[c] if has_sink else None, want_lse)
            o_ref[rows, lanes] = o.astype(o_ref.dtype)
            if want_lse:
                lse_ref[rows, lanes] = lse


def _band_attention(q, kvs, bias, sink, *, nbatch, seq, dil, halo, kv_width, want_lse):
    n = nbatch * seq
    sub_len = seq // dil
    qw = q.shape[1] // dil
    rows = min(ATTN_ROWS, sub_len)
    n_seqs = min(dil, ATTN_ROWS // rows)
    assert sub_len % rows == 0 and rows % ATTN_BLOCK == 0 and rows % halo == 0 and dil % n_seqs == 0
    nq = sub_len // rows
    per = rows // halo
    total_halos = n // dil // halo
    cur = lambda b, r, i: (b * nq + i, r)
    prev = lambda b, r, i: (jnp.maximum((b * nq + i) * per - 1, 0), r)
    nxt = lambda b, r, i: (jnp.minimum((b * nq + i + 1) * per, total_halos - 1), r)
    const = lambda a: pl.BlockSpec(a.shape, lambda b, r, i: (0,) * a.ndim)
    qw_step, kvw_step = qw * n_seqs, kv_width * n_seqs

    in_specs = [pl.BlockSpec((rows, qw_step), cur)]
    args = [q]
    for a in kvs:
        in_specs += [pl.BlockSpec((halo, kvw_step), prev), pl.BlockSpec((rows, kvw_step), cur),
                     pl.BlockSpec((halo, kvw_step), nxt)]
        args += [a, a, a]
    in_specs.append(const(bias))
    args.append(bias)
    if sink is not None:
        in_specs.append(const(sink))
        args.append(sink)
    out_shape = [jax.ShapeDtypeStruct(q.shape, BF16)]
    out_specs = [pl.BlockSpec((rows, qw_step), cur)]
    if want_lse:
        out_shape.append(jax.ShapeDtypeStruct(q.shape, F32))
        out_specs.append(pl.BlockSpec((rows, qw_step), cur))
    return pl.pallas_call(
        functools.partial(_attn_kernel, halo=halo, nblk=rows // ATTN_BLOCK, n_pairs=qw // LANES,
                          n_seqs=n_seqs, shared_kv=len(kvs) == 4, has_sink=sink is not None,
                          want_lse=want_lse, n_steps=nq),
        out_shape=out_shape,
        grid=(nbatch, dil // n_seqs, nq),
        in_specs=in_specs,
        out_specs=out_specs,
        compiler_params=_cparams(3),
        name=f"band_attn_d{dil}",
    )(*args)


def _pack_rows(y):
    lo = pltpu.bitcast(y[:, :HALF_D].astype(BF16).astype(F32), U32)
    hi = pltpu.bitcast(y[:, HALF_D:].astype(BF16).astype(F32), U32)
    return (hi & jnp.uint32(0xFFFF0000)) | (lo >> 16)


def _store_slabs(ref, packed):
    t = packed.shape[0]
    for c in range(ROW_SLABS):
        ref[pl.ds(c, t, stride=ROW_SLABS), :] = packed[:, c * LANES:(c + 1) * LANES]


def _load_slabs(ref):
    t = ref.shape[0] // ROW_SLABS
    return jnp.concatenate([ref[pl.ds(c, t, stride=ROW_SLABS), :] for c in range(ROW_SLABS)], axis=1)


def _unpack_rows(p):
    lo = pltpu.bitcast(p << 16, F32)
    hi = pltpu.bitcast(p & jnp.uint32(0xFFFF0000), F32)
    return lo, hi


def _route(sel, scores):
    t = sel.shape[-1]
    per = N_EXPERTS // N_EXPERT_GROUPS
    shape3 = (N_EXPERT_GROUPS, per, t)
    sel3 = sel.reshape(shape3)
    sc3 = scores.reshape(shape3)
    iota_g = lax.broadcasted_iota(I32, shape3, 0)
    iota_m = lax.broadcasted_iota(I32, shape3, 1)
    iota_e = iota_g * per + iota_m
    m1 = jnp.max(sel3, axis=1, keepdims=True)
    i1 = jnp.min(jnp.where(sel3 == m1, iota_m, per), axis=1, keepdims=True)
    m2 = jnp.max(jnp.where(iota_m == i1, REMOVED, sel3), axis=1, keepdims=True)
    gscore = m1 + m2
    iota_g1 = lax.broadcasted_iota(I32, gscore.shape, 0)
    gmask = jnp.zeros(gscore.shape, jnp.bool_)
    for _ in range(TOPK_GROUPS):
        mx = jnp.max(gscore, axis=0, keepdims=True)
        ix = jnp.min(jnp.where(gscore == mx, iota_g1, N_EXPERT_GROUPS), axis=0, keepdims=True)
        hit = iota_g1 == ix
        gmask = gmask | hit
        gscore = jnp.where(hit, REMOVED, gscore)
    cur = jnp.where(gmask, sel3, NEG_INF)
    sum_all = lambda a: jnp.sum(jnp.sum(a, axis=1, keepdims=True), axis=0, keepdims=True)
    idxs, wts, hits = [], [], []
    for _ in range(TOP_K):
        mx = jnp.max(jnp.max(cur, axis=1, keepdims=True), axis=0, keepdims=True)
        cand = jnp.where(cur == mx, iota_e, N_EXPERTS)
        ix = jnp.min(jnp.min(cand, axis=1, keepdims=True), axis=0, keepdims=True)
        hit = iota_e == ix
        wts.append(sum_all(jnp.where(hit, sc3, 0.0)).reshape(1, t))
        cur = jnp.where(hit, REMOVED, cur)
        idxs.append(ix.reshape(1, t))
        hits.append(hit)
    wsum = wts[0]
    for w in wts[1:]:
        wsum = wsum + w
    wts = [w / wsum * ROUTED_SCALE for w in wts]
    onehot = hits[0].astype(F32)
    for hit in hits[1:]:
        onehot = onehot + hit.astype(F32)
    tri = (lax.broadcasted_iota(I32, (t, t), 0) <= lax.broadcasted_iota(I32, (t, t), 1))
    cum = jnp.dot(onehot.reshape(N_EXPERTS, t).astype(BF16), tri.astype(F32).astype(BF16),
                  preferred_element_type=F32)
    cum3 = cum.reshape(shape3) - 1.0
    ranks = [sum_all(jnp.where(hit, cum3, 0.0)).reshape(1, t).astype(I32) for hit in hits]
    counts = cum[:, t - 1:t].astype(I32)
    return idxs, wts, ranks, counts


def _merge_kernel(x_ref, ya_ref, o1, o2, o3, l1, l2, l3, sga_ref, sgb_ref, mod_ref, n2_ref,
                  wpa, wpb, wo, wrt, rbias, wsg, wsu, wsd,
                  base_ref, hp_ref, ridx_ref, rw_ref, rank_ref, cnt_ref, scr):
    mod = mod_ref[0]
    tm = x_ref.shape[0]

    def token_major(ref, slot, d):
        if d == 1:
            return ref[...].astype(F32)
        nc = B_OUT_W // LANES
        for c in range(nc):
            for res in range(d):
                col = res * B_OUT_W + c * LANES
                scr[slot * nc + c, pl.ds(res, tm // d, stride=d), :] = ref[
                    :, col:col + LANES].astype(F32)
        return jnp.concatenate([scr[slot * nc + c] for c in range(nc)], axis=1)

    dils = [d for _, d in B_GROUPS]
    os_ = [token_major(r, i, d) for i, (r, d) in enumerate(zip((o1, o2, o3), dils))]
    ls = [token_major(r, 3 + i, d) for i, (r, d) in enumerate(zip((l1, l2, l3), dils))]
    mx = jnp.maximum(jnp.maximum(ls[0], ls[1]), ls[2])
    es = [jnp.exp(l - mx) for l in ls]
    den = es[0] + es[1] + es[2]
    ob = (es[0] / den) * os_[0] + (es[1] / den) * os_[1] + (es[2] / den) * os_[2]
    pa = jnp.dot(ya_ref[...], wpa[...], preferred_element_type=F32)
    pb = jnp.dot(ob.astype(BF16), wpb[...], preferred_element_type=F32)
    merged = sga_ref[...].astype(F32) * pa + sgb_ref[...].astype(F32) * pb
    mix = jnp.dot(merged.astype(BF16), wo[...], preferred_element_type=F32)
    x1 = x_ref[...] + mod[2:3] * mix
    h2 = _rms(x1, n2_ref[...]) * (1.0 + mod[4:5]) + mod[3:4]
    logits = lax.dot_general(wrt[...], h2, (((1,), (1,)), ((), ())), preferred_element_type=F32,
                             precision=lax.Precision.HIGHEST)
    scores = jax.nn.sigmoid(logits)
    idxs, wts, ranks, counts = _route(scores + rbias[...], scores)
    for k in range(TOP_K):
        ridx_ref[k:k + 1, :] = idxs[k]
        rw_ref[k:k + 1, :] = wts[k]
        rank_ref[k:k + 1, :] = ranks[k]
    ridx_ref[TOP_K:, :] = jnp.zeros((8 - TOP_K, tm), I32)
    rw_ref[TOP_K:, :] = jnp.zeros((8 - TOP_K, tm), F32)
    rank_ref[TOP_K:, :] = jnp.zeros((8 - TOP_K, tm), I32)
    cnt_ref[0] = counts
    hb = h2.astype(BF16)
    g = jnp.dot(hb, wsg[...], preferred_element_type=F32)
    u = jnp.dot(hb, wsu[...], preferred_element_type=F32)
    act = (g * jax.nn.sigmoid(g) * u).astype(BF16)
    shared = jnp.dot(act, wsd[...], preferred_element_type=F32)
    base_ref[...] = x1 + mod[5:6] * shared
    _store_slabs(hp_ref, _pack_rows(h2))


def _merge(x2, ya, outs, lses, sga, sgb, mod3, norm2, wpa, wpb, wo, wrt, rbias, wsg, wsu, wsd, seq):
    n = x2.shape[0]
    tm = TM_MERGE
    assert seq % tm == 0
    row = lambda i: (i, 0)
    full = lambda a: pl.BlockSpec(a.shape, lambda i: (0,) * a.ndim)
    weights = [wpa, wpb, wo, wrt, rbias, wsg, wsu, wsd]
    group_specs = [pl.BlockSpec((tm // d, d * B_OUT_W), row) for _, d in B_GROUPS]
    lanes = lambda i: (0, i)
    return pl.pallas_call(
        _merge_kernel,
        out_shape=[jax.ShapeDtypeStruct((n, D_MODEL), F32),
                   jax.ShapeDtypeStruct((n * ROW_SLABS, LANES), U32),
                   jax.ShapeDtypeStruct((8, n), I32),
                   jax.ShapeDtypeStruct((8, n), F32),
                   jax.ShapeDtypeStruct((8, n), I32),
                   jax.ShapeDtypeStruct((n // tm, N_EXPERTS, 1), I32)],
        grid=(n // tm,),
        in_specs=[pl.BlockSpec((tm, D_MODEL), row), pl.BlockSpec((tm, A_Q_W), row)]
                 + group_specs * 2
                 + [pl.BlockSpec((tm, D_MODEL), row)] * 2
                 + [pl.BlockSpec((1, 6, D_MODEL), lambda i: (i * tm // seq, 0, 0)),
                    pl.BlockSpec((1, D_MODEL), lambda i: (0, 0))]
                 + [full(w) for w in weights],
        out_specs=[pl.BlockSpec((tm, D_MODEL), row),
                   pl.BlockSpec((tm * ROW_SLABS, LANES), row),
                   pl.BlockSpec((8, tm), lanes), pl.BlockSpec((8, tm), lanes),
                   pl.BlockSpec((8, tm), lanes),
                   pl.BlockSpec((1, N_EXPERTS, 1), lambda i: (i, 0, 0))],
        scratch_shapes=[pltpu.VMEM((6 * B_OUT_W // LANES, tm, LANES), F32)],
        compiler_params=_cparams(1),
        name="merge_route",
    )(x2, ya, *outs, *lses, sga, sgb, mod3, norm2.reshape(1, D_MODEL), *weights)


def _plan_kernel(ridx_ref, rank_ref, tb_ref, dest_ref):
    tb = tb_ref[0]
    t = ridx_ref.shape[1]
    iota_e = lax.broadcasted_iota(I32, (N_EXPERTS, t), 0)
    for k in range(TOP_K):
        base = jnp.sum(jnp.where(iota_e == ridx_ref[k:k + 1, :], tb, 0), axis=0, keepdims=True)
        dest_ref[0, :, k * t:(k + 1) * t] = base + rank_ref[k:k + 1, :]


def _plan(ridx, rank, tile_base, tm):
    n = ridx.shape[1]
    lanes = lambda i: (0, i)
    return pl.pallas_call(
        _plan_kernel,
        out_shape=jax.ShapeDtypeStruct((n // tm, 1, TOP_K * tm), I32),
        grid=(n // tm,),
        in_specs=[pl.BlockSpec((8, tm), lanes), pl.BlockSpec((8, tm), lanes),
                  pl.BlockSpec((1, N_EXPERTS, 1), lambda i: (i, 0, 0))],
        out_specs=pl.BlockSpec((1, 1, TOP_K * tm), lambda i: (i, 0, 0)),
        compiler_params=_cparams(1),
        name="moe_plan",
    )(ridx, rank, tile_base)


def _sc_mesh():
    return plsc.VectorSubcoreMesh(core_axis_name="core", subcore_axis_name="subcore",
                                  num_cores=SC_CORES, num_subcores=SC_SUBCORES)


def _sc_worker():
    return lax.axis_index("subcore") * SC_CORES + lax.axis_index("core")


def _sc_dispatch(hp, dest_rows, n_rows):
    n = hp.shape[0]
    halves = TM_MERGE // SC_CHUNK
    workers = SC_CORES * SC_SUBCORES
    tiles_per_w = n // TM_MERGE // workers
    idx_per_w = tiles_per_w * TOP_K * halves
    chunks = tiles_per_w * halves
    assert tiles_per_w * workers * TM_MERGE == n and chunks % 2 == 0
    rows_buf = pltpu.VMEM((SC_CHUNK, ROW_SLABS, LANES), U32)

    @functools.partial(
        pl.kernel, mesh=_sc_mesh(),
        out_type=jax.ShapeDtypeStruct((n_rows, ROW_SLABS, LANES), U32),
        scratch_types=[pltpu.VMEM((idx_per_w, SC_CHUNK), I32), rows_buf, rows_buf,
                       pltpu.SemaphoreType.DMA, pltpu.SemaphoreType.DMA, pltpu.SemaphoreType.DMA],
        name="moe_dispatch_sc")
    def scatter(hp_hbm, idx_hbm, xs_hbm, idx_v, rows_a, rows_b, sem_a, sem_b, sem_s):
        wid = _sc_worker()
        pltpu.sync_copy(idx_hbm.at[pl.ds(wid * idx_per_w, idx_per_w)], idx_v)

        def load(j, buf, sem):
            tile = j // halves
            tok = (wid * tiles_per_w + tile) * TM_MERGE + (j - tile * halves) * SC_CHUNK
            return pltpu.make_async_copy(hp_hbm.at[pl.ds(tok, SC_CHUNK)], buf, sem)

        def scatter_all(j, buf):
            tile = j // halves
            h = j - tile * halves
            copies = [pltpu.make_async_copy(
                buf, xs_hbm.at[idx_v.at[(tile * TOP_K + k) * halves + h]], sem_s)
                for k in range(TOP_K)]
            for c in copies:
                c.start()
            for c in copies:
                c.wait()

        load(0, rows_a, sem_a).start()

        @pl.loop(0, chunks, step=2)
        def _(j):
            load(j + 1, rows_b, sem_b).start()
            load(j, rows_a, sem_a).wait()
            scatter_all(j, rows_a)

            @pl.when(j + 2 < chunks)
            def _():
                load(j + 2, rows_a, sem_a).start()

            load(j + 1, rows_b, sem_b).wait()
            scatter_all(j + 1, rows_b)

    return scatter(hp, dest_rows)


def _sc_collect(ys, dest_rows):
    n_chunks = dest_rows.shape[0]
    workers = SC_CORES * SC_SUBCORES
    per_w = n_chunks // workers
    assert per_w * workers == n_chunks and per_w % 2 == 0
    rows_buf = pltpu.VMEM((SC_CHUNK, ROW_SLABS, LANES), U32)

    @functools.partial(
        pl.kernel, mesh=_sc_mesh(),
        out_type=jax.ShapeDtypeStruct((n_chunks * SC_CHUNK, ROW_SLABS, LANES), U32),
        scratch_types=[pltpu.VMEM((per_w, SC_CHUNK), I32), rows_buf, rows_buf,
                       pltpu.SemaphoreType.DMA, pltpu.SemaphoreType.DMA],
        name="moe_collect_sc")
    def gather(ys_hbm, idx_hbm, out_hbm, idx_v, rows_a, rows_b, sem_a, sem_b):
        wid = _sc_worker()
        base = wid * per_w
        pltpu.sync_copy(idx_hbm.at[pl.ds(base, per_w)], idx_v)

        def fetch(j, buf, sem):
            return pltpu.make_async_copy(ys_hbm.at[idx_v.at[j]], buf, sem)

        def emit(j, buf):
            pltpu.sync_copy(buf, out_hbm.at[pl.ds((base + j) * SC_CHUNK, SC_CHUNK)])

        fetch(0, rows_a, sem_a).start()

        @pl.loop(0, per_w, step=2)
        def _(j):
            fetch(j + 1, rows_b, sem_b).start()
            fetch(j, rows_a, sem_a).wait()
            emit(j, rows_a)

            @pl.when(j + 2 < per_w)
            def _():
                fetch(j + 2, rows_a, sem_a).start()

            fetch(j + 1, rows_b, sem_b).wait()
            emit(j + 1, rows_b)

    return gather(ys, dest_rows)


def _expert_kernel(blk_e_ref, valid_ref, xs_ref, wgu_ref, wd_ref, ys_ref):
    del blk_e_ref
    valid = valid_ref[pl.program_id(0)]

    @pl.when(valid > 0)
    def _():
        rows = lax.broadcasted_iota(I32, (xs_ref.shape[0] // ROW_SLABS, 1), 0)
        lo, hi = _unpack_rows(jnp.where(rows < valid, _load_slabs(xs_ref), jnp.uint32(0)))
        wgu = wgu_ref[0]
        gu = jnp.dot(lo.astype(BF16), wgu[:HALF_D], preferred_element_type=F32) \
            + jnp.dot(hi.astype(BF16), wgu[HALF_D:], preferred_element_type=F32)
        g = gu[:, :D_EXPERT]
        u = gu[:, D_EXPERT:]
        act = (g * jax.nn.sigmoid(g) * u).astype(BF16)
        _store_slabs(ys_ref, _pack_rows(jnp.dot(act, wd_ref[0], preferred_element_type=F32)))

    @pl.when(valid <= 0)
    def _():
        ys_ref[...] = jnp.zeros(ys_ref.shape, U32)


def _experts(xs, blk_e, blk_valid, wgu, wd):
    n_rows = xs.shape[0] // ROW_SLABS
    bm = MOE_BM
    nblk = n_rows // bm
    return pl.pallas_call(
        _expert_kernel,
        out_shape=jax.ShapeDtypeStruct((n_rows * ROW_SLABS, LANES), U32),
        grid_spec=pltpu.PrefetchScalarGridSpec(
            num_scalar_prefetch=2,
            grid=(nblk,),
            in_specs=[pl.BlockSpec((bm * ROW_SLABS, LANES), lambda i, be, nu: (i, 0)),
                      pl.BlockSpec((1, D_MODEL, 2 * D_EXPERT), lambda i, be, nu: (be[i], 0, 0)),
                      pl.BlockSpec((1, D_EXPERT, D_MODEL), lambda i, be, nu: (be[i], 0, 0))],
            out_specs=pl.BlockSpec((bm * ROW_SLABS, LANES), lambda i, be, nu: (i, 0))),
        compiler_params=_cparams(1),
        name="moe_experts",
    )(blk_e, blk_valid, xs, wgu, wd)


def _combine_kernel(g_ref, base_ref, rw_ref, mod_ref, fn_ref, y_ref):
    tm = base_ref.shape[0]
    rw = rw_ref[...]
    acc_lo = jnp.zeros((tm, HALF_D), F32)
    acc_hi = jnp.zeros((tm, HALF_D), F32)
    for k in range(TOP_K):
        lo, hi = _unpack_rows(_load_slabs(g_ref.at[0, k]))
        w = rw[:, k:k + 1]
        acc_lo = acc_lo + w * lo
        acc_hi = acc_hi + w * hi
    routed = jnp.concatenate([acc_lo, acc_hi], axis=1)
    x2 = base_ref[...] + mod_ref[0][5:6] * routed
    y_ref[...] = _rms(x2, fn_ref[...])


def _combine(g, base, rw_t, mod3, final_norm, seq):
    n = base.shape[0]
    tm = TM_COMBINE
    assert seq % tm == 0
    g4 = g.reshape(n // tm, TOP_K, tm * ROW_SLABS, LANES)
    return pl.pallas_call(
        _combine_kernel,
        out_shape=jax.ShapeDtypeStruct((n, D_MODEL), F32),
        grid=(n // tm,),
        in_specs=[pl.BlockSpec((1, TOP_K, tm * ROW_SLABS, LANES), lambda i: (i, 0, 0, 0)),
                  pl.BlockSpec((tm, D_MODEL), lambda i: (i, 0)),
                  pl.BlockSpec((tm, 8), lambda i: (i, 0)),
                  pl.BlockSpec((1, 6, D_MODEL), lambda i: (i * tm // seq, 0, 0)),
                  pl.BlockSpec((1, D_MODEL), lambda i: (0, 0))],
        out_specs=pl.BlockSpec((tm, D_MODEL), lambda i: (i, 0)),
        compiler_params=_cparams(1),
        name="moe_combine",
    )(g4, base, rw_t, mod3, final_norm.reshape(1, D_MODEL))


def _block_layout(tile_counts, n):
    bm = MOE_BM
    c = tile_counts[:, :, 0]
    ntiles = c.shape[0]
    counts = jnp.sum(c, axis=0)
    padded = (counts + bm - 1) // bm * bm
    earlier_e = np.tri(N_EXPERTS, k=-1, dtype=bool)
    pstart = jnp.sum(jnp.where(earlier_e, padded[None, :], 0), axis=1)
    pend = pstart + padded
    earlier_t = np.tri(ntiles, k=-1, dtype=bool)
    tile_base = pstart[None, :] + jnp.sum(jnp.where(earlier_t[:, :, None], c[None], 0), axis=1)
    nblk = -(-n * TOP_K // bm) + N_EXPERTS
    blk = jnp.arange(nblk, dtype=I32)
    blk_e = jnp.minimum(jnp.sum((pend[None, :] <= blk[:, None] * bm).astype(I32), axis=1),
                        N_EXPERTS - 1)
    onehot_e = blk_e[:, None] == jnp.arange(N_EXPERTS, dtype=I32)[None, :]
    end_e = jnp.sum(jnp.where(onehot_e, (pstart + counts)[None, :], 0), axis=1)
    blk_valid = jnp.clip(end_e - blk * bm, 0, bm).astype(I32)
    return tile_base.astype(I32)[:, :, None], blk_e.astype(I32), blk_valid, nblk * bm


def _trunk(x, mod3, p):
    nbatch, seq, _ = x.shape
    n = nbatch * seq
    x2 = x.reshape(n, D_MODEL)
    proj = _inproj(x2, mod3, p["norm1"], p["w_in"], seq)
    qa = proj[0]
    sga, sgb = proj[14:]
    (ya,) = _band_attention(qa, proj[1:5], p["bias_a"], p["sink_a"], nbatch=nbatch, seq=seq, dil=1,
                            halo=A_HALF_WINDOW, kv_width=A_KV_W, want_lse=False)
    outs, lses = [], []
    for gi, (w, d) in enumerate(B_GROUPS):
        q, k, v = proj[5 + 3 * gi:8 + 3 * gi]
        o, lse = _band_attention(q, (k, v), p["bias_b"][gi], None, nbatch=nbatch, seq=seq, dil=d,
                                 halo=w // (2 * d), kv_width=B_OUT_W, want_lse=True)
        outs.append(o)
        lses.append(lse)
    base, hp, ridx, rw, rank, tile_counts = _merge(
        x2, ya, outs, lses, sga, sgb, mod3, p["norm2"], p["w_pa"], p["w_pb"], p["w_o"], p["w_rt"],
        p["rbias"], p["ws_gate"], p["ws_up"], p["ws_down"], seq)
    assert TM_MERGE == TM_COMBINE
    tile_base, blk_e, blk_valid, n_rows = _block_layout(tile_counts, n)
    dest_rows = _plan(ridx, rank, tile_base, TM_MERGE).reshape(-1, SC_CHUNK)
    slabs = lambda a: a.reshape(-1, ROW_SLABS, LANES)
    flat = lambda a: a.reshape(-1, LANES)
    xs = _sc_dispatch(slabs(hp), dest_rows, n_rows)
    ys = _experts(flat(xs), blk_e, blk_valid, p["w_gu"], p["w_down"])
    g = _sc_collect(slabs(ys), dest_rows)
    y = _combine(g, base, rw.T, mod3, p["final_norm"], seq)
    return y.reshape(nbatch, seq, D_MODEL)


def kernel(x_prompt, x_sample, c_prompt, c_sample, rel_bias, w_ada, b_ada, norm1, w_in, sink, w_pa, w_pb, w_o, norm2, w_router, router_bias, w_gate, w_up, w_down, ws_gate, ws_up, ws_down, final_norm):
    assert w_ada.shape[0] == 1
    nbp = x_prompt.shape[0]
    mod = _ada(jnp.concatenate([c_prompt, c_sample], axis=0), w_ada[0], b_ada[0])
    mod3 = mod.reshape(-1, 6, D_MODEL)

    def pair_bias(heads, halo, dist_scale):
        kb = ATTN_BLOCK + 2 * halo
        tab = _bias_table(heads, ATTN_BLOCK, kb, halo, halo, dist_scale)
        return tab.reshape(-1, 2, ATTN_BLOCK, kb).transpose(0, 2, 1, 3).reshape(-1, ATTN_BLOCK, 2 * kb)

    bias_a = pair_bias(rel_bias[:, :A_Q_HEADS], A_HALF_WINDOW, 1)
    bias_b = []
    for gi, (w, d) in enumerate(B_GROUPS):
        h0 = A_Q_HEADS + gi * B_HEADS_PER_GROUP
        bias_b.append(pair_bias(rel_bias[:, h0:h0 + B_HEADS_PER_GROUP], w // (2 * d), d))
    p = {
        "norm1": norm1[0], "norm2": norm2[0], "final_norm": final_norm,
        "w_in": w_in[0].astype(BF16),
        "bias_a": bias_a,
        "sink_a": jnp.repeat(sink[0].astype(F32), HEAD_DIM).reshape(A_Q_HEADS // 2, 1, LANES),
        "bias_b": bias_b,
        "w_pa": w_pa[0].astype(BF16), "w_pb": w_pb[0].astype(BF16), "w_o": w_o[0].astype(BF16),
        "w_rt": w_router[0].T, "rbias": router_bias[0].reshape(N_EXPERTS, 1),
        "ws_gate": ws_gate[0].astype(BF16), "ws_up": ws_up[0].astype(BF16),
        "ws_down": ws_down[0].astype(BF16),
        "w_gu": jnp.concatenate([w_gate[0], w_up[0]], axis=-1).astype(BF16),
        "w_down": w_down[0].astype(BF16),
    }
    y_prompt = _trunk(x_prompt, mod3[:nbp], p)
    y_sample = _trunk(x_sample, mod3[nbp:], p)
    return (y_prompt, y_sample)
```

```python
import functools
import math

import jax
import jax.numpy as jnp
import numpy as np
from jax import lax
from jax.experimental import pallas as pl
from jax.experimental.pallas import tpu as pltpu
from jax.experimental.pallas import tpu_sc as plsc

F32 = jnp.float32
BF16 = jnp.bfloat16
U32 = jnp.uint32
I32 = jnp.int32

D_MODEL = 1024
HEAD_DIM = 64
A_Q_HEADS = 8
A_KV_HEADS = 2
A_HALF_WINDOW = 128
B_GROUPS = ((128, 1), (512, 4), (2048, 16))
B_HEADS_PER_GROUP = 4
N_BUCKETS = 32
MAX_DISTANCE = 1024
N_EXPERTS = 64
TOP_K = 6
N_EXPERT_GROUPS = 8
TOPK_GROUPS = 4
D_EXPERT = 256
D_SHARED = 256
ROUTED_SCALE = 2.5
RMS_EPS = 1e-6
NEG_INF = -1e30
REMOVED = -3e38

A_Q_W = A_Q_HEADS * HEAD_DIM
A_KV_W = A_KV_HEADS * HEAD_DIM
B_W = len(B_GROUPS) * B_HEADS_PER_GROUP * HEAD_DIM
B_OUT_W = B_HEADS_PER_GROUP * HEAD_DIM
IN_WIDTHS = (A_Q_W, A_KV_W, A_KV_W, B_W, B_W, B_W, D_MODEL, D_MODEL)
D_IN = sum(IN_WIDTHS)
HALF_D = D_MODEL // 2
LANES = 128
ROW_SLABS = HALF_D // LANES
SC_CORES = 2
SC_SUBCORES = 16
SC_CHUNK = 128

TM_INPROJ = 512
TM_MERGE = 512
ATTN_BLOCK = 128
ATTN_ROWS = 512
MOE_BM = 512
TM_COMBINE = 512
VMEM_LIMIT = 56 * 1024 * 1024


def _cparams(n_axes):
    return pltpu.CompilerParams(
        dimension_semantics=("arbitrary",) * n_axes, vmem_limit_bytes=VMEM_LIMIT)


def _ada_kernel(c_ref, w_ref, b_ref, o_ref):
    c = c_ref[...]
    s = c * jax.nn.sigmoid(c)
    o_ref[...] = jnp.dot(s, w_ref[...], preferred_element_type=F32,
                         precision=lax.Precision.HIGHEST) + b_ref[...]


def _ada(c_all, w_ada, b_ada):
    nb = c_all.shape[0]
    return pl.pallas_call(
        _ada_kernel,
        out_shape=jax.ShapeDtypeStruct((nb, 6 * D_MODEL), F32),
        grid=(6,),
        in_specs=[pl.BlockSpec((nb, D_MODEL), lambda j: (0, 0)),
                  pl.BlockSpec((D_MODEL, D_MODEL), lambda j: (0, j)),
                  pl.BlockSpec((1, D_MODEL), lambda j: (0, j))],
        out_specs=pl.BlockSpec((nb, D_MODEL), lambda j: (0, j)),
        compiler_params=_cparams(1),
        name="ada",
    )(c_all, w_ada, b_ada.reshape(1, 6 * D_MODEL))


def _rms(x, g):
    return x * lax.rsqrt(jnp.mean(x * x, axis=-1, keepdims=True) + RMS_EPS) * g


def _inproj_kernel(x_ref, mod_ref, n1_ref, w_ref, qa, kva, *rest):
    q_b = rest[0:6:2]
    kv_b = rest[1:6:2]
    sga, sgb, scr = rest[6:]
    mod = mod_ref[0]
    h = _rms(x_ref[...], n1_ref[...]) * (1.0 + mod[1:2]) + mod[0:1]
    hb = h.astype(BF16)
    tm = hb.shape[0]
    scale = HEAD_DIM ** -0.5

    def proj(off, width):
        return jnp.dot(hb, w_ref[:, off:off + width], preferred_element_type=F32)

    qa[...] = (proj(0, A_Q_W) * scale).astype(BF16)
    for j, off in enumerate((A_Q_W, A_Q_W + A_KV_W)):
        r = proj(off, A_KV_W)
        kva[:, 2 * j * A_KV_W:(2 * j + 1) * A_KV_W] = r.astype(BF16)
        kva[:, (2 * j + 1) * A_KV_W:(2 * j + 2) * A_KV_W] = pltpu.roll(r, HEAD_DIM, 1).astype(BF16)
    off = A_Q_W + 2 * A_KV_W
    for t in range(3):
        for gi, (_, d) in enumerate(B_GROUPS):
            r = proj(off + t * B_W + gi * B_OUT_W, B_OUT_W)
            if t == 0:
                r = r * scale
            ref, width, base = (q_b[gi], B_OUT_W, 0) if t == 0 else (kv_b[gi], 2 * B_OUT_W,
                                                                     (t - 1) * B_OUT_W)
            if d == 1:
                ref[:, base:base + B_OUT_W] = r.astype(BF16)
            else:
                for c in range(B_OUT_W // LANES):
                    slot = (t * 2 + gi - 1) * (B_OUT_W // LANES) + c
                    scr[slot] = r[:, c * LANES:(c + 1) * LANES]
                    for res in range(d):
                        col = res * width + base + c * LANES
                        ref[:, col:col + LANES] = scr[
                            slot, pl.ds(res, tm // d, stride=d), :].astype(BF16)
    off += 3 * B_W
    sga[...] = jax.nn.sigmoid(proj(off, D_MODEL)).astype(BF16)
    sgb[...] = jax.nn.sigmoid(proj(off + D_MODEL, D_MODEL)).astype(BF16)


def _inproj(x2, mod3, norm1, w_in_bf, seq):
    n = x2.shape[0]
    tm = TM_INPROJ
    assert seq % tm == 0 and n % tm == 0
    row = lambda i: (i, 0)
    shapes = [(n, A_Q_W, tm), (n, 4 * A_KV_W, tm)]
    for _, d in B_GROUPS:
        shapes += [(n // d, d * B_OUT_W, tm // d), (n // d, d * 2 * B_OUT_W, tm // d)]
    shapes += [(n, D_MODEL, tm)] * 2
    return pl.pallas_call(
        _inproj_kernel,
        out_shape=[jax.ShapeDtypeStruct((r, c), BF16) for r, c, _ in shapes],
        grid=(n // tm,),
        in_specs=[pl.BlockSpec((tm, D_MODEL), row),
                  pl.BlockSpec((1, 6, D_MODEL), lambda i: (i * tm // seq, 0, 0)),
                  pl.BlockSpec((1, D_MODEL), lambda i: (0, 0)),
                  pl.BlockSpec((D_MODEL, D_IN), lambda i: (0, 0))],
        out_specs=[pl.BlockSpec((b, c), row) for _, c, b in shapes],
        scratch_shapes=[pltpu.VMEM((6 * B_OUT_W // LANES, tm, LANES), F32)],
        compiler_params=_cparams(1),
        name="inproj",
    )(x2, mod3, norm1.reshape(1, D_MODEL), w_in_bf)


def _rel_bucket_np(rel):
    half = N_BUCKETS // 2
    max_exact = half // 2
    n = np.abs(rel)
    large = max_exact + (np.log(np.maximum(n, 1) / max_exact) / math.log(MAX_DISTANCE / max_exact)
                         * (half - max_exact)).astype(np.int32)
    large = np.minimum(large, half - 1)
    return ((rel > 0).astype(np.int32) * half + np.where(n < max_exact, n, large)).astype(np.int32)


def _bias_table(rel_bias_heads, n_q, n_k, key_off, band, dist_scale):
    p = n_q + n_k
    rel = np.arange(p) - (n_q - 1) - key_off
    bucket = _rel_bucket_np(rel * dist_scale)
    t = jnp.where((np.abs(rel) <= band)[None], rel_bias_heads.astype(F32)[bucket].T, NEG_INF)
    big = jnp.tile(t, (1, n_q + 1))
    tab = big[:, n_q - 1:n_q - 1 + n_q * (p - 1)].reshape(-1, n_q, p - 1)
    return tab[:, :, :n_k]


def _pair_rhs(k_top, k_bot, v_top, v_bot):
    kb = k_top.shape[0]
    low = jnp.where(lax.broadcasted_iota(I32, (kb, LANES), 1) < HEAD_DIM, 1.0, 0.0).astype(BF16)
    high = jnp.where(lax.broadcasted_iota(I32, (kb, LANES), 1) < HEAD_DIM, 0.0, 1.0).astype(BF16)
    rhs_k = jnp.concatenate([k_top * low, k_bot * high], axis=0)
    rhs_v = jnp.concatenate([jnp.concatenate([v_top * low, low], axis=1),
                             jnp.concatenate([v_bot * high, high], axis=1)], axis=0)
    return rhs_k, rhs_v


def _pair_attention(q_pair, rhs_k, rhs_v, bias_pair, pen, sink_pair, want_lse):
    kb = rhs_k.shape[0] // 2
    s = lax.dot_general(q_pair, rhs_k, (((1,), (1,)), ((), ())), preferred_element_type=F32)
    s = s + bias_pair
    if pen is not None:
        s = s + pen
    s0, s1 = s[:, :kb], s[:, kb:]
    m0 = jnp.max(s0, axis=-1, keepdims=True)
    m1 = jnp.max(s1, axis=-1, keepdims=True)
    if sink_pair is not None:
        m0 = jnp.maximum(m0, sink_pair[:, 0:1])
        m1 = jnp.maximum(m1, sink_pair[:, HEAD_DIM:HEAD_DIM + 1])
    p = jnp.concatenate([jnp.exp(s0 - m0), jnp.exp(s1 - m1)], axis=1).astype(BF16)
    od = jnp.dot(p, rhs_v, preferred_element_type=F32)
    o, den = od[:, :LANES], od[:, LANES:]
    low = lax.broadcasted_iota(I32, o.shape, 1) < HEAD_DIM
    m_full = jnp.where(low, m0, m1)
    if sink_pair is not None:
        den = den + jnp.exp(sink_pair - m_full)
    return o / den, (m_full + jnp.log(den)) if want_lse else None


def _attn_kernel(*refs, halo, nblk, n_pairs, n_seqs, shared_kv, has_sink, want_lse, n_steps):
    it = iter(refs)
    q_ref = next(it)
    kv_prev, kv_cur, kv_next = next(it), next(it), next(it)
    bias_ref = next(it)
    sink_ref = next(it) if has_sink else None
    o_ref = next(it)
    lse_ref = next(it) if want_lse else None
    step = pl.program_id(2)
    blk = ATTN_BLOCK
    kb = blk + 2 * halo
    qw = n_pairs * LANES
    kv = jnp.concatenate([kv_prev[...], kv_cur[...], kv_next[...]], axis=0)
    col = lax.broadcasted_iota(I32, (1, kb), 1)
    for b in range(nblk):
        lo = jnp.where(step == 0, halo, 0) if b == 0 else 0
        hi = jnp.where(step == n_steps - 1, kb - halo, kb) if b == nblk - 1 else kb
        pen = None
        if b == 0 or b == nblk - 1:
            pen1 = jnp.where((col >= lo) & (col < hi), 0.0, NEG_INF)
            pen = jnp.concatenate([pen1, pen1], axis=1)
        rows = slice(b * blk, (b + 1) * blk)
        krows = slice(b * blk, b * blk + kb)
        lane_col = lambda j: kv[krows, j * LANES:(j + 1) * LANES]
        if shared_kv:
            k, k_sw, v, v_sw = (lane_col(j) for j in range(4))
            rhs = [_pair_rhs(k, k_sw, v, v_sw), _pair_rhs(k_sw, k, v_sw, v)]
        for s in range(n_seqs):
            for c in range(n_pairs):
                lanes = slice(s * qw + c * LANES, s * qw + (c + 1) * LANES)
                if shared_kv:
                    rhs_k, rhs_v = rhs[c // (n_pairs // 2)]
                else:
                    k, v = lane_col(2 * s * n_pairs + c), lane_col((2 * s + 1) * n_pairs + c)
                    rhs_k, rhs_v = _pair_rhs(k, k, v, v)
                o, lse = _pair_attention(q_ref[rows, lanes], rhs_k, rhs_v, bias_ref[c], pen,
                                         sink_ref[c] if has_sink else None, want_lse)
                o_ref[rows, lanes] = o.astype(o_ref.dtype)
                if want_lse:
                    lse_ref[rows, lanes] = lse


def _band_attention(q, kv, bias, sink, *, nbatch, seq, dil, halo, shared_kv, want_lse):
    n = nbatch * seq
    sub_len = seq // dil
    qw = q.shape[1] // dil
    kv_width = kv.shape[1] // dil
    rows = min(ATTN_ROWS, sub_len)
    n_seqs = min(dil, ATTN_ROWS // rows)
    assert sub_len % rows == 0 and rows % ATTN_BLOCK == 0 and rows % halo == 0 and dil % n_seqs == 0
    nq = sub_len // rows
    per = rows // halo
    total_halos = n // dil // halo
    cur = lambda b, r, i: (b * nq + i, r)
    prev = lambda b, r, i: (jnp.maximum((b * nq + i) * per - 1, 0), r)
    nxt = lambda b, r, i: (jnp.minimum((b * nq + i + 1) * per, total_halos - 1), r)
    const = lambda a: pl.BlockSpec(a.shape, lambda b, r, i: (0,) * a.ndim)
    qw_step, kvw_step = qw * n_seqs, kv_width * n_seqs

    in_specs = [pl.BlockSpec((rows, qw_step), cur), pl.BlockSpec((halo, kvw_step), prev),
                pl.BlockSpec((rows, kvw_step), cur), pl.BlockSpec((halo, kvw_step), nxt),
                const(bias)]
    args = [q, kv, kv, kv, bias]
    if sink is not None:
        in_specs.append(const(sink))
        args.append(sink)
    out_shape = [jax.ShapeDtypeStruct(q.shape, BF16)]
    out_specs = [pl.BlockSpec((rows, qw_step), cur)]
    if want_lse:
        out_shape.append(jax.ShapeDtypeStruct(q.shape, F32))
        out_specs.append(pl.BlockSpec((rows, qw_step), cur))
    return pl.pallas_call(
        functools.partial(_attn_kernel, halo=halo, nblk=rows // ATTN_BLOCK, n_pairs=qw // LANES,
                          n_seqs=n_seqs, shared_kv=shared_kv, has_sink=sink is not None,
                          want_lse=want_lse, n_steps=nq),
        out_shape=out_shape,
        grid=(nbatch, dil // n_seqs, nq),
        in_specs=in_specs,
        out_specs=out_specs,
        compiler_params=_cparams(3),
        name=f"band_attn_d{dil}",
    )(*args)


def _pack_rows(y):
    lo = pltpu.bitcast(y[:, :HALF_D].astype(BF16).astype(F32), U32)
    hi = pltpu.bitcast(y[:, HALF_D:].astype(BF16).astype(F32), U32)
    return (hi & jnp.uint32(0xFFFF0000)) | (lo >> 16)


def _store_slabs(ref, packed):
    t = packed.shape[0]
    for c in range(ROW_SLABS):
        ref[pl.ds(c, t, stride=ROW_SLABS), :] = packed[:, c * LANES:(c + 1) * LANES]


def _load_slabs(ref):
    t = ref.shape[0] // ROW_SLABS
    return jnp.concatenate([ref[pl.ds(c, t, stride=ROW_SLABS), :] for c in range(ROW_SLABS)], axis=1)


def _unpack_rows(p):
    lo = pltpu.bitcast(p << 16, F32)
    hi = pltpu.bitcast(p & jnp.uint32(0xFFFF0000), F32)
    return lo, hi


def _route(sel, scores):
    t = sel.shape[-1]
    per = N_EXPERTS // N_EXPERT_GROUPS
    shape3 = (N_EXPERT_GROUPS, per, t)
    sel3 = sel.reshape(shape3)
    sc3 = scores.reshape(shape3)
    iota_g = lax.broadcasted_iota(I32, shape3, 0)
    iota_m = lax.broadcasted_iota(I32, shape3, 1)
    iota_e = iota_g * per + iota_m
    m1 = jnp.max(sel3, axis=1, keepdims=True)
    i1 = jnp.min(jnp.where(sel3 == m1, iota_m, per), axis=1, keepdims=True)
    m2 = jnp.max(jnp.where(iota_m == i1, REMOVED, sel3), axis=1, keepdims=True)
    gscore = m1 + m2
    iota_g1 = lax.broadcasted_iota(I32, gscore.shape, 0)
    gmask = jnp.zeros(gscore.shape, jnp.bool_)
    for _ in range(TOPK_GROUPS):
        mx = jnp.max(gscore, axis=0, keepdims=True)
        ix = jnp.min(jnp.where(gscore == mx, iota_g1, N_EXPERT_GROUPS), axis=0, keepdims=True)
        hit = iota_g1 == ix
        gmask = gmask | hit
        gscore = jnp.where(hit, REMOVED, gscore)
    cur = jnp.where(gmask, sel3, NEG_INF)
    sum_all = lambda a: jnp.sum(jnp.sum(a, axis=1, keepdims=True), axis=0, keepdims=True)
    idxs, wts, hits = [], [], []
    for _ in range(TOP_K):
        mx = jnp.max(jnp.max(cur, axis=1, keepdims=True), axis=0, keepdims=True)
        cand = jnp.where(cur == mx, iota_e, N_EXPERTS)
        ix = jnp.min(jnp.min(cand, axis=1, keepdims=True), axis=0, keepdims=True)
        hit = iota_e == ix
        wts.append(sum_all(jnp.where(hit, sc3, 0.0)).reshape(1, t))
        cur = jnp.where(hit, REMOVED, cur)
        idxs.append(ix.reshape(1, t))
        hits.append(hit)
    wsum = wts[0]
    for w in wts[1:]:
        wsum = wsum + w
    wts = [w / wsum * ROUTED_SCALE for w in wts]
    onehot = hits[0].astype(F32)
    for hit in hits[1:]:
        onehot = onehot + hit.astype(F32)
    tri = (lax.broadcasted_iota(I32, (t, t), 0) <= lax.broadcasted_iota(I32, (t, t), 1))
    cum = jnp.dot(onehot.reshape(N_EXPERTS, t).astype(BF16), tri.astype(F32).astype(BF16),
                  preferred_element_type=F32)
    cum3 = cum.reshape(shape3) - 1.0
    ranks = [sum_all(jnp.where(hit, cum3, 0.0)).reshape(1, t).astype(I32) for hit in hits]
    counts = cum[:, t - 1:t].astype(I32)
    return idxs, wts, ranks, counts


def _merge_kernel(x_ref, ya_ref, o1, o2, o3, l1, l2, l3, sga_ref, sgb_ref, mod_ref, n2_ref,
                  wpa, wpb, wo, wrt, rbias, wsg, wsu, wsd,
                  base_ref, hp_ref, ridx_ref, rw_ref, rank_ref, cnt_ref, scr):
    mod = mod_ref[0]
    tm = x_ref.shape[0]

    def token_major(ref, slot, d):
        if d == 1:
            return ref[...].astype(F32)
        nc = B_OUT_W // LANES
        for c in range(nc):
            for res in range(d):
                col = res * B_OUT_W + c * LANES
                scr[slot * nc + c, pl.ds(res, tm // d, stride=d), :] = ref[
                    :, col:col + LANES].astype(F32)
        return jnp.concatenate([scr[slot * nc + c] for c in range(nc)], axis=1)

    dils = [d for _, d in B_GROUPS]
    os_ = [token_major(r, i, d) for i, (r, d) in enumerate(zip((o1, o2, o3), dils))]
    ls = [token_major(r, 3 + i, d) for i, (r, d) in enumerate(zip((l1, l2, l3), dils))]
    mx = jnp.maximum(jnp.maximum(ls[0], ls[1]), ls[2])
    es = [jnp.exp(l - mx) for l in ls]
    den = es[0] + es[1] + es[2]
    ob = (es[0] / den) * os_[0] + (es[1] / den) * os_[1] + (es[2] / den) * os_[2]
    pa = jnp.dot(ya_ref[...], wpa[...], preferred_element_type=F32)
    pb = jnp.dot(ob.astype(BF16), wpb[...], preferred_element_type=F32)
    merged = sga_ref[...].astype(F32) * pa + sgb_ref[...].astype(F32) * pb
    mix = jnp.dot(merged.astype(BF16), wo[...], preferred_element_type=F32)
    x1 = x_ref[...] + mod[2:3] * mix
    h2 = _rms(x1, n2_ref[...]) * (1.0 + mod[4:5]) + mod[3:4]
    logits = lax.dot_general(wrt[...], h2, (((1,), (1,)), ((), ())), preferred_element_type=F32,
                             precision=lax.Precision.HIGHEST)
    scores = jax.nn.sigmoid(logits)
    idxs, wts, ranks, counts = _route(scores + rbias[...], scores)
    for k in range(TOP_K):
        ridx_ref[k:k + 1, :] = idxs[k]
        rw_ref[k:k + 1, :] = wts[k]
        rank_ref[k:k + 1, :] = ranks[k]
    ridx_ref[TOP_K:, :] = jnp.zeros((8 - TOP_K, tm), I32)
    rw_ref[TOP_K:, :] = jnp.zeros((8 - TOP_K, tm), F32)
    rank_ref[TOP_K:, :] = jnp.zeros((8 - TOP_K, tm), I32)
    cnt_ref[0] = counts
    hb = h2.astype(BF16)
    g = jnp.dot(hb, wsg[...], preferred_element_type=F32)
    u = jnp.dot(hb, wsu[...], preferred_element_type=F32)
    act = (g * jax.nn.sigmoid(g) * u).astype(BF16)
    shared = jnp.dot(act, wsd[...], preferred_element_type=F32)
    base_ref[...] = x1 + mod[5:6] * shared
    _store_slabs(hp_ref, _pack_rows(h2))


def _merge(x2, ya, outs, lses, sga, sgb, mod3, norm2, wpa, wpb, wo, wrt, rbias, wsg, wsu, wsd, seq):
    n = x2.shape[0]
    tm = TM_MERGE
    assert seq % tm == 0
    row = lambda i: (i, 0)
    full = lambda a: pl.BlockSpec(a.shape, lambda i: (0,) * a.ndim)
    weights = [wpa, wpb, wo, wrt, rbias, wsg, wsu, wsd]
    group_specs = [pl.BlockSpec((tm // d, d * B_OUT_W), row) for _, d in B_GROUPS]
    lanes = lambda i: (0, i)
    return pl.pallas_call(
        _merge_kernel,
        out_shape=[jax.ShapeDtypeStruct((n, D_MODEL), F32),
                   jax.ShapeDtypeStruct((n * ROW_SLABS, LANES), U32),
                   jax.ShapeDtypeStruct((8, n), I32),
                   jax.ShapeDtypeStruct((8, n), F32),
                   jax.ShapeDtypeStruct((8, n), I32),
                   jax.ShapeDtypeStruct((n // tm, N_EXPERTS, 1), I32)],
        grid=(n // tm,),
        in_specs=[pl.BlockSpec((tm, D_MODEL), row), pl.BlockSpec((tm, A_Q_W), row)]
                 + group_specs * 2
                 + [pl.BlockSpec((tm, D_MODEL), row)] * 2
                 + [pl.BlockSpec((1, 6, D_MODEL), lambda i: (i * tm // seq, 0, 0)),
                    pl.BlockSpec((1, D_MODEL), lambda i: (0, 0))]
                 + [full(w) for w in weights],
        out_specs=[pl.BlockSpec((tm, D_MODEL), row),
                   pl.BlockSpec((tm * ROW_SLABS, LANES), row),
                   pl.BlockSpec((8, tm), lanes), pl.BlockSpec((8, tm), lanes),
                   pl.BlockSpec((8, tm), lanes),
                   pl.BlockSpec((1, N_EXPERTS, 1), lambda i: (i, 0, 0))],
        scratch_shapes=[pltpu.VMEM((6 * B_OUT_W // LANES, tm, LANES), F32)],
        compiler_params=_cparams(1),
        name="merge_route",
    )(x2, ya, *outs, *lses, sga, sgb, mod3, norm2.reshape(1, D_MODEL), *weights)


def _plan_kernel(ridx_ref, rank_ref, tb_ref, dest_ref):
    tb = tb_ref[0]
    t = ridx_ref.shape[1]
    iota_e = lax.broadcasted_iota(I32, (N_EXPERTS, t), 0)
    for k in range(TOP_K):
        base = jnp.sum(jnp.where(iota_e == ridx_ref[k:k + 1, :], tb, 0), axis=0, keepdims=True)
        dest_ref[0, :, k * t:(k + 1) * t] = base + rank_ref[k:k + 1, :]


def _plan(ridx, rank, tile_base, tm):
    n = ridx.shape[1]
    lanes = lambda i: (0, i)
    return pl.pallas_call(
        _plan_kernel,
        out_shape=jax.ShapeDtypeStruct((n // tm, 1, TOP_K * tm), I32),
        grid=(n // tm,),
        in_specs=[pl.BlockSpec((8, tm), lanes), pl.BlockSpec((8, tm), lanes),
                  pl.BlockSpec((1, N_EXPERTS, 1), lambda i: (i, 0, 0))],
        out_specs=pl.BlockSpec((1, 1, TOP_K * tm), lambda i: (i, 0, 0)),
        compiler_params=_cparams(1),
        name="moe_plan",
    )(ridx, rank, tile_base)


def _sc_mesh():
    return plsc.VectorSubcoreMesh(core_axis_name="core", subcore_axis_name="subcore",
                                  num_cores=SC_CORES, num_subcores=SC_SUBCORES)


def _sc_worker():
    return lax.axis_index("subcore") * SC_CORES + lax.axis_index("core")


def _sc_dispatch(hp, dest_rows, n_rows):
    n = hp.shape[0]
    halves = TM_MERGE // SC_CHUNK
    workers = SC_CORES * SC_SUBCORES
    tiles_per_w = n // TM_MERGE // workers
    idx_per_w = tiles_per_w * TOP_K * halves
    assert tiles_per_w * workers * TM_MERGE == n

    @functools.partial(
        pl.kernel, mesh=_sc_mesh(),
        out_type=jax.ShapeDtypeStruct((n_rows, ROW_SLABS, LANES), U32),
        scratch_types=[pltpu.VMEM((idx_per_w, SC_CHUNK), I32),
                       pltpu.VMEM((SC_CHUNK, ROW_SLABS, LANES), U32),
                       pltpu.SemaphoreType.DMA],
        name="moe_dispatch_sc")
    def scatter(hp_hbm, idx_hbm, xs_hbm, idx_v, rows_v, sem):
        wid = _sc_worker()
        pltpu.sync_copy(idx_hbm.at[pl.ds(wid * idx_per_w, idx_per_w)], idx_v)

        @pl.loop(0, tiles_per_w * halves)
        def _(j):
            tile = j // halves
            h = j - tile * halves
            tok = (wid * tiles_per_w + tile) * TM_MERGE + h * SC_CHUNK
            pltpu.sync_copy(hp_hbm.at[pl.ds(tok, SC_CHUNK)], rows_v)
            for k in range(TOP_K):
                row = (tile * TOP_K + k) * halves + h
                pltpu.async_copy(rows_v, xs_hbm.at[idx_v.at[row]], sem).wait()

    return scatter(hp, dest_rows)


def _sc_collect(ys, dest_rows):
    n_chunks = dest_rows.shape[0]
    workers = SC_CORES * SC_SUBCORES
    per_w = n_chunks // workers
    assert per_w * workers == n_chunks

    @functools.partial(
        pl.kernel, mesh=_sc_mesh(),
        out_type=jax.ShapeDtypeStruct((n_chunks * SC_CHUNK, ROW_SLABS, LANES), U32),
        scratch_types=[pltpu.VMEM((per_w, SC_CHUNK), I32),
                       pltpu.VMEM((SC_CHUNK, ROW_SLABS, LANES), U32),
                       pltpu.SemaphoreType.DMA],
        name="moe_collect_sc")
    def gather(ys_hbm, idx_hbm, out_hbm, idx_v, rows_v, sem):
        wid = _sc_worker()
        base = wid * per_w
        pltpu.sync_copy(idx_hbm.at[pl.ds(base, per_w)], idx_v)

        @pl.loop(0, per_w)
        def _(j):
            pltpu.async_copy(ys_hbm.at[idx_v.at[j]], rows_v, sem).wait()
            pltpu.sync_copy(rows_v, out_hbm.at[pl.ds((base + j) * SC_CHUNK, SC_CHUNK)])

    return gather(ys, dest_rows)


def _expert_kernel(blk_e_ref, valid_ref, xs_ref, wgu_ref, wd_ref, ys_ref):
    del blk_e_ref
    valid = valid_ref[pl.program_id(0)]

    @pl.when(valid > 0)
    def _():
        rows = lax.broadcasted_iota(I32, (xs_ref.shape[0] // ROW_SLABS, 1), 0)
        lo, hi = _unpack_rows(jnp.where(rows < valid, _load_slabs(xs_ref), jnp.uint32(0)))
        wgu = wgu_ref[0]
        gu = jnp.dot(lo.astype(BF16), wgu[:HALF_D], preferred_element_type=F32) \
            + jnp.dot(hi.astype(BF16), wgu[HALF_D:], preferred_element_type=F32)
        g = gu[:, :D_EXPERT]
        u = gu[:, D_EXPERT:]
        act = (g * jax.nn.sigmoid(g) * u).astype(BF16)
        _store_slabs(ys_ref, _pack_rows(jnp.dot(act, wd_ref[0], preferred_element_type=F32)))

    @pl.when(valid <= 0)
    def _():
        ys_ref[...] = jnp.zeros(ys_ref.shape, U32)


def _experts(xs, blk_e, blk_valid, wgu, wd):
    n_rows = xs.shape[0] // ROW_SLABS
    bm = MOE_BM
    nblk = n_rows // bm
    return pl.pallas_call(
        _expert_kernel,
        out_shape=jax.ShapeDtypeStruct((n_rows * ROW_SLABS, LANES), U32),
        grid_spec=pltpu.PrefetchScalarGridSpec(
            num_scalar_prefetch=2,
            grid=(nblk,),
            in_specs=[pl.BlockSpec((bm * ROW_SLABS, LANES), lambda i, be, nu: (i, 0)),
                      pl.BlockSpec((1, D_MODEL, 2 * D_EXPERT), lambda i, be, nu: (be[i], 0, 0)),
                      pl.BlockSpec((1, D_EXPERT, D_MODEL), lambda i, be, nu: (be[i], 0, 0))],
            out_specs=pl.BlockSpec((bm * ROW_SLABS, LANES), lambda i, be, nu: (i, 0))),
        compiler_params=_cparams(1),
        name="moe_experts",
    )(blk_e, blk_valid, xs, wgu, wd)


def _combine_kernel(g_ref, base_ref, rw_ref, mod_ref, fn_ref, y_ref):
    tm = base_ref.shape[0]
    rw = rw_ref[...]
    acc_lo = jnp.zeros((tm, HALF_D), F32)
    acc_hi = jnp.zeros((tm, HALF_D), F32)
    for k in range(TOP_K):
        lo, hi = _unpack_rows(_load_slabs(g_ref.at[0, k]))
        w = rw[:, k:k + 1]
        acc_lo = acc_lo + w * lo
        acc_hi = acc_hi + w * hi
    routed = jnp.concatenate([acc_lo, acc_hi], axis=1)
    x2 = base_ref[...] + mod_ref[0][5:6] * routed
    y_ref[...] = _rms(x2, fn_ref[...])


def _combine(g, base, rw_t, mod3, final_norm, seq):
    n = base.shape[0]
    tm = TM_COMBINE
    assert seq % tm == 0
    g4 = g.reshape(n // tm, TOP_K, tm * ROW_SLABS, LANES)
    return pl.pallas_call(
        _combine_kernel,
        out_shape=jax.ShapeDtypeStruct((n, D_MODEL), F32),
        grid=(n // tm,),
        in_specs=[pl.BlockSpec((1, TOP_K, tm * ROW_SLABS, LANES), lambda i: (i, 0, 0, 0)),
                  pl.BlockSpec((tm, D_MODEL), lambda i: (i, 0)),
                  pl.BlockSpec((tm, 8), lambda i: (i, 0)),
                  pl.BlockSpec((1, 6, D_MODEL), lambda i: (i * tm // seq, 0, 0)),
                  pl.BlockSpec((1, D_MODEL), lambda i: (0, 0))],
        out_specs=pl.BlockSpec((tm, D_MODEL), lambda i: (i, 0)),
        compiler_params=_cparams(1),
        name="moe_combine",
    )(g4, base, rw_t, mod3, final_norm.reshape(1, D_MODEL))


def _block_layout(tile_counts, n):
    bm = MOE_BM
    c = tile_counts[:, :, 0]
    ntiles = c.shape[0]
    counts = jnp.sum(c, axis=0)
    padded = (counts + bm - 1) // bm * bm
    earlier_e = np.tri(N_EXPERTS, k=-1, dtype=bool)
    pstart = jnp.sum(jnp.where(earlier_e, padded[None, :], 0), axis=1)
    pend = pstart + padded
    earlier_t = np.tri(ntiles, k=-1, dtype=bool)
    tile_base = pstart[None, :] + jnp.sum(jnp.where(earlier_t[:, :, None], c[None], 0), axis=1)
    nblk = -(-n * TOP_K // bm) + N_EXPERTS
    blk = jnp.arange(nblk, dtype=I32)
    blk_e = jnp.minimum(jnp.sum((pend[None, :] <= blk[:, None] * bm).astype(I32), axis=1),
                        N_EXPERTS - 1)
    onehot_e = blk_e[:, None] == jnp.arange(N_EXPERTS, dtype=I32)[None, :]
    end_e = jnp.sum(jnp.where(onehot_e, (pstart + counts)[None, :], 0), axis=1)
    blk_valid = jnp.clip(end_e - blk * bm, 0, bm).astype(I32)
    return tile_base.astype(I32)[:, :, None], blk_e.astype(I32), blk_valid, nblk * bm


def _trunk(x, mod3, p):
    nbatch, seq, _ = x.shape
    n = nbatch * seq
    x2 = x.reshape(n, D_MODEL)
    proj = _inproj(x2, mod3, p["norm1"], p["w_in"], seq)
    qa, kva = proj[:2]
    sga, sgb = proj[8:]
    (ya,) = _band_attention(qa, kva, p["bias_a"], p["sink_a"], nbatch=nbatch, seq=seq, dil=1,
                            halo=A_HALF_WINDOW, shared_kv=True, want_lse=False)
    outs, lses = [], []
    for gi, (w, d) in enumerate(B_GROUPS):
        q, kv = proj[2 + 2 * gi:4 + 2 * gi]
        o, lse = _band_attention(q, kv, p["bias_b"][gi], None, nbatch=nbatch, seq=seq, dil=d,
                                 halo=w // (2 * d), shared_kv=False, want_lse=True)
        outs.append(o)
        lses.append(lse)
    base, hp, ridx, rw, rank, tile_counts = _merge(
        x2, ya, outs, lses, sga, sgb, mod3, p["norm2"], p["w_pa"], p["w_pb"], p["w_o"], p["w_rt"],
        p["rbias"], p["ws_gate"], p["ws_up"], p["ws_down"], seq)
    assert TM_MERGE == TM_COMBINE
    tile_base, blk_e, blk_valid, n_rows = _block_layout(tile_counts, n)
    dest_rows = _plan(ridx, rank, tile_base, TM_MERGE).reshape(-1, SC_CHUNK)
    slabs = lambda a: a.reshape(-1, ROW_SLABS, LANES)
    flat = lambda a: a.reshape(-1, LANES)
    xs = _sc_dispatch(slabs(hp), dest_rows, n_rows)
    ys = _experts(flat(xs), blk_e, blk_valid, p["w_gu"], p["w_down"])
    g = _sc_collect(slabs(ys), dest_rows)
    y = _combine(g, base, rw.T, mod3, p["final_norm"], seq)
    return y.reshape(nbatch, seq, D_MODEL)


def kernel(x_prompt, x_sample, c_prompt, c_sample, rel_bias, w_ada, b_ada, norm1, w_in, sink, w_pa, w_pb, w_o, norm2, w_router, router_bias, w_gate, w_up, w_down, ws_gate, ws_up, ws_down, final_norm):
    assert w_ada.shape[0] == 1
    nbp = x_prompt.shape[0]
    mod = _ada(jnp.concatenate([c_prompt, c_sample], axis=0), w_ada[0], b_ada[0])
    mod3 = mod.reshape(-1, 6, D_MODEL)

    def pair_bias(heads, halo, dist_scale):
        kb = ATTN_BLOCK + 2 * halo
        tab = _bias_table(heads, ATTN_BLOCK, kb, halo, halo, dist_scale)
        return tab.reshape(-1, 2, ATTN_BLOCK, kb).transpose(0, 2, 1, 3).reshape(-1, ATTN_BLOCK, 2 * kb)

    bias_a = pair_bias(rel_bias[:, :A_Q_HEADS], A_HALF_WINDOW, 1)
    bias_b = []
    for gi, (w, d) in enumerate(B_GROUPS):
        h0 = A_Q_HEADS + gi * B_HEADS_PER_GROUP
        bias_b.append(pair_bias(rel_bias[:, h0:h0 + B_HEADS_PER_GROUP], w // (2 * d), d))
    p = {
        "norm1": norm1[0], "norm2": norm2[0], "final_norm": final_norm,
        "w_in": w_in[0].astype(BF16),
        "bias_a": bias_a,
        "sink_a": jnp.repeat(sink[0].astype(F32), HEAD_DIM).reshape(A_Q_HEADS // 2, 1, LANES),
        "bias_b": bias_b,
        "w_pa": w_pa[0].astype(BF16), "w_pb": w_pb[0].astype(BF16), "w_o": w_o[0].astype(BF16),
        "w_rt": w_router[0].T, "rbias": router_bias[0].reshape(N_EXPERTS, 1),
        "ws_gate": ws_gate[0].astype(BF16), "ws_up": ws_up[0].astype(BF16),
        "ws_down": ws_down[0].astype(BF16),
        "w_gu": jnp.concatenate([w_gate[0], w_up[0]], axis=-1).astype(BF16),
        "w_down": w_down[0].astype(BF16),
    }
    y_prompt = _trunk(x_prompt, mod3[:nbp], p)
    y_sample = _trunk(x_sample, mod3[nbp:], p)
    return (y_prompt, y_sample)
```

```python
import functools
import math

import jax
import jax.numpy as jnp
import numpy as np
from jax import lax
from jax.experimental import pallas as pl
from jax.experimental.pallas import tpu as pltpu
from jax.experimental.pallas import tpu_sc as plsc

F32 = jnp.float32
BF16 = jnp.bfloat16
U32 = jnp.uint32
I32 = jnp.int32

D_MODEL = 1024
HEAD_DIM = 64
A_Q_HEADS = 8
A_KV_HEADS = 2
A_HALF_WINDOW = 128
B_GROUPS = ((128, 1), (512, 4), (2048, 16))
B_HEADS_PER_GROUP = 4
N_BUCKETS = 32
MAX_DISTANCE = 1024
N_EXPERTS = 64
TOP_K = 6
N_EXPERT_GROUPS = 8
TOPK_GROUPS = 4
D_EXPERT = 256
ROUTED_SCALE = 2.5
RMS_EPS = 1e-6
NEG_INF = -1e30
REMOVED = -3e38

A_Q_W = A_Q_HEADS * HEAD_DIM
A_KV_W = A_KV_HEADS * HEAD_DIM
B_W = len(B_GROUPS) * B_HEADS_PER_GROUP * HEAD_DIM
B_OUT_W = B_HEADS_PER_GROUP * HEAD_DIM
D_IN = A_Q_W + 2 * A_KV_W + 3 * B_W + 2 * D_MODEL
HALF_D = D_MODEL // 2
LANES = 128
SUBLANES = 8
ROW_SLABS = HALF_D // LANES
SC_CORES = 2
SC_SUBCORES = 16
SC_CHUNK = 128

TM_INPROJ = 1024
TM_MERGE = 512
ATTN_BLOCK = 128
ATTN_ROWS = 512
MOE_BM = 512
TM_COMBINE = 512
VMEM_LIMIT = 60 * 1024 * 1024


def _cparams(n_axes):
    return pltpu.CompilerParams(
        dimension_semantics=("arbitrary",) * n_axes, vmem_limit_bytes=VMEM_LIMIT)


def _ada_kernel(c_ref, w_ref, b_ref, o_ref):
    c = c_ref[...]
    s = c * jax.nn.sigmoid(c)
    o_ref[...] = jnp.dot(s, w_ref[...], preferred_element_type=F32,
                         precision=lax.Precision.HIGHEST) + b_ref[...]


def _ada(c_all, w_ada, b_ada):
    nb = c_all.shape[0]
    return pl.pallas_call(
        _ada_kernel,
        out_shape=jax.ShapeDtypeStruct((nb, 6 * D_MODEL), F32),
        grid=(6,),
        in_specs=[pl.BlockSpec((nb, D_MODEL), lambda j: (0, 0)),
                  pl.BlockSpec((D_MODEL, D_MODEL), lambda j: (0, j)),
                  pl.BlockSpec((1, D_MODEL), lambda j: (0, j))],
        out_specs=pl.BlockSpec((nb, D_MODEL), lambda j: (0, j)),
        compiler_params=_cparams(1),
        name="ada",
    )(c_all, w_ada, b_ada.reshape(1, 6 * D_MODEL))


def _rms(x, g):
    return x * lax.rsqrt(jnp.mean(x * x, axis=-1, keepdims=True) + RMS_EPS) * g


def _inproj_kernel(x_ref, mod_ref, n1_ref, w_ref, qa, kva, *rest):
    q_b = rest[0:6:2]
    kv_b = rest[1:6:2]
    sga, sgb, scr = rest[6:]
    mod = mod_ref[0]
    h = _rms(x_ref[...], n1_ref[...]) * (1.0 + mod[1:2]) + mod[0:1]
    hb = h.astype(BF16)
    tm = hb.shape[0]
    scale = HEAD_DIM ** -0.5

    def proj(off, width):
        return jnp.dot(hb, w_ref[:, off:off + width], preferred_element_type=F32)

    qa[...] = (proj(0, A_Q_W) * scale).astype(BF16)
    for j, off in enumerate((A_Q_W, A_Q_W + A_KV_W)):
        r = proj(off, A_KV_W)
        kva[:, 2 * j * A_KV_W:(2 * j + 1) * A_KV_W] = r.astype(BF16)
        kva[:, (2 * j + 1) * A_KV_W:(2 * j + 2) * A_KV_W] = pltpu.roll(r, HEAD_DIM, 1).astype(BF16)
    off = A_Q_W + 2 * A_KV_W
    for t in range(3):
        for gi, (_, d) in enumerate(B_GROUPS):
            r = proj(off + t * B_W + gi * B_OUT_W, B_OUT_W)
            if t == 0:
                r = r * scale
            ref, width, base = (q_b[gi], B_OUT_W, 0) if t == 0 else (kv_b[gi], 2 * B_OUT_W,
                                                                     (t - 1) * B_OUT_W)
            if d == 1:
                ref[:, base:base + B_OUT_W] = r.astype(BF16)
            else:
                for c in range(B_OUT_W // LANES):
                    slot = (t * 2 + gi - 1) * (B_OUT_W // LANES) + c
                    scr[slot] = r[:, c * LANES:(c + 1) * LANES]
                    for res in range(d):
                        col = res * width + base + c * LANES
                        ref[:, col:col + LANES] = scr[
                            slot, pl.ds(res, tm // d, stride=d), :].astype(BF16)
    off += 3 * B_W
    sga[...] = jax.nn.sigmoid(proj(off, D_MODEL)).astype(BF16)
    sgb[...] = jax.nn.sigmoid(proj(off + D_MODEL, D_MODEL)).astype(BF16)


def _inproj(x2, mod3, norm1, w_in_bf, seq):
    n = x2.shape[0]
    tm = TM_INPROJ
    assert seq % tm == 0 and n % tm == 0
    row = lambda i: (i, 0)
    shapes = [(n, A_Q_W, tm), (n, 4 * A_KV_W, tm)]
    for _, d in B_GROUPS:
        shapes += [(n // d, d * B_OUT_W, tm // d), (n // d, d * 2 * B_OUT_W, tm // d)]
    shapes += [(n, D_MODEL, tm)] * 2
    return pl.pallas_call(
        _inproj_kernel,
        out_shape=[jax.ShapeDtypeStruct((r, c), BF16) for r, c, _ in shapes],
        grid=(n // tm,),
        in_specs=[pl.BlockSpec((tm, D_MODEL), row),
                  pl.BlockSpec((1, 6, D_MODEL), lambda i: (i * tm // seq, 0, 0)),
                  pl.BlockSpec((1, D_MODEL), lambda i: (0, 0)),
                  pl.BlockSpec((D_MODEL, D_IN), lambda i: (0, 0), pipeline_mode=pl.Buffered(1))],
        out_specs=[pl.BlockSpec((b, c), row) for _, c, b in shapes],
        scratch_shapes=[pltpu.VMEM((6 * B_OUT_W // LANES, tm, LANES), F32)],
        compiler_params=_cparams(1),
        name="inproj",
    )(x2, mod3, norm1.reshape(1, D_MODEL), w_in_bf)


def _rel_bucket_np(rel):
    half = N_BUCKETS // 2
    max_exact = half // 2
    n = np.abs(rel)
    large = max_exact + (np.log(np.maximum(n, 1) / max_exact) / math.log(MAX_DISTANCE / max_exact)
                         * (half - max_exact)).astype(np.int32)
    large = np.minimum(large, half - 1)
    return ((rel > 0).astype(np.int32) * half + np.where(n < max_exact, n, large)).astype(np.int32)


def _bias_table(rel_bias_heads, n_q, n_k, key_off, band, dist_scale):
    p = n_q + n_k
    rel = np.arange(p) - (n_q - 1) - key_off
    bucket = _rel_bucket_np(rel * dist_scale)
    t = jnp.where((np.abs(rel) <= band)[None], rel_bias_heads.astype(F32)[bucket].T, NEG_INF)
    big = jnp.tile(t, (1, n_q + 1))
    tab = big[:, n_q - 1:n_q - 1 + n_q * (p - 1)].reshape(-1, n_q, p - 1)
    return tab[:, :, :n_k]


def _pair_rhs(k_top, k_bot, v_top, v_bot):
    kb = k_top.shape[0]
    low = jnp.where(lax.broadcasted_iota(I32, (kb, LANES), 1) < HEAD_DIM, 1.0, 0.0).astype(BF16)
    high = jnp.where(lax.broadcasted_iota(I32, (kb, LANES), 1) < HEAD_DIM, 0.0, 1.0).astype(BF16)
    rhs_k = jnp.concatenate([k_top * low, k_bot * high], axis=0)
    rhs_v = jnp.concatenate([jnp.concatenate([v_top * low, low], axis=1),
                             jnp.concatenate([v_bot * high, high], axis=1)], axis=0)
    return rhs_k, rhs_v


def _pair_attention(q_pair, rhs_k, rhs_v, bias_pair, pen, sink_pair, want_lse):
    kb = rhs_k.shape[0] // 2
    s = lax.dot_general(q_pair, rhs_k, (((1,), (1,)), ((), ())), preferred_element_type=F32)
    s = s + bias_pair
    if pen is not None:
        s = s + pen
    s0, s1 = s[:, :kb], s[:, kb:]
    m0 = jnp.max(s0, axis=-1, keepdims=True)
    m1 = jnp.max(s1, axis=-1, keepdims=True)
    if sink_pair is not None:
        m0 = jnp.maximum(m0, sink_pair[:, 0:1])
        m1 = jnp.maximum(m1, sink_pair[:, HEAD_DIM:HEAD_DIM + 1])
    p = jnp.concatenate([jnp.exp(s0 - m0), jnp.exp(s1 - m1)], axis=1).astype(BF16)
    od = jnp.dot(p, rhs_v, preferred_element_type=F32)
    o, den = od[:, :LANES], od[:, LANES:]
    low = lax.broadcasted_iota(I32, o.shape, 1) < HEAD_DIM
    m_full = jnp.where(low, m0, m1)
    if sink_pair is not None:
        den = den + jnp.exp(sink_pair - m_full)
    return o / den, (m_full + jnp.log(den)) if want_lse else None


def _attn_kernel(*refs, halo, nblk, n_pairs, n_seqs, shared_kv, has_sink, want_lse, n_steps):
    it = iter(refs)
    q_ref = next(it)
    kv_prev, kv_cur, kv_next = next(it), next(it), next(it)
    bias_ref = next(it)
    sink_ref = next(it) if has_sink else None
    o_ref = next(it)
    lse_ref = next(it) if want_lse else None
    step = pl.program_id(2)
    blk = ATTN_BLOCK
    kb = blk + 2 * halo
    qw = n_pairs * LANES
    kv = jnp.concatenate([kv_prev[...], kv_cur[...], kv_next[...]], axis=0)
    col = lax.broadcasted_iota(I32, (1, kb), 1)
    for b in range(nblk):
        lo = jnp.where(step == 0, halo, 0) if b == 0 else 0
        hi = jnp.where(step == n_steps - 1, kb - halo, kb) if b == nblk - 1 else kb
        pen = None
        if b == 0 or b == nblk - 1:
            pen1 = jnp.where((col >= lo) & (col < hi), 0.0, NEG_INF)
            pen = jnp.concatenate([pen1, pen1], axis=1)
        rows = slice(b * blk, (b + 1) * blk)
        krows = slice(b * blk, b * blk + kb)
        lane_col = lambda j: kv[krows, j * LANES:(j + 1) * LANES]
        if shared_kv:
            k, k_sw, v, v_sw = (lane_col(j) for j in range(4))
            rhs = [_pair_rhs(k, k_sw, v, v_sw), _pair_rhs(k_sw, k, v_sw, v)]
        for s in range(n_seqs):
            for c in range(n_pairs):
                lanes = slice(s * qw + c * LANES, s * qw + (c + 1) * LANES)
                if shared_kv:
                    rhs_k, rhs_v = rhs[c // (n_pairs // 2)]
                else:
                    k, v = lane_col(2 * s * n_pairs + c), lane_col((2 * s + 1) * n_pairs + c)
                    rhs_k, rhs_v = _pair_rhs(k, k, v, v)
                o, lse = _pair_attention(q_ref[rows, lanes], rhs_k, rhs_v, bias_ref[c], pen,
                                         sink_ref[c] if has_sink else None, want_lse)
                o_ref[rows, lanes] = o.astype(o_ref.dtype)
                if want_lse:
                    lse_ref[rows, lanes] = lse


def _band_attention(q, kv, bias, sink, *, nbatch, seq, dil, halo, shared_kv, want_lse):
    n = nbatch * seq
    sub_len = seq // dil
    qw = q.shape[1] // dil
    kv_width = kv.shape[1] // dil
    rows = min(ATTN_ROWS, sub_len)
    n_seqs = min(dil, ATTN_ROWS // rows)
    assert sub_len % rows == 0 and rows % ATTN_BLOCK == 0 and rows % halo == 0 and dil % n_seqs == 0
    nq = sub_len // rows
    per = rows // halo
    total_halos = n // dil // halo
    cur = lambda b, r, i: (b * nq + i, r)
    prev = lambda b, r, i: (jnp.maximum((b * nq + i) * per - 1, 0), r)
    nxt = lambda b, r, i: (jnp.minimum((b * nq + i + 1) * per, total_halos - 1), r)
    const = lambda a: pl.BlockSpec(a.shape, lambda b, r, i: (0,) * a.ndim)
    qw_step, kvw_step = qw * n_seqs, kv_width * n_seqs

    in_specs = [pl.BlockSpec((rows, qw_step), cur), pl.BlockSpec((halo, kvw_step), prev),
                pl.BlockSpec((rows, kvw_step), cur), pl.BlockSpec((halo, kvw_step), nxt),
                const(bias)]
    args = [q, kv, kv, kv, bias]
    if sink is not None:
        in_specs.append(const(sink))
        args.append(sink)
    out_shape = [jax.ShapeDtypeStruct(q.shape, BF16)]
    out_specs = [pl.BlockSpec((rows, qw_step), cur)]
    if want_lse:
        out_shape.append(jax.ShapeDtypeStruct(q.shape, F32))
        out_specs.append(pl.BlockSpec((rows, qw_step), cur))
    return pl.pallas_call(
        functools.partial(_attn_kernel, halo=halo, nblk=rows // ATTN_BLOCK, n_pairs=qw // LANES,
                          n_seqs=n_seqs, shared_kv=shared_kv, has_sink=sink is not None,
                          want_lse=want_lse, n_steps=nq),
        out_shape=out_shape,
        grid=(nbatch, dil // n_seqs, nq),
        in_specs=in_specs,
        out_specs=out_specs,
        compiler_params=_cparams(3),
        name=f"band_attn_d{dil}",
    )(*args)


def _pack_rows(y):
    lo = pltpu.bitcast(y[:, :HALF_D].astype(BF16).astype(F32), U32)
    hi = pltpu.bitcast(y[:, HALF_D:].astype(BF16).astype(F32), U32)
    return (hi & jnp.uint32(0xFFFF0000)) | (lo >> 16)


def _store_slabs(ref, packed):
    t = packed.shape[0]
    for c in range(ROW_SLABS):
        ref[pl.ds(c, t, stride=ROW_SLABS), :] = packed[:, c * LANES:(c + 1) * LANES]


def _load_slabs(ref):
    t = ref.shape[0] // ROW_SLABS
    return jnp.concatenate([ref[pl.ds(c, t, stride=ROW_SLABS), :] for c in range(ROW_SLABS)], axis=1)


def _unpack_rows(p):
    lo = pltpu.bitcast(p << 16, F32)
    hi = pltpu.bitcast(p & jnp.uint32(0xFFFF0000), F32)
    return lo, hi


def _route(sel, scores):
    t = sel.shape[-1]
    per = N_EXPERTS // N_EXPERT_GROUPS
    shape3 = (N_EXPERT_GROUPS, per, t)
    sel3 = sel.reshape(shape3)
    sc3 = scores.reshape(shape3)
    iota_g = lax.broadcasted_iota(I32, shape3, 0)
    iota_m = lax.broadcasted_iota(I32, shape3, 1)
    iota_e = iota_g * per + iota_m
    m1 = jnp.max(sel3, axis=1, keepdims=True)
    i1 = jnp.min(jnp.where(sel3 == m1, iota_m, per), axis=1, keepdims=True)
    m2 = jnp.max(jnp.where(iota_m == i1, REMOVED, sel3), axis=1, keepdims=True)
    gscore = m1 + m2
    iota_g1 = lax.broadcasted_iota(I32, gscore.shape, 0)
    gmask = jnp.zeros(gscore.shape, jnp.bool_)
    for _ in range(TOPK_GROUPS):
        mx = jnp.max(gscore, axis=0, keepdims=True)
        ix = jnp.min(jnp.where(gscore == mx, iota_g1, N_EXPERT_GROUPS), axis=0, keepdims=True)
        hit = iota_g1 == ix
        gmask = gmask | hit
        gscore = jnp.where(hit, REMOVED, gscore)
    cur = jnp.where(gmask, sel3, NEG_INF)
    sum_all = lambda a: jnp.sum(jnp.sum(a, axis=1, keepdims=True), axis=0, keepdims=True)
    idxs, wts, hits = [], [], []
    for _ in range(TOP_K):
        mx = jnp.max(jnp.max(cur, axis=1, keepdims=True), axis=0, keepdims=True)
        cand = jnp.where(cur == mx, iota_e, N_EXPERTS)
        ix = jnp.min(jnp.min(cand, axis=1, keepdims=True), axis=0, keepdims=True)
        hit = iota_e == ix
        wts.append(sum_all(jnp.where(hit, sc3, 0.0)).reshape(1, t))
        cur = jnp.where(hit, REMOVED, cur)
        idxs.append(ix.reshape(1, t))
        hits.append(hit)
    wsum = wts[0]
    for w in wts[1:]:
        wsum = wsum + w
    wts = [w / wsum * ROUTED_SCALE for w in wts]
    onehot = hits[0].astype(F32)
    for hit in hits[1:]:
        onehot = onehot + hit.astype(F32)
    tri = (lax.broadcasted_iota(I32, (t, t), 0) <= lax.broadcasted_iota(I32, (t, t), 1))
    cum = jnp.dot(onehot.reshape(N_EXPERTS, t).astype(BF16), tri.astype(F32).astype(BF16),
                  preferred_element_type=F32)
    cum3 = cum.reshape(shape3) - 1.0
    ranks = [sum_all(jnp.where(hit, cum3, 0.0)).reshape(1, t).astype(I32) for hit in hits]
    counts = cum[:, t - 1:t].astype(I32)
    return idxs, wts, ranks, counts


def _merge_kernel(x_ref, ya_ref, o1, o2, o3, l1, l2, l3, sga_ref, sgb_ref, mod_ref, n2_ref,
                  wpa, wpb, wo, wrt, rbias, wsg, wsu, wsd,
                  base_ref, hp_ref, ridx_ref, rw_ref, rank_ref, cnt_ref, scr):
    mod = mod_ref[0]
    tm = x_ref.shape[0]

    def token_major(ref, slot, d):
        if d == 1:
            return ref[...].astype(F32)
        nc = B_OUT_W // LANES
        for c in range(nc):
            for res in range(d):
                col = res * B_OUT_W + c * LANES
                scr[slot * nc + c, pl.ds(res, tm // d, stride=d), :] = ref[
                    :, col:col + LANES].astype(F32)
        return jnp.concatenate([scr[slot * nc + c] for c in range(nc)], axis=1)

    dils = [d for _, d in B_GROUPS]
    os_ = [token_major(r, i, d) for i, (r, d) in enumerate(zip((o1, o2, o3), dils))]
    ls = [token_major(r, 3 + i, d) for i, (r, d) in enumerate(zip((l1, l2, l3), dils))]
    mx = jnp.maximum(jnp.maximum(ls[0], ls[1]), ls[2])
    es = [jnp.exp(l - mx) for l in ls]
    den = es[0] + es[1] + es[2]
    ob = (es[0] / den) * os_[0] + (es[1] / den) * os_[1] + (es[2] / den) * os_[2]
    pa = jnp.dot(ya_ref[...], wpa[...], preferred_element_type=F32)
    pb = jnp.dot(ob.astype(BF16), wpb[...], preferred_element_type=F32)
    merged = sga_ref[...].astype(F32) * pa + sgb_ref[...].astype(F32) * pb
    mix = jnp.dot(merged.astype(BF16), wo[...], preferred_element_type=F32)
    x1 = x_ref[...] + mod[2:3] * mix
    h2 = _rms(x1, n2_ref[...]) * (1.0 + mod[4:5]) + mod[3:4]
    logits = lax.dot_general(wrt[...], h2, (((1,), (1,)), ((), ())), preferred_element_type=F32,
                             precision=lax.Precision.HIGHEST)
    scores = jax.nn.sigmoid(logits)
    idxs, wts, ranks, counts = _route(scores + rbias[...], scores)
    for k in range(TOP_K):
        ridx_ref[k:k + 1, :] = idxs[k]
        rw_ref[k:k + 1, :] = wts[k]
        rank_ref[k:k + 1, :] = ranks[k]
    ridx_ref[TOP_K:, :] = jnp.zeros((SUBLANES - TOP_K, tm), I32)
    rw_ref[TOP_K:, :] = jnp.zeros((SUBLANES - TOP_K, tm), F32)
    rank_ref[TOP_K:, :] = jnp.zeros((SUBLANES - TOP_K, tm), I32)
    cnt_ref[0] = counts
    hb = h2.astype(BF16)
    g = jnp.dot(hb, wsg[...], preferred_element_type=F32)
    u = jnp.dot(hb, wsu[...], preferred_element_type=F32)
    act = (g * jax.nn.sigmoid(g) * u).astype(BF16)
    shared = jnp.dot(act, wsd[...], preferred_element_type=F32)
    base_ref[...] = x1 + mod[5:6] * shared
    _store_slabs(hp_ref, _pack_rows(h2))


def _merge(x2, ya, outs, lses, sga, sgb, mod3, norm2, wpa, wpb, wo, wrt, rbias, wsg, wsu, wsd, seq):
    n = x2.shape[0]
    tm = TM_MERGE
    assert seq % tm == 0
    row = lambda i: (i, 0)
    full = lambda a: pl.BlockSpec(a.shape, lambda i: (0,) * a.ndim)
    weights = [wpa, wpb, wo, wrt, rbias, wsg, wsu, wsd]
    group_specs = [pl.BlockSpec((tm // d, d * B_OUT_W), row) for _, d in B_GROUPS]
    lanes = lambda i: (0, i)
    return pl.pallas_call(
        _merge_kernel,
        out_shape=[jax.ShapeDtypeStruct((n, D_MODEL), F32),
                   jax.ShapeDtypeStruct((n * ROW_SLABS, LANES), U32),
                   jax.ShapeDtypeStruct((SUBLANES, n), I32),
                   jax.ShapeDtypeStruct((SUBLANES, n), F32),
                   jax.ShapeDtypeStruct((SUBLANES, n), I32),
                   jax.ShapeDtypeStruct((n // tm, N_EXPERTS, 1), I32)],
        grid=(n // tm,),
        in_specs=[pl.BlockSpec((tm, D_MODEL), row), pl.BlockSpec((tm, A_Q_W), row)]
                 + group_specs * 2
                 + [pl.BlockSpec((tm, D_MODEL), row)] * 2
                 + [pl.BlockSpec((1, 6, D_MODEL), lambda i: (i * tm // seq, 0, 0)),
                    pl.BlockSpec((1, D_MODEL), lambda i: (0, 0))]
                 + [full(w) for w in weights],
        out_specs=[pl.BlockSpec((tm, D_MODEL), row),
                   pl.BlockSpec((tm * ROW_SLABS, LANES), row),
                   pl.BlockSpec((SUBLANES, tm), lanes), pl.BlockSpec((SUBLANES, tm), lanes),
                   pl.BlockSpec((SUBLANES, tm), lanes),
                   pl.BlockSpec((1, N_EXPERTS, 1), lambda i: (i, 0, 0))],
        scratch_shapes=[pltpu.VMEM((6 * B_OUT_W // LANES, tm, LANES), F32)],
        compiler_params=_cparams(1),
        name="merge_route",
    )(x2, ya, *outs, *lses, sga, sgb, mod3, norm2.reshape(1, D_MODEL), *weights)


def _plan_kernel(ridx_ref, rank_ref, tb_ref, dest_ref):
    tb = tb_ref[0]
    t = ridx_ref.shape[1]
    iota_e = lax.broadcasted_iota(I32, (N_EXPERTS, t), 0)
    for k in range(TOP_K):
        base = jnp.sum(jnp.where(iota_e == ridx_ref[k:k + 1, :], tb, 0), axis=0, keepdims=True)
        dest_ref[0, :, k * t:(k + 1) * t] = base + rank_ref[k:k + 1, :]


def _plan(ridx, rank, tile_base, tm):
    n = ridx.shape[1]
    lanes = lambda i: (0, i)
    return pl.pallas_call(
        _plan_kernel,
        out_shape=jax.ShapeDtypeStruct((n // tm, 1, TOP_K * tm), I32),
        grid=(n // tm,),
        in_specs=[pl.BlockSpec((SUBLANES, tm), lanes), pl.BlockSpec((SUBLANES, tm), lanes),
                  pl.BlockSpec((1, N_EXPERTS, 1), lambda i: (i, 0, 0))],
        out_specs=pl.BlockSpec((1, 1, TOP_K * tm), lambda i: (i, 0, 0)),
        compiler_params=_cparams(1),
        name="moe_plan",
    )(ridx, rank, tile_base)


def _sc_mesh():
    return plsc.VectorSubcoreMesh(core_axis_name="core", subcore_axis_name="subcore",
                                  num_cores=SC_CORES, num_subcores=SC_SUBCORES)


def _sc_worker():
    return lax.axis_index("subcore") * SC_CORES + lax.axis_index("core")


def _sc_dispatch(hp, dest_rows, n_rows):
    n = hp.shape[0]
    halves = TM_MERGE // SC_CHUNK
    workers = SC_CORES * SC_SUBCORES
    tiles_per_w = n // TM_MERGE // workers
    idx_per_w = tiles_per_w * TOP_K * halves
    assert tiles_per_w * workers * TM_MERGE == n

    @functools.partial(
        pl.kernel, mesh=_sc_mesh(),
        out_type=jax.ShapeDtypeStruct((n_rows, ROW_SLABS, LANES), U32),
        scratch_types=[pltpu.VMEM((idx_per_w, SC_CHUNK), I32),
                       pltpu.VMEM((SC_CHUNK, ROW_SLABS, LANES), U32),
                       pltpu.SemaphoreType.DMA],
        name="moe_dispatch_sc")
    def scatter(hp_hbm, idx_hbm, xs_hbm, idx_v, rows_v, sem):
        wid = _sc_worker()
        pltpu.sync_copy(idx_hbm.at[pl.ds(wid * idx_per_w, idx_per_w)], idx_v)

        @pl.loop(0, tiles_per_w * halves)
        def _(j):
            tile = j // halves
            h = j - tile * halves
            tok = (wid * tiles_per_w + tile) * TM_MERGE + h * SC_CHUNK
            pltpu.sync_copy(hp_hbm.at[pl.ds(tok, SC_CHUNK)], rows_v)
            for k in range(TOP_K):
                row = (tile * TOP_K + k) * halves + h
                pltpu.async_copy(rows_v, xs_hbm.at[idx_v.at[row]], sem).wait()

    return scatter(hp, dest_rows)


def _sc_collect(ys, dest_rows):
    n_chunks = dest_rows.shape[0]
    workers = SC_CORES * SC_SUBCORES
    per_w = n_chunks // workers
    assert per_w * workers == n_chunks

    @functools.partial(
        pl.kernel, mesh=_sc_mesh(),
        out_type=jax.ShapeDtypeStruct((n_chunks * SC_CHUNK, ROW_SLABS, LANES), U32),
        scratch_types=[pltpu.VMEM((per_w, SC_CHUNK), I32),
                       pltpu.VMEM((SC_CHUNK, ROW_SLABS, LANES), U32),
                       pltpu.SemaphoreType.DMA],
        name="moe_collect_sc")
    def gather(ys_hbm, idx_hbm, out_hbm, idx_v, rows_v, sem):
        wid = _sc_worker()
        base = wid * per_w
        pltpu.sync_copy(idx_hbm.at[pl.ds(base, per_w)], idx_v)

        @pl.loop(0, per_w)
        def _(j):
            pltpu.async_copy(ys_hbm.at[idx_v.at[j]], rows_v, sem).wait()
            pltpu.sync_copy(rows_v, out_hbm.at[pl.ds((base + j) * SC_CHUNK, SC_CHUNK)])

    return gather(ys, dest_rows)


def _expert_kernel(blk_e_ref, valid_ref, xs_ref, wgu_ref, wd_ref, ys_ref):
    del blk_e_ref
    valid = valid_ref[pl.program_id(0)]

    @pl.when(valid > 0)
    def _():
        rows = lax.broadcasted_iota(I32, (xs_ref.shape[0] // ROW_SLABS, 1), 0)
        lo, hi = _unpack_rows(jnp.where(rows < valid, _load_slabs(xs_ref), jnp.uint32(0)))
        wgu = wgu_ref[0]
        gu = jnp.dot(lo.astype(BF16), wgu[:HALF_D], preferred_element_type=F32) \
            + jnp.dot(hi.astype(BF16), wgu[HALF_D:], preferred_element_type=F32)
        g = gu[:, :D_EXPERT]
        u = gu[:, D_EXPERT:]
        act = (g * jax.nn.sigmoid(g) * u).astype(BF16)
        _store_slabs(ys_ref, _pack_rows(jnp.dot(act, wd_ref[0], preferred_element_type=F32)))

    @pl.when(valid <= 0)
    def _():
        ys_ref[...] = jnp.zeros(ys_ref.shape, U32)


def _experts(xs, blk_e, blk_valid, wgu, wd):
    n_rows = xs.shape[0] // ROW_SLABS
    bm = MOE_BM
    nblk = n_rows // bm
    return pl.pallas_call(
        _expert_kernel,
        out_shape=jax.ShapeDtypeStruct((n_rows * ROW_SLABS, LANES), U32),
        grid_spec=pltpu.PrefetchScalarGridSpec(
            num_scalar_prefetch=2,
            grid=(nblk,),
            in_specs=[pl.BlockSpec((bm * ROW_SLABS, LANES), lambda i, be, nu: (i, 0)),
                      pl.BlockSpec((1, D_MODEL, 2 * D_EXPERT), lambda i, be, nu: (be[i], 0, 0)),
                      pl.BlockSpec((1, D_EXPERT, D_MODEL), lambda i, be, nu: (be[i], 0, 0))],
            out_specs=pl.BlockSpec((bm * ROW_SLABS, LANES), lambda i, be, nu: (i, 0))),
        compiler_params=_cparams(1),
        name="moe_experts",
    )(blk_e, blk_valid, xs, wgu, wd)


def _combine_kernel(g_ref, base_ref, rw_ref, mod_ref, fn_ref, y_ref):
    tm = base_ref.shape[0]
    rw = rw_ref[...]
    acc_lo = jnp.zeros((tm, HALF_D), F32)
    acc_hi = jnp.zeros((tm, HALF_D), F32)
    for k in range(TOP_K):
        lo, hi = _unpack_rows(_load_slabs(g_ref.at[0, k]))
        w = rw[:, k:k + 1]
        acc_lo = acc_lo + w * lo
        acc_hi = acc_hi + w * hi
    routed = jnp.concatenate([acc_lo, acc_hi], axis=1)
    x2 = base_ref[...] + mod_ref[0][5:6] * routed
    y_ref[...] = _rms(x2, fn_ref[...])


def _combine(g, base, rw_t, mod3, final_norm, seq):
    n = base.shape[0]
    tm = TM_COMBINE
    assert seq % tm == 0
    g4 = g.reshape(n // tm, TOP_K, tm * ROW_SLABS, LANES)
    return pl.pallas_call(
        _combine_kernel,
        out_shape=jax.ShapeDtypeStruct((n, D_MODEL), F32),
        grid=(n // tm,),
        in_specs=[pl.BlockSpec((1, TOP_K, tm * ROW_SLABS, LANES), lambda i: (i, 0, 0, 0)),
                  pl.BlockSpec((tm, D_MODEL), lambda i: (i, 0)),
                  pl.BlockSpec((tm, SUBLANES), lambda i: (i, 0)),
                  pl.BlockSpec((1, 6, D_MODEL), lambda i: (i * tm // seq, 0, 0)),
                  pl.BlockSpec((1, D_MODEL), lambda i: (0, 0))],
        out_specs=pl.BlockSpec((tm, D_MODEL), lambda i: (i, 0)),
        compiler_params=_cparams(1),
        name="moe_combine",
    )(g4, base, rw_t, mod3, final_norm.reshape(1, D_MODEL))


def _block_layout(tile_counts, n):
    bm = MOE_BM
    c = tile_counts[:, :, 0]
    ntiles = c.shape[0]
    counts = jnp.sum(c, axis=0)
    padded = (counts + bm - 1) // bm * bm
    earlier_e = np.tri(N_EXPERTS, k=-1, dtype=bool)
    pstart = jnp.sum(jnp.where(earlier_e, padded[None, :], 0), axis=1)
    pend = pstart + padded
    earlier_t = np.tri(ntiles, k=-1, dtype=bool)
    tile_base = pstart[None, :] + jnp.sum(jnp.where(earlier_t[:, :, None], c[None], 0), axis=1)
    nblk = -(-n * TOP_K // bm) + N_EXPERTS
    blk = jnp.arange(nblk, dtype=I32)
    blk_e = jnp.minimum(jnp.sum((pend[None, :] <= blk[:, None] * bm).astype(I32), axis=1),
                        N_EXPERTS - 1)
    onehot_e = blk_e[:, None] == jnp.arange(N_EXPERTS, dtype=I32)[None, :]
    end_e = jnp.sum(jnp.where(onehot_e, (pstart + counts)[None, :], 0), axis=1)
    blk_valid = jnp.clip(end_e - blk * bm, 0, bm).astype(I32)
    return tile_base.astype(I32)[:, :, None], blk_e.astype(I32), blk_valid, nblk * bm


def _trunk(x, mod3, p):
    nbatch, seq, _ = x.shape
    n = nbatch * seq
    x2 = x.reshape(n, D_MODEL)
    proj = _inproj(x2, mod3, p["norm1"], p["w_in"], seq)
    qa, kva = proj[:2]
    sga, sgb = proj[8:]
    (ya,) = _band_attention(qa, kva, p["bias_a"], p["sink_a"], nbatch=nbatch, seq=seq, dil=1,
                            halo=A_HALF_WINDOW, shared_kv=True, want_lse=False)
    outs, lses = [], []
    for gi, (w, d) in enumerate(B_GROUPS):
        q, kv = proj[2 + 2 * gi:4 + 2 * gi]
        o, lse = _band_attention(q, kv, p["bias_b"][gi], None, nbatch=nbatch, seq=seq, dil=d,
                                 halo=w // (2 * d), shared_kv=False, want_lse=True)
        outs.append(o)
        lses.append(lse)
    base, hp, ridx, rw, rank, tile_counts = _merge(
        x2, ya, outs, lses, sga, sgb, mod3, p["norm2"], p["w_pa"], p["w_pb"], p["w_o"], p["w_rt"],
        p["rbias"], p["ws_gate"], p["ws_up"], p["ws_down"], seq)
    assert TM_MERGE == TM_COMBINE
    tile_base, blk_e, blk_valid, n_rows = _block_layout(tile_counts, n)
    dest_rows = _plan(ridx, rank, tile_base, TM_MERGE).reshape(-1, SC_CHUNK)
    slabs = lambda a: a.reshape(-1, ROW_SLABS, LANES)
    flat = lambda a: a.reshape(-1, LANES)
    xs = _sc_dispatch(slabs(hp), dest_rows, n_rows)
    ys = _experts(flat(xs), blk_e, blk_valid, p["w_gu"], p["w_down"])
    g = _sc_collect(slabs(ys), dest_rows)
    y = _combine(g, base, rw.T, mod3, p["final_norm"], seq)
    return y.reshape(nbatch, seq, D_MODEL)


def kernel(x_prompt, x_sample, c_prompt, c_sample, rel_bias, w_ada, b_ada, norm1, w_in, sink, w_pa, w_pb, w_o, norm2, w_router, router_bias, w_gate, w_up, w_down, ws_gate, ws_up, ws_down, final_norm):
    assert w_ada.shape[0] == 1
    nbp = x_prompt.shape[0]
    mod = _ada(jnp.concatenate([c_prompt, c_sample], axis=0), w_ada[0], b_ada[0])
    mod3 = mod.reshape(-1, 6, D_MODEL)

    def pair_bias(heads, halo, dist_scale):
        kb = ATTN_BLOCK + 2 * halo
        tab = _bias_table(heads, ATTN_BLOCK, kb, halo, halo, dist_scale)
        return tab.reshape(-1, 2, ATTN_BLOCK, kb).transpose(0, 2, 1, 3).reshape(-1, ATTN_BLOCK, 2 * kb)

    bias_a = pair_bias(rel_bias[:, :A_Q_HEADS], A_HALF_WINDOW, 1)
    bias_b = []
    for gi, (w, d) in enumerate(B_GROUPS):
        h0 = A_Q_HEADS + gi * B_HEADS_PER_GROUP
        bias_b.append(pair_bias(rel_bias[:, h0:h0 + B_HEADS_PER_GROUP], w // (2 * d), d))
    p = {
        "norm1": norm1[0], "norm2": norm2[0], "final_norm": final_norm,
        "w_in": w_in[0].astype(BF16),
        "bias_a": bias_a,
        "sink_a": jnp.repeat(sink[0].astype(F32), HEAD_DIM).reshape(A_Q_HEADS // 2, 1, LANES),
        "bias_b": bias_b,
        "w_pa": w_pa[0].astype(BF16), "w_pb": w_pb[0].astype(BF16), "w_o": w_o[0].astype(BF16),
        "w_rt": w_router[0].T, "rbias": router_bias[0].reshape(N_EXPERTS, 1),
        "ws_gate": ws_gate[0].astype(BF16), "ws_up": ws_up[0].astype(BF16),
        "ws_down": ws_down[0].astype(BF16),
        "w_gu": jnp.concatenate([w_gate[0], w_up[0]], axis=-1).astype(BF16),
        "w_down": w_down[0].astype(BF16),
    }
    y_prompt = _trunk(x_prompt, mod3[:nbp], p)
    y_sample = _trunk(x_sample, mod3[nbp:], p)
    return (y_prompt, y_sample)
```

```python
import functools
import math

import jax
import jax.numpy as jnp
import numpy as np
from jax import lax
from jax.experimental import pallas as pl
from jax.experimental.pallas import tpu as pltpu
from jax.experimental.pallas import tpu_sc as plsc

F32 = jnp.float32
BF16 = jnp.bfloat16
U32 = jnp.uint32
I32 = jnp.int32

D_MODEL = 1024
HEAD_DIM = 64
A_Q_HEADS = 8
A_KV_HEADS = 2
A_HALF_WINDOW = 128
B_GROUPS = ((128, 1), (512, 4), (2048, 16))
B_HEADS_PER_GROUP = 4
N_BUCKETS = 32
MAX_DISTANCE = 1024
N_EXPERTS = 64
TOP_K = 6
N_EXPERT_GROUPS = 8
TOPK_GROUPS = 4
D_EXPERT = 256
ROUTED_SCALE = 2.5
RMS_EPS = 1e-6
NEG_INF = -1e30
REMOVED = -3e38

A_Q_W = A_Q_HEADS * HEAD_DIM
A_KV_W = A_KV_HEADS * HEAD_DIM
B_W = len(B_GROUPS) * B_HEADS_PER_GROUP * HEAD_DIM
B_OUT_W = B_HEADS_PER_GROUP * HEAD_DIM
D_IN = A_Q_W + 2 * A_KV_W + 3 * B_W + 2 * D_MODEL
HALF_D = D_MODEL // 2
LANES = 128
SUBLANES = 8
ROW_SLABS = HALF_D // LANES
SC_CORES = 2
SC_SUBCORES = 16
SC_CHUNK = 128

TM_INPROJ = 1024
TM_MERGE = 512
ATTN_BLOCK = 128
ATTN_ROWS = 512
MOE_BM = 512
TM_COMBINE = 512
VMEM_LIMIT = 60 * 1024 * 1024


def _cparams(n_axes):
    return pltpu.CompilerParams(
        dimension_semantics=("arbitrary",) * n_axes, vmem_limit_bytes=VMEM_LIMIT)


def _ada_kernel(c_ref, w_ref, b_ref, o_ref):
    c = c_ref[...]
    s = c * jax.nn.sigmoid(c)
    o_ref[...] = jnp.dot(s, w_ref[...], preferred_element_type=F32,
                         precision=lax.Precision.HIGHEST) + b_ref[...]


def _ada(c_all, w_ada, b_ada):
    nb = c_all.shape[0]
    return pl.pallas_call(
        _ada_kernel,
        out_shape=jax.ShapeDtypeStruct((nb, 6 * D_MODEL), F32),
        grid=(6,),
        in_specs=[pl.BlockSpec((nb, D_MODEL), lambda j: (0, 0)),
                  pl.BlockSpec((D_MODEL, D_MODEL), lambda j: (0, j)),
                  pl.BlockSpec((1, D_MODEL), lambda j: (0, j))],
        out_specs=pl.BlockSpec((nb, D_MODEL), lambda j: (0, j)),
        compiler_params=_cparams(1),
        name="ada",
    )(c_all, w_ada, b_ada.reshape(1, 6 * D_MODEL))


def _rms(x, g):
    return x * lax.rsqrt(jnp.mean(x * x, axis=-1, keepdims=True) + RMS_EPS) * g


def _inproj_kernel(x_ref, mod_ref, n1_ref, w_ref, qa, kva, *rest):
    q_b = rest[0:6:2]
    kv_b = rest[1:6:2]
    sga, sgb, scr = rest[6:]
    mod = mod_ref[0]
    h = _rms(x_ref[...], n1_ref[...]) * (1.0 + mod[1:2]) + mod[0:1]
    hb = h.astype(BF16)
    tm = hb.shape[0]
    scale = HEAD_DIM ** -0.5

    def proj(off, width):
        return jnp.dot(hb, w_ref[:, off:off + width], preferred_element_type=F32)

    qa[...] = (proj(0, A_Q_W) * scale).astype(BF16)
    for j, off in enumerate((A_Q_W, A_Q_W + A_KV_W)):
        r = proj(off, A_KV_W)
        kva[:, 2 * j * A_KV_W:(2 * j + 1) * A_KV_W] = r.astype(BF16)
        kva[:, (2 * j + 1) * A_KV_W:(2 * j + 2) * A_KV_W] = pltpu.roll(r, HEAD_DIM, 1).astype(BF16)
    off = A_Q_W + 2 * A_KV_W
    for t in range(3):
        for gi, (_, d) in enumerate(B_GROUPS):
            r = proj(off + t * B_W + gi * B_OUT_W, B_OUT_W)
            if t == 0:
                r = r * scale
            ref, width, base = (q_b[gi], B_OUT_W, 0) if t == 0 else (kv_b[gi], 2 * B_OUT_W,
                                                                     (t - 1) * B_OUT_W)
            if d == 1:
                ref[:, base:base + B_OUT_W] = r.astype(BF16)
            else:
                for c in range(B_OUT_W // LANES):
                    slot = (t * 2 + gi - 1) * (B_OUT_W // LANES) + c
                    scr[slot] = r[:, c * LANES:(c + 1) * LANES]
                    for res in range(d):
                        col = res * width + base + c * LANES
                        ref[:, col:col + LANES] = scr[
                            slot, pl.ds(res, tm // d, stride=d), :].astype(BF16)
    off += 3 * B_W
    sga[...] = jax.nn.sigmoid(proj(off, D_MODEL)).astype(BF16)
    sgb[...] = jax.nn.sigmoid(proj(off + D_MODEL, D_MODEL)).astype(BF16)


def _inproj(x2, mod3, norm1, w_in_bf, seq):
    n = x2.shape[0]
    tm = TM_INPROJ
    assert seq % tm == 0 and n % tm == 0
    row = lambda i: (i, 0)
    shapes = [(n, A_Q_W, tm), (n, 4 * A_KV_W, tm)]
    for _, d in B_GROUPS:
        shapes += [(n // d, d * B_OUT_W, tm // d), (n // d, d * 2 * B_OUT_W, tm // d)]
    shapes += [(n, D_MODEL, tm)] * 2
    return pl.pallas_call(
        _inproj_kernel,
        out_shape=[jax.ShapeDtypeStruct((r, c), BF16) for r, c, _ in shapes],
        grid=(n // tm,),
        in_specs=[pl.BlockSpec((tm, D_MODEL), row),
                  pl.BlockSpec((1, 6, D_MODEL), lambda i: (i * tm // seq, 0, 0)),
                  pl.BlockSpec((1, D_MODEL), lambda i: (0, 0)),
                  pl.BlockSpec((D_MODEL, D_IN), lambda i: (0, 0), pipeline_mode=pl.Buffered(1))],
        out_specs=[pl.BlockSpec((b, c), row) for _, c, b in shapes],
        scratch_shapes=[pltpu.VMEM((6 * B_OUT_W // LANES, tm, LANES), F32)],
        compiler_params=_cparams(1),
        name="inproj",
    )(x2, mod3, norm1.reshape(1, D_MODEL), w_in_bf)


def _rel_bucket_np(rel):
    half = N_BUCKETS // 2
    max_exact = half // 2
    n = np.abs(rel)
    large = max_exact + (np.log(np.maximum(n, 1) / max_exact) / math.log(MAX_DISTANCE / max_exact)
                         * (half - max_exact)).astype(np.int32)
    large = np.minimum(large, half - 1)
    return ((rel > 0).astype(np.int32) * half + np.where(n < max_exact, n, large)).astype(np.int32)


def _bias_table(rel_bias_heads, n_q, n_k, key_off, band, dist_scale):
    p = n_q + n_k
    rel = np.arange(p) - (n_q - 1) - key_off
    bucket = _rel_bucket_np(rel * dist_scale)
    t = jnp.where((np.abs(rel) <= band)[None], rel_bias_heads.astype(F32)[bucket].T, NEG_INF)
    big = jnp.tile(t, (1, n_q + 1))
    tab = big[:, n_q - 1:n_q - 1 + n_q * (p - 1)].reshape(-1, n_q, p - 1)
    return tab[:, :, :n_k]


def _pair_rhs(k_top, k_bot, v_top, v_bot):
    kb = k_top.shape[0]
    low = jnp.where(lax.broadcasted_iota(I32, (kb, LANES), 1) < HEAD_DIM, 1.0, 0.0).astype(BF16)
    high = jnp.where(lax.broadcasted_iota(I32, (kb, LANES), 1) < HEAD_DIM, 0.0, 1.0).astype(BF16)
    rhs_k = jnp.concatenate([k_top * low, k_bot * high], axis=0)
    rhs_v = jnp.concatenate([jnp.concatenate([v_top * low, low], axis=1),
                             jnp.concatenate([v_bot * high, high], axis=1)], axis=0)
    return rhs_k, rhs_v


def _pair_attention(q_pair, rhs_k, rhs_v, bias_pair, pen, sink_pair, want_lse):
    kb = rhs_k.shape[0] // 2
    s = lax.dot_general(q_pair, rhs_k, (((1,), (1,)), ((), ())), preferred_element_type=F32)
    s = s + bias_pair
    if pen is not None:
        s = s + pen
    s0, s1 = s[:, :kb], s[:, kb:]
    m0 = jnp.max(s0, axis=-1, keepdims=True)
    m1 = jnp.max(s1, axis=-1, keepdims=True)
    if sink_pair is not None:
        m0 = jnp.maximum(m0, sink_pair[:, 0:1])
        m1 = jnp.maximum(m1, sink_pair[:, HEAD_DIM:HEAD_DIM + 1])
    p = jnp.concatenate([jnp.exp(s0 - m0), jnp.exp(s1 - m1)], axis=1).astype(BF16)
    od = jnp.dot(p, rhs_v, preferred_element_type=F32)
    o, den = od[:, :LANES], od[:, LANES:]
    low = lax.broadcasted_iota(I32, o.shape, 1) < HEAD_DIM
    m_full = jnp.where(low, m0, m1)
    if sink_pair is not None:
        den = den + jnp.exp(sink_pair - m_full)
    return o / den, (m_full + jnp.log(den)) if want_lse else None


def _attn_kernel(*refs, halo, nblk, n_pairs, n_seqs, shared_kv, has_sink, want_lse, n_steps):
    it = iter(refs)
    q_ref = next(it)
    kv_prev, kv_cur, kv_next = next(it), next(it), next(it)
    bias_ref = next(it)
    sink_ref = next(it) if has_sink else None
    o_ref = next(it)
    lse_ref = next(it) if want_lse else None
    step = pl.program_id(2)
    blk = ATTN_BLOCK
    kb = blk + 2 * halo
    qw = n_pairs * LANES
    kv = jnp.concatenate([kv_prev[...], kv_cur[...], kv_next[...]], axis=0)
    col = lax.broadcasted_iota(I32, (1, kb), 1)
    for b in range(nblk):
        lo = jnp.where(step == 0, halo, 0) if b == 0 else 0
        hi = jnp.where(step == n_steps - 1, kb - halo, kb) if b == nblk - 1 else kb
        pen = None
        if b == 0 or b == nblk - 1:
            pen1 = jnp.where((col >= lo) & (col < hi), 0.0, NEG_INF)
            pen = jnp.concatenate([pen1, pen1], axis=1)
        rows = slice(b * blk, (b + 1) * blk)
        krows = slice(b * blk, b * blk + kb)
        lane_col = lambda j: kv[krows, j * LANES:(j + 1) * LANES]
        if shared_kv:
            k, k_sw, v, v_sw = (lane_col(j) for j in range(4))
            rhs = [_pair_rhs(k, k_sw, v, v_sw), _pair_rhs(k_sw, k, v_sw, v)]
        for s in range(n_seqs):
            for c in range(n_pairs):
                lanes = slice(s * qw + c * LANES, s * qw + (c + 1) * LANES)
                if shared_kv:
                    rhs_k, rhs_v = rhs[c // (n_pairs // 2)]
                else:
                    k, v = lane_col(2 * s * n_pairs + c), lane_col((2 * s + 1) * n_pairs + c)
                    rhs_k, rhs_v = _pair_rhs(k, k, v, v)
                o, lse = _pair_attention(q_ref[rows, lanes], rhs_k, rhs_v, bias_ref[c], pen,
                                         sink_ref[c] if has_sink else None, want_lse)
                o_ref[rows, lanes] = o.astype(o_ref.dtype)
                if want_lse:
                    lse_ref[rows, lanes] = lse


def _band_attention(q, kv, bias, sink, *, nbatch, seq, dil, halo, shared_kv, want_lse):
    n = nbatch * seq
    sub_len = seq // dil
    qw = q.shape[1] // dil
    kv_width = kv.shape[1] // dil
    rows = min(ATTN_ROWS, sub_len)
    n_seqs = min(dil, ATTN_ROWS // rows)
    assert sub_len % rows == 0 and rows % ATTN_BLOCK == 0 and rows % halo == 0 and dil % n_seqs == 0
    nq = sub_len // rows
    per = rows // halo
    total_halos = n // dil // halo
    cur = lambda b, r, i: (b * nq + i, r)
    prev = lambda b, r, i: (jnp.maximum((b * nq + i) * per - 1, 0), r)
    nxt = lambda b, r, i: (jnp.minimum((b * nq + i + 1) * per, total_halos - 1), r)
    const = lambda a: pl.BlockSpec(a.shape, lambda b, r, i: (0,) * a.ndim)
    qw_step, kvw_step = qw * n_seqs, kv_width * n_seqs

    in_specs = [pl.BlockSpec((rows, qw_step), cur), pl.BlockSpec((halo, kvw_step), prev),
                pl.BlockSpec((rows, kvw_step), cur), pl.BlockSpec((halo, kvw_step), nxt),
                const(bias)]
    args = [q, kv, kv, kv, bias]
    if sink is not None:
        in_specs.append(const(sink))
        args.append(sink)
    out_shape = [jax.ShapeDtypeStruct(q.shape, BF16)]
    out_specs = [pl.BlockSpec((rows, qw_step), cur)]
    if want_lse:
        out_shape.append(jax.ShapeDtypeStruct(q.shape, F32))
        out_specs.append(pl.BlockSpec((rows, qw_step), cur))
    return pl.pallas_call(
        functools.partial(_attn_kernel, halo=halo, nblk=rows // ATTN_BLOCK, n_pairs=qw // LANES,
                          n_seqs=n_seqs, shared_kv=shared_kv, has_sink=sink is not None,
                          want_lse=want_lse, n_steps=nq),
        out_shape=out_shape,
        grid=(nbatch, dil // n_seqs, nq),
        in_specs=in_specs,
        out_specs=out_specs,
        compiler_params=_cparams(3),
        name=f"band_attn_d{dil}",
    )(*args)


def _pack_rows(y):
    return pltpu.pack_elementwise([y[:, :HALF_D], y[:, HALF_D:]], packed_dtype=BF16)


def _store_slabs(ref, packed):
    t = packed.shape[0]
    for c in range(ROW_SLABS):
        ref[pl.ds(c, t, stride=ROW_SLABS), :] = packed[:, c * LANES:(c + 1) * LANES]


def _load_slabs(ref):
    t = ref.shape[0] // ROW_SLABS
    return jnp.concatenate([ref[pl.ds(c, t, stride=ROW_SLABS), :] for c in range(ROW_SLABS)], axis=1)


def _unpack_rows(p):
    return tuple(pltpu.unpack_elementwise(p, index=i, packed_dtype=BF16, unpacked_dtype=F32)
                 for i in range(2))


def _route(sel, scores):
    t = sel.shape[-1]
    per = N_EXPERTS // N_EXPERT_GROUPS
    shape3 = (N_EXPERT_GROUPS, per, t)
    sel3 = sel.reshape(shape3)
    sc3 = scores.reshape(shape3)
    iota_g = lax.broadcasted_iota(I32, shape3, 0)
    iota_m = lax.broadcasted_iota(I32, shape3, 1)
    iota_e = iota_g * per + iota_m
    m1 = jnp.max(sel3, axis=1, keepdims=True)
    i1 = jnp.min(jnp.where(sel3 == m1, iota_m, per), axis=1, keepdims=True)
    m2 = jnp.max(jnp.where(iota_m == i1, REMOVED, sel3), axis=1, keepdims=True)
    gscore = m1 + m2
    iota_g1 = lax.broadcasted_iota(I32, gscore.shape, 0)
    gmask = jnp.zeros(gscore.shape, jnp.bool_)
    for _ in range(TOPK_GROUPS):
        mx = jnp.max(gscore, axis=0, keepdims=True)
        ix = jnp.min(jnp.where(gscore == mx, iota_g1, N_EXPERT_GROUPS), axis=0, keepdims=True)
        hit = iota_g1 == ix
        gmask = gmask | hit
        gscore = jnp.where(hit, REMOVED, gscore)
    cur = jnp.where(gmask, sel3, NEG_INF)
    sum_all = lambda a: jnp.sum(jnp.sum(a, axis=1, keepdims=True), axis=0, keepdims=True)
    idxs, wts, hits = [], [], []
    for _ in range(TOP_K):
        mx = jnp.max(jnp.max(cur, axis=1, keepdims=True), axis=0, keepdims=True)
        cand = jnp.where(cur == mx, iota_e, N_EXPERTS)
        ix = jnp.min(jnp.min(cand, axis=1, keepdims=True), axis=0, keepdims=True)
        hit = iota_e == ix
        wts.append(sum_all(jnp.where(hit, sc3, 0.0)).reshape(1, t))
        cur = jnp.where(hit, REMOVED, cur)
        idxs.append(ix.reshape(1, t))
        hits.append(hit)
    wsum = wts[0]
    for w in wts[1:]:
        wsum = wsum + w
    wts = [w / wsum * ROUTED_SCALE for w in wts]
    onehot = hits[0].astype(F32)
    for hit in hits[1:]:
        onehot = onehot + hit.astype(F32)
    tri = (lax.broadcasted_iota(I32, (t, t), 0) <= lax.broadcasted_iota(I32, (t, t), 1))
    cum = jnp.dot(onehot.reshape(N_EXPERTS, t).astype(BF16), tri.astype(F32).astype(BF16),
                  preferred_element_type=F32)
    cum3 = cum.reshape(shape3) - 1.0
    ranks = [sum_all(jnp.where(hit, cum3, 0.0)).reshape(1, t).astype(I32) for hit in hits]
    counts = cum[:, t - 1:t].astype(I32)
    return idxs, wts, ranks, counts


def _merge_kernel(x_ref, ya_ref, o1, o2, o3, l1, l2, l3, sga_ref, sgb_ref, mod_ref, n2_ref,
                  wpa, wpb, wo, wrt, rbias, wsg, wsu, wsd,
                  base_ref, hp_ref, ridx_ref, rw_ref, rank_ref, cnt_ref, scr):
    mod = mod_ref[0]
    tm = x_ref.shape[0]

    def token_major(ref, slot, d):
        if d == 1:
            return ref[...].astype(F32)
        nc = B_OUT_W // LANES
        for c in range(nc):
            for res in range(d):
                col = res * B_OUT_W + c * LANES
                scr[slot * nc + c, pl.ds(res, tm // d, stride=d), :] = ref[
                    :, col:col + LANES].astype(F32)
        return jnp.concatenate([scr[slot * nc + c] for c in range(nc)], axis=1)

    dils = [d for _, d in B_GROUPS]
    os_ = [token_major(r, i, d) for i, (r, d) in enumerate(zip((o1, o2, o3), dils))]
    ls = [token_major(r, 3 + i, d) for i, (r, d) in enumerate(zip((l1, l2, l3), dils))]
    mx = jnp.maximum(jnp.maximum(ls[0], ls[1]), ls[2])
    es = [jnp.exp(l - mx) for l in ls]
    den = es[0] + es[1] + es[2]
    ob = (es[0] / den) * os_[0] + (es[1] / den) * os_[1] + (es[2] / den) * os_[2]
    pa = jnp.dot(ya_ref[...], wpa[...], preferred_element_type=F32)
    pb = jnp.dot(ob.astype(BF16), wpb[...], preferred_element_type=F32)
    merged = sga_ref[...].astype(F32) * pa + sgb_ref[...].astype(F32) * pb
    mix = jnp.dot(merged.astype(BF16), wo[...], preferred_element_type=F32)
    x1 = x_ref[...] + mod[2:3] * mix
    h2 = _rms(x1, n2_ref[...]) * (1.0 + mod[4:5]) + mod[3:4]
    logits = lax.dot_general(wrt[...], h2, (((1,), (1,)), ((), ())), preferred_element_type=F32,
                             precision=lax.Precision.HIGHEST)
    scores = jax.nn.sigmoid(logits)
    idxs, wts, ranks, counts = _route(scores + rbias[...], scores)
    for k in range(TOP_K):
        ridx_ref[k:k + 1, :] = idxs[k]
        rw_ref[k:k + 1, :] = wts[k]
        rank_ref[k:k + 1, :] = ranks[k]
    ridx_ref[TOP_K:, :] = jnp.zeros((SUBLANES - TOP_K, tm), I32)
    rw_ref[TOP_K:, :] = jnp.zeros((SUBLANES - TOP_K, tm), F32)
    rank_ref[TOP_K:, :] = jnp.zeros((SUBLANES - TOP_K, tm), I32)
    cnt_ref[0] = counts
    hb = h2.astype(BF16)
    g = jnp.dot(hb, wsg[...], preferred_element_type=F32)
    u = jnp.dot(hb, wsu[...], preferred_element_type=F32)
    act = (g * jax.nn.sigmoid(g) * u).astype(BF16)
    shared = jnp.dot(act, wsd[...], preferred_element_type=F32)
    base_ref[...] = x1 + mod[5:6] * shared
    _store_slabs(hp_ref, _pack_rows(h2))


def _merge(x2, ya, outs, lses, sga, sgb, mod3, norm2, wpa, wpb, wo, wrt, rbias, wsg, wsu, wsd, seq):
    n = x2.shape[0]
    tm = TM_MERGE
    assert seq % tm == 0
    row = lambda i: (i, 0)
    full = lambda a: pl.BlockSpec(a.shape, lambda i: (0,) * a.ndim)
    weights = [wpa, wpb, wo, wrt, rbias, wsg, wsu, wsd]
    group_specs = [pl.BlockSpec((tm // d, d * B_OUT_W), row) for _, d in B_GROUPS]
    lanes = lambda i: (0, i)
    return pl.pallas_call(
        _merge_kernel,
        out_shape=[jax.ShapeDtypeStruct((n, D_MODEL), F32),
                   jax.ShapeDtypeStruct((n * ROW_SLABS, LANES), U32),
                   jax.ShapeDtypeStruct((SUBLANES, n), I32),
                   jax.ShapeDtypeStruct((SUBLANES, n), F32),
                   jax.ShapeDtypeStruct((SUBLANES, n), I32),
                   jax.ShapeDtypeStruct((n // tm, N_EXPERTS, 1), I32)],
        grid=(n // tm,),
        in_specs=[pl.BlockSpec((tm, D_MODEL), row), pl.BlockSpec((tm, A_Q_W), row)]
                 + group_specs * 2
                 + [pl.BlockSpec((tm, D_MODEL), row)] * 2
                 + [pl.BlockSpec((1, 6, D_MODEL), lambda i: (i * tm // seq, 0, 0)),
                    pl.BlockSpec((1, D_MODEL), lambda i: (0, 0))]
                 + [full(w) for w in weights],
        out_specs=[pl.BlockSpec((tm, D_MODEL), row),
                   pl.BlockSpec((tm * ROW_SLABS, LANES), row),
                   pl.BlockSpec((SUBLANES, tm), lanes), pl.BlockSpec((SUBLANES, tm), lanes),
                   pl.BlockSpec((SUBLANES, tm), lanes),
                   pl.BlockSpec((1, N_EXPERTS, 1), lambda i: (i, 0, 0))],
        scratch_shapes=[pltpu.VMEM((6 * B_OUT_W // LANES, tm, LANES), F32)],
        compiler_params=_cparams(1),
        name="merge_route",
    )(x2, ya, *outs, *lses, sga, sgb, mod3, norm2.reshape(1, D_MODEL), *weights)


def _plan_kernel(ridx_ref, rank_ref, tb_ref, dest_ref):
    tb = tb_ref[0]
    t = ridx_ref.shape[1]
    iota_e = lax.broadcasted_iota(I32, (N_EXPERTS, t), 0)
    for k in range(TOP_K):
        base = jnp.sum(jnp.where(iota_e == ridx_ref[k:k + 1, :], tb, 0), axis=0, keepdims=True)
        dest_ref[0, :, k * t:(k + 1) * t] = base + rank_ref[k:k + 1, :]


def _plan(ridx, rank, tile_base, tm):
    n = ridx.shape[1]
    lanes = lambda i: (0, i)
    return pl.pallas_call(
        _plan_kernel,
        out_shape=jax.ShapeDtypeStruct((n // tm, 1, TOP_K * tm), I32),
        grid=(n // tm,),
        in_specs=[pl.BlockSpec((SUBLANES, tm), lanes), pl.BlockSpec((SUBLANES, tm), lanes),
                  pl.BlockSpec((1, N_EXPERTS, 1), lambda i: (i, 0, 0))],
        out_specs=pl.BlockSpec((1, 1, TOP_K * tm), lambda i: (i, 0, 0)),
        compiler_params=_cparams(1),
        name="moe_plan",
    )(ridx, rank, tile_base)


def _sc_mesh():
    return plsc.VectorSubcoreMesh(core_axis_name="core", subcore_axis_name="subcore",
                                  num_cores=SC_CORES, num_subcores=SC_SUBCORES)


def _sc_worker():
    return lax.axis_index("subcore") * SC_CORES + lax.axis_index("core")


def _sc_dispatch(hp, dest_rows, n_rows):
    n = hp.shape[0]
    halves = TM_MERGE // SC_CHUNK
    workers = SC_CORES * SC_SUBCORES
    tiles_per_w = n // TM_MERGE // workers
    idx_per_w = tiles_per_w * TOP_K * halves
    assert tiles_per_w * workers * TM_MERGE == n

    @functools.partial(
        pl.kernel, mesh=_sc_mesh(),
        out_type=jax.ShapeDtypeStruct((n_rows, ROW_SLABS, LANES), U32),
        scratch_types=[pltpu.VMEM((idx_per_w, SC_CHUNK), I32),
                       pltpu.VMEM((SC_CHUNK, ROW_SLABS, LANES), U32),
                       pltpu.SemaphoreType.DMA],
        name="moe_dispatch_sc")
    def scatter(hp_hbm, idx_hbm, xs_hbm, idx_v, rows_v, sem):
        wid = _sc_worker()
        pltpu.sync_copy(idx_hbm.at[pl.ds(wid * idx_per_w, idx_per_w)], idx_v)

        @pl.loop(0, tiles_per_w * halves)
        def _(j):
            tile = j // halves
            h = j - tile * halves
            tok = (wid * tiles_per_w + tile) * TM_MERGE + h * SC_CHUNK
            pltpu.sync_copy(hp_hbm.at[pl.ds(tok, SC_CHUNK)], rows_v)
            for k in range(TOP_K):
                row = (tile * TOP_K + k) * halves + h
                pltpu.async_copy(rows_v, xs_hbm.at[idx_v.at[row]], sem).wait()

    return scatter(hp, dest_rows)


def _sc_collect(ys, dest_rows):
    n_chunks = dest_rows.shape[0]
    workers = SC_CORES * SC_SUBCORES
    per_w = n_chunks // workers
    assert per_w * workers == n_chunks

    @functools.partial(
        pl.kernel, mesh=_sc_mesh(),
        out_type=jax.ShapeDtypeStruct((n_chunks * SC_CHUNK, ROW_SLABS, LANES), U32),
        scratch_types=[pltpu.VMEM((per_w, SC_CHUNK), I32),
                       pltpu.VMEM((SC_CHUNK, ROW_SLABS, LANES), U32),
                       pltpu.SemaphoreType.DMA],
        name="moe_collect_sc")
    def gather(ys_hbm, idx_hbm, out_hbm, idx_v, rows_v, sem):
        wid = _sc_worker()
        base = wid * per_w
        pltpu.sync_copy(idx_hbm.at[pl.ds(base, per_w)], idx_v)

        @pl.loop(0, per_w)
        def _(j):
            pltpu.async_copy(ys_hbm.at[idx_v.at[j]], rows_v, sem).wait()
            pltpu.sync_copy(rows_v, out_hbm.at[pl.ds((base + j) * SC_CHUNK, SC_CHUNK)])

    return gather(ys, dest_rows)


def _expert_kernel(blk_e_ref, valid_ref, xs_ref, wgu_ref, wd_ref, ys_ref):
    del blk_e_ref
    valid = valid_ref[pl.program_id(0)]

    @pl.when(valid > 0)
    def _():
        rows = lax.broadcasted_iota(I32, (xs_ref.shape[0] // ROW_SLABS, 1), 0)
        lo, hi = _unpack_rows(jnp.where(rows < valid, _load_slabs(xs_ref), jnp.uint32(0)))
        wgu = wgu_ref[0]
        gu = jnp.dot(lo.astype(BF16), wgu[:HALF_D], preferred_element_type=F32) \
            + jnp.dot(hi.astype(BF16), wgu[HALF_D:], preferred_element_type=F32)
        g = gu[:, :D_EXPERT]
        u = gu[:, D_EXPERT:]
        act = (g * jax.nn.sigmoid(g) * u).astype(BF16)
        _store_slabs(ys_ref, _pack_rows(jnp.dot(act, wd_ref[0], preferred_element_type=F32)))

    @pl.when(valid <= 0)
    def _():
        ys_ref[...] = jnp.zeros(ys_ref.shape, U32)


def _experts(xs, blk_e, blk_valid, wgu, wd):
    n_rows = xs.shape[0] // ROW_SLABS
    bm = MOE_BM
    nblk = n_rows // bm
    return pl.pallas_call(
        _expert_kernel,
        out_shape=jax.ShapeDtypeStruct((n_rows * ROW_SLABS, LANES), U32),
        grid_spec=pltpu.PrefetchScalarGridSpec(
            num_scalar_prefetch=2,
            grid=(nblk,),
            in_specs=[pl.BlockSpec((bm * ROW_SLABS, LANES), lambda i, be, nu: (i, 0)),
                      pl.BlockSpec((1, D_MODEL, 2 * D_EXPERT), lambda i, be, nu: (be[i], 0, 0)),
                      pl.BlockSpec((1, D_EXPERT, D_MODEL), lambda i, be, nu: (be[i], 0, 0))],
            out_specs=pl.BlockSpec((bm * ROW_SLABS, LANES), lambda i, be, nu: (i, 0))),
        compiler_params=_cparams(1),
        name="moe_experts",
    )(blk_e, blk_valid, xs, wgu, wd)


def _combine_kernel(g_ref, base_ref, rw_ref, mod_ref, fn_ref, y_ref):
    tm = base_ref.shape[0]
    rw = rw_ref[...]
    acc_lo = jnp.zeros((tm, HALF_D), F32)
    acc_hi = jnp.zeros((tm, HALF_D), F32)
    for k in range(TOP_K):
        lo, hi = _unpack_rows(_load_slabs(g_ref.at[0, k]))
        w = rw[:, k:k + 1]
        acc_lo = acc_lo + w * lo
        acc_hi = acc_hi + w * hi
    routed = jnp.concatenate([acc_lo, acc_hi], axis=1)
    x2 = base_ref[...] + mod_ref[0][5:6] * routed
    y_ref[...] = _rms(x2, fn_ref[...])


def _combine(g, base, rw_t, mod3, final_norm, seq):
    n = base.shape[0]
    tm = TM_COMBINE
    assert seq % tm == 0
    g4 = g.reshape(n // tm, TOP_K, tm * ROW_SLABS, LANES)
    return pl.pallas_call(
        _combine_kernel,
        out_shape=jax.ShapeDtypeStruct((n, D_MODEL), F32),
        grid=(n // tm,),
        in_specs=[pl.BlockSpec((1, TOP_K, tm * ROW_SLABS, LANES), lambda i: (i, 0, 0, 0)),
                  pl.BlockSpec((tm, D_MODEL), lambda i: (i, 0)),
                  pl.BlockSpec((tm, SUBLANES), lambda i: (i, 0)),
                  pl.BlockSpec((1, 6, D_MODEL), lambda i: (i * tm // seq, 0, 0)),
                  pl.BlockSpec((1, D_MODEL), lambda i: (0, 0))],
        out_specs=pl.BlockSpec((tm, D_MODEL), lambda i: (i, 0)),
        compiler_params=_cparams(1),
        name="moe_combine",
    )(g4, base, rw_t, mod3, final_norm.reshape(1, D_MODEL))


def _block_layout(tile_counts, n):
    bm = MOE_BM
    c = tile_counts[:, :, 0]
    ntiles = c.shape[0]
    counts = jnp.sum(c, axis=0)
    padded = (counts + bm - 1) // bm * bm
    earlier_e = np.tri(N_EXPERTS, k=-1, dtype=bool)
    pstart = jnp.sum(jnp.where(earlier_e, padded[None, :], 0), axis=1)
    pend = pstart + padded
    earlier_t = np.tri(ntiles, k=-1, dtype=bool)
    tile_base = pstart[None, :] + jnp.sum(jnp.where(earlier_t[:, :, None], c[None], 0), axis=1)
    nblk = -(-n * TOP_K // bm) + N_EXPERTS
    blk = jnp.arange(nblk, dtype=I32)
    blk_e = jnp.minimum(jnp.sum((pend[None, :] <= blk[:, None] * bm).astype(I32), axis=1),
                        N_EXPERTS - 1)
    onehot_e = blk_e[:, None] == jnp.arange(N_EXPERTS, dtype=I32)[None, :]
    end_e = jnp.sum(jnp.where(onehot_e, (pstart + counts)[None, :], 0), axis=1)
    blk_valid = jnp.clip(end_e - blk * bm, 0, bm).astype(I32)
    return tile_base.astype(I32)[:, :, None], blk_e.astype(I32), blk_valid, nblk * bm


def _trunk(x, mod3, p):
    nbatch, seq, _ = x.shape
    n = nbatch * seq
    x2 = x.reshape(n, D_MODEL)
    proj = _inproj(x2, mod3, p["norm1"], p["w_in"], seq)
    qa, kva = proj[:2]
    sga, sgb = proj[8:]
    (ya,) = _band_attention(qa, kva, p["bias_a"], p["sink_a"], nbatch=nbatch, seq=seq, dil=1,
                            halo=A_HALF_WINDOW, shared_kv=True, want_lse=False)
    outs, lses = [], []
    for gi, (w, d) in enumerate(B_GROUPS):
        q, kv = proj[2 + 2 * gi:4 + 2 * gi]
        o, lse = _band_attention(q, kv, p["bias_b"][gi], None, nbatch=nbatch, seq=seq, dil=d,
                                 halo=w // (2 * d), shared_kv=False, want_lse=True)
        outs.append(o)
        lses.append(lse)
    base, hp, ridx, rw, rank, tile_counts = _merge(
        x2, ya, outs, lses, sga, sgb, mod3, p["norm2"], p["w_pa"], p["w_pb"], p["w_o"], p["w_rt"],
        p["rbias"], p["ws_gate"], p["ws_up"], p["ws_down"], seq)
    assert TM_MERGE == TM_COMBINE
    tile_base, blk_e, blk_valid, n_rows = _block_layout(tile_counts, n)
    dest_rows = _plan(ridx, rank, tile_base, TM_MERGE).reshape(-1, SC_CHUNK)
    slabs = lambda a: a.reshape(-1, ROW_SLABS, LANES)
    flat = lambda a: a.reshape(-1, LANES)
    xs = _sc_dispatch(slabs(hp), dest_rows, n_rows)
    ys = _experts(flat(xs), blk_e, blk_valid, p["w_gu"], p["w_down"])
    g = _sc_collect(slabs(ys), dest_rows)
    y = _combine(g, base, rw.T, mod3, p["final_norm"], seq)
    return y.reshape(nbatch, seq, D_MODEL)


def kernel(x_prompt, x_sample, c_prompt, c_sample, rel_bias, w_ada, b_ada, norm1, w_in, sink, w_pa, w_pb, w_o, norm2, w_router, router_bias, w_gate, w_up, w_down, ws_gate, ws_up, ws_down, final_norm):
    assert w_ada.shape[0] == 1
    nbp = x_prompt.shape[0]
    mod = _ada(jnp.concatenate([c_prompt, c_sample], axis=0), w_ada[0], b_ada[0])
    mod3 = mod.reshape(-1, 6, D_MODEL)

    def pair_bias(heads, halo, dist_scale):
        kb = ATTN_BLOCK + 2 * halo
        tab = _bias_table(heads, ATTN_BLOCK, kb, halo, halo, dist_scale)
        return tab.reshape(-1, 2, ATTN_BLOCK, kb).transpose(0, 2, 1, 3).reshape(-1, ATTN_BLOCK, 2 * kb)

    bias_a = pair_bias(rel_bias[:, :A_Q_HEADS], A_HALF_WINDOW, 1)
    bias_b = []
    for gi, (w, d) in enumerate(B_GROUPS):
        h0 = A_Q_HEADS + gi * B_HEADS_PER_GROUP
        bias_b.append(pair_bias(rel_bias[:, h0:h0 + B_HEADS_PER_GROUP], w // (2 * d), d))
    p = {
        "norm1": norm1[0], "norm2": norm2[0], "final_norm": final_norm,
        "w_in": w_in[0].astype(BF16),
        "bias_a": bias_a,
        "sink_a": jnp.repeat(sink[0].astype(F32), HEAD_DIM).reshape(A_Q_HEADS // 2, 1, LANES),
        "bias_b": bias_b,
        "w_pa": w_pa[0].astype(BF16), "w_pb": w_pb[0].astype(BF16), "w_o": w_o[0].astype(BF16),
        "w_rt": w_router[0].T, "rbias": router_bias[0].reshape(N_EXPERTS, 1),
        "ws_gate": ws_gate[0].astype(BF16), "ws_up": ws_up[0].astype(BF16),
        "ws_down": ws_down[0].astype(BF16),
        "w_gu": jnp.concatenate([w_gate[0], w_up[0]], axis=-1).astype(BF16),
        "w_down": w_down[0].astype(BF16),
    }
    y_prompt = _trunk(x_prompt, mod3[:nbp], p)
    y_sample = _trunk(x_sample, mod3[nbp:], p)
    return (y_prompt, y_sample)
```

```python
import functools
import math

import jax
import jax.numpy as jnp
import numpy as np
from jax import lax
from jax.experimental import pallas as pl
from jax.experimental.pallas import tpu as pltpu
from jax.experimental.pallas import tpu_sc as plsc

F32 = jnp.float32
BF16 = jnp.bfloat16
U32 = jnp.uint32
I32 = jnp.int32

D_MODEL = 1024
HEAD_DIM = 64
A_Q_HEADS = 8
A_KV_HEADS = 2
A_HALF_WINDOW = 128
B_GROUPS = ((128, 1), (512, 4), (2048, 16))
B_HEADS_PER_GROUP = 4
N_BUCKETS = 32
MAX_DISTANCE = 1024
N_EXPERTS = 64
TOP_K = 6
N_EXPERT_GROUPS = 8
TOPK_GROUPS = 4
D_EXPERT = 256
ROUTED_SCALE = 2.5
RMS_EPS = 1e-6
NEG_INF = -1e30
REMOVED = -3e38

A_Q_W = A_Q_HEADS * HEAD_DIM
A_KV_W = A_KV_HEADS * HEAD_DIM
B_W = len(B_GROUPS) * B_HEADS_PER_GROUP * HEAD_DIM
B_OUT_W = B_HEADS_PER_GROUP * HEAD_DIM
D_IN = A_Q_W + 2 * A_KV_W + 3 * B_W + 2 * D_MODEL
HALF_D = D_MODEL // 2
LANES = 128
SUBLANES = 8
ROW_SLABS = HALF_D // LANES
SC_CORES = 2
SC_SUBCORES = 16
SC_CHUNK = 128

TM_INPROJ = 1024
TM_MERGE = 512
ATTN_BLOCK = 128
ATTN_ROWS = 512
MOE_BM = 1024
TM_COMBINE = 512
VMEM_LIMIT = 60 * 1024 * 1024


def _cparams(n_axes):
    return pltpu.CompilerParams(
        dimension_semantics=("arbitrary",) * n_axes, vmem_limit_bytes=VMEM_LIMIT)


def _ada_kernel(c_ref, w_ref, b_ref, o_ref):
    c = c_ref[...]
    s = c * jax.nn.sigmoid(c)
    o_ref[...] = jnp.dot(s, w_ref[...], preferred_element_type=F32,
                         precision=lax.Precision.HIGHEST) + b_ref[...]


def _ada(c_all, w_ada, b_ada):
    nb = c_all.shape[0]
    return pl.pallas_call(
        _ada_kernel,
        out_shape=jax.ShapeDtypeStruct((nb, 6 * D_MODEL), F32),
        grid=(6,),
        in_specs=[pl.BlockSpec((nb, D_MODEL), lambda j: (0, 0)),
                  pl.BlockSpec((D_MODEL, D_MODEL), lambda j: (0, j)),
                  pl.BlockSpec((1, D_MODEL), lambda j: (0, j))],
        out_specs=pl.BlockSpec((nb, D_MODEL), lambda j: (0, j)),
        compiler_params=_cparams(1),
        name="ada",
    )(c_all, w_ada, b_ada.reshape(1, 6 * D_MODEL))


def _rms(x, g):
    return x * lax.rsqrt(jnp.mean(x * x, axis=-1, keepdims=True) + RMS_EPS) * g


def _inproj_kernel(x_ref, mod_ref, n1_ref, w_ref, qa, kva, *rest):
    q_b = rest[0:6:2]
    kv_b = rest[1:6:2]
    sga, sgb, scr = rest[6:]
    mod = mod_ref[0]
    h = _rms(x_ref[...], n1_ref[...]) * (1.0 + mod[1:2]) + mod[0:1]
    hb = h.astype(BF16)
    tm = hb.shape[0]
    scale = HEAD_DIM ** -0.5

    def proj(off, width):
        return jnp.dot(hb, w_ref[:, off:off + width], preferred_element_type=F32)

    qa[...] = (proj(0, A_Q_W) * scale).astype(BF16)
    for j, off in enumerate((A_Q_W, A_Q_W + A_KV_W)):
        r = proj(off, A_KV_W)
        kva[:, 2 * j * A_KV_W:(2 * j + 1) * A_KV_W] = r.astype(BF16)
        kva[:, (2 * j + 1) * A_KV_W:(2 * j + 2) * A_KV_W] = pltpu.roll(r, HEAD_DIM, 1).astype(BF16)
    off = A_Q_W + 2 * A_KV_W
    for t in range(3):
        for gi, (_, d) in enumerate(B_GROUPS):
            r = proj(off + t * B_W + gi * B_OUT_W, B_OUT_W)
            if t == 0:
                r = r * scale
            ref, width, base = (q_b[gi], B_OUT_W, 0) if t == 0 else (kv_b[gi], 2 * B_OUT_W,
                                                                     (t - 1) * B_OUT_W)
            if d == 1:
                ref[:, base:base + B_OUT_W] = r.astype(BF16)
            else:
                for c in range(B_OUT_W // LANES):
                    slot = (t * 2 + gi - 1) * (B_OUT_W // LANES) + c
                    scr[slot] = r[:, c * LANES:(c + 1) * LANES]
                    for res in range(d):
                        col = res * width + base + c * LANES
                        ref[:, col:col + LANES] = scr[
                            slot, pl.ds(res, tm // d, stride=d), :].astype(BF16)
    off += 3 * B_W
    sga[...] = jax.nn.sigmoid(proj(off, D_MODEL)).astype(BF16)
    sgb[...] = jax.nn.sigmoid(proj(off + D_MODEL, D_MODEL)).astype(BF16)


def _inproj(x2, mod3, norm1, w_in_bf, seq):
    n = x2.shape[0]
    tm = TM_INPROJ
    assert seq % tm == 0 and n % tm == 0
    row = lambda i: (i, 0)
    shapes = [(n, A_Q_W, tm), (n, 4 * A_KV_W, tm)]
    for _, d in B_GROUPS:
        shapes += [(n // d, d * B_OUT_W, tm // d), (n // d, d * 2 * B_OUT_W, tm // d)]
    shapes += [(n, D_MODEL, tm)] * 2
    return pl.pallas_call(
        _inproj_kernel,
        out_shape=[jax.ShapeDtypeStruct((r, c), BF16) for r, c, _ in shapes],
        grid=(n // tm,),
        in_specs=[pl.BlockSpec((tm, D_MODEL), row),
                  pl.BlockSpec((1, 6, D_MODEL), lambda i: (i * tm // seq, 0, 0)),
                  pl.BlockSpec((1, D_MODEL), lambda i: (0, 0)),
                  pl.BlockSpec((D_MODEL, D_IN), lambda i: (0, 0), pipeline_mode=pl.Buffered(1))],
        out_specs=[pl.BlockSpec((b, c), row) for _, c, b in shapes],
        scratch_shapes=[pltpu.VMEM((6 * B_OUT_W // LANES, tm, LANES), F32)],
        compiler_params=_cparams(1),
        name="inproj",
    )(x2, mod3, norm1.reshape(1, D_MODEL), w_in_bf)


def _rel_bucket_np(rel):
    half = N_BUCKETS // 2
    max_exact = half // 2
    n = np.abs(rel)
    large = max_exact + (np.log(np.maximum(n, 1) / max_exact) / math.log(MAX_DISTANCE / max_exact)
                         * (half - max_exact)).astype(np.int32)
    large = np.minimum(large, half - 1)
    return ((rel > 0).astype(np.int32) * half + np.where(n < max_exact, n, large)).astype(np.int32)


def _bias_table(rel_bias_heads, n_q, n_k, key_off, band, dist_scale):
    p = n_q + n_k
    rel = np.arange(p) - (n_q - 1) - key_off
    bucket = _rel_bucket_np(rel * dist_scale)
    t = jnp.where((np.abs(rel) <= band)[None], rel_bias_heads.astype(F32)[bucket].T, NEG_INF)
    big = jnp.tile(t, (1, n_q + 1))
    tab = big[:, n_q - 1:n_q - 1 + n_q * (p - 1)].reshape(-1, n_q, p - 1)
    return tab[:, :, :n_k]


def _pair_rhs(k_top, k_bot, v_top, v_bot):
    kb = k_top.shape[0]
    low = jnp.where(lax.broadcasted_iota(I32, (kb, LANES), 1) < HEAD_DIM, 1.0, 0.0).astype(BF16)
    high = jnp.where(lax.broadcasted_iota(I32, (kb, LANES), 1) < HEAD_DIM, 0.0, 1.0).astype(BF16)
    rhs_k = jnp.concatenate([k_top * low, k_bot * high], axis=0)
    rhs_v = jnp.concatenate([jnp.concatenate([v_top * low, low], axis=1),
                             jnp.concatenate([v_bot * high, high], axis=1)], axis=0)
    return rhs_k, rhs_v


def _pair_attention(q_pair, rhs_k, rhs_v, bias_pair, pen, sink_pair, want_lse):
    kb = rhs_k.shape[0] // 2
    s = lax.dot_general(q_pair, rhs_k, (((1,), (1,)), ((), ())), preferred_element_type=F32)
    s = s + bias_pair
    if pen is not None:
        s = s + pen
    s0, s1 = s[:, :kb], s[:, kb:]
    m0 = jnp.max(s0, axis=-1, keepdims=True)
    m1 = jnp.max(s1, axis=-1, keepdims=True)
    if sink_pair is not None:
        m0 = jnp.maximum(m0, sink_pair[:, 0:1])
        m1 = jnp.maximum(m1, sink_pair[:, HEAD_DIM:HEAD_DIM + 1])
    p = jnp.concatenate([jnp.exp(s0 - m0), jnp.exp(s1 - m1)], axis=1).astype(BF16)
    od = jnp.dot(p, rhs_v, preferred_element_type=F32)
    o, den = od[:, :LANES], od[:, LANES:]
    low = lax.broadcasted_iota(I32, o.shape, 1) < HEAD_DIM
    m_full = jnp.where(low, m0, m1)
    if sink_pair is not None:
        den = den + jnp.exp(sink_pair - m_full)
    return o / den, (m_full + jnp.log(den)) if want_lse else None


def _attn_kernel(*refs, halo, nblk, n_pairs, n_seqs, shared_kv, has_sink, want_lse, n_steps):
    it = iter(refs)
    q_ref = next(it)
    kv_prev, kv_cur, kv_next = next(it), next(it), next(it)
    bias_ref = next(it)
    sink_ref = next(it) if has_sink else None
    o_ref = next(it)
    lse_ref = next(it) if want_lse else None
    step = pl.program_id(2)
    blk = ATTN_BLOCK
    kb = blk + 2 * halo
    qw = n_pairs * LANES
    kv = jnp.concatenate([kv_prev[...], kv_cur[...], kv_next[...]], axis=0)
    col = lax.broadcasted_iota(I32, (1, kb), 1)
    for b in range(nblk):
        lo = jnp.where(step == 0, halo, 0) if b == 0 else 0
        hi = jnp.where(step == n_steps - 1, kb - halo, kb) if b == nblk - 1 else kb
        pen = None
        if b == 0 or b == nblk - 1:
            pen1 = jnp.where((col >= lo) & (col < hi), 0.0, NEG_INF)
            pen = jnp.concatenate([pen1, pen1], axis=1)
        rows = slice(b * blk, (b + 1) * blk)
        krows = slice(b * blk, b * blk + kb)
        lane_col = lambda j: kv[krows, j * LANES:(j + 1) * LANES]
        if shared_kv:
            k, k_sw, v, v_sw = (lane_col(j) for j in range(4))
            rhs = [_pair_rhs(k, k_sw, v, v_sw), _pair_rhs(k_sw, k, v_sw, v)]
        for s in range(n_seqs):
            for c in range(n_pairs):
                lanes = slice(s * qw + c * LANES, s * qw + (c + 1) * LANES)
                if shared_kv:
                    rhs_k, rhs_v = rhs[c // (n_pairs // 2)]
                else:
                    k, v = lane_col(2 * s * n_pairs + c), lane_col((2 * s + 1) * n_pairs + c)
                    rhs_k, rhs_v = _pair_rhs(k, k, v, v)
                o, lse = _pair_attention(q_ref[rows, lanes], rhs_k, rhs_v, bias_ref[c], pen,
                                         sink_ref[c] if has_sink else None, want_lse)
                o_ref[rows, lanes] = o.astype(o_ref.dtype)
                if want_lse:
                    lse_ref[rows, lanes] = lse


def _band_attention(q, kv, bias, sink, *, nbatch, seq, dil, halo, shared_kv, want_lse):
    n = nbatch * seq
    sub_len = seq // dil
    qw = q.shape[1] // dil
    kv_width = kv.shape[1] // dil
    rows = min(ATTN_ROWS, sub_len)
    n_seqs = min(dil, ATTN_ROWS // rows)
    assert sub_len % rows == 0 and rows % ATTN_BLOCK == 0 and rows % halo == 0 and dil % n_seqs == 0
    nq = sub_len // rows
    per = rows // halo
    total_halos = n // dil // halo
    cur = lambda b, r, i: (b * nq + i, r)
    prev = lambda b, r, i: (jnp.maximum((b * nq + i) * per - 1, 0), r)
    nxt = lambda b, r, i: (jnp.minimum((b * nq + i + 1) * per, total_halos - 1), r)
    const = lambda a: pl.BlockSpec(a.shape, lambda b, r, i: (0,) * a.ndim)
    qw_step, kvw_step = qw * n_seqs, kv_width * n_seqs

    in_specs = [pl.BlockSpec((rows, qw_step), cur), pl.BlockSpec((halo, kvw_step), prev),
                pl.BlockSpec((rows, kvw_step), cur), pl.BlockSpec((halo, kvw_step), nxt),
                const(bias)]
    args = [q, kv, kv, kv, bias]
    if sink is not None:
        in_specs.append(const(sink))
        args.append(sink)
    out_shape = [jax.ShapeDtypeStruct(q.shape, BF16)]
    out_specs = [pl.BlockSpec((rows, qw_step), cur)]
    if want_lse:
        out_shape.append(jax.ShapeDtypeStruct(q.shape, F32))
        out_specs.append(pl.BlockSpec((rows, qw_step), cur))
    return pl.pallas_call(
        functools.partial(_attn_kernel, halo=halo, nblk=rows // ATTN_BLOCK, n_pairs=qw // LANES,
                          n_seqs=n_seqs, shared_kv=shared_kv, has_sink=sink is not None,
                          want_lse=want_lse, n_steps=nq),
        out_shape=out_shape,
        grid=(nbatch, dil // n_seqs, nq),
        in_specs=in_specs,
        out_specs=out_specs,
        compiler_params=_cparams(3),
        name=f"band_attn_d{dil}",
    )(*args)


def _pack_rows(y):
    return pltpu.pack_elementwise([y[:, :HALF_D], y[:, HALF_D:]], packed_dtype=BF16)


def _store_slabs(ref, packed):
    t = packed.shape[0]
    for c in range(ROW_SLABS):
        ref[pl.ds(c, t, stride=ROW_SLABS), :] = packed[:, c * LANES:(c + 1) * LANES]


def _load_slabs(ref):
    t = ref.shape[0] // ROW_SLABS
    return jnp.concatenate([ref[pl.ds(c, t, stride=ROW_SLABS), :] for c in range(ROW_SLABS)], axis=1)


def _unpack_rows(p):
    return tuple(pltpu.unpack_elementwise(p, index=i, packed_dtype=BF16, unpacked_dtype=F32)
                 for i in range(2))


def _route(sel, scores):
    t = sel.shape[-1]
    per = N_EXPERTS // N_EXPERT_GROUPS
    shape3 = (N_EXPERT_GROUPS, per, t)
    sel3 = sel.reshape(shape3)
    sc3 = scores.reshape(shape3)
    iota_g = lax.broadcasted_iota(I32, shape3, 0)
    iota_m = lax.broadcasted_iota(I32, shape3, 1)
    iota_e = iota_g * per + iota_m
    m1 = jnp.max(sel3, axis=1, keepdims=True)
    i1 = jnp.min(jnp.where(sel3 == m1, iota_m, per), axis=1, keepdims=True)
    m2 = jnp.max(jnp.where(iota_m == i1, REMOVED, sel3), axis=1, keepdims=True)
    gscore = m1 + m2
    iota_g1 = lax.broadcasted_iota(I32, gscore.shape, 0)
    gmask = jnp.zeros(gscore.shape, jnp.bool_)
    for _ in range(TOPK_GROUPS):
        mx = jnp.max(gscore, axis=0, keepdims=True)
        ix = jnp.min(jnp.where(gscore == mx, iota_g1, N_EXPERT_GROUPS), axis=0, keepdims=True)
        hit = iota_g1 == ix
        gmask = gmask | hit
        gscore = jnp.where(hit, REMOVED, gscore)
    cur = jnp.where(gmask, sel3, NEG_INF)
    sum_all = lambda a: jnp.sum(jnp.sum(a, axis=1, keepdims=True), axis=0, keepdims=True)
    idxs, wts, hits = [], [], []
    for _ in range(TOP_K):
        mx = jnp.max(jnp.max(cur, axis=1, keepdims=True), axis=0, keepdims=True)
        cand = jnp.where(cur == mx, iota_e, N_EXPERTS)
        ix = jnp.min(jnp.min(cand, axis=1, keepdims=True), axis=0, keepdims=True)
        hit = iota_e == ix
        wts.append(sum_all(jnp.where(hit, sc3, 0.0)).reshape(1, t))
        cur = jnp.where(hit, REMOVED, cur)
        idxs.append(ix.reshape(1, t))
        hits.append(hit)
    wsum = wts[0]
    for w in wts[1:]:
        wsum = wsum + w
    wts = [w / wsum * ROUTED_SCALE for w in wts]
    onehot = hits[0].astype(F32)
    for hit in hits[1:]:
        onehot = onehot + hit.astype(F32)
    tri = (lax.broadcasted_iota(I32, (t, t), 0) <= lax.broadcasted_iota(I32, (t, t), 1))
    cum = jnp.dot(onehot.reshape(N_EXPERTS, t).astype(BF16), tri.astype(F32).astype(BF16),
                  preferred_element_type=F32)
    cum3 = cum.reshape(shape3) - 1.0
    ranks = [sum_all(jnp.where(hit, cum3, 0.0)).reshape(1, t).astype(I32) for hit in hits]
    counts = cum[:, t - 1:t].astype(I32)
    return idxs, wts, ranks, counts


def _merge_kernel(x_ref, ya_ref, o1, o2, o3, l1, l2, l3, sga_ref, sgb_ref, mod_ref, n2_ref,
                  wpa, wpb, wo, wrt, rbias, wsg, wsu, wsd,
                  base_ref, hp_ref, ridx_ref, rw_ref, rank_ref, cnt_ref, scr):
    mod = mod_ref[0]
    tm = x_ref.shape[0]

    def token_major(ref, slot, d):
        if d == 1:
            return ref[...].astype(F32)
        nc = B_OUT_W // LANES
        for c in range(nc):
            for res in range(d):
                col = res * B_OUT_W + c * LANES
                scr[slot * nc + c, pl.ds(res, tm // d, stride=d), :] = ref[
                    :, col:col + LANES].astype(F32)
        return jnp.concatenate([scr[slot * nc + c] for c in range(nc)], axis=1)

    dils = [d for _, d in B_GROUPS]
    os_ = [token_major(r, i, d) for i, (r, d) in enumerate(zip((o1, o2, o3), dils))]
    ls = [token_major(r, 3 + i, d) for i, (r, d) in enumerate(zip((l1, l2, l3), dils))]
    mx = jnp.maximum(jnp.maximum(ls[0], ls[1]), ls[2])
    es = [jnp.exp(l - mx) for l in ls]
    den = es[0] + es[1] + es[2]
    ob = (es[0] / den) * os_[0] + (es[1] / den) * os_[1] + (es[2] / den) * os_[2]
    pa = jnp.dot(ya_ref[...], wpa[...], preferred_element_type=F32)
    pb = jnp.dot(ob.astype(BF16), wpb[...], preferred_element_type=F32)
    merged = sga_ref[...].astype(F32) * pa + sgb_ref[...].astype(F32) * pb
    mix = jnp.dot(merged.astype(BF16), wo[...], preferred_element_type=F32)
    x1 = x_ref[...] + mod[2:3] * mix
    h2 = _rms(x1, n2_ref[...]) * (1.0 + mod[4:5]) + mod[3:4]
    logits = lax.dot_general(wrt[...], h2, (((1,), (1,)), ((), ())), preferred_element_type=F32,
                             precision=lax.Precision.HIGHEST)
    scores = jax.nn.sigmoid(logits)
    idxs, wts, ranks, counts = _route(scores + rbias[...], scores)
    for k in range(TOP_K):
        ridx_ref[k:k + 1, :] = idxs[k]
        rw_ref[k:k + 1, :] = wts[k]
        rank_ref[k:k + 1, :] = ranks[k]
    ridx_ref[TOP_K:, :] = jnp.zeros((SUBLANES - TOP_K, tm), I32)
    rw_ref[TOP_K:, :] = jnp.zeros((SUBLANES - TOP_K, tm), F32)
    rank_ref[TOP_K:, :] = jnp.zeros((SUBLANES - TOP_K, tm), I32)
    cnt_ref[0] = counts
    hb = h2.astype(BF16)
    g = jnp.dot(hb, wsg[...], preferred_element_type=F32)
    u = jnp.dot(hb, wsu[...], preferred_element_type=F32)
    act = (g * jax.nn.sigmoid(g) * u).astype(BF16)
    shared = jnp.dot(act, wsd[...], preferred_element_type=F32)
    base_ref[...] = x1 + mod[5:6] * shared
    _store_slabs(hp_ref, _pack_rows(h2))


def _merge(x2, ya, outs, lses, sga, sgb, mod3, norm2, wpa, wpb, wo, wrt, rbias, wsg, wsu, wsd, seq):
    n = x2.shape[0]
    tm = TM_MERGE
    assert seq % tm == 0
    row = lambda i: (i, 0)
    full = lambda a: pl.BlockSpec(a.shape, lambda i: (0,) * a.ndim)
    weights = [wpa, wpb, wo, wrt, rbias, wsg, wsu, wsd]
    group_specs = [pl.BlockSpec((tm // d, d * B_OUT_W), row) for _, d in B_GROUPS]
    lanes = lambda i: (0, i)
    return pl.pallas_call(
        _merge_kernel,
        out_shape=[jax.ShapeDtypeStruct((n, D_MODEL), F32),
                   jax.ShapeDtypeStruct((n * ROW_SLABS, LANES), U32),
                   jax.ShapeDtypeStruct((SUBLANES, n), I32),
                   jax.ShapeDtypeStruct((SUBLANES, n), F32),
                   jax.ShapeDtypeStruct((SUBLANES, n), I32),
                   jax.ShapeDtypeStruct((n // tm, N_EXPERTS, 1), I32)],
        grid=(n // tm,),
        in_specs=[pl.BlockSpec((tm, D_MODEL), row), pl.BlockSpec((tm, A_Q_W), row)]
                 + group_specs * 2
                 + [pl.BlockSpec((tm, D_MODEL), row)] * 2
                 + [pl.BlockSpec((1, 6, D_MODEL), lambda i: (i * tm // seq, 0, 0)),
                    pl.BlockSpec((1, D_MODEL), lambda i: (0, 0))]
                 + [full(w) for w in weights],
        out_specs=[pl.BlockSpec((tm, D_MODEL), row),
                   pl.BlockSpec((tm * ROW_SLABS, LANES), row),
                   pl.BlockSpec((SUBLANES, tm), lanes), pl.BlockSpec((SUBLANES, tm), lanes),
                   pl.BlockSpec((SUBLANES, tm), lanes),
                   pl.BlockSpec((1, N_EXPERTS, 1), lambda i: (i, 0, 0))],
        scratch_shapes=[pltpu.VMEM((6 * B_OUT_W // LANES, tm, LANES), F32)],
        compiler_params=_cparams(1),
        name="merge_route",
    )(x2, ya, *outs, *lses, sga, sgb, mod3, norm2.reshape(1, D_MODEL), *weights)


def _plan_kernel(ridx_ref, rank_ref, tb_ref, dest_ref):
    tb = tb_ref[0]
    t = ridx_ref.shape[1]
    iota_e = lax.broadcasted_iota(I32, (N_EXPERTS, t), 0)
    for k in range(TOP_K):
        base = jnp.sum(jnp.where(iota_e == ridx_ref[k:k + 1, :], tb, 0), axis=0, keepdims=True)
        dest_ref[0, :, k * t:(k + 1) * t] = base + rank_ref[k:k + 1, :]


def _plan(ridx, rank, tile_base, tm):
    n = ridx.shape[1]
    lanes = lambda i: (0, i)
    return pl.pallas_call(
        _plan_kernel,
        out_shape=jax.ShapeDtypeStruct((n // tm, 1, TOP_K * tm), I32),
        grid=(n // tm,),
        in_specs=[pl.BlockSpec((SUBLANES, tm), lanes), pl.BlockSpec((SUBLANES, tm), lanes),
                  pl.BlockSpec((1, N_EXPERTS, 1), lambda i: (i, 0, 0))],
        out_specs=pl.BlockSpec((1, 1, TOP_K * tm), lambda i: (i, 0, 0)),
        compiler_params=_cparams(1),
        name="moe_plan",
    )(ridx, rank, tile_base)


def _sc_mesh():
    return plsc.VectorSubcoreMesh(core_axis_name="core", subcore_axis_name="subcore",
                                  num_cores=SC_CORES, num_subcores=SC_SUBCORES)


def _sc_worker():
    return lax.axis_index("subcore") * SC_CORES + lax.axis_index("core")


def _sc_dispatch(hp, dest_rows, n_rows):
    n = hp.shape[0]
    halves = TM_MERGE // SC_CHUNK
    workers = SC_CORES * SC_SUBCORES
    tiles_per_w = n // TM_MERGE // workers
    idx_per_w = tiles_per_w * TOP_K * halves
    assert tiles_per_w * workers * TM_MERGE == n

    @functools.partial(
        pl.kernel, mesh=_sc_mesh(),
        out_type=jax.ShapeDtypeStruct((n_rows, ROW_SLABS, LANES), U32),
        scratch_types=[pltpu.VMEM((idx_per_w, SC_CHUNK), I32),
                       pltpu.VMEM((SC_CHUNK, ROW_SLABS, LANES), U32),
                       pltpu.SemaphoreType.DMA],
        name="moe_dispatch_sc")
    def scatter(hp_hbm, idx_hbm, xs_hbm, idx_v, rows_v, sem):
        wid = _sc_worker()
        pltpu.sync_copy(idx_hbm.at[pl.ds(wid * idx_per_w, idx_per_w)], idx_v)

        @pl.loop(0, tiles_per_w * halves)
        def _(j):
            tile = j // halves
            h = j - tile * halves
            tok = (wid * tiles_per_w + tile) * TM_MERGE + h * SC_CHUNK
            pltpu.sync_copy(hp_hbm.at[pl.ds(tok, SC_CHUNK)], rows_v)
            for k in range(TOP_K):
                row = (tile * TOP_K + k) * halves + h
                pltpu.async_copy(rows_v, xs_hbm.at[idx_v.at[row]], sem).wait()

    return scatter(hp, dest_rows)


def _sc_collect(ys, dest_rows):
    n_chunks = dest_rows.shape[0]
    workers = SC_CORES * SC_SUBCORES
    per_w = n_chunks // workers
    assert per_w * workers == n_chunks

    @functools.partial(
        pl.kernel, mesh=_sc_mesh(),
        out_type=jax.ShapeDtypeStruct((n_chunks * SC_CHUNK, ROW_SLABS, LANES), U32),
        scratch_types=[pltpu.VMEM((per_w, SC_CHUNK), I32),
                       pltpu.VMEM((SC_CHUNK, ROW_SLABS, LANES), U32),
                       pltpu.SemaphoreType.DMA],
        name="moe_collect_sc")
    def gather(ys_hbm, idx_hbm, out_hbm, idx_v, rows_v, sem):
        wid = _sc_worker()
        base = wid * per_w
        pltpu.sync_copy(idx_hbm.at[pl.ds(base, per_w)], idx_v)

        @pl.loop(0, per_w)
        def _(j):
            pltpu.async_copy(ys_hbm.at[idx_v.at[j]], rows_v, sem).wait()
            pltpu.sync_copy(rows_v, out_hbm.at[pl.ds((base + j) * SC_CHUNK, SC_CHUNK)])

    return gather(ys, dest_rows)


def _expert_kernel(blk_e_ref, valid_ref, xs_ref, wgu_ref, wd_ref, ys_ref):
    del blk_e_ref
    valid = valid_ref[pl.program_id(0)]

    @pl.when(valid > 0)
    def _():
        rows = lax.broadcasted_iota(I32, (xs_ref.shape[0] // ROW_SLABS, 1), 0)
        lo, hi = _unpack_rows(jnp.where(rows < valid, _load_slabs(xs_ref), jnp.uint32(0)))
        wgu = wgu_ref[0]
        gu = jnp.dot(lo.astype(BF16), wgu[:HALF_D], preferred_element_type=F32) \
            + jnp.dot(hi.astype(BF16), wgu[HALF_D:], preferred_element_type=F32)
        g = gu[:, :D_EXPERT]
        u = gu[:, D_EXPERT:]
        act = (g * jax.nn.sigmoid(g) * u).astype(BF16)
        _store_slabs(ys_ref, _pack_rows(jnp.dot(act, wd_ref[0], preferred_element_type=F32)))

    @pl.when(valid <= 0)
    def _():
        ys_ref[...] = jnp.zeros(ys_ref.shape, U32)


def _experts(xs, blk_e, blk_valid, wgu, wd):
    n_rows = xs.shape[0] // ROW_SLABS
    bm = MOE_BM
    nblk = n_rows // bm
    return pl.pallas_call(
        _expert_kernel,
        out_shape=jax.ShapeDtypeStruct((n_rows * ROW_SLABS, LANES), U32),
        grid_spec=pltpu.PrefetchScalarGridSpec(
            num_scalar_prefetch=2,
            grid=(nblk,),
            in_specs=[pl.BlockSpec((bm * ROW_SLABS, LANES), lambda i, be, nu: (i, 0)),
                      pl.BlockSpec((1, D_MODEL, 2 * D_EXPERT), lambda i, be, nu: (be[i], 0, 0)),
                      pl.BlockSpec((1, D_EXPERT, D_MODEL), lambda i, be, nu: (be[i], 0, 0))],
            out_specs=pl.BlockSpec((bm * ROW_SLABS, LANES), lambda i, be, nu: (i, 0))),
        compiler_params=_cparams(1),
        name="moe_experts",
    )(blk_e, blk_valid, xs, wgu, wd)


def _combine_kernel(g_ref, base_ref, rw_ref, mod_ref, fn_ref, y_ref):
    tm = base_ref.shape[0]
    rw = rw_ref[...]
    acc_lo = jnp.zeros((tm, HALF_D), F32)
    acc_hi = jnp.zeros((tm, HALF_D), F32)
    for k in range(TOP_K):
        lo, hi = _unpack_rows(_load_slabs(g_ref.at[0, k]))
        w = rw[:, k:k + 1]
        acc_lo = acc_lo + w * lo
        acc_hi = acc_hi + w * hi
    routed = jnp.concatenate([acc_lo, acc_hi], axis=1)
    x2 = base_ref[...] + mod_ref[0][5:6] * routed
    y_ref[...] = _rms(x2, fn_ref[...])


def _combine(g, base, rw_t, mod3, final_norm, seq):
    n = base.shape[0]
    tm = TM_COMBINE
    assert seq % tm == 0
    g4 = g.reshape(n // tm, TOP_K, tm * ROW_SLABS, LANES)
    return pl.pallas_call(
        _combine_kernel,
        out_shape=jax.ShapeDtypeStruct((n, D_MODEL), F32),
        grid=(n // tm,),
        in_specs=[pl.BlockSpec((1, TOP_K, tm * ROW_SLABS, LANES), lambda i: (i, 0, 0, 0)),
                  pl.BlockSpec((tm, D_MODEL), lambda i: (i, 0)),
                  pl.BlockSpec((tm, SUBLANES), lambda i: (i, 0)),
                  pl.BlockSpec((1, 6, D_MODEL), lambda i: (i * tm // seq, 0, 0)),
                  pl.BlockSpec((1, D_MODEL), lambda i: (0, 0))],
        out_specs=pl.BlockSpec((tm, D_MODEL), lambda i: (i, 0)),
        compiler_params=_cparams(1),
        name="moe_combine",
    )(g4, base, rw_t, mod3, final_norm.reshape(1, D_MODEL))


def _block_layout(tile_counts, n):
    bm = MOE_BM
    c = tile_counts[:, :, 0]
    ntiles = c.shape[0]
    counts = jnp.sum(c, axis=0)
    padded = (counts + bm - 1) // bm * bm
    earlier_e = np.tri(N_EXPERTS, k=-1, dtype=bool)
    pstart = jnp.sum(jnp.where(earlier_e, padded[None, :], 0), axis=1)
    pend = pstart + padded
    earlier_t = np.tri(ntiles, k=-1, dtype=bool)
    tile_base = pstart[None, :] + jnp.sum(jnp.where(earlier_t[:, :, None], c[None], 0), axis=1)
    nblk = -(-n * TOP_K // bm) + N_EXPERTS
    blk = jnp.arange(nblk, dtype=I32)
    blk_e = jnp.minimum(jnp.sum((pend[None, :] <= blk[:, None] * bm).astype(I32), axis=1),
                        N_EXPERTS - 1)
    onehot_e = blk_e[:, None] == jnp.arange(N_EXPERTS, dtype=I32)[None, :]
    end_e = jnp.sum(jnp.where(onehot_e, (pstart + counts)[None, :], 0), axis=1)
    blk_valid = jnp.clip(end_e - blk * bm, 0, bm).astype(I32)
    return tile_base.astype(I32)[:, :, None], blk_e.astype(I32), blk_valid, nblk * bm


def _trunk(x, mod3, p):
    nbatch, seq, _ = x.shape
    n = nbatch * seq
    x2 = x.reshape(n, D_MODEL)
    proj = _inproj(x2, mod3, p["norm1"], p["w_in"], seq)
    qa, kva = proj[:2]
    sga, sgb = proj[8:]
    (ya,) = _band_attention(qa, kva, p["bias_a"], p["sink_a"], nbatch=nbatch, seq=seq, dil=1,
                            halo=A_HALF_WINDOW, shared_kv=True, want_lse=False)
    outs, lses = [], []
    for gi, (w, d) in enumerate(B_GROUPS):
        q, kv = proj[2 + 2 * gi:4 + 2 * gi]
        o, lse = _band_attention(q, kv, p["bias_b"][gi], None, nbatch=nbatch, seq=seq, dil=d,
                                 halo=w // (2 * d), shared_kv=False, want_lse=True)
        outs.append(o)
        lses.append(lse)
    base, hp, ridx, rw, rank, tile_counts = _merge(
        x2, ya, outs, lses, sga, sgb, mod3, p["norm2"], p["w_pa"], p["w_pb"], p["w_o"], p["w_rt"],
        p["rbias"], p["ws_gate"], p["ws_up"], p["ws_down"], seq)
    assert TM_MERGE == TM_COMBINE
    tile_base, blk_e, blk_valid, n_rows = _block_layout(tile_counts, n)
    dest_rows = _plan(ridx, rank, tile_base, TM_MERGE).reshape(-1, SC_CHUNK)
    slabs = lambda a: a.reshape(-1, ROW_SLABS, LANES)
    flat = lambda a: a.reshape(-1, LANES)
    xs = _sc_dispatch(slabs(hp), dest_rows, n_rows)
    ys = _experts(flat(xs), blk_e, blk_valid, p["w_gu"], p["w_down"])
    g = _sc_collect(slabs(ys), dest_rows)
    y = _combine(g, base, rw.T, mod3, p["final_norm"], seq)
    return y.reshape(nbatch, seq, D_MODEL)


def kernel(x_prompt, x_sample, c_prompt, c_sample, rel_bias, w_ada, b_ada, norm1, w_in, sink, w_pa, w_pb, w_o, norm2, w_router, router_bias, w_gate, w_up, w_down, ws_gate, ws_up, ws_down, final_norm):
    assert w_ada.shape[0] == 1
    nbp = x_prompt.shape[0]
    mod = _ada(jnp.concatenate([c_prompt, c_sample], axis=0), w_ada[0], b_ada[0])
    mod3 = mod.reshape(-1, 6, D_MODEL)

    def pair_bias(heads, halo, dist_scale):
        kb = ATTN_BLOCK + 2 * halo
        tab = _bias_table(heads, ATTN_BLOCK, kb, halo, halo, dist_scale)
        return tab.reshape(-1, 2, ATTN_BLOCK, kb).transpose(0, 2, 1, 3).reshape(-1, ATTN_BLOCK, 2 * kb)

    bias_a = pair_bias(rel_bias[:, :A_Q_HEADS], A_HALF_WINDOW, 1)
    bias_b = []
    for gi, (w, d) in enumerate(B_GROUPS):
        h0 = A_Q_HEADS + gi * B_HEADS_PER_GROUP
        bias_b.append(pair_bias(rel_bias[:, h0:h0 + B_HEADS_PER_GROUP], w // (2 * d), d))
    p = {
        "norm1": norm1[0], "norm2": norm2[0], "final_norm": final_norm,
        "w_in": w_in[0].astype(BF16),
        "bias_a": bias_a,
        "sink_a": jnp.repeat(sink[0].astype(F32), HEAD_DIM).reshape(A_Q_HEADS // 2, 1, LANES),
        "bias_b": bias_b,
        "w_pa": w_pa[0].astype(BF16), "w_pb": w_pb[0].astype(BF16), "w_o": w_o[0].astype(BF16),
        "w_rt": w_router[0].T, "rbias": router_bias[0].reshape(N_EXPERTS, 1),
        "ws_gate": ws_gate[0].astype(BF16), "ws_up": ws_up[0].astype(BF16),
        "ws_down": ws_down[0].astype(BF16),
        "w_gu": jnp.concatenate([w_gate[0], w_up[0]], axis=-1).astype(BF16),
        "w_down": w_down[0].astype(BF16),
    }
    y_prompt = _trunk(x_prompt, mod3[:nbp], p)
    y_sample = _trunk(x_sample, mod3[nbp:], p)
    return (y_prompt, y_sample)
```

```python
import functools
import math

import jax
import jax.numpy as jnp
import numpy as np
from jax import lax
from jax.experimental import pallas as pl
from jax.experimental.pallas import tpu as pltpu
from jax.experimental.pallas import tpu_sc as plsc

F32 = jnp.float32
BF16 = jnp.bfloat16
U32 = jnp.uint32
I32 = jnp.int32

D_MODEL = 1024
HEAD_DIM = 64
A_Q_HEADS = 8
A_KV_HEADS = 2
A_HALF_WINDOW = 128
B_GROUPS = ((128, 1), (512, 4), (2048, 16))
B_HEADS_PER_GROUP = 4
N_BUCKETS = 32
MAX_DISTANCE = 1024
N_EXPERTS = 64
TOP_K = 6
N_EXPERT_GROUPS = 8
TOPK_GROUPS = 4
D_EXPERT = 256
ROUTED_SCALE = 2.5
RMS_EPS = 1e-6
NEG_INF = -1e30
REMOVED = -3e38

A_Q_W = A_Q_HEADS * HEAD_DIM
A_KV_W = A_KV_HEADS * HEAD_DIM
B_W = len(B_GROUPS) * B_HEADS_PER_GROUP * HEAD_DIM
B_OUT_W = B_HEADS_PER_GROUP * HEAD_DIM
D_IN = A_Q_W + 2 * A_KV_W + 3 * B_W + 2 * D_MODEL
HALF_D = D_MODEL // 2
LANES = 128
SUBLANES = 8
ROW_SLABS = HALF_D // LANES
SC_CORES = 2
SC_SUBCORES = 16
SC_CHUNK = 128

TM_INPROJ = 1024
TM_MERGE = 512
ATTN_BLOCK = 128
ATTN_ROWS_A = 512
ATTN_ROWS_B = 1024
MOE_BM = 1024
TM_COMBINE = 512
VMEM_LIMIT = 60 * 1024 * 1024


def _cparams(n_axes):
    return pltpu.CompilerParams(
        dimension_semantics=("arbitrary",) * n_axes, vmem_limit_bytes=VMEM_LIMIT)


def _ada_kernel(c_ref, w_ref, b_ref, o_ref):
    c = c_ref[...]
    s = c * jax.nn.sigmoid(c)
    o_ref[...] = jnp.dot(s, w_ref[...], preferred_element_type=F32,
                         precision=lax.Precision.HIGHEST) + b_ref[...]


def _ada(c_all, w_ada, b_ada):
    nb = c_all.shape[0]
    return pl.pallas_call(
        _ada_kernel,
        out_shape=jax.ShapeDtypeStruct((nb, 6 * D_MODEL), F32),
        grid=(6,),
        in_specs=[pl.BlockSpec((nb, D_MODEL), lambda j: (0, 0)),
                  pl.BlockSpec((D_MODEL, D_MODEL), lambda j: (0, j)),
                  pl.BlockSpec((1, D_MODEL), lambda j: (0, j))],
        out_specs=pl.BlockSpec((nb, D_MODEL), lambda j: (0, j)),
        compiler_params=_cparams(1),
        name="ada",
    )(c_all, w_ada, b_ada.reshape(1, 6 * D_MODEL))


def _rms(x, g):
    return x * lax.rsqrt(jnp.mean(x * x, axis=-1, keepdims=True) + RMS_EPS) * g


def _inproj_kernel(x_ref, mod_ref, n1_ref, w_ref, qa, kva, *rest):
    q_b = rest[0:6:2]
    kv_b = rest[1:6:2]
    sga, sgb, scr = rest[6:]
    mod = mod_ref[0]
    h = _rms(x_ref[...], n1_ref[...]) * (1.0 + mod[1:2]) + mod[0:1]
    hb = h.astype(BF16)
    tm = hb.shape[0]
    scale = HEAD_DIM ** -0.5

    def proj(off, width):
        return jnp.dot(hb, w_ref[:, off:off + width], preferred_element_type=F32)

    qa[...] = (proj(0, A_Q_W) * scale).astype(BF16)
    for j, off in enumerate((A_Q_W, A_Q_W + A_KV_W)):
        r = proj(off, A_KV_W)
        kva[:, 2 * j * A_KV_W:(2 * j + 1) * A_KV_W] = r.astype(BF16)
        kva[:, (2 * j + 1) * A_KV_W:(2 * j + 2) * A_KV_W] = pltpu.roll(r, HEAD_DIM, 1).astype(BF16)
    off = A_Q_W + 2 * A_KV_W
    for t in range(3):
        for gi, (_, d) in enumerate(B_GROUPS):
            r = proj(off + t * B_W + gi * B_OUT_W, B_OUT_W)
            if t == 0:
                r = r * scale
            ref, width, base = (q_b[gi], B_OUT_W, 0) if t == 0 else (kv_b[gi], 2 * B_OUT_W,
                                                                     (t - 1) * B_OUT_W)
            if d == 1:
                ref[:, base:base + B_OUT_W] = r.astype(BF16)
            else:
                for c in range(B_OUT_W // LANES):
                    slot = (t * 2 + gi - 1) * (B_OUT_W // LANES) + c
                    scr[slot] = r[:, c * LANES:(c + 1) * LANES]
                    for res in range(d):
                        col = res * width + base + c * LANES
                        ref[:, col:col + LANES] = scr[
                            slot, pl.ds(res, tm // d, stride=d), :].astype(BF16)
    off += 3 * B_W
    sga[...] = jax.nn.sigmoid(proj(off, D_MODEL)).astype(BF16)
    sgb[...] = jax.nn.sigmoid(proj(off + D_MODEL, D_MODEL)).astype(BF16)


def _inproj(x2, mod3, norm1, w_in_bf, seq):
    n = x2.shape[0]
    tm = TM_INPROJ
    assert seq % tm == 0 and n % tm == 0
    row = lambda i: (i, 0)
    shapes = [(n, A_Q_W, tm), (n, 4 * A_KV_W, tm)]
    for _, d in B_GROUPS:
        shapes += [(n // d, d * B_OUT_W, tm // d), (n // d, d * 2 * B_OUT_W, tm // d)]
    shapes += [(n, D_MODEL, tm)] * 2
    return pl.pallas_call(
        _inproj_kernel,
        out_shape=[jax.ShapeDtypeStruct((r, c), BF16) for r, c, _ in shapes],
        grid=(n // tm,),
        in_specs=[pl.BlockSpec((tm, D_MODEL), row),
                  pl.BlockSpec((1, 6, D_MODEL), lambda i: (i * tm // seq, 0, 0)),
                  pl.BlockSpec((1, D_MODEL), lambda i: (0, 0)),
                  pl.BlockSpec((D_MODEL, D_IN), lambda i: (0, 0), pipeline_mode=pl.Buffered(1))],
        out_specs=[pl.BlockSpec((b, c), row) for _, c, b in shapes],
        scratch_shapes=[pltpu.VMEM((6 * B_OUT_W // LANES, tm, LANES), F32)],
        compiler_params=_cparams(1),
        name="inproj",
    )(x2, mod3, norm1.reshape(1, D_MODEL), w_in_bf)


def _rel_bucket_np(rel):
    half = N_BUCKETS // 2
    max_exact = half // 2
    n = np.abs(rel)
    large = max_exact + (np.log(np.maximum(n, 1) / max_exact) / math.log(MAX_DISTANCE / max_exact)
                         * (half - max_exact)).astype(np.int32)
    large = np.minimum(large, half - 1)
    return ((rel > 0).astype(np.int32) * half + np.where(n < max_exact, n, large)).astype(np.int32)


def _bias_table(rel_bias_heads, n_q, n_k, key_off, band, dist_scale):
    p = n_q + n_k
    rel = np.arange(p) - (n_q - 1) - key_off
    bucket = _rel_bucket_np(rel * dist_scale)
    t = jnp.where((np.abs(rel) <= band)[None], rel_bias_heads.astype(F32)[bucket].T, NEG_INF)
    big = jnp.tile(t, (1, n_q + 1))
    tab = big[:, n_q - 1:n_q - 1 + n_q * (p - 1)].reshape(-1, n_q, p - 1)
    return tab[:, :, :n_k]


def _pair_rhs(k_top, k_bot, v_top, v_bot):
    kb = k_top.shape[0]
    low = jnp.where(lax.broadcasted_iota(I32, (kb, LANES), 1) < HEAD_DIM, 1.0, 0.0).astype(BF16)
    high = jnp.where(lax.broadcasted_iota(I32, (kb, LANES), 1) < HEAD_DIM, 0.0, 1.0).astype(BF16)
    rhs_k = jnp.concatenate([k_top * low, k_bot * high], axis=0)
    rhs_v = jnp.concatenate([jnp.concatenate([v_top * low, low], axis=1),
                             jnp.concatenate([v_bot * high, high], axis=1)], axis=0)
    return rhs_k, rhs_v


def _pair_attention(q_pair, rhs_k, rhs_v, bias_pair, pen, sink_pair, want_lse):
    kb = rhs_k.shape[0] // 2
    s = lax.dot_general(q_pair, rhs_k, (((1,), (1,)), ((), ())), preferred_element_type=F32)
    s = s + bias_pair
    if pen is not None:
        s = s + pen
    s0, s1 = s[:, :kb], s[:, kb:]
    m0 = jnp.max(s0, axis=-1, keepdims=True)
    m1 = jnp.max(s1, axis=-1, keepdims=True)
    if sink_pair is not None:
        m0 = jnp.maximum(m0, sink_pair[:, 0:1])
        m1 = jnp.maximum(m1, sink_pair[:, HEAD_DIM:HEAD_DIM + 1])
    p = jnp.concatenate([jnp.exp(s0 - m0), jnp.exp(s1 - m1)], axis=1).astype(BF16)
    od = jnp.dot(p, rhs_v, preferred_element_type=F32)
    o, den = od[:, :LANES], od[:, LANES:]
    low = lax.broadcasted_iota(I32, o.shape, 1) < HEAD_DIM
    m_full = jnp.where(low, m0, m1)
    if sink_pair is not None:
        den = den + jnp.exp(sink_pair - m_full)
    return o / den, (m_full + jnp.log(den)) if want_lse else None


def _attn_kernel(*refs, halo, nblk, n_pairs, n_seqs, shared_kv, has_sink, want_lse, n_steps):
    it = iter(refs)
    q_ref = next(it)
    kv_prev, kv_cur, kv_next = next(it), next(it), next(it)
    bias_ref = next(it)
    sink_ref = next(it) if has_sink else None
    o_ref = next(it)
    lse_ref = next(it) if want_lse else None
    step = pl.program_id(2)
    blk = ATTN_BLOCK
    kb = blk + 2 * halo
    qw = n_pairs * LANES
    kv = jnp.concatenate([kv_prev[...], kv_cur[...], kv_next[...]], axis=0)
    col = lax.broadcasted_iota(I32, (1, kb), 1)
    for b in range(nblk):
        lo = jnp.where(step == 0, halo, 0) if b == 0 else 0
        hi = jnp.where(step == n_steps - 1, kb - halo, kb) if b == nblk - 1 else kb
        pen = None
        if b == 0 or b == nblk - 1:
            pen1 = jnp.where((col >= lo) & (col < hi), 0.0, NEG_INF)
            pen = jnp.concatenate([pen1, pen1], axis=1)
        rows = slice(b * blk, (b + 1) * blk)
        krows = slice(b * blk, b * blk + kb)
        lane_col = lambda j: kv[krows, j * LANES:(j + 1) * LANES]
        if shared_kv:
            k, k_sw, v, v_sw = (lane_col(j) for j in range(4))
            rhs = [_pair_rhs(k, k_sw, v, v_sw), _pair_rhs(k_sw, k, v_sw, v)]
        for s in range(n_seqs):
            for c in range(n_pairs):
                lanes = slice(s * qw + c * LANES, s * qw + (c + 1) * LANES)
                if shared_kv:
                    rhs_k, rhs_v = rhs[c // (n_pairs // 2)]
                else:
                    k, v = lane_col(2 * s * n_pairs + c), lane_col((2 * s + 1) * n_pairs + c)
                    rhs_k, rhs_v = _pair_rhs(k, k, v, v)
                o, lse = _pair_attention(q_ref[rows, lanes], rhs_k, rhs_v, bias_ref[c], pen,
                                         sink_ref[c] if has_sink else None, want_lse)
                o_ref[rows, lanes] = o.astype(o_ref.dtype)
                if want_lse:
                    lse_ref[rows, lanes] = lse


def _band_attention(q, kv, bias, sink, *, nbatch, seq, dil, halo, step_rows, shared_kv, want_lse):
    n = nbatch * seq
    sub_len = seq // dil
    qw = q.shape[1] // dil
    kv_width = kv.shape[1] // dil
    rows = min(step_rows, sub_len)
    n_seqs = min(dil, step_rows // rows)
    assert sub_len % rows == 0 and rows % ATTN_BLOCK == 0 and rows % halo == 0 and dil % n_seqs == 0
    nq = sub_len // rows
    per = rows // halo
    total_halos = n // dil // halo
    cur = lambda b, r, i: (b * nq + i, r)
    prev = lambda b, r, i: (jnp.maximum((b * nq + i) * per - 1, 0), r)
    nxt = lambda b, r, i: (jnp.minimum((b * nq + i + 1) * per, total_halos - 1), r)
    const = lambda a: pl.BlockSpec(a.shape, lambda b, r, i: (0,) * a.ndim)
    qw_step, kvw_step = qw * n_seqs, kv_width * n_seqs

    in_specs = [pl.BlockSpec((rows, qw_step), cur), pl.BlockSpec((halo, kvw_step), prev),
                pl.BlockSpec((rows, kvw_step), cur), pl.BlockSpec((halo, kvw_step), nxt),
                const(bias)]
    args = [q, kv, kv, kv, bias]
    if sink is not None:
        in_specs.append(const(sink))
        args.append(sink)
    out_shape = [jax.ShapeDtypeStruct(q.shape, BF16)]
    out_specs = [pl.BlockSpec((rows, qw_step), cur)]
    if want_lse:
        out_shape.append(jax.ShapeDtypeStruct(q.shape, F32))
        out_specs.append(pl.BlockSpec((rows, qw_step), cur))
    return pl.pallas_call(
        functools.partial(_attn_kernel, halo=halo, nblk=rows // ATTN_BLOCK, n_pairs=qw // LANES,
                          n_seqs=n_seqs, shared_kv=shared_kv, has_sink=sink is not None,
                          want_lse=want_lse, n_steps=nq),
        out_shape=out_shape,
        grid=(nbatch, dil // n_seqs, nq),
        in_specs=in_specs,
        out_specs=out_specs,
        compiler_params=_cparams(3),
        name=f"band_attn_d{dil}",
    )(*args)


def _pack_rows(y):
    return pltpu.pack_elementwise([y[:, :HALF_D], y[:, HALF_D:]], packed_dtype=BF16)


def _store_slabs(ref, packed):
    t = packed.shape[0]
    for c in range(ROW_SLABS):
        ref[pl.ds(c, t, stride=ROW_SLABS), :] = packed[:, c * LANES:(c + 1) * LANES]


def _load_slabs(ref):
    t = ref.shape[0] // ROW_SLABS
    return jnp.concatenate([ref[pl.ds(c, t, stride=ROW_SLABS), :] for c in range(ROW_SLABS)], axis=1)


def _unpack_rows(p):
    return tuple(pltpu.unpack_elementwise(p, index=i, packed_dtype=BF16, unpacked_dtype=F32)
                 for i in range(2))


def _route(sel, scores):
    t = sel.shape[-1]
    per = N_EXPERTS // N_EXPERT_GROUPS
    shape3 = (N_EXPERT_GROUPS, per, t)
    sel3 = sel.reshape(shape3)
    sc3 = scores.reshape(shape3)
    iota_g = lax.broadcasted_iota(I32, shape3, 0)
    iota_m = lax.broadcasted_iota(I32, shape3, 1)
    iota_e = iota_g * per + iota_m
    m1 = jnp.max(sel3, axis=1, keepdims=True)
    i1 = jnp.min(jnp.where(sel3 == m1, iota_m, per), axis=1, keepdims=True)
    m2 = jnp.max(jnp.where(iota_m == i1, REMOVED, sel3), axis=1, keepdims=True)
    gscore = m1 + m2
    iota_g1 = lax.broadcasted_iota(I32, gscore.shape, 0)
    gmask = jnp.zeros(gscore.shape, jnp.bool_)
    for _ in range(TOPK_GROUPS):
        mx = jnp.max(gscore, axis=0, keepdims=True)
        ix = jnp.min(jnp.where(gscore == mx, iota_g1, N_EXPERT_GROUPS), axis=0, keepdims=True)
        hit = iota_g1 == ix
        gmask = gmask | hit
        gscore = jnp.where(hit, REMOVED, gscore)
    cur = jnp.where(gmask, sel3, NEG_INF)
    sum_all = lambda a: jnp.sum(jnp.sum(a, axis=1, keepdims=True), axis=0, keepdims=True)
    idxs, wts, hits = [], [], []
    for _ in range(TOP_K):
        mx = jnp.max(jnp.max(cur, axis=1, keepdims=True), axis=0, keepdims=True)
        cand = jnp.where(cur == mx, iota_e, N_EXPERTS)
        ix = jnp.min(jnp.min(cand, axis=1, keepdims=True), axis=0, keepdims=True)
        hit = iota_e == ix
        wts.append(sum_all(jnp.where(hit, sc3, 0.0)).reshape(1, t))
        cur = jnp.where(hit, REMOVED, cur)
        idxs.append(ix.reshape(1, t))
        hits.append(hit)
    wsum = wts[0]
    for w in wts[1:]:
        wsum = wsum + w
    wts = [w / wsum * ROUTED_SCALE for w in wts]
    onehot = hits[0].astype(F32)
    for hit in hits[1:]:
        onehot = onehot + hit.astype(F32)
    tri = (lax.broadcasted_iota(I32, (t, t), 0) <= lax.broadcasted_iota(I32, (t, t), 1))
    cum = jnp.dot(onehot.reshape(N_EXPERTS, t).astype(BF16), tri.astype(F32).astype(BF16),
                  preferred_element_type=F32)
    cum3 = cum.reshape(shape3) - 1.0
    ranks = [sum_all(jnp.where(hit, cum3, 0.0)).reshape(1, t).astype(I32) for hit in hits]
    counts = cum[:, t - 1:t].astype(I32)
    return idxs, wts, ranks, counts


def _merge_kernel(x_ref, ya_ref, o1, o2, o3, l1, l2, l3, sga_ref, sgb_ref, mod_ref, n2_ref,
                  wpa, wpb, wo, wrt, rbias, wsg, wsu, wsd,
                  base_ref, hp_ref, ridx_ref, rw_ref, rank_ref, cnt_ref, scr):
    mod = mod_ref[0]
    tm = x_ref.shape[0]

    def token_major(ref, slot, d):
        if d == 1:
            return ref[...].astype(F32)
        nc = B_OUT_W // LANES
        for c in range(nc):
            for res in range(d):
                col = res * B_OUT_W + c * LANES
                scr[slot * nc + c, pl.ds(res, tm // d, stride=d), :] = ref[
                    :, col:col + LANES].astype(F32)
        return jnp.concatenate([scr[slot * nc + c] for c in range(nc)], axis=1)

    dils = [d for _, d in B_GROUPS]
    os_ = [token_major(r, i, d) for i, (r, d) in enumerate(zip((o1, o2, o3), dils))]
    ls = [token_major(r, 3 + i, d) for i, (r, d) in enumerate(zip((l1, l2, l3), dils))]
    mx = jnp.maximum(jnp.maximum(ls[0], ls[1]), ls[2])
    es = [jnp.exp(l - mx) for l in ls]
    den = es[0] + es[1] + es[2]
    ob = (es[0] / den) * os_[0] + (es[1] / den) * os_[1] + (es[2] / den) * os_[2]
    pa = jnp.dot(ya_ref[...], wpa[...], preferred_element_type=F32)
    pb = jnp.dot(ob.astype(BF16), wpb[...], preferred_element_type=F32)
    merged = sga_ref[...].astype(F32) * pa + sgb_ref[...].astype(F32) * pb
    mix = jnp.dot(merged.astype(BF16), wo[...], preferred_element_type=F32)
    x1 = x_ref[...] + mod[2:3] * mix
    h2 = _rms(x1, n2_ref[...]) * (1.0 + mod[4:5]) + mod[3:4]
    logits = lax.dot_general(wrt[...], h2, (((1,), (1,)), ((), ())), preferred_element_type=F32,
                             precision=lax.Precision.HIGHEST)
    scores = jax.nn.sigmoid(logits)
    idxs, wts, ranks, counts = _route(scores + rbias[...], scores)
    for k in range(TOP_K):
        ridx_ref[k:k + 1, :] = idxs[k]
        rw_ref[k:k + 1, :] = wts[k]
        rank_ref[k:k + 1, :] = ranks[k]
    ridx_ref[TOP_K:, :] = jnp.zeros((SUBLANES - TOP_K, tm), I32)
    rw_ref[TOP_K:, :] = jnp.zeros((SUBLANES - TOP_K, tm), F32)
    rank_ref[TOP_K:, :] = jnp.zeros((SUBLANES - TOP_K, tm), I32)
    cnt_ref[0] = counts
    hb = h2.astype(BF16)
    g = jnp.dot(hb, wsg[...], preferred_element_type=F32)
    u = jnp.dot(hb, wsu[...], preferred_element_type=F32)
    act = (g * jax.nn.sigmoid(g) * u).astype(BF16)
    shared = jnp.dot(act, wsd[...], preferred_element_type=F32)
    base_ref[...] = x1 + mod[5:6] * shared
    _store_slabs(hp_ref, _pack_rows(h2))


def _merge(x2, ya, outs, lses, sga, sgb, mod3, norm2, wpa, wpb, wo, wrt, rbias, wsg, wsu, wsd, seq):
    n = x2.shape[0]
    tm = TM_MERGE
    assert seq % tm == 0
    row = lambda i: (i, 0)
    full = lambda a: pl.BlockSpec(a.shape, lambda i: (0,) * a.ndim)
    weights = [wpa, wpb, wo, wrt, rbias, wsg, wsu, wsd]
    group_specs = [pl.BlockSpec((tm // d, d * B_OUT_W), row) for _, d in B_GROUPS]
    lanes = lambda i: (0, i)
    return pl.pallas_call(
        _merge_kernel,
        out_shape=[jax.ShapeDtypeStruct((n, D_MODEL), F32),
                   jax.ShapeDtypeStruct((n * ROW_SLABS, LANES), U32),
                   jax.ShapeDtypeStruct((SUBLANES, n), I32),
                   jax.ShapeDtypeStruct((SUBLANES, n), F32),
                   jax.ShapeDtypeStruct((SUBLANES, n), I32),
                   jax.ShapeDtypeStruct((n // tm, N_EXPERTS, 1), I32)],
        grid=(n // tm,),
        in_specs=[pl.BlockSpec((tm, D_MODEL), row), pl.BlockSpec((tm, A_Q_W), row)]
                 + group_specs * 2
                 + [pl.BlockSpec((tm, D_MODEL), row)] * 2
                 + [pl.BlockSpec((1, 6, D_MODEL), lambda i: (i * tm // seq, 0, 0)),
                    pl.BlockSpec((1, D_MODEL), lambda i: (0, 0))]
                 + [full(w) for w in weights],
        out_specs=[pl.BlockSpec((tm, D_MODEL), row),
                   pl.BlockSpec((tm * ROW_SLABS, LANES), row),
                   pl.BlockSpec((SUBLANES, tm), lanes), pl.BlockSpec((SUBLANES, tm), lanes),
                   pl.BlockSpec((SUBLANES, tm), lanes),
                   pl.BlockSpec((1, N_EXPERTS, 1), lambda i: (i, 0, 0))],
        scratch_shapes=[pltpu.VMEM((6 * B_OUT_W // LANES, tm, LANES), F32)],
        compiler_params=_cparams(1),
        name="merge_route",
    )(x2, ya, *outs, *lses, sga, sgb, mod3, norm2.reshape(1, D_MODEL), *weights)


def _plan_kernel(ridx_ref, rank_ref, tb_ref, dest_ref):
    tb = tb_ref[0]
    t = ridx_ref.shape[1]
    iota_e = lax.broadcasted_iota(I32, (N_EXPERTS, t), 0)
    for k in range(TOP_K):
        base = jnp.sum(jnp.where(iota_e == ridx_ref[k:k + 1, :], tb, 0), axis=0, keepdims=True)
        dest_ref[0, :, k * t:(k + 1) * t] = base + rank_ref[k:k + 1, :]


def _plan(ridx, rank, tile_base, tm):
    n = ridx.shape[1]
    lanes = lambda i: (0, i)
    return pl.pallas_call(
        _plan_kernel,
        out_shape=jax.ShapeDtypeStruct((n // tm, 1, TOP_K * tm), I32),
        grid=(n // tm,),
        in_specs=[pl.BlockSpec((SUBLANES, tm), lanes), pl.BlockSpec((SUBLANES, tm), lanes),
                  pl.BlockSpec((1, N_EXPERTS, 1), lambda i: (i, 0, 0))],
        out_specs=pl.BlockSpec((1, 1, TOP_K * tm), lambda i: (i, 0, 0)),
        compiler_params=_cparams(1),
        name="moe_plan",
    )(ridx, rank, tile_base)


def _sc_mesh():
    return plsc.VectorSubcoreMesh(core_axis_name="core", subcore_axis_name="subcore",
                                  num_cores=SC_CORES, num_subcores=SC_SUBCORES)


def _sc_worker():
    return lax.axis_index("subcore") * SC_CORES + lax.axis_index("core")


def _sc_dispatch(hp, dest_rows, n_rows):
    n = hp.shape[0]
    halves = TM_MERGE // SC_CHUNK
    workers = SC_CORES * SC_SUBCORES
    tiles_per_w = n // TM_MERGE // workers
    idx_per_w = tiles_per_w * TOP_K * halves
    assert tiles_per_w * workers * TM_MERGE == n

    @functools.partial(
        pl.kernel, mesh=_sc_mesh(),
        out_type=jax.ShapeDtypeStruct((n_rows, ROW_SLABS, LANES), U32),
        scratch_types=[pltpu.VMEM((idx_per_w, SC_CHUNK), I32),
                       pltpu.VMEM((SC_CHUNK, ROW_SLABS, LANES), U32),
                       pltpu.SemaphoreType.DMA],
        name="moe_dispatch_sc")
    def scatter(hp_hbm, idx_hbm, xs_hbm, idx_v, rows_v, sem):
        wid = _sc_worker()
        pltpu.sync_copy(idx_hbm.at[pl.ds(wid * idx_per_w, idx_per_w)], idx_v)

        @pl.loop(0, tiles_per_w * halves)
        def _(j):
            tile = j // halves
            h = j - tile * halves
            tok = (wid * tiles_per_w + tile) * TM_MERGE + h * SC_CHUNK
            pltpu.sync_copy(hp_hbm.at[pl.ds(tok, SC_CHUNK)], rows_v)
            for k in range(TOP_K):
                row = (tile * TOP_K + k) * halves + h
                pltpu.async_copy(rows_v, xs_hbm.at[idx_v.at[row]], sem).wait()

    return scatter(hp, dest_rows)


def _sc_collect(ys, dest_rows):
    n_chunks = dest_rows.shape[0]
    workers = SC_CORES * SC_SUBCORES
    per_w = n_chunks // workers
    assert per_w * workers == n_chunks

    @functools.partial(
        pl.kernel, mesh=_sc_mesh(),
        out_type=jax.ShapeDtypeStruct((n_chunks * SC_CHUNK, ROW_SLABS, LANES), U32),
        scratch_types=[pltpu.VMEM((per_w, SC_CHUNK), I32),
                       pltpu.VMEM((SC_CHUNK, ROW_SLABS, LANES), U32),
                       pltpu.SemaphoreType.DMA],
        name="moe_collect_sc")
    def gather(ys_hbm, idx_hbm, out_hbm, idx_v, rows_v, sem):
        wid = _sc_worker()
        base = wid * per_w
        pltpu.sync_copy(idx_hbm.at[pl.ds(base, per_w)], idx_v)

        @pl.loop(0, per_w)
        def _(j):
            pltpu.async_copy(ys_hbm.at[idx_v.at[j]], rows_v, sem).wait()
            pltpu.sync_copy(rows_v, out_hbm.at[pl.ds((base + j) * SC_CHUNK, SC_CHUNK)])

    return gather(ys, dest_rows)


def _expert_kernel(blk_e_ref, valid_ref, xs_ref, wgu_ref, wd_ref, ys_ref):
    del blk_e_ref
    valid = valid_ref[pl.program_id(0)]

    @pl.when(valid > 0)
    def _():
        rows = lax.broadcasted_iota(I32, (xs_ref.shape[0] // ROW_SLABS, 1), 0)
        lo, hi = _unpack_rows(jnp.where(rows < valid, _load_slabs(xs_ref), jnp.uint32(0)))
        wgu = wgu_ref[0]
        gu = jnp.dot(lo.astype(BF16), wgu[:HALF_D], preferred_element_type=F32) \
            + jnp.dot(hi.astype(BF16), wgu[HALF_D:], preferred_element_type=F32)
        g = gu[:, :D_EXPERT]
        u = gu[:, D_EXPERT:]
        act = (g * jax.nn.sigmoid(g) * u).astype(BF16)
        _store_slabs(ys_ref, _pack_rows(jnp.dot(act, wd_ref[0], preferred_element_type=F32)))

    @pl.when(valid <= 0)
    def _():
        ys_ref[...] = jnp.zeros(ys_ref.shape, U32)


def _experts(xs, blk_e, blk_valid, wgu, wd):
    n_rows = xs.shape[0] // ROW_SLABS
    bm = MOE_BM
    nblk = n_rows // bm
    return pl.pallas_call(
        _expert_kernel,
        out_shape=jax.ShapeDtypeStruct((n_rows * ROW_SLABS, LANES), U32),
        grid_spec=pltpu.PrefetchScalarGridSpec(
            num_scalar_prefetch=2,
            grid=(nblk,),
            in_specs=[pl.BlockSpec((bm * ROW_SLABS, LANES), lambda i, be, nu: (i, 0)),
                      pl.BlockSpec((1, D_MODEL, 2 * D_EXPERT), lambda i, be, nu: (be[i], 0, 0)),
                      pl.BlockSpec((1, D_EXPERT, D_MODEL), lambda i, be, nu: (be[i], 0, 0))],
            out_specs=pl.BlockSpec((bm * ROW_SLABS, LANES), lambda i, be, nu: (i, 0))),
        compiler_params=_cparams(1),
        name="moe_experts",
    )(blk_e, blk_valid, xs, wgu, wd)


def _combine_kernel(g_ref, base_ref, rw_ref, mod_ref, fn_ref, y_ref):
    tm = base_ref.shape[0]
    rw = rw_ref[...]
    acc_lo = jnp.zeros((tm, HALF_D), F32)
    acc_hi = jnp.zeros((tm, HALF_D), F32)
    for k in range(TOP_K):
        lo, hi = _unpack_rows(_load_slabs(g_ref.at[0, k]))
        w = rw[:, k:k + 1]
        acc_lo = acc_lo + w * lo
        acc_hi = acc_hi + w * hi
    routed = jnp.concatenate([acc_lo, acc_hi], axis=1)
    x2 = base_ref[...] + mod_ref[0][5:6] * routed
    y_ref[...] = _rms(x2, fn_ref[...])


def _combine(g, base, rw_t, mod3, final_norm, seq):
    n = base.shape[0]
    tm = TM_COMBINE
    assert seq % tm == 0
    g4 = g.reshape(n // tm, TOP_K, tm * ROW_SLABS, LANES)
    return pl.pallas_call(
        _combine_kernel,
        out_shape=jax.ShapeDtypeStruct((n, D_MODEL), F32),
        grid=(n // tm,),
        in_specs=[pl.BlockSpec((1, TOP_K, tm * ROW_SLABS, LANES), lambda i: (i, 0, 0, 0)),
                  pl.BlockSpec((tm, D_MODEL), lambda i: (i, 0)),
                  pl.BlockSpec((tm, SUBLANES), lambda i: (i, 0)),
                  pl.BlockSpec((1, 6, D_MODEL), lambda i: (i * tm // seq, 0, 0)),
                  pl.BlockSpec((1, D_MODEL), lambda i: (0, 0))],
        out_specs=pl.BlockSpec((tm, D_MODEL), lambda i: (i, 0)),
        compiler_params=_cparams(1),
        name="moe_combine",
    )(g4, base, rw_t, mod3, final_norm.reshape(1, D_MODEL))


def _block_layout(tile_counts, n):
    bm = MOE_BM
    c = tile_counts[:, :, 0]
    ntiles = c.shape[0]
    counts = jnp.sum(c, axis=0)
    padded = (counts + bm - 1) // bm * bm
    earlier_e = np.tri(N_EXPERTS, k=-1, dtype=bool)
    pstart = jnp.sum(jnp.where(earlier_e, padded[None, :], 0), axis=1)
    pend = pstart + padded
    earlier_t = np.tri(ntiles, k=-1, dtype=bool)
    tile_base = pstart[None, :] + jnp.sum(jnp.where(earlier_t[:, :, None], c[None], 0), axis=1)
    nblk = -(-n * TOP_K // bm) + N_EXPERTS
    blk = jnp.arange(nblk, dtype=I32)
    blk_e = jnp.minimum(jnp.sum((pend[None, :] <= blk[:, None] * bm).astype(I32), axis=1),
                        N_EXPERTS - 1)
    onehot_e = blk_e[:, None] == jnp.arange(N_EXPERTS, dtype=I32)[None, :]
    end_e = jnp.sum(jnp.where(onehot_e, (pstart + counts)[None, :], 0), axis=1)
    blk_valid = jnp.clip(end_e - blk * bm, 0, bm).astype(I32)
    return tile_base.astype(I32)[:, :, None], blk_e.astype(I32), blk_valid, nblk * bm


def _trunk(x, mod3, p):
    nbatch, seq, _ = x.shape
    n = nbatch * seq
    x2 = x.reshape(n, D_MODEL)
    proj = _inproj(x2, mod3, p["norm1"], p["w_in"], seq)
    qa, kva = proj[:2]
    sga, sgb = proj[8:]
    (ya,) = _band_attention(qa, kva, p["bias_a"], p["sink_a"], nbatch=nbatch, seq=seq, dil=1,
                            halo=A_HALF_WINDOW, step_rows=ATTN_ROWS_A, shared_kv=True, want_lse=False)
    outs, lses = [], []
    for gi, (w, d) in enumerate(B_GROUPS):
        q, kv = proj[2 + 2 * gi:4 + 2 * gi]
        o, lse = _band_attention(q, kv, p["bias_b"][gi], None, nbatch=nbatch, seq=seq, dil=d,
                                 halo=w // (2 * d), step_rows=ATTN_ROWS_B, shared_kv=False,
                                 want_lse=True)
        outs.append(o)
        lses.append(lse)
    base, hp, ridx, rw, rank, tile_counts = _merge(
        x2, ya, outs, lses, sga, sgb, mod3, p["norm2"], p["w_pa"], p["w_pb"], p["w_o"], p["w_rt"],
        p["rbias"], p["ws_gate"], p["ws_up"], p["ws_down"], seq)
    assert TM_MERGE == TM_COMBINE
    tile_base, blk_e, blk_valid, n_rows = _block_layout(tile_counts, n)
    dest_rows = _plan(ridx, rank, tile_base, TM_MERGE).reshape(-1, SC_CHUNK)
    slabs = lambda a: a.reshape(-1, ROW_SLABS, LANES)
    flat = lambda a: a.reshape(-1, LANES)
    xs = _sc_dispatch(slabs(hp), dest_rows, n_rows)
    ys = _experts(flat(xs), blk_e, blk_valid, p["w_gu"], p["w_down"])
    g = _sc_collect(slabs(ys), dest_rows)
    y = _combine(g, base, rw.T, mod3, p["final_norm"], seq)
    return y.reshape(nbatch, seq, D_MODEL)


def kernel(x_prompt, x_sample, c_prompt, c_sample, rel_bias, w_ada, b_ada, norm1, w_in, sink, w_pa, w_pb, w_o, norm2, w_router, router_bias, w_gate, w_up, w_down, ws_gate, ws_up, ws_down, final_norm):
    assert w_ada.shape[0] == 1
    nbp = x_prompt.shape[0]
    mod = _ada(jnp.concatenate([c_prompt, c_sample], axis=0), w_ada[0], b_ada[0])
    mod3 = mod.reshape(-1, 6, D_MODEL)

    def pair_bias(heads, halo, dist_scale):
        kb = ATTN_BLOCK + 2 * halo
        tab = _bias_table(heads, ATTN_BLOCK, kb, halo, halo, dist_scale)
        return tab.reshape(-1, 2, ATTN_BLOCK, kb).transpose(0, 2, 1, 3).reshape(-1, ATTN_BLOCK, 2 * kb)

    bias_a = pair_bias(rel_bias[:, :A_Q_HEADS], A_HALF_WINDOW, 1)
    bias_b = []
    for gi, (w, d) in enumerate(B_GROUPS):
        h0 = A_Q_HEADS + gi * B_HEADS_PER_GROUP
        bias_b.append(pair_bias(rel_bias[:, h0:h0 + B_HEADS_PER_GROUP], w // (2 * d), d))
    p = {
        "norm1": norm1[0], "norm2": norm2[0], "final_norm": final_norm,
        "w_in": w_in[0].astype(BF16),
        "bias_a": bias_a,
        "sink_a": jnp.repeat(sink[0].astype(F32), HEAD_DIM).reshape(A_Q_HEADS // 2, 1, LANES),
        "bias_b": bias_b,
        "w_pa": w_pa[0].astype(BF16), "w_pb": w_pb[0].astype(BF16), "w_o": w_o[0].astype(BF16),
        "w_rt": w_router[0].T, "rbias": router_bias[0].reshape(N_EXPERTS, 1),
        "ws_gate": ws_gate[0].astype(BF16), "ws_up": ws_up[0].astype(BF16),
        "ws_down": ws_down[0].astype(BF16),
        "w_gu": jnp.concatenate([w_gate[0], w_up[0]], axis=-1).astype(BF16),
        "w_down": w_down[0].astype(BF16),
    }
    y_prompt = _trunk(x_prompt, mod3[:nbp], p)
    y_sample = _trunk(x_sample, mod3[nbp:], p)
    return (y_prompt, y_sample)
```

```python
import functools
import math

import jax
import jax.numpy as jnp
import numpy as np
from jax import lax
from jax.experimental import pallas as pl
from jax.experimental.pallas import tpu as pltpu
from jax.experimental.pallas import tpu_sc as plsc

F32 = jnp.float32
BF16 = jnp.bfloat16
U32 = jnp.uint32
I32 = jnp.int32

D_MODEL = 1024
HEAD_DIM = 64
A_Q_HEADS = 8
A_KV_HEADS = 2
A_HALF_WINDOW = 128
B_GROUPS = ((128, 1), (512, 4), (2048, 16))
B_HEADS_PER_GROUP = 4
N_BUCKETS = 32
MAX_DISTANCE = 1024
N_EXPERTS = 64
TOP_K = 6
N_EXPERT_GROUPS = 8
TOPK_GROUPS = 4
D_EXPERT = 256
ROUTED_SCALE = 2.5
RMS_EPS = 1e-6
NEG_INF = -1e30
REMOVED = -3e38

A_Q_W = A_Q_HEADS * HEAD_DIM
A_KV_W = A_KV_HEADS * HEAD_DIM
B_W = len(B_GROUPS) * B_HEADS_PER_GROUP * HEAD_DIM
B_OUT_W = B_HEADS_PER_GROUP * HEAD_DIM
D_IN = A_Q_W + 2 * A_KV_W + 3 * B_W + 2 * D_MODEL
HALF_D = D_MODEL // 2
LANES = 128
SUBLANES = 8
ROW_SLABS = HALF_D // LANES
SC_CORES = 2
SC_SUBCORES = 16
SC_CHUNK = 128

TM_INPROJ = 1024
TM_MERGE = 512
ATTN_BLOCK = 128
ATTN_ROWS_A = 1024
ATTN_ROWS_B = 2048
MOE_BM = 1024
TM_COMBINE = 512
VMEM_LIMIT = 60 * 1024 * 1024


def _cparams(n_axes):
    return pltpu.CompilerParams(
        dimension_semantics=("arbitrary",) * n_axes, vmem_limit_bytes=VMEM_LIMIT)


def _ada_kernel(c_ref, w_ref, b_ref, o_ref):
    c = c_ref[...]
    s = c * jax.nn.sigmoid(c)
    o_ref[...] = jnp.dot(s, w_ref[...], preferred_element_type=F32,
                         precision=lax.Precision.HIGHEST) + b_ref[...]


def _ada(c_all, w_ada, b_ada):
    nb = c_all.shape[0]
    return pl.pallas_call(
        _ada_kernel,
        out_shape=jax.ShapeDtypeStruct((nb, 6 * D_MODEL), F32),
        grid=(6,),
        in_specs=[pl.BlockSpec((nb, D_MODEL), lambda j: (0, 0)),
                  pl.BlockSpec((D_MODEL, D_MODEL), lambda j: (0, j)),
                  pl.BlockSpec((1, D_MODEL), lambda j: (0, j))],
        out_specs=pl.BlockSpec((nb, D_MODEL), lambda j: (0, j)),
        compiler_params=_cparams(1),
        name="ada",
    )(c_all, w_ada, b_ada.reshape(1, 6 * D_MODEL))


def _rms(x, g):
    return x * lax.rsqrt(jnp.mean(x * x, axis=-1, keepdims=True) + RMS_EPS) * g


def _inproj_kernel(x_ref, mod_ref, n1_ref, w_ref, qa, kva, *rest):
    q_b = rest[0:6:2]
    kv_b = rest[1:6:2]
    sga, sgb, scr = rest[6:]
    mod = mod_ref[0]
    h = _rms(x_ref[...], n1_ref[...]) * (1.0 + mod[1:2]) + mod[0:1]
    hb = h.astype(BF16)
    tm = hb.shape[0]
    scale = HEAD_DIM ** -0.5

    def proj(off, width):
        return jnp.dot(hb, w_ref[:, off:off + width], preferred_element_type=F32)

    qa[...] = (proj(0, A_Q_W) * scale).astype(BF16)
    for j, off in enumerate((A_Q_W, A_Q_W + A_KV_W)):
        r = proj(off, A_KV_W)
        kva[:, 2 * j * A_KV_W:(2 * j + 1) * A_KV_W] = r.astype(BF16)
        kva[:, (2 * j + 1) * A_KV_W:(2 * j + 2) * A_KV_W] = pltpu.roll(r, HEAD_DIM, 1).astype(BF16)
    off = A_Q_W + 2 * A_KV_W
    for t in range(3):
        for gi, (_, d) in enumerate(B_GROUPS):
            r = proj(off + t * B_W + gi * B_OUT_W, B_OUT_W)
            if t == 0:
                r = r * scale
            ref, width, base = (q_b[gi], B_OUT_W, 0) if t == 0 else (kv_b[gi], 2 * B_OUT_W,
                                                                     (t - 1) * B_OUT_W)
            if d == 1:
                ref[:, base:base + B_OUT_W] = r.astype(BF16)
            else:
                for c in range(B_OUT_W // LANES):
                    slot = (t * 2 + gi - 1) * (B_OUT_W // LANES) + c
                    scr[slot] = r[:, c * LANES:(c + 1) * LANES]
                    for res in range(d):
                        col = res * width + base + c * LANES
                        ref[:, col:col + LANES] = scr[
                            slot, pl.ds(res, tm // d, stride=d), :].astype(BF16)
    off += 3 * B_W
    sga[...] = jax.nn.sigmoid(proj(off, D_MODEL)).astype(BF16)
    sgb[...] = jax.nn.sigmoid(proj(off + D_MODEL, D_MODEL)).astype(BF16)


def _inproj(x2, mod3, norm1, w_in_bf, seq):
    n = x2.shape[0]
    tm = TM_INPROJ
    assert seq % tm == 0 and n % tm == 0
    row = lambda i: (i, 0)
    shapes = [(n, A_Q_W, tm), (n, 4 * A_KV_W, tm)]
    for _, d in B_GROUPS:
        shapes += [(n // d, d * B_OUT_W, tm // d), (n // d, d * 2 * B_OUT_W, tm // d)]
    shapes += [(n, D_MODEL, tm)] * 2
    return pl.pallas_call(
        _inproj_kernel,
        out_shape=[jax.ShapeDtypeStruct((r, c), BF16) for r, c, _ in shapes],
        grid=(n // tm,),
        in_specs=[pl.BlockSpec((tm, D_MODEL), row),
                  pl.BlockSpec((1, 6, D_MODEL), lambda i: (i * tm // seq, 0, 0)),
                  pl.BlockSpec((1, D_MODEL), lambda i: (0, 0)),
                  pl.BlockSpec((D_MODEL, D_IN), lambda i: (0, 0), pipeline_mode=pl.Buffered(1))],
        out_specs=[pl.BlockSpec((b, c), row) for _, c, b in shapes],
        scratch_shapes=[pltpu.VMEM((6 * B_OUT_W // LANES, tm, LANES), F32)],
        compiler_params=_cparams(1),
        name="inproj",
    )(x2, mod3, norm1.reshape(1, D_MODEL), w_in_bf)


def _rel_bucket_np(rel):
    half = N_BUCKETS // 2
    max_exact = half // 2
    n = np.abs(rel)
    large = max_exact + (np.log(np.maximum(n, 1) / max_exact) / math.log(MAX_DISTANCE / max_exact)
                         * (half - max_exact)).astype(np.int32)
    large = np.minimum(large, half - 1)
    return ((rel > 0).astype(np.int32) * half + np.where(n < max_exact, n, large)).astype(np.int32)


def _bias_table(rel_bias_heads, n_q, n_k, key_off, band, dist_scale):
    p = n_q + n_k
    rel = np.arange(p) - (n_q - 1) - key_off
    bucket = _rel_bucket_np(rel * dist_scale)
    t = jnp.where((np.abs(rel) <= band)[None], rel_bias_heads.astype(F32)[bucket].T, NEG_INF)
    big = jnp.tile(t, (1, n_q + 1))
    tab = big[:, n_q - 1:n_q - 1 + n_q * (p - 1)].reshape(-1, n_q, p - 1)
    return tab[:, :, :n_k]


def _pair_rhs(k_top, k_bot, v_top, v_bot):
    kb = k_top.shape[0]
    low = jnp.where(lax.broadcasted_iota(I32, (kb, LANES), 1) < HEAD_DIM, 1.0, 0.0).astype(BF16)
    high = jnp.where(lax.broadcasted_iota(I32, (kb, LANES), 1) < HEAD_DIM, 0.0, 1.0).astype(BF16)
    rhs_k = jnp.concatenate([k_top * low, k_bot * high], axis=0)
    rhs_v = jnp.concatenate([jnp.concatenate([v_top * low, low], axis=1),
                             jnp.concatenate([v_bot * high, high], axis=1)], axis=0)
    return rhs_k, rhs_v


def _pair_attention(q_pair, rhs_k, rhs_v, bias_pair, pen, sink_pair, want_lse):
    kb = rhs_k.shape[0] // 2
    s = lax.dot_general(q_pair, rhs_k, (((1,), (1,)), ((), ())), preferred_element_type=F32)
    s = s + bias_pair
    if pen is not None:
        s = s + pen
    s0, s1 = s[:, :kb], s[:, kb:]
    m0 = jnp.max(s0, axis=-1, keepdims=True)
    m1 = jnp.max(s1, axis=-1, keepdims=True)
    if sink_pair is not None:
        m0 = jnp.maximum(m0, sink_pair[:, 0:1])
        m1 = jnp.maximum(m1, sink_pair[:, HEAD_DIM:HEAD_DIM + 1])
    p = jnp.concatenate([jnp.exp(s0 - m0), jnp.exp(s1 - m1)], axis=1).astype(BF16)
    od = jnp.dot(p, rhs_v, preferred_element_type=F32)
    o, den = od[:, :LANES], od[:, LANES:]
    low = lax.broadcasted_iota(I32, o.shape, 1) < HEAD_DIM
    m_full = jnp.where(low, m0, m1)
    if sink_pair is not None:
        den = den + jnp.exp(sink_pair - m_full)
    return o / den, (m_full + jnp.log(den)) if want_lse else None


def _attn_kernel(*refs, halo, nblk, n_pairs, n_seqs, shared_kv, has_sink, want_lse, n_steps):
    it = iter(refs)
    q_ref = next(it)
    kv_prev, kv_cur, kv_next = next(it), next(it), next(it)
    bias_ref = next(it)
    sink_ref = next(it) if has_sink else None
    o_ref = next(it)
    lse_ref = next(it) if want_lse else None
    step = pl.program_id(2)
    blk = ATTN_BLOCK
    kb = blk + 2 * halo
    qw = n_pairs * LANES
    kv = jnp.concatenate([kv_prev[...], kv_cur[...], kv_next[...]], axis=0)
    col = lax.broadcasted_iota(I32, (1, kb), 1)
    for b in range(nblk):
        lo = jnp.where(step == 0, halo, 0) if b == 0 else 0
        hi = jnp.where(step == n_steps - 1, kb - halo, kb) if b == nblk - 1 else kb
        pen = None
        if b == 0 or b == nblk - 1:
            pen1 = jnp.where((col >= lo) & (col < hi), 0.0, NEG_INF)
            pen = jnp.concatenate([pen1, pen1], axis=1)
        rows = slice(b * blk, (b + 1) * blk)
        krows = slice(b * blk, b * blk + kb)
        lane_col = lambda j: kv[krows, j * LANES:(j + 1) * LANES]
        if shared_kv:
            k, k_sw, v, v_sw = (lane_col(j) for j in range(4))
            rhs = [_pair_rhs(k, k_sw, v, v_sw), _pair_rhs(k_sw, k, v_sw, v)]
        for s in range(n_seqs):
            for c in range(n_pairs):
                lanes = slice(s * qw + c * LANES, s * qw + (c + 1) * LANES)
                if shared_kv:
                    rhs_k, rhs_v = rhs[c // (n_pairs // 2)]
                else:
                    k, v = lane_col(2 * s * n_pairs + c), lane_col((2 * s + 1) * n_pairs + c)
                    rhs_k, rhs_v = _pair_rhs(k, k, v, v)
                o, lse = _pair_attention(q_ref[rows, lanes], rhs_k, rhs_v, bias_ref[c], pen,
                                         sink_ref[c] if has_sink else None, want_lse)
                o_ref[rows, lanes] = o.astype(o_ref.dtype)
                if want_lse:
                    lse_ref[rows, lanes] = lse


def _band_attention(q, kv, bias, sink, *, nbatch, seq, dil, halo, step_rows, shared_kv, want_lse):
    n = nbatch * seq
    sub_len = seq // dil
    qw = q.shape[1] // dil
    kv_width = kv.shape[1] // dil
    rows = min(step_rows, sub_len)
    n_seqs = min(dil, step_rows // rows)
    assert sub_len % rows == 0 and rows % ATTN_BLOCK == 0 and rows % halo == 0 and dil % n_seqs == 0
    nq = sub_len // rows
    per = rows // halo
    total_halos = n // dil // halo
    cur = lambda b, r, i: (b * nq + i, r)
    prev = lambda b, r, i: (jnp.maximum((b * nq + i) * per - 1, 0), r)
    nxt = lambda b, r, i: (jnp.minimum((b * nq + i + 1) * per, total_halos - 1), r)
    const = lambda a: pl.BlockSpec(a.shape, lambda b, r, i: (0,) * a.ndim)
    qw_step, kvw_step = qw * n_seqs, kv_width * n_seqs

    in_specs = [pl.BlockSpec((rows, qw_step), cur), pl.BlockSpec((halo, kvw_step), prev),
                pl.BlockSpec((rows, kvw_step), cur), pl.BlockSpec((halo, kvw_step), nxt),
                const(bias)]
    args = [q, kv, kv, kv, bias]
    if sink is not None:
        in_specs.append(const(sink))
        args.append(sink)
    out_shape = [jax.ShapeDtypeStruct(q.shape, BF16)]
    out_specs = [pl.BlockSpec((rows, qw_step), cur)]
    if want_lse:
        out_shape.append(jax.ShapeDtypeStruct(q.shape, F32))
        out_specs.append(pl.BlockSpec((rows, qw_step), cur))
    return pl.pallas_call(
        functools.partial(_attn_kernel, halo=halo, nblk=rows // ATTN_BLOCK, n_pairs=qw // LANES,
                          n_seqs=n_seqs, shared_kv=shared_kv, has_sink=sink is not None,
                          want_lse=want_lse, n_steps=nq),
        out_shape=out_shape,
        grid=(nbatch, dil // n_seqs, nq),
        in_specs=in_specs,
        out_specs=out_specs,
        compiler_params=_cparams(3),
        name=f"band_attn_d{dil}",
    )(*args)


def _pack_rows(y):
    return pltpu.pack_elementwise([y[:, :HALF_D], y[:, HALF_D:]], packed_dtype=BF16)


def _store_slabs(ref, packed):
    t = packed.shape[0]
    for c in range(ROW_SLABS):
        ref[pl.ds(c, t, stride=ROW_SLABS), :] = packed[:, c * LANES:(c + 1) * LANES]


def _load_slabs(ref):
    t = ref.shape[0] // ROW_SLABS
    return jnp.concatenate([ref[pl.ds(c, t, stride=ROW_SLABS), :] for c in range(ROW_SLABS)], axis=1)


def _unpack_rows(p):
    return tuple(pltpu.unpack_elementwise(p, index=i, packed_dtype=BF16, unpacked_dtype=F32)
                 for i in range(2))


def _route(sel, scores):
    t = sel.shape[-1]
    per = N_EXPERTS // N_EXPERT_GROUPS
    shape3 = (N_EXPERT_GROUPS, per, t)
    sel3 = sel.reshape(shape3)
    sc3 = scores.reshape(shape3)
    iota_g = lax.broadcasted_iota(I32, shape3, 0)
    iota_m = lax.broadcasted_iota(I32, shape3, 1)
    iota_e = iota_g * per + iota_m
    m1 = jnp.max(sel3, axis=1, keepdims=True)
    i1 = jnp.min(jnp.where(sel3 == m1, iota_m, per), axis=1, keepdims=True)
    m2 = jnp.max(jnp.where(iota_m == i1, REMOVED, sel3), axis=1, keepdims=True)
    gscore = m1 + m2
    iota_g1 = lax.broadcasted_iota(I32, gscore.shape, 0)
    gmask = jnp.zeros(gscore.shape, jnp.bool_)
    for _ in range(TOPK_GROUPS):
        mx = jnp.max(gscore, axis=0, keepdims=True)
        ix = jnp.min(jnp.where(gscore == mx, iota_g1, N_EXPERT_GROUPS), axis=0, keepdims=True)
        hit = iota_g1 == ix
        gmask = gmask | hit
        gscore = jnp.where(hit, REMOVED, gscore)
    cur = jnp.where(gmask, sel3, NEG_INF)
    sum_all = lambda a: jnp.sum(jnp.sum(a, axis=1, keepdims=True), axis=0, keepdims=True)
    idxs, wts, hits = [], [], []
    for _ in range(TOP_K):
        mx = jnp.max(jnp.max(cur, axis=1, keepdims=True), axis=0, keepdims=True)
        cand = jnp.where(cur == mx, iota_e, N_EXPERTS)
        ix = jnp.min(jnp.min(cand, axis=1, keepdims=True), axis=0, keepdims=True)
        hit = iota_e == ix
        wts.append(sum_all(jnp.where(hit, sc3, 0.0)).reshape(1, t))
        cur = jnp.where(hit, REMOVED, cur)
        idxs.append(ix.reshape(1, t))
        hits.append(hit)
    wsum = wts[0]
    for w in wts[1:]:
        wsum = wsum + w
    wts = [w / wsum * ROUTED_SCALE for w in wts]
    onehot = hits[0].astype(F32)
    for hit in hits[1:]:
        onehot = onehot + hit.astype(F32)
    tri = (lax.broadcasted_iota(I32, (t, t), 0) <= lax.broadcasted_iota(I32, (t, t), 1))
    cum = jnp.dot(onehot.reshape(N_EXPERTS, t).astype(BF16), tri.astype(F32).astype(BF16),
                  preferred_element_type=F32)
    cum3 = cum.reshape(shape3) - 1.0
    ranks = [sum_all(jnp.where(hit, cum3, 0.0)).reshape(1, t).astype(I32) for hit in hits]
    counts = cum[:, t - 1:t].astype(I32)
    return idxs, wts, ranks, counts


def _merge_kernel(x_ref, ya_ref, o1, o2, o3, l1, l2, l3, sga_ref, sgb_ref, mod_ref, n2_ref,
                  wpa, wpb, wo, wrt, rbias, wsg, wsu, wsd,
                  base_ref, hp_ref, ridx_ref, rw_ref, rank_ref, cnt_ref, scr):
    mod = mod_ref[0]
    tm = x_ref.shape[0]

    def token_major(ref, slot, d):
        if d == 1:
            return ref[...].astype(F32)
        nc = B_OUT_W // LANES
        for c in range(nc):
            for res in range(d):
                col = res * B_OUT_W + c * LANES
                scr[slot * nc + c, pl.ds(res, tm // d, stride=d), :] = ref[
                    :, col:col + LANES].astype(F32)
        return jnp.concatenate([scr[slot * nc + c] for c in range(nc)], axis=1)

    dils = [d for _, d in B_GROUPS]
    os_ = [token_major(r, i, d) for i, (r, d) in enumerate(zip((o1, o2, o3), dils))]
    ls = [token_major(r, 3 + i, d) for i, (r, d) in enumerate(zip((l1, l2, l3), dils))]
    mx = jnp.maximum(jnp.maximum(ls[0], ls[1]), ls[2])
    es = [jnp.exp(l - mx) for l in ls]
    den = es[0] + es[1] + es[2]
    ob = (es[0] / den) * os_[0] + (es[1] / den) * os_[1] + (es[2] / den) * os_[2]
    pa = jnp.dot(ya_ref[...], wpa[...], preferred_element_type=F32)
    pb = jnp.dot(ob.astype(BF16), wpb[...], preferred_element_type=F32)
    merged = sga_ref[...].astype(F32) * pa + sgb_ref[...].astype(F32) * pb
    mix = jnp.dot(merged.astype(BF16), wo[...], preferred_element_type=F32)
    x1 = x_ref[...] + mod[2:3] * mix
    h2 = _rms(x1, n2_ref[...]) * (1.0 + mod[4:5]) + mod[3:4]
    logits = lax.dot_general(wrt[...], h2, (((1,), (1,)), ((), ())), preferred_element_type=F32,
                             precision=lax.Precision.HIGHEST)
    scores = jax.nn.sigmoid(logits)
    idxs, wts, ranks, counts = _route(scores + rbias[...], scores)
    for k in range(TOP_K):
        ridx_ref[k:k + 1, :] = idxs[k]
        rw_ref[k:k + 1, :] = wts[k]
        rank_ref[k:k + 1, :] = ranks[k]
    ridx_ref[TOP_K:, :] = jnp.zeros((SUBLANES - TOP_K, tm), I32)
    rw_ref[TOP_K:, :] = jnp.zeros((SUBLANES - TOP_K, tm), F32)
    rank_ref[TOP_K:, :] = jnp.zeros((SUBLANES - TOP_K, tm), I32)
    cnt_ref[0] = counts
    hb = h2.astype(BF16)
    g = jnp.dot(hb, wsg[...], preferred_element_type=F32)
    u = jnp.dot(hb, wsu[...], preferred_element_type=F32)
    act = (g * jax.nn.sigmoid(g) * u).astype(BF16)
    shared = jnp.dot(act, wsd[...], preferred_element_type=F32)
    base_ref[...] = x1 + mod[5:6] * shared
    _store_slabs(hp_ref, _pack_rows(h2))


def _merge(x2, ya, outs, lses, sga, sgb, mod3, norm2, wpa, wpb, wo, wrt, rbias, wsg, wsu, wsd, seq):
    n = x2.shape[0]
    tm = TM_MERGE
    assert seq % tm == 0
    row = lambda i: (i, 0)
    full = lambda a: pl.BlockSpec(a.shape, lambda i: (0,) * a.ndim)
    weights = [wpa, wpb, wo, wrt, rbias, wsg, wsu, wsd]
    group_specs = [pl.BlockSpec((tm // d, d * B_OUT_W), row) for _, d in B_GROUPS]
    lanes = lambda i: (0, i)
    return pl.pallas_call(
        _merge_kernel,
        out_shape=[jax.ShapeDtypeStruct((n, D_MODEL), F32),
                   jax.ShapeDtypeStruct((n * ROW_SLABS, LANES), U32),
                   jax.ShapeDtypeStruct((SUBLANES, n), I32),
                   jax.ShapeDtypeStruct((SUBLANES, n), F32),
                   jax.ShapeDtypeStruct((SUBLANES, n), I32),
                   jax.ShapeDtypeStruct((n // tm, N_EXPERTS, 1), I32)],
        grid=(n // tm,),
        in_specs=[pl.BlockSpec((tm, D_MODEL), row), pl.BlockSpec((tm, A_Q_W), row)]
                 + group_specs * 2
                 + [pl.BlockSpec((tm, D_MODEL), row)] * 2
                 + [pl.BlockSpec((1, 6, D_MODEL), lambda i: (i * tm // seq, 0, 0)),
                    pl.BlockSpec((1, D_MODEL), lambda i: (0, 0))]
                 + [full(w) for w in weights],
        out_specs=[pl.BlockSpec((tm, D_MODEL), row),
                   pl.BlockSpec((tm * ROW_SLABS, LANES), row),
                   pl.BlockSpec((SUBLANES, tm), lanes), pl.BlockSpec((SUBLANES, tm), lanes),
                   pl.BlockSpec((SUBLANES, tm), lanes),
                   pl.BlockSpec((1, N_EXPERTS, 1), lambda i: (i, 0, 0))],
        scratch_shapes=[pltpu.VMEM((6 * B_OUT_W // LANES, tm, LANES), F32)],
        compiler_params=_cparams(1),
        name="merge_route",
    )(x2, ya, *outs, *lses, sga, sgb, mod3, norm2.reshape(1, D_MODEL), *weights)


def _plan_kernel(ridx_ref, rank_ref, tb_ref, dest_ref):
    tb = tb_ref[0]
    t = ridx_ref.shape[1]
    iota_e = lax.broadcasted_iota(I32, (N_EXPERTS, t), 0)
    for k in range(TOP_K):
        base = jnp.sum(jnp.where(iota_e == ridx_ref[k:k + 1, :], tb, 0), axis=0, keepdims=True)
        dest_ref[0, :, k * t:(k + 1) * t] = base + rank_ref[k:k + 1, :]


def _plan(ridx, rank, tile_base, tm):
    n = ridx.shape[1]
    lanes = lambda i: (0, i)
    return pl.pallas_call(
        _plan_kernel,
        out_shape=jax.ShapeDtypeStruct((n // tm, 1, TOP_K * tm), I32),
        grid=(n // tm,),
        in_specs=[pl.BlockSpec((SUBLANES, tm), lanes), pl.BlockSpec((SUBLANES, tm), lanes),
                  pl.BlockSpec((1, N_EXPERTS, 1), lambda i: (i, 0, 0))],
        out_specs=pl.BlockSpec((1, 1, TOP_K * tm), lambda i: (i, 0, 0)),
        compiler_params=_cparams(1),
        name="moe_plan",
    )(ridx, rank, tile_base)


def _sc_mesh():
    return plsc.VectorSubcoreMesh(core_axis_name="core", subcore_axis_name="subcore",
                                  num_cores=SC_CORES, num_subcores=SC_SUBCORES)


def _sc_worker():
    return lax.axis_index("subcore") * SC_CORES + lax.axis_index("core")


def _sc_dispatch(hp, dest_rows, n_rows):
    n = hp.shape[0]
    halves = TM_MERGE // SC_CHUNK
    workers = SC_CORES * SC_SUBCORES
    tiles_per_w = n // TM_MERGE // workers
    idx_per_w = tiles_per_w * TOP_K * halves
    assert tiles_per_w * workers * TM_MERGE == n

    @functools.partial(
        pl.kernel, mesh=_sc_mesh(),
        out_type=jax.ShapeDtypeStruct((n_rows, ROW_SLABS, LANES), U32),
        scratch_types=[pltpu.VMEM((idx_per_w, SC_CHUNK), I32),
                       pltpu.VMEM((SC_CHUNK, ROW_SLABS, LANES), U32),
                       pltpu.SemaphoreType.DMA],
        name="moe_dispatch_sc")
    def scatter(hp_hbm, idx_hbm, xs_hbm, idx_v, rows_v, sem):
        wid = _sc_worker()
        pltpu.sync_copy(idx_hbm.at[pl.ds(wid * idx_per_w, idx_per_w)], idx_v)

        @pl.loop(0, tiles_per_w * halves)
        def _(j):
            tile = j // halves
            h = j - tile * halves
            tok = (wid * tiles_per_w + tile) * TM_MERGE + h * SC_CHUNK
            pltpu.sync_copy(hp_hbm.at[pl.ds(tok, SC_CHUNK)], rows_v)
            for k in range(TOP_K):
                row = (tile * TOP_K + k) * halves + h
                pltpu.async_copy(rows_v, xs_hbm.at[idx_v.at[row]], sem).wait()

    return scatter(hp, dest_rows)


def _sc_collect(ys, dest_rows):
    n_chunks = dest_rows.shape[0]
    workers = SC_CORES * SC_SUBCORES
    per_w = n_chunks // workers
    assert per_w * workers == n_chunks

    @functools.partial(
        pl.kernel, mesh=_sc_mesh(),
        out_type=jax.ShapeDtypeStruct((n_chunks * SC_CHUNK, ROW_SLABS, LANES), U32),
        scratch_types=[pltpu.VMEM((per_w, SC_CHUNK), I32),
                       pltpu.VMEM((SC_CHUNK, ROW_SLABS, LANES), U32),
                       pltpu.SemaphoreType.DMA],
        name="moe_collect_sc")
    def gather(ys_hbm, idx_hbm, out_hbm, idx_v, rows_v, sem):
        wid = _sc_worker()
        base = wid * per_w
        pltpu.sync_copy(idx_hbm.at[pl.ds(base, per_w)], idx_v)

        @pl.loop(0, per_w)
        def _(j):
            pltpu.async_copy(ys_hbm.at[idx_v.at[j]], rows_v, sem).wait()
            pltpu.sync_copy(rows_v, out_hbm.at[pl.ds((base + j) * SC_CHUNK, SC_CHUNK)])

    return gather(ys, dest_rows)


def _expert_kernel(blk_e_ref, valid_ref, xs_ref, wgu_ref, wd_ref, ys_ref):
    del blk_e_ref
    valid = valid_ref[pl.program_id(0)]

    @pl.when(valid > 0)
    def _():
        rows = lax.broadcasted_iota(I32, (xs_ref.shape[0] // ROW_SLABS, 1), 0)
        lo, hi = _unpack_rows(jnp.where(rows < valid, _load_slabs(xs_ref), jnp.uint32(0)))
        wgu = wgu_ref[0]
        gu = jnp.dot(lo.astype(BF16), wgu[:HALF_D], preferred_element_type=F32) \
            + jnp.dot(hi.astype(BF16), wgu[HALF_D:], preferred_element_type=F32)
        g = gu[:, :D_EXPERT]
        u = gu[:, D_EXPERT:]
        act = (g * jax.nn.sigmoid(g) * u).astype(BF16)
        _store_slabs(ys_ref, _pack_rows(jnp.dot(act, wd_ref[0], preferred_element_type=F32)))

    @pl.when(valid <= 0)
    def _():
        ys_ref[...] = jnp.zeros(ys_ref.shape, U32)


def _experts(xs, blk_e, blk_valid, wgu, wd):
    n_rows = xs.shape[0] // ROW_SLABS
    bm = MOE_BM
    nblk = n_rows // bm
    return pl.pallas_call(
        _expert_kernel,
        out_shape=jax.ShapeDtypeStruct((n_rows * ROW_SLABS, LANES), U32),
        grid_spec=pltpu.PrefetchScalarGridSpec(
            num_scalar_prefetch=2,
            grid=(nblk,),
            in_specs=[pl.BlockSpec((bm * ROW_SLABS, LANES), lambda i, be, nu: (i, 0)),
                      pl.BlockSpec((1, D_MODEL, 2 * D_EXPERT), lambda i, be, nu: (be[i], 0, 0)),
                      pl.BlockSpec((1, D_EXPERT, D_MODEL), lambda i, be, nu: (be[i], 0, 0))],
            out_specs=pl.BlockSpec((bm * ROW_SLABS, LANES), lambda i, be, nu: (i, 0))),
        compiler_params=_cparams(1),
        name="moe_experts",
    )(blk_e, blk_valid, xs, wgu, wd)


def _combine_kernel(g_ref, base_ref, rw_ref, mod_ref, fn_ref, y_ref):
    tm = base_ref.shape[0]
    rw = rw_ref[...]
    acc_lo = jnp.zeros((tm, HALF_D), F32)
    acc_hi = jnp.zeros((tm, HALF_D), F32)
    for k in range(TOP_K):
        lo, hi = _unpack_rows(_load_slabs(g_ref.at[0, k]))
        w = rw[:, k:k + 1]
        acc_lo = acc_lo + w * lo
        acc_hi = acc_hi + w * hi
    routed = jnp.concatenate([acc_lo, acc_hi], axis=1)
    x2 = base_ref[...] + mod_ref[0][5:6] * routed
    y_ref[...] = _rms(x2, fn_ref[...])


def _combine(g, base, rw_t, mod3, final_norm, seq):
    n = base.shape[0]
    tm = TM_COMBINE
    assert seq % tm == 0
    g4 = g.reshape(n // tm, TOP_K, tm * ROW_SLABS, LANES)
    return pl.pallas_call(
        _combine_kernel,
        out_shape=jax.ShapeDtypeStruct((n, D_MODEL), F32),
        grid=(n // tm,),
        in_specs=[pl.BlockSpec((1, TOP_K, tm * ROW_SLABS, LANES), lambda i: (i, 0, 0, 0)),
                  pl.BlockSpec((tm, D_MODEL), lambda i: (i, 0)),
                  pl.BlockSpec((tm, SUBLANES), lambda i: (i, 0)),
                  pl.BlockSpec((1, 6, D_MODEL), lambda i: (i * tm // seq, 0, 0)),
                  pl.BlockSpec((1, D_MODEL), lambda i: (0, 0))],
        out_specs=pl.BlockSpec((tm, D_MODEL), lambda i: (i, 0)),
        compiler_params=_cparams(1),
        name="moe_combine",
    )(g4, base, rw_t, mod3, final_norm.reshape(1, D_MODEL))


def _block_layout(tile_counts, n):
    bm = MOE_BM
    c = tile_counts[:, :, 0]
    ntiles = c.shape[0]
    counts = jnp.sum(c, axis=0)
    padded = (counts + bm - 1) // bm * bm
    earlier_e = np.tri(N_EXPERTS, k=-1, dtype=bool)
    pstart = jnp.sum(jnp.where(earlier_e, padded[None, :], 0), axis=1)
    pend = pstart + padded
    earlier_t = np.tri(ntiles, k=-1, dtype=bool)
    tile_base = pstart[None, :] + jnp.sum(jnp.where(earlier_t[:, :, None], c[None], 0), axis=1)
    nblk = -(-n * TOP_K // bm) + N_EXPERTS
    blk = jnp.arange(nblk, dtype=I32)
    blk_e = jnp.minimum(jnp.sum((pend[None, :] <= blk[:, None] * bm).astype(I32), axis=1),
                        N_EXPERTS - 1)
    onehot_e = blk_e[:, None] == jnp.arange(N_EXPERTS, dtype=I32)[None, :]
    end_e = jnp.sum(jnp.where(onehot_e, (pstart + counts)[None, :], 0), axis=1)
    blk_valid = jnp.clip(end_e - blk * bm, 0, bm).astype(I32)
    return tile_base.astype(I32)[:, :, None], blk_e.astype(I32), blk_valid, nblk * bm


def _trunk(x, mod3, p):
    nbatch, seq, _ = x.shape
    n = nbatch * seq
    x2 = x.reshape(n, D_MODEL)
    proj = _inproj(x2, mod3, p["norm1"], p["w_in"], seq)
    qa, kva = proj[:2]
    sga, sgb = proj[8:]
    (ya,) = _band_attention(qa, kva, p["bias_a"], p["sink_a"], nbatch=nbatch, seq=seq, dil=1,
                            halo=A_HALF_WINDOW, step_rows=ATTN_ROWS_A, shared_kv=True, want_lse=False)
    outs, lses = [], []
    for gi, (w, d) in enumerate(B_GROUPS):
        q, kv = proj[2 + 2 * gi:4 + 2 * gi]
        o, lse = _band_attention(q, kv, p["bias_b"][gi], None, nbatch=nbatch, seq=seq, dil=d,
                                 halo=w // (2 * d), step_rows=ATTN_ROWS_B, shared_kv=False,
                                 want_lse=True)
        outs.append(o)
        lses.append(lse)
    base, hp, ridx, rw, rank, tile_counts = _merge(
        x2, ya, outs, lses, sga, sgb, mod3, p["norm2"], p["w_pa"], p["w_pb"], p["w_o"], p["w_rt"],
        p["rbias"], p["ws_gate"], p["ws_up"], p["ws_down"], seq)
    assert TM_MERGE == TM_COMBINE
    tile_base, blk_e, blk_valid, n_rows = _block_layout(tile_counts, n)
    dest_rows = _plan(ridx, rank, tile_base, TM_MERGE).reshape(-1, SC_CHUNK)
    slabs = lambda a: a.reshape(-1, ROW_SLABS, LANES)
    flat = lambda a: a.reshape(-1, LANES)
    xs = _sc_dispatch(slabs(hp), dest_rows, n_rows)
    ys = _experts(flat(xs), blk_e, blk_valid, p["w_gu"], p["w_down"])
    g = _sc_collect(slabs(ys), dest_rows)
    y = _combine(g, base, rw.T, mod3, p["final_norm"], seq)
    return y.reshape(nbatch, seq, D_MODEL)


def kernel(x_prompt, x_sample, c_prompt, c_sample, rel_bias, w_ada, b_ada, norm1, w_in, sink, w_pa, w_pb, w_o, norm2, w_router, router_bias, w_gate, w_up, w_down, ws_gate, ws_up, ws_down, final_norm):
    assert w_ada.shape[0] == 1
    nbp = x_prompt.shape[0]
    mod = _ada(jnp.concatenate([c_prompt, c_sample], axis=0), w_ada[0], b_ada[0])
    mod3 = mod.reshape(-1, 6, D_MODEL)

    def pair_bias(heads, halo, dist_scale):
        kb = ATTN_BLOCK + 2 * halo
        tab = _bias_table(heads, ATTN_BLOCK, kb, halo, halo, dist_scale)
        return tab.reshape(-1, 2, ATTN_BLOCK, kb).transpose(0, 2, 1, 3).reshape(-1, ATTN_BLOCK, 2 * kb)

    bias_a = pair_bias(rel_bias[:, :A_Q_HEADS], A_HALF_WINDOW, 1)
    bias_b = []
    for gi, (w, d) in enumerate(B_GROUPS):
        h0 = A_Q_HEADS + gi * B_HEADS_PER_GROUP
        bias_b.append(pair_bias(rel_bias[:, h0:h0 + B_HEADS_PER_GROUP], w // (2 * d), d))
    p = {
        "norm1": norm1[0], "norm2": norm2[0], "final_norm": final_norm,
        "w_in": w_in[0].astype(BF16),
        "bias_a": bias_a,
        "sink_a": jnp.repeat(sink[0].astype(F32), HEAD_DIM).reshape(A_Q_HEADS // 2, 1, LANES),
        "bias_b": bias_b,
        "w_pa": w_pa[0].astype(BF16), "w_pb": w_pb[0].astype(BF16), "w_o": w_o[0].astype(BF16),
        "w_rt": w_router[0].T, "rbias": router_bias[0].reshape(N_EXPERTS, 1),
        "ws_gate": ws_gate[0].astype(BF16), "ws_up": ws_up[0].astype(BF16),
        "ws_down": ws_down[0].astype(BF16),
        "w_gu": jnp.concatenate([w_gate[0], w_up[0]], axis=-1).astype(BF16),
        "w_down": w_down[0].astype(BF16),
    }
    y_prompt = _trunk(x_prompt, mod3[:nbp], p)
    y_sample = _trunk(x_sample, mod3[nbp:], p)
    return (y_prompt, y_sample)
```

```python
import functools
import math

import jax
import jax.numpy as jnp
import numpy as np
from jax import lax
from jax.experimental import pallas as pl
from jax.experimental.pallas import tpu as pltpu
from jax.experimental.pallas import tpu_sc as plsc

F32 = jnp.float32
BF16 = jnp.bfloat16
U32 = jnp.uint32
I32 = jnp.int32

D_MODEL = 1024
HEAD_DIM = 64
A_Q_HEADS = 8
A_KV_HEADS = 2
A_HALF_WINDOW = 128
B_GROUPS = ((128, 1), (512, 4), (2048, 16))
B_HEADS_PER_GROUP = 4
N_BUCKETS = 32
MAX_DISTANCE = 1024
N_EXPERTS = 64
TOP_K = 6
N_EXPERT_GROUPS = 8
TOPK_GROUPS = 4
D_EXPERT = 256
ROUTED_SCALE = 2.5
RMS_EPS = 1e-6
NEG_INF = -1e30
REMOVED = -3e38

A_Q_W = A_Q_HEADS * HEAD_DIM
A_KV_W = A_KV_HEADS * HEAD_DIM
B_W = len(B_GROUPS) * B_HEADS_PER_GROUP * HEAD_DIM
B_OUT_W = B_HEADS_PER_GROUP * HEAD_DIM
D_IN = A_Q_W + 2 * A_KV_W + 3 * B_W + 2 * D_MODEL
HALF_D = D_MODEL // 2
LANES = 128
SUBLANES = 8
ROW_SLABS = HALF_D // LANES
SC_CORES = 2
SC_SUBCORES = 16
SC_CHUNK = 128

TM_INPROJ = 1024
TM_MERGE = 512
ATTN_BLOCK = 128
ATTN_ROWS_A = 512
ATTN_ROWS_B = 2048
MOE_BM = 1024
TM_COMBINE = 512
VMEM_LIMIT = 60 * 1024 * 1024


def _cparams(n_axes):
    return pltpu.CompilerParams(
        dimension_semantics=("arbitrary",) * n_axes, vmem_limit_bytes=VMEM_LIMIT)


def _ada_kernel(c_ref, w_ref, b_ref, o_ref):
    c = c_ref[...]
    s = c * jax.nn.sigmoid(c)
    o_ref[...] = jnp.dot(s, w_ref[...], preferred_element_type=F32,
                         precision=lax.Precision.HIGHEST) + b_ref[...]


def _ada(c_all, w_ada, b_ada):
    nb = c_all.shape[0]
    return pl.pallas_call(
        _ada_kernel,
        out_shape=jax.ShapeDtypeStruct((nb, 6 * D_MODEL), F32),
        grid=(6,),
        in_specs=[pl.BlockSpec((nb, D_MODEL), lambda j: (0, 0)),
                  pl.BlockSpec((D_MODEL, D_MODEL), lambda j: (0, j)),
                  pl.BlockSpec((1, D_MODEL), lambda j: (0, j))],
        out_specs=pl.BlockSpec((nb, D_MODEL), lambda j: (0, j)),
        compiler_params=_cparams(1),
        name="ada",
    )(c_all, w_ada, b_ada.reshape(1, 6 * D_MODEL))


def _rms(x, g):
    return x * lax.rsqrt(jnp.mean(x * x, axis=-1, keepdims=True) + RMS_EPS) * g


def _inproj_kernel(x_ref, mod_ref, n1_ref, w_ref, qa, kva, *rest):
    q_b = rest[0:6:2]
    kv_b = rest[1:6:2]
    sga, sgb, scr = rest[6:]
    mod = mod_ref[0]
    h = _rms(x_ref[...], n1_ref[...]) * (1.0 + mod[1:2]) + mod[0:1]
    hb = h.astype(BF16)
    tm = hb.shape[0]
    scale = HEAD_DIM ** -0.5

    def proj(off, width):
        return jnp.dot(hb, w_ref[:, off:off + width], preferred_element_type=F32)

    qa[...] = (proj(0, A_Q_W) * scale).astype(BF16)
    for j, off in enumerate((A_Q_W, A_Q_W + A_KV_W)):
        r = proj(off, A_KV_W)
        kva[:, 2 * j * A_KV_W:(2 * j + 1) * A_KV_W] = r.astype(BF16)
        kva[:, (2 * j + 1) * A_KV_W:(2 * j + 2) * A_KV_W] = pltpu.roll(r, HEAD_DIM, 1).astype(BF16)
    off = A_Q_W + 2 * A_KV_W
    for t in range(3):
        for gi, (_, d) in enumerate(B_GROUPS):
            r = proj(off + t * B_W + gi * B_OUT_W, B_OUT_W)
            if t == 0:
                r = r * scale
            ref, width, base = (q_b[gi], B_OUT_W, 0) if t == 0 else (kv_b[gi], 2 * B_OUT_W,
                                                                     (t - 1) * B_OUT_W)
            if d == 1:
                ref[:, base:base + B_OUT_W] = r.astype(BF16)
            else:
                for c in range(B_OUT_W // LANES):
                    slot = (t * 2 + gi - 1) * (B_OUT_W // LANES) + c
                    scr[slot] = r[:, c * LANES:(c + 1) * LANES]
                    for res in range(d):
                        col = res * width + base + c * LANES
                        ref[:, col:col + LANES] = scr[
                            slot, pl.ds(res, tm // d, stride=d), :].astype(BF16)
    off += 3 * B_W
    sga[...] = jax.nn.sigmoid(proj(off, D_MODEL)).astype(BF16)
    sgb[...] = jax.nn.sigmoid(proj(off + D_MODEL, D_MODEL)).astype(BF16)


def _inproj(x2, mod3, norm1, w_in_bf, seq):
    n = x2.shape[0]
    tm = TM_INPROJ
    assert seq % tm == 0 and n % tm == 0
    row = lambda i: (i, 0)
    shapes = [(n, A_Q_W, tm), (n, 4 * A_KV_W, tm)]
    for _, d in B_GROUPS:
        shapes += [(n // d, d * B_OUT_W, tm // d), (n // d, d * 2 * B_OUT_W, tm // d)]
    shapes += [(n, D_MODEL, tm)] * 2
    return pl.pallas_call(
        _inproj_kernel,
        out_shape=[jax.ShapeDtypeStruct((r, c), BF16) for r, c, _ in shapes],
        grid=(n // tm,),
        in_specs=[pl.BlockSpec((tm, D_MODEL), row),
                  pl.BlockSpec((1, 6, D_MODEL), lambda i: (i * tm // seq, 0, 0)),
                  pl.BlockSpec((1, D_MODEL), lambda i: (0, 0)),
                  pl.BlockSpec((D_MODEL, D_IN), lambda i: (0, 0), pipeline_mode=pl.Buffered(1))],
        out_specs=[pl.BlockSpec((b, c), row) for _, c, b in shapes],
        scratch_shapes=[pltpu.VMEM((6 * B_OUT_W // LANES, tm, LANES), F32)],
        compiler_params=_cparams(1),
        name="inproj",
    )(x2, mod3, norm1.reshape(1, D_MODEL), w_in_bf)


def _rel_bucket_np(rel):
    half = N_BUCKETS // 2
    max_exact = half // 2
    n = np.abs(rel)
    large = max_exact + (np.log(np.maximum(n, 1) / max_exact) / math.log(MAX_DISTANCE / max_exact)
                         * (half - max_exact)).astype(np.int32)
    large = np.minimum(large, half - 1)
    return ((rel > 0).astype(np.int32) * half + np.where(n < max_exact, n, large)).astype(np.int32)


def _bias_table(rel_bias_heads, n_q, n_k, key_off, band, dist_scale):
    p = n_q + n_k
    rel = np.arange(p) - (n_q - 1) - key_off
    bucket = _rel_bucket_np(rel * dist_scale)
    t = jnp.where((np.abs(rel) <= band)[None], rel_bias_heads.astype(F32)[bucket].T, NEG_INF)
    big = jnp.tile(t, (1, n_q + 1))
    tab = big[:, n_q - 1:n_q - 1 + n_q * (p - 1)].reshape(-1, n_q, p - 1)
    return tab[:, :, :n_k]


def _pair_rhs(k_top, k_bot, v_top, v_bot):
    kb = k_top.shape[0]
    low = jnp.where(lax.broadcasted_iota(I32, (kb, LANES), 1) < HEAD_DIM, 1.0, 0.0).astype(BF16)
    high = jnp.where(lax.broadcasted_iota(I32, (kb, LANES), 1) < HEAD_DIM, 0.0, 1.0).astype(BF16)
    rhs_k = jnp.concatenate([k_top * low, k_bot * high], axis=0)
    rhs_v = jnp.concatenate([jnp.concatenate([v_top * low, low], axis=1),
                             jnp.concatenate([v_bot * high, high], axis=1)], axis=0)
    return rhs_k, rhs_v


def _pair_attention(q_pair, rhs_k, rhs_v, bias_pair, pen, sink_pair, want_lse):
    kb = rhs_k.shape[0] // 2
    s = lax.dot_general(q_pair, rhs_k, (((1,), (1,)), ((), ())), preferred_element_type=F32)
    s = s + bias_pair
    if pen is not None:
        s = s + pen
    s0, s1 = s[:, :kb], s[:, kb:]
    m0 = jnp.max(s0, axis=-1, keepdims=True)
    m1 = jnp.max(s1, axis=-1, keepdims=True)
    if sink_pair is not None:
        m0 = jnp.maximum(m0, sink_pair[:, 0:1])
        m1 = jnp.maximum(m1, sink_pair[:, HEAD_DIM:HEAD_DIM + 1])
    p = jnp.concatenate([jnp.exp(s0 - m0), jnp.exp(s1 - m1)], axis=1).astype(BF16)
    od = jnp.dot(p, rhs_v, preferred_element_type=F32)
    o, den = od[:, :LANES], od[:, LANES:]
    low = lax.broadcasted_iota(I32, o.shape, 1) < HEAD_DIM
    m_full = jnp.where(low, m0, m1)
    if sink_pair is not None:
        den = den + jnp.exp(sink_pair - m_full)
    return o / den, (m_full + jnp.log(den)) if want_lse else None


def _attn_kernel(*refs, halo, nblk, n_pairs, n_seqs, shared_kv, has_sink, want_lse, n_steps):
    it = iter(refs)
    q_ref = next(it)
    kv_prev, kv_cur, kv_next = next(it), next(it), next(it)
    bias_ref = next(it)
    sink_ref = next(it) if has_sink else None
    o_ref = next(it)
    lse_ref = next(it) if want_lse else None
    step = pl.program_id(2)
    blk = ATTN_BLOCK
    kb = blk + 2 * halo
    qw = n_pairs * LANES
    kv = jnp.concatenate([kv_prev[...], kv_cur[...], kv_next[...]], axis=0)
    col = lax.broadcasted_iota(I32, (1, kb), 1)
    for b in range(nblk):
        lo = jnp.where(step == 0, halo, 0) if b == 0 else 0
        hi = jnp.where(step == n_steps - 1, kb - halo, kb) if b == nblk - 1 else kb
        pen = None
        if b == 0 or b == nblk - 1:
            pen1 = jnp.where((col >= lo) & (col < hi), 0.0, NEG_INF)
            pen = jnp.concatenate([pen1, pen1], axis=1)
        rows = slice(b * blk, (b + 1) * blk)
        krows = slice(b * blk, b * blk + kb)
        lane_col = lambda j: kv[krows, j * LANES:(j + 1) * LANES]
        if shared_kv:
            k, k_sw, v, v_sw = (lane_col(j) for j in range(4))
            rhs = [_pair_rhs(k, k_sw, v, v_sw), _pair_rhs(k_sw, k, v_sw, v)]
        for s in range(n_seqs):
            for c in range(n_pairs):
                lanes = slice(s * qw + c * LANES, s * qw + (c + 1) * LANES)
                if shared_kv:
                    rhs_k, rhs_v = rhs[c // (n_pairs // 2)]
                else:
                    k, v = lane_col(2 * s * n_pairs + c), lane_col((2 * s + 1) * n_pairs + c)
                    rhs_k, rhs_v = _pair_rhs(k, k, v, v)
                o, lse = _pair_attention(q_ref[rows, lanes], rhs_k, rhs_v, bias_ref[c], pen,
                                         sink_ref[c] if has_sink else None, want_lse)
                o_ref[rows, lanes] = o.astype(o_ref.dtype)
                if want_lse:
                    lse_ref[rows, lanes] = lse


def _band_attention(q, kv, bias, sink, *, nbatch, seq, dil, halo, step_rows, shared_kv, want_lse):
    n = nbatch * seq
    sub_len = seq // dil
    qw = q.shape[1] // dil
    kv_width = kv.shape[1] // dil
    rows = min(step_rows, sub_len)
    n_seqs = min(dil, step_rows // rows)
    assert sub_len % rows == 0 and rows % ATTN_BLOCK == 0 and rows % halo == 0 and dil % n_seqs == 0
    nq = sub_len // rows
    per = rows // halo
    total_halos = n // dil // halo
    cur = lambda b, r, i: (b * nq + i, r)
    prev = lambda b, r, i: (jnp.maximum((b * nq + i) * per - 1, 0), r)
    nxt = lambda b, r, i: (jnp.minimum((b * nq + i + 1) * per, total_halos - 1), r)
    const = lambda a: pl.BlockSpec(a.shape, lambda b, r, i: (0,) * a.ndim)
    qw_step, kvw_step = qw * n_seqs, kv_width * n_seqs

    in_specs = [pl.BlockSpec((rows, qw_step), cur), pl.BlockSpec((halo, kvw_step), prev),
                pl.BlockSpec((rows, kvw_step), cur), pl.BlockSpec((halo, kvw_step), nxt),
                const(bias)]
    args = [q, kv, kv, kv, bias]
    if sink is not None:
        in_specs.append(const(sink))
        args.append(sink)
    out_shape = [jax.ShapeDtypeStruct(q.shape, BF16)]
    out_specs = [pl.BlockSpec((rows, qw_step), cur)]
    if want_lse:
        out_shape.append(jax.ShapeDtypeStruct(q.shape, F32))
        out_specs.append(pl.BlockSpec((rows, qw_step), cur))
    return pl.pallas_call(
        functools.partial(_attn_kernel, halo=halo, nblk=rows // ATTN_BLOCK, n_pairs=qw // LANES,
                          n_seqs=n_seqs, shared_kv=shared_kv, has_sink=sink is not None,
                          want_lse=want_lse, n_steps=nq),
        out_shape=out_shape,
        grid=(nbatch, dil // n_seqs, nq),
        in_specs=in_specs,
        out_specs=out_specs,
        compiler_params=_cparams(3),
        name=f"band_attn_d{dil}",
    )(*args)


def _pack_rows(y):
    return pltpu.pack_elementwise([y[:, :HALF_D], y[:, HALF_D:]], packed_dtype=BF16)


def _store_slabs(ref, packed):
    t = packed.shape[0]
    for c in range(ROW_SLABS):
        ref[pl.ds(c, t, stride=ROW_SLABS), :] = packed[:, c * LANES:(c + 1) * LANES]


def _load_slabs(ref):
    t = ref.shape[0] // ROW_SLABS
    return jnp.concatenate([ref[pl.ds(c, t, stride=ROW_SLABS), :] for c in range(ROW_SLABS)], axis=1)


def _unpack_rows(p):
    return tuple(pltpu.unpack_elementwise(p, index=i, packed_dtype=BF16, unpacked_dtype=F32)
                 for i in range(2))


def _route(sel, scores):
    t = sel.shape[-1]
    per = N_EXPERTS // N_EXPERT_GROUPS
    shape3 = (N_EXPERT_GROUPS, per, t)
    sel3 = sel.reshape(shape3)
    sc3 = scores.reshape(shape3)
    iota_g = lax.broadcasted_iota(I32, shape3, 0)
    iota_m = lax.broadcasted_iota(I32, shape3, 1)
    iota_e = iota_g * per + iota_m
    m1 = jnp.max(sel3, axis=1, keepdims=True)
    i1 = jnp.min(jnp.where(sel3 == m1, iota_m, per), axis=1, keepdims=True)
    m2 = jnp.max(jnp.where(iota_m == i1, REMOVED, sel3), axis=1, keepdims=True)
    gscore = m1 + m2
    iota_g1 = lax.broadcasted_iota(I32, gscore.shape, 0)
    gmask = jnp.zeros(gscore.shape, jnp.bool_)
    for _ in range(TOPK_GROUPS):
        mx = jnp.max(gscore, axis=0, keepdims=True)
        ix = jnp.min(jnp.where(gscore == mx, iota_g1, N_EXPERT_GROUPS), axis=0, keepdims=True)
        hit = iota_g1 == ix
        gmask = gmask | hit
        gscore = jnp.where(hit, REMOVED, gscore)
    cur = jnp.where(gmask, sel3, NEG_INF)
    sum_all = lambda a: jnp.sum(jnp.sum(a, axis=1, keepdims=True), axis=0, keepdims=True)
    idxs, wts, hits = [], [], []
    for _ in range(TOP_K):
        mx = jnp.max(jnp.max(cur, axis=1, keepdims=True), axis=0, keepdims=True)
        cand = jnp.where(cur == mx, iota_e, N_EXPERTS)
        ix = jnp.min(jnp.min(cand, axis=1, keepdims=True), axis=0, keepdims=True)
        hit = iota_e == ix
        wts.append(sum_all(jnp.where(hit, sc3, 0.0)).reshape(1, t))
        cur = jnp.where(hit, REMOVED, cur)
        idxs.append(ix.reshape(1, t))
        hits.append(hit)
    wsum = wts[0]
    for w in wts[1:]:
        wsum = wsum + w
    wts = [w / wsum * ROUTED_SCALE for w in wts]
    onehot = hits[0].astype(F32)
    for hit in hits[1:]:
        onehot = onehot + hit.astype(F32)
    tri = (lax.broadcasted_iota(I32, (t, t), 0) <= lax.broadcasted_iota(I32, (t, t), 1))
    cum = jnp.dot(onehot.reshape(N_EXPERTS, t).astype(BF16), tri.astype(F32).astype(BF16),
                  preferred_element_type=F32)
    cum3 = cum.reshape(shape3) - 1.0
    ranks = [sum_all(jnp.where(hit, cum3, 0.0)).reshape(1, t).astype(I32) for hit in hits]
    counts = cum[:, t - 1:t].astype(I32)
    return idxs, wts, ranks, counts


def _merge_kernel(x_ref, ya_ref, o1, o2, o3, l1, l2, l3, sga_ref, sgb_ref, mod_ref, n2_ref,
                  wpa, wpb, wo, wrt, rbias, wsg, wsu, wsd,
                  base_ref, hp_ref, ridx_ref, rw_ref, rank_ref, cnt_ref, scr):
    mod = mod_ref[0]
    tm = x_ref.shape[0]

    def token_major(ref, slot, d):
        if d == 1:
            return ref[...].astype(F32)
        nc = B_OUT_W // LANES
        for c in range(nc):
            for res in range(d):
                col = res * B_OUT_W + c * LANES
                scr[slot * nc + c, pl.ds(res, tm // d, stride=d), :] = ref[
                    :, col:col + LANES].astype(F32)
        return jnp.concatenate([scr[slot * nc + c] for c in range(nc)], axis=1)

    dils = [d for _, d in B_GROUPS]
    os_ = [token_major(r, i, d) for i, (r, d) in enumerate(zip((o1, o2, o3), dils))]
    ls = [token_major(r, 3 + i, d) for i, (r, d) in enumerate(zip((l1, l2, l3), dils))]
    mx = jnp.maximum(jnp.maximum(ls[0], ls[1]), ls[2])
    es = [jnp.exp(l - mx) for l in ls]
    den = es[0] + es[1] + es[2]
    ob = (es[0] / den) * os_[0] + (es[1] / den) * os_[1] + (es[2] / den) * os_[2]
    pa = jnp.dot(ya_ref[...], wpa[...], preferred_element_type=F32)
    pb = jnp.dot(ob.astype(BF16), wpb[...], preferred_element_type=F32)
    merged = sga_ref[...].astype(F32) * pa + sgb_ref[...].astype(F32) * pb
    mix = jnp.dot(merged.astype(BF16), wo[...], preferred_element_type=F32)
    x1 = x_ref[...] + mod[2:3] * mix
    h2 = _rms(x1, n2_ref[...]) * (1.0 + mod[4:5]) + mod[3:4]
    logits = lax.dot_general(wrt[...], h2, (((1,), (1,)), ((), ())), preferred_element_type=F32,
                             precision=lax.Precision.HIGHEST)
    scores = jax.nn.sigmoid(logits)
    idxs, wts, ranks, counts = _route(scores + rbias[...], scores)
    for k in range(TOP_K):
        ridx_ref[k:k + 1, :] = idxs[k]
        rw_ref[k:k + 1, :] = wts[k]
        rank_ref[k:k + 1, :] = ranks[k]
    ridx_ref[TOP_K:, :] = jnp.zeros((SUBLANES - TOP_K, tm), I32)
    rw_ref[TOP_K:, :] = jnp.zeros((SUBLANES - TOP_K, tm), F32)
    rank_ref[TOP_K:, :] = jnp.zeros((SUBLANES - TOP_K, tm), I32)
    cnt_ref[0] = counts
    hb = h2.astype(BF16)
    g = jnp.dot(hb, wsg[...], preferred_element_type=F32)
    u = jnp.dot(hb, wsu[...], preferred_element_type=F32)
    act = (g * jax.nn.sigmoid(g) * u).astype(BF16)
    shared = jnp.dot(act, wsd[...], preferred_element_type=F32)
    base_ref[...] = x1 + mod[5:6] * shared
    _store_slabs(hp_ref, _pack_rows(h2))


def _merge(x2, ya, outs, lses, sga, sgb, mod3, norm2, wpa, wpb, wo, wrt, rbias, wsg, wsu, wsd, seq):
    n = x2.shape[0]
    tm = TM_MERGE
    assert seq % tm == 0
    row = lambda i: (i, 0)
    full = lambda a: pl.BlockSpec(a.shape, lambda i: (0,) * a.ndim)
    weights = [wpa, wpb, wo, wrt, rbias, wsg, wsu, wsd]
    group_specs = [pl.BlockSpec((tm // d, d * B_OUT_W), row) for _, d in B_GROUPS]
    lanes = lambda i: (0, i)
    return pl.pallas_call(
        _merge_kernel,
        out_shape=[jax.ShapeDtypeStruct((n, D_MODEL), F32),
                   jax.ShapeDtypeStruct((n * ROW_SLABS, LANES), U32),
                   jax.ShapeDtypeStruct((SUBLANES, n), I32),
                   jax.ShapeDtypeStruct((SUBLANES, n), F32),
                   jax.ShapeDtypeStruct((SUBLANES, n), I32),
                   jax.ShapeDtypeStruct((n // tm, N_EXPERTS, 1), I32)],
        grid=(n // tm,),
        in_specs=[pl.BlockSpec((tm, D_MODEL), row), pl.BlockSpec((tm, A_Q_W), row)]
                 + group_specs * 2
                 + [pl.BlockSpec((tm, D_MODEL), row)] * 2
                 + [pl.BlockSpec((1, 6, D_MODEL), lambda i: (i * tm // seq, 0, 0)),
                    pl.BlockSpec((1, D_MODEL), lambda i: (0, 0))]
                 + [full(w) for w in weights],
        out_specs=[pl.BlockSpec((tm, D_MODEL), row),
                   pl.BlockSpec((tm * ROW_SLABS, LANES), row),
                   pl.BlockSpec((SUBLANES, tm), lanes), pl.BlockSpec((SUBLANES, tm), lanes),
                   pl.BlockSpec((SUBLANES, tm), lanes),
                   pl.BlockSpec((1, N_EXPERTS, 1), lambda i: (i, 0, 0))],
        scratch_shapes=[pltpu.VMEM((6 * B_OUT_W // LANES, tm, LANES), F32)],
        compiler_params=_cparams(1),
        name="merge_route",
    )(x2, ya, *outs, *lses, sga, sgb, mod3, norm2.reshape(1, D_MODEL), *weights)


def _plan_kernel(ridx_ref, rank_ref, tb_ref, dest_ref):
    tb = tb_ref[0]
    t = ridx_ref.shape[1]
    iota_e = lax.broadcasted_iota(I32, (N_EXPERTS, t), 0)
    for k in range(TOP_K):
        base = jnp.sum(jnp.where(iota_e == ridx_ref[k:k + 1, :], tb, 0), axis=0, keepdims=True)
        dest_ref[0, :, k * t:(k + 1) * t] = base + rank_ref[k:k + 1, :]


def _plan(ridx, rank, tile_base, tm):
    n = ridx.shape[1]
    lanes = lambda i: (0, i)
    return pl.pallas_call(
        _plan_kernel,
        out_shape=jax.ShapeDtypeStruct((n // tm, 1, TOP_K * tm), I32),
        grid=(n // tm,),
        in_specs=[pl.BlockSpec((SUBLANES, tm), lanes), pl.BlockSpec((SUBLANES, tm), lanes),
                  pl.BlockSpec((1, N_EXPERTS, 1), lambda i: (i, 0, 0))],
        out_specs=pl.BlockSpec((1, 1, TOP_K * tm), lambda i: (i, 0, 0)),
        compiler_params=_cparams(1),
        name="moe_plan",
    )(ridx, rank, tile_base)


def _sc_mesh():
    return plsc.VectorSubcoreMesh(core_axis_name="core", subcore_axis_name="subcore",
                                  num_cores=SC_CORES, num_subcores=SC_SUBCORES)


def _sc_worker():
    return lax.axis_index("subcore") * SC_CORES + lax.axis_index("core")


def _sc_dispatch(hp, dest_rows, n_rows):
    n = hp.shape[0]
    halves = TM_MERGE // SC_CHUNK
    workers = SC_CORES * SC_SUBCORES
    tiles_per_w = n // TM_MERGE // workers
    idx_per_w = tiles_per_w * TOP_K * halves
    assert tiles_per_w * workers * TM_MERGE == n

    @functools.partial(
        pl.kernel, mesh=_sc_mesh(),
        out_type=jax.ShapeDtypeStruct((n_rows, ROW_SLABS, LANES), U32),
        scratch_types=[pltpu.VMEM((idx_per_w, SC_CHUNK), I32),
                       pltpu.VMEM((SC_CHUNK, ROW_SLABS, LANES), U32),
                       pltpu.SemaphoreType.DMA],
        name="moe_dispatch_sc")
    def scatter(hp_hbm, idx_hbm, xs_hbm, idx_v, rows_v, sem):
        wid = _sc_worker()
        pltpu.sync_copy(idx_hbm.at[pl.ds(wid * idx_per_w, idx_per_w)], idx_v)

        @pl.loop(0, tiles_per_w * halves)
        def _(j):
            tile = j // halves
            h = j - tile * halves
            tok = (wid * tiles_per_w + tile) * TM_MERGE + h * SC_CHUNK
            pltpu.sync_copy(hp_hbm.at[pl.ds(tok, SC_CHUNK)], rows_v)
            for k in range(TOP_K):
                row = (tile * TOP_K + k) * halves + h
                pltpu.async_copy(rows_v, xs_hbm.at[idx_v.at[row]], sem).wait()

    return scatter(hp, dest_rows)


def _sc_collect(ys, dest_rows):
    n_chunks = dest_rows.shape[0]
    workers = SC_CORES * SC_SUBCORES
    per_w = n_chunks // workers
    assert per_w * workers == n_chunks

    @functools.partial(
        pl.kernel, mesh=_sc_mesh(),
        out_type=jax.ShapeDtypeStruct((n_chunks * SC_CHUNK, ROW_SLABS, LANES), U32),
        scratch_types=[pltpu.VMEM((per_w, SC_CHUNK), I32),
                       pltpu.VMEM((SC_CHUNK, ROW_SLABS, LANES), U32),
                       pltpu.SemaphoreType.DMA],
        name="moe_collect_sc")
    def gather(ys_hbm, idx_hbm, out_hbm, idx_v, rows_v, sem):
        wid = _sc_worker()
        base = wid * per_w
        pltpu.sync_copy(idx_hbm.at[pl.ds(base, per_w)], idx_v)

        @pl.loop(0, per_w)
        def _(j):
            pltpu.async_copy(ys_hbm.at[idx_v.at[j]], rows_v, sem).wait()
            pltpu.sync_copy(rows_v, out_hbm.at[pl.ds((base + j) * SC_CHUNK, SC_CHUNK)])

    return gather(ys, dest_rows)


def _expert_kernel(blk_e_ref, valid_ref, xs_ref, wgu_ref, wd_ref, ys_ref):
    del blk_e_ref
    valid = valid_ref[pl.program_id(0)]

    @pl.when(valid > 0)
    def _():
        rows = lax.broadcasted_iota(I32, (xs_ref.shape[0] // ROW_SLABS, 1), 0)
        lo, hi = _unpack_rows(jnp.where(rows < valid, _load_slabs(xs_ref), jnp.uint32(0)))
        wgu = wgu_ref[0]
        gu = jnp.dot(lo.astype(BF16), wgu[:HALF_D], preferred_element_type=F32) \
            + jnp.dot(hi.astype(BF16), wgu[HALF_D:], preferred_element_type=F32)
        g = gu[:, :D_EXPERT]
        u = gu[:, D_EXPERT:]
        act = (g * jax.nn.sigmoid(g) * u).astype(BF16)
        _store_slabs(ys_ref, _pack_rows(jnp.dot(act, wd_ref[0], preferred_element_type=F32)))

    @pl.when(valid <= 0)
    def _():
        ys_ref[...] = jnp.zeros(ys_ref.shape, U32)


def _experts(xs, blk_e, blk_valid, wgu, wd):
    n_rows = xs.shape[0] // ROW_SLABS
    bm = MOE_BM
    nblk = n_rows // bm
    return pl.pallas_call(
        _expert_kernel,
        out_shape=jax.ShapeDtypeStruct((n_rows * ROW_SLABS, LANES), U32),
        grid_spec=pltpu.PrefetchScalarGridSpec(
            num_scalar_prefetch=2,
            grid=(nblk,),
            in_specs=[pl.BlockSpec((bm * ROW_SLABS, LANES), lambda i, be, nu: (i, 0)),
                      pl.BlockSpec((1, D_MODEL, 2 * D_EXPERT), lambda i, be, nu: (be[i], 0, 0)),
                      pl.BlockSpec((1, D_EXPERT, D_MODEL), lambda i, be, nu: (be[i], 0, 0))],
            out_specs=pl.BlockSpec((bm * ROW_SLABS, LANES), lambda i, be, nu: (i, 0))),
        compiler_params=_cparams(1),
        name="moe_experts",
    )(blk_e, blk_valid, xs, wgu, wd)


def _combine_kernel(g_ref, base_ref, rw_ref, mod_ref, fn_ref, y_ref):
    tm = base_ref.shape[0]
    rw = rw_ref[...]
    acc_lo = jnp.zeros((tm, HALF_D), F32)
    acc_hi = jnp.zeros((tm, HALF_D), F32)
    for k in range(TOP_K):
        lo, hi = _unpack_rows(_load_slabs(g_ref.at[0, k]))
        w = rw[:, k:k + 1]
        acc_lo = acc_lo + w * lo
        acc_hi = acc_hi + w * hi
    routed = jnp.concatenate([acc_lo, acc_hi], axis=1)
    x2 = base_ref[...] + mod_ref[0][5:6] * routed
    y_ref[...] = _rms(x2, fn_ref[...])


def _combine(g, base, rw_t, mod3, final_norm, seq):
    n = base.shape[0]
    tm = TM_COMBINE
    assert seq % tm == 0
    g4 = g.reshape(n // tm, TOP_K, tm * ROW_SLABS, LANES)
    return pl.pallas_call(
        _combine_kernel,
        out_shape=jax.ShapeDtypeStruct((n, D_MODEL), F32),
        grid=(n // tm,),
        in_specs=[pl.BlockSpec((1, TOP_K, tm * ROW_SLABS, LANES), lambda i: (i, 0, 0, 0)),
                  pl.BlockSpec((tm, D_MODEL), lambda i: (i, 0)),
                  pl.BlockSpec((tm, SUBLANES), lambda i: (i, 0)),
                  pl.BlockSpec((1, 6, D_MODEL), lambda i: (i * tm // seq, 0, 0)),
                  pl.BlockSpec((1, D_MODEL), lambda i: (0, 0))],
        out_specs=pl.BlockSpec((tm, D_MODEL), lambda i: (i, 0)),
        compiler_params=_cparams(1),
        name="moe_combine",
    )(g4, base, rw_t, mod3, final_norm.reshape(1, D_MODEL))


def _block_layout(tile_counts, n):
    bm = MOE_BM
    c = tile_counts[:, :, 0]
    ntiles = c.shape[0]
    counts = jnp.sum(c, axis=0)
    padded = (counts + bm - 1) // bm * bm
    earlier_e = np.tri(N_EXPERTS, k=-1, dtype=bool)
    pstart = jnp.sum(jnp.where(earlier_e, padded[None, :], 0), axis=1)
    pend = pstart + padded
    earlier_t = np.tri(ntiles, k=-1, dtype=bool)
    tile_base = pstart[None, :] + jnp.sum(jnp.where(earlier_t[:, :, None], c[None], 0), axis=1)
    nblk = -(-n * TOP_K // bm) + N_EXPERTS
    blk = jnp.arange(nblk, dtype=I32)
    blk_e = jnp.minimum(jnp.sum((pend[None, :] <= blk[:, None] * bm).astype(I32), axis=1),
                        N_EXPERTS - 1)
    onehot_e = blk_e[:, None] == jnp.arange(N_EXPERTS, dtype=I32)[None, :]
    end_e = jnp.sum(jnp.where(onehot_e, (pstart + counts)[None, :], 0), axis=1)
    blk_valid = jnp.clip(end_e - blk * bm, 0, bm).astype(I32)
    return tile_base.astype(I32)[:, :, None], blk_e.astype(I32), blk_valid, nblk * bm


def _trunk(x, mod3, p):
    nbatch, seq, _ = x.shape
    n = nbatch * seq
    x2 = x.reshape(n, D_MODEL)
    proj = _inproj(x2, mod3, p["norm1"], p["w_in"], seq)
    qa, kva = proj[:2]
    sga, sgb = proj[8:]
    (ya,) = _band_attention(qa, kva, p["bias_a"], p["sink_a"], nbatch=nbatch, seq=seq, dil=1,
                            halo=A_HALF_WINDOW, step_rows=ATTN_ROWS_A, shared_kv=True, want_lse=False)
    outs, lses = [], []
    for gi, (w, d) in enumerate(B_GROUPS):
        q, kv = proj[2 + 2 * gi:4 + 2 * gi]
        o, lse = _band_attention(q, kv, p["bias_b"][gi], None, nbatch=nbatch, seq=seq, dil=d,
                                 halo=w // (2 * d), step_rows=ATTN_ROWS_B, shared_kv=False,
                                 want_lse=True)
        outs.append(o)
        lses.append(lse)
    base, hp, ridx, rw, rank, tile_counts = _merge(
        x2, ya, outs, lses, sga, sgb, mod3, p["norm2"], p["w_pa"], p["w_pb"], p["w_o"], p["w_rt"],
        p["rbias"], p["ws_gate"], p["ws_up"], p["ws_down"], seq)
    assert TM_MERGE == TM_COMBINE
    tile_base, blk_e, blk_valid, n_rows = _block_layout(tile_counts, n)
    dest_rows = _plan(ridx, rank, tile_base, TM_MERGE).reshape(-1, SC_CHUNK)
    slabs = lambda a: a.reshape(-1, ROW_SLABS, LANES)
    flat = lambda a: a.reshape(-1, LANES)
    xs = _sc_dispatch(slabs(hp), dest_rows, n_rows)
    ys = _experts(flat(xs), blk_e, blk_valid, p["w_gu"], p["w_down"])
    g = _sc_collect(slabs(ys), dest_rows)
    y = _combine(g, base, rw.T, mod3, p["final_norm"], seq)
    return y.reshape(nbatch, seq, D_MODEL)


def kernel(x_prompt, x_sample, c_prompt, c_sample, rel_bias, w_ada, b_ada, norm1, w_in, sink, w_pa, w_pb, w_o, norm2, w_router, router_bias, w_gate, w_up, w_down, ws_gate, ws_up, ws_down, final_norm):
    assert w_ada.shape[0] == 1
    nbp = x_prompt.shape[0]
    mod = _ada(jnp.concatenate([c_prompt, c_sample], axis=0), w_ada[0], b_ada[0])
    mod3 = mod.reshape(-1, 6, D_MODEL)

    def pair_bias(heads, halo, dist_scale):
        kb = ATTN_BLOCK + 2 * halo
        tab = _bias_table(heads, ATTN_BLOCK, kb, halo, halo, dist_scale)
        return tab.reshape(-1, 2, ATTN_BLOCK, kb).transpose(0, 2, 1, 3).reshape(-1, ATTN_BLOCK, 2 * kb)

    bias_a = pair_bias(rel_bias[:, :A_Q_HEADS], A_HALF_WINDOW, 1)
    bias_b = []
    for gi, (w, d) in enumerate(B_GROUPS):
        h0 = A_Q_HEADS + gi * B_HEADS_PER_GROUP
        bias_b.append(pair_bias(rel_bias[:, h0:h0 + B_HEADS_PER_GROUP], w // (2 * d), d))
    p = {
        "norm1": norm1[0], "norm2": norm2[0], "final_norm": final_norm,
        "w_in": w_in[0].astype(BF16),
        "bias_a": bias_a,
        "sink_a": jnp.repeat(sink[0].astype(F32), HEAD_DIM).reshape(A_Q_HEADS // 2, 1, LANES),
        "bias_b": bias_b,
        "w_pa": w_pa[0].astype(BF16), "w_pb": w_pb[0].astype(BF16), "w_o": w_o[0].astype(BF16),
        "w_rt": w_router[0].T, "rbias": router_bias[0].reshape(N_EXPERTS, 1),
        "ws_gate": ws_gate[0].astype(BF16), "ws_up": ws_up[0].astype(BF16),
        "ws_down": ws_down[0].astype(BF16),
        "w_gu": jnp.concatenate([w_gate[0], w_up[0]], axis=-1).astype(BF16),
        "w_down": w_down[0].astype(BF16),
    }
    y_prompt = _trunk(x_prompt, mod3[:nbp], p)
    y_sample = _trunk(x_sample, mod3[nbp:], p)
    return (y_prompt, y_sample)
```

```python
import functools
import math

import jax
import jax.numpy as jnp
import numpy as np
from jax import lax
from jax.experimental import pallas as pl
from jax.experimental.pallas import tpu as pltpu
from jax.experimental.pallas import tpu_sc as plsc

F32 = jnp.float32
BF16 = jnp.bfloat16
U32 = jnp.uint32
I32 = jnp.int32

D_MODEL = 1024
HEAD_DIM = 64
A_Q_HEADS = 8
A_KV_HEADS = 2
A_HALF_WINDOW = 128
B_GROUPS = ((128, 1), (512, 4), (2048, 16))
B_HEADS_PER_GROUP = 4
N_BUCKETS = 32
MAX_DISTANCE = 1024
N_EXPERTS = 64
TOP_K = 6
N_EXPERT_GROUPS = 8
TOPK_GROUPS = 4
D_EXPERT = 256
ROUTED_SCALE = 2.5
RMS_EPS = 1e-6
NEG_INF = -1e30
REMOVED = -3e38

A_Q_W = A_Q_HEADS * HEAD_DIM
A_KV_W = A_KV_HEADS * HEAD_DIM
B_W = len(B_GROUPS) * B_HEADS_PER_GROUP * HEAD_DIM
B_OUT_W = B_HEADS_PER_GROUP * HEAD_DIM
D_IN = A_Q_W + 2 * A_KV_W + 3 * B_W + 2 * D_MODEL
HALF_D = D_MODEL // 2
LANES = 128
SUBLANES = 8
ROW_SLABS = HALF_D // LANES
SC_CORES = 2
SC_SUBCORES = 16
SC_CHUNK = 128

TM_INPROJ = 1024
TM_MERGE = 512
ATTN_BLOCK = 128
ATTN_ROWS_A = 512
ATTN_ROWS_B = 2048
MOE_BM = 1024
TM_COMBINE = 512
COLLECT_PARTS = 2
VMEM_LIMIT = 60 * 1024 * 1024


def _cparams(n_axes):
    return pltpu.CompilerParams(
        dimension_semantics=("arbitrary",) * n_axes, vmem_limit_bytes=VMEM_LIMIT)


def _ada_kernel(c_ref, w_ref, b_ref, o_ref):
    c = c_ref[...]
    s = c * jax.nn.sigmoid(c)
    o_ref[...] = jnp.dot(s, w_ref[...], preferred_element_type=F32,
                         precision=lax.Precision.HIGHEST) + b_ref[...]


def _ada(c_all, w_ada, b_ada):
    nb = c_all.shape[0]
    return pl.pallas_call(
        _ada_kernel,
        out_shape=jax.ShapeDtypeStruct((nb, 6 * D_MODEL), F32),
        grid=(6,),
        in_specs=[pl.BlockSpec((nb, D_MODEL), lambda j: (0, 0)),
                  pl.BlockSpec((D_MODEL, D_MODEL), lambda j: (0, j)),
                  pl.BlockSpec((1, D_MODEL), lambda j: (0, j))],
        out_specs=pl.BlockSpec((nb, D_MODEL), lambda j: (0, j)),
        compiler_params=_cparams(1),
        name="ada",
    )(c_all, w_ada, b_ada.reshape(1, 6 * D_MODEL))


def _rms(x, g):
    return x * lax.rsqrt(jnp.mean(x * x, axis=-1, keepdims=True) + RMS_EPS) * g


def _inproj_kernel(x_ref, mod_ref, n1_ref, w_ref, qa, kva, *rest):
    q_b = rest[0:6:2]
    kv_b = rest[1:6:2]
    sga, sgb, scr = rest[6:]
    mod = mod_ref[0]
    h = _rms(x_ref[...], n1_ref[...]) * (1.0 + mod[1:2]) + mod[0:1]
    hb = h.astype(BF16)
    tm = hb.shape[0]
    scale = HEAD_DIM ** -0.5

    def proj(off, width):
        return jnp.dot(hb, w_ref[:, off:off + width], preferred_element_type=F32)

    qa[...] = (proj(0, A_Q_W) * scale).astype(BF16)
    for j, off in enumerate((A_Q_W, A_Q_W + A_KV_W)):
        r = proj(off, A_KV_W)
        kva[:, 2 * j * A_KV_W:(2 * j + 1) * A_KV_W] = r.astype(BF16)
        kva[:, (2 * j + 1) * A_KV_W:(2 * j + 2) * A_KV_W] = pltpu.roll(r, HEAD_DIM, 1).astype(BF16)
    off = A_Q_W + 2 * A_KV_W
    for t in range(3):
        for gi, (_, d) in enumerate(B_GROUPS):
            r = proj(off + t * B_W + gi * B_OUT_W, B_OUT_W)
            if t == 0:
                r = r * scale
            ref, width, base = (q_b[gi], B_OUT_W, 0) if t == 0 else (kv_b[gi], 2 * B_OUT_W,
                                                                     (t - 1) * B_OUT_W)
            if d == 1:
                ref[:, base:base + B_OUT_W] = r.astype(BF16)
            else:
                for c in range(B_OUT_W // LANES):
                    slot = (t * 2 + gi - 1) * (B_OUT_W // LANES) + c
                    scr[slot] = r[:, c * LANES:(c + 1) * LANES]
                    for res in range(d):
                        col = res * width + base + c * LANES
                        ref[:, col:col + LANES] = scr[
                            slot, pl.ds(res, tm // d, stride=d), :].astype(BF16)
    off += 3 * B_W
    sga[...] = jax.nn.sigmoid(proj(off, D_MODEL)).astype(BF16)
    sgb[...] = jax.nn.sigmoid(proj(off + D_MODEL, D_MODEL)).astype(BF16)


def _inproj(x2, mod3, norm1, w_in_bf, seq):
    n = x2.shape[0]
    tm = TM_INPROJ
    assert seq % tm == 0 and n % tm == 0
    row = lambda i: (i, 0)
    shapes = [(n, A_Q_W, tm), (n, 4 * A_KV_W, tm)]
    for _, d in B_GROUPS:
        shapes += [(n // d, d * B_OUT_W, tm // d), (n // d, d * 2 * B_OUT_W, tm // d)]
    shapes += [(n, D_MODEL, tm)] * 2
    return pl.pallas_call(
        _inproj_kernel,
        out_shape=[jax.ShapeDtypeStruct((r, c), BF16) for r, c, _ in shapes],
        grid=(n // tm,),
        in_specs=[pl.BlockSpec((tm, D_MODEL), row),
                  pl.BlockSpec((1, 6, D_MODEL), lambda i: (i * tm // seq, 0, 0)),
                  pl.BlockSpec((1, D_MODEL), lambda i: (0, 0)),
                  pl.BlockSpec((D_MODEL, D_IN), lambda i: (0, 0), pipeline_mode=pl.Buffered(1))],
        out_specs=[pl.BlockSpec((b, c), row) for _, c, b in shapes],
        scratch_shapes=[pltpu.VMEM((6 * B_OUT_W // LANES, tm, LANES), F32)],
        compiler_params=_cparams(1),
        name="inproj",
    )(x2, mod3, norm1.reshape(1, D_MODEL), w_in_bf)


def _rel_bucket_np(rel):
    half = N_BUCKETS // 2
    max_exact = half // 2
    n = np.abs(rel)
    large = max_exact + (np.log(np.maximum(n, 1) / max_exact) / math.log(MAX_DISTANCE / max_exact)
                         * (half - max_exact)).astype(np.int32)
    large = np.minimum(large, half - 1)
    return ((rel > 0).astype(np.int32) * half + np.where(n < max_exact, n, large)).astype(np.int32)


def _bias_table(rel_bias_heads, n_q, n_k, key_off, band, dist_scale):
    p = n_q + n_k
    rel = np.arange(p) - (n_q - 1) - key_off
    bucket = _rel_bucket_np(rel * dist_scale)
    t = jnp.where((np.abs(rel) <= band)[None], rel_bias_heads.astype(F32)[bucket].T, NEG_INF)
    big = jnp.tile(t, (1, n_q + 1))
    tab = big[:, n_q - 1:n_q - 1 + n_q * (p - 1)].reshape(-1, n_q, p - 1)
    return tab[:, :, :n_k]


def _pair_rhs(k_top, k_bot, v_top, v_bot):
    kb = k_top.shape[0]
    low = jnp.where(lax.broadcasted_iota(I32, (kb, LANES), 1) < HEAD_DIM, 1.0, 0.0).astype(BF16)
    high = jnp.where(lax.broadcasted_iota(I32, (kb, LANES), 1) < HEAD_DIM, 0.0, 1.0).astype(BF16)
    rhs_k = jnp.concatenate([k_top * low, k_bot * high], axis=0)
    rhs_v = jnp.concatenate([jnp.concatenate([v_top * low, low], axis=1),
                             jnp.concatenate([v_bot * high, high], axis=1)], axis=0)
    return rhs_k, rhs_v


def _pair_attention(q_pair, rhs_k, rhs_v, bias_pair, pen, sink_pair, want_lse):
    kb = rhs_k.shape[0] // 2
    s = lax.dot_general(q_pair, rhs_k, (((1,), (1,)), ((), ())), preferred_element_type=F32)
    s = s + bias_pair
    if pen is not None:
        s = s + pen
    s0, s1 = s[:, :kb], s[:, kb:]
    m0 = jnp.max(s0, axis=-1, keepdims=True)
    m1 = jnp.max(s1, axis=-1, keepdims=True)
    if sink_pair is not None:
        m0 = jnp.maximum(m0, sink_pair[:, 0:1])
        m1 = jnp.maximum(m1, sink_pair[:, HEAD_DIM:HEAD_DIM + 1])
    p = jnp.concatenate([jnp.exp(s0 - m0), jnp.exp(s1 - m1)], axis=1).astype(BF16)
    od = jnp.dot(p, rhs_v, preferred_element_type=F32)
    o, den = od[:, :LANES], od[:, LANES:]
    low = lax.broadcasted_iota(I32, o.shape, 1) < HEAD_DIM
    m_full = jnp.where(low, m0, m1)
    if sink_pair is not None:
        den = den + jnp.exp(sink_pair - m_full)
    return o / den, (m_full + jnp.log(den)) if want_lse else None


def _attn_kernel(*refs, halo, nblk, n_pairs, n_seqs, shared_kv, has_sink, want_lse, n_steps):
    it = iter(refs)
    q_ref = next(it)
    kv_prev, kv_cur, kv_next = next(it), next(it), next(it)
    bias_ref = next(it)
    sink_ref = next(it) if has_sink else None
    o_ref = next(it)
    lse_ref = next(it) if want_lse else None
    step = pl.program_id(2)
    blk = ATTN_BLOCK
    kb = blk + 2 * halo
    qw = n_pairs * LANES
    kv = jnp.concatenate([kv_prev[...], kv_cur[...], kv_next[...]], axis=0)
    col = lax.broadcasted_iota(I32, (1, kb), 1)
    for b in range(nblk):
        lo = jnp.where(step == 0, halo, 0) if b == 0 else 0
        hi = jnp.where(step == n_steps - 1, kb - halo, kb) if b == nblk - 1 else kb
        pen = None
        if b == 0 or b == nblk - 1:
            pen1 = jnp.where((col >= lo) & (col < hi), 0.0, NEG_INF)
            pen = jnp.concatenate([pen1, pen1], axis=1)
        rows = slice(b * blk, (b + 1) * blk)
        krows = slice(b * blk, b * blk + kb)
        lane_col = lambda j: kv[krows, j * LANES:(j + 1) * LANES]
        if shared_kv:
            k, k_sw, v, v_sw = (lane_col(j) for j in range(4))
            rhs = [_pair_rhs(k, k_sw, v, v_sw), _pair_rhs(k_sw, k, v_sw, v)]
        for s in range(n_seqs):
            for c in range(n_pairs):
                lanes = slice(s * qw + c * LANES, s * qw + (c + 1) * LANES)
                if shared_kv:
                    rhs_k, rhs_v = rhs[c // (n_pairs // 2)]
                else:
                    k, v = lane_col(2 * s * n_pairs + c), lane_col((2 * s + 1) * n_pairs + c)
                    rhs_k, rhs_v = _pair_rhs(k, k, v, v)
                o, lse = _pair_attention(q_ref[rows, lanes], rhs_k, rhs_v, bias_ref[c], pen,
                                         sink_ref[c] if has_sink else None, want_lse)
                o_ref[rows, lanes] = o.astype(o_ref.dtype)
                if want_lse:
                    lse_ref[rows, lanes] = lse


def _band_attention(q, kv, bias, sink, *, nbatch, seq, dil, halo, step_rows, shared_kv, want_lse):
    n = nbatch * seq
    sub_len = seq // dil
    qw = q.shape[1] // dil
    kv_width = kv.shape[1] // dil
    rows = min(step_rows, sub_len)
    n_seqs = min(dil, step_rows // rows)
    assert sub_len % rows == 0 and rows % ATTN_BLOCK == 0 and rows % halo == 0 and dil % n_seqs == 0
    nq = sub_len // rows
    per = rows // halo
    total_halos = n // dil // halo
    cur = lambda b, r, i: (b * nq + i, r)
    prev = lambda b, r, i: (jnp.maximum((b * nq + i) * per - 1, 0), r)
    nxt = lambda b, r, i: (jnp.minimum((b * nq + i + 1) * per, total_halos - 1), r)
    const = lambda a: pl.BlockSpec(a.shape, lambda b, r, i: (0,) * a.ndim)
    qw_step, kvw_step = qw * n_seqs, kv_width * n_seqs

    in_specs = [pl.BlockSpec((rows, qw_step), cur), pl.BlockSpec((halo, kvw_step), prev),
                pl.BlockSpec((rows, kvw_step), cur), pl.BlockSpec((halo, kvw_step), nxt),
                const(bias)]
    args = [q, kv, kv, kv, bias]
    if sink is not None:
        in_specs.append(const(sink))
        args.append(sink)
    out_shape = [jax.ShapeDtypeStruct(q.shape, BF16)]
    out_specs = [pl.BlockSpec((rows, qw_step), cur)]
    if want_lse:
        out_shape.append(jax.ShapeDtypeStruct(q.shape, F32))
        out_specs.append(pl.BlockSpec((rows, qw_step), cur))
    return pl.pallas_call(
        functools.partial(_attn_kernel, halo=halo, nblk=rows // ATTN_BLOCK, n_pairs=qw // LANES,
                          n_seqs=n_seqs, shared_kv=shared_kv, has_sink=sink is not None,
                          want_lse=want_lse, n_steps=nq),
        out_shape=out_shape,
        grid=(nbatch, dil // n_seqs, nq),
        in_specs=in_specs,
        out_specs=out_specs,
        compiler_params=_cparams(3),
        name=f"band_attn_d{dil}",
    )(*args)


def _pack_rows(y):
    return pltpu.pack_elementwise([y[:, :HALF_D], y[:, HALF_D:]], packed_dtype=BF16)


def _store_slabs(ref, packed):
    t = packed.shape[0]
    for c in range(ROW_SLABS):
        ref[pl.ds(c, t, stride=ROW_SLABS), :] = packed[:, c * LANES:(c + 1) * LANES]


def _load_slabs(ref):
    t = ref.shape[0] // ROW_SLABS
    return jnp.concatenate([ref[pl.ds(c, t, stride=ROW_SLABS), :] for c in range(ROW_SLABS)], axis=1)


def _unpack_rows(p):
    return tuple(pltpu.unpack_elementwise(p, index=i, packed_dtype=BF16, unpacked_dtype=F32)
                 for i in range(2))


def _route(sel, scores):
    t = sel.shape[-1]
    per = N_EXPERTS // N_EXPERT_GROUPS
    shape3 = (N_EXPERT_GROUPS, per, t)
    sel3 = sel.reshape(shape3)
    sc3 = scores.reshape(shape3)
    iota_g = lax.broadcasted_iota(I32, shape3, 0)
    iota_m = lax.broadcasted_iota(I32, shape3, 1)
    iota_e = iota_g * per + iota_m
    m1 = jnp.max(sel3, axis=1, keepdims=True)
    i1 = jnp.min(jnp.where(sel3 == m1, iota_m, per), axis=1, keepdims=True)
    m2 = jnp.max(jnp.where(iota_m == i1, REMOVED, sel3), axis=1, keepdims=True)
    gscore = m1 + m2
    iota_g1 = lax.broadcasted_iota(I32, gscore.shape, 0)
    gmask = jnp.zeros(gscore.shape, jnp.bool_)
    for _ in range(TOPK_GROUPS):
        mx = jnp.max(gscore, axis=0, keepdims=True)
        ix = jnp.min(jnp.where(gscore == mx, iota_g1, N_EXPERT_GROUPS), axis=0, keepdims=True)
        hit = iota_g1 == ix
        gmask = gmask | hit
        gscore = jnp.where(hit, REMOVED, gscore)
    cur = jnp.where(gmask, sel3, NEG_INF)
    sum_all = lambda a: jnp.sum(jnp.sum(a, axis=1, keepdims=True), axis=0, keepdims=True)
    idxs, wts, hits = [], [], []
    for _ in range(TOP_K):
        mx = jnp.max(jnp.max(cur, axis=1, keepdims=True), axis=0, keepdims=True)
        cand = jnp.where(cur == mx, iota_e, N_EXPERTS)
        ix = jnp.min(jnp.min(cand, axis=1, keepdims=True), axis=0, keepdims=True)
        hit = iota_e == ix
        wts.append(sum_all(jnp.where(hit, sc3, 0.0)).reshape(1, t))
        cur = jnp.where(hit, REMOVED, cur)
        idxs.append(ix.reshape(1, t))
        hits.append(hit)
    wsum = wts[0]
    for w in wts[1:]:
        wsum = wsum + w
    wts = [w / wsum * ROUTED_SCALE for w in wts]
    onehot = hits[0].astype(F32)
    for hit in hits[1:]:
        onehot = onehot + hit.astype(F32)
    tri = (lax.broadcasted_iota(I32, (t, t), 0) <= lax.broadcasted_iota(I32, (t, t), 1))
    cum = jnp.dot(onehot.reshape(N_EXPERTS, t).astype(BF16), tri.astype(F32).astype(BF16),
                  preferred_element_type=F32)
    cum3 = cum.reshape(shape3) - 1.0
    ranks = [sum_all(jnp.where(hit, cum3, 0.0)).reshape(1, t).astype(I32) for hit in hits]
    counts = cum[:, t - 1:t].astype(I32)
    return idxs, wts, ranks, counts


def _merge_kernel(x_ref, ya_ref, o1, o2, o3, l1, l2, l3, sga_ref, sgb_ref, mod_ref, n2_ref,
                  wpa, wpb, wo, wrt, rbias, wsg, wsu, wsd,
                  base_ref, hp_ref, ridx_ref, rw_ref, rank_ref, cnt_ref, scr):
    mod = mod_ref[0]
    tm = x_ref.shape[0]

    def token_major(ref, slot, d):
        if d == 1:
            return ref[...].astype(F32)
        nc = B_OUT_W // LANES
        for c in range(nc):
            for res in range(d):
                col = res * B_OUT_W + c * LANES
                scr[slot * nc + c, pl.ds(res, tm // d, stride=d), :] = ref[
                    :, col:col + LANES].astype(F32)
        return jnp.concatenate([scr[slot * nc + c] for c in range(nc)], axis=1)

    dils = [d for _, d in B_GROUPS]
    os_ = [token_major(r, i, d) for i, (r, d) in enumerate(zip((o1, o2, o3), dils))]
    ls = [token_major(r, 3 + i, d) for i, (r, d) in enumerate(zip((l1, l2, l3), dils))]
    mx = jnp.maximum(jnp.maximum(ls[0], ls[1]), ls[2])
    es = [jnp.exp(l - mx) for l in ls]
    den = es[0] + es[1] + es[2]
    ob = (es[0] / den) * os_[0] + (es[1] / den) * os_[1] + (es[2] / den) * os_[2]
    pa = jnp.dot(ya_ref[...], wpa[...], preferred_element_type=F32)
    pb = jnp.dot(ob.astype(BF16), wpb[...], preferred_element_type=F32)
    merged = sga_ref[...].astype(F32) * pa + sgb_ref[...].astype(F32) * pb
    mix = jnp.dot(merged.astype(BF16), wo[...], preferred_element_type=F32)
    x1 = x_ref[...] + mod[2:3] * mix
    h2 = _rms(x1, n2_ref[...]) * (1.0 + mod[4:5]) + mod[3:4]
    logits = lax.dot_general(wrt[...], h2, (((1,), (1,)), ((), ())), preferred_element_type=F32,
                             precision=lax.Precision.HIGHEST)
    scores = jax.nn.sigmoid(logits)
    idxs, wts, ranks, counts = _route(scores + rbias[...], scores)
    for k in range(TOP_K):
        ridx_ref[k:k + 1, :] = idxs[k]
        rw_ref[k:k + 1, :] = wts[k]
        rank_ref[k:k + 1, :] = ranks[k]
    ridx_ref[TOP_K:, :] = jnp.zeros((SUBLANES - TOP_K, tm), I32)
    rw_ref[TOP_K:, :] = jnp.zeros((SUBLANES - TOP_K, tm), F32)
    rank_ref[TOP_K:, :] = jnp.zeros((SUBLANES - TOP_K, tm), I32)
    cnt_ref[0] = counts
    hb = h2.astype(BF16)
    g = jnp.dot(hb, wsg[...], preferred_element_type=F32)
    u = jnp.dot(hb, wsu[...], preferred_element_type=F32)
    act = (g * jax.nn.sigmoid(g) * u).astype(BF16)
    shared = jnp.dot(act, wsd[...], preferred_element_type=F32)
    base_ref[...] = x1 + mod[5:6] * shared
    _store_slabs(hp_ref, _pack_rows(h2))


def _merge(x2, ya, outs, lses, sga, sgb, mod3, norm2, wpa, wpb, wo, wrt, rbias, wsg, wsu, wsd, seq):
    n = x2.shape[0]
    tm = TM_MERGE
    assert seq % tm == 0
    row = lambda i: (i, 0)
    full = lambda a: pl.BlockSpec(a.shape, lambda i: (0,) * a.ndim)
    weights = [wpa, wpb, wo, wrt, rbias, wsg, wsu, wsd]
    group_specs = [pl.BlockSpec((tm // d, d * B_OUT_W), row) for _, d in B_GROUPS]
    lanes = lambda i: (0, i)
    return pl.pallas_call(
        _merge_kernel,
        out_shape=[jax.ShapeDtypeStruct((n, D_MODEL), F32),
                   jax.ShapeDtypeStruct((n * ROW_SLABS, LANES), U32),
                   jax.ShapeDtypeStruct((SUBLANES, n), I32),
                   jax.ShapeDtypeStruct((SUBLANES, n), F32),
                   jax.ShapeDtypeStruct((SUBLANES, n), I32),
                   jax.ShapeDtypeStruct((n // tm, N_EXPERTS, 1), I32)],
        grid=(n // tm,),
        in_specs=[pl.BlockSpec((tm, D_MODEL), row), pl.BlockSpec((tm, A_Q_W), row)]
                 + group_specs * 2
                 + [pl.BlockSpec((tm, D_MODEL), row)] * 2
                 + [pl.BlockSpec((1, 6, D_MODEL), lambda i: (i * tm // seq, 0, 0)),
                    pl.BlockSpec((1, D_MODEL), lambda i: (0, 0))]
                 + [full(w) for w in weights],
        out_specs=[pl.BlockSpec((tm, D_MODEL), row),
                   pl.BlockSpec((tm * ROW_SLABS, LANES), row),
                   pl.BlockSpec((SUBLANES, tm), lanes), pl.BlockSpec((SUBLANES, tm), lanes),
                   pl.BlockSpec((SUBLANES, tm), lanes),
                   pl.BlockSpec((1, N_EXPERTS, 1), lambda i: (i, 0, 0))],
        scratch_shapes=[pltpu.VMEM((6 * B_OUT_W // LANES, tm, LANES), F32)],
        compiler_params=_cparams(1),
        name="merge_route",
    )(x2, ya, *outs, *lses, sga, sgb, mod3, norm2.reshape(1, D_MODEL), *weights)


def _plan_kernel(ridx_ref, rank_ref, tb_ref, dest_ref):
    tb = tb_ref[0]
    t = ridx_ref.shape[1]
    iota_e = lax.broadcasted_iota(I32, (N_EXPERTS, t), 0)
    for k in range(TOP_K):
        base = jnp.sum(jnp.where(iota_e == ridx_ref[k:k + 1, :], tb, 0), axis=0, keepdims=True)
        dest_ref[0, :, k * t:(k + 1) * t] = base + rank_ref[k:k + 1, :]


def _plan(ridx, rank, tile_base, tm):
    n = ridx.shape[1]
    lanes = lambda i: (0, i)
    return pl.pallas_call(
        _plan_kernel,
        out_shape=jax.ShapeDtypeStruct((n // tm, 1, TOP_K * tm), I32),
        grid=(n // tm,),
        in_specs=[pl.BlockSpec((SUBLANES, tm), lanes), pl.BlockSpec((SUBLANES, tm), lanes),
                  pl.BlockSpec((1, N_EXPERTS, 1), lambda i: (i, 0, 0))],
        out_specs=pl.BlockSpec((1, 1, TOP_K * tm), lambda i: (i, 0, 0)),
        compiler_params=_cparams(1),
        name="moe_plan",
    )(ridx, rank, tile_base)


def _sc_mesh():
    return plsc.VectorSubcoreMesh(core_axis_name="core", subcore_axis_name="subcore",
                                  num_cores=SC_CORES, num_subcores=SC_SUBCORES)


def _sc_worker():
    return lax.axis_index("subcore") * SC_CORES + lax.axis_index("core")


def _sc_dispatch(hp, dest_rows, n_rows):
    n = hp.shape[0]
    halves = TM_MERGE // SC_CHUNK
    workers = SC_CORES * SC_SUBCORES
    tiles_per_w = n // TM_MERGE // workers
    idx_per_w = tiles_per_w * TOP_K * halves
    assert tiles_per_w * workers * TM_MERGE == n

    @functools.partial(
        pl.kernel, mesh=_sc_mesh(),
        out_type=jax.ShapeDtypeStruct((n_rows, ROW_SLABS, LANES), U32),
        scratch_types=[pltpu.VMEM((idx_per_w, SC_CHUNK), I32),
                       pltpu.VMEM((SC_CHUNK, ROW_SLABS, LANES), U32),
                       pltpu.SemaphoreType.DMA],
        name="moe_dispatch_sc")
    def scatter(hp_hbm, idx_hbm, xs_hbm, idx_v, rows_v, sem):
        wid = _sc_worker()
        pltpu.sync_copy(idx_hbm.at[pl.ds(wid * idx_per_w, idx_per_w)], idx_v)

        @pl.loop(0, tiles_per_w * halves)
        def _(j):
            tile = j // halves
            h = j - tile * halves
            tok = (wid * tiles_per_w + tile) * TM_MERGE + h * SC_CHUNK
            pltpu.sync_copy(hp_hbm.at[pl.ds(tok, SC_CHUNK)], rows_v)
            for k in range(TOP_K):
                row = (tile * TOP_K + k) * halves + h
                pltpu.async_copy(rows_v, xs_hbm.at[idx_v.at[row]], sem).wait()

    return scatter(hp, dest_rows)


def _sc_collect(ys, dest_rows):
    n_chunks = dest_rows.shape[0]
    workers = SC_CORES * SC_SUBCORES
    per_w = n_chunks // workers
    assert per_w * workers == n_chunks

    @functools.partial(
        pl.kernel, mesh=_sc_mesh(),
        out_type=jax.ShapeDtypeStruct((n_chunks * SC_CHUNK, ROW_SLABS, LANES), U32),
        scratch_types=[pltpu.VMEM((per_w, SC_CHUNK), I32),
                       pltpu.VMEM((SC_CHUNK, ROW_SLABS, LANES), U32),
                       pltpu.SemaphoreType.DMA],
        name="moe_collect_sc")
    def gather(ys_hbm, idx_hbm, out_hbm, idx_v, rows_v, sem):
        wid = _sc_worker()
        base = wid * per_w
        pltpu.sync_copy(idx_hbm.at[pl.ds(base, per_w)], idx_v)

        @pl.loop(0, per_w)
        def _(j):
            pltpu.async_copy(ys_hbm.at[idx_v.at[j]], rows_v, sem).wait()
            pltpu.sync_copy(rows_v, out_hbm.at[pl.ds((base + j) * SC_CHUNK, SC_CHUNK)])

    return gather(ys, dest_rows)


def _expert_kernel(blk_e_ref, valid_ref, xs_ref, wgu_ref, wd_ref, ys_ref):
    del blk_e_ref
    valid = valid_ref[pl.program_id(0)]

    @pl.when(valid > 0)
    def _():
        rows = lax.broadcasted_iota(I32, (xs_ref.shape[0] // ROW_SLABS, 1), 0)
        lo, hi = _unpack_rows(jnp.where(rows < valid, _load_slabs(xs_ref), jnp.uint32(0)))
        wgu = wgu_ref[0]
        gu = jnp.dot(lo.astype(BF16), wgu[:HALF_D], preferred_element_type=F32) \
            + jnp.dot(hi.astype(BF16), wgu[HALF_D:], preferred_element_type=F32)
        g = gu[:, :D_EXPERT]
        u = gu[:, D_EXPERT:]
        act = (g * jax.nn.sigmoid(g) * u).astype(BF16)
        _store_slabs(ys_ref, _pack_rows(jnp.dot(act, wd_ref[0], preferred_element_type=F32)))

    @pl.when(valid <= 0)
    def _():
        ys_ref[...] = jnp.zeros(ys_ref.shape, U32)


def _experts(xs, blk_e, blk_valid, wgu, wd):
    n_rows = xs.shape[0] // ROW_SLABS
    bm = MOE_BM
    nblk = n_rows // bm
    return pl.pallas_call(
        _expert_kernel,
        out_shape=jax.ShapeDtypeStruct((n_rows * ROW_SLABS, LANES), U32),
        grid_spec=pltpu.PrefetchScalarGridSpec(
            num_scalar_prefetch=2,
            grid=(nblk,),
            in_specs=[pl.BlockSpec((bm * ROW_SLABS, LANES), lambda i, be, nu: (i, 0)),
                      pl.BlockSpec((1, D_MODEL, 2 * D_EXPERT), lambda i, be, nu: (be[i], 0, 0)),
                      pl.BlockSpec((1, D_EXPERT, D_MODEL), lambda i, be, nu: (be[i], 0, 0))],
            out_specs=pl.BlockSpec((bm * ROW_SLABS, LANES), lambda i, be, nu: (i, 0))),
        compiler_params=_cparams(1),
        name="moe_experts",
    )(blk_e, blk_valid, xs, wgu, wd)


def _combine_kernel(g_ref, base_ref, rw_ref, mod_ref, fn_ref, *rest):
    y_ref = rest[-1]
    tm = base_ref.shape[0]
    rw = rw_ref[...]
    acc_lo = jnp.zeros((tm, HALF_D), F32)
    acc_hi = jnp.zeros((tm, HALF_D), F32)
    for k in range(TOP_K):
        lo, hi = _unpack_rows(_load_slabs(g_ref.at[0, k]))
        w = rw[:, k:k + 1]
        acc_lo = acc_lo + w * lo
        acc_hi = acc_hi + w * hi
    routed = jnp.concatenate([acc_lo, acc_hi], axis=1)
    x2 = base_ref[...] + mod_ref[0][5:6] * routed
    y_ref[...] = _rms(x2, fn_ref[...])


def _combine(g, base, rw_t, mod3, final_norm, seq, part, y_prev):
    n = base.shape[0]
    tm = TM_COMBINE
    assert seq % tm == 0
    g4 = g.reshape(-1, TOP_K, tm * ROW_SLABS, LANES)
    tiles = g4.shape[0]
    off = part * tiles
    row = lambda i: (i + off, 0)
    in_specs = [pl.BlockSpec((1, TOP_K, tm * ROW_SLABS, LANES), lambda i: (i, 0, 0, 0)),
                pl.BlockSpec((tm, D_MODEL), row),
                pl.BlockSpec((tm, SUBLANES), row),
                pl.BlockSpec((1, 6, D_MODEL), lambda i: ((i + off) * tm // seq, 0, 0)),
                pl.BlockSpec((1, D_MODEL), lambda i: (0, 0))]
    args = [g4, base, rw_t, mod3, final_norm.reshape(1, D_MODEL)]
    aliases = {}
    if y_prev is not None:
        in_specs.append(pl.BlockSpec(memory_space=pl.ANY))
        args.append(y_prev)
        aliases = {len(args) - 1: 0}
    return pl.pallas_call(
        _combine_kernel,
        out_shape=jax.ShapeDtypeStruct((n, D_MODEL), F32),
        grid=(tiles,),
        in_specs=in_specs,
        out_specs=pl.BlockSpec((tm, D_MODEL), row),
        input_output_aliases=aliases,
        compiler_params=_cparams(1),
        name="moe_combine",
    )(*args)


def _block_layout(tile_counts, n):
    bm = MOE_BM
    c = tile_counts[:, :, 0]
    ntiles = c.shape[0]
    counts = jnp.sum(c, axis=0)
    padded = (counts + bm - 1) // bm * bm
    earlier_e = np.tri(N_EXPERTS, k=-1, dtype=bool)
    pstart = jnp.sum(jnp.where(earlier_e, padded[None, :], 0), axis=1)
    pend = pstart + padded
    earlier_t = np.tri(ntiles, k=-1, dtype=bool)
    tile_base = pstart[None, :] + jnp.sum(jnp.where(earlier_t[:, :, None], c[None], 0), axis=1)
    nblk = -(-n * TOP_K // bm) + N_EXPERTS
    blk = jnp.arange(nblk, dtype=I32)
    blk_e = jnp.minimum(jnp.sum((pend[None, :] <= blk[:, None] * bm).astype(I32), axis=1),
                        N_EXPERTS - 1)
    onehot_e = blk_e[:, None] == jnp.arange(N_EXPERTS, dtype=I32)[None, :]
    end_e = jnp.sum(jnp.where(onehot_e, (pstart + counts)[None, :], 0), axis=1)
    blk_valid = jnp.clip(end_e - blk * bm, 0, bm).astype(I32)
    return tile_base.astype(I32)[:, :, None], blk_e.astype(I32), blk_valid, nblk * bm


def _trunk(x, mod3, p):
    nbatch, seq, _ = x.shape
    n = nbatch * seq
    x2 = x.reshape(n, D_MODEL)
    proj = _inproj(x2, mod3, p["norm1"], p["w_in"], seq)
    qa, kva = proj[:2]
    sga, sgb = proj[8:]
    (ya,) = _band_attention(qa, kva, p["bias_a"], p["sink_a"], nbatch=nbatch, seq=seq, dil=1,
                            halo=A_HALF_WINDOW, step_rows=ATTN_ROWS_A, shared_kv=True, want_lse=False)
    outs, lses = [], []
    for gi, (w, d) in enumerate(B_GROUPS):
        q, kv = proj[2 + 2 * gi:4 + 2 * gi]
        o, lse = _band_attention(q, kv, p["bias_b"][gi], None, nbatch=nbatch, seq=seq, dil=d,
                                 halo=w // (2 * d), step_rows=ATTN_ROWS_B, shared_kv=False,
                                 want_lse=True)
        outs.append(o)
        lses.append(lse)
    base, hp, ridx, rw, rank, tile_counts = _merge(
        x2, ya, outs, lses, sga, sgb, mod3, p["norm2"], p["w_pa"], p["w_pb"], p["w_o"], p["w_rt"],
        p["rbias"], p["ws_gate"], p["ws_up"], p["ws_down"], seq)
    assert TM_MERGE == TM_COMBINE
    tile_base, blk_e, blk_valid, n_rows = _block_layout(tile_counts, n)
    dest_rows = _plan(ridx, rank, tile_base, TM_MERGE).reshape(-1, SC_CHUNK)
    slabs = lambda a: a.reshape(-1, ROW_SLABS, LANES)
    flat = lambda a: a.reshape(-1, LANES)
    xs = _sc_dispatch(slabs(hp), dest_rows, n_rows)
    ys = _experts(flat(xs), blk_e, blk_valid, p["w_gu"], p["w_down"])
    rows_per_part = dest_rows.shape[0] // COLLECT_PARTS
    rw_t = rw.T
    y = None
    for part in range(COLLECT_PARTS):
        g = _sc_collect(slabs(ys), dest_rows[part * rows_per_part:(part + 1) * rows_per_part])
        y = _combine(g, base, rw_t, mod3, p["final_norm"], seq, part, y)
    return y.reshape(nbatch, seq, D_MODEL)


def kernel(x_prompt, x_sample, c_prompt, c_sample, rel_bias, w_ada, b_ada, norm1, w_in, sink, w_pa, w_pb, w_o, norm2, w_router, router_bias, w_gate, w_up, w_down, ws_gate, ws_up, ws_down, final_norm):
    assert w_ada.shape[0] == 1
    nbp = x_prompt.shape[0]
    mod = _ada(jnp.concatenate([c_prompt, c_sample], axis=0), w_ada[0], b_ada[0])
    mod3 = mod.reshape(-1, 6, D_MODEL)

    def pair_bias(heads, halo, dist_scale):
        kb = ATTN_BLOCK + 2 * halo
        tab = _bias_table(heads, ATTN_BLOCK, kb, halo, halo, dist_scale)
        return tab.reshape(-1, 2, ATTN_BLOCK, kb).transpose(0, 2, 1, 3).reshape(-1, ATTN_BLOCK, 2 * kb)

    bias_a = pair_bias(rel_bias[:, :A_Q_HEADS], A_HALF_WINDOW, 1)
    bias_b = []
    for gi, (w, d) in enumerate(B_GROUPS):
        h0 = A_Q_HEADS + gi * B_HEADS_PER_GROUP
        bias_b.append(pair_bias(rel_bias[:, h0:h0 + B_HEADS_PER_GROUP], w // (2 * d), d))
    p = {
        "norm1": norm1[0], "norm2": norm2[0], "final_norm": final_norm,
        "w_in": w_in[0].astype(BF16),
        "bias_a": bias_a,
        "sink_a": jnp.repeat(sink[0].astype(F32), HEAD_DIM).reshape(A_Q_HEADS // 2, 1, LANES),
        "bias_b": bias_b,
        "w_pa": w_pa[0].astype(BF16), "w_pb": w_pb[0].astype(BF16), "w_o": w_o[0].astype(BF16),
        "w_rt": w_router[0].T, "rbias": router_bias[0].reshape(N_EXPERTS, 1),
        "ws_gate": ws_gate[0].astype(BF16), "ws_up": ws_up[0].astype(BF16),
        "ws_down": ws_down[0].astype(BF16),
        "w_gu": jnp.concatenate([w_gate[0], w_up[0]], axis=-1).astype(BF16),
        "w_down": w_down[0].astype(BF16),
    }
    y_prompt = _trunk(x_prompt, mod3[:nbp], p)
    y_sample = _trunk(x_sample, mod3[nbp:], p)
    return (y_prompt, y_sample)
```

```python
import functools
import math

import jax
import jax.numpy as jnp
import numpy as np
from jax import lax
from jax.experimental import pallas as pl
from jax.experimental.pallas import tpu as pltpu
from jax.experimental.pallas import tpu_sc as plsc

F32 = jnp.float32
BF16 = jnp.bfloat16
U32 = jnp.uint32
I32 = jnp.int32

D_MODEL = 1024
HEAD_DIM = 64
A_Q_HEADS = 8
A_KV_HEADS = 2
A_HALF_WINDOW = 128
B_GROUPS = ((128, 1), (512, 4), (2048, 16))
B_HEADS_PER_GROUP = 4
N_BUCKETS = 32
MAX_DISTANCE = 1024
N_EXPERTS = 64
TOP_K = 6
N_EXPERT_GROUPS = 8
TOPK_GROUPS = 4
D_EXPERT = 256
ROUTED_SCALE = 2.5
RMS_EPS = 1e-6
NEG_INF = -1e30
REMOVED = -3e38

A_Q_W = A_Q_HEADS * HEAD_DIM
A_KV_W = A_KV_HEADS * HEAD_DIM
B_W = len(B_GROUPS) * B_HEADS_PER_GROUP * HEAD_DIM
B_OUT_W = B_HEADS_PER_GROUP * HEAD_DIM
D_IN = A_Q_W + 2 * A_KV_W + 3 * B_W + 2 * D_MODEL
HALF_D = D_MODEL // 2
LANES = 128
SUBLANES = 8
ROW_SLABS = HALF_D // LANES
SC_CORES = 2
SC_SUBCORES = 16
SC_CHUNK = 128

TM_INPROJ = 1024
TM_MERGE = 512
ATTN_BLOCK = 128
ATTN_ROWS_A = 512
ATTN_ROWS_B = 2048
MOE_BM = 1024
TM_COMBINE = 512
COLLECT_PARTS = 2
VMEM_LIMIT = 60 * 1024 * 1024


def _cparams(n_axes):
    return pltpu.CompilerParams(
        dimension_semantics=("arbitrary",) * n_axes, vmem_limit_bytes=VMEM_LIMIT)


def _ada_kernel(c_ref, w_ref, b_ref, o_ref):
    c = c_ref[...]
    s = c * jax.nn.sigmoid(c)
    o_ref[...] = jnp.dot(s, w_ref[...], preferred_element_type=F32,
                         precision=lax.Precision.HIGHEST) + b_ref[...]


def _ada(c_all, w_ada, b_ada):
    nb = c_all.shape[0]
    return pl.pallas_call(
        _ada_kernel,
        out_shape=jax.ShapeDtypeStruct((nb, 6 * D_MODEL), F32),
        grid=(6,),
        in_specs=[pl.BlockSpec((nb, D_MODEL), lambda j: (0, 0)),
                  pl.BlockSpec((D_MODEL, D_MODEL), lambda j: (0, j)),
                  pl.BlockSpec((1, D_MODEL), lambda j: (0, j))],
        out_specs=pl.BlockSpec((nb, D_MODEL), lambda j: (0, j)),
        compiler_params=_cparams(1),
        name="ada",
    )(c_all, w_ada, b_ada.reshape(1, 6 * D_MODEL))


def _rms(x, g):
    return x * lax.rsqrt(jnp.mean(x * x, axis=-1, keepdims=True) + RMS_EPS) * g


def _inproj_kernel(x_ref, mod_ref, n1_ref, w_ref, qa, kva, *rest):
    q_b = rest[0:6:2]
    kv_b = rest[1:6:2]
    sga, sgb, scr = rest[6:]
    mod = mod_ref[0]
    h = _rms(x_ref[...], n1_ref[...]) * (1.0 + mod[1:2]) + mod[0:1]
    hb = h.astype(BF16)
    tm = hb.shape[0]
    scale = HEAD_DIM ** -0.5

    def proj(off, width):
        return jnp.dot(hb, w_ref[:, off:off + width], preferred_element_type=F32)

    qa[...] = (proj(0, A_Q_W) * scale).astype(BF16)
    for j, off in enumerate((A_Q_W, A_Q_W + A_KV_W)):
        r = proj(off, A_KV_W)
        kva[:, 2 * j * A_KV_W:(2 * j + 1) * A_KV_W] = r.astype(BF16)
        kva[:, (2 * j + 1) * A_KV_W:(2 * j + 2) * A_KV_W] = pltpu.roll(r, HEAD_DIM, 1).astype(BF16)
    off = A_Q_W + 2 * A_KV_W
    for t in range(3):
        for gi, (_, d) in enumerate(B_GROUPS):
            r = proj(off + t * B_W + gi * B_OUT_W, B_OUT_W)
            if t == 0:
                r = r * scale
            ref, width, base = (q_b[gi], B_OUT_W, 0) if t == 0 else (kv_b[gi], 2 * B_OUT_W,
                                                                     (t - 1) * B_OUT_W)
            if d == 1:
                ref[:, base:base + B_OUT_W] = r.astype(BF16)
            else:
                for c in range(B_OUT_W // LANES):
                    slot = (t * 2 + gi - 1) * (B_OUT_W // LANES) + c
                    scr[slot] = r[:, c * LANES:(c + 1) * LANES]
                    for res in range(d):
                        col = res * width + base + c * LANES
                        ref[:, col:col + LANES] = scr[
                            slot, pl.ds(res, tm // d, stride=d), :].astype(BF16)
    off += 3 * B_W
    sga[...] = jax.nn.sigmoid(proj(off, D_MODEL)).astype(BF16)
    sgb[...] = jax.nn.sigmoid(proj(off + D_MODEL, D_MODEL)).astype(BF16)


def _inproj(x2, mod3, norm1, w_in_bf, seq):
    n = x2.shape[0]
    tm = TM_INPROJ
    assert seq % tm == 0 and n % tm == 0
    row = lambda i: (i, 0)
    shapes = [(n, A_Q_W, tm), (n, 4 * A_KV_W, tm)]
    for _, d in B_GROUPS:
        shapes += [(n // d, d * B_OUT_W, tm // d), (n // d, d * 2 * B_OUT_W, tm // d)]
    shapes += [(n, D_MODEL, tm)] * 2
    return pl.pallas_call(
        _inproj_kernel,
        out_shape=[jax.ShapeDtypeStruct((r, c), BF16) for r, c, _ in shapes],
        grid=(n // tm,),
        in_specs=[pl.BlockSpec((tm, D_MODEL), row),
                  pl.BlockSpec((1, 6, D_MODEL), lambda i: (i * tm // seq, 0, 0)),
                  pl.BlockSpec((1, D_MODEL), lambda i: (0, 0)),
                  pl.BlockSpec((D_MODEL, D_IN), lambda i: (0, 0), pipeline_mode=pl.Buffered(1))],
        out_specs=[pl.BlockSpec((b, c), row) for _, c, b in shapes],
        scratch_shapes=[pltpu.VMEM((6 * B_OUT_W // LANES, tm, LANES), F32)],
        compiler_params=_cparams(1),
        name="inproj",
    )(x2, mod3, norm1.reshape(1, D_MODEL), w_in_bf)


def _rel_bucket_np(rel):
    half = N_BUCKETS // 2
    max_exact = half // 2
    n = np.abs(rel)
    large = max_exact + (np.log(np.maximum(n, 1) / max_exact) / math.log(MAX_DISTANCE / max_exact)
                         * (half - max_exact)).astype(np.int32)
    large = np.minimum(large, half - 1)
    return ((rel > 0).astype(np.int32) * half + np.where(n < max_exact, n, large)).astype(np.int32)


def _bias_table(rel_bias_heads, n_q, n_k, key_off, band, dist_scale):
    p = n_q + n_k
    rel = np.arange(p) - (n_q - 1) - key_off
    bucket = _rel_bucket_np(rel * dist_scale)
    t = jnp.where((np.abs(rel) <= band)[None], rel_bias_heads.astype(F32)[bucket].T, NEG_INF)
    big = jnp.tile(t, (1, n_q + 1))
    tab = big[:, n_q - 1:n_q - 1 + n_q * (p - 1)].reshape(-1, n_q, p - 1)
    return tab[:, :, :n_k]


def _pair_rhs(k_top, k_bot, v_top, v_bot):
    kb = k_top.shape[0]
    low = jnp.where(lax.broadcasted_iota(I32, (kb, LANES), 1) < HEAD_DIM, 1.0, 0.0).astype(BF16)
    high = jnp.where(lax.broadcasted_iota(I32, (kb, LANES), 1) < HEAD_DIM, 0.0, 1.0).astype(BF16)
    rhs_k = jnp.concatenate([k_top * low, k_bot * high], axis=0)
    rhs_v = jnp.concatenate([jnp.concatenate([v_top * low, low], axis=1),
                             jnp.concatenate([v_bot * high, high], axis=1)], axis=0)
    return rhs_k, rhs_v


def _pair_attention(q_pair, rhs_k, rhs_v, bias_pair, pen, sink_pair, want_lse):
    kb = rhs_k.shape[0] // 2
    s = lax.dot_general(q_pair, rhs_k, (((1,), (1,)), ((), ())), preferred_element_type=F32)
    s = s + bias_pair
    if pen is not None:
        s = s + pen
    s0, s1 = s[:, :kb], s[:, kb:]
    m0 = jnp.max(s0, axis=-1, keepdims=True)
    m1 = jnp.max(s1, axis=-1, keepdims=True)
    if sink_pair is not None:
        m0 = jnp.maximum(m0, sink_pair[:, 0:1])
        m1 = jnp.maximum(m1, sink_pair[:, HEAD_DIM:HEAD_DIM + 1])
    p = jnp.concatenate([jnp.exp(s0 - m0), jnp.exp(s1 - m1)], axis=1).astype(BF16)
    od = jnp.dot(p, rhs_v, preferred_element_type=F32)
    o, den = od[:, :LANES], od[:, LANES:]
    low = lax.broadcasted_iota(I32, o.shape, 1) < HEAD_DIM
    m_full = jnp.where(low, m0, m1)
    if sink_pair is not None:
        den = den + jnp.exp(sink_pair - m_full)
    return o / den, (m_full + jnp.log(den)) if want_lse else None


def _attn_kernel(*refs, halo, nblk, n_pairs, n_seqs, shared_kv, has_sink, want_lse, n_steps):
    it = iter(refs)
    q_ref = next(it)
    kv_prev, kv_cur, kv_next = next(it), next(it), next(it)
    bias_ref = next(it)
    sink_ref = next(it) if has_sink else None
    o_ref = next(it)
    lse_ref = next(it) if want_lse else None
    step = pl.program_id(2)
    blk = ATTN_BLOCK
    kb = blk + 2 * halo
    qw = n_pairs * LANES
    kv = jnp.concatenate([kv_prev[...], kv_cur[...], kv_next[...]], axis=0)
    col = lax.broadcasted_iota(I32, (1, kb), 1)
    for b in range(nblk):
        lo = jnp.where(step == 0, halo, 0) if b == 0 else 0
        hi = jnp.where(step == n_steps - 1, kb - halo, kb) if b == nblk - 1 else kb
        pen = None
        if b == 0 or b == nblk - 1:
            pen1 = jnp.where((col >= lo) & (col < hi), 0.0, NEG_INF)
            pen = jnp.concatenate([pen1, pen1], axis=1)
        rows = slice(b * blk, (b + 1) * blk)
        krows = slice(b * blk, b * blk + kb)
        lane_col = lambda j: kv[krows, j * LANES:(j + 1) * LANES]
        if shared_kv:
            k, k_sw, v, v_sw = (lane_col(j) for j in range(4))
            rhs = [_pair_rhs(k, k_sw, v, v_sw), _pair_rhs(k_sw, k, v_sw, v)]
        for s in range(n_seqs):
            for c in range(n_pairs):
                lanes = slice(s * qw + c * LANES, s * qw + (c + 1) * LANES)
                if shared_kv:
                    rhs_k, rhs_v = rhs[c // (n_pairs // 2)]
                else:
                    k, v = lane_col(2 * s * n_pairs + c), lane_col((2 * s + 1) * n_pairs + c)
                    rhs_k, rhs_v = _pair_rhs(k, k, v, v)
                o, lse = _pair_attention(q_ref[rows, lanes], rhs_k, rhs_v, bias_ref[c], pen,
                                         sink_ref[c] if has_sink else None, want_lse)
                o_ref[rows, lanes] = o.astype(o_ref.dtype)
                if want_lse:
                    lse_ref[rows, lanes] = lse


def _band_attention(q, kv, bias, sink, *, nbatch, seq, dil, halo, step_rows, shared_kv, want_lse):
    n = nbatch * seq
    sub_len = seq // dil
    qw = q.shape[1] // dil
    kv_width = kv.shape[1] // dil
    rows = min(step_rows, sub_len)
    n_seqs = min(dil, step_rows // rows)
    assert sub_len % rows == 0 and rows % ATTN_BLOCK == 0 and rows % halo == 0 and dil % n_seqs == 0
    nq = sub_len // rows
    per = rows // halo
    total_halos = n // dil // halo
    cur = lambda b, r, i: (b * nq + i, r)
    prev = lambda b, r, i: (jnp.maximum((b * nq + i) * per - 1, 0), r)
    nxt = lambda b, r, i: (jnp.minimum((b * nq + i + 1) * per, total_halos - 1), r)
    const = lambda a: pl.BlockSpec(a.shape, lambda b, r, i: (0,) * a.ndim)
    qw_step, kvw_step = qw * n_seqs, kv_width * n_seqs

    in_specs = [pl.BlockSpec((rows, qw_step), cur), pl.BlockSpec((halo, kvw_step), prev),
                pl.BlockSpec((rows, kvw_step), cur), pl.BlockSpec((halo, kvw_step), nxt),
                const(bias)]
    args = [q, kv, kv, kv, bias]
    if sink is not None:
        in_specs.append(const(sink))
        args.append(sink)
    out_shape = [jax.ShapeDtypeStruct(q.shape, BF16)]
    out_specs = [pl.BlockSpec((rows, qw_step), cur)]
    if want_lse:
        out_shape.append(jax.ShapeDtypeStruct(q.shape, F32))
        out_specs.append(pl.BlockSpec((rows, qw_step), cur))
    return pl.pallas_call(
        functools.partial(_attn_kernel, halo=halo, nblk=rows // ATTN_BLOCK, n_pairs=qw // LANES,
                          n_seqs=n_seqs, shared_kv=shared_kv, has_sink=sink is not None,
                          want_lse=want_lse, n_steps=nq),
        out_shape=out_shape,
        grid=(nbatch, dil // n_seqs, nq),
        in_specs=in_specs,
        out_specs=out_specs,
        compiler_params=_cparams(3),
        name=f"band_attn_d{dil}",
    )(*args)


def _pack_rows(y):
    return pltpu.pack_elementwise([y[:, :HALF_D], y[:, HALF_D:]], packed_dtype=BF16)


def _store_slabs(ref, packed):
    t = packed.shape[0]
    for c in range(ROW_SLABS):
        ref[pl.ds(c, t, stride=ROW_SLABS), :] = packed[:, c * LANES:(c + 1) * LANES]


def _load_slabs(ref):
    t = ref.shape[0] // ROW_SLABS
    return jnp.concatenate([ref[pl.ds(c, t, stride=ROW_SLABS), :] for c in range(ROW_SLABS)], axis=1)


def _unpack_rows(p):
    return tuple(pltpu.unpack_elementwise(p, index=i, packed_dtype=BF16, unpacked_dtype=F32)
                 for i in range(2))


def _route(sel, scores):
    t = sel.shape[-1]
    per = N_EXPERTS // N_EXPERT_GROUPS
    shape3 = (N_EXPERT_GROUPS, per, t)
    sel3 = sel.reshape(shape3)
    sc3 = scores.reshape(shape3)
    iota_g = lax.broadcasted_iota(I32, shape3, 0)
    iota_m = lax.broadcasted_iota(I32, shape3, 1)
    iota_e = iota_g * per + iota_m
    m1 = jnp.max(sel3, axis=1, keepdims=True)
    i1 = jnp.min(jnp.where(sel3 == m1, iota_m, per), axis=1, keepdims=True)
    m2 = jnp.max(jnp.where(iota_m == i1, REMOVED, sel3), axis=1, keepdims=True)
    gscore = m1 + m2
    iota_g1 = lax.broadcasted_iota(I32, gscore.shape, 0)
    gmask = jnp.zeros(gscore.shape, jnp.bool_)
    for _ in range(TOPK_GROUPS):
        mx = jnp.max(gscore, axis=0, keepdims=True)
        ix = jnp.min(jnp.where(gscore == mx, iota_g1, N_EXPERT_GROUPS), axis=0, keepdims=True)
        hit = iota_g1 == ix
        gmask = gmask | hit
        gscore = jnp.where(hit, REMOVED, gscore)
    cur = jnp.where(gmask, sel3, NEG_INF)
    sum_all = lambda a: jnp.sum(jnp.sum(a, axis=1, keepdims=True), axis=0, keepdims=True)
    idxs, wts, hits = [], [], []
    for _ in range(TOP_K):
        mx = jnp.max(jnp.max(cur, axis=1, keepdims=True), axis=0, keepdims=True)
        cand = jnp.where(cur == mx, iota_e, N_EXPERTS)
        ix = jnp.min(jnp.min(cand, axis=1, keepdims=True), axis=0, keepdims=True)
        hit = iota_e == ix
        wts.append(sum_all(jnp.where(hit, sc3, 0.0)).reshape(1, t))
        cur = jnp.where(hit, REMOVED, cur)
        idxs.append(ix.reshape(1, t))
        hits.append(hit)
    wsum = wts[0]
    for w in wts[1:]:
        wsum = wsum + w
    wts = [w / wsum * ROUTED_SCALE for w in wts]
    onehot = hits[0].astype(F32)
    for hit in hits[1:]:
        onehot = onehot + hit.astype(F32)
    tri = (lax.broadcasted_iota(I32, (t, t), 0) <= lax.broadcasted_iota(I32, (t, t), 1))
    cum = jnp.dot(onehot.reshape(N_EXPERTS, t).astype(BF16), tri.astype(F32).astype(BF16),
                  preferred_element_type=F32)
    cum3 = cum.reshape(shape3) - 1.0
    ranks = [sum_all(jnp.where(hit, cum3, 0.0)).reshape(1, t).astype(I32) for hit in hits]
    counts = cum[:, t - 1:t].astype(I32)
    return idxs, wts, ranks, counts


def _merge_kernel(x_ref, ya_ref, o1, o2, o3, l1, l2, l3, sga_ref, sgb_ref, mod_ref, n2_ref,
                  wpa, wpb, wo, wrt, rbias, wsg, wsu, wsd,
                  base_ref, hp_ref, ridx_ref, rw_ref, rank_ref, cnt_ref, scr):
    mod = mod_ref[0]
    tm = x_ref.shape[0]

    def token_major(ref, slot, d):
        if d == 1:
            return ref[...].astype(F32)
        nc = B_OUT_W // LANES
        for c in range(nc):
            for res in range(d):
                col = res * B_OUT_W + c * LANES
                scr[slot * nc + c, pl.ds(res, tm // d, stride=d), :] = ref[
                    :, col:col + LANES].astype(F32)
        return jnp.concatenate([scr[slot * nc + c] for c in range(nc)], axis=1)

    dils = [d for _, d in B_GROUPS]
    os_ = [token_major(r, i, d) for i, (r, d) in enumerate(zip((o1, o2, o3), dils))]
    ls = [token_major(r, 3 + i, d) for i, (r, d) in enumerate(zip((l1, l2, l3), dils))]
    mx = jnp.maximum(jnp.maximum(ls[0], ls[1]), ls[2])
    es = [jnp.exp(l - mx) for l in ls]
    den = es[0] + es[1] + es[2]
    ob = (es[0] / den) * os_[0] + (es[1] / den) * os_[1] + (es[2] / den) * os_[2]
    pa = jnp.dot(ya_ref[...], wpa[...], preferred_element_type=F32)
    pb = jnp.dot(ob.astype(BF16), wpb[...], preferred_element_type=F32)
    merged = sga_ref[...].astype(F32) * pa + sgb_ref[...].astype(F32) * pb
    mix = jnp.dot(merged.astype(BF16), wo[...], preferred_element_type=F32)
    x1 = x_ref[...] + mod[2:3] * mix
    h2 = _rms(x1, n2_ref[...]) * (1.0 + mod[4:5]) + mod[3:4]
    logits = lax.dot_general(wrt[...], h2, (((1,), (1,)), ((), ())), preferred_element_type=F32,
                             precision=lax.Precision.HIGHEST)
    scores = jax.nn.sigmoid(logits)
    idxs, wts, ranks, counts = _route(scores + rbias[...], scores)
    for k in range(TOP_K):
        ridx_ref[k:k + 1, :] = idxs[k]
        rw_ref[k:k + 1, :] = wts[k]
        rank_ref[k:k + 1, :] = ranks[k]
    ridx_ref[TOP_K:, :] = jnp.zeros((SUBLANES - TOP_K, tm), I32)
    rw_ref[TOP_K:, :] = jnp.zeros((SUBLANES - TOP_K, tm), F32)
    rank_ref[TOP_K:, :] = jnp.zeros((SUBLANES - TOP_K, tm), I32)
    cnt_ref[0] = counts
    hb = h2.astype(BF16)
    g = jnp.dot(hb, wsg[...], preferred_element_type=F32)
    u = jnp.dot(hb, wsu[...], preferred_element_type=F32)
    act = (g * jax.nn.sigmoid(g) * u).astype(BF16)
    shared = jnp.dot(act, wsd[...], preferred_element_type=F32)
    base_ref[...] = x1 + mod[5:6] * shared
    _store_slabs(hp_ref, _pack_rows(h2))


def _merge(x2, ya, outs, lses, sga, sgb, mod3, norm2, wpa, wpb, wo, wrt, rbias, wsg, wsu, wsd, seq):
    n = x2.shape[0]
    tm = TM_MERGE
    assert seq % tm == 0
    row = lambda i: (i, 0)
    full = lambda a: pl.BlockSpec(a.shape, lambda i: (0,) * a.ndim)
    weights = [wpa, wpb, wo, wrt, rbias, wsg, wsu, wsd]
    group_specs = [pl.BlockSpec((tm // d, d * B_OUT_W), row) for _, d in B_GROUPS]
    lanes = lambda i: (0, i)
    return pl.pallas_call(
        _merge_kernel,
        out_shape=[jax.ShapeDtypeStruct((n, D_MODEL), F32),
                   jax.ShapeDtypeStruct((n * ROW_SLABS, LANES), U32),
                   jax.ShapeDtypeStruct((SUBLANES, n), I32),
                   jax.ShapeDtypeStruct((SUBLANES, n), F32),
                   jax.ShapeDtypeStruct((SUBLANES, n), I32),
                   jax.ShapeDtypeStruct((n // tm, N_EXPERTS, 1), I32)],
        grid=(n // tm,),
        in_specs=[pl.BlockSpec((tm, D_MODEL), row), pl.BlockSpec((tm, A_Q_W), row)]
                 + group_specs * 2
                 + [pl.BlockSpec((tm, D_MODEL), row)] * 2
                 + [pl.BlockSpec((1, 6, D_MODEL), lambda i: (i * tm // seq, 0, 0)),
                    pl.BlockSpec((1, D_MODEL), lambda i: (0, 0))]
                 + [full(w) for w in weights],
        out_specs=[pl.BlockSpec((tm, D_MODEL), row),
                   pl.BlockSpec((tm * ROW_SLABS, LANES), row),
                   pl.BlockSpec((SUBLANES, tm), lanes), pl.BlockSpec((SUBLANES, tm), lanes),
                   pl.BlockSpec((SUBLANES, tm), lanes),
                   pl.BlockSpec((1, N_EXPERTS, 1), lambda i: (i, 0, 0))],
        scratch_shapes=[pltpu.VMEM((6 * B_OUT_W // LANES, tm, LANES), F32)],
        compiler_params=_cparams(1),
        name="merge_route",
    )(x2, ya, *outs, *lses, sga, sgb, mod3, norm2.reshape(1, D_MODEL), *weights)


def _plan_kernel(ridx_ref, rank_ref, tb_ref, dest_ref):
    tb = tb_ref[0]
    t = ridx_ref.shape[1]
    iota_e = lax.broadcasted_iota(I32, (N_EXPERTS, t), 0)
    for k in range(TOP_K):
        base = jnp.sum(jnp.where(iota_e == ridx_ref[k:k + 1, :], tb, 0), axis=0, keepdims=True)
        dest_ref[0, :, k * t:(k + 1) * t] = base + rank_ref[k:k + 1, :]


def _plan(ridx, rank, tile_base, tm):
    n = ridx.shape[1]
    lanes = lambda i: (0, i)
    return pl.pallas_call(
        _plan_kernel,
        out_shape=jax.ShapeDtypeStruct((n // tm, 1, TOP_K * tm), I32),
        grid=(n // tm,),
        in_specs=[pl.BlockSpec((SUBLANES, tm), lanes), pl.BlockSpec((SUBLANES, tm), lanes),
                  pl.BlockSpec((1, N_EXPERTS, 1), lambda i: (i, 0, 0))],
        out_specs=pl.BlockSpec((1, 1, TOP_K * tm), lambda i: (i, 0, 0)),
        compiler_params=_cparams(1),
        name="moe_plan",
    )(ridx, rank, tile_base)


def _sc_mesh():
    return plsc.VectorSubcoreMesh(core_axis_name="core", subcore_axis_name="subcore",
                                  num_cores=SC_CORES, num_subcores=SC_SUBCORES)


def _sc_worker():
    return lax.axis_index("subcore") * SC_CORES + lax.axis_index("core")


def _sc_dispatch(hp, dest_rows, n_rows):
    n = hp.shape[0]
    halves = TM_MERGE // SC_CHUNK
    workers = SC_CORES * SC_SUBCORES
    tiles_per_w = n // TM_MERGE // workers
    idx_per_w = tiles_per_w * TOP_K * halves
    assert tiles_per_w * workers * TM_MERGE == n

    @functools.partial(
        pl.kernel, mesh=_sc_mesh(),
        out_type=jax.ShapeDtypeStruct((n_rows, ROW_SLABS, LANES), U32),
        scratch_types=[pltpu.VMEM((idx_per_w, SC_CHUNK), I32),
                       pltpu.VMEM((SC_CHUNK, ROW_SLABS, LANES), U32),
                       pltpu.SemaphoreType.DMA],
        name="moe_dispatch_sc")
    def scatter(hp_hbm, idx_hbm, xs_hbm, idx_v, rows_v, sem):
        wid = _sc_worker()
        pltpu.sync_copy(idx_hbm.at[pl.ds(wid * idx_per_w, idx_per_w)], idx_v)

        @pl.loop(0, tiles_per_w * halves)
        def _(j):
            tile = j // halves
            h = j - tile * halves
            tok = (wid * tiles_per_w + tile) * TM_MERGE + h * SC_CHUNK
            pltpu.sync_copy(hp_hbm.at[pl.ds(tok, SC_CHUNK)], rows_v)
            for k in range(TOP_K):
                row = (tile * TOP_K + k) * halves + h
                pltpu.async_copy(rows_v, xs_hbm.at[idx_v.at[row]], sem).wait()

    return scatter(hp, dest_rows)


def _sc_collect(ys, dest_rows):
    n_chunks = dest_rows.shape[0]
    workers = SC_CORES * SC_SUBCORES
    per_w = n_chunks // workers
    assert per_w * workers == n_chunks

    @functools.partial(
        pl.kernel, mesh=_sc_mesh(),
        out_type=jax.ShapeDtypeStruct((n_chunks * SC_CHUNK, ROW_SLABS, LANES), U32),
        scratch_types=[pltpu.VMEM((per_w, SC_CHUNK), I32),
                       pltpu.VMEM((SC_CHUNK, ROW_SLABS, LANES), U32),
                       pltpu.SemaphoreType.DMA],
        name="moe_collect_sc")
    def gather(ys_hbm, idx_hbm, out_hbm, idx_v, rows_v, sem):
        wid = _sc_worker()
        base = wid * per_w
        pltpu.sync_copy(idx_hbm.at[pl.ds(base, per_w)], idx_v)

        @pl.loop(0, per_w)
        def _(j):
            pltpu.async_copy(ys_hbm.at[idx_v.at[j]], rows_v, sem).wait()
            pltpu.sync_copy(rows_v, out_hbm.at[pl.ds((base + j) * SC_CHUNK, SC_CHUNK)])

    return gather(ys, dest_rows)


def _expert_kernel(blk_e_ref, valid_ref, src_ref, xs_ref, wgu_ref, wd_ref, ys_ref):
    del blk_e_ref, src_ref
    valid = valid_ref[pl.program_id(0)]

    @pl.when(valid > 0)
    def _():
        rows = lax.broadcasted_iota(I32, (xs_ref.shape[0] // ROW_SLABS, 1), 0)
        lo, hi = _unpack_rows(jnp.where(rows < valid, _load_slabs(xs_ref), jnp.uint32(0)))
        wgu = wgu_ref[0]
        gu = jnp.dot(lo.astype(BF16), wgu[:HALF_D], preferred_element_type=F32) \
            + jnp.dot(hi.astype(BF16), wgu[HALF_D:], preferred_element_type=F32)
        g = gu[:, :D_EXPERT]
        u = gu[:, D_EXPERT:]
        act = (g * jax.nn.sigmoid(g) * u).astype(BF16)
        _store_slabs(ys_ref, _pack_rows(jnp.dot(act, wd_ref[0], preferred_element_type=F32)))


def _experts(xs, blk_e, blk_valid, blk_src, wgu, wd):
    n_rows = xs.shape[0] // ROW_SLABS
    bm = MOE_BM
    nblk = n_rows // bm
    return pl.pallas_call(
        _expert_kernel,
        out_shape=jax.ShapeDtypeStruct((n_rows * ROW_SLABS, LANES), U32),
        grid_spec=pltpu.PrefetchScalarGridSpec(
            num_scalar_prefetch=3,
            grid=(nblk,),
            in_specs=[pl.BlockSpec((bm * ROW_SLABS, LANES), lambda i, be, bv, bs: (bs[i], 0)),
                      pl.BlockSpec((1, D_MODEL, 2 * D_EXPERT), lambda i, be, bv, bs: (be[i], 0, 0)),
                      pl.BlockSpec((1, D_EXPERT, D_MODEL), lambda i, be, bv, bs: (be[i], 0, 0))],
            out_specs=pl.BlockSpec((bm * ROW_SLABS, LANES), lambda i, be, bv, bs: (bs[i], 0))),
        compiler_params=_cparams(1),
        name="moe_experts",
    )(blk_e, blk_valid, blk_src, xs, wgu, wd)


def _combine_kernel(g_ref, base_ref, rw_ref, mod_ref, fn_ref, *rest):
    y_ref = rest[-1]
    tm = base_ref.shape[0]
    rw = rw_ref[...]
    acc_lo = jnp.zeros((tm, HALF_D), F32)
    acc_hi = jnp.zeros((tm, HALF_D), F32)
    for k in range(TOP_K):
        lo, hi = _unpack_rows(_load_slabs(g_ref.at[0, k]))
        w = rw[:, k:k + 1]
        acc_lo = acc_lo + w * lo
        acc_hi = acc_hi + w * hi
    routed = jnp.concatenate([acc_lo, acc_hi], axis=1)
    x2 = base_ref[...] + mod_ref[0][5:6] * routed
    y_ref[...] = _rms(x2, fn_ref[...])


def _combine(g, base, rw_t, mod3, final_norm, seq, part, y_prev):
    n = base.shape[0]
    tm = TM_COMBINE
    assert seq % tm == 0
    g4 = g.reshape(-1, TOP_K, tm * ROW_SLABS, LANES)
    tiles = g4.shape[0]
    off = part * tiles
    row = lambda i: (i + off, 0)
    in_specs = [pl.BlockSpec((1, TOP_K, tm * ROW_SLABS, LANES), lambda i: (i, 0, 0, 0)),
                pl.BlockSpec((tm, D_MODEL), row),
                pl.BlockSpec((tm, SUBLANES), row),
                pl.BlockSpec((1, 6, D_MODEL), lambda i: ((i + off) * tm // seq, 0, 0)),
                pl.BlockSpec((1, D_MODEL), lambda i: (0, 0))]
    args = [g4, base, rw_t, mod3, final_norm.reshape(1, D_MODEL)]
    aliases = {}
    if y_prev is not None:
        in_specs.append(pl.BlockSpec(memory_space=pl.ANY))
        args.append(y_prev)
        aliases = {len(args) - 1: 0}
    return pl.pallas_call(
        _combine_kernel,
        out_shape=jax.ShapeDtypeStruct((n, D_MODEL), F32),
        grid=(tiles,),
        in_specs=in_specs,
        out_specs=pl.BlockSpec((tm, D_MODEL), row),
        input_output_aliases=aliases,
        compiler_params=_cparams(1),
        name="moe_combine",
    )(*args)


def _block_layout(tile_counts, n):
    bm = MOE_BM
    c = tile_counts[:, :, 0]
    ntiles = c.shape[0]
    counts = jnp.sum(c, axis=0)
    padded = (counts + bm - 1) // bm * bm
    earlier_e = np.tri(N_EXPERTS, k=-1, dtype=bool)
    pstart = jnp.sum(jnp.where(earlier_e, padded[None, :], 0), axis=1)
    pend = pstart + padded
    earlier_t = np.tri(ntiles, k=-1, dtype=bool)
    tile_base = pstart[None, :] + jnp.sum(jnp.where(earlier_t[:, :, None], c[None], 0), axis=1)
    nblk = -(-n * TOP_K // bm) + N_EXPERTS
    blk = jnp.arange(nblk, dtype=I32)
    blk_e = jnp.minimum(jnp.sum((pend[None, :] <= blk[:, None] * bm).astype(I32), axis=1),
                        N_EXPERTS - 1)
    onehot_e = blk_e[:, None] == jnp.arange(N_EXPERTS, dtype=I32)[None, :]
    end_e = jnp.sum(jnp.where(onehot_e, (pstart + counts)[None, :], 0), axis=1)
    blk_valid = jnp.clip(end_e - blk * bm, 0, bm).astype(I32)
    blk_src = jnp.minimum(blk, pend[N_EXPERTS - 1] // bm - 1).astype(I32)
    last_e = jnp.max(jnp.where(counts > 0, jnp.arange(N_EXPERTS, dtype=I32), 0))
    blk_e = jnp.where(blk == blk_src, blk_e, last_e).astype(I32)
    return tile_base.astype(I32)[:, :, None], blk_e, blk_valid, blk_src, nblk * bm


def _trunk(x, mod3, p):
    nbatch, seq, _ = x.shape
    n = nbatch * seq
    x2 = x.reshape(n, D_MODEL)
    proj = _inproj(x2, mod3, p["norm1"], p["w_in"], seq)
    qa, kva = proj[:2]
    sga, sgb = proj[8:]
    (ya,) = _band_attention(qa, kva, p["bias_a"], p["sink_a"], nbatch=nbatch, seq=seq, dil=1,
                            halo=A_HALF_WINDOW, step_rows=ATTN_ROWS_A, shared_kv=True, want_lse=False)
    outs, lses = [], []
    for gi, (w, d) in enumerate(B_GROUPS):
        q, kv = proj[2 + 2 * gi:4 + 2 * gi]
        o, lse = _band_attention(q, kv, p["bias_b"][gi], None, nbatch=nbatch, seq=seq, dil=d,
                                 halo=w // (2 * d), step_rows=ATTN_ROWS_B, shared_kv=False,
                                 want_lse=True)
        outs.append(o)
        lses.append(lse)
    base, hp, ridx, rw, rank, tile_counts = _merge(
        x2, ya, outs, lses, sga, sgb, mod3, p["norm2"], p["w_pa"], p["w_pb"], p["w_o"], p["w_rt"],
        p["rbias"], p["ws_gate"], p["ws_up"], p["ws_down"], seq)
    assert TM_MERGE == TM_COMBINE
    tile_base, blk_e, blk_valid, blk_src, n_rows = _block_layout(tile_counts, n)
    dest_rows = _plan(ridx, rank, tile_base, TM_MERGE).reshape(-1, SC_CHUNK)
    slabs = lambda a: a.reshape(-1, ROW_SLABS, LANES)
    flat = lambda a: a.reshape(-1, LANES)
    xs = _sc_dispatch(slabs(hp), dest_rows, n_rows)
    ys = _experts(flat(xs), blk_e, blk_valid, blk_src, p["w_gu"], p["w_down"])
    rows_per_part = dest_rows.shape[0] // COLLECT_PARTS
    rw_t = rw.T
    y = None
    for part in range(COLLECT_PARTS):
        g = _sc_collect(slabs(ys), dest_rows[part * rows_per_part:(part + 1) * rows_per_part])
        y = _combine(g, base, rw_t, mod3, p["final_norm"], seq, part, y)
    return y.reshape(nbatch, seq, D_MODEL)


def kernel(x_prompt, x_sample, c_prompt, c_sample, rel_bias, w_ada, b_ada, norm1, w_in, sink, w_pa, w_pb, w_o, norm2, w_router, router_bias, w_gate, w_up, w_down, ws_gate, ws_up, ws_down, final_norm):
    assert w_ada.shape[0] == 1
    nbp = x_prompt.shape[0]
    mod = _ada(jnp.concatenate([c_prompt, c_sample], axis=0), w_ada[0], b_ada[0])
    mod3 = mod.reshape(-1, 6, D_MODEL)

    def pair_bias(heads, halo, dist_scale):
        kb = ATTN_BLOCK + 2 * halo
        tab = _bias_table(heads, ATTN_BLOCK, kb, halo, halo, dist_scale)
        return tab.reshape(-1, 2, ATTN_BLOCK, kb).transpose(0, 2, 1, 3).reshape(-1, ATTN_BLOCK, 2 * kb)

    bias_a = pair_bias(rel_bias[:, :A_Q_HEADS], A_HALF_WINDOW, 1)
    bias_b = []
    for gi, (w, d) in enumerate(B_GROUPS):
        h0 = A_Q_HEADS + gi * B_HEADS_PER_GROUP
        bias_b.append(pair_bias(rel_bias[:, h0:h0 + B_HEADS_PER_GROUP], w // (2 * d), d))
    p = {
        "norm1": norm1[0], "norm2": norm2[0], "final_norm": final_norm,
        "w_in": w_in[0].astype(BF16),
        "bias_a": bias_a,
        "sink_a": jnp.repeat(sink[0].astype(F32), HEAD_DIM).reshape(A_Q_HEADS // 2, 1, LANES),
        "bias_b": bias_b,
        "w_pa": w_pa[0].astype(BF16), "w_pb": w_pb[0].astype(BF16), "w_o": w_o[0].astype(BF16),
        "w_rt": w_router[0].T, "rbias": router_bias[0].reshape(N_EXPERTS, 1),
        "ws_gate": ws_gate[0].astype(BF16), "ws_up": ws_up[0].astype(BF16),
        "ws_down": ws_down[0].astype(BF16),
        "w_gu": jnp.concatenate([w_gate[0], w_up[0]], axis=-1).astype(BF16),
        "w_down": w_down[0].astype(BF16),
    }
    y_prompt = _trunk(x_prompt, mod3[:nbp], p)
    y_sample = _trunk(x_sample, mod3[nbp:], p)
    return (y_prompt, y_sample)
```

```python
import functools
import math

import jax
import jax.numpy as jnp
import numpy as np
from jax import lax
from jax.experimental import pallas as pl
from jax.experimental.pallas import tpu as pltpu
from jax.experimental.pallas import tpu_sc as plsc

F32 = jnp.float32
BF16 = jnp.bfloat16
U32 = jnp.uint32
I32 = jnp.int32

D_MODEL = 1024
HEAD_DIM = 64
A_Q_HEADS = 8
A_KV_HEADS = 2
A_HALF_WINDOW = 128
B_GROUPS = ((128, 1), (512, 4), (2048, 16))
B_HEADS_PER_GROUP = 4
N_BUCKETS = 32
MAX_DISTANCE = 1024
N_EXPERTS = 64
TOP_K = 6
N_EXPERT_GROUPS = 8
TOPK_GROUPS = 4
D_EXPERT = 256
ROUTED_SCALE = 2.5
RMS_EPS = 1e-6
NEG_INF = -1e30
REMOVED = -3e38

A_Q_W = A_Q_HEADS * HEAD_DIM
A_KV_W = A_KV_HEADS * HEAD_DIM
B_W = len(B_GROUPS) * B_HEADS_PER_GROUP * HEAD_DIM
B_OUT_W = B_HEADS_PER_GROUP * HEAD_DIM
D_IN = A_Q_W + 2 * A_KV_W + 3 * B_W + 2 * D_MODEL
HALF_D = D_MODEL // 2
LANES = 128
SUBLANES = 8
ROW_SLABS = HALF_D // LANES
SC_CORES = 2
SC_SUBCORES = 16
SC_CHUNK = 128

TM_INPROJ = 1024
TM_MERGE = 512
ATTN_BLOCK = 128
ATTN_ROWS_A = 512
ATTN_ROWS_B = 2048
MOE_BM = 1024
TM_COMBINE = 512
COLLECT_PARTS = 2
VMEM_LIMIT = 60 * 1024 * 1024


def _cparams(n_axes):
    return pltpu.CompilerParams(
        dimension_semantics=("arbitrary",) * n_axes, vmem_limit_bytes=VMEM_LIMIT)


def _ada_kernel(c_ref, w_ref, b_ref, o_ref):
    c = c_ref[...]
    s = c * jax.nn.sigmoid(c)
    o_ref[...] = jnp.dot(s, w_ref[...], preferred_element_type=F32,
                         precision=lax.Precision.HIGHEST) + b_ref[...]


def _ada(c_all, w_ada, b_ada):
    nb = c_all.shape[0]
    return pl.pallas_call(
        _ada_kernel,
        out_shape=jax.ShapeDtypeStruct((nb, 6 * D_MODEL), F32),
        grid=(6,),
        in_specs=[pl.BlockSpec((nb, D_MODEL), lambda j: (0, 0)),
                  pl.BlockSpec((D_MODEL, D_MODEL), lambda j: (0, j)),
                  pl.BlockSpec((1, D_MODEL), lambda j: (0, j))],
        out_specs=pl.BlockSpec((nb, D_MODEL), lambda j: (0, j)),
        compiler_params=_cparams(1),
        name="ada",
    )(c_all, w_ada, b_ada.reshape(1, 6 * D_MODEL))


def _rms(x, g):
    return x * lax.rsqrt(jnp.mean(x * x, axis=-1, keepdims=True) + RMS_EPS) * g


def _inproj_kernel(x_ref, mod_ref, n1_ref, w_ref, qa, kva, *rest):
    q_b = rest[0:6:2]
    kv_b = rest[1:6:2]
    sga, sgb, scr = rest[6:]
    mod = mod_ref[0]
    h = _rms(x_ref[...], n1_ref[...]) * (1.0 + mod[1:2]) + mod[0:1]
    hb = h.astype(BF16)
    tm = hb.shape[0]
    scale = HEAD_DIM ** -0.5

    def proj(off, width):
        return jnp.dot(hb, w_ref[:, off:off + width], preferred_element_type=F32)

    qa[...] = (proj(0, A_Q_W) * scale).astype(BF16)
    for j, off in enumerate((A_Q_W, A_Q_W + A_KV_W)):
        r = proj(off, A_KV_W)
        kva[:, 2 * j * A_KV_W:(2 * j + 1) * A_KV_W] = r.astype(BF16)
        kva[:, (2 * j + 1) * A_KV_W:(2 * j + 2) * A_KV_W] = pltpu.roll(r, HEAD_DIM, 1).astype(BF16)
    off = A_Q_W + 2 * A_KV_W
    for t in range(3):
        for gi, (_, d) in enumerate(B_GROUPS):
            r = proj(off + t * B_W + gi * B_OUT_W, B_OUT_W)
            if t == 0:
                r = r * scale
            ref, width, base = (q_b[gi], B_OUT_W, 0) if t == 0 else (kv_b[gi], 2 * B_OUT_W,
                                                                     (t - 1) * B_OUT_W)
            if d == 1:
                ref[:, base:base + B_OUT_W] = r.astype(BF16)
            else:
                for c in range(B_OUT_W // LANES):
                    slot = (t * 2 + gi - 1) * (B_OUT_W // LANES) + c
                    scr[slot] = r[:, c * LANES:(c + 1) * LANES]
                    for res in range(d):
                        col = res * width + base + c * LANES
                        ref[:, col:col + LANES] = scr[
                            slot, pl.ds(res, tm // d, stride=d), :].astype(BF16)
    off += 3 * B_W
    sga[...] = jax.nn.sigmoid(proj(off, D_MODEL)).astype(BF16)
    sgb[...] = jax.nn.sigmoid(proj(off + D_MODEL, D_MODEL)).astype(BF16)


def _inproj(x2, mod3, norm1, w_in_bf, seq):
    n = x2.shape[0]
    tm = TM_INPROJ
    assert seq % tm == 0 and n % tm == 0
    row = lambda i: (i, 0)
    shapes = [(n, A_Q_W, tm), (n, 4 * A_KV_W, tm)]
    for _, d in B_GROUPS:
        shapes += [(n // d, d * B_OUT_W, tm // d), (n // d, d * 2 * B_OUT_W, tm // d)]
    shapes += [(n, D_MODEL, tm)] * 2
    return pl.pallas_call(
        _inproj_kernel,
        out_shape=[jax.ShapeDtypeStruct((r, c), BF16) for r, c, _ in shapes],
        grid=(n // tm,),
        in_specs=[pl.BlockSpec((tm, D_MODEL), row),
                  pl.BlockSpec((1, 6, D_MODEL), lambda i: (i * tm // seq, 0, 0)),
                  pl.BlockSpec((1, D_MODEL), lambda i: (0, 0)),
                  pl.BlockSpec((D_MODEL, D_IN), lambda i: (0, 0), pipeline_mode=pl.Buffered(1))],
        out_specs=[pl.BlockSpec((b, c), row) for _, c, b in shapes],
        scratch_shapes=[pltpu.VMEM((6 * B_OUT_W // LANES, tm, LANES), F32)],
        compiler_params=_cparams(1),
        name="inproj",
    )(x2, mod3, norm1.reshape(1, D_MODEL), w_in_bf)


def _rel_bucket_np(rel):
    half = N_BUCKETS // 2
    max_exact = half // 2
    n = np.abs(rel)
    large = max_exact + (np.log(np.maximum(n, 1) / max_exact) / math.log(MAX_DISTANCE / max_exact)
                         * (half - max_exact)).astype(np.int32)
    large = np.minimum(large, half - 1)
    return ((rel > 0).astype(np.int32) * half + np.where(n < max_exact, n, large)).astype(np.int32)


def _bias_table(rel_bias_heads, n_q, n_k, key_off, band, dist_scale):
    p = n_q + n_k
    rel = np.arange(p) - (n_q - 1) - key_off
    bucket = _rel_bucket_np(rel * dist_scale)
    t = jnp.where((np.abs(rel) <= band)[None], rel_bias_heads.astype(F32)[bucket].T, NEG_INF)
    big = jnp.tile(t, (1, n_q + 1))
    tab = big[:, n_q - 1:n_q - 1 + n_q * (p - 1)].reshape(-1, n_q, p - 1)
    return tab[:, :, :n_k]


def _pair_rhs(k_top, k_bot, v_top, v_bot):
    kb = k_top.shape[0]
    low = jnp.where(lax.broadcasted_iota(I32, (kb, LANES), 1) < HEAD_DIM, 1.0, 0.0).astype(BF16)
    high = jnp.where(lax.broadcasted_iota(I32, (kb, LANES), 1) < HEAD_DIM, 0.0, 1.0).astype(BF16)
    rhs_k = jnp.concatenate([k_top * low, k_bot * high], axis=0)
    rhs_v = jnp.concatenate([jnp.concatenate([v_top * low, low], axis=1),
                             jnp.concatenate([v_bot * high, high], axis=1)], axis=0)
    return rhs_k, rhs_v


def _pair_attention(q_pair, rhs_k, rhs_v, bias_pair, pen, sink_pair, want_lse):
    kb = rhs_k.shape[0] // 2
    s = lax.dot_general(q_pair, rhs_k, (((1,), (1,)), ((), ())), preferred_element_type=F32)
    s = s + bias_pair
    if pen is not None:
        s = s + pen
    s0, s1 = s[:, :kb], s[:, kb:]
    m0 = jnp.max(s0, axis=-1, keepdims=True)
    m1 = jnp.max(s1, axis=-1, keepdims=True)
    if sink_pair is not None:
        m0 = jnp.maximum(m0, sink_pair[:, 0:1])
        m1 = jnp.maximum(m1, sink_pair[:, HEAD_DIM:HEAD_DIM + 1])
    p = jnp.concatenate([jnp.exp(s0 - m0), jnp.exp(s1 - m1)], axis=1).astype(BF16)
    od = jnp.dot(p, rhs_v, preferred_element_type=F32)
    o, den = od[:, :LANES], od[:, LANES:]
    low = lax.broadcasted_iota(I32, o.shape, 1) < HEAD_DIM
    m_full = jnp.where(low, m0, m1)
    if sink_pair is not None:
        den = den + jnp.exp(sink_pair - m_full)
    return o / den, (m_full + jnp.log(den)) if want_lse else None


def _attn_kernel(*refs, halo, nblk, n_pairs, n_seqs, shared_kv, has_sink, want_lse, n_steps):
    it = iter(refs)
    q_ref = next(it)
    kv_prev, kv_cur, kv_next = next(it), next(it), next(it)
    bias_ref = next(it)
    sink_ref = next(it) if has_sink else None
    o_ref = next(it)
    lse_ref = next(it) if want_lse else None
    step = pl.program_id(2)
    blk = ATTN_BLOCK
    kb = blk + 2 * halo
    qw = n_pairs * LANES
    kv = jnp.concatenate([kv_prev[...], kv_cur[...], kv_next[...]], axis=0)
    col = lax.broadcasted_iota(I32, (1, kb), 1)
    for b in range(nblk):
        lo = jnp.where(step == 0, halo, 0) if b == 0 else 0
        hi = jnp.where(step == n_steps - 1, kb - halo, kb) if b == nblk - 1 else kb
        pen = None
        if b == 0 or b == nblk - 1:
            pen1 = jnp.where((col >= lo) & (col < hi), 0.0, NEG_INF)
            pen = jnp.concatenate([pen1, pen1], axis=1)
        rows = slice(b * blk, (b + 1) * blk)
        krows = slice(b * blk, b * blk + kb)
        lane_col = lambda j: kv[krows, j * LANES:(j + 1) * LANES]
        if shared_kv:
            k, k_sw, v, v_sw = (lane_col(j) for j in range(4))
            rhs = [_pair_rhs(k, k_sw, v, v_sw), _pair_rhs(k_sw, k, v_sw, v)]
        for s in range(n_seqs):
            for c in range(n_pairs):
                lanes = slice(s * qw + c * LANES, s * qw + (c + 1) * LANES)
                if shared_kv:
                    rhs_k, rhs_v = rhs[c // (n_pairs // 2)]
                else:
                    k, v = lane_col(2 * s * n_pairs + c), lane_col((2 * s + 1) * n_pairs + c)
                    rhs_k, rhs_v = _pair_rhs(k, k, v, v)
                o, lse = _pair_attention(q_ref[rows, lanes], rhs_k, rhs_v, bias_ref[c], pen,
                                         sink_ref[c] if has_sink else None, want_lse)
                o_ref[rows, lanes] = o.astype(o_ref.dtype)
                if want_lse:
                    lse_ref[rows, lanes] = lse


def _band_attention(q, kv, bias, sink, *, nbatch, seq, dil, halo, step_rows, shared_kv, want_lse):
    n = nbatch * seq
    sub_len = seq // dil
    qw = q.shape[1] // dil
    kv_width = kv.shape[1] // dil
    rows = min(step_rows, sub_len)
    n_seqs = min(dil, step_rows // rows)
    assert sub_len % rows == 0 and rows % ATTN_BLOCK == 0 and rows % halo == 0 and dil % n_seqs == 0
    nq = sub_len // rows
    per = rows // halo
    total_halos = n // dil // halo
    cur = lambda b, r, i: (b * nq + i, r)
    prev = lambda b, r, i: (jnp.maximum((b * nq + i) * per - 1, 0), r)
    nxt = lambda b, r, i: (jnp.minimum((b * nq + i + 1) * per, total_halos - 1), r)
    const = lambda a: pl.BlockSpec(a.shape, lambda b, r, i: (0,) * a.ndim)
    qw_step, kvw_step = qw * n_seqs, kv_width * n_seqs

    in_specs = [pl.BlockSpec((rows, qw_step), cur), pl.BlockSpec((halo, kvw_step), prev),
                pl.BlockSpec((rows, kvw_step), cur), pl.BlockSpec((halo, kvw_step), nxt),
                const(bias)]
    args = [q, kv, kv, kv, bias]
    if sink is not None:
        in_specs.append(const(sink))
        args.append(sink)
    out_shape = [jax.ShapeDtypeStruct(q.shape, BF16)]
    out_specs = [pl.BlockSpec((rows, qw_step), cur)]
    if want_lse:
        out_shape.append(jax.ShapeDtypeStruct(q.shape, F32))
        out_specs.append(pl.BlockSpec((rows, qw_step), cur))
    return pl.pallas_call(
        functools.partial(_attn_kernel, halo=halo, nblk=rows // ATTN_BLOCK, n_pairs=qw // LANES,
                          n_seqs=n_seqs, shared_kv=shared_kv, has_sink=sink is not None,
                          want_lse=want_lse, n_steps=nq),
        out_shape=out_shape,
        grid=(nbatch, dil // n_seqs, nq),
        in_specs=in_specs,
        out_specs=out_specs,
        compiler_params=_cparams(3),
        name=f"band_attn_d{dil}",
    )(*args)


def _pack_rows(y):
    return pltpu.pack_elementwise([y[:, :HALF_D], y[:, HALF_D:]], packed_dtype=BF16)


def _store_slabs(ref, packed):
    t = packed.shape[0]
    for c in range(ROW_SLABS):
        ref[pl.ds(c, t, stride=ROW_SLABS), :] = packed[:, c * LANES:(c + 1) * LANES]


def _load_slabs(ref):
    t = ref.shape[0] // ROW_SLABS
    return jnp.concatenate([ref[pl.ds(c, t, stride=ROW_SLABS), :] for c in range(ROW_SLABS)], axis=1)


def _unpack_rows(p):
    return tuple(pltpu.unpack_elementwise(p, index=i, packed_dtype=BF16, unpacked_dtype=F32)
                 for i in range(2))


def _route(sel, scores):
    t = sel.shape[-1]
    per = N_EXPERTS // N_EXPERT_GROUPS
    shape3 = (N_EXPERT_GROUPS, per, t)
    sel3 = sel.reshape(shape3)
    sc3 = scores.reshape(shape3)
    iota_g = lax.broadcasted_iota(I32, shape3, 0)
    iota_m = lax.broadcasted_iota(I32, shape3, 1)
    iota_e = iota_g * per + iota_m
    m1 = jnp.max(sel3, axis=1, keepdims=True)
    i1 = jnp.min(jnp.where(sel3 == m1, iota_m, per), axis=1, keepdims=True)
    m2 = jnp.max(jnp.where(iota_m == i1, REMOVED, sel3), axis=1, keepdims=True)
    gscore = m1 + m2
    iota_g1 = lax.broadcasted_iota(I32, gscore.shape, 0)
    gmask = jnp.zeros(gscore.shape, jnp.bool_)
    for _ in range(TOPK_GROUPS):
        mx = jnp.max(gscore, axis=0, keepdims=True)
        ix = jnp.min(jnp.where(gscore == mx, iota_g1, N_EXPERT_GROUPS), axis=0, keepdims=True)
        hit = iota_g1 == ix
        gmask = gmask | hit
        gscore = jnp.where(hit, REMOVED, gscore)
    cur = jnp.where(gmask, sel3, NEG_INF)
    sum_all = lambda a: jnp.sum(jnp.sum(a, axis=1, keepdims=True), axis=0, keepdims=True)
    idxs, wts, hits = [], [], []
    for _ in range(TOP_K):
        mx = jnp.max(jnp.max(cur, axis=1, keepdims=True), axis=0, keepdims=True)
        cand = jnp.where(cur == mx, iota_e, N_EXPERTS)
        ix = jnp.min(jnp.min(cand, axis=1, keepdims=True), axis=0, keepdims=True)
        hit = iota_e == ix
        wts.append(sum_all(jnp.where(hit, sc3, 0.0)).reshape(1, t))
        cur = jnp.where(hit, REMOVED, cur)
        idxs.append(ix.reshape(1, t))
        hits.append(hit)
    wsum = wts[0]
    for w in wts[1:]:
        wsum = wsum + w
    wts = [w / wsum * ROUTED_SCALE for w in wts]
    onehot = hits[0].astype(F32)
    for hit in hits[1:]:
        onehot = onehot + hit.astype(F32)
    tri = (lax.broadcasted_iota(I32, (t, t), 0) <= lax.broadcasted_iota(I32, (t, t), 1))
    cum = jnp.dot(onehot.reshape(N_EXPERTS, t).astype(BF16), tri.astype(F32).astype(BF16),
                  preferred_element_type=F32)
    cum3 = cum.reshape(shape3) - 1.0
    ranks = [sum_all(jnp.where(hit, cum3, 0.0)).reshape(1, t).astype(I32) for hit in hits]
    counts = cum[:, t - 1:t].astype(I32)
    return idxs, wts, ranks, counts


def _merge_kernel(x_ref, ya_ref, o1, o2, o3, l1, l2, l3, sga_ref, sgb_ref, mod_ref, n2_ref,
                  wpa, wpb, wo, wrt, rbias, wsg, wsu, wsd,
                  base_ref, hp_ref, ridx_ref, rw_ref, rank_ref, cnt_ref, scr):
    mod = mod_ref[0]
    tm = x_ref.shape[0]

    def token_major(ref, slot, d):
        if d == 1:
            return ref[...].astype(F32)
        nc = B_OUT_W // LANES
        for c in range(nc):
            for res in range(d):
                col = res * B_OUT_W + c * LANES
                scr[slot * nc + c, pl.ds(res, tm // d, stride=d), :] = ref[
                    :, col:col + LANES].astype(F32)
        return jnp.concatenate([scr[slot * nc + c] for c in range(nc)], axis=1)

    dils = [d for _, d in B_GROUPS]
    os_ = [token_major(r, i, d) for i, (r, d) in enumerate(zip((o1, o2, o3), dils))]
    ls = [token_major(r, 3 + i, d) for i, (r, d) in enumerate(zip((l1, l2, l3), dils))]
    mx = jnp.maximum(jnp.maximum(ls[0], ls[1]), ls[2])
    es = [jnp.exp(l - mx) for l in ls]
    den = es[0] + es[1] + es[2]
    ob = (es[0] / den) * os_[0] + (es[1] / den) * os_[1] + (es[2] / den) * os_[2]
    pa = jnp.dot(ya_ref[...], wpa[...], preferred_element_type=F32)
    pb = jnp.dot(ob.astype(BF16), wpb[...], preferred_element_type=F32)
    merged = sga_ref[...].astype(F32) * pa + sgb_ref[...].astype(F32) * pb
    mix = jnp.dot(merged.astype(BF16), wo[...], preferred_element_type=F32)
    x1 = x_ref[...] + mod[2:3] * mix
    h2 = _rms(x1, n2_ref[...]) * (1.0 + mod[4:5]) + mod[3:4]
    logits = lax.dot_general(wrt[...], h2, (((1,), (1,)), ((), ())), preferred_element_type=F32,
                             precision=lax.Precision.HIGHEST)
    scores = jax.nn.sigmoid(logits)
    idxs, wts, ranks, counts = _route(scores + rbias[...], scores)
    for k in range(TOP_K):
        ridx_ref[k:k + 1, :] = idxs[k]
        rw_ref[k:k + 1, :] = wts[k]
        rank_ref[k:k + 1, :] = ranks[k]
    ridx_ref[TOP_K:, :] = jnp.zeros((SUBLANES - TOP_K, tm), I32)
    rw_ref[TOP_K:, :] = jnp.zeros((SUBLANES - TOP_K, tm), F32)
    rank_ref[TOP_K:, :] = jnp.zeros((SUBLANES - TOP_K, tm), I32)
    cnt_ref[0] = counts
    hb = h2.astype(BF16)
    g = jnp.dot(hb, wsg[...], preferred_element_type=F32)
    u = jnp.dot(hb, wsu[...], preferred_element_type=F32)
    act = (g * jax.nn.sigmoid(g) * u).astype(BF16)
    shared = jnp.dot(act, wsd[...], preferred_element_type=F32)
    base_ref[...] = x1 + mod[5:6] * shared
    _store_slabs(hp_ref, _pack_rows(h2))


def _merge(x2, ya, outs, lses, sga, sgb, mod3, norm2, wpa, wpb, wo, wrt, rbias, wsg, wsu, wsd, seq):
    n = x2.shape[0]
    tm = TM_MERGE
    assert seq % tm == 0
    row = lambda i: (i, 0)
    full = lambda a: pl.BlockSpec(a.shape, lambda i: (0,) * a.ndim)
    weights = [wpa, wpb, wo, wrt, rbias, wsg, wsu, wsd]
    group_specs = [pl.BlockSpec((tm // d, d * B_OUT_W), row) for _, d in B_GROUPS]
    lanes = lambda i: (0, i)
    return pl.pallas_call(
        _merge_kernel,
        out_shape=[jax.ShapeDtypeStruct((n, D_MODEL), F32),
                   jax.ShapeDtypeStruct((n * ROW_SLABS, LANES), U32),
                   jax.ShapeDtypeStruct((SUBLANES, n), I32),
                   jax.ShapeDtypeStruct((SUBLANES, n), F32),
                   jax.ShapeDtypeStruct((SUBLANES, n), I32),
                   jax.ShapeDtypeStruct((n // tm, N_EXPERTS, 1), I32)],
        grid=(n // tm,),
        in_specs=[pl.BlockSpec((tm, D_MODEL), row), pl.BlockSpec((tm, A_Q_W), row)]
                 + group_specs * 2
                 + [pl.BlockSpec((tm, D_MODEL), row)] * 2
                 + [pl.BlockSpec((1, 6, D_MODEL), lambda i: (i * tm // seq, 0, 0)),
                    pl.BlockSpec((1, D_MODEL), lambda i: (0, 0))]
                 + [full(w) for w in weights],
        out_specs=[pl.BlockSpec((tm, D_MODEL), row),
                   pl.BlockSpec((tm * ROW_SLABS, LANES), row),
                   pl.BlockSpec((SUBLANES, tm), lanes), pl.BlockSpec((SUBLANES, tm), lanes),
                   pl.BlockSpec((SUBLANES, tm), lanes),
                   pl.BlockSpec((1, N_EXPERTS, 1), lambda i: (i, 0, 0))],
        scratch_shapes=[pltpu.VMEM((6 * B_OUT_W // LANES, tm, LANES), F32)],
        compiler_params=_cparams(1),
        name="merge_route",
    )(x2, ya, *outs, *lses, sga, sgb, mod3, norm2.reshape(1, D_MODEL), *weights)


def _plan_kernel(ridx_ref, rank_ref, tb_ref, dest_ref):
    tb = tb_ref[0]
    t = ridx_ref.shape[1]
    iota_e = lax.broadcasted_iota(I32, (N_EXPERTS, t), 0)
    for k in range(TOP_K):
        base = jnp.sum(jnp.where(iota_e == ridx_ref[k:k + 1, :], tb, 0), axis=0, keepdims=True)
        dest_ref[0, :, k * t:(k + 1) * t] = base + rank_ref[k:k + 1, :]


def _plan(ridx, rank, tile_base, tm):
    n = ridx.shape[1]
    lanes = lambda i: (0, i)
    return pl.pallas_call(
        _plan_kernel,
        out_shape=jax.ShapeDtypeStruct((n // tm, 1, TOP_K * tm), I32),
        grid=(n // tm,),
        in_specs=[pl.BlockSpec((SUBLANES, tm), lanes), pl.BlockSpec((SUBLANES, tm), lanes),
                  pl.BlockSpec((1, N_EXPERTS, 1), lambda i: (i, 0, 0))],
        out_specs=pl.BlockSpec((1, 1, TOP_K * tm), lambda i: (i, 0, 0)),
        compiler_params=_cparams(1),
        name="moe_plan",
    )(ridx, rank, tile_base)


def _sc_mesh():
    return plsc.VectorSubcoreMesh(core_axis_name="core", subcore_axis_name="subcore",
                                  num_cores=SC_CORES, num_subcores=SC_SUBCORES)


def _sc_worker():
    return lax.axis_index("subcore") * SC_CORES + lax.axis_index("core")


def _sc_dispatch(hp, dest_rows, n_rows):
    n = hp.shape[0]
    halves = TM_MERGE // SC_CHUNK
    workers = SC_CORES * SC_SUBCORES
    tiles_per_w = n // TM_MERGE // workers
    idx_per_w = tiles_per_w * TOP_K * halves
    assert tiles_per_w * workers * TM_MERGE == n

    @functools.partial(
        pl.kernel, mesh=_sc_mesh(),
        out_type=jax.ShapeDtypeStruct((n_rows, ROW_SLABS, LANES), U32),
        scratch_types=[pltpu.VMEM((idx_per_w, SC_CHUNK), I32),
                       pltpu.VMEM((SC_CHUNK, ROW_SLABS, LANES), U32),
                       pltpu.SemaphoreType.DMA],
        name="moe_dispatch_sc")
    def scatter(hp_hbm, idx_hbm, xs_hbm, idx_v, rows_v, sem):
        wid = _sc_worker()
        pltpu.sync_copy(idx_hbm.at[pl.ds(wid * idx_per_w, idx_per_w)], idx_v)

        @pl.loop(0, tiles_per_w * halves)
        def _(j):
            tile = j // halves
            h = j - tile * halves
            tok = (wid * tiles_per_w + tile) * TM_MERGE + h * SC_CHUNK
            pltpu.sync_copy(hp_hbm.at[pl.ds(tok, SC_CHUNK)], rows_v)
            for k in range(TOP_K):
                row = (tile * TOP_K + k) * halves + h
                pltpu.async_copy(rows_v, xs_hbm.at[idx_v.at[row]], sem).wait()

    return scatter(hp, dest_rows)


def _sc_collect(ys, dest_rows):
    n_chunks = dest_rows.shape[0]
    workers = SC_CORES * SC_SUBCORES
    per_w = n_chunks // workers
    assert per_w * workers == n_chunks

    @functools.partial(
        pl.kernel, mesh=_sc_mesh(),
        out_type=jax.ShapeDtypeStruct((n_chunks * SC_CHUNK, ROW_SLABS, LANES), U32),
        scratch_types=[pltpu.VMEM((per_w, SC_CHUNK), I32),
                       pltpu.VMEM((SC_CHUNK, ROW_SLABS, LANES), U32),
                       pltpu.SemaphoreType.DMA],
        name="moe_collect_sc")
    def gather(ys_hbm, idx_hbm, out_hbm, idx_v, rows_v, sem):
        wid = _sc_worker()
        base = wid * per_w
        pltpu.sync_copy(idx_hbm.at[pl.ds(base, per_w)], idx_v)

        @pl.loop(0, per_w)
        def _(j):
            pltpu.async_copy(ys_hbm.at[idx_v.at[j]], rows_v, sem).wait()
            pltpu.sync_copy(rows_v, out_hbm.at[pl.ds((base + j) * SC_CHUNK, SC_CHUNK)])

    return gather(ys, dest_rows)


def _expert_kernel(blk_e_ref, valid_ref, src_ref, xs_ref, wgu_ref, wd_ref, ys_ref):
    del blk_e_ref, src_ref
    valid = valid_ref[pl.program_id(0)]

    @pl.when(valid > 0)
    def _():
        rows = lax.broadcasted_iota(I32, (xs_ref.shape[0] // ROW_SLABS, 1), 0)
        lo, hi = _unpack_rows(jnp.where(rows < valid, _load_slabs(xs_ref), jnp.uint32(0)))
        wgu = wgu_ref[0]
        gu = jnp.dot(lo.astype(BF16), wgu[:HALF_D], preferred_element_type=F32) \
            + jnp.dot(hi.astype(BF16), wgu[HALF_D:], preferred_element_type=F32)
        g = gu[:, :D_EXPERT]
        u = gu[:, D_EXPERT:]
        act = (g * jax.nn.sigmoid(g) * u).astype(BF16)
        _store_slabs(ys_ref, _pack_rows(jnp.dot(act, wd_ref[0], preferred_element_type=F32)))


def _experts(xs, blk_e, blk_valid, blk_src, wgu, wd):
    n_rows = xs.shape[0] // ROW_SLABS
    bm = MOE_BM
    nblk = n_rows // bm
    return pl.pallas_call(
        _expert_kernel,
        out_shape=jax.ShapeDtypeStruct((n_rows * ROW_SLABS, LANES), U32),
        grid_spec=pltpu.PrefetchScalarGridSpec(
            num_scalar_prefetch=3,
            grid=(nblk,),
            in_specs=[pl.BlockSpec((bm * ROW_SLABS, LANES), lambda i, be, bv, bs: (bs[i], 0)),
                      pl.BlockSpec((1, D_MODEL, 2 * D_EXPERT), lambda i, be, bv, bs: (be[i], 0, 0)),
                      pl.BlockSpec((1, D_EXPERT, D_MODEL), lambda i, be, bv, bs: (be[i], 0, 0))],
            out_specs=pl.BlockSpec((bm * ROW_SLABS, LANES), lambda i, be, bv, bs: (bs[i], 0))),
        compiler_params=_cparams(1),
        name="moe_experts",
    )(blk_e, blk_valid, blk_src, xs, wgu, wd)


def _combine_kernel(g_ref, base_ref, rw_ref, mod_ref, fn_ref, *rest):
    y_ref = rest[-1]
    tm = base_ref.shape[0]
    rw = rw_ref[...]
    acc_lo = jnp.zeros((tm, HALF_D), F32)
    acc_hi = jnp.zeros((tm, HALF_D), F32)
    for k in range(TOP_K):
        lo, hi = _unpack_rows(_load_slabs(g_ref.at[0, k]))
        w = rw[:, k:k + 1]
        acc_lo = acc_lo + w * lo
        acc_hi = acc_hi + w * hi
    routed = jnp.concatenate([acc_lo, acc_hi], axis=1)
    x2 = base_ref[...] + mod_ref[0][5:6] * routed
    y_ref[...] = _rms(x2, fn_ref[...])


def _combine(g, base, rw_t, mod3, final_norm, seq, part, y_prev):
    n = base.shape[0]
    tm = TM_COMBINE
    assert seq % tm == 0
    g4 = g.reshape(-1, TOP_K, tm * ROW_SLABS, LANES)
    tiles = g4.shape[0]
    off = part * tiles
    row = lambda i: (i + off, 0)
    in_specs = [pl.BlockSpec((1, TOP_K, tm * ROW_SLABS, LANES), lambda i: (i, 0, 0, 0)),
                pl.BlockSpec((tm, D_MODEL), row),
                pl.BlockSpec((tm, SUBLANES), row),
                pl.BlockSpec((1, 6, D_MODEL), lambda i: ((i + off) * tm // seq, 0, 0)),
                pl.BlockSpec((1, D_MODEL), lambda i: (0, 0))]
    args = [g4, base, rw_t, mod3, final_norm.reshape(1, D_MODEL)]
    aliases = {}
    if y_prev is not None:
        in_specs.append(pl.BlockSpec(memory_space=pl.ANY))
        args.append(y_prev)
        aliases = {len(args) - 1: 0}
    return pl.pallas_call(
        _combine_kernel,
        out_shape=jax.ShapeDtypeStruct((n, D_MODEL), F32),
        grid=(tiles,),
        in_specs=in_specs,
        out_specs=pl.BlockSpec((tm, D_MODEL), row),
        input_output_aliases=aliases,
        compiler_params=_cparams(1),
        name="moe_combine",
    )(*args)


def _block_layout(tile_counts, n):
    bm = MOE_BM
    c = tile_counts[:, :, 0]
    ntiles = c.shape[0]
    counts = jnp.sum(c, axis=0)
    padded = (counts + bm - 1) // bm * bm
    earlier_e = np.tri(N_EXPERTS, k=-1, dtype=bool)
    pstart = jnp.sum(jnp.where(earlier_e, padded[None, :], 0), axis=1)
    pend = pstart + padded
    earlier_t = np.tri(ntiles, k=-1, dtype=bool)
    tile_base = pstart[None, :] + jnp.sum(jnp.where(earlier_t[:, :, None], c[None], 0), axis=1)
    nblk = -(-n * TOP_K // bm) + N_EXPERTS
    blk = jnp.arange(nblk, dtype=I32)
    blk_e = jnp.minimum(jnp.sum((pend[None, :] <= blk[:, None] * bm).astype(I32), axis=1),
                        N_EXPERTS - 1)
    onehot_e = blk_e[:, None] == jnp.arange(N_EXPERTS, dtype=I32)[None, :]
    end_e = jnp.sum(jnp.where(onehot_e, (pstart + counts)[None, :], 0), axis=1)
    blk_valid = jnp.clip(end_e - blk * bm, 0, bm).astype(I32)
    blk_src = jnp.minimum(blk, pend[N_EXPERTS - 1] // bm - 1).astype(I32)
    last_e = jnp.max(jnp.where(counts > 0, jnp.arange(N_EXPERTS, dtype=I32), 0))
    blk_e = jnp.where(blk == blk_src, blk_e, last_e).astype(I32)
    return tile_base.astype(I32)[:, :, None], blk_e, blk_valid, blk_src, nblk * bm


def _front(x, mod3, p):
    nbatch, seq, _ = x.shape
    n = nbatch * seq
    x2 = x.reshape(n, D_MODEL)
    proj = _inproj(x2, mod3, p["norm1"], p["w_in"], seq)
    qa, kva = proj[:2]
    sga, sgb = proj[8:]
    (ya,) = _band_attention(qa, kva, p["bias_a"], p["sink_a"], nbatch=nbatch, seq=seq, dil=1,
                            halo=A_HALF_WINDOW, step_rows=ATTN_ROWS_A, shared_kv=True, want_lse=False)
    outs, lses = [], []
    for gi, (w, d) in enumerate(B_GROUPS):
        q, kv = proj[2 + 2 * gi:4 + 2 * gi]
        o, lse = _band_attention(q, kv, p["bias_b"][gi], None, nbatch=nbatch, seq=seq, dil=d,
                                 halo=w // (2 * d), step_rows=ATTN_ROWS_B, shared_kv=False,
                                 want_lse=True)
        outs.append(o)
        lses.append(lse)
    base, hp, ridx, rw, rank, tile_counts = _merge(
        x2, ya, outs, lses, sga, sgb, mod3, p["norm2"], p["w_pa"], p["w_pb"], p["w_o"], p["w_rt"],
        p["rbias"], p["ws_gate"], p["ws_up"], p["ws_down"], seq)
    assert TM_MERGE == TM_COMBINE
    tile_base, blk_e, blk_valid, blk_src, n_rows = _block_layout(tile_counts, n)
    dest_rows = _plan(ridx, rank, tile_base, TM_MERGE).reshape(-1, SC_CHUNK)
    return dict(base=base, hp=hp, rw=rw, dest_rows=dest_rows, blk_e=blk_e, blk_valid=blk_valid,
                blk_src=blk_src, n_rows=n_rows, shape=x.shape)


def _moe(st, mod3, p):
    nbatch, seq, _ = st["shape"]
    dest_rows = st["dest_rows"]
    slabs = lambda a: a.reshape(-1, ROW_SLABS, LANES)
    flat = lambda a: a.reshape(-1, LANES)
    xs = _sc_dispatch(slabs(st["hp"]), dest_rows, st["n_rows"])
    ys = _experts(flat(xs), st["blk_e"], st["blk_valid"], st["blk_src"], p["w_gu"], p["w_down"])
    rows_per_part = dest_rows.shape[0] // COLLECT_PARTS
    rw_t = st["rw"].T
    y = None
    for part in range(COLLECT_PARTS):
        g = _sc_collect(slabs(ys), dest_rows[part * rows_per_part:(part + 1) * rows_per_part])
        y = _combine(g, st["base"], rw_t, mod3, p["final_norm"], seq, part, y)
    return y.reshape(nbatch, seq, D_MODEL)


def kernel(x_prompt, x_sample, c_prompt, c_sample, rel_bias, w_ada, b_ada, norm1, w_in, sink, w_pa, w_pb, w_o, norm2, w_router, router_bias, w_gate, w_up, w_down, ws_gate, ws_up, ws_down, final_norm):
    assert w_ada.shape[0] == 1
    nbp = x_prompt.shape[0]
    mod = _ada(jnp.concatenate([c_prompt, c_sample], axis=0), w_ada[0], b_ada[0])
    mod3 = mod.reshape(-1, 6, D_MODEL)

    def pair_bias(heads, halo, dist_scale):
        kb = ATTN_BLOCK + 2 * halo
        tab = _bias_table(heads, ATTN_BLOCK, kb, halo, halo, dist_scale)
        return tab.reshape(-1, 2, ATTN_BLOCK, kb).transpose(0, 2, 1, 3).reshape(-1, ATTN_BLOCK, 2 * kb)

    bias_a = pair_bias(rel_bias[:, :A_Q_HEADS], A_HALF_WINDOW, 1)
    bias_b = []
    for gi, (w, d) in enumerate(B_GROUPS):
        h0 = A_Q_HEADS + gi * B_HEADS_PER_GROUP
        bias_b.append(pair_bias(rel_bias[:, h0:h0 + B_HEADS_PER_GROUP], w // (2 * d), d))
    p = {
        "norm1": norm1[0], "norm2": norm2[0], "final_norm": final_norm,
        "w_in": w_in[0].astype(BF16),
        "bias_a": bias_a,
        "sink_a": jnp.repeat(sink[0].astype(F32), HEAD_DIM).reshape(A_Q_HEADS // 2, 1, LANES),
        "bias_b": bias_b,
        "w_pa": w_pa[0].astype(BF16), "w_pb": w_pb[0].astype(BF16), "w_o": w_o[0].astype(BF16),
        "w_rt": w_router[0].T, "rbias": router_bias[0].reshape(N_EXPERTS, 1),
        "ws_gate": ws_gate[0].astype(BF16), "ws_up": ws_up[0].astype(BF16),
        "ws_down": ws_down[0].astype(BF16),
        "w_gu": jnp.concatenate([w_gate[0], w_up[0]], axis=-1).astype(BF16),
        "w_down": w_down[0].astype(BF16),
    }
    st_p = _front(x_prompt, mod3[:nbp], p)
    x_sample, st_p["dest_rows"] = lax.optimization_barrier((x_sample, st_p["dest_rows"]))
    st_s = _front(x_sample, mod3[nbp:], p)
    return (_moe(st_p, mod3[:nbp], p), _moe(st_s, mod3[nbp:], p))
```

```python
import functools
import math

import jax
import jax.numpy as jnp
import numpy as np
from jax import lax
from jax.experimental import pallas as pl
from jax.experimental.pallas import tpu as pltpu
from jax.experimental.pallas import tpu_sc as plsc

F32 = jnp.float32
BF16 = jnp.bfloat16
U32 = jnp.uint32
I32 = jnp.int32

D_MODEL = 1024
HEAD_DIM = 64
A_Q_HEADS = 8
A_KV_HEADS = 2
A_HALF_WINDOW = 128
B_GROUPS = ((128, 1), (512, 4), (2048, 16))
B_HEADS_PER_GROUP = 4
N_BUCKETS = 32
MAX_DISTANCE = 1024
N_EXPERTS = 64
TOP_K = 6
N_EXPERT_GROUPS = 8
TOPK_GROUPS = 4
D_EXPERT = 256
ROUTED_SCALE = 2.5
RMS_EPS = 1e-6
NEG_INF = -1e30
REMOVED = -3e38

A_Q_W = A_Q_HEADS * HEAD_DIM
A_KV_W = A_KV_HEADS * HEAD_DIM
B_W = len(B_GROUPS) * B_HEADS_PER_GROUP * HEAD_DIM
B_OUT_W = B_HEADS_PER_GROUP * HEAD_DIM
D_IN = A_Q_W + 2 * A_KV_W + 3 * B_W + 2 * D_MODEL
HALF_D = D_MODEL // 2
LANES = 128
SUBLANES = 8
ROW_SLABS = HALF_D // LANES
SC_CORES = 2
SC_SUBCORES = 16
SC_CHUNK = 128

TM_INPROJ = 1024
TM_MERGE = 512
ATTN_BLOCK = 128
ATTN_ROWS_A = 512
ATTN_ROWS_B = 2048
MOE_BM = 1024
TM_COMBINE = 512
COLLECT_PARTS = 2
VMEM_LIMIT = 60 * 1024 * 1024


def _cparams(n_axes):
    return pltpu.CompilerParams(
        dimension_semantics=("arbitrary",) * n_axes, vmem_limit_bytes=VMEM_LIMIT)


def _ada_kernel(c_ref, w_ref, b_ref, o_ref):
    c = c_ref[...]
    s = c * jax.nn.sigmoid(c)
    o_ref[...] = jnp.dot(s, w_ref[...], preferred_element_type=F32,
                         precision=lax.Precision.HIGHEST) + b_ref[...]


def _ada(c_all, w_ada, b_ada):
    nb = c_all.shape[0]
    return pl.pallas_call(
        _ada_kernel,
        out_shape=jax.ShapeDtypeStruct((nb, 6 * D_MODEL), F32),
        grid=(6,),
        in_specs=[pl.BlockSpec((nb, D_MODEL), lambda j: (0, 0)),
                  pl.BlockSpec((D_MODEL, D_MODEL), lambda j: (0, j)),
                  pl.BlockSpec((1, D_MODEL), lambda j: (0, j))],
        out_specs=pl.BlockSpec((nb, D_MODEL), lambda j: (0, j)),
        compiler_params=_cparams(1),
        name="ada",
    )(c_all, w_ada, b_ada.reshape(1, 6 * D_MODEL))


def _rms(x, g):
    return x * lax.rsqrt(jnp.mean(x * x, axis=-1, keepdims=True) + RMS_EPS) * g


def _inproj_kernel(x_ref, mod_ref, n1_ref, w_ref, qa, kva, *rest):
    q_b = rest[0:6:2]
    kv_b = rest[1:6:2]
    sga, sgb, scr = rest[6:]
    mod = mod_ref[0]
    h = _rms(x_ref[...], n1_ref[...]) * (1.0 + mod[1:2]) + mod[0:1]
    hb = h.astype(BF16)
    tm = hb.shape[0]
    scale = HEAD_DIM ** -0.5

    def proj(off, width):
        return jnp.dot(hb, w_ref[:, off:off + width], preferred_element_type=F32)

    qa[...] = (proj(0, A_Q_W) * scale).astype(BF16)
    for j, off in enumerate((A_Q_W, A_Q_W + A_KV_W)):
        r = proj(off, A_KV_W)
        kva[:, 2 * j * A_KV_W:(2 * j + 1) * A_KV_W] = r.astype(BF16)
        kva[:, (2 * j + 1) * A_KV_W:(2 * j + 2) * A_KV_W] = pltpu.roll(r, HEAD_DIM, 1).astype(BF16)
    off = A_Q_W + 2 * A_KV_W
    for t in range(3):
        for gi, (_, d) in enumerate(B_GROUPS):
            r = proj(off + t * B_W + gi * B_OUT_W, B_OUT_W)
            if t == 0:
                r = r * scale
            ref, width, base = (q_b[gi], B_OUT_W, 0) if t == 0 else (kv_b[gi], 2 * B_OUT_W,
                                                                     (t - 1) * B_OUT_W)
            if d == 1:
                ref[:, base:base + B_OUT_W] = r.astype(BF16)
            else:
                for c in range(B_OUT_W // LANES):
                    slot = (t * 2 + gi - 1) * (B_OUT_W // LANES) + c
                    scr[slot] = r[:, c * LANES:(c + 1) * LANES]
                    for res in range(d):
                        col = res * width + base + c * LANES
                        ref[:, col:col + LANES] = scr[
                            slot, pl.ds(res, tm // d, stride=d), :].astype(BF16)
    off += 3 * B_W
    sga[...] = jax.nn.sigmoid(proj(off, D_MODEL)).astype(BF16)
    sgb[...] = jax.nn.sigmoid(proj(off + D_MODEL, D_MODEL)).astype(BF16)


def _inproj(x2, mod3, norm1, w_in_bf, seq):
    n = x2.shape[0]
    tm = TM_INPROJ
    assert seq % tm == 0 and n % tm == 0
    row = lambda i: (i, 0)
    shapes = [(n, A_Q_W, tm), (n, 4 * A_KV_W, tm)]
    for _, d in B_GROUPS:
        shapes += [(n // d, d * B_OUT_W, tm // d), (n // d, d * 2 * B_OUT_W, tm // d)]
    shapes += [(n, D_MODEL, tm)] * 2
    return pl.pallas_call(
        _inproj_kernel,
        out_shape=[jax.ShapeDtypeStruct((r, c), BF16) for r, c, _ in shapes],
        grid=(n // tm,),
        in_specs=[pl.BlockSpec((tm, D_MODEL), row),
                  pl.BlockSpec((1, 6, D_MODEL), lambda i: (i * tm // seq, 0, 0)),
                  pl.BlockSpec((1, D_MODEL), lambda i: (0, 0)),
                  pl.BlockSpec((D_MODEL, D_IN), lambda i: (0, 0), pipeline_mode=pl.Buffered(1))],
        out_specs=[pl.BlockSpec((b, c), row) for _, c, b in shapes],
        scratch_shapes=[pltpu.VMEM((6 * B_OUT_W // LANES, tm, LANES), F32)],
        compiler_params=_cparams(1),
        name="inproj",
    )(x2, mod3, norm1.reshape(1, D_MODEL), w_in_bf)


def _rel_bucket_np(rel):
    half = N_BUCKETS // 2
    max_exact = half // 2
    n = np.abs(rel)
    large = max_exact + (np.log(np.maximum(n, 1) / max_exact) / math.log(MAX_DISTANCE / max_exact)
                         * (half - max_exact)).astype(np.int32)
    large = np.minimum(large, half - 1)
    return ((rel > 0).astype(np.int32) * half + np.where(n < max_exact, n, large)).astype(np.int32)


def _bias_table(rel_bias_heads, n_q, n_k, key_off, band, dist_scale):
    p = n_q + n_k
    rel = np.arange(p) - (n_q - 1) - key_off
    bucket = _rel_bucket_np(rel * dist_scale)
    t = jnp.where((np.abs(rel) <= band)[None], rel_bias_heads.astype(F32)[bucket].T, NEG_INF)
    big = jnp.tile(t, (1, n_q + 1))
    tab = big[:, n_q - 1:n_q - 1 + n_q * (p - 1)].reshape(-1, n_q, p - 1)
    return tab[:, :, :n_k]


def _pair_rhs(k_top, k_bot, v_top, v_bot):
    kb = k_top.shape[0]
    low = jnp.where(lax.broadcasted_iota(I32, (kb, LANES), 1) < HEAD_DIM, 1.0, 0.0).astype(BF16)
    high = jnp.where(lax.broadcasted_iota(I32, (kb, LANES), 1) < HEAD_DIM, 0.0, 1.0).astype(BF16)
    rhs_k = jnp.concatenate([k_top * low, k_bot * high], axis=0)
    rhs_v = jnp.concatenate([jnp.concatenate([v_top * low, low], axis=1),
                             jnp.concatenate([v_bot * high, high], axis=1)], axis=0)
    return rhs_k, rhs_v


def _pair_attention(q_pair, rhs_k, rhs_v, bias_pair, pen, sink_pair, want_lse):
    kb = rhs_k.shape[0] // 2
    s = lax.dot_general(q_pair, rhs_k, (((1,), (1,)), ((), ())), preferred_element_type=F32)
    s = s + bias_pair
    if pen is not None:
        s = s + pen
    s0, s1 = s[:, :kb], s[:, kb:]
    m0 = jnp.max(s0, axis=-1, keepdims=True)
    m1 = jnp.max(s1, axis=-1, keepdims=True)
    if sink_pair is not None:
        m0 = jnp.maximum(m0, sink_pair[:, 0:1])
        m1 = jnp.maximum(m1, sink_pair[:, HEAD_DIM:HEAD_DIM + 1])
    p = jnp.concatenate([jnp.exp(s0 - m0), jnp.exp(s1 - m1)], axis=1).astype(BF16)
    od = jnp.dot(p, rhs_v, preferred_element_type=F32)
    o, den = od[:, :LANES], od[:, LANES:]
    low = lax.broadcasted_iota(I32, o.shape, 1) < HEAD_DIM
    m_full = jnp.where(low, m0, m1)
    if sink_pair is not None:
        den = den + jnp.exp(sink_pair - m_full)
    return o / den, (m_full + jnp.log(den)) if want_lse else None


def _attn_kernel(*refs, halo, nblk, n_pairs, n_seqs, shared_kv, has_sink, want_lse, n_steps):
    it = iter(refs)
    q_ref = next(it)
    kv_prev, kv_cur, kv_next = next(it), next(it), next(it)
    bias_ref = next(it)
    sink_ref = next(it) if has_sink else None
    o_ref = next(it)
    lse_ref = next(it) if want_lse else None
    step = pl.program_id(2)
    blk = ATTN_BLOCK
    kb = blk + 2 * halo
    qw = n_pairs * LANES
    kv = jnp.concatenate([kv_prev[...], kv_cur[...], kv_next[...]], axis=0)
    col = lax.broadcasted_iota(I32, (1, kb), 1)
    for b in range(nblk):
        lo = jnp.where(step == 0, halo, 0) if b == 0 else 0
        hi = jnp.where(step == n_steps - 1, kb - halo, kb) if b == nblk - 1 else kb
        pen = None
        if b == 0 or b == nblk - 1:
            pen1 = jnp.where((col >= lo) & (col < hi), 0.0, NEG_INF)
            pen = jnp.concatenate([pen1, pen1], axis=1)
        rows = slice(b * blk, (b + 1) * blk)
        krows = slice(b * blk, b * blk + kb)
        lane_col = lambda j: kv[krows, j * LANES:(j + 1) * LANES]
        if shared_kv:
            k, k_sw, v, v_sw = (lane_col(j) for j in range(4))
            rhs = [_pair_rhs(k, k_sw, v, v_sw), _pair_rhs(k_sw, k, v_sw, v)]
        for s in range(n_seqs):
            for c in range(n_pairs):
                lanes = slice(s * qw + c * LANES, s * qw + (c + 1) * LANES)
                if shared_kv:
                    rhs_k, rhs_v = rhs[c // (n_pairs // 2)]
                else:
                    k, v = lane_col(2 * s * n_pairs + c), lane_col((2 * s + 1) * n_pairs + c)
                    rhs_k, rhs_v = _pair_rhs(k, k, v, v)
                o, lse = _pair_attention(q_ref[rows, lanes], rhs_k, rhs_v, bias_ref[c], pen,
                                         sink_ref[c] if has_sink else None, want_lse)
                o_ref[rows, lanes] = o.astype(o_ref.dtype)
                if want_lse:
                    lse_ref[rows, lanes] = lse


def _band_attention(q, kv, bias, sink, *, nbatch, seq, dil, halo, step_rows, shared_kv, want_lse):
    n = nbatch * seq
    sub_len = seq // dil
    qw = q.shape[1] // dil
    kv_width = kv.shape[1] // dil
    rows = min(step_rows, sub_len)
    n_seqs = min(dil, step_rows // rows)
    assert sub_len % rows == 0 and rows % ATTN_BLOCK == 0 and rows % halo == 0 and dil % n_seqs == 0
    nq = sub_len // rows
    per = rows // halo
    total_halos = n // dil // halo
    cur = lambda b, r, i: (b * nq + i, r)
    prev = lambda b, r, i: (jnp.maximum((b * nq + i) * per - 1, 0), r)
    nxt = lambda b, r, i: (jnp.minimum((b * nq + i + 1) * per, total_halos - 1), r)
    const = lambda a: pl.BlockSpec(a.shape, lambda b, r, i: (0,) * a.ndim)
    qw_step, kvw_step = qw * n_seqs, kv_width * n_seqs

    in_specs = [pl.BlockSpec((rows, qw_step), cur), pl.BlockSpec((halo, kvw_step), prev),
                pl.BlockSpec((rows, kvw_step), cur), pl.BlockSpec((halo, kvw_step), nxt),
                const(bias)]
    args = [q, kv, kv, kv, bias]
    if sink is not None:
        in_specs.append(const(sink))
        args.append(sink)
    out_shape = [jax.ShapeDtypeStruct(q.shape, BF16)]
    out_specs = [pl.BlockSpec((rows, qw_step), cur)]
    if want_lse:
        out_shape.append(jax.ShapeDtypeStruct(q.shape, F32))
        out_specs.append(pl.BlockSpec((rows, qw_step), cur))
    return pl.pallas_call(
        functools.partial(_attn_kernel, halo=halo, nblk=rows // ATTN_BLOCK, n_pairs=qw // LANES,
                          n_seqs=n_seqs, shared_kv=shared_kv, has_sink=sink is not None,
                          want_lse=want_lse, n_steps=nq),
        out_shape=out_shape,
        grid=(nbatch, dil // n_seqs, nq),
        in_specs=in_specs,
        out_specs=out_specs,
        compiler_params=_cparams(3),
        name=f"band_attn_d{dil}",
    )(*args)


def _pack_rows(y):
    return pltpu.pack_elementwise([y[:, :HALF_D], y[:, HALF_D:]], packed_dtype=BF16)


def _store_slabs(ref, packed):
    t = packed.shape[0]
    for c in range(ROW_SLABS):
        ref[pl.ds(c, t, stride=ROW_SLABS), :] = packed[:, c * LANES:(c + 1) * LANES]


def _load_slabs(ref):
    t = ref.shape[0] // ROW_SLABS
    return jnp.concatenate([ref[pl.ds(c, t, stride=ROW_SLABS), :] for c in range(ROW_SLABS)], axis=1)


def _unpack_rows(p):
    return tuple(pltpu.unpack_elementwise(p, index=i, packed_dtype=BF16, unpacked_dtype=F32)
                 for i in range(2))


def _route(sel, scores):
    t = sel.shape[-1]
    per = N_EXPERTS // N_EXPERT_GROUPS
    shape3 = (N_EXPERT_GROUPS, per, t)
    sel3 = sel.reshape(shape3)
    sc3 = scores.reshape(shape3)
    iota_g = lax.broadcasted_iota(I32, shape3, 0)
    iota_m = lax.broadcasted_iota(I32, shape3, 1)
    iota_e = iota_g * per + iota_m
    m1 = jnp.max(sel3, axis=1, keepdims=True)
    i1 = jnp.min(jnp.where(sel3 == m1, iota_m, per), axis=1, keepdims=True)
    m2 = jnp.max(jnp.where(iota_m == i1, REMOVED, sel3), axis=1, keepdims=True)
    gscore = m1 + m2
    iota_g1 = lax.broadcasted_iota(I32, gscore.shape, 0)
    gmask = jnp.zeros(gscore.shape, jnp.bool_)
    for _ in range(TOPK_GROUPS):
        mx = jnp.max(gscore, axis=0, keepdims=True)
        ix = jnp.min(jnp.where(gscore == mx, iota_g1, N_EXPERT_GROUPS), axis=0, keepdims=True)
        hit = iota_g1 == ix
        gmask = gmask | hit
        gscore = jnp.where(hit, REMOVED, gscore)
    cur = jnp.where(gmask, sel3, NEG_INF)
    sum_all = lambda a: jnp.sum(jnp.sum(a, axis=1, keepdims=True), axis=0, keepdims=True)
    idxs, wts, hits = [], [], []
    for _ in range(TOP_K):
        mx = jnp.max(jnp.max(cur, axis=1, keepdims=True), axis=0, keepdims=True)
        cand = jnp.where(cur == mx, iota_e, N_EXPERTS)
        ix = jnp.min(jnp.min(cand, axis=1, keepdims=True), axis=0, keepdims=True)
        hit = iota_e == ix
        wts.append(sum_all(jnp.where(hit, sc3, 0.0)).reshape(1, t))
        cur = jnp.where(hit, REMOVED, cur)
        idxs.append(ix.reshape(1, t))
        hits.append(hit)
    wsum = wts[0]
    for w in wts[1:]:
        wsum = wsum + w
    wts = [w / wsum * ROUTED_SCALE for w in wts]
    onehot = hits[0].astype(F32)
    for hit in hits[1:]:
        onehot = onehot + hit.astype(F32)
    tri = (lax.broadcasted_iota(I32, (t, t), 0) <= lax.broadcasted_iota(I32, (t, t), 1))
    cum = jnp.dot(onehot.reshape(N_EXPERTS, t).astype(BF16), tri.astype(F32).astype(BF16),
                  preferred_element_type=F32)
    cum3 = cum.reshape(shape3) - 1.0
    ranks = [sum_all(jnp.where(hit, cum3, 0.0)).reshape(1, t).astype(I32) for hit in hits]
    counts = cum[:, t - 1:t].astype(I32)
    return idxs, wts, ranks, counts


def _merge_kernel(x_ref, ya_ref, o1, o2, o3, l1, l2, l3, sga_ref, sgb_ref, mod_ref, n2_ref,
                  wpa, wpb, wo, wrt, rbias, wsg, wsu, wsd,
                  base_ref, hp_ref, ridx_ref, rw_ref, rank_ref, cnt_ref, scr):
    mod = mod_ref[0]
    tm = x_ref.shape[0]

    def token_major(ref, slot, d):
        if d == 1:
            return ref[...].astype(F32)
        nc = B_OUT_W // LANES
        for c in range(nc):
            for res in range(d):
                col = res * B_OUT_W + c * LANES
                scr[slot * nc + c, pl.ds(res, tm // d, stride=d), :] = ref[
                    :, col:col + LANES].astype(F32)
        return jnp.concatenate([scr[slot * nc + c] for c in range(nc)], axis=1)

    dils = [d for _, d in B_GROUPS]
    os_ = [token_major(r, i, d) for i, (r, d) in enumerate(zip((o1, o2, o3), dils))]
    ls = [token_major(r, 3 + i, d) for i, (r, d) in enumerate(zip((l1, l2, l3), dils))]
    mx = jnp.maximum(jnp.maximum(ls[0], ls[1]), ls[2])
    es = [jnp.exp(l - mx) for l in ls]
    den = es[0] + es[1] + es[2]
    ob = (es[0] / den) * os_[0] + (es[1] / den) * os_[1] + (es[2] / den) * os_[2]
    pa = jnp.dot(ya_ref[...], wpa[...], preferred_element_type=F32)
    pb = jnp.dot(ob.astype(BF16), wpb[...], preferred_element_type=F32)
    merged = sga_ref[...].astype(F32) * pa + sgb_ref[...].astype(F32) * pb
    mix = jnp.dot(merged.astype(BF16), wo[...], preferred_element_type=F32)
    x1 = x_ref[...] + mod[2:3] * mix
    h2 = _rms(x1, n2_ref[...]) * (1.0 + mod[4:5]) + mod[3:4]
    logits = lax.dot_general(wrt[...], h2, (((1,), (1,)), ((), ())), preferred_element_type=F32,
                             precision=lax.Precision.HIGHEST)
    scores = jax.nn.sigmoid(logits)
    idxs, wts, ranks, counts = _route(scores + rbias[...], scores)
    for k in range(TOP_K):
        ridx_ref[k:k + 1, :] = idxs[k]
        rw_ref[k:k + 1, :] = wts[k]
        rank_ref[k:k + 1, :] = ranks[k]
    ridx_ref[TOP_K:, :] = jnp.zeros((SUBLANES - TOP_K, tm), I32)
    rw_ref[TOP_K:, :] = jnp.zeros((SUBLANES - TOP_K, tm), F32)
    rank_ref[TOP_K:, :] = jnp.zeros((SUBLANES - TOP_K, tm), I32)
    cnt_ref[0] = counts
    hb = h2.astype(BF16)
    g = jnp.dot(hb, wsg[...], preferred_element_type=F32)
    u = jnp.dot(hb, wsu[...], preferred_element_type=F32)
    act = (g * jax.nn.sigmoid(g) * u).astype(BF16)
    shared = jnp.dot(act, wsd[...], preferred_element_type=F32)
    base_ref[...] = x1 + mod[5:6] * shared
    _store_slabs(hp_ref, _pack_rows(h2))


def _merge(x2, ya, outs, lses, sga, sgb, mod3, norm2, wpa, wpb, wo, wrt, rbias, wsg, wsu, wsd, seq):
    n = x2.shape[0]
    tm = TM_MERGE
    assert seq % tm == 0
    row = lambda i: (i, 0)
    full = lambda a: pl.BlockSpec(a.shape, lambda i: (0,) * a.ndim)
    weights = [wpa, wpb, wo, wrt, rbias, wsg, wsu, wsd]
    group_specs = [pl.BlockSpec((tm // d, d * B_OUT_W), row) for _, d in B_GROUPS]
    lanes = lambda i: (0, i)
    return pl.pallas_call(
        _merge_kernel,
        out_shape=[jax.ShapeDtypeStruct((n, D_MODEL), F32),
                   jax.ShapeDtypeStruct((n * ROW_SLABS, LANES), U32),
                   jax.ShapeDtypeStruct((SUBLANES, n), I32),
                   jax.ShapeDtypeStruct((SUBLANES, n), F32),
                   jax.ShapeDtypeStruct((SUBLANES, n), I32),
                   jax.ShapeDtypeStruct((n // tm, N_EXPERTS, 1), I32)],
        grid=(n // tm,),
        in_specs=[pl.BlockSpec((tm, D_MODEL), row), pl.BlockSpec((tm, A_Q_W), row)]
                 + group_specs * 2
                 + [pl.BlockSpec((tm, D_MODEL), row)] * 2
                 + [pl.BlockSpec((1, 6, D_MODEL), lambda i: (i * tm // seq, 0, 0)),
                    pl.BlockSpec((1, D_MODEL), lambda i: (0, 0))]
                 + [full(w) for w in weights],
        out_specs=[pl.BlockSpec((tm, D_MODEL), row),
                   pl.BlockSpec((tm * ROW_SLABS, LANES), row),
                   pl.BlockSpec((SUBLANES, tm), lanes), pl.BlockSpec((SUBLANES, tm), lanes),
                   pl.BlockSpec((SUBLANES, tm), lanes),
                   pl.BlockSpec((1, N_EXPERTS, 1), lambda i: (i, 0, 0))],
        scratch_shapes=[pltpu.VMEM((6 * B_OUT_W // LANES, tm, LANES), F32)],
        compiler_params=_cparams(1),
        name="merge_route",
    )(x2, ya, *outs, *lses, sga, sgb, mod3, norm2.reshape(1, D_MODEL), *weights)


def _plan_kernel(ridx_ref, rank_ref, tb_ref, dest_ref):
    tb = tb_ref[0]
    t = ridx_ref.shape[1]
    iota_e = lax.broadcasted_iota(I32, (N_EXPERTS, t), 0)
    for k in range(TOP_K):
        base = jnp.sum(jnp.where(iota_e == ridx_ref[k:k + 1, :], tb, 0), axis=0, keepdims=True)
        dest_ref[0, :, k * t:(k + 1) * t] = base + rank_ref[k:k + 1, :]


def _plan(ridx, rank, tile_base, tm):
    n = ridx.shape[1]
    lanes = lambda i: (0, i)
    return pl.pallas_call(
        _plan_kernel,
        out_shape=jax.ShapeDtypeStruct((n // tm, 1, TOP_K * tm), I32),
        grid=(n // tm,),
        in_specs=[pl.BlockSpec((SUBLANES, tm), lanes), pl.BlockSpec((SUBLANES, tm), lanes),
                  pl.BlockSpec((1, N_EXPERTS, 1), lambda i: (i, 0, 0))],
        out_specs=pl.BlockSpec((1, 1, TOP_K * tm), lambda i: (i, 0, 0)),
        compiler_params=_cparams(1),
        name="moe_plan",
    )(ridx, rank, tile_base)


def _sc_mesh():
    return plsc.VectorSubcoreMesh(core_axis_name="core", subcore_axis_name="subcore",
                                  num_cores=SC_CORES, num_subcores=SC_SUBCORES)


def _sc_worker():
    return lax.axis_index("subcore") * SC_CORES + lax.axis_index("core")


def _sc_dispatch(hp, dest_rows, n_rows):
    n = hp.shape[0]
    halves = TM_MERGE // SC_CHUNK
    workers = SC_CORES * SC_SUBCORES
    tiles_per_w = n // TM_MERGE // workers
    idx_per_w = tiles_per_w * TOP_K * halves
    assert tiles_per_w * workers * TM_MERGE == n

    @functools.partial(
        pl.kernel, mesh=_sc_mesh(),
        out_type=jax.ShapeDtypeStruct((n_rows, ROW_SLABS, LANES), U32),
        scratch_types=[pltpu.VMEM((idx_per_w, SC_CHUNK), I32),
                       pltpu.VMEM((SC_CHUNK, ROW_SLABS, LANES), U32),
                       pltpu.SemaphoreType.DMA],
        name="moe_dispatch_sc")
    def scatter(hp_hbm, idx_hbm, xs_hbm, idx_v, rows_v, sem):
        wid = _sc_worker()
        pltpu.sync_copy(idx_hbm.at[pl.ds(wid * idx_per_w, idx_per_w)], idx_v)

        @pl.loop(0, tiles_per_w * halves)
        def _(j):
            tile = j // halves
            h = j - tile * halves
            tok = (wid * tiles_per_w + tile) * TM_MERGE + h * SC_CHUNK
            pltpu.sync_copy(hp_hbm.at[pl.ds(tok, SC_CHUNK)], rows_v)
            for k in range(TOP_K):
                row = (tile * TOP_K + k) * halves + h
                pltpu.async_copy(rows_v, xs_hbm.at[idx_v.at[row]], sem).wait()

    return scatter(hp, dest_rows)


def _sc_collect(ys, dest_rows):
    n_chunks = dest_rows.shape[0]
    workers = SC_CORES * SC_SUBCORES
    per_w = n_chunks // workers
    assert per_w * workers == n_chunks

    @functools.partial(
        pl.kernel, mesh=_sc_mesh(),
        out_type=jax.ShapeDtypeStruct((n_chunks * SC_CHUNK, ROW_SLABS, LANES), U32),
        scratch_types=[pltpu.VMEM((per_w, SC_CHUNK), I32),
                       pltpu.VMEM((SC_CHUNK, ROW_SLABS, LANES), U32),
                       pltpu.SemaphoreType.DMA],
        name="moe_collect_sc")
    def gather(ys_hbm, idx_hbm, out_hbm, idx_v, rows_v, sem):
        wid = _sc_worker()
        base = wid * per_w
        pltpu.sync_copy(idx_hbm.at[pl.ds(base, per_w)], idx_v)

        @pl.loop(0, per_w)
        def _(j):
            pltpu.async_copy(ys_hbm.at[idx_v.at[j]], rows_v, sem).wait()
            pltpu.sync_copy(rows_v, out_hbm.at[pl.ds((base + j) * SC_CHUNK, SC_CHUNK)])

    return gather(ys, dest_rows)


def _expert_kernel(blk_e_ref, valid_ref, src_ref, xs_ref, wgu_ref, wd_ref, ys_ref):
    del blk_e_ref, src_ref
    valid = valid_ref[pl.program_id(0)]

    @pl.when(valid > 0)
    def _():
        rows = lax.broadcasted_iota(I32, (xs_ref.shape[0] // ROW_SLABS, 1), 0)
        lo, hi = _unpack_rows(jnp.where(rows < valid, _load_slabs(xs_ref), jnp.uint32(0)))
        wgu = wgu_ref[0]
        gu = jnp.dot(lo.astype(BF16), wgu[:HALF_D], preferred_element_type=F32) \
            + jnp.dot(hi.astype(BF16), wgu[HALF_D:], preferred_element_type=F32)
        g = gu[:, :D_EXPERT]
        u = gu[:, D_EXPERT:]
        act = (g * jax.nn.sigmoid(g) * u).astype(BF16)
        _store_slabs(ys_ref, _pack_rows(jnp.dot(act, wd_ref[0], preferred_element_type=F32)))


def _experts(xs, blk_e, blk_valid, blk_src, wgu, wd):
    n_rows = xs.shape[0] // ROW_SLABS
    bm = MOE_BM
    nblk = n_rows // bm
    return pl.pallas_call(
        _expert_kernel,
        out_shape=jax.ShapeDtypeStruct((n_rows * ROW_SLABS, LANES), U32),
        grid_spec=pltpu.PrefetchScalarGridSpec(
            num_scalar_prefetch=3,
            grid=(nblk,),
            in_specs=[pl.BlockSpec((bm * ROW_SLABS, LANES), lambda i, be, bv, bs: (bs[i], 0)),
                      pl.BlockSpec((1, D_MODEL, 2 * D_EXPERT), lambda i, be, bv, bs: (be[i], 0, 0)),
                      pl.BlockSpec((1, D_EXPERT, D_MODEL), lambda i, be, bv, bs: (be[i], 0, 0))],
            out_specs=pl.BlockSpec((bm * ROW_SLABS, LANES), lambda i, be, bv, bs: (bs[i], 0))),
        compiler_params=_cparams(1),
        name="moe_experts",
    )(blk_e, blk_valid, blk_src, xs, wgu, wd)


def _combine_kernel(g_ref, base_ref, rw_ref, mod_ref, fn_ref, *rest):
    y_ref = rest[-1]
    tm = base_ref.shape[0]
    rw = rw_ref[...]
    acc_lo = jnp.zeros((tm, HALF_D), F32)
    acc_hi = jnp.zeros((tm, HALF_D), F32)
    for k in range(TOP_K):
        lo, hi = _unpack_rows(_load_slabs(g_ref.at[0, k]))
        w = rw[:, k:k + 1]
        acc_lo = acc_lo + w * lo
        acc_hi = acc_hi + w * hi
    routed = jnp.concatenate([acc_lo, acc_hi], axis=1)
    x2 = base_ref[...] + mod_ref[0][5:6] * routed
    y_ref[...] = _rms(x2, fn_ref[...])


def _combine(g, base, rw_t, mod3, final_norm, seq, part, y_prev):
    n = base.shape[0]
    tm = TM_COMBINE
    assert seq % tm == 0
    g4 = g.reshape(-1, TOP_K, tm * ROW_SLABS, LANES)
    tiles = g4.shape[0]
    off = part * tiles
    row = lambda i: (i + off, 0)
    in_specs = [pl.BlockSpec((1, TOP_K, tm * ROW_SLABS, LANES), lambda i: (i, 0, 0, 0)),
                pl.BlockSpec((tm, D_MODEL), row),
                pl.BlockSpec((tm, SUBLANES), row),
                pl.BlockSpec((1, 6, D_MODEL), lambda i: ((i + off) * tm // seq, 0, 0)),
                pl.BlockSpec((1, D_MODEL), lambda i: (0, 0))]
    args = [g4, base, rw_t, mod3, final_norm.reshape(1, D_MODEL)]
    aliases = {}
    if y_prev is not None:
        in_specs.append(pl.BlockSpec(memory_space=pl.ANY))
        args.append(y_prev)
        aliases = {len(args) - 1: 0}
    return pl.pallas_call(
        _combine_kernel,
        out_shape=jax.ShapeDtypeStruct((n, D_MODEL), F32),
        grid=(tiles,),
        in_specs=in_specs,
        out_specs=pl.BlockSpec((tm, D_MODEL), row),
        input_output_aliases=aliases,
        compiler_params=_cparams(1),
        name="moe_combine",
    )(*args)


def _block_layout(tile_counts, n):
    bm = MOE_BM
    c = tile_counts[:, :, 0]
    ntiles = c.shape[0]
    counts = jnp.sum(c, axis=0)
    padded = (counts + bm - 1) // bm * bm
    earlier_e = np.tri(N_EXPERTS, k=-1, dtype=bool)
    pstart = jnp.sum(jnp.where(earlier_e, padded[None, :], 0), axis=1)
    pend = pstart + padded
    earlier_t = np.tri(ntiles, k=-1, dtype=bool)
    tile_base = pstart[None, :] + jnp.sum(jnp.where(earlier_t[:, :, None], c[None], 0), axis=1)
    nblk = -(-n * TOP_K // bm) + N_EXPERTS
    blk = jnp.arange(nblk, dtype=I32)
    blk_e = jnp.minimum(jnp.sum((pend[None, :] <= blk[:, None] * bm).astype(I32), axis=1),
                        N_EXPERTS - 1)
    onehot_e = blk_e[:, None] == jnp.arange(N_EXPERTS, dtype=I32)[None, :]
    end_e = jnp.sum(jnp.where(onehot_e, (pstart + counts)[None, :], 0), axis=1)
    blk_valid = jnp.clip(end_e - blk * bm, 0, bm).astype(I32)
    blk_src = jnp.minimum(blk, pend[N_EXPERTS - 1] // bm - 1).astype(I32)
    last_e = jnp.max(jnp.where(counts > 0, jnp.arange(N_EXPERTS, dtype=I32), 0))
    blk_e = jnp.where(blk == blk_src, blk_e, last_e).astype(I32)
    return tile_base.astype(I32)[:, :, None], blk_e, blk_valid, blk_src, nblk * bm


def _front(x, mod3, p, after=None):
    nbatch, seq, _ = x.shape
    n = nbatch * seq
    x2 = x.reshape(n, D_MODEL)
    proj = _inproj(x2, mod3, p["norm1"], p["w_in"], seq)
    qa, kva = proj[:2]
    sga, sgb = proj[8:]
    (ya,) = _band_attention(qa, kva, p["bias_a"], p["sink_a"], nbatch=nbatch, seq=seq, dil=1,
                            halo=A_HALF_WINDOW, step_rows=ATTN_ROWS_A, shared_kv=True, want_lse=False)
    outs, lses = [], []
    for gi, (w, d) in enumerate(B_GROUPS):
        q, kv = proj[2 + 2 * gi:4 + 2 * gi]
        o, lse = _band_attention(q, kv, p["bias_b"][gi], None, nbatch=nbatch, seq=seq, dil=d,
                                 halo=w // (2 * d), step_rows=ATTN_ROWS_B, shared_kv=False,
                                 want_lse=True)
        outs.append(o)
        lses.append(lse)
    if after is not None:
        ya, after = lax.optimization_barrier((ya, after))
    base, hp, ridx, rw, rank, tile_counts = _merge(
        x2, ya, outs, lses, sga, sgb, mod3, p["norm2"], p["w_pa"], p["w_pb"], p["w_o"], p["w_rt"],
        p["rbias"], p["ws_gate"], p["ws_up"], p["ws_down"], seq)
    assert TM_MERGE == TM_COMBINE
    tile_base, blk_e, blk_valid, blk_src, n_rows = _block_layout(tile_counts, n)
    dest_rows = _plan(ridx, rank, tile_base, TM_MERGE).reshape(-1, SC_CHUNK)
    return dict(base=base, hp=hp, rw=rw, dest_rows=dest_rows, blk_e=blk_e, blk_valid=blk_valid,
                blk_src=blk_src, n_rows=n_rows, shape=x.shape, after=after)


def _slabs(a):
    return a.reshape(-1, ROW_SLABS, LANES)


def _moe(st, xs, mod3, p):
    nbatch, seq, _ = st["shape"]
    dest_rows = st["dest_rows"]
    ys = _experts(xs.reshape(-1, LANES), st["blk_e"], st["blk_valid"], st["blk_src"], p["w_gu"], p["w_down"])
    rows_per_part = dest_rows.shape[0] // COLLECT_PARTS
    rw_t = st["rw"].T
    y = None
    for part in range(COLLECT_PARTS):
        g = _sc_collect(_slabs(ys), dest_rows[part * rows_per_part:(part + 1) * rows_per_part])
        y = _combine(g, st["base"], rw_t, mod3, p["final_norm"], seq, part, y)
    return y.reshape(nbatch, seq, D_MODEL)


def kernel(x_prompt, x_sample, c_prompt, c_sample, rel_bias, w_ada, b_ada, norm1, w_in, sink, w_pa, w_pb, w_o, norm2, w_router, router_bias, w_gate, w_up, w_down, ws_gate, ws_up, ws_down, final_norm):
    assert w_ada.shape[0] == 1
    nbp = x_prompt.shape[0]
    mod = _ada(jnp.concatenate([c_prompt, c_sample], axis=0), w_ada[0], b_ada[0])
    mod3 = mod.reshape(-1, 6, D_MODEL)

    def pair_bias(heads, halo, dist_scale):
        kb = ATTN_BLOCK + 2 * halo
        tab = _bias_table(heads, ATTN_BLOCK, kb, halo, halo, dist_scale)
        return tab.reshape(-1, 2, ATTN_BLOCK, kb).transpose(0, 2, 1, 3).reshape(-1, ATTN_BLOCK, 2 * kb)

    bias_a = pair_bias(rel_bias[:, :A_Q_HEADS], A_HALF_WINDOW, 1)
    bias_b = []
    for gi, (w, d) in enumerate(B_GROUPS):
        h0 = A_Q_HEADS + gi * B_HEADS_PER_GROUP
        bias_b.append(pair_bias(rel_bias[:, h0:h0 + B_HEADS_PER_GROUP], w // (2 * d), d))
    p = {
        "norm1": norm1[0], "norm2": norm2[0], "final_norm": final_norm,
        "w_in": w_in[0].astype(BF16),
        "bias_a": bias_a,
        "sink_a": jnp.repeat(sink[0].astype(F32), HEAD_DIM).reshape(A_Q_HEADS // 2, 1, LANES),
        "bias_b": bias_b,
        "w_pa": w_pa[0].astype(BF16), "w_pb": w_pb[0].astype(BF16), "w_o": w_o[0].astype(BF16),
        "w_rt": w_router[0].T, "rbias": router_bias[0].reshape(N_EXPERTS, 1),
        "ws_gate": ws_gate[0].astype(BF16), "ws_up": ws_up[0].astype(BF16),
        "ws_down": ws_down[0].astype(BF16),
        "w_gu": jnp.concatenate([w_gate[0], w_up[0]], axis=-1).astype(BF16),
        "w_down": w_down[0].astype(BF16),
    }
    dispatch = lambda st: _sc_dispatch(_slabs(st["hp"]), st["dest_rows"], st["n_rows"])
    st_p = _front(x_prompt, mod3[:nbp], p)
    st_s = _front(x_sample, mod3[nbp:], p, after=dispatch(st_p))
    return (_moe(st_p, st_s["after"], mod3[:nbp], p), _moe(st_s, dispatch(st_s), mod3[nbp:], p))
```

```python
import functools
import math

import jax
import jax.numpy as jnp
import numpy as np
from jax import lax
from jax.experimental import pallas as pl
from jax.experimental.pallas import tpu as pltpu
from jax.experimental.pallas import tpu_sc as plsc

F32 = jnp.float32
BF16 = jnp.bfloat16
U32 = jnp.uint32
I32 = jnp.int32

D_MODEL = 1024
HEAD_DIM = 64
A_Q_HEADS = 8
A_KV_HEADS = 2
A_HALF_WINDOW = 128
B_GROUPS = ((128, 1), (512, 4), (2048, 16))
B_HEADS_PER_GROUP = 4
N_BUCKETS = 32
MAX_DISTANCE = 1024
N_EXPERTS = 64
TOP_K = 6
N_EXPERT_GROUPS = 8
TOPK_GROUPS = 4
D_EXPERT = 256
ROUTED_SCALE = 2.5
RMS_EPS = 1e-6
NEG_INF = -1e30
REMOVED = -3e38

A_Q_W = A_Q_HEADS * HEAD_DIM
A_KV_W = A_KV_HEADS * HEAD_DIM
B_W = len(B_GROUPS) * B_HEADS_PER_GROUP * HEAD_DIM
B_OUT_W = B_HEADS_PER_GROUP * HEAD_DIM
D_IN = A_Q_W + 2 * A_KV_W + 3 * B_W + 2 * D_MODEL
HALF_D = D_MODEL // 2
LANES = 128
SUBLANES = 8
ROW_SLABS = HALF_D // LANES
SC_CORES = 2
SC_SUBCORES = 16
SC_CHUNK = 128

TM_INPROJ = 1024
TM_MERGE = 512
ATTN_BLOCK = 128
ATTN_ROWS_A = 512
ATTN_ROWS_B = 2048
MOE_BM = 1024
TM_COMBINE = 512
COLLECT_PARTS = 2
VMEM_LIMIT = 60 * 1024 * 1024


def _cparams(n_axes):
    return pltpu.CompilerParams(
        dimension_semantics=("arbitrary",) * n_axes, vmem_limit_bytes=VMEM_LIMIT)


def _ada_kernel(c_ref, w_ref, b_ref, o_ref):
    c = c_ref[...]
    s = c * jax.nn.sigmoid(c)
    o_ref[...] = jnp.dot(s, w_ref[...], preferred_element_type=F32,
                         precision=lax.Precision.HIGHEST) + b_ref[...]


def _ada(c_all, w_ada, b_ada):
    nb = c_all.shape[0]
    return pl.pallas_call(
        _ada_kernel,
        out_shape=jax.ShapeDtypeStruct((nb, 6 * D_MODEL), F32),
        grid=(6,),
        in_specs=[pl.BlockSpec((nb, D_MODEL), lambda j: (0, 0)),
                  pl.BlockSpec((D_MODEL, D_MODEL), lambda j: (0, j)),
                  pl.BlockSpec((1, D_MODEL), lambda j: (0, j))],
        out_specs=pl.BlockSpec((nb, D_MODEL), lambda j: (0, j)),
        compiler_params=_cparams(1),
        name="ada",
    )(c_all, w_ada, b_ada.reshape(1, 6 * D_MODEL))


def _rms(x, g):
    return x * lax.rsqrt(jnp.mean(x * x, axis=-1, keepdims=True) + RMS_EPS) * g


def _inproj_kernel(x_ref, mod_ref, n1_ref, w_ref, qa, kva, *rest):
    q_b = rest[0:6:2]
    kv_b = rest[1:6:2]
    sga, sgb, scr = rest[6:]
    mod = mod_ref[0]
    h = _rms(x_ref[...], n1_ref[...]) * (1.0 + mod[1:2]) + mod[0:1]
    hb = h.astype(BF16)
    tm = hb.shape[0]
    scale = HEAD_DIM ** -0.5

    def proj(off, width):
        return jnp.dot(hb, w_ref[:, off:off + width], preferred_element_type=F32)

    qa[...] = (proj(0, A_Q_W) * scale).astype(BF16)
    for j, off in enumerate((A_Q_W, A_Q_W + A_KV_W)):
        r = proj(off, A_KV_W)
        kva[:, 2 * j * A_KV_W:(2 * j + 1) * A_KV_W] = r.astype(BF16)
        kva[:, (2 * j + 1) * A_KV_W:(2 * j + 2) * A_KV_W] = pltpu.roll(r, HEAD_DIM, 1).astype(BF16)
    off = A_Q_W + 2 * A_KV_W
    for t in range(3):
        for gi, (_, d) in enumerate(B_GROUPS):
            r = proj(off + t * B_W + gi * B_OUT_W, B_OUT_W)
            if t == 0:
                r = r * scale
            ref, width, base = (q_b[gi], B_OUT_W, 0) if t == 0 else (kv_b[gi], 2 * B_OUT_W,
                                                                     (t - 1) * B_OUT_W)
            if d == 1:
                ref[:, base:base + B_OUT_W] = r.astype(BF16)
            else:
                for c in range(B_OUT_W // LANES):
                    slot = (t * 2 + gi - 1) * (B_OUT_W // LANES) + c
                    scr[slot] = r[:, c * LANES:(c + 1) * LANES]
                    for res in range(d):
                        col = res * width + base + c * LANES
                        ref[:, col:col + LANES] = scr[
                            slot, pl.ds(res, tm // d, stride=d), :].astype(BF16)
    off += 3 * B_W
    sga[...] = jax.nn.sigmoid(proj(off, D_MODEL)).astype(BF16)
    sgb[...] = jax.nn.sigmoid(proj(off + D_MODEL, D_MODEL)).astype(BF16)


def _inproj(x2, mod3, norm1, w_in_bf, seq):
    n = x2.shape[0]
    tm = TM_INPROJ
    assert seq % tm == 0 and n % tm == 0
    row = lambda i: (i, 0)
    shapes = [(n, A_Q_W, tm), (n, 4 * A_KV_W, tm)]
    for _, d in B_GROUPS:
        shapes += [(n // d, d * B_OUT_W, tm // d), (n // d, d * 2 * B_OUT_W, tm // d)]
    shapes += [(n, D_MODEL, tm)] * 2
    return pl.pallas_call(
        _inproj_kernel,
        out_shape=[jax.ShapeDtypeStruct((r, c), BF16) for r, c, _ in shapes],
        grid=(n // tm,),
        in_specs=[pl.BlockSpec((tm, D_MODEL), row),
                  pl.BlockSpec((1, 6, D_MODEL), lambda i: (i * tm // seq, 0, 0)),
                  pl.BlockSpec((1, D_MODEL), lambda i: (0, 0)),
                  pl.BlockSpec((D_MODEL, D_IN), lambda i: (0, 0), pipeline_mode=pl.Buffered(1))],
        out_specs=[pl.BlockSpec((b, c), row) for _, c, b in shapes],
        scratch_shapes=[pltpu.VMEM((6 * B_OUT_W // LANES, tm, LANES), F32)],
        compiler_params=_cparams(1),
        name="inproj",
    )(x2, mod3, norm1.reshape(1, D_MODEL), w_in_bf)


def _rel_bucket_np(rel):
    half = N_BUCKETS // 2
    max_exact = half // 2
    n = np.abs(rel)
    large = max_exact + (np.log(np.maximum(n, 1) / max_exact) / math.log(MAX_DISTANCE / max_exact)
                         * (half - max_exact)).astype(np.int32)
    large = np.minimum(large, half - 1)
    return ((rel > 0).astype(np.int32) * half + np.where(n < max_exact, n, large)).astype(np.int32)


def _bias_table(rel_bias_heads, n_q, n_k, key_off, band, dist_scale):
    p = n_q + n_k
    rel = np.arange(p) - (n_q - 1) - key_off
    bucket = _rel_bucket_np(rel * dist_scale)
    t = jnp.where((np.abs(rel) <= band)[None], rel_bias_heads.astype(F32)[bucket].T, NEG_INF)
    big = jnp.tile(t, (1, n_q + 1))
    tab = big[:, n_q - 1:n_q - 1 + n_q * (p - 1)].reshape(-1, n_q, p - 1)
    return tab[:, :, :n_k]


def _pair_rhs(k_top, k_bot, v_top, v_bot):
    kb = k_top.shape[0]
    low = jnp.where(lax.broadcasted_iota(I32, (kb, LANES), 1) < HEAD_DIM, 1.0, 0.0).astype(BF16)
    high = jnp.where(lax.broadcasted_iota(I32, (kb, LANES), 1) < HEAD_DIM, 0.0, 1.0).astype(BF16)
    rhs_k = jnp.concatenate([k_top * low, k_bot * high], axis=0)
    rhs_v = jnp.concatenate([jnp.concatenate([v_top * low, low], axis=1),
                             jnp.concatenate([v_bot * high, high], axis=1)], axis=0)
    return rhs_k, rhs_v


def _pair_attention(q_pair, rhs_k, rhs_v, bias_pair, pen, sink_pair, want_lse):
    kb = rhs_k.shape[0] // 2
    s = lax.dot_general(q_pair, rhs_k, (((1,), (1,)), ((), ())), preferred_element_type=F32)
    s = s + bias_pair
    if pen is not None:
        s = s + pen
    s0, s1 = s[:, :kb], s[:, kb:]
    m0 = jnp.max(s0, axis=-1, keepdims=True)
    m1 = jnp.max(s1, axis=-1, keepdims=True)
    if sink_pair is not None:
        m0 = jnp.maximum(m0, sink_pair[:, 0:1])
        m1 = jnp.maximum(m1, sink_pair[:, HEAD_DIM:HEAD_DIM + 1])
    p = jnp.concatenate([jnp.exp(s0 - m0), jnp.exp(s1 - m1)], axis=1).astype(BF16)
    od = jnp.dot(p, rhs_v, preferred_element_type=F32)
    o, den = od[:, :LANES], od[:, LANES:]
    low = lax.broadcasted_iota(I32, o.shape, 1) < HEAD_DIM
    m_full = jnp.where(low, m0, m1)
    if sink_pair is not None:
        den = den + jnp.exp(sink_pair - m_full)
    return o / den, (m_full + jnp.log(den)) if want_lse else None


def _attn_kernel(*refs, halo, nblk, n_pairs, n_seqs, shared_kv, has_sink, want_lse, n_steps):
    it = iter(refs)
    q_ref = next(it)
    kv_prev, kv_cur, kv_next = next(it), next(it), next(it)
    bias_ref = next(it)
    sink_ref = next(it) if has_sink else None
    o_ref = next(it)
    lse_ref = next(it) if want_lse else None
    step = pl.program_id(2)
    blk = ATTN_BLOCK
    kb = blk + 2 * halo
    qw = n_pairs * LANES
    kv = jnp.concatenate([kv_prev[...], kv_cur[...], kv_next[...]], axis=0)
    col = lax.broadcasted_iota(I32, (1, kb), 1)
    for b in range(nblk):
        lo = jnp.where(step == 0, halo, 0) if b == 0 else 0
        hi = jnp.where(step == n_steps - 1, kb - halo, kb) if b == nblk - 1 else kb
        pen = None
        if b == 0 or b == nblk - 1:
            pen1 = jnp.where((col >= lo) & (col < hi), 0.0, NEG_INF)
            pen = jnp.concatenate([pen1, pen1], axis=1)
        rows = slice(b * blk, (b + 1) * blk)
        krows = slice(b * blk, b * blk + kb)
        lane_col = lambda j: kv[krows, j * LANES:(j + 1) * LANES]
        if shared_kv:
            k, k_sw, v, v_sw = (lane_col(j) for j in range(4))
            rhs = [_pair_rhs(k, k_sw, v, v_sw), _pair_rhs(k_sw, k, v_sw, v)]
        for s in range(n_seqs):
            for c in range(n_pairs):
                lanes = slice(s * qw + c * LANES, s * qw + (c + 1) * LANES)
                if shared_kv:
                    rhs_k, rhs_v = rhs[c // (n_pairs // 2)]
                else:
                    k, v = lane_col(2 * s * n_pairs + c), lane_col((2 * s + 1) * n_pairs + c)
                    rhs_k, rhs_v = _pair_rhs(k, k, v, v)
                o, lse = _pair_attention(q_ref[rows, lanes], rhs_k, rhs_v, bias_ref[c], pen,
                                         sink_ref[c] if has_sink else None, want_lse)
                o_ref[rows, lanes] = o.astype(o_ref.dtype)
                if want_lse:
                    lse_ref[rows, lanes] = lse


def _band_attention(q, kv, bias, sink, *, nbatch, seq, dil, halo, step_rows, shared_kv, want_lse):
    n = nbatch * seq
    sub_len = seq // dil
    qw = q.shape[1] // dil
    kv_width = kv.shape[1] // dil
    rows = min(step_rows, sub_len)
    n_seqs = min(dil, step_rows // rows)
    assert sub_len % rows == 0 and rows % ATTN_BLOCK == 0 and rows % halo == 0 and dil % n_seqs == 0
    nq = sub_len // rows
    per = rows // halo
    total_halos = n // dil // halo
    cur = lambda b, r, i: (b * nq + i, r)
    prev = lambda b, r, i: (jnp.maximum((b * nq + i) * per - 1, 0), r)
    nxt = lambda b, r, i: (jnp.minimum((b * nq + i + 1) * per, total_halos - 1), r)
    const = lambda a: pl.BlockSpec(a.shape, lambda b, r, i: (0,) * a.ndim)
    qw_step, kvw_step = qw * n_seqs, kv_width * n_seqs

    in_specs = [pl.BlockSpec((rows, qw_step), cur), pl.BlockSpec((halo, kvw_step), prev),
                pl.BlockSpec((rows, kvw_step), cur), pl.BlockSpec((halo, kvw_step), nxt),
                const(bias)]
    args = [q, kv, kv, kv, bias]
    if sink is not None:
        in_specs.append(const(sink))
        args.append(sink)
    out_shape = [jax.ShapeDtypeStruct(q.shape, BF16)]
    out_specs = [pl.BlockSpec((rows, qw_step), cur)]
    if want_lse:
        out_shape.append(jax.ShapeDtypeStruct(q.shape, F32))
        out_specs.append(pl.BlockSpec((rows, qw_step), cur))
    return pl.pallas_call(
        functools.partial(_attn_kernel, halo=halo, nblk=rows // ATTN_BLOCK, n_pairs=qw // LANES,
                          n_seqs=n_seqs, shared_kv=shared_kv, has_sink=sink is not None,
                          want_lse=want_lse, n_steps=nq),
        out_shape=out_shape,
        grid=(nbatch, dil // n_seqs, nq),
        in_specs=in_specs,
        out_specs=out_specs,
        compiler_params=_cparams(3),
        name=f"band_attn_d{dil}",
    )(*args)


def _pack_rows(y):
    return pltpu.pack_elementwise([y[:, :HALF_D], y[:, HALF_D:]], packed_dtype=BF16)


def _store_slabs(ref, packed):
    t = packed.shape[0]
    for c in range(ROW_SLABS):
        ref[pl.ds(c, t, stride=ROW_SLABS), :] = packed[:, c * LANES:(c + 1) * LANES]


def _load_slabs(ref):
    t = ref.shape[0] // ROW_SLABS
    return jnp.concatenate([ref[pl.ds(c, t, stride=ROW_SLABS), :] for c in range(ROW_SLABS)], axis=1)


def _unpack_rows(p):
    return tuple(pltpu.unpack_elementwise(p, index=i, packed_dtype=BF16, unpacked_dtype=F32)
                 for i in range(2))


def _route(sel, scores):
    t = sel.shape[-1]
    per = N_EXPERTS // N_EXPERT_GROUPS
    shape3 = (N_EXPERT_GROUPS, per, t)
    sel3 = sel.reshape(shape3)
    sc3 = scores.reshape(shape3)
    iota_g = lax.broadcasted_iota(I32, shape3, 0)
    iota_m = lax.broadcasted_iota(I32, shape3, 1)
    iota_e = iota_g * per + iota_m
    m1 = jnp.max(sel3, axis=1, keepdims=True)
    i1 = jnp.min(jnp.where(sel3 == m1, iota_m, per), axis=1, keepdims=True)
    m2 = jnp.max(jnp.where(iota_m == i1, REMOVED, sel3), axis=1, keepdims=True)
    gscore = m1 + m2
    iota_g1 = lax.broadcasted_iota(I32, gscore.shape, 0)
    gmask = jnp.zeros(gscore.shape, jnp.bool_)
    for _ in range(TOPK_GROUPS):
        mx = jnp.max(gscore, axis=0, keepdims=True)
        ix = jnp.min(jnp.where(gscore == mx, iota_g1, N_EXPERT_GROUPS), axis=0, keepdims=True)
        hit = iota_g1 == ix
        gmask = gmask | hit
        gscore = jnp.where(hit, REMOVED, gscore)
    cur = jnp.where(gmask, sel3, NEG_INF)
    sum_all = lambda a: jnp.sum(jnp.sum(a, axis=1, keepdims=True), axis=0, keepdims=True)
    idxs, wts, hits = [], [], []
    for _ in range(TOP_K):
        mx = jnp.max(jnp.max(cur, axis=1, keepdims=True), axis=0, keepdims=True)
        cand = jnp.where(cur == mx, iota_e, N_EXPERTS)
        ix = jnp.min(jnp.min(cand, axis=1, keepdims=True), axis=0, keepdims=True)
        hit = iota_e == ix
        wts.append(sum_all(jnp.where(hit, sc3, 0.0)).reshape(1, t))
        cur = jnp.where(hit, REMOVED, cur)
        idxs.append(ix.reshape(1, t))
        hits.append(hit)
    wsum = wts[0]
    for w in wts[1:]:
        wsum = wsum + w
    wts = [w / wsum * ROUTED_SCALE for w in wts]
    onehot = hits[0].astype(F32)
    for hit in hits[1:]:
        onehot = onehot + hit.astype(F32)
    tri = (lax.broadcasted_iota(I32, (t, t), 0) <= lax.broadcasted_iota(I32, (t, t), 1))
    cum = jnp.dot(onehot.reshape(N_EXPERTS, t).astype(BF16), tri.astype(F32).astype(BF16),
                  preferred_element_type=F32)
    cum3 = cum.reshape(shape3) - 1.0
    ranks = [sum_all(jnp.where(hit, cum3, 0.0)).reshape(1, t).astype(I32) for hit in hits]
    counts = cum[:, t - 1:t].astype(I32)
    return idxs, wts, ranks, counts


def _merge_kernel(x_ref, ya_ref, o1, o2, o3, l1, l2, l3, sga_ref, sgb_ref, mod_ref, n2_ref,
                  wpa, wpb, wo, wrt, rbias, wsg, wsu, wsd,
                  base_ref, hp_ref, ridx_ref, rw_ref, rank_ref, cnt_ref, scr):
    mod = mod_ref[0]
    tm = x_ref.shape[0]

    def token_major(ref, slot, d):
        if d == 1:
            return ref[...].astype(F32)
        nc = B_OUT_W // LANES
        for c in range(nc):
            for res in range(d):
                col = res * B_OUT_W + c * LANES
                scr[slot * nc + c, pl.ds(res, tm // d, stride=d), :] = ref[
                    :, col:col + LANES].astype(F32)
        return jnp.concatenate([scr[slot * nc + c] for c in range(nc)], axis=1)

    dils = [d for _, d in B_GROUPS]
    os_ = [token_major(r, i, d) for i, (r, d) in enumerate(zip((o1, o2, o3), dils))]
    ls = [token_major(r, 3 + i, d) for i, (r, d) in enumerate(zip((l1, l2, l3), dils))]
    mx = jnp.maximum(jnp.maximum(ls[0], ls[1]), ls[2])
    es = [jnp.exp(l - mx) for l in ls]
    den = es[0] + es[1] + es[2]
    ob = (es[0] / den) * os_[0] + (es[1] / den) * os_[1] + (es[2] / den) * os_[2]
    pa = jnp.dot(ya_ref[...], wpa[...], preferred_element_type=F32)
    pb = jnp.dot(ob.astype(BF16), wpb[...], preferred_element_type=F32)
    merged = sga_ref[...].astype(F32) * pa + sgb_ref[...].astype(F32) * pb
    mix = jnp.dot(merged.astype(BF16), wo[...], preferred_element_type=F32)
    x1 = x_ref[...] + mod[2:3] * mix
    h2 = _rms(x1, n2_ref[...]) * (1.0 + mod[4:5]) + mod[3:4]
    logits = lax.dot_general(wrt[...], h2, (((1,), (1,)), ((), ())), preferred_element_type=F32,
                             precision=lax.Precision.HIGHEST)
    scores = jax.nn.sigmoid(logits)
    idxs, wts, ranks, counts = _route(scores + rbias[...], scores)
    for k in range(TOP_K):
        ridx_ref[k:k + 1, :] = idxs[k]
        rw_ref[k:k + 1, :] = wts[k]
        rank_ref[k:k + 1, :] = ranks[k]
    ridx_ref[TOP_K:, :] = jnp.zeros((SUBLANES - TOP_K, tm), I32)
    rw_ref[TOP_K:, :] = jnp.zeros((SUBLANES - TOP_K, tm), F32)
    rank_ref[TOP_K:, :] = jnp.zeros((SUBLANES - TOP_K, tm), I32)
    cnt_ref[0] = counts
    hb = h2.astype(BF16)
    g = jnp.dot(hb, wsg[...], preferred_element_type=F32)
    u = jnp.dot(hb, wsu[...], preferred_element_type=F32)
    act = (g * jax.nn.sigmoid(g) * u).astype(BF16)
    shared = jnp.dot(act, wsd[...], preferred_element_type=F32)
    base_ref[...] = x1 + mod[5:6] * shared
    _store_slabs(hp_ref, _pack_rows(h2))


def _merge(x2, ya, outs, lses, sga, sgb, mod3, norm2, wpa, wpb, wo, wrt, rbias, wsg, wsu, wsd, seq):
    n = x2.shape[0]
    tm = TM_MERGE
    assert seq % tm == 0
    row = lambda i: (i, 0)
    full = lambda a: pl.BlockSpec(a.shape, lambda i: (0,) * a.ndim)
    weights = [wpa, wpb, wo, wrt, rbias, wsg, wsu, wsd]
    group_specs = [pl.BlockSpec((tm // d, d * B_OUT_W), row) for _, d in B_GROUPS]
    lanes = lambda i: (0, i)
    return pl.pallas_call(
        _merge_kernel,
        out_shape=[jax.ShapeDtypeStruct((n, D_MODEL), F32),
                   jax.ShapeDtypeStruct((n * ROW_SLABS, LANES), U32),
                   jax.ShapeDtypeStruct((SUBLANES, n), I32),
                   jax.ShapeDtypeStruct((SUBLANES, n), F32),
                   jax.ShapeDtypeStruct((SUBLANES, n), I32),
                   jax.ShapeDtypeStruct((n // tm, N_EXPERTS, 1), I32)],
        grid=(n // tm,),
        in_specs=[pl.BlockSpec((tm, D_MODEL), row), pl.BlockSpec((tm, A_Q_W), row)]
                 + group_specs * 2
                 + [pl.BlockSpec((tm, D_MODEL), row)] * 2
                 + [pl.BlockSpec((1, 6, D_MODEL), lambda i: (i * tm // seq, 0, 0)),
                    pl.BlockSpec((1, D_MODEL), lambda i: (0, 0))]
                 + [full(w) for w in weights],
        out_specs=[pl.BlockSpec((tm, D_MODEL), row),
                   pl.BlockSpec((tm * ROW_SLABS, LANES), row),
                   pl.BlockSpec((SUBLANES, tm), lanes), pl.BlockSpec((SUBLANES, tm), lanes),
                   pl.BlockSpec((SUBLANES, tm), lanes),
                   pl.BlockSpec((1, N_EXPERTS, 1), lambda i: (i, 0, 0))],
        scratch_shapes=[pltpu.VMEM((6 * B_OUT_W // LANES, tm, LANES), F32)],
        compiler_params=_cparams(1),
        name="merge_route",
    )(x2, ya, *outs, *lses, sga, sgb, mod3, norm2.reshape(1, D_MODEL), *weights)


def _plan_kernel(ridx_ref, rank_ref, tb_ref, dest_ref):
    tb = tb_ref[0]
    t = ridx_ref.shape[1]
    iota_e = lax.broadcasted_iota(I32, (N_EXPERTS, t), 0)
    for k in range(TOP_K):
        base = jnp.sum(jnp.where(iota_e == ridx_ref[k:k + 1, :], tb, 0), axis=0, keepdims=True)
        dest_ref[0, :, k * t:(k + 1) * t] = base + rank_ref[k:k + 1, :]


def _plan(ridx, rank, tile_base, tm):
    n = ridx.shape[1]
    lanes = lambda i: (0, i)
    return pl.pallas_call(
        _plan_kernel,
        out_shape=jax.ShapeDtypeStruct((n // tm, 1, TOP_K * tm), I32),
        grid=(n // tm,),
        in_specs=[pl.BlockSpec((SUBLANES, tm), lanes), pl.BlockSpec((SUBLANES, tm), lanes),
                  pl.BlockSpec((1, N_EXPERTS, 1), lambda i: (i, 0, 0))],
        out_specs=pl.BlockSpec((1, 1, TOP_K * tm), lambda i: (i, 0, 0)),
        compiler_params=_cparams(1),
        name="moe_plan",
    )(ridx, rank, tile_base)


def _sc_mesh():
    return plsc.VectorSubcoreMesh(core_axis_name="core", subcore_axis_name="subcore",
                                  num_cores=SC_CORES, num_subcores=SC_SUBCORES)


def _sc_worker():
    return lax.axis_index("subcore") * SC_CORES + lax.axis_index("core")


def _sc_dispatch(hp, dest_rows, n_rows):
    n = hp.shape[0]
    halves = TM_MERGE // SC_CHUNK
    workers = SC_CORES * SC_SUBCORES
    tiles_per_w = n // TM_MERGE // workers
    idx_per_w = tiles_per_w * TOP_K * halves
    assert tiles_per_w * workers * TM_MERGE == n

    @functools.partial(
        pl.kernel, mesh=_sc_mesh(),
        out_type=jax.ShapeDtypeStruct((n_rows, ROW_SLABS, LANES), U32),
        scratch_types=[pltpu.VMEM((idx_per_w, SC_CHUNK), I32),
                       pltpu.VMEM((SC_CHUNK, ROW_SLABS, LANES), U32),
                       pltpu.SemaphoreType.DMA],
        name="moe_dispatch_sc")
    def scatter(hp_hbm, idx_hbm, xs_hbm, idx_v, rows_v, sem):
        wid = _sc_worker()
        pltpu.sync_copy(idx_hbm.at[pl.ds(wid * idx_per_w, idx_per_w)], idx_v)

        @pl.loop(0, tiles_per_w * halves)
        def _(j):
            tile = j // halves
            h = j - tile * halves
            tok = (wid * tiles_per_w + tile) * TM_MERGE + h * SC_CHUNK
            pltpu.sync_copy(hp_hbm.at[pl.ds(tok, SC_CHUNK)], rows_v)
            for k in range(TOP_K):
                row = (tile * TOP_K + k) * halves + h
                pltpu.async_copy(rows_v, xs_hbm.at[idx_v.at[row]], sem).wait()

    return scatter(hp, dest_rows)


def _sc_collect(ys, dest_rows):
    n_chunks = dest_rows.shape[0]
    workers = SC_CORES * SC_SUBCORES
    per_w = n_chunks // workers
    assert per_w * workers == n_chunks

    @functools.partial(
        pl.kernel, mesh=_sc_mesh(),
        out_type=jax.ShapeDtypeStruct((n_chunks * SC_CHUNK, ROW_SLABS, LANES), U32),
        scratch_types=[pltpu.VMEM((per_w, SC_CHUNK), I32),
                       pltpu.VMEM((SC_CHUNK, ROW_SLABS, LANES), U32),
                       pltpu.SemaphoreType.DMA],
        name="moe_collect_sc")
    def gather(ys_hbm, idx_hbm, out_hbm, idx_v, rows_v, sem):
        wid = _sc_worker()
        base = wid * per_w
        pltpu.sync_copy(idx_hbm.at[pl.ds(base, per_w)], idx_v)

        @pl.loop(0, per_w)
        def _(j):
            pltpu.async_copy(ys_hbm.at[idx_v.at[j]], rows_v, sem).wait()
            pltpu.sync_copy(rows_v, out_hbm.at[pl.ds((base + j) * SC_CHUNK, SC_CHUNK)])

    return gather(ys, dest_rows)


def _expert_kernel(blk_e_ref, valid_ref, src_ref, xs_ref, wg_ref, wu_ref, wd_ref, ys_ref,
                   wg_bf, wu_bf, wd_bf):
    del src_ref
    step = pl.program_id(0)
    valid = valid_ref[step]

    @pl.when((step == 0) | (blk_e_ref[step] != blk_e_ref[jnp.maximum(step - 1, 0)]))
    def _():
        wg_bf[...] = wg_ref[0].astype(BF16)
        wu_bf[...] = wu_ref[0].astype(BF16)
        wd_bf[...] = wd_ref[0].astype(BF16)

    @pl.when(valid > 0)
    def _():
        rows = lax.broadcasted_iota(I32, (xs_ref.shape[0] // ROW_SLABS, 1), 0)
        lo, hi = _unpack_rows(jnp.where(rows < valid, _load_slabs(xs_ref), jnp.uint32(0)))
        lo, hi = lo.astype(BF16), hi.astype(BF16)

        def up(w_bf):
            return jnp.dot(lo, w_bf[:HALF_D], preferred_element_type=F32) \
                + jnp.dot(hi, w_bf[HALF_D:], preferred_element_type=F32)

        g = up(wg_bf)
        act = (g * jax.nn.sigmoid(g) * up(wu_bf)).astype(BF16)
        _store_slabs(ys_ref, _pack_rows(jnp.dot(act, wd_bf[...], preferred_element_type=F32)))


def _experts(xs, blk_e, blk_valid, blk_src, wg, wu, wd):
    n_rows = xs.shape[0] // ROW_SLABS
    bm = MOE_BM
    nblk = n_rows // bm
    return pl.pallas_call(
        _expert_kernel,
        out_shape=jax.ShapeDtypeStruct((n_rows * ROW_SLABS, LANES), U32),
        grid_spec=pltpu.PrefetchScalarGridSpec(
            num_scalar_prefetch=3,
            grid=(nblk,),
            in_specs=[pl.BlockSpec((bm * ROW_SLABS, LANES), lambda i, be, bv, bs: (bs[i], 0)),
                      pl.BlockSpec((1, D_MODEL, D_EXPERT), lambda i, be, bv, bs: (be[i], 0, 0)),
                      pl.BlockSpec((1, D_MODEL, D_EXPERT), lambda i, be, bv, bs: (be[i], 0, 0)),
                      pl.BlockSpec((1, D_EXPERT, D_MODEL), lambda i, be, bv, bs: (be[i], 0, 0))],
            out_specs=pl.BlockSpec((bm * ROW_SLABS, LANES), lambda i, be, bv, bs: (bs[i], 0)),
            scratch_shapes=[pltpu.VMEM((D_MODEL, D_EXPERT), BF16), pltpu.VMEM((D_MODEL, D_EXPERT), BF16),
                            pltpu.VMEM((D_EXPERT, D_MODEL), BF16)]),
        compiler_params=_cparams(1),
        name="moe_experts",
    )(blk_e, blk_valid, blk_src, xs, wg, wu, wd)


def _combine_kernel(g_ref, base_ref, rw_ref, mod_ref, fn_ref, *rest):
    y_ref = rest[-1]
    tm = base_ref.shape[0]
    rw = rw_ref[...]
    acc_lo = jnp.zeros((tm, HALF_D), F32)
    acc_hi = jnp.zeros((tm, HALF_D), F32)
    for k in range(TOP_K):
        lo, hi = _unpack_rows(_load_slabs(g_ref.at[0, k]))
        w = rw[:, k:k + 1]
        acc_lo = acc_lo + w * lo
        acc_hi = acc_hi + w * hi
    routed = jnp.concatenate([acc_lo, acc_hi], axis=1)
    x2 = base_ref[...] + mod_ref[0][5:6] * routed
    y_ref[...] = _rms(x2, fn_ref[...])


def _combine(g, base, rw_t, mod3, final_norm, seq, part, y_prev):
    n = base.shape[0]
    tm = TM_COMBINE
    assert seq % tm == 0
    g4 = g.reshape(-1, TOP_K, tm * ROW_SLABS, LANES)
    tiles = g4.shape[0]
    off = part * tiles
    row = lambda i: (i + off, 0)
    in_specs = [pl.BlockSpec((1, TOP_K, tm * ROW_SLABS, LANES), lambda i: (i, 0, 0, 0)),
                pl.BlockSpec((tm, D_MODEL), row),
                pl.BlockSpec((tm, SUBLANES), row),
                pl.BlockSpec((1, 6, D_MODEL), lambda i: ((i + off) * tm // seq, 0, 0)),
                pl.BlockSpec((1, D_MODEL), lambda i: (0, 0))]
    args = [g4, base, rw_t, mod3, final_norm.reshape(1, D_MODEL)]
    aliases = {}
    if y_prev is not None:
        in_specs.append(pl.BlockSpec(memory_space=pl.ANY))
        args.append(y_prev)
        aliases = {len(args) - 1: 0}
    return pl.pallas_call(
        _combine_kernel,
        out_shape=jax.ShapeDtypeStruct((n, D_MODEL), F32),
        grid=(tiles,),
        in_specs=in_specs,
        out_specs=pl.BlockSpec((tm, D_MODEL), row),
        input_output_aliases=aliases,
        compiler_params=_cparams(1),
        name="moe_combine",
    )(*args)


def _block_layout(tile_counts, n):
    bm = MOE_BM
    c = tile_counts[:, :, 0]
    ntiles = c.shape[0]
    counts = jnp.sum(c, axis=0)
    padded = (counts + bm - 1) // bm * bm
    earlier_e = np.tri(N_EXPERTS, k=-1, dtype=bool)
    pstart = jnp.sum(jnp.where(earlier_e, padded[None, :], 0), axis=1)
    pend = pstart + padded
    earlier_t = np.tri(ntiles, k=-1, dtype=bool)
    tile_base = pstart[None, :] + jnp.sum(jnp.where(earlier_t[:, :, None], c[None], 0), axis=1)
    nblk = -(-n * TOP_K // bm) + N_EXPERTS
    blk = jnp.arange(nblk, dtype=I32)
    blk_e = jnp.minimum(jnp.sum((pend[None, :] <= blk[:, None] * bm).astype(I32), axis=1),
                        N_EXPERTS - 1)
    onehot_e = blk_e[:, None] == jnp.arange(N_EXPERTS, dtype=I32)[None, :]
    end_e = jnp.sum(jnp.where(onehot_e, (pstart + counts)[None, :], 0), axis=1)
    blk_valid = jnp.clip(end_e - blk * bm, 0, bm).astype(I32)
    blk_src = jnp.minimum(blk, pend[N_EXPERTS - 1] // bm - 1).astype(I32)
    last_e = jnp.max(jnp.where(counts > 0, jnp.arange(N_EXPERTS, dtype=I32), 0))
    blk_e = jnp.where(blk == blk_src, blk_e, last_e).astype(I32)
    return tile_base.astype(I32)[:, :, None], blk_e, blk_valid, blk_src, nblk * bm


def _front(x, mod3, p, after=None):
    nbatch, seq, _ = x.shape
    n = nbatch * seq
    x2 = x.reshape(n, D_MODEL)
    proj = _inproj(x2, mod3, p["norm1"], p["w_in"], seq)
    qa, kva = proj[:2]
    sga, sgb = proj[8:]
    (ya,) = _band_attention(qa, kva, p["bias_a"], p["sink_a"], nbatch=nbatch, seq=seq, dil=1,
                            halo=A_HALF_WINDOW, step_rows=ATTN_ROWS_A, shared_kv=True, want_lse=False)
    outs, lses = [], []
    for gi, (w, d) in enumerate(B_GROUPS):
        q, kv = proj[2 + 2 * gi:4 + 2 * gi]
        o, lse = _band_attention(q, kv, p["bias_b"][gi], None, nbatch=nbatch, seq=seq, dil=d,
                                 halo=w // (2 * d), step_rows=ATTN_ROWS_B, shared_kv=False,
                                 want_lse=True)
        outs.append(o)
        lses.append(lse)
    if after is not None:
        ya, after = lax.optimization_barrier((ya, after))
    base, hp, ridx, rw, rank, tile_counts = _merge(
        x2, ya, outs, lses, sga, sgb, mod3, p["norm2"], p["w_pa"], p["w_pb"], p["w_o"], p["w_rt"],
        p["rbias"], p["ws_gate"], p["ws_up"], p["ws_down"], seq)
    assert TM_MERGE == TM_COMBINE
    tile_base, blk_e, blk_valid, blk_src, n_rows = _block_layout(tile_counts, n)
    dest_rows = _plan(ridx, rank, tile_base, TM_MERGE).reshape(-1, SC_CHUNK)
    return dict(base=base, hp=hp, rw=rw, dest_rows=dest_rows, blk_e=blk_e, blk_valid=blk_valid,
                blk_src=blk_src, n_rows=n_rows, shape=x.shape, after=after)


def _slabs(a):
    return a.reshape(-1, ROW_SLABS, LANES)


def _moe(st, xs, mod3, p):
    nbatch, seq, _ = st["shape"]
    dest_rows = st["dest_rows"]
    ys = _experts(xs.reshape(-1, LANES), st["blk_e"], st["blk_valid"], st["blk_src"], p["w_gate"], p["w_up"],
                  p["w_down"])
    rows_per_part = dest_rows.shape[0] // COLLECT_PARTS
    rw_t = st["rw"].T
    y = None
    for part in range(COLLECT_PARTS):
        g = _sc_collect(_slabs(ys), dest_rows[part * rows_per_part:(part + 1) * rows_per_part])
        y = _combine(g, st["base"], rw_t, mod3, p["final_norm"], seq, part, y)
    return y.reshape(nbatch, seq, D_MODEL)


def kernel(x_prompt, x_sample, c_prompt, c_sample, rel_bias, w_ada, b_ada, norm1, w_in, sink, w_pa, w_pb, w_o, norm2, w_router, router_bias, w_gate, w_up, w_down, ws_gate, ws_up, ws_down, final_norm):
    assert w_ada.shape[0] == 1
    nbp = x_prompt.shape[0]
    mod = _ada(jnp.concatenate([c_prompt, c_sample], axis=0), w_ada[0], b_ada[0])
    mod3 = mod.reshape(-1, 6, D_MODEL)

    def pair_bias(heads, halo, dist_scale):
        kb = ATTN_BLOCK + 2 * halo
        tab = _bias_table(heads, ATTN_BLOCK, kb, halo, halo, dist_scale)
        return tab.reshape(-1, 2, ATTN_BLOCK, kb).transpose(0, 2, 1, 3).reshape(-1, ATTN_BLOCK, 2 * kb)

    bias_a = pair_bias(rel_bias[:, :A_Q_HEADS], A_HALF_WINDOW, 1)
    bias_b = []
    for gi, (w, d) in enumerate(B_GROUPS):
        h0 = A_Q_HEADS + gi * B_HEADS_PER_GROUP
        bias_b.append(pair_bias(rel_bias[:, h0:h0 + B_HEADS_PER_GROUP], w // (2 * d), d))
    p = {
        "norm1": norm1[0], "norm2": norm2[0], "final_norm": final_norm,
        "w_in": w_in[0].astype(BF16),
        "bias_a": bias_a,
        "sink_a": jnp.repeat(sink[0].astype(F32), HEAD_DIM).reshape(A_Q_HEADS // 2, 1, LANES),
        "bias_b": bias_b,
        "w_pa": w_pa[0].astype(BF16), "w_pb": w_pb[0].astype(BF16), "w_o": w_o[0].astype(BF16),
        "w_rt": w_router[0].T, "rbias": router_bias[0].reshape(N_EXPERTS, 1),
        "ws_gate": ws_gate[0].astype(BF16), "ws_up": ws_up[0].astype(BF16),
        "ws_down": ws_down[0].astype(BF16),
        "w_gate": w_gate[0], "w_up": w_up[0], "w_down": w_down[0],
    }
    dispatch = lambda st: _sc_dispatch(_slabs(st["hp"]), st["dest_rows"], st["n_rows"])
    st_p = _front(x_prompt, mod3[:nbp], p)
    st_s = _front(x_sample, mod3[nbp:], p, after=dispatch(st_p))
    return (_moe(st_p, st_s["after"], mod3[:nbp], p), _moe(st_s, dispatch(st_s), mod3[nbp:], p))
```

```python
import functools
import math

import jax
import jax.numpy as jnp
import numpy as np
from jax import lax
from jax.experimental import pallas as pl
from jax.experimental.pallas import tpu as pltpu
from jax.experimental.pallas import tpu_sc as plsc

F32 = jnp.float32
BF16 = jnp.bfloat16
U32 = jnp.uint32
I32 = jnp.int32

D_MODEL = 1024
HEAD_DIM = 64
A_Q_HEADS = 8
A_KV_HEADS = 2
A_HALF_WINDOW = 128
B_GROUPS = ((128, 1), (512, 4), (2048, 16))
B_HEADS_PER_GROUP = 4
N_BUCKETS = 32
MAX_DISTANCE = 1024
N_EXPERTS = 64
TOP_K = 6
N_EXPERT_GROUPS = 8
TOPK_GROUPS = 4
D_EXPERT = 256
ROUTED_SCALE = 2.5
RMS_EPS = 1e-6
NEG_INF = -1e30
N_MOD = 6
SH1, SC1, G1, SH2, SC2, G2 = range(N_MOD)
REMOVED = -3e38

A_Q_W = A_Q_HEADS * HEAD_DIM
A_KV_W = A_KV_HEADS * HEAD_DIM
B_W = len(B_GROUPS) * B_HEADS_PER_GROUP * HEAD_DIM
B_OUT_W = B_HEADS_PER_GROUP * HEAD_DIM
D_IN = A_Q_W + 2 * A_KV_W + 3 * B_W + 2 * D_MODEL
HALF_D = D_MODEL // 2
LANES = 128
SUBLANES = 8
ROW_SLABS = HALF_D // LANES
SC_CORES = 2
SC_SUBCORES = 16
SC_CHUNK = 128

TM_INPROJ = 1024
TM_MERGE = 512
ATTN_BLOCK = 128
ATTN_ROWS_A = 512
ATTN_ROWS_B = 2048
MOE_BM = 1024
TM_COMBINE = 512
COLLECT_PARTS = 2
VMEM_LIMIT = 60 * 1024 * 1024


def _cparams(n_axes):
    return pltpu.CompilerParams(
        dimension_semantics=("arbitrary",) * n_axes, vmem_limit_bytes=VMEM_LIMIT)


def _ada_kernel(c_ref, w_ref, b_ref, o_ref):
    c = c_ref[...]
    s = c * jax.nn.sigmoid(c)
    o_ref[...] = jnp.dot(s, w_ref[...], preferred_element_type=F32,
                         precision=lax.Precision.HIGHEST) + b_ref[...]


def _ada(c_all, w_ada, b_ada):
    nb = c_all.shape[0]
    return pl.pallas_call(
        _ada_kernel,
        out_shape=jax.ShapeDtypeStruct((nb, N_MOD * D_MODEL), F32),
        grid=(N_MOD,),
        in_specs=[pl.BlockSpec((nb, D_MODEL), lambda j: (0, 0)),
                  pl.BlockSpec((D_MODEL, D_MODEL), lambda j: (0, j)),
                  pl.BlockSpec((1, D_MODEL), lambda j: (0, j))],
        out_specs=pl.BlockSpec((nb, D_MODEL), lambda j: (0, j)),
        compiler_params=_cparams(1),
        name="ada",
    )(c_all, w_ada, b_ada.reshape(1, N_MOD * D_MODEL))


def _rms(x, g):
    return x * lax.rsqrt(jnp.mean(x * x, axis=-1, keepdims=True) + RMS_EPS) * g


def _inproj_kernel(x_ref, mod_ref, n1_ref, w_ref, qa, kva, *rest):
    q_b = rest[0:6:2]
    kv_b = rest[1:6:2]
    sga, sgb, scr = rest[6:]
    mod = mod_ref[0]
    h = _rms(x_ref[...], n1_ref[...]) * (1.0 + mod[SC1:SC1 + 1]) + mod[SH1:SH1 + 1]
    hb = h.astype(BF16)
    tm = hb.shape[0]
    scale = HEAD_DIM ** -0.5

    def proj(off, width):
        return jnp.dot(hb, w_ref[:, off:off + width], preferred_element_type=F32)

    qa[...] = (proj(0, A_Q_W) * scale).astype(BF16)
    for j, off in enumerate((A_Q_W, A_Q_W + A_KV_W)):
        r = proj(off, A_KV_W)
        kva[:, 2 * j * A_KV_W:(2 * j + 1) * A_KV_W] = r.astype(BF16)
        kva[:, (2 * j + 1) * A_KV_W:(2 * j + 2) * A_KV_W] = pltpu.roll(r, HEAD_DIM, 1).astype(BF16)
    off = A_Q_W + 2 * A_KV_W
    for t in range(3):
        for gi, (_, d) in enumerate(B_GROUPS):
            r = proj(off + t * B_W + gi * B_OUT_W, B_OUT_W)
            if t == 0:
                r = r * scale
            ref, width, base = (q_b[gi], B_OUT_W, 0) if t == 0 else (kv_b[gi], 2 * B_OUT_W,
                                                                     (t - 1) * B_OUT_W)
            if d == 1:
                ref[:, base:base + B_OUT_W] = r.astype(BF16)
            else:
                for c in range(B_OUT_W // LANES):
                    slot = (t * 2 + gi - 1) * (B_OUT_W // LANES) + c
                    scr[slot] = r[:, c * LANES:(c + 1) * LANES]
                    for res in range(d):
                        col = res * width + base + c * LANES
                        ref[:, col:col + LANES] = scr[
                            slot, pl.ds(res, tm // d, stride=d), :].astype(BF16)
    off += 3 * B_W
    sga[...] = jax.nn.sigmoid(proj(off, D_MODEL)).astype(BF16)
    sgb[...] = jax.nn.sigmoid(proj(off + D_MODEL, D_MODEL)).astype(BF16)


def _inproj(x2, mod3, norm1, w_in_bf, seq):
    n = x2.shape[0]
    tm = TM_INPROJ
    assert seq % tm == 0 and n % tm == 0
    row = lambda i: (i, 0)
    shapes = [(n, A_Q_W, tm), (n, 4 * A_KV_W, tm)]
    for _, d in B_GROUPS:
        shapes += [(n // d, d * B_OUT_W, tm // d), (n // d, d * 2 * B_OUT_W, tm // d)]
    shapes += [(n, D_MODEL, tm)] * 2
    return pl.pallas_call(
        _inproj_kernel,
        out_shape=[jax.ShapeDtypeStruct((r, c), BF16) for r, c, _ in shapes],
        grid=(n // tm,),
        in_specs=[pl.BlockSpec((tm, D_MODEL), row),
                  pl.BlockSpec((1, N_MOD, D_MODEL), lambda i: (i * tm // seq, 0, 0)),
                  pl.BlockSpec((1, D_MODEL), lambda i: (0, 0)),
                  pl.BlockSpec((D_MODEL, D_IN), lambda i: (0, 0), pipeline_mode=pl.Buffered(1))],
        out_specs=[pl.BlockSpec((b, c), row) for _, c, b in shapes],
        scratch_shapes=[pltpu.VMEM((6 * B_OUT_W // LANES, tm, LANES), F32)],
        compiler_params=_cparams(1),
        name="inproj",
    )(x2, mod3, norm1.reshape(1, D_MODEL), w_in_bf)


def _rel_bucket_np(rel):
    half = N_BUCKETS // 2
    max_exact = half // 2
    n = np.abs(rel)
    large = max_exact + (np.log(np.maximum(n, 1) / max_exact) / math.log(MAX_DISTANCE / max_exact)
                         * (half - max_exact)).astype(np.int32)
    large = np.minimum(large, half - 1)
    return ((rel > 0).astype(np.int32) * half + np.where(n < max_exact, n, large)).astype(np.int32)


def _bias_table(rel_bias_heads, n_q, n_k, key_off, band, dist_scale):
    p = n_q + n_k
    rel = np.arange(p) - (n_q - 1) - key_off
    bucket = _rel_bucket_np(rel * dist_scale)
    t = jnp.where((np.abs(rel) <= band)[None], rel_bias_heads.astype(F32)[bucket].T, NEG_INF)
    big = jnp.tile(t, (1, n_q + 1))
    tab = big[:, n_q - 1:n_q - 1 + n_q * (p - 1)].reshape(-1, n_q, p - 1)
    return tab[:, :, :n_k]


def _pair_rhs(k_top, k_bot, v_top, v_bot):
    kb = k_top.shape[0]
    low = jnp.where(lax.broadcasted_iota(I32, (kb, LANES), 1) < HEAD_DIM, 1.0, 0.0).astype(BF16)
    high = jnp.where(lax.broadcasted_iota(I32, (kb, LANES), 1) < HEAD_DIM, 0.0, 1.0).astype(BF16)
    rhs_k = jnp.concatenate([k_top * low, k_bot * high], axis=0)
    rhs_v = jnp.concatenate([jnp.concatenate([v_top * low, low], axis=1),
                             jnp.concatenate([v_bot * high, high], axis=1)], axis=0)
    return rhs_k, rhs_v


def _pair_attention(q_pair, rhs_k, rhs_v, bias_pair, pen, sink_pair, want_lse):
    kb = rhs_k.shape[0] // 2
    s = lax.dot_general(q_pair, rhs_k, (((1,), (1,)), ((), ())), preferred_element_type=F32)
    s = s + bias_pair
    if pen is not None:
        s = s + pen
    s0, s1 = s[:, :kb], s[:, kb:]
    m0 = jnp.max(s0, axis=-1, keepdims=True)
    m1 = jnp.max(s1, axis=-1, keepdims=True)
    if sink_pair is not None:
        m0 = jnp.maximum(m0, sink_pair[:, 0:1])
        m1 = jnp.maximum(m1, sink_pair[:, HEAD_DIM:HEAD_DIM + 1])
    p = jnp.concatenate([jnp.exp((s0 - m0).astype(BF16)), jnp.exp((s1 - m1).astype(BF16))], axis=1)
    od = jnp.dot(p, rhs_v, preferred_element_type=F32)
    o, den = od[:, :LANES], od[:, LANES:]
    low = lax.broadcasted_iota(I32, o.shape, 1) < HEAD_DIM
    m_full = jnp.where(low, m0, m1)
    if sink_pair is not None:
        den = den + jnp.exp(sink_pair - m_full)
    return o / den, (m_full + jnp.log(den)) if want_lse else None


def _attn_kernel(*refs, halo, nblk, n_pairs, n_seqs, shared_kv, has_sink, want_lse, n_steps):
    it = iter(refs)
    q_ref = next(it)
    kv_prev, kv_cur, kv_next = next(it), next(it), next(it)
    bias_ref = next(it)
    sink_ref = next(it) if has_sink else None
    o_ref = next(it)
    lse_ref = next(it) if want_lse else None
    step = pl.program_id(2)
    blk = ATTN_BLOCK
    kb = blk + 2 * halo
    qw = n_pairs * LANES
    kv = jnp.concatenate([kv_prev[...], kv_cur[...], kv_next[...]], axis=0)
    col = lax.broadcasted_iota(I32, (1, kb), 1)
    for b in range(nblk):
        lo = jnp.where(step == 0, halo, 0) if b == 0 else 0
        hi = jnp.where(step == n_steps - 1, kb - halo, kb) if b == nblk - 1 else kb
        pen = None
        if b == 0 or b == nblk - 1:
            pen1 = jnp.where((col >= lo) & (col < hi), 0.0, NEG_INF)
            pen = jnp.concatenate([pen1, pen1], axis=1)
        rows = slice(b * blk, (b + 1) * blk)
        krows = slice(b * blk, b * blk + kb)
        lane_col = lambda j: kv[krows, j * LANES:(j + 1) * LANES]
        if shared_kv:
            k, k_sw, v, v_sw = (lane_col(j) for j in range(4))
            rhs = [_pair_rhs(k, k_sw, v, v_sw), _pair_rhs(k_sw, k, v_sw, v)]
        for s in range(n_seqs):
            for c in range(n_pairs):
                lanes = slice(s * qw + c * LANES, s * qw + (c + 1) * LANES)
                if shared_kv:
                    rhs_k, rhs_v = rhs[c // (n_pairs // 2)]
                else:
                    k, v = lane_col(2 * s * n_pairs + c), lane_col((2 * s + 1) * n_pairs + c)
                    rhs_k, rhs_v = _pair_rhs(k, k, v, v)
                o, lse = _pair_attention(q_ref[rows, lanes], rhs_k, rhs_v, bias_ref[c], pen,
                                         sink_ref[c] if has_sink else None, want_lse)
                o_ref[rows, lanes] = o.astype(o_ref.dtype)
                if want_lse:
                    lse_ref[rows, lanes] = lse


def _band_attention(q, kv, bias, sink, *, nbatch, seq, dil, halo, step_rows, shared_kv, want_lse):
    n = nbatch * seq
    sub_len = seq // dil
    qw = q.shape[1] // dil
    kv_width = kv.shape[1] // dil
    rows = min(step_rows, sub_len)
    n_seqs = min(dil, step_rows // rows)
    assert sub_len % rows == 0 and rows % ATTN_BLOCK == 0 and rows % halo == 0 and dil % n_seqs == 0
    nq = sub_len // rows
    per = rows // halo
    total_halos = n // dil // halo
    cur = lambda b, r, i: (b * nq + i, r)
    prev = lambda b, r, i: (jnp.maximum((b * nq + i) * per - 1, 0), r)
    nxt = lambda b, r, i: (jnp.minimum((b * nq + i + 1) * per, total_halos - 1), r)
    const = lambda a: pl.BlockSpec(a.shape, lambda b, r, i: (0,) * a.ndim)
    qw_step, kvw_step = qw * n_seqs, kv_width * n_seqs

    in_specs = [pl.BlockSpec((rows, qw_step), cur), pl.BlockSpec((halo, kvw_step), prev),
                pl.BlockSpec((rows, kvw_step), cur), pl.BlockSpec((halo, kvw_step), nxt),
                const(bias)]
    args = [q, kv, kv, kv, bias]
    if sink is not None:
        in_specs.append(const(sink))
        args.append(sink)
    out_shape = [jax.ShapeDtypeStruct(q.shape, BF16)]
    out_specs = [pl.BlockSpec((rows, qw_step), cur)]
    if want_lse:
        out_shape.append(jax.ShapeDtypeStruct(q.shape, F32))
        out_specs.append(pl.BlockSpec((rows, qw_step), cur))
    return pl.pallas_call(
        functools.partial(_attn_kernel, halo=halo, nblk=rows // ATTN_BLOCK, n_pairs=qw // LANES,
                          n_seqs=n_seqs, shared_kv=shared_kv, has_sink=sink is not None,
                          want_lse=want_lse, n_steps=nq),
        out_shape=out_shape,
        grid=(nbatch, dil // n_seqs, nq),
        in_specs=in_specs,
        out_specs=out_specs,
        compiler_params=_cparams(3),
        name=f"band_attn_d{dil}",
    )(*args)


def _pack_rows(y):
    return pltpu.pack_elementwise([y[:, :HALF_D], y[:, HALF_D:]], packed_dtype=BF16)


def _store_slabs(ref, packed):
    t = packed.shape[0]
    for c in range(ROW_SLABS):
        ref[pl.ds(c, t, stride=ROW_SLABS), :] = packed[:, c * LANES:(c + 1) * LANES]


def _load_slabs(ref):
    t = ref.shape[0] // ROW_SLABS
    return jnp.concatenate([ref[pl.ds(c, t, stride=ROW_SLABS), :] for c in range(ROW_SLABS)], axis=1)


def _unpack_rows(p):
    return tuple(pltpu.unpack_elementwise(p, index=i, packed_dtype=BF16, unpacked_dtype=F32)
                 for i in range(2))


def _route(sel, scores):
    t = sel.shape[-1]
    per = N_EXPERTS // N_EXPERT_GROUPS
    shape3 = (N_EXPERT_GROUPS, per, t)
    sel3 = sel.reshape(shape3)
    sc3 = scores.reshape(shape3)
    iota_g = lax.broadcasted_iota(I32, shape3, 0)
    iota_m = lax.broadcasted_iota(I32, shape3, 1)
    iota_e = iota_g * per + iota_m
    m1 = jnp.max(sel3, axis=1, keepdims=True)
    i1 = jnp.min(jnp.where(sel3 == m1, iota_m, per), axis=1, keepdims=True)
    m2 = jnp.max(jnp.where(iota_m == i1, REMOVED, sel3), axis=1, keepdims=True)
    gscore = m1 + m2
    iota_g1 = lax.broadcasted_iota(I32, gscore.shape, 0)
    gmask = jnp.zeros(gscore.shape, jnp.bool_)
    for _ in range(TOPK_GROUPS):
        mx = jnp.max(gscore, axis=0, keepdims=True)
        ix = jnp.min(jnp.where(gscore == mx, iota_g1, N_EXPERT_GROUPS), axis=0, keepdims=True)
        hit = iota_g1 == ix
        gmask = gmask | hit
        gscore = jnp.where(hit, REMOVED, gscore)
    cur = jnp.where(gmask, sel3, NEG_INF)
    sum_all = lambda a: jnp.sum(jnp.sum(a, axis=1, keepdims=True), axis=0, keepdims=True)
    idxs, wts, hits = [], [], []
    for _ in range(TOP_K):
        mx = jnp.max(jnp.max(cur, axis=1, keepdims=True), axis=0, keepdims=True)
        cand = jnp.where(cur == mx, iota_e, N_EXPERTS)
        ix = jnp.min(jnp.min(cand, axis=1, keepdims=True), axis=0, keepdims=True)
        hit = iota_e == ix
        wts.append(sum_all(jnp.where(hit, sc3, 0.0)).reshape(1, t))
        cur = jnp.where(hit, REMOVED, cur)
        idxs.append(ix.reshape(1, t))
        hits.append(hit)
    wsum = wts[0]
    for w in wts[1:]:
        wsum = wsum + w
    wts = [w / wsum * ROUTED_SCALE for w in wts]
    onehot = hits[0].astype(F32)
    for hit in hits[1:]:
        onehot = onehot + hit.astype(F32)
    tri = (lax.broadcasted_iota(I32, (t, t), 0) <= lax.broadcasted_iota(I32, (t, t), 1))
    cum = jnp.dot(onehot.reshape(N_EXPERTS, t).astype(BF16), tri.astype(F32).astype(BF16),
                  preferred_element_type=F32)
    cum3 = cum.reshape(shape3) - 1.0
    ranks = [sum_all(jnp.where(hit, cum3, 0.0)).reshape(1, t).astype(I32) for hit in hits]
    counts = cum[:, t - 1:t].astype(I32)
    return idxs, wts, ranks, counts


def _merge_kernel(x_ref, ya_ref, o1, o2, o3, l1, l2, l3, sga_ref, sgb_ref, mod_ref, n2_ref,
                  wpa, wpb, wo, wrt, rbias, wsg, wsu, wsd,
                  base_ref, hp_ref, ridx_ref, rw_ref, rank_ref, cnt_ref, scr):
    mod = mod_ref[0]
    tm = x_ref.shape[0]

    def token_major(ref, slot, d):
        if d == 1:
            return ref[...].astype(F32)
        nc = B_OUT_W // LANES
        for c in range(nc):
            for res in range(d):
                col = res * B_OUT_W + c * LANES
                scr[slot * nc + c, pl.ds(res, tm // d, stride=d), :] = ref[
                    :, col:col + LANES].astype(F32)
        return jnp.concatenate([scr[slot * nc + c] for c in range(nc)], axis=1)

    dils = [d for _, d in B_GROUPS]
    os_ = [token_major(r, i, d) for i, (r, d) in enumerate(zip((o1, o2, o3), dils))]
    ls = [token_major(r, 3 + i, d) for i, (r, d) in enumerate(zip((l1, l2, l3), dils))]
    mx = jnp.maximum(jnp.maximum(ls[0], ls[1]), ls[2])
    es = [jnp.exp(l - mx) for l in ls]
    den = es[0] + es[1] + es[2]
    ob = (es[0] / den) * os_[0] + (es[1] / den) * os_[1] + (es[2] / den) * os_[2]
    pa = jnp.dot(ya_ref[...], wpa[...], preferred_element_type=F32)
    pb = jnp.dot(ob.astype(BF16), wpb[...], preferred_element_type=F32)
    merged = sga_ref[...].astype(F32) * pa + sgb_ref[...].astype(F32) * pb
    mix = jnp.dot(merged.astype(BF16), wo[...], preferred_element_type=F32)
    x1 = x_ref[...] + mod[G1:G1 + 1] * mix
    h2 = _rms(x1, n2_ref[...]) * (1.0 + mod[SC2:SC2 + 1]) + mod[SH2:SH2 + 1]
    logits = lax.dot_general(wrt[...], h2, (((1,), (1,)), ((), ())), preferred_element_type=F32,
                             precision=lax.Precision.HIGHEST)
    scores = jax.nn.sigmoid(logits)
    idxs, wts, ranks, counts = _route(scores + rbias[...], scores)
    for k in range(TOP_K):
        ridx_ref[k:k + 1, :] = idxs[k]
        rw_ref[k:k + 1, :] = wts[k]
        rank_ref[k:k + 1, :] = ranks[k]
    ridx_ref[TOP_K:, :] = jnp.zeros((SUBLANES - TOP_K, tm), I32)
    rw_ref[TOP_K:, :] = jnp.zeros((SUBLANES - TOP_K, tm), F32)
    rank_ref[TOP_K:, :] = jnp.zeros((SUBLANES - TOP_K, tm), I32)
    cnt_ref[0] = counts
    hb = h2.astype(BF16)
    g = jnp.dot(hb, wsg[...], preferred_element_type=F32)
    u = jnp.dot(hb, wsu[...], preferred_element_type=F32)
    act = (g * jax.nn.sigmoid(g) * u).astype(BF16)
    shared = jnp.dot(act, wsd[...], preferred_element_type=F32)
    base_ref[...] = x1 + mod[G2:G2 + 1] * shared
    _store_slabs(hp_ref, _pack_rows(h2))


def _merge(x2, ya, outs, lses, sga, sgb, mod3, norm2, wpa, wpb, wo, wrt, rbias, wsg, wsu, wsd, seq):
    n = x2.shape[0]
    tm = TM_MERGE
    assert seq % tm == 0
    row = lambda i: (i, 0)
    full = lambda a: pl.BlockSpec(a.shape, lambda i: (0,) * a.ndim)
    weights = [wpa, wpb, wo, wrt, rbias, wsg, wsu, wsd]
    group_specs = [pl.BlockSpec((tm // d, d * B_OUT_W), row) for _, d in B_GROUPS]
    lanes = lambda i: (0, i)
    return pl.pallas_call(
        _merge_kernel,
        out_shape=[jax.ShapeDtypeStruct((n, D_MODEL), F32),
                   jax.ShapeDtypeStruct((n * ROW_SLABS, LANES), U32),
                   jax.ShapeDtypeStruct((SUBLANES, n), I32),
                   jax.ShapeDtypeStruct((SUBLANES, n), F32),
                   jax.ShapeDtypeStruct((SUBLANES, n), I32),
                   jax.ShapeDtypeStruct((n // tm, N_EXPERTS, 1), I32)],
        grid=(n // tm,),
        in_specs=[pl.BlockSpec((tm, D_MODEL), row), pl.BlockSpec((tm, A_Q_W), row)]
                 + group_specs * 2
                 + [pl.BlockSpec((tm, D_MODEL), row)] * 2
                 + [pl.BlockSpec((1, N_MOD, D_MODEL), lambda i: (i * tm // seq, 0, 0)),
                    pl.BlockSpec((1, D_MODEL), lambda i: (0, 0))]
                 + [full(w) for w in weights],
        out_specs=[pl.BlockSpec((tm, D_MODEL), row),
                   pl.BlockSpec((tm * ROW_SLABS, LANES), row),
                   pl.BlockSpec((SUBLANES, tm), lanes), pl.BlockSpec((SUBLANES, tm), lanes),
                   pl.BlockSpec((SUBLANES, tm), lanes),
                   pl.BlockSpec((1, N_EXPERTS, 1), lambda i: (i, 0, 0))],
        scratch_shapes=[pltpu.VMEM((6 * B_OUT_W // LANES, tm, LANES), F32)],
        compiler_params=_cparams(1),
        name="merge_route",
    )(x2, ya, *outs, *lses, sga, sgb, mod3, norm2.reshape(1, D_MODEL), *weights)


def _plan_kernel(ridx_ref, rank_ref, tb_ref, dest_ref):
    tb = tb_ref[0]
    t = ridx_ref.shape[1]
    iota_e = lax.broadcasted_iota(I32, (N_EXPERTS, t), 0)
    for k in range(TOP_K):
        base = jnp.sum(jnp.where(iota_e == ridx_ref[k:k + 1, :], tb, 0), axis=0, keepdims=True)
        dest_ref[0, :, k * t:(k + 1) * t] = base + rank_ref[k:k + 1, :]


def _plan(ridx, rank, tile_base, tm):
    n = ridx.shape[1]
    lanes = lambda i: (0, i)
    return pl.pallas_call(
        _plan_kernel,
        out_shape=jax.ShapeDtypeStruct((n // tm, 1, TOP_K * tm), I32),
        grid=(n // tm,),
        in_specs=[pl.BlockSpec((SUBLANES, tm), lanes), pl.BlockSpec((SUBLANES, tm), lanes),
                  pl.BlockSpec((1, N_EXPERTS, 1), lambda i: (i, 0, 0))],
        out_specs=pl.BlockSpec((1, 1, TOP_K * tm), lambda i: (i, 0, 0)),
        compiler_params=_cparams(1),
        name="moe_plan",
    )(ridx, rank, tile_base)


def _sc_mesh():
    return plsc.VectorSubcoreMesh(core_axis_name="core", subcore_axis_name="subcore",
                                  num_cores=SC_CORES, num_subcores=SC_SUBCORES)


def _sc_worker():
    return lax.axis_index("subcore") * SC_CORES + lax.axis_index("core")


def _sc_dispatch(hp, dest_rows, n_rows):
    n = hp.shape[0]
    halves = TM_MERGE // SC_CHUNK
    workers = SC_CORES * SC_SUBCORES
    tiles_per_w = n // TM_MERGE // workers
    idx_per_w = tiles_per_w * TOP_K * halves
    assert tiles_per_w * workers * TM_MERGE == n

    @functools.partial(
        pl.kernel, mesh=_sc_mesh(),
        out_type=jax.ShapeDtypeStruct((n_rows, ROW_SLABS, LANES), U32),
        scratch_types=[pltpu.VMEM((idx_per_w, SC_CHUNK), I32),
                       pltpu.VMEM((SC_CHUNK, ROW_SLABS, LANES), U32),
                       pltpu.SemaphoreType.DMA],
        name="moe_dispatch_sc")
    def scatter(hp_hbm, idx_hbm, xs_hbm, idx_v, rows_v, sem):
        wid = _sc_worker()
        pltpu.sync_copy(idx_hbm.at[pl.ds(wid * idx_per_w, idx_per_w)], idx_v)

        @pl.loop(0, tiles_per_w * halves)
        def _(j):
            tile = j // halves
            h = j - tile * halves
            tok = (wid * tiles_per_w + tile) * TM_MERGE + h * SC_CHUNK
            pltpu.sync_copy(hp_hbm.at[pl.ds(tok, SC_CHUNK)], rows_v)
            for k in range(TOP_K):
                row = (tile * TOP_K + k) * halves + h
                pltpu.async_copy(rows_v, xs_hbm.at[idx_v.at[row]], sem).wait()

    return scatter(hp, dest_rows)


def _sc_collect(ys, dest_rows):
    n_chunks = dest_rows.shape[0]
    workers = SC_CORES * SC_SUBCORES
    per_w = n_chunks // workers
    assert per_w * workers == n_chunks

    @functools.partial(
        pl.kernel, mesh=_sc_mesh(),
        out_type=jax.ShapeDtypeStruct((n_chunks * SC_CHUNK, ROW_SLABS, LANES), U32),
        scratch_types=[pltpu.VMEM((per_w, SC_CHUNK), I32),
                       pltpu.VMEM((SC_CHUNK, ROW_SLABS, LANES), U32),
                       pltpu.SemaphoreType.DMA],
        name="moe_collect_sc")
    def gather(ys_hbm, idx_hbm, out_hbm, idx_v, rows_v, sem):
        wid = _sc_worker()
        base = wid * per_w
        pltpu.sync_copy(idx_hbm.at[pl.ds(base, per_w)], idx_v)

        @pl.loop(0, per_w)
        def _(j):
            pltpu.async_copy(ys_hbm.at[idx_v.at[j]], rows_v, sem).wait()
            pltpu.sync_copy(rows_v, out_hbm.at[pl.ds((base + j) * SC_CHUNK, SC_CHUNK)])

    return gather(ys, dest_rows)


def _expert_kernel(blk_e_ref, valid_ref, src_ref, xs_ref, wg_ref, wu_ref, wd_ref, ys_ref,
                   wg_bf, wu_bf, wd_bf):
    del src_ref
    step = pl.program_id(0)
    valid = valid_ref[step]

    @pl.when((step == 0) | (blk_e_ref[step] != blk_e_ref[jnp.maximum(step - 1, 0)]))
    def _():
        wg_bf[...] = wg_ref[0].astype(BF16)
        wu_bf[...] = wu_ref[0].astype(BF16)
        wd_bf[...] = wd_ref[0].astype(BF16)

    @pl.when(valid > 0)
    def _():
        rows = lax.broadcasted_iota(I32, (xs_ref.shape[0] // ROW_SLABS, 1), 0)
        lo, hi = _unpack_rows(jnp.where(rows < valid, _load_slabs(xs_ref), jnp.uint32(0)))
        lo, hi = lo.astype(BF16), hi.astype(BF16)

        def up(w_bf):
            return jnp.dot(lo, w_bf[:HALF_D], preferred_element_type=F32) \
                + jnp.dot(hi, w_bf[HALF_D:], preferred_element_type=F32)

        g = up(wg_bf)
        act = (g * jax.nn.sigmoid(g) * up(wu_bf)).astype(BF16)
        _store_slabs(ys_ref, _pack_rows(jnp.dot(act, wd_bf[...], preferred_element_type=F32)))


def _experts(xs, blk_e, blk_valid, blk_src, wg, wu, wd):
    n_rows = xs.shape[0] // ROW_SLABS
    bm = MOE_BM
    nblk = n_rows // bm
    return pl.pallas_call(
        _expert_kernel,
        out_shape=jax.ShapeDtypeStruct((n_rows * ROW_SLABS, LANES), U32),
        grid_spec=pltpu.PrefetchScalarGridSpec(
            num_scalar_prefetch=3,
            grid=(nblk,),
            in_specs=[pl.BlockSpec((bm * ROW_SLABS, LANES), lambda i, be, bv, bs: (bs[i], 0)),
                      pl.BlockSpec((1, D_MODEL, D_EXPERT), lambda i, be, bv, bs: (be[i], 0, 0)),
                      pl.BlockSpec((1, D_MODEL, D_EXPERT), lambda i, be, bv, bs: (be[i], 0, 0)),
                      pl.BlockSpec((1, D_EXPERT, D_MODEL), lambda i, be, bv, bs: (be[i], 0, 0))],
            out_specs=pl.BlockSpec((bm * ROW_SLABS, LANES), lambda i, be, bv, bs: (bs[i], 0)),
            scratch_shapes=[pltpu.VMEM((D_MODEL, D_EXPERT), BF16), pltpu.VMEM((D_MODEL, D_EXPERT), BF16),
                            pltpu.VMEM((D_EXPERT, D_MODEL), BF16)]),
        compiler_params=_cparams(1),
        name="moe_experts",
    )(blk_e, blk_valid, blk_src, xs, wg, wu, wd)


def _combine_kernel(g_ref, base_ref, rw_ref, mod_ref, fn_ref, *rest):
    y_ref = rest[-1]
    tm = base_ref.shape[0]
    rw = rw_ref[...]
    acc_lo = jnp.zeros((tm, HALF_D), F32)
    acc_hi = jnp.zeros((tm, HALF_D), F32)
    for k in range(TOP_K):
        lo, hi = _unpack_rows(_load_slabs(g_ref.at[0, k]))
        w = rw[:, k:k + 1]
        acc_lo = acc_lo + w * lo
        acc_hi = acc_hi + w * hi
    routed = jnp.concatenate([acc_lo, acc_hi], axis=1)
    x2 = base_ref[...] + mod_ref[0][G2:G2 + 1] * routed
    y_ref[...] = _rms(x2, fn_ref[...])


def _combine(g, base, rw_t, mod3, final_norm, seq, part, y_prev):
    n = base.shape[0]
    tm = TM_COMBINE
    assert seq % tm == 0
    g4 = g.reshape(-1, TOP_K, tm * ROW_SLABS, LANES)
    tiles = g4.shape[0]
    off = part * tiles
    row = lambda i: (i + off, 0)
    in_specs = [pl.BlockSpec((1, TOP_K, tm * ROW_SLABS, LANES), lambda i: (i, 0, 0, 0)),
                pl.BlockSpec((tm, D_MODEL), row),
                pl.BlockSpec((tm, SUBLANES), row),
                pl.BlockSpec((1, N_MOD, D_MODEL), lambda i: ((i + off) * tm // seq, 0, 0)),
                pl.BlockSpec((1, D_MODEL), lambda i: (0, 0))]
    args = [g4, base, rw_t, mod3, final_norm.reshape(1, D_MODEL)]
    aliases = {}
    if y_prev is not None:
        in_specs.append(pl.BlockSpec(memory_space=pl.ANY))
        args.append(y_prev)
        aliases = {len(args) - 1: 0}
    return pl.pallas_call(
        _combine_kernel,
        out_shape=jax.ShapeDtypeStruct((n, D_MODEL), F32),
        grid=(tiles,),
        in_specs=in_specs,
        out_specs=pl.BlockSpec((tm, D_MODEL), row),
        input_output_aliases=aliases,
        compiler_params=_cparams(1),
        name="moe_combine",
    )(*args)


def _block_layout(tile_counts, n):
    bm = MOE_BM
    c = tile_counts[:, :, 0]
    ntiles = c.shape[0]
    counts = jnp.sum(c, axis=0)
    padded = (counts + bm - 1) // bm * bm
    earlier_e = np.tri(N_EXPERTS, k=-1, dtype=bool)
    pstart = jnp.sum(jnp.where(earlier_e, padded[None, :], 0), axis=1)
    pend = pstart + padded
    earlier_t = np.tri(ntiles, k=-1, dtype=bool)
    tile_base = pstart[None, :] + jnp.sum(jnp.where(earlier_t[:, :, None], c[None], 0), axis=1)
    nblk = -(-n * TOP_K // bm) + N_EXPERTS
    blk = jnp.arange(nblk, dtype=I32)
    blk_e = jnp.minimum(jnp.sum((pend[None, :] <= blk[:, None] * bm).astype(I32), axis=1),
                        N_EXPERTS - 1)
    onehot_e = blk_e[:, None] == jnp.arange(N_EXPERTS, dtype=I32)[None, :]
    end_e = jnp.sum(jnp.where(onehot_e, (pstart + counts)[None, :], 0), axis=1)
    blk_valid = jnp.clip(end_e - blk * bm, 0, bm).astype(I32)
    blk_src = jnp.minimum(blk, pend[N_EXPERTS - 1] // bm - 1).astype(I32)
    last_e = jnp.max(jnp.where(counts > 0, jnp.arange(N_EXPERTS, dtype=I32), 0))
    blk_e = jnp.where(blk == blk_src, blk_e, last_e).astype(I32)
    return tile_base.astype(I32)[:, :, None], blk_e, blk_valid, blk_src, nblk * bm


def _front(x, mod3, p, after=None):
    nbatch, seq, _ = x.shape
    n = nbatch * seq
    x2 = x.reshape(n, D_MODEL)
    proj = _inproj(x2, mod3, p["norm1"], p["w_in"], seq)
    qa, kva = proj[:2]
    sga, sgb = proj[8:]
    (ya,) = _band_attention(qa, kva, p["bias_a"], p["sink_a"], nbatch=nbatch, seq=seq, dil=1,
                            halo=A_HALF_WINDOW, step_rows=ATTN_ROWS_A, shared_kv=True, want_lse=False)
    outs, lses = [], []
    for gi, (w, d) in enumerate(B_GROUPS):
        q, kv = proj[2 + 2 * gi:4 + 2 * gi]
        o, lse = _band_attention(q, kv, p["bias_b"][gi], None, nbatch=nbatch, seq=seq, dil=d,
                                 halo=w // (2 * d), step_rows=ATTN_ROWS_B, shared_kv=False,
                                 want_lse=True)
        outs.append(o)
        lses.append(lse)
    if after is not None:
        ya, after = lax.optimization_barrier((ya, after))
    base, hp, ridx, rw, rank, tile_counts = _merge(
        x2, ya, outs, lses, sga, sgb, mod3, p["norm2"], p["w_pa"], p["w_pb"], p["w_o"], p["w_rt"],
        p["rbias"], p["ws_gate"], p["ws_up"], p["ws_down"], seq)
    assert TM_MERGE == TM_COMBINE
    tile_base, blk_e, blk_valid, blk_src, n_rows = _block_layout(tile_counts, n)
    dest_rows = _plan(ridx, rank, tile_base, TM_MERGE).reshape(-1, SC_CHUNK)
    return dict(base=base, hp=hp, rw=rw, dest_rows=dest_rows, blk_e=blk_e, blk_valid=blk_valid,
                blk_src=blk_src, n_rows=n_rows, shape=x.shape, after=after)


def _slabs(a):
    return a.reshape(-1, ROW_SLABS, LANES)


def _moe(st, xs, mod3, p):
    nbatch, seq, _ = st["shape"]
    dest_rows = st["dest_rows"]
    ys = _experts(xs.reshape(-1, LANES), st["blk_e"], st["blk_valid"], st["blk_src"], p["w_gate"], p["w_up"],
                  p["w_down"])
    rows_per_part = dest_rows.shape[0] // COLLECT_PARTS
    rw_t = st["rw"].T
    y = None
    for part in range(COLLECT_PARTS):
        g = _sc_collect(_slabs(ys), dest_rows[part * rows_per_part:(part + 1) * rows_per_part])
        y = _combine(g, st["base"], rw_t, mod3, p["final_norm"], seq, part, y)
    return y.reshape(nbatch, seq, D_MODEL)


def kernel(x_prompt, x_sample, c_prompt, c_sample, rel_bias, w_ada, b_ada, norm1, w_in, sink, w_pa, w_pb, w_o, norm2, w_router, router_bias, w_gate, w_up, w_down, ws_gate, ws_up, ws_down, final_norm):
    assert w_ada.shape[0] == 1
    nbp = x_prompt.shape[0]
    mod = _ada(jnp.concatenate([c_prompt, c_sample], axis=0), w_ada[0], b_ada[0])
    mod3 = mod.reshape(-1, N_MOD, D_MODEL)

    def pair_bias(heads, halo, dist_scale):
        kb = ATTN_BLOCK + 2 * halo
        tab = _bias_table(heads, ATTN_BLOCK, kb, halo, halo, dist_scale)
        return tab.reshape(-1, 2, ATTN_BLOCK, kb).transpose(0, 2, 1, 3).reshape(-1, ATTN_BLOCK, 2 * kb)

    bias_a = pair_bias(rel_bias[:, :A_Q_HEADS], A_HALF_WINDOW, 1)
    bias_b = []
    for gi, (w, d) in enumerate(B_GROUPS):
        h0 = A_Q_HEADS + gi * B_HEADS_PER_GROUP
        bias_b.append(pair_bias(rel_bias[:, h0:h0 + B_HEADS_PER_GROUP], w // (2 * d), d))
    p = {
        "norm1": norm1[0], "norm2": norm2[0], "final_norm": final_norm,
        "w_in": w_in[0].astype(BF16),
        "bias_a": bias_a,
        "sink_a": jnp.repeat(sink[0].astype(F32), HEAD_DIM).reshape(A_Q_HEADS // 2, 1, LANES),
        "bias_b": bias_b,
        "w_pa": w_pa[0].astype(BF16), "w_pb": w_pb[0].astype(BF16), "w_o": w_o[0].astype(BF16),
        "w_rt": w_router[0].T, "rbias": router_bias[0].reshape(N_EXPERTS, 1),
        "ws_gate": ws_gate[0].astype(BF16), "ws_up": ws_up[0].astype(BF16),
        "ws_down": ws_down[0].astype(BF16),
        "w_gate": w_gate[0], "w_up": w_up[0], "w_down": w_down[0],
    }
    dispatch = lambda st: _sc_dispatch(_slabs(st["hp"]), st["dest_rows"], st["n_rows"])
    st_p = _front(x_prompt, mod3[:nbp], p)
    st_s = _front(x_sample, mod3[nbp:], p, after=dispatch(st_p))
    return (_moe(st_p, st_s["after"], mod3[:nbp], p), _moe(st_s, dispatch(st_s), mod3[nbp:], p))
```

```python
import functools
import math

import jax
import jax.numpy as jnp
import numpy as np
from jax import lax
from jax.experimental import pallas as pl
from jax.experimental.pallas import tpu as pltpu
from jax.experimental.pallas import tpu_sc as plsc

F32 = jnp.float32
BF16 = jnp.bfloat16
U32 = jnp.uint32
I32 = jnp.int32

D_MODEL = 1024
HEAD_DIM = 64
A_Q_HEADS = 8
A_KV_HEADS = 2
A_HALF_WINDOW = 128
B_GROUPS = ((128, 1), (512, 4), (2048, 16))
B_HEADS_PER_GROUP = 4
N_BUCKETS = 32
MAX_DISTANCE = 1024
N_EXPERTS = 64
TOP_K = 6
N_EXPERT_GROUPS = 8
TOPK_GROUPS = 4
D_EXPERT = 256
ROUTED_SCALE = 2.5
RMS_EPS = 1e-6
NEG_INF = -1e30
N_MOD = 6
SH1, SC1, G1, SH2, SC2, G2 = range(N_MOD)
REMOVED = -3e38

A_Q_W = A_Q_HEADS * HEAD_DIM
A_KV_W = A_KV_HEADS * HEAD_DIM
B_W = len(B_GROUPS) * B_HEADS_PER_GROUP * HEAD_DIM
B_OUT_W = B_HEADS_PER_GROUP * HEAD_DIM
D_IN = A_Q_W + 2 * A_KV_W + 3 * B_W + 2 * D_MODEL
HALF_D = D_MODEL // 2
LANES = 128
SUBLANES = 8
ROW_SLABS = HALF_D // LANES
SC_CORES = 2
SC_SUBCORES = 16
SC_CHUNK = 128

TM_INPROJ = 1024
TM_MERGE = 512
ATTN_BLOCK = 128
ATTN_ROWS_A = 512
ATTN_ROWS_B = 2048
MOE_BM = 1024
TM_COMBINE = 512
COLLECT_PARTS = 2
PLAN_TILES = 8
VMEM_LIMIT = 60 * 1024 * 1024


def _cparams(n_axes):
    return pltpu.CompilerParams(
        dimension_semantics=("arbitrary",) * n_axes, vmem_limit_bytes=VMEM_LIMIT)


def _ada_kernel(c_ref, w_ref, b_ref, o_ref):
    c = c_ref[...]
    s = c * jax.nn.sigmoid(c)
    o_ref[...] = jnp.dot(s, w_ref[...], preferred_element_type=F32,
                         precision=lax.Precision.HIGHEST) + b_ref[...]


def _ada(c_all, w_ada, b_ada):
    nb = c_all.shape[0]
    return pl.pallas_call(
        _ada_kernel,
        out_shape=jax.ShapeDtypeStruct((nb, N_MOD * D_MODEL), F32),
        grid=(N_MOD,),
        in_specs=[pl.BlockSpec((nb, D_MODEL), lambda j: (0, 0)),
                  pl.BlockSpec((D_MODEL, D_MODEL), lambda j: (0, j)),
                  pl.BlockSpec((1, D_MODEL), lambda j: (0, j))],
        out_specs=pl.BlockSpec((nb, D_MODEL), lambda j: (0, j)),
        compiler_params=_cparams(1),
        name="ada",
    )(c_all, w_ada, b_ada.reshape(1, N_MOD * D_MODEL))


def _rms(x, g):
    return x * lax.rsqrt(jnp.mean(x * x, axis=-1, keepdims=True) + RMS_EPS) * g


def _inproj_kernel(x_ref, mod_ref, n1_ref, w_ref, qa, kva, *rest):
    q_b = rest[0:6:2]
    kv_b = rest[1:6:2]
    sga, sgb, scr = rest[6:]
    mod = mod_ref[0]
    h = _rms(x_ref[...], n1_ref[...]) * (1.0 + mod[SC1:SC1 + 1]) + mod[SH1:SH1 + 1]
    hb = h.astype(BF16)
    tm = hb.shape[0]
    scale = HEAD_DIM ** -0.5

    def proj(off, width):
        return jnp.dot(hb, w_ref[:, off:off + width], preferred_element_type=F32)

    qa[...] = (proj(0, A_Q_W) * scale).astype(BF16)
    for j, off in enumerate((A_Q_W, A_Q_W + A_KV_W)):
        r = proj(off, A_KV_W)
        kva[:, 2 * j * A_KV_W:(2 * j + 1) * A_KV_W] = r.astype(BF16)
        kva[:, (2 * j + 1) * A_KV_W:(2 * j + 2) * A_KV_W] = pltpu.roll(r, HEAD_DIM, 1).astype(BF16)
    off = A_Q_W + 2 * A_KV_W
    for t in range(3):
        for gi, (_, d) in enumerate(B_GROUPS):
            r = proj(off + t * B_W + gi * B_OUT_W, B_OUT_W)
            if t == 0:
                r = r * scale
            ref, width, base = (q_b[gi], B_OUT_W, 0) if t == 0 else (kv_b[gi], 2 * B_OUT_W,
                                                                     (t - 1) * B_OUT_W)
            if d == 1:
                ref[:, base:base + B_OUT_W] = r.astype(BF16)
            else:
                for c in range(B_OUT_W // LANES):
                    slot = (t * 2 + gi - 1) * (B_OUT_W // LANES) + c
                    scr[slot] = r[:, c * LANES:(c + 1) * LANES]
                    for res in range(d):
                        col = res * width + base + c * LANES
                        ref[:, col:col + LANES] = scr[
                            slot, pl.ds(res, tm // d, stride=d), :].astype(BF16)
    off += 3 * B_W
    sga[...] = jax.nn.sigmoid(proj(off, D_MODEL)).astype(BF16)
    sgb[...] = jax.nn.sigmoid(proj(off + D_MODEL, D_MODEL)).astype(BF16)


def _inproj(x2, mod3, norm1, w_in_bf, seq):
    n = x2.shape[0]
    tm = TM_INPROJ
    assert seq % tm == 0 and n % tm == 0
    row = lambda i: (i, 0)
    shapes = [(n, A_Q_W, tm), (n, 4 * A_KV_W, tm)]
    for _, d in B_GROUPS:
        shapes += [(n // d, d * B_OUT_W, tm // d), (n // d, d * 2 * B_OUT_W, tm // d)]
    shapes += [(n, D_MODEL, tm)] * 2
    return pl.pallas_call(
        _inproj_kernel,
        out_shape=[jax.ShapeDtypeStruct((r, c), BF16) for r, c, _ in shapes],
        grid=(n // tm,),
        in_specs=[pl.BlockSpec((tm, D_MODEL), row),
                  pl.BlockSpec((1, N_MOD, D_MODEL), lambda i: (i * tm // seq, 0, 0)),
                  pl.BlockSpec((1, D_MODEL), lambda i: (0, 0)),
                  pl.BlockSpec((D_MODEL, D_IN), lambda i: (0, 0), pipeline_mode=pl.Buffered(1))],
        out_specs=[pl.BlockSpec((b, c), row) for _, c, b in shapes],
        scratch_shapes=[pltpu.VMEM((6 * B_OUT_W // LANES, tm, LANES), F32)],
        compiler_params=_cparams(1),
        name="inproj",
    )(x2, mod3, norm1.reshape(1, D_MODEL), w_in_bf)


def _rel_bucket_np(rel):
    half = N_BUCKETS // 2
    max_exact = half // 2
    n = np.abs(rel)
    large = max_exact + (np.log(np.maximum(n, 1) / max_exact) / math.log(MAX_DISTANCE / max_exact)
                         * (half - max_exact)).astype(np.int32)
    large = np.minimum(large, half - 1)
    return ((rel > 0).astype(np.int32) * half + np.where(n < max_exact, n, large)).astype(np.int32)


def _bias_table(rel_bias_heads, n_q, n_k, key_off, band, dist_scale):
    p = n_q + n_k
    rel = np.arange(p) - (n_q - 1) - key_off
    bucket = _rel_bucket_np(rel * dist_scale)
    t = jnp.where((np.abs(rel) <= band)[None], rel_bias_heads.astype(F32)[bucket].T, NEG_INF)
    big = jnp.tile(t, (1, n_q + 1))
    tab = big[:, n_q - 1:n_q - 1 + n_q * (p - 1)].reshape(-1, n_q, p - 1)
    return tab[:, :, :n_k]


def _pair_rhs(k_top, k_bot, v_top, v_bot):
    kb = k_top.shape[0]
    low = jnp.where(lax.broadcasted_iota(I32, (kb, LANES), 1) < HEAD_DIM, 1.0, 0.0).astype(BF16)
    high = jnp.where(lax.broadcasted_iota(I32, (kb, LANES), 1) < HEAD_DIM, 0.0, 1.0).astype(BF16)
    rhs_k = jnp.concatenate([k_top * low, k_bot * high], axis=0)
    rhs_v = jnp.concatenate([jnp.concatenate([v_top * low, low], axis=1),
                             jnp.concatenate([v_bot * high, high], axis=1)], axis=0)
    return rhs_k, rhs_v


def _pair_attention(q_pair, rhs_k, rhs_v, bias_pair, pen, sink_pair, want_lse):
    kb = rhs_k.shape[0] // 2
    s = lax.dot_general(q_pair, rhs_k, (((1,), (1,)), ((), ())), preferred_element_type=F32)
    s = s + bias_pair
    if pen is not None:
        s = s + pen
    s0, s1 = s[:, :kb], s[:, kb:]
    m0 = jnp.max(s0, axis=-1, keepdims=True)
    m1 = jnp.max(s1, axis=-1, keepdims=True)
    if sink_pair is not None:
        m0 = jnp.maximum(m0, sink_pair[:, 0:1])
        m1 = jnp.maximum(m1, sink_pair[:, HEAD_DIM:HEAD_DIM + 1])
    p = jnp.concatenate([jnp.exp(s0 - m0), jnp.exp(s1 - m1)], axis=1).astype(BF16)
    od = jnp.dot(p, rhs_v, preferred_element_type=F32)
    o, den = od[:, :LANES], od[:, LANES:]
    low = lax.broadcasted_iota(I32, o.shape, 1) < HEAD_DIM
    m_full = jnp.where(low, m0, m1)
    if sink_pair is not None:
        den = den + jnp.exp(sink_pair - m_full)
    return o / den, (m_full + jnp.log(den)) if want_lse else None


def _attn_kernel(*refs, halo, nblk, n_pairs, n_seqs, shared_kv, has_sink, want_lse, n_steps):
    it = iter(refs)
    q_ref = next(it)
    kv_prev, kv_cur, kv_next = next(it), next(it), next(it)
    bias_ref = next(it)
    sink_ref = next(it) if has_sink else None
    o_ref = next(it)
    lse_ref = next(it) if want_lse else None
    step = pl.program_id(2)
    blk = ATTN_BLOCK
    kb = blk + 2 * halo
    qw = n_pairs * LANES
    kv = jnp.concatenate([kv_prev[...], kv_cur[...], kv_next[...]], axis=0)
    col = lax.broadcasted_iota(I32, (1, kb), 1)
    for b in range(nblk):
        lo = jnp.where(step == 0, halo, 0) if b == 0 else 0
        hi = jnp.where(step == n_steps - 1, kb - halo, kb) if b == nblk - 1 else kb
        pen = None
        if b == 0 or b == nblk - 1:
            pen1 = jnp.where((col >= lo) & (col < hi), 0.0, NEG_INF)
            pen = jnp.concatenate([pen1, pen1], axis=1)
        rows = slice(b * blk, (b + 1) * blk)
        krows = slice(b * blk, b * blk + kb)
        lane_col = lambda j: kv[krows, j * LANES:(j + 1) * LANES]
        if shared_kv:
            k, k_sw, v, v_sw = (lane_col(j) for j in range(4))
            rhs = [_pair_rhs(k, k_sw, v, v_sw), _pair_rhs(k_sw, k, v_sw, v)]
        for s in range(n_seqs):
            for c in range(n_pairs):
                lanes = slice(s * qw + c * LANES, s * qw + (c + 1) * LANES)
                if shared_kv:
                    rhs_k, rhs_v = rhs[c // (n_pairs // 2)]
                else:
                    k, v = lane_col(2 * s * n_pairs + c), lane_col((2 * s + 1) * n_pairs + c)
                    rhs_k, rhs_v = _pair_rhs(k, k, v, v)
                o, lse = _pair_attention(q_ref[rows, lanes], rhs_k, rhs_v, bias_ref[c], pen,
                                         sink_ref[c] if has_sink else None, want_lse)
                o_ref[rows, lanes] = o.astype(o_ref.dtype)
                if want_lse:
                    lse_ref[rows, lanes] = lse


def _band_attention(q, kv, bias, sink, *, nbatch, seq, dil, halo, step_rows, shared_kv, want_lse):
    n = nbatch * seq
    sub_len = seq // dil
    qw = q.shape[1] // dil
    kv_width = kv.shape[1] // dil
    rows = min(step_rows, sub_len)
    n_seqs = min(dil, step_rows // rows)
    assert sub_len % rows == 0 and rows % ATTN_BLOCK == 0 and rows % halo == 0 and dil % n_seqs == 0
    nq = sub_len // rows
    per = rows // halo
    total_halos = n // dil // halo
    cur = lambda b, r, i: (b * nq + i, r)
    prev = lambda b, r, i: (jnp.maximum((b * nq + i) * per - 1, 0), r)
    nxt = lambda b, r, i: (jnp.minimum((b * nq + i + 1) * per, total_halos - 1), r)
    const = lambda a: pl.BlockSpec(a.shape, lambda b, r, i: (0,) * a.ndim)
    qw_step, kvw_step = qw * n_seqs, kv_width * n_seqs

    in_specs = [pl.BlockSpec((rows, qw_step), cur), pl.BlockSpec((halo, kvw_step), prev),
                pl.BlockSpec((rows, kvw_step), cur), pl.BlockSpec((halo, kvw_step), nxt),
                const(bias)]
    args = [q, kv, kv, kv, bias]
    if sink is not None:
        in_specs.append(const(sink))
        args.append(sink)
    out_shape = [jax.ShapeDtypeStruct(q.shape, BF16)]
    out_specs = [pl.BlockSpec((rows, qw_step), cur)]
    if want_lse:
        out_shape.append(jax.ShapeDtypeStruct(q.shape, F32))
        out_specs.append(pl.BlockSpec((rows, qw_step), cur))
    return pl.pallas_call(
        functools.partial(_attn_kernel, halo=halo, nblk=rows // ATTN_BLOCK, n_pairs=qw // LANES,
                          n_seqs=n_seqs, shared_kv=shared_kv, has_sink=sink is not None,
                          want_lse=want_lse, n_steps=nq),
        out_shape=out_shape,
        grid=(nbatch, dil // n_seqs, nq),
        in_specs=in_specs,
        out_specs=out_specs,
        compiler_params=_cparams(3),
        name=f"band_attn_d{dil}",
    )(*args)


def _pack_rows(y):
    return pltpu.pack_elementwise([y[:, :HALF_D], y[:, HALF_D:]], packed_dtype=BF16)


def _store_slabs(ref, packed):
    t = packed.shape[0]
    for c in range(ROW_SLABS):
        ref[pl.ds(c, t, stride=ROW_SLABS), :] = packed[:, c * LANES:(c + 1) * LANES]


def _load_slabs(ref):
    t = ref.shape[0] // ROW_SLABS
    return jnp.concatenate([ref[pl.ds(c, t, stride=ROW_SLABS), :] for c in range(ROW_SLABS)], axis=1)


def _unpack_rows(p):
    return tuple(pltpu.unpack_elementwise(p, index=i, packed_dtype=BF16, unpacked_dtype=F32)
                 for i in range(2))


def _route(sel, scores):
    t = sel.shape[-1]
    per = N_EXPERTS // N_EXPERT_GROUPS
    shape3 = (N_EXPERT_GROUPS, per, t)
    sel3 = sel.reshape(shape3)
    sc3 = scores.reshape(shape3)
    iota_g = lax.broadcasted_iota(I32, shape3, 0)
    iota_m = lax.broadcasted_iota(I32, shape3, 1)
    iota_e = iota_g * per + iota_m
    m1 = jnp.max(sel3, axis=1, keepdims=True)
    i1 = jnp.min(jnp.where(sel3 == m1, iota_m, per), axis=1, keepdims=True)
    m2 = jnp.max(jnp.where(iota_m == i1, REMOVED, sel3), axis=1, keepdims=True)
    gscore = m1 + m2
    iota_g1 = lax.broadcasted_iota(I32, gscore.shape, 0)
    gmask = jnp.zeros(gscore.shape, jnp.bool_)
    for _ in range(TOPK_GROUPS):
        mx = jnp.max(gscore, axis=0, keepdims=True)
        ix = jnp.min(jnp.where(gscore == mx, iota_g1, N_EXPERT_GROUPS), axis=0, keepdims=True)
        hit = iota_g1 == ix
        gmask = gmask | hit
        gscore = jnp.where(hit, REMOVED, gscore)
    cur = jnp.where(gmask, sel3, NEG_INF)
    sum_all = lambda a: jnp.sum(jnp.sum(a, axis=1, keepdims=True), axis=0, keepdims=True)
    idxs, wts, hits = [], [], []
    for _ in range(TOP_K):
        mx = jnp.max(jnp.max(cur, axis=1, keepdims=True), axis=0, keepdims=True)
        cand = jnp.where(cur == mx, iota_e, N_EXPERTS)
        ix = jnp.min(jnp.min(cand, axis=1, keepdims=True), axis=0, keepdims=True)
        hit = iota_e == ix
        wts.append(sum_all(jnp.where(hit, sc3, 0.0)).reshape(1, t))
        cur = jnp.where(hit, REMOVED, cur)
        idxs.append(ix.reshape(1, t))
        hits.append(hit)
    wsum = wts[0]
    for w in wts[1:]:
        wsum = wsum + w
    wts = [w / wsum * ROUTED_SCALE for w in wts]
    onehot = hits[0].astype(F32)
    for hit in hits[1:]:
        onehot = onehot + hit.astype(F32)
    tri = (lax.broadcasted_iota(I32, (t, t), 0) <= lax.broadcasted_iota(I32, (t, t), 1))
    cum = jnp.dot(onehot.reshape(N_EXPERTS, t).astype(BF16), tri.astype(F32).astype(BF16),
                  preferred_element_type=F32)
    cum3 = cum.reshape(shape3) - 1.0
    ranks = [sum_all(jnp.where(hit, cum3, 0.0)).reshape(1, t).astype(I32) for hit in hits]
    counts = cum[:, t - 1:t].astype(I32)
    return idxs, wts, ranks, counts


def _merge_kernel(x_ref, ya_ref, o1, o2, o3, l1, l2, l3, sga_ref, sgb_ref, mod_ref, n2_ref,
                  wpa, wpb, wo, wrt, rbias, wsg, wsu, wsd,
                  base_ref, hp_ref, ridx_ref, rw_ref, rank_ref, cnt_ref, scr):
    mod = mod_ref[0]
    tm = x_ref.shape[0]

    def token_major(ref, slot, d):
        if d == 1:
            return ref[...].astype(F32)
        nc = B_OUT_W // LANES
        for c in range(nc):
            for res in range(d):
                col = res * B_OUT_W + c * LANES
                scr[slot * nc + c, pl.ds(res, tm // d, stride=d), :] = ref[
                    :, col:col + LANES].astype(F32)
        return jnp.concatenate([scr[slot * nc + c] for c in range(nc)], axis=1)

    dils = [d for _, d in B_GROUPS]
    os_ = [token_major(r, i, d) for i, (r, d) in enumerate(zip((o1, o2, o3), dils))]
    ls = [token_major(r, 3 + i, d) for i, (r, d) in enumerate(zip((l1, l2, l3), dils))]
    mx = jnp.maximum(jnp.maximum(ls[0], ls[1]), ls[2])
    es = [jnp.exp(l - mx) for l in ls]
    den = es[0] + es[1] + es[2]
    ob = (es[0] / den) * os_[0] + (es[1] / den) * os_[1] + (es[2] / den) * os_[2]
    pa = jnp.dot(ya_ref[...], wpa[...], preferred_element_type=F32)
    pb = jnp.dot(ob.astype(BF16), wpb[...], preferred_element_type=F32)
    merged = sga_ref[...].astype(F32) * pa + sgb_ref[...].astype(F32) * pb
    mix = jnp.dot(merged.astype(BF16), wo[...], preferred_element_type=F32)
    x1 = x_ref[...] + mod[G1:G1 + 1] * mix
    h2 = _rms(x1, n2_ref[...]) * (1.0 + mod[SC2:SC2 + 1]) + mod[SH2:SH2 + 1]
    logits = lax.dot_general(wrt[...], h2, (((1,), (1,)), ((), ())), preferred_element_type=F32,
                             precision=lax.Precision.HIGHEST)
    scores = jax.nn.sigmoid(logits)
    idxs, wts, ranks, counts = _route(scores + rbias[...], scores)
    for k in range(TOP_K):
        ridx_ref[k:k + 1, :] = idxs[k]
        rw_ref[k:k + 1, :] = wts[k]
        rank_ref[k:k + 1, :] = ranks[k]
    ridx_ref[TOP_K:, :] = jnp.zeros((SUBLANES - TOP_K, tm), I32)
    rw_ref[TOP_K:, :] = jnp.zeros((SUBLANES - TOP_K, tm), F32)
    rank_ref[TOP_K:, :] = jnp.zeros((SUBLANES - TOP_K, tm), I32)
    cnt_ref[0] = counts
    hb = h2.astype(BF16)
    g = jnp.dot(hb, wsg[...], preferred_element_type=F32)
    u = jnp.dot(hb, wsu[...], preferred_element_type=F32)
    act = (g * jax.nn.sigmoid(g) * u).astype(BF16)
    shared = jnp.dot(act, wsd[...], preferred_element_type=F32)
    base_ref[...] = x1 + mod[G2:G2 + 1] * shared
    _store_slabs(hp_ref, _pack_rows(h2))


def _merge(x2, ya, outs, lses, sga, sgb, mod3, norm2, wpa, wpb, wo, wrt, rbias, wsg, wsu, wsd, seq):
    n = x2.shape[0]
    tm = TM_MERGE
    assert seq % tm == 0
    row = lambda i: (i, 0)
    full = lambda a: pl.BlockSpec(a.shape, lambda i: (0,) * a.ndim)
    weights = [wpa, wpb, wo, wrt, rbias, wsg, wsu, wsd]
    group_specs = [pl.BlockSpec((tm // d, d * B_OUT_W), row) for _, d in B_GROUPS]
    lanes = lambda i: (0, i)
    return pl.pallas_call(
        _merge_kernel,
        out_shape=[jax.ShapeDtypeStruct((n, D_MODEL), F32),
                   jax.ShapeDtypeStruct((n * ROW_SLABS, LANES), U32),
                   jax.ShapeDtypeStruct((SUBLANES, n), I32),
                   jax.ShapeDtypeStruct((SUBLANES, n), F32),
                   jax.ShapeDtypeStruct((SUBLANES, n), I32),
                   jax.ShapeDtypeStruct((n // tm, N_EXPERTS, 1), I32)],
        grid=(n // tm,),
        in_specs=[pl.BlockSpec((tm, D_MODEL), row), pl.BlockSpec((tm, A_Q_W), row)]
                 + group_specs * 2
                 + [pl.BlockSpec((tm, D_MODEL), row)] * 2
                 + [pl.BlockSpec((1, N_MOD, D_MODEL), lambda i: (i * tm // seq, 0, 0)),
                    pl.BlockSpec((1, D_MODEL), lambda i: (0, 0))]
                 + [full(w) for w in weights],
        out_specs=[pl.BlockSpec((tm, D_MODEL), row),
                   pl.BlockSpec((tm * ROW_SLABS, LANES), row),
                   pl.BlockSpec((SUBLANES, tm), lanes), pl.BlockSpec((SUBLANES, tm), lanes),
                   pl.BlockSpec((SUBLANES, tm), lanes),
                   pl.BlockSpec((1, N_EXPERTS, 1), lambda i: (i, 0, 0))],
        scratch_shapes=[pltpu.VMEM((6 * B_OUT_W // LANES, tm, LANES), F32)],
        compiler_params=_cparams(1),
        name="merge_route",
    )(x2, ya, *outs, *lses, sga, sgb, mod3, norm2.reshape(1, D_MODEL), *weights)


def _plan_kernel(ridx_ref, rank_ref, tb_ref, dest_ref, *, tm):
    iota_e = lax.broadcasted_iota(I32, (N_EXPERTS, tm), 0)
    for j in range(dest_ref.shape[0]):
        tb = tb_ref[j]
        cols = slice(j * tm, (j + 1) * tm)
        for k in range(TOP_K):
            base = jnp.sum(jnp.where(iota_e == ridx_ref[k:k + 1, cols], tb, 0), axis=0, keepdims=True)
            dest_ref[j, :, k * tm:(k + 1) * tm] = base + rank_ref[k:k + 1, cols]


def _plan(ridx, rank, tile_base, tm):
    n = ridx.shape[1]
    tiles = math.gcd(PLAN_TILES, n // tm)
    lanes = lambda i: (0, i)
    return pl.pallas_call(
        functools.partial(_plan_kernel, tm=tm),
        out_shape=jax.ShapeDtypeStruct((n // tm, 1, TOP_K * tm), I32),
        grid=(n // tm // tiles,),
        in_specs=[pl.BlockSpec((SUBLANES, tiles * tm), lanes), pl.BlockSpec((SUBLANES, tiles * tm), lanes),
                  pl.BlockSpec((tiles, N_EXPERTS, 1), lambda i: (i, 0, 0))],
        out_specs=pl.BlockSpec((tiles, 1, TOP_K * tm), lambda i: (i, 0, 0)),
        compiler_params=_cparams(1),
        name="moe_plan",
    )(ridx, rank, tile_base)


def _sc_mesh():
    return plsc.VectorSubcoreMesh(core_axis_name="core", subcore_axis_name="subcore",
                                  num_cores=SC_CORES, num_subcores=SC_SUBCORES)


def _sc_worker():
    return lax.axis_index("subcore") * SC_CORES + lax.axis_index("core")


def _sc_dispatch(hp, dest_rows, n_rows):
    n = hp.shape[0]
    halves = TM_MERGE // SC_CHUNK
    workers = SC_CORES * SC_SUBCORES
    tiles_per_w = n // TM_MERGE // workers
    idx_per_w = tiles_per_w * TOP_K * halves
    assert tiles_per_w * workers * TM_MERGE == n

    @functools.partial(
        pl.kernel, mesh=_sc_mesh(),
        out_type=jax.ShapeDtypeStruct((n_rows, ROW_SLABS, LANES), U32),
        scratch_types=[pltpu.VMEM((idx_per_w, SC_CHUNK), I32),
                       pltpu.VMEM((SC_CHUNK, ROW_SLABS, LANES), U32),
                       pltpu.SemaphoreType.DMA],
        name="moe_dispatch_sc")
    def scatter(hp_hbm, idx_hbm, xs_hbm, idx_v, rows_v, sem):
        wid = _sc_worker()
        pltpu.sync_copy(idx_hbm.at[pl.ds(wid * idx_per_w, idx_per_w)], idx_v)

        @pl.loop(0, tiles_per_w * halves)
        def _(j):
            tile = j // halves
            h = j - tile * halves
            tok = (wid * tiles_per_w + tile) * TM_MERGE + h * SC_CHUNK
            pltpu.sync_copy(hp_hbm.at[pl.ds(tok, SC_CHUNK)], rows_v)
            for k in range(TOP_K):
                row = (tile * TOP_K + k) * halves + h
                pltpu.async_copy(rows_v, xs_hbm.at[idx_v.at[row]], sem).wait()

    return scatter(hp, dest_rows)


def _sc_collect(ys, dest_rows):
    n_chunks = dest_rows.shape[0]
    workers = SC_CORES * SC_SUBCORES
    per_w = n_chunks // workers
    assert per_w * workers == n_chunks

    @functools.partial(
        pl.kernel, mesh=_sc_mesh(),
        out_type=jax.ShapeDtypeStruct((n_chunks * SC_CHUNK, ROW_SLABS, LANES), U32),
        scratch_types=[pltpu.VMEM((per_w, SC_CHUNK), I32),
                       pltpu.VMEM((SC_CHUNK, ROW_SLABS, LANES), U32),
                       pltpu.SemaphoreType.DMA],
        name="moe_collect_sc")
    def gather(ys_hbm, idx_hbm, out_hbm, idx_v, rows_v, sem):
        wid = _sc_worker()
        base = wid * per_w
        pltpu.sync_copy(idx_hbm.at[pl.ds(base, per_w)], idx_v)

        @pl.loop(0, per_w)
        def _(j):
            pltpu.async_copy(ys_hbm.at[idx_v.at[j]], rows_v, sem).wait()
            pltpu.sync_copy(rows_v, out_hbm.at[pl.ds((base + j) * SC_CHUNK, SC_CHUNK)])

    return gather(ys, dest_rows)


def _expert_kernel(blk_e_ref, valid_ref, src_ref, xs_ref, wg_ref, wu_ref, wd_ref, ys_ref,
                   wg_bf, wu_bf, wd_bf):
    del src_ref
    step = pl.program_id(0)
    valid = valid_ref[step]

    @pl.when((step == 0) | (blk_e_ref[step] != blk_e_ref[jnp.maximum(step - 1, 0)]))
    def _():
        wg_bf[...] = wg_ref[0].astype(BF16)
        wu_bf[...] = wu_ref[0].astype(BF16)
        wd_bf[...] = wd_ref[0].astype(BF16)

    @pl.when(valid > 0)
    def _():
        rows = lax.broadcasted_iota(I32, (xs_ref.shape[0] // ROW_SLABS, 1), 0)
        lo, hi = _unpack_rows(jnp.where(rows < valid, _load_slabs(xs_ref), jnp.uint32(0)))
        lo, hi = lo.astype(BF16), hi.astype(BF16)

        def up(w_bf):
            return jnp.dot(lo, w_bf[:HALF_D], preferred_element_type=F32) \
                + jnp.dot(hi, w_bf[HALF_D:], preferred_element_type=F32)

        g = up(wg_bf)
        act = (g * jax.nn.sigmoid(g) * up(wu_bf)).astype(BF16)
        _store_slabs(ys_ref, _pack_rows(jnp.dot(act, wd_bf[...], preferred_element_type=F32)))


def _experts(xs, blk_e, blk_valid, blk_src, wg, wu, wd):
    n_rows = xs.shape[0] // ROW_SLABS
    bm = MOE_BM
    nblk = n_rows // bm
    return pl.pallas_call(
        _expert_kernel,
        out_shape=jax.ShapeDtypeStruct((n_rows * ROW_SLABS, LANES), U32),
        grid_spec=pltpu.PrefetchScalarGridSpec(
            num_scalar_prefetch=3,
            grid=(nblk,),
            in_specs=[pl.BlockSpec((bm * ROW_SLABS, LANES), lambda i, be, bv, bs: (bs[i], 0)),
                      pl.BlockSpec((1, D_MODEL, D_EXPERT), lambda i, be, bv, bs: (be[i], 0, 0)),
                      pl.BlockSpec((1, D_MODEL, D_EXPERT), lambda i, be, bv, bs: (be[i], 0, 0)),
                      pl.BlockSpec((1, D_EXPERT, D_MODEL), lambda i, be, bv, bs: (be[i], 0, 0))],
            out_specs=pl.BlockSpec((bm * ROW_SLABS, LANES), lambda i, be, bv, bs: (bs[i], 0)),
            scratch_shapes=[pltpu.VMEM((D_MODEL, D_EXPERT), BF16), pltpu.VMEM((D_MODEL, D_EXPERT), BF16),
                            pltpu.VMEM((D_EXPERT, D_MODEL), BF16)]),
        compiler_params=_cparams(1),
        name="moe_experts",
    )(blk_e, blk_valid, blk_src, xs, wg, wu, wd)


def _combine_kernel(g_ref, base_ref, rw_ref, mod_ref, fn_ref, *rest):
    y_ref = rest[-1]
    tm = base_ref.shape[0]
    rw = rw_ref[...]
    acc_lo = jnp.zeros((tm, HALF_D), F32)
    acc_hi = jnp.zeros((tm, HALF_D), F32)
    for k in range(TOP_K):
        lo, hi = _unpack_rows(_load_slabs(g_ref.at[0, k]))
        w = rw[:, k:k + 1]
        acc_lo = acc_lo + w * lo
        acc_hi = acc_hi + w * hi
    routed = jnp.concatenate([acc_lo, acc_hi], axis=1)
    x2 = base_ref[...] + mod_ref[0][G2:G2 + 1] * routed
    y_ref[...] = _rms(x2, fn_ref[...])


def _combine(g, base, rw_t, mod3, final_norm, seq, part, y_prev):
    n = base.shape[0]
    tm = TM_COMBINE
    assert seq % tm == 0
    g4 = g.reshape(-1, TOP_K, tm * ROW_SLABS, LANES)
    tiles = g4.shape[0]
    off = part * tiles
    row = lambda i: (i + off, 0)
    in_specs = [pl.BlockSpec((1, TOP_K, tm * ROW_SLABS, LANES), lambda i: (i, 0, 0, 0)),
                pl.BlockSpec((tm, D_MODEL), row),
                pl.BlockSpec((tm, SUBLANES), row),
                pl.BlockSpec((1, N_MOD, D_MODEL), lambda i: ((i + off) * tm // seq, 0, 0)),
                pl.BlockSpec((1, D_MODEL), lambda i: (0, 0))]
    args = [g4, base, rw_t, mod3, final_norm.reshape(1, D_MODEL)]
    aliases = {}
    if y_prev is not None:
        in_specs.append(pl.BlockSpec(memory_space=pl.ANY))
        args.append(y_prev)
        aliases = {len(args) - 1: 0}
    return pl.pallas_call(
        _combine_kernel,
        out_shape=jax.ShapeDtypeStruct((n, D_MODEL), F32),
        grid=(tiles,),
        in_specs=in_specs,
        out_specs=pl.BlockSpec((tm, D_MODEL), row),
        input_output_aliases=aliases,
        compiler_params=_cparams(1),
        name="moe_combine",
    )(*args)


def _block_layout(tile_counts, n):
    bm = MOE_BM
    c = tile_counts[:, :, 0]
    ntiles = c.shape[0]
    counts = jnp.sum(c, axis=0)
    padded = (counts + bm - 1) // bm * bm
    earlier_e = np.tri(N_EXPERTS, k=-1, dtype=bool)
    pstart = jnp.sum(jnp.where(earlier_e, padded[None, :], 0), axis=1)
    pend = pstart + padded
    earlier_t = np.tri(ntiles, k=-1, dtype=bool)
    tile_base = pstart[None, :] + jnp.sum(jnp.where(earlier_t[:, :, None], c[None], 0), axis=1)
    nblk = -(-n * TOP_K // bm) + N_EXPERTS
    blk = jnp.arange(nblk, dtype=I32)
    blk_e = jnp.minimum(jnp.sum((pend[None, :] <= blk[:, None] * bm).astype(I32), axis=1),
                        N_EXPERTS - 1)
    onehot_e = blk_e[:, None] == jnp.arange(N_EXPERTS, dtype=I32)[None, :]
    end_e = jnp.sum(jnp.where(onehot_e, (pstart + counts)[None, :], 0), axis=1)
    blk_valid = jnp.clip(end_e - blk * bm, 0, bm).astype(I32)
    blk_src = jnp.minimum(blk, pend[N_EXPERTS - 1] // bm - 1).astype(I32)
    last_e = jnp.max(jnp.where(counts > 0, jnp.arange(N_EXPERTS, dtype=I32), 0))
    blk_e = jnp.where(blk == blk_src, blk_e, last_e).astype(I32)
    return tile_base.astype(I32)[:, :, None], blk_e, blk_valid, blk_src, nblk * bm


def _front(x, mod3, p, after=None):
    nbatch, seq, _ = x.shape
    n = nbatch * seq
    x2 = x.reshape(n, D_MODEL)
    proj = _inproj(x2, mod3, p["norm1"], p["w_in"], seq)
    qa, kva = proj[:2]
    sga, sgb = proj[8:]
    (ya,) = _band_attention(qa, kva, p["bias_a"], p["sink_a"], nbatch=nbatch, seq=seq, dil=1,
                            halo=A_HALF_WINDOW, step_rows=ATTN_ROWS_A, shared_kv=True, want_lse=False)
    outs, lses = [], []
    for gi, (w, d) in enumerate(B_GROUPS):
        q, kv = proj[2 + 2 * gi:4 + 2 * gi]
        o, lse = _band_attention(q, kv, p["bias_b"][gi], None, nbatch=nbatch, seq=seq, dil=d,
                                 halo=w // (2 * d), step_rows=ATTN_ROWS_B, shared_kv=False,
                                 want_lse=True)
        outs.append(o)
        lses.append(lse)
    if after is not None:
        ya, after = lax.optimization_barrier((ya, after))
    base, hp, ridx, rw, rank, tile_counts = _merge(
        x2, ya, outs, lses, sga, sgb, mod3, p["norm2"], p["w_pa"], p["w_pb"], p["w_o"], p["w_rt"],
        p["rbias"], p["ws_gate"], p["ws_up"], p["ws_down"], seq)
    assert TM_MERGE == TM_COMBINE
    tile_base, blk_e, blk_valid, blk_src, n_rows = _block_layout(tile_counts, n)
    dest_rows = _plan(ridx, rank, tile_base, TM_MERGE).reshape(-1, SC_CHUNK)
    return dict(base=base, hp=hp, rw=rw, dest_rows=dest_rows, blk_e=blk_e, blk_valid=blk_valid,
                blk_src=blk_src, n_rows=n_rows, shape=x.shape, after=after)


def _slabs(a):
    return a.reshape(-1, ROW_SLABS, LANES)


def _moe(st, xs, mod3, p):
    nbatch, seq, _ = st["shape"]
    dest_rows = st["dest_rows"]
    ys = _experts(xs.reshape(-1, LANES), st["blk_e"], st["blk_valid"], st["blk_src"], p["w_gate"], p["w_up"],
                  p["w_down"])
    rows_per_part = dest_rows.shape[0] // COLLECT_PARTS
    rw_t = st["rw"].T
    y = None
    for part in range(COLLECT_PARTS):
        g = _sc_collect(_slabs(ys), dest_rows[part * rows_per_part:(part + 1) * rows_per_part])
        y = _combine(g, st["base"], rw_t, mod3, p["final_norm"], seq, part, y)
    return y.reshape(nbatch, seq, D_MODEL)


def kernel(x_prompt, x_sample, c_prompt, c_sample, rel_bias, w_ada, b_ada, norm1, w_in, sink, w_pa, w_pb, w_o, norm2, w_router, router_bias, w_gate, w_up, w_down, ws_gate, ws_up, ws_down, final_norm):
    assert w_ada.shape[0] == 1
    nbp = x_prompt.shape[0]
    mod = _ada(jnp.concatenate([c_prompt, c_sample], axis=0), w_ada[0], b_ada[0])
    mod3 = mod.reshape(-1, N_MOD, D_MODEL)

    def pair_bias(heads, halo, dist_scale):
        kb = ATTN_BLOCK + 2 * halo
        tab = _bias_table(heads, ATTN_BLOCK, kb, halo, halo, dist_scale)
        return tab.reshape(-1, 2, ATTN_BLOCK, kb).transpose(0, 2, 1, 3).reshape(-1, ATTN_BLOCK, 2 * kb)

    bias_a = pair_bias(rel_bias[:, :A_Q_HEADS], A_HALF_WINDOW, 1)
    bias_b = []
    for gi, (w, d) in enumerate(B_GROUPS):
        h0 = A_Q_HEADS + gi * B_HEADS_PER_GROUP
        bias_b.append(pair_bias(rel_bias[:, h0:h0 + B_HEADS_PER_GROUP], w // (2 * d), d))
    p = {
        "norm1": norm1[0], "norm2": norm2[0], "final_norm": final_norm,
        "w_in": w_in[0].astype(BF16),
        "bias_a": bias_a,
        "sink_a": jnp.repeat(sink[0].astype(F32), HEAD_DIM).reshape(A_Q_HEADS // 2, 1, LANES),
        "bias_b": bias_b,
        "w_pa": w_pa[0].astype(BF16), "w_pb": w_pb[0].astype(BF16), "w_o": w_o[0].astype(BF16),
        "w_rt": w_router[0].T, "rbias": router_bias[0].reshape(N_EXPERTS, 1),
        "ws_gate": ws_gate[0].astype(BF16), "ws_up": ws_up[0].astype(BF16),
        "ws_down": ws_down[0].astype(BF16),
        "w_gate": w_gate[0], "w_up": w_up[0], "w_down": w_down[0],
    }
    dispatch = lambda st: _sc_dispatch(_slabs(st["hp"]), st["dest_rows"], st["n_rows"])
    st_p = _front(x_prompt, mod3[:nbp], p)
    st_s = _front(x_sample, mod3[nbp:], p, after=dispatch(st_p))
    return (_moe(st_p, st_s["after"], mod3[:nbp], p), _moe(st_s, dispatch(st_s), mod3[nbp:], p))
```

```python
import functools
import math

import jax
import jax.numpy as jnp
import numpy as np
from jax import lax
from jax.experimental import pallas as pl
from jax.experimental.pallas import tpu as pltpu
from jax.experimental.pallas import tpu_sc as plsc

F32 = jnp.float32
BF16 = jnp.bfloat16
U32 = jnp.uint32
I32 = jnp.int32

D_MODEL = 1024
HEAD_DIM = 64
A_Q_HEADS = 8
A_KV_HEADS = 2
A_HALF_WINDOW = 128
B_GROUPS = ((128, 1), (512, 4), (2048, 16))
B_HEADS_PER_GROUP = 4
N_BUCKETS = 32
MAX_DISTANCE = 1024
N_EXPERTS = 64
TOP_K = 6
N_EXPERT_GROUPS = 8
TOPK_GROUPS = 4
D_EXPERT = 256
ROUTED_SCALE = 2.5
RMS_EPS = 1e-6
NEG_INF = -1e30
N_MOD = 6
SH1, SC1, G1, SH2, SC2, G2 = range(N_MOD)
REMOVED = -3e38

A_Q_W = A_Q_HEADS * HEAD_DIM
A_KV_W = A_KV_HEADS * HEAD_DIM
B_W = len(B_GROUPS) * B_HEADS_PER_GROUP * HEAD_DIM
B_OUT_W = B_HEADS_PER_GROUP * HEAD_DIM
D_IN = A_Q_W + 2 * A_KV_W + 3 * B_W + 2 * D_MODEL
HALF_D = D_MODEL // 2
LANES = 128
SUBLANES = 8
ROW_SLABS = HALF_D // LANES
SC_CORES = 2
SC_SUBCORES = 16
SC_CHUNK = 128

TM_INPROJ = 1024
TM_MERGE = 512
ATTN_BLOCK = 128
ATTN_ROWS_A = 512
ATTN_ROWS_B = 2048
MOE_BM = 1024
EXPERT_SUBBLOCKS = 2
TM_COMBINE = 512
COLLECT_PARTS = 2
PLAN_TILES = 8
VMEM_LIMIT = 48 * 1024 * 1024
VMEM_LIMIT_INPROJ = 60 * 1024 * 1024


def _cparams(n_axes, vmem_limit=VMEM_LIMIT):
    return pltpu.CompilerParams(
        dimension_semantics=("arbitrary",) * n_axes, vmem_limit_bytes=vmem_limit)


def _ada_kernel(c_ref, w_ref, b_ref, o_ref):
    c = c_ref[...]
    s = c * jax.nn.sigmoid(c)
    o_ref[...] = jnp.dot(s, w_ref[...], preferred_element_type=F32,
                         precision=lax.Precision.HIGHEST) + b_ref[...]


def _ada(c_all, w_ada, b_ada):
    nb = c_all.shape[0]
    return pl.pallas_call(
        _ada_kernel,
        out_shape=jax.ShapeDtypeStruct((nb, N_MOD * D_MODEL), F32),
        grid=(N_MOD,),
        in_specs=[pl.BlockSpec((nb, D_MODEL), lambda j: (0, 0)),
                  pl.BlockSpec((D_MODEL, D_MODEL), lambda j: (0, j)),
                  pl.BlockSpec((1, D_MODEL), lambda j: (0, j))],
        out_specs=pl.BlockSpec((nb, D_MODEL), lambda j: (0, j)),
        compiler_params=_cparams(1),
        name="ada",
    )(c_all, w_ada, b_ada.reshape(1, N_MOD * D_MODEL))


def _rms(x, g):
    return x * lax.rsqrt(jnp.mean(x * x, axis=-1, keepdims=True) + RMS_EPS) * g


def _inproj_kernel(x_ref, mod_ref, n1_ref, w_ref, qa, kva, *rest):
    q_b = rest[0:6:2]
    kv_b = rest[1:6:2]
    sga, sgb, scr = rest[6:]
    mod = mod_ref[0]
    h = _rms(x_ref[...], n1_ref[...]) * (1.0 + mod[SC1:SC1 + 1]) + mod[SH1:SH1 + 1]
    hb = h.astype(BF16)
    tm = hb.shape[0]
    scale = HEAD_DIM ** -0.5

    def proj(off, width):
        return jnp.dot(hb, w_ref[:, off:off + width], preferred_element_type=F32)

    qa[...] = (proj(0, A_Q_W) * scale).astype(BF16)
    for j, off in enumerate((A_Q_W, A_Q_W + A_KV_W)):
        r = proj(off, A_KV_W)
        kva[:, 2 * j * A_KV_W:(2 * j + 1) * A_KV_W] = r.astype(BF16)
        kva[:, (2 * j + 1) * A_KV_W:(2 * j + 2) * A_KV_W] = pltpu.roll(r, HEAD_DIM, 1).astype(BF16)
    off = A_Q_W + 2 * A_KV_W
    for t in range(3):
        for gi, (_, d) in enumerate(B_GROUPS):
            r = proj(off + t * B_W + gi * B_OUT_W, B_OUT_W)
            if t == 0:
                r = r * scale
            ref, width, base = (q_b[gi], B_OUT_W, 0) if t == 0 else (kv_b[gi], 2 * B_OUT_W,
                                                                     (t - 1) * B_OUT_W)
            if d == 1:
                ref[:, base:base + B_OUT_W] = r.astype(BF16)
            else:
                for c in range(B_OUT_W // LANES):
                    slot = (t * 2 + gi - 1) * (B_OUT_W // LANES) + c
                    scr[slot] = r[:, c * LANES:(c + 1) * LANES]
                    for res in range(d):
                        col = res * width + base + c * LANES
                        ref[:, col:col + LANES] = scr[
                            slot, pl.ds(res, tm // d, stride=d), :].astype(BF16)
    off += 3 * B_W
    sga[...] = jax.nn.sigmoid(proj(off, D_MODEL)).astype(BF16)
    sgb[...] = jax.nn.sigmoid(proj(off + D_MODEL, D_MODEL)).astype(BF16)


def _inproj(x2, mod3, norm1, w_in_bf, seq):
    n = x2.shape[0]
    tm = TM_INPROJ
    assert seq % tm == 0 and n % tm == 0
    row = lambda i: (i, 0)
    shapes = [(n, A_Q_W, tm), (n, 4 * A_KV_W, tm)]
    for _, d in B_GROUPS:
        shapes += [(n // d, d * B_OUT_W, tm // d), (n // d, d * 2 * B_OUT_W, tm // d)]
    shapes += [(n, D_MODEL, tm)] * 2
    return pl.pallas_call(
        _inproj_kernel,
        out_shape=[jax.ShapeDtypeStruct((r, c), BF16) for r, c, _ in shapes],
        grid=(n // tm,),
        in_specs=[pl.BlockSpec((tm, D_MODEL), row),
                  pl.BlockSpec((1, N_MOD, D_MODEL), lambda i: (i * tm // seq, 0, 0)),
                  pl.BlockSpec((1, D_MODEL), lambda i: (0, 0)),
                  pl.BlockSpec((D_MODEL, D_IN), lambda i: (0, 0), pipeline_mode=pl.Buffered(1))],
        out_specs=[pl.BlockSpec((b, c), row) for _, c, b in shapes],
        scratch_shapes=[pltpu.VMEM((6 * B_OUT_W // LANES, tm, LANES), F32)],
        compiler_params=_cparams(1, VMEM_LIMIT_INPROJ),
        name="inproj",
    )(x2, mod3, norm1.reshape(1, D_MODEL), w_in_bf)


def _rel_bucket_np(rel):
    half = N_BUCKETS // 2
    max_exact = half // 2
    n = np.abs(rel)
    large = max_exact + (np.log(np.maximum(n, 1) / max_exact) / math.log(MAX_DISTANCE / max_exact)
                         * (half - max_exact)).astype(np.int32)
    large = np.minimum(large, half - 1)
    return ((rel > 0).astype(np.int32) * half + np.where(n < max_exact, n, large)).astype(np.int32)


def _bias_table(rel_bias_heads, n_q, n_k, key_off, band, dist_scale):
    p = n_q + n_k
    rel = np.arange(p) - (n_q - 1) - key_off
    bucket = _rel_bucket_np(rel * dist_scale)
    t = jnp.where((np.abs(rel) <= band)[None], rel_bias_heads.astype(F32)[bucket].T, NEG_INF)
    big = jnp.tile(t, (1, n_q + 1))
    tab = big[:, n_q - 1:n_q - 1 + n_q * (p - 1)].reshape(-1, n_q, p - 1)
    return tab[:, :, :n_k]


def _pair_rhs(k_top, k_bot, v_top, v_bot):
    kb = k_top.shape[0]
    low = jnp.where(lax.broadcasted_iota(I32, (kb, LANES), 1) < HEAD_DIM, 1.0, 0.0).astype(BF16)
    high = jnp.where(lax.broadcasted_iota(I32, (kb, LANES), 1) < HEAD_DIM, 0.0, 1.0).astype(BF16)
    rhs_k = jnp.concatenate([k_top * low, k_bot * high], axis=0)
    rhs_v = jnp.concatenate([jnp.concatenate([v_top * low, low], axis=1),
                             jnp.concatenate([v_bot * high, high], axis=1)], axis=0)
    return rhs_k, rhs_v


def _pair_attention(q_pair, rhs_k, rhs_v, bias_pair, pen, sink_pair, want_lse):
    kb = rhs_k.shape[0] // 2
    s = lax.dot_general(q_pair, rhs_k, (((1,), (1,)), ((), ())), preferred_element_type=F32)
    s = s + bias_pair
    if pen is not None:
        s = s + pen
    s0, s1 = s[:, :kb], s[:, kb:]
    m0 = jnp.max(s0, axis=-1, keepdims=True)
    m1 = jnp.max(s1, axis=-1, keepdims=True)
    if sink_pair is not None:
        m0 = jnp.maximum(m0, sink_pair[:, 0:1])
        m1 = jnp.maximum(m1, sink_pair[:, HEAD_DIM:HEAD_DIM + 1])
    p = jnp.concatenate([jnp.exp(s0 - m0), jnp.exp(s1 - m1)], axis=1).astype(BF16)
    od = jnp.dot(p, rhs_v, preferred_element_type=F32)
    o, den = od[:, :LANES], od[:, LANES:]
    low = lax.broadcasted_iota(I32, o.shape, 1) < HEAD_DIM
    m_full = jnp.where(low, m0, m1)
    if sink_pair is not None:
        den = den + jnp.exp(sink_pair - m_full)
    return o / den, (m_full + jnp.log(den)) if want_lse else None


def _attn_kernel(*refs, halo, nblk, n_pairs, n_seqs, shared_kv, has_sink, want_lse, n_steps):
    it = iter(refs)
    q_ref = next(it)
    kv_prev, kv_cur, kv_next = next(it), next(it), next(it)
    bias_ref = next(it)
    sink_ref = next(it) if has_sink else None
    o_ref = next(it)
    lse_ref = next(it) if want_lse else None
    step = pl.program_id(2)
    blk = ATTN_BLOCK
    kb = blk + 2 * halo
    qw = n_pairs * LANES
    kv = jnp.concatenate([kv_prev[...], kv_cur[...], kv_next[...]], axis=0)
    col = lax.broadcasted_iota(I32, (1, kb), 1)
    for b in range(nblk):
        lo = jnp.where(step == 0, halo, 0) if b == 0 else 0
        hi = jnp.where(step == n_steps - 1, kb - halo, kb) if b == nblk - 1 else kb
        pen = None
        if b == 0 or b == nblk - 1:
            pen1 = jnp.where((col >= lo) & (col < hi), 0.0, NEG_INF)
            pen = jnp.concatenate([pen1, pen1], axis=1)
        rows = slice(b * blk, (b + 1) * blk)
        krows = slice(b * blk, b * blk + kb)
        lane_col = lambda j: kv[krows, j * LANES:(j + 1) * LANES]
        if shared_kv:
            k, k_sw, v, v_sw = (lane_col(j) for j in range(4))
            rhs = [_pair_rhs(k, k_sw, v, v_sw), _pair_rhs(k_sw, k, v_sw, v)]
        for s in range(n_seqs):
            for c in range(n_pairs):
                lanes = slice(s * qw + c * LANES, s * qw + (c + 1) * LANES)
                if shared_kv:
                    rhs_k, rhs_v = rhs[c // (n_pairs // 2)]
                else:
                    k, v = lane_col(2 * s * n_pairs + c), lane_col((2 * s + 1) * n_pairs + c)
                    rhs_k, rhs_v = _pair_rhs(k, k, v, v)
                o, lse = _pair_attention(q_ref[rows, lanes], rhs_k, rhs_v, bias_ref[c], pen,
                                         sink_ref[c] if has_sink else None, want_lse)
                o_ref[rows, lanes] = o.astype(o_ref.dtype)
                if want_lse:
                    lse_ref[rows, lanes] = lse


def _band_attention(q, kv, bias, sink, *, nbatch, seq, dil, halo, step_rows, shared_kv, want_lse):
    n = nbatch * seq
    sub_len = seq // dil
    qw = q.shape[1] // dil
    kv_width = kv.shape[1] // dil
    rows = min(step_rows, sub_len)
    n_seqs = min(dil, step_rows // rows)
    assert sub_len % rows == 0 and rows % ATTN_BLOCK == 0 and rows % halo == 0 and dil % n_seqs == 0
    nq = sub_len // rows
    per = rows // halo
    total_halos = n // dil // halo
    cur = lambda b, r, i: (b * nq + i, r)
    prev = lambda b, r, i: (jnp.maximum((b * nq + i) * per - 1, 0), r)
    nxt = lambda b, r, i: (jnp.minimum((b * nq + i + 1) * per, total_halos - 1), r)
    const = lambda a: pl.BlockSpec(a.shape, lambda b, r, i: (0,) * a.ndim)
    qw_step, kvw_step = qw * n_seqs, kv_width * n_seqs

    in_specs = [pl.BlockSpec((rows, qw_step), cur), pl.BlockSpec((halo, kvw_step), prev),
                pl.BlockSpec((rows, kvw_step), cur), pl.BlockSpec((halo, kvw_step), nxt),
                const(bias)]
    args = [q, kv, kv, kv, bias]
    if sink is not None:
        in_specs.append(const(sink))
        args.append(sink)
    out_shape = [jax.ShapeDtypeStruct(q.shape, BF16)]
    out_specs = [pl.BlockSpec((rows, qw_step), cur)]
    if want_lse:
        out_shape.append(jax.ShapeDtypeStruct(q.shape, F32))
        out_specs.append(pl.BlockSpec((rows, qw_step), cur))
    return pl.pallas_call(
        functools.partial(_attn_kernel, halo=halo, nblk=rows // ATTN_BLOCK, n_pairs=qw // LANES,
                          n_seqs=n_seqs, shared_kv=shared_kv, has_sink=sink is not None,
                          want_lse=want_lse, n_steps=nq),
        out_shape=out_shape,
        grid=(nbatch, dil // n_seqs, nq),
        in_specs=in_specs,
        out_specs=out_specs,
        compiler_params=_cparams(3),
        name=f"band_attn_d{dil}",
    )(*args)


def _pack_rows(y):
    return pltpu.pack_elementwise([y[:, :HALF_D], y[:, HALF_D:]], packed_dtype=BF16)


def _store_slabs(ref, packed):
    t = packed.shape[0]
    for c in range(ROW_SLABS):
        ref[pl.ds(c, t, stride=ROW_SLABS), :] = packed[:, c * LANES:(c + 1) * LANES]


def _load_slabs(ref):
    t = ref.shape[0] // ROW_SLABS
    return jnp.concatenate([ref[pl.ds(c, t, stride=ROW_SLABS), :] for c in range(ROW_SLABS)], axis=1)


def _unpack_rows(p):
    return tuple(pltpu.unpack_elementwise(p, index=i, packed_dtype=BF16, unpacked_dtype=F32)
                 for i in range(2))


def _route(sel, scores):
    t = sel.shape[-1]
    per = N_EXPERTS // N_EXPERT_GROUPS
    shape3 = (N_EXPERT_GROUPS, per, t)
    sel3 = sel.reshape(shape3)
    sc3 = scores.reshape(shape3)
    iota_g = lax.broadcasted_iota(I32, shape3, 0)
    iota_m = lax.broadcasted_iota(I32, shape3, 1)
    iota_e = iota_g * per + iota_m
    m1 = jnp.max(sel3, axis=1, keepdims=True)
    i1 = jnp.min(jnp.where(sel3 == m1, iota_m, per), axis=1, keepdims=True)
    m2 = jnp.max(jnp.where(iota_m == i1, REMOVED, sel3), axis=1, keepdims=True)
    gscore = m1 + m2
    iota_g1 = lax.broadcasted_iota(I32, gscore.shape, 0)
    gmask = jnp.zeros(gscore.shape, jnp.bool_)
    for _ in range(TOPK_GROUPS):
        mx = jnp.max(gscore, axis=0, keepdims=True)
        ix = jnp.min(jnp.where(gscore == mx, iota_g1, N_EXPERT_GROUPS), axis=0, keepdims=True)
        hit = iota_g1 == ix
        gmask = gmask | hit
        gscore = jnp.where(hit, REMOVED, gscore)
    cur = jnp.where(gmask, sel3, NEG_INF)
    sum_all = lambda a: jnp.sum(jnp.sum(a, axis=1, keepdims=True), axis=0, keepdims=True)
    idxs, wts, hits = [], [], []
    for _ in range(TOP_K):
        mx = jnp.max(jnp.max(cur, axis=1, keepdims=True), axis=0, keepdims=True)
        cand = jnp.where(cur == mx, iota_e, N_EXPERTS)
        ix = jnp.min(jnp.min(cand, axis=1, keepdims=True), axis=0, keepdims=True)
        hit = iota_e == ix
        wts.append(sum_all(jnp.where(hit, sc3, 0.0)).reshape(1, t))
        cur = jnp.where(hit, REMOVED, cur)
        idxs.append(ix.reshape(1, t))
        hits.append(hit)
    wsum = wts[0]
    for w in wts[1:]:
        wsum = wsum + w
    wts = [w / wsum * ROUTED_SCALE for w in wts]
    onehot = hits[0].astype(F32)
    for hit in hits[1:]:
        onehot = onehot + hit.astype(F32)
    tri = (lax.broadcasted_iota(I32, (t, t), 0) <= lax.broadcasted_iota(I32, (t, t), 1))
    cum = jnp.dot(onehot.reshape(N_EXPERTS, t).astype(BF16), tri.astype(F32).astype(BF16),
                  preferred_element_type=F32)
    cum3 = cum.reshape(shape3) - 1.0
    ranks = [sum_all(jnp.where(hit, cum3, 0.0)).reshape(1, t).astype(I32) for hit in hits]
    counts = cum[:, t - 1:t].astype(I32)
    return idxs, wts, ranks, counts


def _merge_kernel(x_ref, ya_ref, o1, o2, o3, l1, l2, l3, sga_ref, sgb_ref, mod_ref, n2_ref,
                  wpa, wpb, wo, wrt, rbias, wsg, wsu, wsd,
                  base_ref, hp_ref, ridx_ref, rw_ref, rank_ref, cnt_ref, scr):
    mod = mod_ref[0]
    tm = x_ref.shape[0]

    def token_major(ref, slot, d):
        if d == 1:
            return ref[...].astype(F32)
        nc = B_OUT_W // LANES
        for c in range(nc):
            for res in range(d):
                col = res * B_OUT_W + c * LANES
                scr[slot * nc + c, pl.ds(res, tm // d, stride=d), :] = ref[
                    :, col:col + LANES].astype(F32)
        return jnp.concatenate([scr[slot * nc + c] for c in range(nc)], axis=1)

    dils = [d for _, d in B_GROUPS]
    os_ = [token_major(r, i, d) for i, (r, d) in enumerate(zip((o1, o2, o3), dils))]
    ls = [token_major(r, 3 + i, d) for i, (r, d) in enumerate(zip((l1, l2, l3), dils))]
    mx = jnp.maximum(jnp.maximum(ls[0], ls[1]), ls[2])
    es = [jnp.exp(l - mx) for l in ls]
    den = es[0] + es[1] + es[2]
    ob = (es[0] / den) * os_[0] + (es[1] / den) * os_[1] + (es[2] / den) * os_[2]
    pa = jnp.dot(ya_ref[...], wpa[...], preferred_element_type=F32)
    pb = jnp.dot(ob.astype(BF16), wpb[...], preferred_element_type=F32)
    merged = sga_ref[...].astype(F32) * pa + sgb_ref[...].astype(F32) * pb
    mix = jnp.dot(merged.astype(BF16), wo[...], preferred_element_type=F32)
    x1 = x_ref[...] + mod[G1:G1 + 1] * mix
    h2 = _rms(x1, n2_ref[...]) * (1.0 + mod[SC2:SC2 + 1]) + mod[SH2:SH2 + 1]
    logits = lax.dot_general(wrt[...], h2, (((1,), (1,)), ((), ())), preferred_element_type=F32,
                             precision=lax.Precision.HIGHEST)
    scores = jax.nn.sigmoid(logits)
    idxs, wts, ranks, counts = _route(scores + rbias[...], scores)
    for k in range(TOP_K):
        ridx_ref[k:k + 1, :] = idxs[k]
        rw_ref[k:k + 1, :] = wts[k]
        rank_ref[k:k + 1, :] = ranks[k]
    ridx_ref[TOP_K:, :] = jnp.zeros((SUBLANES - TOP_K, tm), I32)
    rw_ref[TOP_K:, :] = jnp.zeros((SUBLANES - TOP_K, tm), F32)
    rank_ref[TOP_K:, :] = jnp.zeros((SUBLANES - TOP_K, tm), I32)
    cnt_ref[0] = counts
    hb = h2.astype(BF16)
    g = jnp.dot(hb, wsg[...], preferred_element_type=F32)
    u = jnp.dot(hb, wsu[...], preferred_element_type=F32)
    act = (g * jax.nn.sigmoid(g) * u).astype(BF16)
    shared = jnp.dot(act, wsd[...], preferred_element_type=F32)
    base_ref[...] = x1 + mod[G2:G2 + 1] * shared
    _store_slabs(hp_ref, _pack_rows(h2))


def _merge(x2, ya, outs, lses, sga, sgb, mod3, norm2, wpa, wpb, wo, wrt, rbias, wsg, wsu, wsd, seq):
    n = x2.shape[0]
    tm = TM_MERGE
    assert seq % tm == 0
    row = lambda i: (i, 0)
    full = lambda a: pl.BlockSpec(a.shape, lambda i: (0,) * a.ndim)
    weights = [wpa, wpb, wo, wrt, rbias, wsg, wsu, wsd]
    group_specs = [pl.BlockSpec((tm // d, d * B_OUT_W), row) for _, d in B_GROUPS]
    lanes = lambda i: (0, i)
    return pl.pallas_call(
        _merge_kernel,
        out_shape=[jax.ShapeDtypeStruct((n, D_MODEL), F32),
                   jax.ShapeDtypeStruct((n * ROW_SLABS, LANES), U32),
                   jax.ShapeDtypeStruct((SUBLANES, n), I32),
                   jax.ShapeDtypeStruct((SUBLANES, n), F32),
                   jax.ShapeDtypeStruct((SUBLANES, n), I32),
                   jax.ShapeDtypeStruct((n // tm, N_EXPERTS, 1), I32)],
        grid=(n // tm,),
        in_specs=[pl.BlockSpec((tm, D_MODEL), row), pl.BlockSpec((tm, A_Q_W), row)]
                 + group_specs * 2
                 + [pl.BlockSpec((tm, D_MODEL), row)] * 2
                 + [pl.BlockSpec((1, N_MOD, D_MODEL), lambda i: (i * tm // seq, 0, 0)),
                    pl.BlockSpec((1, D_MODEL), lambda i: (0, 0))]
                 + [full(w) for w in weights],
        out_specs=[pl.BlockSpec((tm, D_MODEL), row),
                   pl.BlockSpec((tm * ROW_SLABS, LANES), row),
                   pl.BlockSpec((SUBLANES, tm), lanes), pl.BlockSpec((SUBLANES, tm), lanes),
                   pl.BlockSpec((SUBLANES, tm), lanes),
                   pl.BlockSpec((1, N_EXPERTS, 1), lambda i: (i, 0, 0))],
        scratch_shapes=[pltpu.VMEM((6 * B_OUT_W // LANES, tm, LANES), F32)],
        compiler_params=_cparams(1),
        name="merge_route",
    )(x2, ya, *outs, *lses, sga, sgb, mod3, norm2.reshape(1, D_MODEL), *weights)


def _plan_kernel(ridx_ref, rank_ref, tb_ref, dest_ref, *, tm):
    iota_e = lax.broadcasted_iota(I32, (N_EXPERTS, tm), 0)
    for j in range(dest_ref.shape[0]):
        tb = tb_ref[j]
        cols = slice(j * tm, (j + 1) * tm)
        for k in range(TOP_K):
            base = jnp.sum(jnp.where(iota_e == ridx_ref[k:k + 1, cols], tb, 0), axis=0, keepdims=True)
            dest_ref[j, :, k * tm:(k + 1) * tm] = base + rank_ref[k:k + 1, cols]


def _plan(ridx, rank, tile_base, tm):
    n = ridx.shape[1]
    tiles = math.gcd(PLAN_TILES, n // tm)
    lanes = lambda i: (0, i)
    return pl.pallas_call(
        functools.partial(_plan_kernel, tm=tm),
        out_shape=jax.ShapeDtypeStruct((n // tm, 1, TOP_K * tm), I32),
        grid=(n // tm // tiles,),
        in_specs=[pl.BlockSpec((SUBLANES, tiles * tm), lanes), pl.BlockSpec((SUBLANES, tiles * tm), lanes),
                  pl.BlockSpec((tiles, N_EXPERTS, 1), lambda i: (i, 0, 0))],
        out_specs=pl.BlockSpec((tiles, 1, TOP_K * tm), lambda i: (i, 0, 0)),
        compiler_params=_cparams(1),
        name="moe_plan",
    )(ridx, rank, tile_base)


def _sc_mesh():
    return plsc.VectorSubcoreMesh(core_axis_name="core", subcore_axis_name="subcore",
                                  num_cores=SC_CORES, num_subcores=SC_SUBCORES)


def _sc_worker():
    return lax.axis_index("subcore") * SC_CORES + lax.axis_index("core")


def _sc_dispatch(hp, dest_rows, n_rows):
    n = hp.shape[0]
    halves = TM_MERGE // SC_CHUNK
    workers = SC_CORES * SC_SUBCORES
    tiles_per_w = n // TM_MERGE // workers
    idx_per_w = tiles_per_w * TOP_K * halves
    assert tiles_per_w * workers * TM_MERGE == n

    @functools.partial(
        pl.kernel, mesh=_sc_mesh(),
        out_type=jax.ShapeDtypeStruct((n_rows, ROW_SLABS, LANES), U32),
        scratch_types=[pltpu.VMEM((idx_per_w, SC_CHUNK), I32),
                       pltpu.VMEM((SC_CHUNK, ROW_SLABS, LANES), U32),
                       pltpu.SemaphoreType.DMA],
        name="moe_dispatch_sc")
    def scatter(hp_hbm, idx_hbm, xs_hbm, idx_v, rows_v, sem):
        wid = _sc_worker()
        pltpu.sync_copy(idx_hbm.at[pl.ds(wid * idx_per_w, idx_per_w)], idx_v)

        @pl.loop(0, tiles_per_w * halves)
        def _(j):
            tile = j // halves
            h = j - tile * halves
            tok = (wid * tiles_per_w + tile) * TM_MERGE + h * SC_CHUNK
            pltpu.sync_copy(hp_hbm.at[pl.ds(tok, SC_CHUNK)], rows_v)
            for k in range(TOP_K):
                row = (tile * TOP_K + k) * halves + h
                pltpu.async_copy(rows_v, xs_hbm.at[idx_v.at[row]], sem).wait()

    return scatter(hp, dest_rows)


def _sc_collect(ys, dest_rows):
    n_chunks = dest_rows.shape[0]
    workers = SC_CORES * SC_SUBCORES
    per_w = n_chunks // workers
    assert per_w * workers == n_chunks

    @functools.partial(
        pl.kernel, mesh=_sc_mesh(),
        out_type=jax.ShapeDtypeStruct((n_chunks * SC_CHUNK, ROW_SLABS, LANES), U32),
        scratch_types=[pltpu.VMEM((per_w, SC_CHUNK), I32),
                       pltpu.VMEM((SC_CHUNK, ROW_SLABS, LANES), U32),
                       pltpu.SemaphoreType.DMA],
        name="moe_collect_sc")
    def gather(ys_hbm, idx_hbm, out_hbm, idx_v, rows_v, sem):
        wid = _sc_worker()
        base = wid * per_w
        pltpu.sync_copy(idx_hbm.at[pl.ds(base, per_w)], idx_v)

        @pl.loop(0, per_w)
        def _(j):
            pltpu.async_copy(ys_hbm.at[idx_v.at[j]], rows_v, sem).wait()
            pltpu.sync_copy(rows_v, out_hbm.at[pl.ds((base + j) * SC_CHUNK, SC_CHUNK)])

    return gather(ys, dest_rows)


def _expert_kernel(blk_e_ref, valid_ref, src_ref, xs_ref, wg_ref, wu_ref, wd_ref, ys_ref,
                   wg_bf, wu_bf, wd_bf):
    del src_ref
    step = pl.program_id(0)
    valid = valid_ref[step]

    @pl.when((step == 0) | (blk_e_ref[step] != blk_e_ref[jnp.maximum(step - 1, 0)]))
    def _():
        wg_bf[...] = wg_ref[0].astype(BF16)
        wu_bf[...] = wu_ref[0].astype(BF16)
        wd_bf[...] = wd_ref[0].astype(BF16)

    sub = xs_ref.shape[0] // ROW_SLABS // EXPERT_SUBBLOCKS
    for j in range(EXPERT_SUBBLOCKS):
        @pl.when(valid > j * sub)
        def _(j=j):
            window = pl.ds(j * sub * ROW_SLABS, sub * ROW_SLABS)
            rows = j * sub + lax.broadcasted_iota(I32, (sub, 1), 0)
            x = jnp.where(rows < valid, _load_slabs(xs_ref.at[window]), jnp.uint32(0))
            lo, hi = (h.astype(BF16) for h in _unpack_rows(x))

            def up(w_bf):
                return jnp.dot(lo, w_bf[:HALF_D], preferred_element_type=F32) \
                    + jnp.dot(hi, w_bf[HALF_D:], preferred_element_type=F32)

            g = up(wg_bf)
            act = (g * jax.nn.sigmoid(g) * up(wu_bf)).astype(BF16)
            _store_slabs(ys_ref.at[window],
                         _pack_rows(jnp.dot(act, wd_bf[...], preferred_element_type=F32)))


def _experts(xs, blk_e, blk_valid, blk_src, wg, wu, wd):
    n_rows = xs.shape[0] // ROW_SLABS
    bm = MOE_BM
    nblk = n_rows // bm
    return pl.pallas_call(
        _expert_kernel,
        out_shape=jax.ShapeDtypeStruct((n_rows * ROW_SLABS, LANES), U32),
        grid_spec=pltpu.PrefetchScalarGridSpec(
            num_scalar_prefetch=3,
            grid=(nblk,),
            in_specs=[pl.BlockSpec((bm * ROW_SLABS, LANES), lambda i, be, bv, bs: (bs[i], 0)),
                      pl.BlockSpec((1, D_MODEL, D_EXPERT), lambda i, be, bv, bs: (be[i], 0, 0)),
                      pl.BlockSpec((1, D_MODEL, D_EXPERT), lambda i, be, bv, bs: (be[i], 0, 0)),
                      pl.BlockSpec((1, D_EXPERT, D_MODEL), lambda i, be, bv, bs: (be[i], 0, 0))],
            out_specs=pl.BlockSpec((bm * ROW_SLABS, LANES), lambda i, be, bv, bs: (bs[i], 0)),
            scratch_shapes=[pltpu.VMEM((D_MODEL, D_EXPERT), BF16), pltpu.VMEM((D_MODEL, D_EXPERT), BF16),
                            pltpu.VMEM((D_EXPERT, D_MODEL), BF16)]),
        compiler_params=_cparams(1),
        name="moe_experts",
    )(blk_e, blk_valid, blk_src, xs, wg, wu, wd)


def _combine_kernel(g_ref, base_ref, rw_ref, mod_ref, fn_ref, *rest):
    y_ref = rest[-1]
    tm = base_ref.shape[0]
    rw = rw_ref[...]
    acc_lo = jnp.zeros((tm, HALF_D), F32)
    acc_hi = jnp.zeros((tm, HALF_D), F32)
    for k in range(TOP_K):
        lo, hi = _unpack_rows(_load_slabs(g_ref.at[0, k]))
        w = rw[:, k:k + 1]
        acc_lo = acc_lo + w * lo
        acc_hi = acc_hi + w * hi
    routed = jnp.concatenate([acc_lo, acc_hi], axis=1)
    x2 = base_ref[...] + mod_ref[0][G2:G2 + 1] * routed
    y_ref[...] = _rms(x2, fn_ref[...])


def _combine(g, base, rw_t, mod3, final_norm, seq, part, y_prev):
    n = base.shape[0]
    tm = TM_COMBINE
    assert seq % tm == 0
    g4 = g.reshape(-1, TOP_K, tm * ROW_SLABS, LANES)
    tiles = g4.shape[0]
    off = part * tiles
    row = lambda i: (i + off, 0)
    in_specs = [pl.BlockSpec((1, TOP_K, tm * ROW_SLABS, LANES), lambda i: (i, 0, 0, 0)),
                pl.BlockSpec((tm, D_MODEL), row),
                pl.BlockSpec((tm, SUBLANES), row),
                pl.BlockSpec((1, N_MOD, D_MODEL), lambda i: ((i + off) * tm // seq, 0, 0)),
                pl.BlockSpec((1, D_MODEL), lambda i: (0, 0))]
    args = [g4, base, rw_t, mod3, final_norm.reshape(1, D_MODEL)]
    aliases = {}
    if y_prev is not None:
        in_specs.append(pl.BlockSpec(memory_space=pl.ANY))
        args.append(y_prev)
        aliases = {len(args) - 1: 0}
    return pl.pallas_call(
        _combine_kernel,
        out_shape=jax.ShapeDtypeStruct((n, D_MODEL), F32),
        grid=(tiles,),
        in_specs=in_specs,
        out_specs=pl.BlockSpec((tm, D_MODEL), row),
        input_output_aliases=aliases,
        compiler_params=_cparams(1),
        name="moe_combine",
    )(*args)


def _block_layout(tile_counts, n):
    bm = MOE_BM
    c = tile_counts[:, :, 0]
    ntiles = c.shape[0]
    counts = jnp.sum(c, axis=0)
    padded = (counts + bm - 1) // bm * bm
    earlier_e = np.tri(N_EXPERTS, k=-1, dtype=bool)
    pstart = jnp.sum(jnp.where(earlier_e, padded[None, :], 0), axis=1)
    pend = pstart + padded
    earlier_t = np.tri(ntiles, k=-1, dtype=bool)
    tile_base = pstart[None, :] + jnp.sum(jnp.where(earlier_t[:, :, None], c[None], 0), axis=1)
    nblk = -(-n * TOP_K // bm) + N_EXPERTS
    blk = jnp.arange(nblk, dtype=I32)
    blk_e = jnp.minimum(jnp.sum((pend[None, :] <= blk[:, None] * bm).astype(I32), axis=1),
                        N_EXPERTS - 1)
    onehot_e = blk_e[:, None] == jnp.arange(N_EXPERTS, dtype=I32)[None, :]
    end_e = jnp.sum(jnp.where(onehot_e, (pstart + counts)[None, :], 0), axis=1)
    blk_valid = jnp.clip(end_e - blk * bm, 0, bm).astype(I32)
    blk_src = jnp.minimum(blk, pend[N_EXPERTS - 1] // bm - 1).astype(I32)
    last_e = jnp.max(jnp.where(counts > 0, jnp.arange(N_EXPERTS, dtype=I32), 0))
    blk_e = jnp.where(blk == blk_src, blk_e, last_e).astype(I32)
    return tile_base.astype(I32)[:, :, None], blk_e, blk_valid, blk_src, nblk * bm


def _front(x, mod3, p, after=None):
    nbatch, seq, _ = x.shape
    n = nbatch * seq
    x2 = x.reshape(n, D_MODEL)
    proj = _inproj(x2, mod3, p["norm1"], p["w_in"], seq)
    qa, kva = proj[:2]
    sga, sgb = proj[8:]
    (ya,) = _band_attention(qa, kva, p["bias_a"], p["sink_a"], nbatch=nbatch, seq=seq, dil=1,
                            halo=A_HALF_WINDOW, step_rows=ATTN_ROWS_A, shared_kv=True, want_lse=False)
    outs, lses = [], []
    for gi, (w, d) in enumerate(B_GROUPS):
        q, kv = proj[2 + 2 * gi:4 + 2 * gi]
        o, lse = _band_attention(q, kv, p["bias_b"][gi], None, nbatch=nbatch, seq=seq, dil=d,
                                 halo=w // (2 * d), step_rows=ATTN_ROWS_B, shared_kv=False,
                                 want_lse=True)
        outs.append(o)
        lses.append(lse)
    if after is not None:
        ya, after = lax.optimization_barrier((ya, after))
    base, hp, ridx, rw, rank, tile_counts = _merge(
        x2, ya, outs, lses, sga, sgb, mod3, p["norm2"], p["w_pa"], p["w_pb"], p["w_o"], p["w_rt"],
        p["rbias"], p["ws_gate"], p["ws_up"], p["ws_down"], seq)
    assert TM_MERGE == TM_COMBINE
    tile_base, blk_e, blk_valid, blk_src, n_rows = _block_layout(tile_counts, n)
    dest_rows = _plan(ridx, rank, tile_base, TM_MERGE).reshape(-1, SC_CHUNK)
    return dict(base=base, hp=hp, rw=rw, dest_rows=dest_rows, blk_e=blk_e, blk_valid=blk_valid,
                blk_src=blk_src, n_rows=n_rows, shape=x.shape, after=after)


def _slabs(a):
    return a.reshape(-1, ROW_SLABS, LANES)


def _moe(st, xs, mod3, p):
    nbatch, seq, _ = st["shape"]
    dest_rows = st["dest_rows"]
    ys = _experts(xs.reshape(-1, LANES), st["blk_e"], st["blk_valid"], st["blk_src"], p["w_gate"], p["w_up"],
                  p["w_down"])
    rows_per_part = dest_rows.shape[0] // COLLECT_PARTS
    rw_t = st["rw"].T
    y = None
    for part in range(COLLECT_PARTS):
        g = _sc_collect(_slabs(ys), dest_rows[part * rows_per_part:(part + 1) * rows_per_part])
        y = _combine(g, st["base"], rw_t, mod3, p["final_norm"], seq, part, y)
    return y.reshape(nbatch, seq, D_MODEL)


def kernel(x_prompt, x_sample, c_prompt, c_sample, rel_bias, w_ada, b_ada, norm1, w_in, sink, w_pa, w_pb, w_o, norm2, w_router, router_bias, w_gate, w_up, w_down, ws_gate, ws_up, ws_down, final_norm):
    assert w_ada.shape[0] == 1
    nbp = x_prompt.shape[0]
    mod = _ada(jnp.concatenate([c_prompt, c_sample], axis=0), w_ada[0], b_ada[0])
    mod3 = mod.reshape(-1, N_MOD, D_MODEL)

    def pair_bias(heads, halo, dist_scale):
        kb = ATTN_BLOCK + 2 * halo
        tab = _bias_table(heads, ATTN_BLOCK, kb, halo, halo, dist_scale)
        return tab.reshape(-1, 2, ATTN_BLOCK, kb).transpose(0, 2, 1, 3).reshape(-1, ATTN_BLOCK, 2 * kb)

    bias_a = pair_bias(rel_bias[:, :A_Q_HEADS], A_HALF_WINDOW, 1)
    bias_b = []
    for gi, (w, d) in enumerate(B_GROUPS):
        h0 = A_Q_HEADS + gi * B_HEADS_PER_GROUP
        bias_b.append(pair_bias(rel_bias[:, h0:h0 + B_HEADS_PER_GROUP], w // (2 * d), d))
    p = {
        "norm1": norm1[0], "norm2": norm2[0], "final_norm": final_norm,
        "w_in": w_in[0].astype(BF16),
        "bias_a": bias_a,
        "sink_a": jnp.repeat(sink[0].astype(F32), HEAD_DIM).reshape(A_Q_HEADS // 2, 1, LANES),
        "bias_b": bias_b,
        "w_pa": w_pa[0].astype(BF16), "w_pb": w_pb[0].astype(BF16), "w_o": w_o[0].astype(BF16),
        "w_rt": w_router[0].T, "rbias": router_bias[0].reshape(N_EXPERTS, 1),
        "ws_gate": ws_gate[0].astype(BF16), "ws_up": ws_up[0].astype(BF16),
        "ws_down": ws_down[0].astype(BF16),
        "w_gate": w_gate[0], "w_up": w_up[0], "w_down": w_down[0],
    }
    dispatch = lambda st: _sc_dispatch(_slabs(st["hp"]), st["dest_rows"], st["n_rows"])
    st_p = _front(x_prompt, mod3[:nbp], p)
    st_s = _front(x_sample, mod3[nbp:], p, after=dispatch(st_p))
    return (_moe(st_p, st_s["after"], mod3[:nbp], p), _moe(st_s, dispatch(st_s), mod3[nbp:], p))
```

```python
import functools
import math

import jax
import jax.numpy as jnp
import numpy as np
from jax import lax
from jax.experimental import pallas as pl
from jax.experimental.pallas import tpu as pltpu
from jax.experimental.pallas import tpu_sc as plsc

F32 = jnp.float32
BF16 = jnp.bfloat16
U32 = jnp.uint32
I32 = jnp.int32

D_MODEL = 1024
HEAD_DIM = 64
A_Q_HEADS = 8
A_KV_HEADS = 2
A_HALF_WINDOW = 128
B_GROUPS = ((128, 1), (512, 4), (2048, 16))
B_HEADS_PER_GROUP = 4
N_BUCKETS = 32
MAX_DISTANCE = 1024
N_EXPERTS = 64
TOP_K = 6
N_EXPERT_GROUPS = 8
TOPK_GROUPS = 4
D_EXPERT = 256
ROUTED_SCALE = 2.5
RMS_EPS = 1e-6
NEG_INF = -1e30
N_MOD = 6
SH1, SC1, G1, SH2, SC2, G2 = range(N_MOD)
REMOVED = -3e38

A_Q_W = A_Q_HEADS * HEAD_DIM
A_KV_W = A_KV_HEADS * HEAD_DIM
B_W = len(B_GROUPS) * B_HEADS_PER_GROUP * HEAD_DIM
B_OUT_W = B_HEADS_PER_GROUP * HEAD_DIM
D_IN = A_Q_W + 2 * A_KV_W + 3 * B_W + 2 * D_MODEL
HALF_D = D_MODEL // 2
LANES = 128
SUBLANES = 8
ROW_SLABS = HALF_D // LANES
SC_CORES = 2
SC_SUBCORES = 16
SC_CHUNK = 128

TM_INPROJ = 1024
TM_MERGE = 512
ATTN_BLOCK = 128
ATTN_ROWS_A = 512
ATTN_ROWS_B = 2048
MOE_BM = 1024
TM_COMBINE = 512
COLLECT_PARTS = 2
PLAN_TILES = 8
VMEM_LIMIT = 48 * 1024 * 1024
VMEM_LIMIT_INPROJ = 60 * 1024 * 1024


def _cparams(n_axes, vmem_limit=VMEM_LIMIT):
    return pltpu.CompilerParams(
        dimension_semantics=("arbitrary",) * n_axes, vmem_limit_bytes=vmem_limit)


def _ada_kernel(c_ref, w_ref, b_ref, o_ref):
    c = c_ref[...]
    s = c * jax.nn.sigmoid(c)
    o_ref[...] = jnp.dot(s, w_ref[...], preferred_element_type=F32,
                         precision=lax.Precision.HIGHEST) + b_ref[...]


def _ada(c_all, w_ada, b_ada):
    nb = c_all.shape[0]
    return pl.pallas_call(
        _ada_kernel,
        out_shape=jax.ShapeDtypeStruct((nb, N_MOD * D_MODEL), F32),
        grid=(N_MOD,),
        in_specs=[pl.BlockSpec((nb, D_MODEL), lambda j: (0, 0)),
                  pl.BlockSpec((D_MODEL, D_MODEL), lambda j: (0, j)),
                  pl.BlockSpec((1, D_MODEL), lambda j: (0, j))],
        out_specs=pl.BlockSpec((nb, D_MODEL), lambda j: (0, j)),
        compiler_params=_cparams(1),
        name="ada",
    )(c_all, w_ada, b_ada.reshape(1, N_MOD * D_MODEL))


def _rms(x, g):
    return x * lax.rsqrt(jnp.mean(x * x, axis=-1, keepdims=True) + RMS_EPS) * g


def _inproj_kernel(x_ref, mod_ref, n1_ref, w_ref, qa, kva, *rest):
    q_b = rest[0:6:2]
    kv_b = rest[1:6:2]
    sga, sgb, scr = rest[6:]
    mod = mod_ref[0]
    h = _rms(x_ref[...], n1_ref[...]) * (1.0 + mod[SC1:SC1 + 1]) + mod[SH1:SH1 + 1]
    hb = h.astype(BF16)
    tm = hb.shape[0]
    scale = HEAD_DIM ** -0.5

    def proj(off, width):
        return jnp.dot(hb, w_ref[:, off:off + width], preferred_element_type=F32)

    qa[...] = (proj(0, A_Q_W) * scale).astype(BF16)
    for j, off in enumerate((A_Q_W, A_Q_W + A_KV_W)):
        r = proj(off, A_KV_W)
        kva[:, 2 * j * A_KV_W:(2 * j + 1) * A_KV_W] = r.astype(BF16)
        kva[:, (2 * j + 1) * A_KV_W:(2 * j + 2) * A_KV_W] = pltpu.roll(r, HEAD_DIM, 1).astype(BF16)
    off = A_Q_W + 2 * A_KV_W
    for t in range(3):
        for gi, (_, d) in enumerate(B_GROUPS):
            r = proj(off + t * B_W + gi * B_OUT_W, B_OUT_W)
            if t == 0:
                r = r * scale
            ref, width, base = (q_b[gi], B_OUT_W, 0) if t == 0 else (kv_b[gi], 2 * B_OUT_W,
                                                                     (t - 1) * B_OUT_W)
            if d == 1:
                ref[:, base:base + B_OUT_W] = r.astype(BF16)
            else:
                for c in range(B_OUT_W // LANES):
                    slot = (t * 2 + gi - 1) * (B_OUT_W // LANES) + c
                    scr[slot] = r[:, c * LANES:(c + 1) * LANES]
                    for res in range(d):
                        col = res * width + base + c * LANES
                        ref[:, col:col + LANES] = scr[
                            slot, pl.ds(res, tm // d, stride=d), :].astype(BF16)
    off += 3 * B_W
    sga[...] = jax.nn.sigmoid(proj(off, D_MODEL)).astype(BF16)
    sgb[...] = jax.nn.sigmoid(proj(off + D_MODEL, D_MODEL)).astype(BF16)


def _inproj(x2, mod3, norm1, w_in_bf, seq):
    n = x2.shape[0]
    tm = TM_INPROJ
    assert seq % tm == 0 and n % tm == 0
    row = lambda i: (i, 0)
    shapes = [(n, A_Q_W, tm), (n, 4 * A_KV_W, tm)]
    for _, d in B_GROUPS:
        shapes += [(n // d, d * B_OUT_W, tm // d), (n // d, d * 2 * B_OUT_W, tm // d)]
    shapes += [(n, D_MODEL, tm)] * 2
    return pl.pallas_call(
        _inproj_kernel,
        out_shape=[jax.ShapeDtypeStruct((r, c), BF16) for r, c, _ in shapes],
        grid=(n // tm,),
        in_specs=[pl.BlockSpec((tm, D_MODEL), row),
                  pl.BlockSpec((1, N_MOD, D_MODEL), lambda i: (i * tm // seq, 0, 0)),
                  pl.BlockSpec((1, D_MODEL), lambda i: (0, 0)),
                  pl.BlockSpec((D_MODEL, D_IN), lambda i: (0, 0), pipeline_mode=pl.Buffered(1))],
        out_specs=[pl.BlockSpec((b, c), row) for _, c, b in shapes],
        scratch_shapes=[pltpu.VMEM((6 * B_OUT_W // LANES, tm, LANES), F32)],
        compiler_params=_cparams(1, VMEM_LIMIT_INPROJ),
        name="inproj",
    )(x2, mod3, norm1.reshape(1, D_MODEL), w_in_bf)


def _rel_bucket_np(rel):
    half = N_BUCKETS // 2
    max_exact = half // 2
    n = np.abs(rel)
    large = max_exact + (np.log(np.maximum(n, 1) / max_exact) / math.log(MAX_DISTANCE / max_exact)
                         * (half - max_exact)).astype(np.int32)
    large = np.minimum(large, half - 1)
    return ((rel > 0).astype(np.int32) * half + np.where(n < max_exact, n, large)).astype(np.int32)


def _bias_table(rel_bias_heads, n_q, n_k, key_off, band, dist_scale):
    p = n_q + n_k
    rel = np.arange(p) - (n_q - 1) - key_off
    bucket = _rel_bucket_np(rel * dist_scale)
    t = jnp.where((np.abs(rel) <= band)[None], rel_bias_heads.astype(F32)[bucket].T, NEG_INF)
    big = jnp.tile(t, (1, n_q + 1))
    tab = big[:, n_q - 1:n_q - 1 + n_q * (p - 1)].reshape(-1, n_q, p - 1)
    return tab[:, :, :n_k]


def _pair_rhs(k_top, k_bot, v_top, v_bot):
    kb = k_top.shape[0]
    low = jnp.where(lax.broadcasted_iota(I32, (kb, LANES), 1) < HEAD_DIM, 1.0, 0.0).astype(BF16)
    high = jnp.where(lax.broadcasted_iota(I32, (kb, LANES), 1) < HEAD_DIM, 0.0, 1.0).astype(BF16)
    rhs_k = jnp.concatenate([k_top * low, k_bot * high], axis=0)
    rhs_v = jnp.concatenate([jnp.concatenate([v_top * low, low], axis=1),
                             jnp.concatenate([v_bot * high, high], axis=1)], axis=0)
    return rhs_k, rhs_v


def _pair_attention(q_pair, rhs_k, rhs_v, bias_pair, pen, sink_pair, want_lse):
    kb = rhs_k.shape[0] // 2
    s = lax.dot_general(q_pair, rhs_k, (((1,), (1,)), ((), ())), preferred_element_type=F32)
    s = s + bias_pair
    if pen is not None:
        s = s + pen
    s0, s1 = s[:, :kb], s[:, kb:]
    m0 = jnp.max(s0, axis=-1, keepdims=True)
    m1 = jnp.max(s1, axis=-1, keepdims=True)
    if sink_pair is not None:
        m0 = jnp.maximum(m0, sink_pair[:, 0:1])
        m1 = jnp.maximum(m1, sink_pair[:, HEAD_DIM:HEAD_DIM + 1])
    p = jnp.concatenate([jnp.exp(s0 - m0), jnp.exp(s1 - m1)], axis=1).astype(BF16)
    od = jnp.dot(p, rhs_v, preferred_element_type=F32)
    o, den = od[:, :LANES], od[:, LANES:]
    low = lax.broadcasted_iota(I32, o.shape, 1) < HEAD_DIM
    m_full = jnp.where(low, m0, m1)
    if sink_pair is not None:
        den = den + jnp.exp(sink_pair - m_full)
    return o / den, (m_full + jnp.log(den)) if want_lse else None


def _attn_kernel(*refs, halo, nblk, n_pairs, n_seqs, shared_kv, has_sink, want_lse, n_steps):
    it = iter(refs)
    q_ref = next(it)
    kv_prev, kv_cur, kv_next = next(it), next(it), next(it)
    bias_ref = next(it)
    sink_ref = next(it) if has_sink else None
    o_ref = next(it)
    lse_ref = next(it) if want_lse else None
    step = pl.program_id(2)
    blk = ATTN_BLOCK
    kb = blk + 2 * halo
    qw = n_pairs * LANES
    kv = jnp.concatenate([kv_prev[...], kv_cur[...], kv_next[...]], axis=0)
    col = lax.broadcasted_iota(I32, (1, kb), 1)
    for b in range(nblk):
        lo = jnp.where(step == 0, halo, 0) if b == 0 else 0
        hi = jnp.where(step == n_steps - 1, kb - halo, kb) if b == nblk - 1 else kb
        pen = None
        if b == 0 or b == nblk - 1:
            pen1 = jnp.where((col >= lo) & (col < hi), 0.0, NEG_INF)
            pen = jnp.concatenate([pen1, pen1], axis=1)
        rows = slice(b * blk, (b + 1) * blk)
        krows = slice(b * blk, b * blk + kb)
        lane_col = lambda j: kv[krows, j * LANES:(j + 1) * LANES]
        if shared_kv:
            k, k_sw, v, v_sw = (lane_col(j) for j in range(4))
            rhs = [_pair_rhs(k, k_sw, v, v_sw), _pair_rhs(k_sw, k, v_sw, v)]
        for s in range(n_seqs):
            for c in range(n_pairs):
                lanes = slice(s * qw + c * LANES, s * qw + (c + 1) * LANES)
                if shared_kv:
                    rhs_k, rhs_v = rhs[c // (n_pairs // 2)]
                else:
                    k, v = lane_col(2 * s * n_pairs + c), lane_col((2 * s + 1) * n_pairs + c)
                    rhs_k, rhs_v = _pair_rhs(k, k, v, v)
                o, lse = _pair_attention(q_ref[rows, lanes], rhs_k, rhs_v, bias_ref[c], pen,
                                         sink_ref[c] if has_sink else None, want_lse)
                o_ref[rows, lanes] = o.astype(o_ref.dtype)
                if want_lse:
                    lse_ref[rows, lanes] = lse


def _band_attention(q, kv, bias, sink, *, nbatch, seq, dil, halo, step_rows, shared_kv, want_lse):
    n = nbatch * seq
    sub_len = seq // dil
    qw = q.shape[1] // dil
    kv_width = kv.shape[1] // dil
    rows = min(step_rows, sub_len)
    n_seqs = min(dil, step_rows // rows)
    assert sub_len % rows == 0 and rows % ATTN_BLOCK == 0 and rows % halo == 0 and dil % n_seqs == 0
    nq = sub_len // rows
    per = rows // halo
    total_halos = n // dil // halo
    cur = lambda b, r, i: (b * nq + i, r)
    prev = lambda b, r, i: (jnp.maximum((b * nq + i) * per - 1, 0), r)
    nxt = lambda b, r, i: (jnp.minimum((b * nq + i + 1) * per, total_halos - 1), r)
    const = lambda a: pl.BlockSpec(a.shape, lambda b, r, i: (0,) * a.ndim)
    qw_step, kvw_step = qw * n_seqs, kv_width * n_seqs

    in_specs = [pl.BlockSpec((rows, qw_step), cur), pl.BlockSpec((halo, kvw_step), prev),
                pl.BlockSpec((rows, kvw_step), cur), pl.BlockSpec((halo, kvw_step), nxt),
                const(bias)]
    args = [q, kv, kv, kv, bias]
    if sink is not None:
        in_specs.append(const(sink))
        args.append(sink)
    out_shape = [jax.ShapeDtypeStruct(q.shape, BF16)]
    out_specs = [pl.BlockSpec((rows, qw_step), cur)]
    if want_lse:
        out_shape.append(jax.ShapeDtypeStruct(q.shape, F32))
        out_specs.append(pl.BlockSpec((rows, qw_step), cur))
    return pl.pallas_call(
        functools.partial(_attn_kernel, halo=halo, nblk=rows // ATTN_BLOCK, n_pairs=qw // LANES,
                          n_seqs=n_seqs, shared_kv=shared_kv, has_sink=sink is not None,
                          want_lse=want_lse, n_steps=nq),
        out_shape=out_shape,
        grid=(nbatch, dil // n_seqs, nq),
        in_specs=in_specs,
        out_specs=out_specs,
        compiler_params=_cparams(3),
        name=f"band_attn_d{dil}",
    )(*args)


def _pack_rows(y):
    return pltpu.pack_elementwise([y[:, :HALF_D], y[:, HALF_D:]], packed_dtype=BF16)


def _store_slabs(ref, packed):
    t = packed.shape[0]
    for c in range(ROW_SLABS):
        ref[pl.ds(c, t, stride=ROW_SLABS), :] = packed[:, c * LANES:(c + 1) * LANES]


def _load_slabs(ref):
    t = ref.shape[0] // ROW_SLABS
    return jnp.concatenate([ref[pl.ds(c, t, stride=ROW_SLABS), :] for c in range(ROW_SLABS)], axis=1)


def _unpack_rows(p):
    return tuple(pltpu.unpack_elementwise(p, index=i, packed_dtype=BF16, unpacked_dtype=F32)
                 for i in range(2))


def _route(sel, scores):
    t = sel.shape[-1]
    per = N_EXPERTS // N_EXPERT_GROUPS
    shape3 = (N_EXPERT_GROUPS, per, t)
    sel3 = sel.reshape(shape3)
    sc3 = scores.reshape(shape3)
    iota_g = lax.broadcasted_iota(I32, shape3, 0)
    iota_m = lax.broadcasted_iota(I32, shape3, 1)
    iota_e = iota_g * per + iota_m
    m1 = jnp.max(sel3, axis=1, keepdims=True)
    i1 = jnp.min(jnp.where(sel3 == m1, iota_m, per), axis=1, keepdims=True)
    m2 = jnp.max(jnp.where(iota_m == i1, REMOVED, sel3), axis=1, keepdims=True)
    gscore = m1 + m2
    iota_g1 = lax.broadcasted_iota(I32, gscore.shape, 0)
    gmask = jnp.zeros(gscore.shape, jnp.bool_)
    for _ in range(TOPK_GROUPS):
        mx = jnp.max(gscore, axis=0, keepdims=True)
        ix = jnp.min(jnp.where(gscore == mx, iota_g1, N_EXPERT_GROUPS), axis=0, keepdims=True)
        hit = iota_g1 == ix
        gmask = gmask | hit
        gscore = jnp.where(hit, REMOVED, gscore)
    cur = jnp.where(gmask, sel3, NEG_INF)
    sum_all = lambda a: jnp.sum(jnp.sum(a, axis=1, keepdims=True), axis=0, keepdims=True)
    idxs, wts, hits = [], [], []
    for _ in range(TOP_K):
        mx = jnp.max(jnp.max(cur, axis=1, keepdims=True), axis=0, keepdims=True)
        cand = jnp.where(cur == mx, iota_e, N_EXPERTS)
        ix = jnp.min(jnp.min(cand, axis=1, keepdims=True), axis=0, keepdims=True)
        hit = iota_e == ix
        wts.append(sum_all(jnp.where(hit, sc3, 0.0)).reshape(1, t))
        cur = jnp.where(hit, REMOVED, cur)
        idxs.append(ix.reshape(1, t))
        hits.append(hit)
    wsum = wts[0]
    for w in wts[1:]:
        wsum = wsum + w
    wts = [w / wsum * ROUTED_SCALE for w in wts]
    onehot = hits[0].astype(F32)
    for hit in hits[1:]:
        onehot = onehot + hit.astype(F32)
    tri = (lax.broadcasted_iota(I32, (t, t), 0) <= lax.broadcasted_iota(I32, (t, t), 1))
    cum = jnp.dot(onehot.reshape(N_EXPERTS, t).astype(BF16), tri.astype(F32).astype(BF16),
                  preferred_element_type=F32)
    cum3 = cum.reshape(shape3) - 1.0
    ranks = [sum_all(jnp.where(hit, cum3, 0.0)).reshape(1, t).astype(I32) for hit in hits]
    counts = cum[:, t - 1:t].astype(I32)
    return idxs, wts, ranks, counts


def _merge_kernel(x_ref, ya_ref, o1, o2, o3, l1, l2, l3, sga_ref, sgb_ref, mod_ref, n2_ref,
                  wpa, wpb, wo, wrt, rbias, wsg, wsu, wsd,
                  base_ref, hp_ref, ridx_ref, rw_ref, rank_ref, cnt_ref, scr):
    mod = mod_ref[0]
    tm = x_ref.shape[0]

    def token_major(ref, slot, d):
        if d == 1:
            return ref[...].astype(F32)
        nc = B_OUT_W // LANES
        for c in range(nc):
            for res in range(d):
                col = res * B_OUT_W + c * LANES
                scr[slot * nc + c, pl.ds(res, tm // d, stride=d), :] = ref[
                    :, col:col + LANES].astype(F32)
        return jnp.concatenate([scr[slot * nc + c] for c in range(nc)], axis=1)

    dils = [d for _, d in B_GROUPS]
    os_ = [token_major(r, i, d) for i, (r, d) in enumerate(zip((o1, o2, o3), dils))]
    ls = [token_major(r, 3 + i, d) for i, (r, d) in enumerate(zip((l1, l2, l3), dils))]
    mx = jnp.maximum(jnp.maximum(ls[0], ls[1]), ls[2])
    es = [jnp.exp(l - mx) for l in ls]
    den = es[0] + es[1] + es[2]
    ob = (es[0] / den) * os_[0] + (es[1] / den) * os_[1] + (es[2] / den) * os_[2]
    pa = jnp.dot(ya_ref[...], wpa[...], preferred_element_type=F32)
    pb = jnp.dot(ob.astype(BF16), wpb[...], preferred_element_type=F32)
    merged = sga_ref[...].astype(F32) * pa + sgb_ref[...].astype(F32) * pb
    mix = jnp.dot(merged.astype(BF16), wo[...], preferred_element_type=F32)
    x1 = x_ref[...] + mod[G1:G1 + 1] * mix
    h2 = _rms(x1, n2_ref[...]) * (1.0 + mod[SC2:SC2 + 1]) + mod[SH2:SH2 + 1]
    logits = lax.dot_general(wrt[...], h2, (((1,), (1,)), ((), ())), preferred_element_type=F32,
                             precision=lax.Precision.HIGHEST)
    scores = jax.nn.sigmoid(logits)
    idxs, wts, ranks, counts = _route(scores + rbias[...], scores)
    for k in range(TOP_K):
        ridx_ref[k:k + 1, :] = idxs[k]
        rw_ref[k:k + 1, :] = wts[k]
        rank_ref[k:k + 1, :] = ranks[k]
    ridx_ref[TOP_K:, :] = jnp.zeros((SUBLANES - TOP_K, tm), I32)
    rw_ref[TOP_K:, :] = jnp.zeros((SUBLANES - TOP_K, tm), F32)
    rank_ref[TOP_K:, :] = jnp.zeros((SUBLANES - TOP_K, tm), I32)
    cnt_ref[0] = counts
    hb = h2.astype(BF16)
    g = jnp.dot(hb, wsg[...], preferred_element_type=F32)
    u = jnp.dot(hb, wsu[...], preferred_element_type=F32)
    act = (g * jax.nn.sigmoid(g) * u).astype(BF16)
    shared = jnp.dot(act, wsd[...], preferred_element_type=F32)
    base_ref[...] = x1 + mod[G2:G2 + 1] * shared
    _store_slabs(hp_ref, _pack_rows(h2))


def _merge(x2, ya, outs, lses, sga, sgb, mod3, norm2, wpa, wpb, wo, wrt, rbias, wsg, wsu, wsd, seq):
    n = x2.shape[0]
    tm = TM_MERGE
    assert seq % tm == 0
    row = lambda i: (i, 0)
    full = lambda a: pl.BlockSpec(a.shape, lambda i: (0,) * a.ndim)
    weights = [wpa, wpb, wo, wrt, rbias, wsg, wsu, wsd]
    group_specs = [pl.BlockSpec((tm // d, d * B_OUT_W), row) for _, d in B_GROUPS]
    lanes = lambda i: (0, i)
    return pl.pallas_call(
        _merge_kernel,
        out_shape=[jax.ShapeDtypeStruct((n, D_MODEL), F32),
                   jax.ShapeDtypeStruct((n * ROW_SLABS, LANES), U32),
                   jax.ShapeDtypeStruct((SUBLANES, n), I32),
                   jax.ShapeDtypeStruct((SUBLANES, n), F32),
                   jax.ShapeDtypeStruct((SUBLANES, n), I32),
                   jax.ShapeDtypeStruct((n // tm, N_EXPERTS, 1), I32)],
        grid=(n // tm,),
        in_specs=[pl.BlockSpec((tm, D_MODEL), row), pl.BlockSpec((tm, A_Q_W), row)]
                 + group_specs * 2
                 + [pl.BlockSpec((tm, D_MODEL), row)] * 2
                 + [pl.BlockSpec((1, N_MOD, D_MODEL), lambda i: (i * tm // seq, 0, 0)),
                    pl.BlockSpec((1, D_MODEL), lambda i: (0, 0))]
                 + [full(w) for w in weights],
        out_specs=[pl.BlockSpec((tm, D_MODEL), row),
                   pl.BlockSpec((tm * ROW_SLABS, LANES), row),
                   pl.BlockSpec((SUBLANES, tm), lanes), pl.BlockSpec((SUBLANES, tm), lanes),
                   pl.BlockSpec((SUBLANES, tm), lanes),
                   pl.BlockSpec((1, N_EXPERTS, 1), lambda i: (i, 0, 0))],
        scratch_shapes=[pltpu.VMEM((6 * B_OUT_W // LANES, tm, LANES), F32)],
        compiler_params=_cparams(1),
        name="merge_route",
    )(x2, ya, *outs, *lses, sga, sgb, mod3, norm2.reshape(1, D_MODEL), *weights)


def _plan_kernel(ridx_ref, rank_ref, tb_ref, dest_ref, *, tm):
    iota_e = lax.broadcasted_iota(I32, (N_EXPERTS, tm), 0)
    for j in range(dest_ref.shape[0]):
        tb = tb_ref[j]
        cols = slice(j * tm, (j + 1) * tm)
        for k in range(TOP_K):
            base = jnp.sum(jnp.where(iota_e == ridx_ref[k:k + 1, cols], tb, 0), axis=0, keepdims=True)
            dest_ref[j, :, k * tm:(k + 1) * tm] = base + rank_ref[k:k + 1, cols]


def _plan(ridx, rank, tile_base, tm):
    n = ridx.shape[1]
    tiles = math.gcd(PLAN_TILES, n // tm)
    lanes = lambda i: (0, i)
    return pl.pallas_call(
        functools.partial(_plan_kernel, tm=tm),
        out_shape=jax.ShapeDtypeStruct((n // tm, 1, TOP_K * tm), I32),
        grid=(n // tm // tiles,),
        in_specs=[pl.BlockSpec((SUBLANES, tiles * tm), lanes), pl.BlockSpec((SUBLANES, tiles * tm), lanes),
                  pl.BlockSpec((tiles, N_EXPERTS, 1), lambda i: (i, 0, 0))],
        out_specs=pl.BlockSpec((tiles, 1, TOP_K * tm), lambda i: (i, 0, 0)),
        compiler_params=_cparams(1),
        name="moe_plan",
    )(ridx, rank, tile_base)


def _sc_mesh():
    return plsc.VectorSubcoreMesh(core_axis_name="core", subcore_axis_name="subcore",
                                  num_cores=SC_CORES, num_subcores=SC_SUBCORES)


def _sc_worker():
    return lax.axis_index("subcore") * SC_CORES + lax.axis_index("core")


def _sc_dispatch(hp, dest_rows, n_rows):
    n = hp.shape[0]
    halves = TM_MERGE // SC_CHUNK
    workers = SC_CORES * SC_SUBCORES
    tiles_per_w = n // TM_MERGE // workers
    idx_per_w = tiles_per_w * TOP_K * halves
    assert tiles_per_w * workers * TM_MERGE == n

    @functools.partial(
        pl.kernel, mesh=_sc_mesh(),
        out_type=jax.ShapeDtypeStruct((n_rows, ROW_SLABS, LANES), U32),
        scratch_types=[pltpu.VMEM((idx_per_w, SC_CHUNK), I32),
                       pltpu.VMEM((SC_CHUNK, ROW_SLABS, LANES), U32),
                       pltpu.SemaphoreType.DMA],
        name="moe_dispatch_sc")
    def scatter(hp_hbm, idx_hbm, xs_hbm, idx_v, rows_v, sem):
        wid = _sc_worker()
        pltpu.sync_copy(idx_hbm.at[pl.ds(wid * idx_per_w, idx_per_w)], idx_v)

        @pl.loop(0, tiles_per_w * halves)
        def _(j):
            tile = j // halves
            h = j - tile * halves
            tok = (wid * tiles_per_w + tile) * TM_MERGE + h * SC_CHUNK
            pltpu.sync_copy(hp_hbm.at[pl.ds(tok, SC_CHUNK)], rows_v)
            for k in range(TOP_K):
                row = (tile * TOP_K + k) * halves + h
                pltpu.async_copy(rows_v, xs_hbm.at[idx_v.at[row]], sem).wait()

    return scatter(hp, dest_rows)


def _sc_collect(ys, dest_rows):
    n_chunks = dest_rows.shape[0]
    workers = SC_CORES * SC_SUBCORES
    per_w = n_chunks // workers
    assert per_w * workers == n_chunks

    @functools.partial(
        pl.kernel, mesh=_sc_mesh(),
        out_type=jax.ShapeDtypeStruct((n_chunks * SC_CHUNK, ROW_SLABS, LANES), U32),
        scratch_types=[pltpu.VMEM((per_w, SC_CHUNK), I32),
                       pltpu.VMEM((SC_CHUNK, ROW_SLABS, LANES), U32),
                       pltpu.SemaphoreType.DMA],
        name="moe_collect_sc")
    def gather(ys_hbm, idx_hbm, out_hbm, idx_v, rows_v, sem):
        wid = _sc_worker()
        base = wid * per_w
        pltpu.sync_copy(idx_hbm.at[pl.ds(base, per_w)], idx_v)

        @pl.loop(0, per_w)
        def _(j):
            pltpu.async_copy(ys_hbm.at[idx_v.at[j]], rows_v, sem).wait()
            pltpu.sync_copy(rows_v, out_hbm.at[pl.ds((base + j) * SC_CHUNK, SC_CHUNK)])

    return gather(ys, dest_rows)


def _expert_kernel(blk_e_ref, valid_ref, src_ref, xs_ref, wg_ref, wu_ref, wd_ref, ys_ref,
                   wg_bf, wu_bf, wd_bf):
    del src_ref
    step = pl.program_id(0)
    valid = valid_ref[step]

    @pl.when((step == 0) | (blk_e_ref[step] != blk_e_ref[jnp.maximum(step - 1, 0)]))
    def _():
        wg_bf[...] = wg_ref[0].astype(BF16)
        wu_bf[...] = wu_ref[0].astype(BF16)
        wd_bf[...] = wd_ref[0].astype(BF16)

    @pl.when(valid > 0)
    def _():
        rows = lax.broadcasted_iota(I32, (xs_ref.shape[0] // ROW_SLABS, 1), 0)
        lo, hi = _unpack_rows(jnp.where(rows < valid, _load_slabs(xs_ref), jnp.uint32(0)))
        lo, hi = lo.astype(BF16), hi.astype(BF16)

        def up(w_bf):
            return jnp.dot(lo, w_bf[:HALF_D], preferred_element_type=F32) \
                + jnp.dot(hi, w_bf[HALF_D:], preferred_element_type=F32)

        g = up(wg_bf)
        act = (g * jax.nn.sigmoid(g) * up(wu_bf)).astype(BF16)
        _store_slabs(ys_ref, _pack_rows(jnp.dot(act, wd_bf[...], preferred_element_type=F32)))


def _experts(xs, blk_e, blk_valid, blk_src, wg, wu, wd):
    n_rows = xs.shape[0] // ROW_SLABS
    bm = MOE_BM
    nblk = n_rows // bm
    return pl.pallas_call(
        _expert_kernel,
        out_shape=jax.ShapeDtypeStruct((n_rows * ROW_SLABS, LANES), U32),
        grid_spec=pltpu.PrefetchScalarGridSpec(
            num_scalar_prefetch=3,
            grid=(nblk,),
            in_specs=[pl.BlockSpec((bm * ROW_SLABS, LANES), lambda i, be, bv, bs: (bs[i], 0)),
                      pl.BlockSpec((1, D_MODEL, D_EXPERT), lambda i, be, bv, bs: (be[i], 0, 0)),
                      pl.BlockSpec((1, D_MODEL, D_EXPERT), lambda i, be, bv, bs: (be[i], 0, 0)),
                      pl.BlockSpec((1, D_EXPERT, D_MODEL), lambda i, be, bv, bs: (be[i], 0, 0))],
            out_specs=pl.BlockSpec((bm * ROW_SLABS, LANES), lambda i, be, bv, bs: (bs[i], 0)),
            scratch_shapes=[pltpu.VMEM((D_MODEL, D_EXPERT), BF16), pltpu.VMEM((D_MODEL, D_EXPERT), BF16),
                            pltpu.VMEM((D_EXPERT, D_MODEL), BF16)]),
        compiler_params=_cparams(1),
        name="moe_experts",
    )(blk_e, blk_valid, blk_src, xs, wg, wu, wd)


def _combine_kernel(g_ref, base_ref, rw_ref, mod_ref, fn_ref, *rest):
    y_ref = rest[-1]
    tm = base_ref.shape[0]
    rw = rw_ref[...]
    acc_lo = jnp.zeros((tm, HALF_D), F32)
    acc_hi = jnp.zeros((tm, HALF_D), F32)
    for k in range(TOP_K):
        lo, hi = _unpack_rows(_load_slabs(g_ref.at[0, k]))
        w = rw[:, k:k + 1]
        acc_lo = acc_lo + w * lo
        acc_hi = acc_hi + w * hi
    routed = jnp.concatenate([acc_lo, acc_hi], axis=1)
    x2 = base_ref[...] + mod_ref[0][G2:G2 + 1] * routed
    y_ref[...] = _rms(x2, fn_ref[...])


def _combine(g, base, rw_t, mod3, final_norm, seq, part, y_prev):
    n = base.shape[0]
    tm = TM_COMBINE
    assert seq % tm == 0
    g4 = g.reshape(-1, TOP_K, tm * ROW_SLABS, LANES)
    tiles = g4.shape[0]
    off = part * tiles
    row = lambda i: (i + off, 0)
    in_specs = [pl.BlockSpec((1, TOP_K, tm * ROW_SLABS, LANES), lambda i: (i, 0, 0, 0)),
                pl.BlockSpec((tm, D_MODEL), row),
                pl.BlockSpec((tm, SUBLANES), row),
                pl.BlockSpec((1, N_MOD, D_MODEL), lambda i: ((i + off) * tm // seq, 0, 0)),
                pl.BlockSpec((1, D_MODEL), lambda i: (0, 0))]
    args = [g4, base, rw_t, mod3, final_norm.reshape(1, D_MODEL)]
    aliases = {}
    if y_prev is not None:
        in_specs.append(pl.BlockSpec(memory_space=pl.ANY))
        args.append(y_prev)
        aliases = {len(args) - 1: 0}
    return pl.pallas_call(
        _combine_kernel,
        out_shape=jax.ShapeDtypeStruct((n, D_MODEL), F32),
        grid=(tiles,),
        in_specs=in_specs,
        out_specs=pl.BlockSpec((tm, D_MODEL), row),
        input_output_aliases=aliases,
        compiler_params=_cparams(1),
        name="moe_combine",
    )(*args)


def _block_layout(tile_counts, n):
    bm = MOE_BM
    c = tile_counts[:, :, 0]
    ntiles = c.shape[0]
    counts = jnp.sum(c, axis=0)
    padded = (counts + bm - 1) // bm * bm
    earlier_e = np.tri(N_EXPERTS, k=-1, dtype=bool)
    pstart = jnp.sum(jnp.where(earlier_e, padded[None, :], 0), axis=1)
    pend = pstart + padded
    earlier_t = np.tri(ntiles, k=-1, dtype=bool)
    tile_base = pstart[None, :] + jnp.sum(jnp.where(earlier_t[:, :, None], c[None], 0), axis=1)
    nblk = -(-n * TOP_K // bm) + N_EXPERTS
    blk = jnp.arange(nblk, dtype=I32)
    blk_e = jnp.minimum(jnp.sum((pend[None, :] <= blk[:, None] * bm).astype(I32), axis=1),
                        N_EXPERTS - 1)
    onehot_e = blk_e[:, None] == jnp.arange(N_EXPERTS, dtype=I32)[None, :]
    end_e = jnp.sum(jnp.where(onehot_e, (pstart + counts)[None, :], 0), axis=1)
    blk_valid = jnp.clip(end_e - blk * bm, 0, bm).astype(I32)
    blk_src = jnp.minimum(blk, pend[N_EXPERTS - 1] // bm - 1).astype(I32)
    last_e = jnp.max(jnp.where(counts > 0, jnp.arange(N_EXPERTS, dtype=I32), 0))
    blk_e = jnp.where(blk == blk_src, blk_e, last_e).astype(I32)
    return tile_base.astype(I32)[:, :, None], blk_e, blk_valid, blk_src, nblk * bm


def _front(x, mod3, p, after=None):
    nbatch, seq, _ = x.shape
    n = nbatch * seq
    x2 = x.reshape(n, D_MODEL)
    proj = _inproj(x2, mod3, p["norm1"], p["w_in"], seq)
    qa, kva = proj[:2]
    sga, sgb = proj[8:]
    (ya,) = _band_attention(qa, kva, p["bias_a"], p["sink_a"], nbatch=nbatch, seq=seq, dil=1,
                            halo=A_HALF_WINDOW, step_rows=ATTN_ROWS_A, shared_kv=True, want_lse=False)
    outs, lses = [], []
    for gi, (w, d) in enumerate(B_GROUPS):
        q, kv = proj[2 + 2 * gi:4 + 2 * gi]
        o, lse = _band_attention(q, kv, p["bias_b"][gi], None, nbatch=nbatch, seq=seq, dil=d,
                                 halo=w // (2 * d), step_rows=ATTN_ROWS_B, shared_kv=False,
                                 want_lse=True)
        outs.append(o)
        lses.append(lse)
    if after is not None:
        ya, after = lax.optimization_barrier((ya, after))
    base, hp, ridx, rw, rank, tile_counts = _merge(
        x2, ya, outs, lses, sga, sgb, mod3, p["norm2"], p["w_pa"], p["w_pb"], p["w_o"], p["w_rt"],
        p["rbias"], p["ws_gate"], p["ws_up"], p["ws_down"], seq)
    assert TM_MERGE == TM_COMBINE
    tile_base, blk_e, blk_valid, blk_src, n_rows = _block_layout(tile_counts, n)
    dest_rows = _plan(ridx, rank, tile_base, TM_MERGE).reshape(-1, SC_CHUNK)
    return dict(base=base, hp=hp, rw=rw, dest_rows=dest_rows, blk_e=blk_e, blk_valid=blk_valid,
                blk_src=blk_src, n_rows=n_rows, shape=x.shape, after=after)


def _slabs(a):
    return a.reshape(-1, ROW_SLABS, LANES)


def _moe(st, xs, mod3, p):
    nbatch, seq, _ = st["shape"]
    dest_rows = st["dest_rows"]
    ys = _experts(xs.reshape(-1, LANES), st["blk_e"], st["blk_valid"], st["blk_src"], p["w_gate"], p["w_up"],
                  p["w_down"])
    rows_per_part = dest_rows.shape[0] // COLLECT_PARTS
    rw_t = st["rw"].T
    y = None
    for part in range(COLLECT_PARTS):
        g = _sc_collect(_slabs(ys), dest_rows[part * rows_per_part:(part + 1) * rows_per_part])
        y = _combine(g, st["base"], rw_t, mod3, p["final_norm"], seq, part, y)
    return y.reshape(nbatch, seq, D_MODEL)


def kernel(x_prompt, x_sample, c_prompt, c_sample, rel_bias, w_ada, b_ada, norm1, w_in, sink, w_pa, w_pb, w_o, norm2, w_router, router_bias, w_gate, w_up, w_down, ws_gate, ws_up, ws_down, final_norm):
    assert w_ada.shape[0] == 1
    nbp = x_prompt.shape[0]
    mod = _ada(jnp.concatenate([c_prompt, c_sample], axis=0), w_ada[0], b_ada[0])
    mod3 = mod.reshape(-1, N_MOD, D_MODEL)

    def pair_bias(heads, halo, dist_scale):
        kb = ATTN_BLOCK + 2 * halo
        tab = _bias_table(heads, ATTN_BLOCK, kb, halo, halo, dist_scale)
        return tab.reshape(-1, 2, ATTN_BLOCK, kb).transpose(0, 2, 1, 3).reshape(-1, ATTN_BLOCK, 2 * kb)

    bias_a = pair_bias(rel_bias[:, :A_Q_HEADS], A_HALF_WINDOW, 1)
    bias_b = []
    for gi, (w, d) in enumerate(B_GROUPS):
        h0 = A_Q_HEADS + gi * B_HEADS_PER_GROUP
        bias_b.append(pair_bias(rel_bias[:, h0:h0 + B_HEADS_PER_GROUP], w // (2 * d), d))
    p = {
        "norm1": norm1[0], "norm2": norm2[0], "final_norm": final_norm,
        "w_in": w_in[0].astype(BF16),
        "bias_a": bias_a,
        "sink_a": jnp.repeat(sink[0].astype(F32), HEAD_DIM).reshape(A_Q_HEADS // 2, 1, LANES),
        "bias_b": bias_b,
        "w_pa": w_pa[0].astype(BF16), "w_pb": w_pb[0].astype(BF16), "w_o": w_o[0].astype(BF16),
        "w_rt": w_router[0].T, "rbias": router_bias[0].reshape(N_EXPERTS, 1),
        "ws_gate": ws_gate[0].astype(BF16), "ws_up": ws_up[0].astype(BF16),
        "ws_down": ws_down[0].astype(BF16),
        "w_gate": w_gate[0], "w_up": w_up[0], "w_down": w_down[0],
    }
    dispatch = lambda st: _sc_dispatch(_slabs(st["hp"]), st["dest_rows"], st["n_rows"])
    st_p = _front(x_prompt, mod3[:nbp], p)
    st_s = _front(x_sample, mod3[nbp:], p, after=dispatch(st_p))
    return (_moe(st_p, st_s["after"], mod3[:nbp], p), _moe(st_s, dispatch(st_s), mod3[nbp:], p))
```

```python
import functools
import math

import jax
import jax.numpy as jnp
import numpy as np
from jax import lax
from jax.experimental import pallas as pl
from jax.experimental.pallas import tpu as pltpu
from jax.experimental.pallas import tpu_sc as plsc

F32 = jnp.float32
BF16 = jnp.bfloat16
U32 = jnp.uint32
I32 = jnp.int32

D_MODEL = 1024
HEAD_DIM = 64
A_Q_HEADS = 8
A_KV_HEADS = 2
A_HALF_WINDOW = 128
B_GROUPS = ((128, 1), (512, 4), (2048, 16))
B_HEADS_PER_GROUP = 4
N_BUCKETS = 32
MAX_DISTANCE = 1024
N_EXPERTS = 64
TOP_K = 6
N_EXPERT_GROUPS = 8
TOPK_GROUPS = 4
D_EXPERT = 256
ROUTED_SCALE = 2.5
RMS_EPS = 1e-6
NEG_INF = -1e30
N_MOD = 6
SH1, SC1, G1, SH2, SC2, G2 = range(N_MOD)
REMOVED = -3e38

A_Q_W = A_Q_HEADS * HEAD_DIM
A_KV_W = A_KV_HEADS * HEAD_DIM
B_W = len(B_GROUPS) * B_HEADS_PER_GROUP * HEAD_DIM
B_OUT_W = B_HEADS_PER_GROUP * HEAD_DIM
D_IN = A_Q_W + 2 * A_KV_W + 3 * B_W + 2 * D_MODEL
HALF_D = D_MODEL // 2
LANES = 128
SUBLANES = 8
ROW_SLABS = HALF_D // LANES
SC_CORES = 2
SC_SUBCORES = 16
SC_CHUNK = 128

TM_INPROJ = 1024
TM_MERGE = 512
ATTN_BLOCK = 128
ATTN_ROWS_A = 512
ATTN_ROWS_B = 2048
MOE_BM = 1024
TM_COMBINE = 512
COLLECT_PARTS = 2
PLAN_TILES = 8
VMEM_LIMIT = 48 * 1024 * 1024
VMEM_LIMIT_INPROJ = 60 * 1024 * 1024


def _cparams(n_axes, vmem_limit=VMEM_LIMIT):
    return pltpu.CompilerParams(
        dimension_semantics=("arbitrary",) * n_axes, vmem_limit_bytes=vmem_limit)


def _ada_kernel(c_ref, w_ref, b_ref, o_ref):
    c = c_ref[...]
    s = c * jax.nn.sigmoid(c)
    o_ref[...] = jnp.dot(s, w_ref[...], preferred_element_type=F32,
                         precision=lax.Precision.HIGHEST) + b_ref[...]


def _ada(c_all, w_ada, b_ada):
    nb = c_all.shape[0]
    return pl.pallas_call(
        _ada_kernel,
        out_shape=jax.ShapeDtypeStruct((nb, N_MOD * D_MODEL), F32),
        grid=(N_MOD,),
        in_specs=[pl.BlockSpec((nb, D_MODEL), lambda j: (0, 0)),
                  pl.BlockSpec((D_MODEL, D_MODEL), lambda j: (0, j)),
                  pl.BlockSpec((1, D_MODEL), lambda j: (0, j))],
        out_specs=pl.BlockSpec((nb, D_MODEL), lambda j: (0, j)),
        compiler_params=_cparams(1),
        name="ada",
    )(c_all, w_ada, b_ada.reshape(1, N_MOD * D_MODEL))


def _rms(x, g):
    return x * lax.rsqrt(jnp.mean(x * x, axis=-1, keepdims=True) + RMS_EPS) * g


def _inproj_kernel(x_ref, mod_ref, n1_ref, w_ref, qa, kva, *rest):
    q_b = rest[0:6:2]
    kv_b = rest[1:6:2]
    sga, sgb, scr = rest[6:]
    mod = mod_ref[0]
    h = _rms(x_ref[...], n1_ref[...]) * (1.0 + mod[SC1:SC1 + 1]) + mod[SH1:SH1 + 1]
    hb = h.astype(BF16)
    tm = hb.shape[0]
    scale = HEAD_DIM ** -0.5

    def proj(off, width):
        return jnp.dot(hb, w_ref[:, off:off + width], preferred_element_type=F32)

    qa[...] = (proj(0, A_Q_W) * scale).astype(BF16)
    for j, off in enumerate((A_Q_W, A_Q_W + A_KV_W)):
        r = proj(off, A_KV_W)
        kva[:, 2 * j * A_KV_W:(2 * j + 1) * A_KV_W] = r.astype(BF16)
        kva[:, (2 * j + 1) * A_KV_W:(2 * j + 2) * A_KV_W] = pltpu.roll(r, HEAD_DIM, 1).astype(BF16)
    off = A_Q_W + 2 * A_KV_W
    for t in range(3):
        for gi, (_, d) in enumerate(B_GROUPS):
            r = proj(off + t * B_W + gi * B_OUT_W, B_OUT_W)
            if t == 0:
                r = r * scale
            ref, width, base = (q_b[gi], B_OUT_W, 0) if t == 0 else (kv_b[gi], 2 * B_OUT_W,
                                                                     (t - 1) * B_OUT_W)
            if d == 1:
                ref[:, base:base + B_OUT_W] = r.astype(BF16)
            else:
                for c in range(B_OUT_W // LANES):
                    slot = (t * 2 + gi - 1) * (B_OUT_W // LANES) + c
                    scr[slot] = r[:, c * LANES:(c + 1) * LANES]
                    for res in range(d):
                        col = res * width + base + c * LANES
                        ref[:, col:col + LANES] = scr[
                            slot, pl.ds(res, tm // d, stride=d), :].astype(BF16)
    off += 3 * B_W
    sga[...] = jax.nn.sigmoid(proj(off, D_MODEL)).astype(BF16)
    sgb[...] = jax.nn.sigmoid(proj(off + D_MODEL, D_MODEL)).astype(BF16)


def _inproj(x2, mod3, norm1, w_in_bf, seq):
    n = x2.shape[0]
    tm = TM_INPROJ
    assert seq % tm == 0 and n % tm == 0
    row = lambda i: (i, 0)
    shapes = [(n, A_Q_W, tm), (n, 4 * A_KV_W, tm)]
    for _, d in B_GROUPS:
        shapes += [(n // d, d * B_OUT_W, tm // d), (n // d, d * 2 * B_OUT_W, tm // d)]
    shapes += [(n, D_MODEL, tm)] * 2
    return pl.pallas_call(
        _inproj_kernel,
        out_shape=[jax.ShapeDtypeStruct((r, c), BF16) for r, c, _ in shapes],
        grid=(n // tm,),
        in_specs=[pl.BlockSpec((tm, D_MODEL), row),
                  pl.BlockSpec((1, N_MOD, D_MODEL), lambda i: (i * tm // seq, 0, 0)),
                  pl.BlockSpec((1, D_MODEL), lambda i: (0, 0)),
                  pl.BlockSpec((D_MODEL, D_IN), lambda i: (0, 0), pipeline_mode=pl.Buffered(1))],
        out_specs=[pl.BlockSpec((b, c), row) for _, c, b in shapes],
        scratch_shapes=[pltpu.VMEM((6 * B_OUT_W // LANES, tm, LANES), F32)],
        compiler_params=_cparams(1, VMEM_LIMIT_INPROJ),
        name="inproj",
    )(x2, mod3, norm1.reshape(1, D_MODEL), w_in_bf)


def _rel_bucket_np(rel):
    half = N_BUCKETS // 2
    max_exact = half // 2
    n = np.abs(rel)
    large = max_exact + (np.log(np.maximum(n, 1) / max_exact) / math.log(MAX_DISTANCE / max_exact)
                         * (half - max_exact)).astype(np.int32)
    large = np.minimum(large, half - 1)
    return ((rel > 0).astype(np.int32) * half + np.where(n < max_exact, n, large)).astype(np.int32)


def _bias_table(rel_bias_heads, n_q, n_k, key_off, band, dist_scale):
    p = n_q + n_k
    rel = np.arange(p) - (n_q - 1) - key_off
    bucket = _rel_bucket_np(rel * dist_scale)
    t = jnp.where((np.abs(rel) <= band)[None], rel_bias_heads.astype(F32)[bucket].T, NEG_INF)
    big = jnp.tile(t, (1, n_q + 1))
    tab = big[:, n_q - 1:n_q - 1 + n_q * (p - 1)].reshape(-1, n_q, p - 1)
    return tab[:, :, :n_k]


def _pair_rhs(k_top, k_bot, v_top, v_bot):
    kb = k_top.shape[0]
    low = jnp.where(lax.broadcasted_iota(I32, (kb, LANES), 1) < HEAD_DIM, 1.0, 0.0).astype(BF16)
    high = jnp.where(lax.broadcasted_iota(I32, (kb, LANES), 1) < HEAD_DIM, 0.0, 1.0).astype(BF16)
    rhs_k = jnp.concatenate([k_top * low, k_bot * high], axis=0)
    rhs_v = jnp.concatenate([jnp.concatenate([v_top * low, low], axis=1),
                             jnp.concatenate([v_bot * high, high], axis=1)], axis=0)
    return rhs_k, rhs_v


def _pair_attention(q_pair, rhs_k, rhs_v, bias_pair, pen, sink_pair, want_lse):
    kb = rhs_k.shape[0] // 2
    s = lax.dot_general(q_pair, rhs_k, (((1,), (1,)), ((), ())), preferred_element_type=F32)
    s = s + bias_pair
    if pen is not None:
        s = s + pen
    s0, s1 = s[:, :kb], s[:, kb:]
    m0 = jnp.max(s0, axis=-1, keepdims=True)
    m1 = jnp.max(s1, axis=-1, keepdims=True)
    if sink_pair is not None:
        m0 = jnp.maximum(m0, sink_pair[:, 0:1])
        m1 = jnp.maximum(m1, sink_pair[:, HEAD_DIM:HEAD_DIM + 1])
    p = jnp.concatenate([jnp.exp(s0 - m0), jnp.exp(s1 - m1)], axis=1).astype(BF16)
    od = jnp.dot(p, rhs_v, preferred_element_type=F32)
    o, den = od[:, :LANES], od[:, LANES:]
    low = lax.broadcasted_iota(I32, o.shape, 1) < HEAD_DIM
    m_full = jnp.where(low, m0, m1)
    if sink_pair is not None:
        den = den + jnp.exp(sink_pair - m_full)
    return o / den, (m_full + jnp.log(den)) if want_lse else None


def _attn_kernel(*refs, halo, nblk, n_pairs, n_seqs, shared_kv, has_sink, want_lse, n_steps):
    it = iter(refs)
    q_ref = next(it)
    kv_prev, kv_cur, kv_next = next(it), next(it), next(it)
    bias_ref = next(it)
    sink_ref = next(it) if has_sink else None
    o_ref = next(it)
    lse_ref = next(it) if want_lse else None
    step = pl.program_id(2)
    blk = ATTN_BLOCK
    kb = blk + 2 * halo
    qw = n_pairs * LANES
    kv = jnp.concatenate([kv_prev[...], kv_cur[...], kv_next[...]], axis=0)
    col = lax.broadcasted_iota(I32, (1, kb), 1)
    for b in range(nblk):
        lo = jnp.where(step == 0, halo, 0) if b == 0 else 0
        hi = jnp.where(step == n_steps - 1, kb - halo, kb) if b == nblk - 1 else kb
        pen = None
        if b == 0 or b == nblk - 1:
            pen1 = jnp.where((col >= lo) & (col < hi), 0.0, NEG_INF)
            pen = jnp.concatenate([pen1, pen1], axis=1)
        rows = slice(b * blk, (b + 1) * blk)
        krows = slice(b * blk, b * blk + kb)
        lane_col = lambda j: kv[krows, j * LANES:(j + 1) * LANES]
        if shared_kv:
            k, k_sw, v, v_sw = (lane_col(j) for j in range(4))
            rhs = [_pair_rhs(k, k_sw, v, v_sw), _pair_rhs(k_sw, k, v_sw, v)]
        for s in range(n_seqs):
            for c in range(n_pairs):
                lanes = slice(s * qw + c * LANES, s * qw + (c + 1) * LANES)
                if shared_kv:
                    rhs_k, rhs_v = rhs[c // (n_pairs // 2)]
                else:
                    k, v = lane_col(2 * s * n_pairs + c), lane_col((2 * s + 1) * n_pairs + c)
                    rhs_k, rhs_v = _pair_rhs(k, k, v, v)
                o, lse = _pair_attention(q_ref[rows, lanes], rhs_k, rhs_v, bias_ref[c], pen,
                                         sink_ref[c] if has_sink else None, want_lse)
                o_ref[rows, lanes] = o.astype(o_ref.dtype)
                if want_lse:
                    lse_ref[rows, lanes] = lse


def _band_attention(q, kv, bias, sink, *, nbatch, seq, dil, halo, step_rows, shared_kv, want_lse):
    n = nbatch * seq
    sub_len = seq // dil
    qw = q.shape[1] // dil
    kv_width = kv.shape[1] // dil
    rows = min(step_rows, sub_len)
    n_seqs = min(dil, step_rows // rows)
    assert sub_len % rows == 0 and rows % ATTN_BLOCK == 0 and rows % halo == 0 and dil % n_seqs == 0
    nq = sub_len // rows
    per = rows // halo
    total_halos = n // dil // halo
    cur = lambda b, r, i: (b * nq + i, r)
    prev = lambda b, r, i: (jnp.maximum((b * nq + i) * per - 1, 0), r)
    nxt = lambda b, r, i: (jnp.minimum((b * nq + i + 1) * per, total_halos - 1), r)
    const = lambda a: pl.BlockSpec(a.shape, lambda b, r, i: (0,) * a.ndim)
    qw_step, kvw_step = qw * n_seqs, kv_width * n_seqs

    in_specs = [pl.BlockSpec((rows, qw_step), cur), pl.BlockSpec((halo, kvw_step), prev),
                pl.BlockSpec((rows, kvw_step), cur), pl.BlockSpec((halo, kvw_step), nxt),
                const(bias)]
    args = [q, kv, kv, kv, bias]
    if sink is not None:
        in_specs.append(const(sink))
        args.append(sink)
    out_shape = [jax.ShapeDtypeStruct(q.shape, BF16)]
    out_specs = [pl.BlockSpec((rows, qw_step), cur)]
    if want_lse:
        out_shape.append(jax.ShapeDtypeStruct(q.shape, F32))
        out_specs.append(pl.BlockSpec((rows, qw_step), cur))
    return pl.pallas_call(
        functools.partial(_attn_kernel, halo=halo, nblk=rows // ATTN_BLOCK, n_pairs=qw // LANES,
                          n_seqs=n_seqs, shared_kv=shared_kv, has_sink=sink is not None,
                          want_lse=want_lse, n_steps=nq),
        out_shape=out_shape,
        grid=(nbatch, dil // n_seqs, nq),
        in_specs=in_specs,
        out_specs=out_specs,
        compiler_params=_cparams(3),
        name=f"band_attn_d{dil}",
    )(*args)


def _pack_rows(y):
    return pltpu.pack_elementwise([y[:, :HALF_D], y[:, HALF_D:]], packed_dtype=BF16)


def _store_slabs(ref, packed):
    t = packed.shape[0]
    for c in range(ROW_SLABS):
        ref[pl.ds(c, t, stride=ROW_SLABS), :] = packed[:, c * LANES:(c + 1) * LANES]


def _load_slabs(ref):
    t = ref.shape[0] // ROW_SLABS
    return jnp.concatenate([ref[pl.ds(c, t, stride=ROW_SLABS), :] for c in range(ROW_SLABS)], axis=1)


def _unpack_rows(p):
    return tuple(pltpu.unpack_elementwise(p, index=i, packed_dtype=BF16, unpacked_dtype=F32)
                 for i in range(2))


def _route(sel, scores):
    t = sel.shape[-1]
    per = N_EXPERTS // N_EXPERT_GROUPS
    shape3 = (N_EXPERT_GROUPS, per, t)
    sel3 = sel.reshape(shape3)
    sc3 = scores.reshape(shape3)
    iota_g = lax.broadcasted_iota(I32, shape3, 0)
    iota_m = lax.broadcasted_iota(I32, shape3, 1)
    iota_e = iota_g * per + iota_m
    m1 = jnp.max(sel3, axis=1, keepdims=True)
    i1 = jnp.min(jnp.where(sel3 == m1, iota_m, per), axis=1, keepdims=True)
    m2 = jnp.max(jnp.where(iota_m == i1, REMOVED, sel3), axis=1, keepdims=True)
    gscore = m1 + m2
    iota_g1 = lax.broadcasted_iota(I32, gscore.shape, 0)
    gmask = jnp.zeros(gscore.shape, jnp.bool_)
    for _ in range(TOPK_GROUPS):
        mx = jnp.max(gscore, axis=0, keepdims=True)
        ix = jnp.min(jnp.where(gscore == mx, iota_g1, N_EXPERT_GROUPS), axis=0, keepdims=True)
        hit = iota_g1 == ix
        gmask = gmask | hit
        gscore = jnp.where(hit, REMOVED, gscore)
    cur = jnp.where(gmask, sel3, NEG_INF)
    sum_all = lambda a: jnp.sum(jnp.sum(a, axis=1, keepdims=True), axis=0, keepdims=True)
    idxs, wts, hits = [], [], []
    for _ in range(TOP_K):
        mx = jnp.max(jnp.max(cur, axis=1, keepdims=True), axis=0, keepdims=True)
        cand = jnp.where(cur == mx, iota_e, N_EXPERTS)
        ix = jnp.min(jnp.min(cand, axis=1, keepdims=True), axis=0, keepdims=True)
        hit = iota_e == ix
        wts.append(sum_all(jnp.where(hit, sc3, 0.0)).reshape(1, t))
        cur = jnp.where(hit, REMOVED, cur)
        idxs.append(ix.reshape(1, t))
        hits.append(hit)
    wsum = wts[0]
    for w in wts[1:]:
        wsum = wsum + w
    wts = [w / wsum * ROUTED_SCALE for w in wts]
    onehot = hits[0].astype(F32)
    for hit in hits[1:]:
        onehot = onehot + hit.astype(F32)
    tri = (lax.broadcasted_iota(I32, (t, t), 0) <= lax.broadcasted_iota(I32, (t, t), 1))
    cum = jnp.dot(onehot.reshape(N_EXPERTS, t).astype(BF16), tri.astype(F32).astype(BF16),
                  preferred_element_type=F32)
    cum3 = cum.reshape(shape3) - 1.0
    ranks = [sum_all(jnp.where(hit, cum3, 0.0)).reshape(1, t).astype(I32) for hit in hits]
    counts = cum[:, t - 1:t].astype(I32)
    return idxs, wts, ranks, counts


def _merge_kernel(x_ref, ya_ref, o1, o2, o3, l1, l2, l3, sga_ref, sgb_ref, mod_ref, n2_ref,
                  wpa, wpb, wo, wrt, rbias, wsg, wsu, wsd,
                  base_ref, hp_ref, ridx_ref, rw_ref, rank_ref, cnt_ref, scr):
    mod = mod_ref[0]
    tm = x_ref.shape[0]

    def token_major(ref, slot, d):
        if d == 1:
            return ref[...].astype(F32)
        nc = B_OUT_W // LANES
        for c in range(nc):
            for res in range(d):
                col = res * B_OUT_W + c * LANES
                scr[slot * nc + c, pl.ds(res, tm // d, stride=d), :] = ref[
                    :, col:col + LANES].astype(F32)
        return jnp.concatenate([scr[slot * nc + c] for c in range(nc)], axis=1)

    dils = [d for _, d in B_GROUPS]
    os_ = [token_major(r, i, d) for i, (r, d) in enumerate(zip((o1, o2, o3), dils))]
    ls = [token_major(r, 3 + i, d) for i, (r, d) in enumerate(zip((l1, l2, l3), dils))]
    mx = jnp.maximum(jnp.maximum(ls[0], ls[1]), ls[2])
    es = [jnp.exp(l - mx) for l in ls]
    den = es[0] + es[1] + es[2]
    ob = (es[0] / den) * os_[0] + (es[1] / den) * os_[1] + (es[2] / den) * os_[2]
    pa = jnp.dot(ya_ref[...], wpa[...], preferred_element_type=F32)
    pb = jnp.dot(ob.astype(BF16), wpb[...], preferred_element_type=F32)
    merged = sga_ref[...].astype(F32) * pa + sgb_ref[...].astype(F32) * pb
    mix = jnp.dot(merged.astype(BF16), wo[...], preferred_element_type=F32)
    x1 = x_ref[...] + mod[G1:G1 + 1] * mix
    h2 = _rms(x1, n2_ref[...]) * (1.0 + mod[SC2:SC2 + 1]) + mod[SH2:SH2 + 1]
    logits = lax.dot_general(wrt[...], h2, (((1,), (1,)), ((), ())), preferred_element_type=F32,
                             precision=lax.Precision.HIGHEST)
    scores = jax.nn.sigmoid(logits)
    idxs, wts, ranks, counts = _route(scores + rbias[...], scores)
    for k in range(TOP_K):
        ridx_ref[k:k + 1, :] = idxs[k]
        rank_ref[k:k + 1, :] = ranks[k]
    ridx_ref[TOP_K:, :] = jnp.zeros((SUBLANES - TOP_K, tm), I32)
    rw_ref[...] = jnp.transpose(jnp.concatenate(
        wts + [jnp.zeros((SUBLANES - TOP_K, tm), F32)], axis=0))
    rank_ref[TOP_K:, :] = jnp.zeros((SUBLANES - TOP_K, tm), I32)
    cnt_ref[0] = counts
    hb = h2.astype(BF16)
    g = jnp.dot(hb, wsg[...], preferred_element_type=F32)
    u = jnp.dot(hb, wsu[...], preferred_element_type=F32)
    act = (g * jax.nn.sigmoid(g) * u).astype(BF16)
    shared = jnp.dot(act, wsd[...], preferred_element_type=F32)
    base_ref[...] = x1 + mod[G2:G2 + 1] * shared
    _store_slabs(hp_ref, _pack_rows(h2))


def _merge(x2, ya, outs, lses, sga, sgb, mod3, norm2, wpa, wpb, wo, wrt, rbias, wsg, wsu, wsd, seq):
    n = x2.shape[0]
    tm = TM_MERGE
    assert seq % tm == 0
    row = lambda i: (i, 0)
    full = lambda a: pl.BlockSpec(a.shape, lambda i: (0,) * a.ndim)
    weights = [wpa, wpb, wo, wrt, rbias, wsg, wsu, wsd]
    group_specs = [pl.BlockSpec((tm // d, d * B_OUT_W), row) for _, d in B_GROUPS]
    lanes = lambda i: (0, i)
    return pl.pallas_call(
        _merge_kernel,
        out_shape=[jax.ShapeDtypeStruct((n, D_MODEL), F32),
                   jax.ShapeDtypeStruct((n * ROW_SLABS, LANES), U32),
                   jax.ShapeDtypeStruct((SUBLANES, n), I32),
                   jax.ShapeDtypeStruct((n, SUBLANES), F32),
                   jax.ShapeDtypeStruct((SUBLANES, n), I32),
                   jax.ShapeDtypeStruct((n // tm, N_EXPERTS, 1), I32)],
        grid=(n // tm,),
        in_specs=[pl.BlockSpec((tm, D_MODEL), row), pl.BlockSpec((tm, A_Q_W), row)]
                 + group_specs * 2
                 + [pl.BlockSpec((tm, D_MODEL), row)] * 2
                 + [pl.BlockSpec((1, N_MOD, D_MODEL), lambda i: (i * tm // seq, 0, 0)),
                    pl.BlockSpec((1, D_MODEL), lambda i: (0, 0))]
                 + [full(w) for w in weights],
        out_specs=[pl.BlockSpec((tm, D_MODEL), row),
                   pl.BlockSpec((tm * ROW_SLABS, LANES), row),
                   pl.BlockSpec((SUBLANES, tm), lanes), pl.BlockSpec((tm, SUBLANES), row),
                   pl.BlockSpec((SUBLANES, tm), lanes),
                   pl.BlockSpec((1, N_EXPERTS, 1), lambda i: (i, 0, 0))],
        scratch_shapes=[pltpu.VMEM((6 * B_OUT_W // LANES, tm, LANES), F32)],
        compiler_params=_cparams(1),
        name="merge_route",
    )(x2, ya, *outs, *lses, sga, sgb, mod3, norm2.reshape(1, D_MODEL), *weights)


def _plan_kernel(ridx_ref, rank_ref, tb_ref, dest_ref, *, tm):
    iota_e = lax.broadcasted_iota(I32, (N_EXPERTS, tm), 0)
    for j in range(dest_ref.shape[0]):
        tb = tb_ref[j]
        cols = slice(j * tm, (j + 1) * tm)
        for k in range(TOP_K):
            base = jnp.sum(jnp.where(iota_e == ridx_ref[k:k + 1, cols], tb, 0), axis=0, keepdims=True)
            dest_ref[j, :, k * tm:(k + 1) * tm] = base + rank_ref[k:k + 1, cols]


def _plan(ridx, rank, tile_base, tm):
    n = ridx.shape[1]
    tiles = math.gcd(PLAN_TILES, n // tm)
    lanes = lambda i: (0, i)
    return pl.pallas_call(
        functools.partial(_plan_kernel, tm=tm),
        out_shape=jax.ShapeDtypeStruct((n // tm, 1, TOP_K * tm), I32),
        grid=(n // tm // tiles,),
        in_specs=[pl.BlockSpec((SUBLANES, tiles * tm), lanes), pl.BlockSpec((SUBLANES, tiles * tm), lanes),
                  pl.BlockSpec((tiles, N_EXPERTS, 1), lambda i: (i, 0, 0))],
        out_specs=pl.BlockSpec((tiles, 1, TOP_K * tm), lambda i: (i, 0, 0)),
        compiler_params=_cparams(1),
        name="moe_plan",
    )(ridx, rank, tile_base)


def _sc_mesh():
    return plsc.VectorSubcoreMesh(core_axis_name="core", subcore_axis_name="subcore",
                                  num_cores=SC_CORES, num_subcores=SC_SUBCORES)


def _sc_worker():
    return lax.axis_index("subcore") * SC_CORES + lax.axis_index("core")


def _sc_dispatch(hp, dest_rows, n_rows):
    n = hp.shape[0]
    halves = TM_MERGE // SC_CHUNK
    workers = SC_CORES * SC_SUBCORES
    tiles_per_w = n // TM_MERGE // workers
    idx_per_w = tiles_per_w * TOP_K * halves
    assert tiles_per_w * workers * TM_MERGE == n

    @functools.partial(
        pl.kernel, mesh=_sc_mesh(),
        out_type=jax.ShapeDtypeStruct((n_rows, ROW_SLABS, LANES), U32),
        scratch_types=[pltpu.VMEM((idx_per_w, SC_CHUNK), I32),
                       pltpu.VMEM((SC_CHUNK, ROW_SLABS, LANES), U32),
                       pltpu.SemaphoreType.DMA],
        name="moe_dispatch_sc")
    def scatter(hp_hbm, idx_hbm, xs_hbm, idx_v, rows_v, sem):
        wid = _sc_worker()
        pltpu.sync_copy(idx_hbm.at[pl.ds(wid * idx_per_w, idx_per_w)], idx_v)

        @pl.loop(0, tiles_per_w * halves)
        def _(j):
            tile = j // halves
            h = j - tile * halves
            tok = (wid * tiles_per_w + tile) * TM_MERGE + h * SC_CHUNK
            pltpu.sync_copy(hp_hbm.at[pl.ds(tok, SC_CHUNK)], rows_v)
            for k in range(TOP_K):
                row = (tile * TOP_K + k) * halves + h
                pltpu.async_copy(rows_v, xs_hbm.at[idx_v.at[row]], sem).wait()

    return scatter(hp, dest_rows)


def _sc_collect(ys, dest_rows):
    n_chunks = dest_rows.shape[0]
    workers = SC_CORES * SC_SUBCORES
    per_w = n_chunks // workers
    assert per_w * workers == n_chunks

    @functools.partial(
        pl.kernel, mesh=_sc_mesh(),
        out_type=jax.ShapeDtypeStruct((n_chunks * SC_CHUNK, ROW_SLABS, LANES), U32),
        scratch_types=[pltpu.VMEM((per_w, SC_CHUNK), I32),
                       pltpu.VMEM((SC_CHUNK, ROW_SLABS, LANES), U32),
                       pltpu.SemaphoreType.DMA],
        name="moe_collect_sc")
    def gather(ys_hbm, idx_hbm, out_hbm, idx_v, rows_v, sem):
        wid = _sc_worker()
        base = wid * per_w
        pltpu.sync_copy(idx_hbm.at[pl.ds(base, per_w)], idx_v)

        @pl.loop(0, per_w)
        def _(j):
            pltpu.async_copy(ys_hbm.at[idx_v.at[j]], rows_v, sem).wait()
            pltpu.sync_copy(rows_v, out_hbm.at[pl.ds((base + j) * SC_CHUNK, SC_CHUNK)])

    return gather(ys, dest_rows)


def _expert_kernel(blk_e_ref, valid_ref, src_ref, xs_ref, wg_ref, wu_ref, wd_ref, ys_ref,
                   wg_bf, wu_bf, wd_bf):
    del src_ref
    step = pl.program_id(0)
    valid = valid_ref[step]

    @pl.when((step == 0) | (blk_e_ref[step] != blk_e_ref[jnp.maximum(step - 1, 0)]))
    def _():
        wg_bf[...] = wg_ref[0].astype(BF16)
        wu_bf[...] = wu_ref[0].astype(BF16)
        wd_bf[...] = wd_ref[0].astype(BF16)

    @pl.when(valid > 0)
    def _():
        rows = lax.broadcasted_iota(I32, (xs_ref.shape[0] // ROW_SLABS, 1), 0)
        lo, hi = _unpack_rows(jnp.where(rows < valid, _load_slabs(xs_ref), jnp.uint32(0)))
        lo, hi = lo.astype(BF16), hi.astype(BF16)

        def up(w_bf):
            return jnp.dot(lo, w_bf[:HALF_D], preferred_element_type=F32) \
                + jnp.dot(hi, w_bf[HALF_D:], preferred_element_type=F32)

        g = up(wg_bf)
        act = (g * jax.nn.sigmoid(g) * up(wu_bf)).astype(BF16)
        _store_slabs(ys_ref, _pack_rows(jnp.dot(act, wd_bf[...], preferred_element_type=F32)))


def _experts(xs, blk_e, blk_valid, blk_src, wg, wu, wd):
    n_rows = xs.shape[0] // ROW_SLABS
    bm = MOE_BM
    nblk = n_rows // bm
    return pl.pallas_call(
        _expert_kernel,
        out_shape=jax.ShapeDtypeStruct((n_rows * ROW_SLABS, LANES), U32),
        grid_spec=pltpu.PrefetchScalarGridSpec(
            num_scalar_prefetch=3,
            grid=(nblk,),
            in_specs=[pl.BlockSpec((bm * ROW_SLABS, LANES), lambda i, be, bv, bs: (bs[i], 0)),
                      pl.BlockSpec((1, D_MODEL, D_EXPERT), lambda i, be, bv, bs: (be[i], 0, 0)),
                      pl.BlockSpec((1, D_MODEL, D_EXPERT), lambda i, be, bv, bs: (be[i], 0, 0)),
                      pl.BlockSpec((1, D_EXPERT, D_MODEL), lambda i, be, bv, bs: (be[i], 0, 0))],
            out_specs=pl.BlockSpec((bm * ROW_SLABS, LANES), lambda i, be, bv, bs: (bs[i], 0)),
            scratch_shapes=[pltpu.VMEM((D_MODEL, D_EXPERT), BF16), pltpu.VMEM((D_MODEL, D_EXPERT), BF16),
                            pltpu.VMEM((D_EXPERT, D_MODEL), BF16)]),
        compiler_params=_cparams(1),
        name="moe_experts",
    )(blk_e, blk_valid, blk_src, xs, wg, wu, wd)


def _combine_kernel(g_ref, base_ref, rw_ref, mod_ref, fn_ref, *rest):
    y_ref = rest[-1]
    tm = base_ref.shape[0]
    rw = rw_ref[...]
    acc_lo = jnp.zeros((tm, HALF_D), F32)
    acc_hi = jnp.zeros((tm, HALF_D), F32)
    for k in range(TOP_K):
        lo, hi = _unpack_rows(_load_slabs(g_ref.at[0, k]))
        w = rw[:, k:k + 1]
        acc_lo = acc_lo + w * lo
        acc_hi = acc_hi + w * hi
    routed = jnp.concatenate([acc_lo, acc_hi], axis=1)
    x2 = base_ref[...] + mod_ref[0][G2:G2 + 1] * routed
    y_ref[...] = _rms(x2, fn_ref[...])


def _combine(g, base, rw_t, mod3, final_norm, seq, part, y_prev):
    n = base.shape[0]
    tm = TM_COMBINE
    assert seq % tm == 0
    g4 = g.reshape(-1, TOP_K, tm * ROW_SLABS, LANES)
    tiles = g4.shape[0]
    off = part * tiles
    row = lambda i: (i + off, 0)
    in_specs = [pl.BlockSpec((1, TOP_K, tm * ROW_SLABS, LANES), lambda i: (i, 0, 0, 0)),
                pl.BlockSpec((tm, D_MODEL), row),
                pl.BlockSpec((tm, SUBLANES), row),
                pl.BlockSpec((1, N_MOD, D_MODEL), lambda i: ((i + off) * tm // seq, 0, 0)),
                pl.BlockSpec((1, D_MODEL), lambda i: (0, 0))]
    args = [g4, base, rw_t, mod3, final_norm.reshape(1, D_MODEL)]
    aliases = {}
    if y_prev is not None:
        in_specs.append(pl.BlockSpec(memory_space=pl.ANY))
        args.append(y_prev)
        aliases = {len(args) - 1: 0}
    return pl.pallas_call(
        _combine_kernel,
        out_shape=jax.ShapeDtypeStruct((n, D_MODEL), F32),
        grid=(tiles,),
        in_specs=in_specs,
        out_specs=pl.BlockSpec((tm, D_MODEL), row),
        input_output_aliases=aliases,
        compiler_params=_cparams(1),
        name="moe_combine",
    )(*args)


def _block_layout(tile_counts, n):
    bm = MOE_BM
    c = tile_counts[:, :, 0]
    ntiles = c.shape[0]
    counts = jnp.sum(c, axis=0)
    padded = (counts + bm - 1) // bm * bm
    earlier_e = np.tri(N_EXPERTS, k=-1, dtype=bool)
    pstart = jnp.sum(jnp.where(earlier_e, padded[None, :], 0), axis=1)
    pend = pstart + padded
    earlier_t = np.tri(ntiles, k=-1, dtype=bool)
    tile_base = pstart[None, :] + jnp.sum(jnp.where(earlier_t[:, :, None], c[None], 0), axis=1)
    nblk = -(-n * TOP_K // bm) + N_EXPERTS
    blk = jnp.arange(nblk, dtype=I32)
    blk_e = jnp.minimum(jnp.sum((pend[None, :] <= blk[:, None] * bm).astype(I32), axis=1),
                        N_EXPERTS - 1)
    onehot_e = blk_e[:, None] == jnp.arange(N_EXPERTS, dtype=I32)[None, :]
    end_e = jnp.sum(jnp.where(onehot_e, (pstart + counts)[None, :], 0), axis=1)
    blk_valid = jnp.clip(end_e - blk * bm, 0, bm).astype(I32)
    blk_src = jnp.minimum(blk, pend[N_EXPERTS - 1] // bm - 1).astype(I32)
    last_e = jnp.max(jnp.where(counts > 0, jnp.arange(N_EXPERTS, dtype=I32), 0))
    blk_e = jnp.where(blk == blk_src, blk_e, last_e).astype(I32)
    return tile_base.astype(I32)[:, :, None], blk_e, blk_valid, blk_src, nblk * bm


def _front(x, mod3, p, after=None):
    nbatch, seq, _ = x.shape
    n = nbatch * seq
    x2 = x.reshape(n, D_MODEL)
    proj = _inproj(x2, mod3, p["norm1"], p["w_in"], seq)
    qa, kva = proj[:2]
    sga, sgb = proj[8:]
    (ya,) = _band_attention(qa, kva, p["bias_a"], p["sink_a"], nbatch=nbatch, seq=seq, dil=1,
                            halo=A_HALF_WINDOW, step_rows=ATTN_ROWS_A, shared_kv=True, want_lse=False)
    outs, lses = [], []
    for gi, (w, d) in enumerate(B_GROUPS):
        q, kv = proj[2 + 2 * gi:4 + 2 * gi]
        o, lse = _band_attention(q, kv, p["bias_b"][gi], None, nbatch=nbatch, seq=seq, dil=d,
                                 halo=w // (2 * d), step_rows=ATTN_ROWS_B, shared_kv=False,
                                 want_lse=True)
        outs.append(o)
        lses.append(lse)
    if after is not None:
        ya, after = lax.optimization_barrier((ya, after))
    base, hp, ridx, rw, rank, tile_counts = _merge(
        x2, ya, outs, lses, sga, sgb, mod3, p["norm2"], p["w_pa"], p["w_pb"], p["w_o"], p["w_rt"],
        p["rbias"], p["ws_gate"], p["ws_up"], p["ws_down"], seq)
    assert TM_MERGE == TM_COMBINE
    tile_base, blk_e, blk_valid, blk_src, n_rows = _block_layout(tile_counts, n)
    dest_rows = _plan(ridx, rank, tile_base, TM_MERGE).reshape(-1, SC_CHUNK)
    return dict(base=base, hp=hp, rw=rw, dest_rows=dest_rows, blk_e=blk_e, blk_valid=blk_valid,
                blk_src=blk_src, n_rows=n_rows, shape=x.shape, after=after)


def _slabs(a):
    return a.reshape(-1, ROW_SLABS, LANES)


def _moe(st, xs, mod3, p):
    nbatch, seq, _ = st["shape"]
    dest_rows = st["dest_rows"]
    ys = _experts(xs.reshape(-1, LANES), st["blk_e"], st["blk_valid"], st["blk_src"], p["w_gate"], p["w_up"],
                  p["w_down"])
    rows_per_part = dest_rows.shape[0] // COLLECT_PARTS
    rw_t = st["rw"]
    y = None
    for part in range(COLLECT_PARTS):
        g = _sc_collect(_slabs(ys), dest_rows[part * rows_per_part:(part + 1) * rows_per_part])
        y = _combine(g, st["base"], rw_t, mod3, p["final_norm"], seq, part, y)
    return y.reshape(nbatch, seq, D_MODEL)


def kernel(x_prompt, x_sample, c_prompt, c_sample, rel_bias, w_ada, b_ada, norm1, w_in, sink, w_pa, w_pb, w_o, norm2, w_router, router_bias, w_gate, w_up, w_down, ws_gate, ws_up, ws_down, final_norm):
    assert w_ada.shape[0] == 1
    nbp = x_prompt.shape[0]
    mod = _ada(jnp.concatenate([c_prompt, c_sample], axis=0), w_ada[0], b_ada[0])
    mod3 = mod.reshape(-1, N_MOD, D_MODEL)

    def pair_bias(heads, halo, dist_scale):
        kb = ATTN_BLOCK + 2 * halo
        tab = _bias_table(heads, ATTN_BLOCK, kb, halo, halo, dist_scale)
        return tab.reshape(-1, 2, ATTN_BLOCK, kb).transpose(0, 2, 1, 3).reshape(-1, ATTN_BLOCK, 2 * kb)

    bias_a = pair_bias(rel_bias[:, :A_Q_HEADS], A_HALF_WINDOW, 1)
    bias_b = []
    for gi, (w, d) in enumerate(B_GROUPS):
        h0 = A_Q_HEADS + gi * B_HEADS_PER_GROUP
        bias_b.append(pair_bias(rel_bias[:, h0:h0 + B_HEADS_PER_GROUP], w // (2 * d), d))
    p = {
        "norm1": norm1[0], "norm2": norm2[0], "final_norm": final_norm,
        "w_in": w_in[0].astype(BF16),
        "bias_a": bias_a,
        "sink_a": jnp.repeat(sink[0].astype(F32), HEAD_DIM).reshape(A_Q_HEADS // 2, 1, LANES),
        "bias_b": bias_b,
        "w_pa": w_pa[0].astype(BF16), "w_pb": w_pb[0].astype(BF16), "w_o": w_o[0].astype(BF16),
        "w_rt": w_router[0].T, "rbias": router_bias[0].reshape(N_EXPERTS, 1),
        "ws_gate": ws_gate[0].astype(BF16), "ws_up": ws_up[0].astype(BF16),
        "ws_down": ws_down[0].astype(BF16),
        "w_gate": w_gate[0], "w_up": w_up[0], "w_down": w_down[0],
    }
    dispatch = lambda st: _sc_dispatch(_slabs(st["hp"]), st["dest_rows"], st["n_rows"])
    st_p = _front(x_prompt, mod3[:nbp], p)
    st_s = _front(x_sample, mod3[nbp:], p, after=dispatch(st_p))
    return (_moe(st_p, st_s["after"], mod3[:nbp], p), _moe(st_s, dispatch(st_s), mod3[nbp:], p))
```

```python
import functools
import math

import jax
import jax.numpy as jnp
import numpy as np
from jax import lax
from jax.experimental import pallas as pl
from jax.experimental.pallas import tpu as pltpu
from jax.experimental.pallas import tpu_sc as plsc

F32 = jnp.float32
BF16 = jnp.bfloat16
U32 = jnp.uint32
I32 = jnp.int32

D_MODEL = 1024
HEAD_DIM = 64
A_Q_HEADS = 8
A_KV_HEADS = 2
A_HALF_WINDOW = 128
B_GROUPS = ((128, 1), (512, 4), (2048, 16))
B_HEADS_PER_GROUP = 4
N_BUCKETS = 32
MAX_DISTANCE = 1024
N_EXPERTS = 64
TOP_K = 6
N_EXPERT_GROUPS = 8
TOPK_GROUPS = 4
D_EXPERT = 256
ROUTED_SCALE = 2.5
RMS_EPS = 1e-6
NEG_INF = -1e30
N_MOD = 6
SH1, SC1, G1, SH2, SC2, G2 = range(N_MOD)
REMOVED = -3e38

A_Q_W = A_Q_HEADS * HEAD_DIM
A_KV_W = A_KV_HEADS * HEAD_DIM
B_W = len(B_GROUPS) * B_HEADS_PER_GROUP * HEAD_DIM
B_OUT_W = B_HEADS_PER_GROUP * HEAD_DIM
D_IN = A_Q_W + 2 * A_KV_W + 3 * B_W + 2 * D_MODEL
HALF_D = D_MODEL // 2
LANES = 128
SUBLANES = 8
ROW_SLABS = HALF_D // LANES
SC_CORES = 2
SC_SUBCORES = 16
SC_CHUNK = 128

TM_INPROJ = 1024
TM_MERGE = 512
ATTN_BLOCK = 128
ATTN_ROWS_A = 512
ATTN_ROWS_B = 4096
MOE_BM = 1024
COMBINE_TILES = 2
COLLECT_PARTS = 2
PLAN_TILES = 8
VMEM_LIMIT = 48 * 1024 * 1024
VMEM_LIMIT_INPROJ = 60 * 1024 * 1024


def _cparams(n_axes, vmem_limit=VMEM_LIMIT):
    return pltpu.CompilerParams(
        dimension_semantics=("arbitrary",) * n_axes, vmem_limit_bytes=vmem_limit)


def _ada_kernel(c_ref, w_ref, b_ref, o_ref):
    c = c_ref[...]
    s = c * jax.nn.sigmoid(c)
    o_ref[...] = jnp.dot(s, w_ref[...], preferred_element_type=F32,
                         precision=lax.Precision.HIGHEST) + b_ref[...]


def _ada(c_all, w_ada, b_ada):
    nb = c_all.shape[0]
    return pl.pallas_call(
        _ada_kernel,
        out_shape=jax.ShapeDtypeStruct((nb, N_MOD * D_MODEL), F32),
        grid=(N_MOD,),
        in_specs=[pl.BlockSpec((nb, D_MODEL), lambda j: (0, 0)),
                  pl.BlockSpec((D_MODEL, D_MODEL), lambda j: (0, j)),
                  pl.BlockSpec((1, D_MODEL), lambda j: (0, j))],
        out_specs=pl.BlockSpec((nb, D_MODEL), lambda j: (0, j)),
        compiler_params=_cparams(1),
        name="ada",
    )(c_all, w_ada, b_ada.reshape(1, N_MOD * D_MODEL))


def _rms(x, g):
    return x * lax.rsqrt(jnp.mean(x * x, axis=-1, keepdims=True) + RMS_EPS) * g


def _inproj_kernel(x_ref, mod_ref, n1_ref, w_ref, qa, kva, *rest):
    q_b = rest[0:6:2]
    kv_b = rest[1:6:2]
    sga, sgb, scr = rest[6:]
    mod = mod_ref[0]
    h = _rms(x_ref[...], n1_ref[...]) * (1.0 + mod[SC1:SC1 + 1]) + mod[SH1:SH1 + 1]
    hb = h.astype(BF16)
    tm = hb.shape[0]
    scale = HEAD_DIM ** -0.5

    def proj(off, width):
        return jnp.dot(hb, w_ref[:, off:off + width], preferred_element_type=F32)

    qa[...] = (proj(0, A_Q_W) * scale).astype(BF16)
    for j, off in enumerate((A_Q_W, A_Q_W + A_KV_W)):
        r = proj(off, A_KV_W)
        kva[:, 2 * j * A_KV_W:(2 * j + 1) * A_KV_W] = r.astype(BF16)
        kva[:, (2 * j + 1) * A_KV_W:(2 * j + 2) * A_KV_W] = pltpu.roll(r, HEAD_DIM, 1).astype(BF16)
    off = A_Q_W + 2 * A_KV_W
    for t in range(3):
        for gi, (_, d) in enumerate(B_GROUPS):
            r = proj(off + t * B_W + gi * B_OUT_W, B_OUT_W)
            if t == 0:
                r = r * scale
            ref, width, base = (q_b[gi], B_OUT_W, 0) if t == 0 else (kv_b[gi], 2 * B_OUT_W,
                                                                     (t - 1) * B_OUT_W)
            if d == 1:
                ref[:, base:base + B_OUT_W] = r.astype(BF16)
            else:
                for c in range(B_OUT_W // LANES):
                    slot = (t * 2 + gi - 1) * (B_OUT_W // LANES) + c
                    scr[slot] = r[:, c * LANES:(c + 1) * LANES]
                    for res in range(d):
                        col = res * width + base + c * LANES
                        ref[:, col:col + LANES] = scr[
                            slot, pl.ds(res, tm // d, stride=d), :].astype(BF16)
    off += 3 * B_W
    sga[...] = jax.nn.sigmoid(proj(off, D_MODEL)).astype(BF16)
    sgb[...] = jax.nn.sigmoid(proj(off + D_MODEL, D_MODEL)).astype(BF16)


def _inproj(x2, mod3, norm1, w_in_bf, seq):
    n = x2.shape[0]
    tm = TM_INPROJ
    assert seq % tm == 0 and n % tm == 0
    row = lambda i: (i, 0)
    shapes = [(n, A_Q_W, tm), (n, 4 * A_KV_W, tm)]
    for _, d in B_GROUPS:
        shapes += [(n // d, d * B_OUT_W, tm // d), (n // d, d * 2 * B_OUT_W, tm // d)]
    shapes += [(n, D_MODEL, tm)] * 2
    return pl.pallas_call(
        _inproj_kernel,
        out_shape=[jax.ShapeDtypeStruct((r, c), BF16) for r, c, _ in shapes],
        grid=(n // tm,),
        in_specs=[pl.BlockSpec((tm, D_MODEL), row),
                  pl.BlockSpec((1, N_MOD, D_MODEL), lambda i: (i * tm // seq, 0, 0)),
                  pl.BlockSpec((1, D_MODEL), lambda i: (0, 0)),
                  pl.BlockSpec((D_MODEL, D_IN), lambda i: (0, 0), pipeline_mode=pl.Buffered(1))],
        out_specs=[pl.BlockSpec((b, c), row) for _, c, b in shapes],
        scratch_shapes=[pltpu.VMEM((6 * B_OUT_W // LANES, tm, LANES), F32)],
        compiler_params=_cparams(1, VMEM_LIMIT_INPROJ),
        name="inproj",
    )(x2, mod3, norm1.reshape(1, D_MODEL), w_in_bf)


def _rel_bucket_np(rel):
    half = N_BUCKETS // 2
    max_exact = half // 2
    n = np.abs(rel)
    large = max_exact + (np.log(np.maximum(n, 1) / max_exact) / math.log(MAX_DISTANCE / max_exact)
                         * (half - max_exact)).astype(np.int32)
    large = np.minimum(large, half - 1)
    return ((rel > 0).astype(np.int32) * half + np.where(n < max_exact, n, large)).astype(np.int32)


def _bias_table(rel_bias_heads, n_q, n_k, key_off, band, dist_scale):
    p = n_q + n_k
    rel = np.arange(p) - (n_q - 1) - key_off
    bucket = _rel_bucket_np(rel * dist_scale)
    t = jnp.where((np.abs(rel) <= band)[None], rel_bias_heads.astype(F32)[bucket].T, NEG_INF)
    big = jnp.tile(t, (1, n_q + 1))
    tab = big[:, n_q - 1:n_q - 1 + n_q * (p - 1)].reshape(-1, n_q, p - 1)
    return tab[:, :, :n_k]


def _pair_rhs(k_top, k_bot, v_top, v_bot):
    kb = k_top.shape[0]
    low = jnp.where(lax.broadcasted_iota(I32, (kb, LANES), 1) < HEAD_DIM, 1.0, 0.0).astype(BF16)
    high = jnp.where(lax.broadcasted_iota(I32, (kb, LANES), 1) < HEAD_DIM, 0.0, 1.0).astype(BF16)
    rhs_k = jnp.concatenate([k_top * low, k_bot * high], axis=0)
    rhs_v = jnp.concatenate([jnp.concatenate([v_top * low, low], axis=1),
                             jnp.concatenate([v_bot * high, high], axis=1)], axis=0)
    return rhs_k, rhs_v


def _pair_attention(q_pair, rhs_k, rhs_v, bias_pair, pen, sink_pair, want_lse):
    kb = rhs_k.shape[0] // 2
    s = lax.dot_general(q_pair, rhs_k, (((1,), (1,)), ((), ())), preferred_element_type=F32)
    s = s + bias_pair
    if pen is not None:
        s = s + pen
    s0, s1 = s[:, :kb], s[:, kb:]
    m0 = jnp.max(s0, axis=-1, keepdims=True)
    m1 = jnp.max(s1, axis=-1, keepdims=True)
    if sink_pair is not None:
        m0 = jnp.maximum(m0, sink_pair[:, 0:1])
        m1 = jnp.maximum(m1, sink_pair[:, HEAD_DIM:HEAD_DIM + 1])
    p = jnp.concatenate([jnp.exp(s0 - m0), jnp.exp(s1 - m1)], axis=1).astype(BF16)
    od = jnp.dot(p, rhs_v, preferred_element_type=F32)
    o, den = od[:, :LANES], od[:, LANES:]
    low = lax.broadcasted_iota(I32, o.shape, 1) < HEAD_DIM
    m_full = jnp.where(low, m0, m1)
    if sink_pair is not None:
        den = den + jnp.exp(sink_pair - m_full)
    return o / den, (m_full + jnp.log(den)) if want_lse else None


def _attn_kernel(*refs, halo, nblk, n_pairs, n_seqs, shared_kv, has_sink, want_lse, n_steps):
    it = iter(refs)
    q_ref = next(it)
    kv_prev, kv_cur, kv_next = next(it), next(it), next(it)
    bias_ref = next(it)
    sink_ref = next(it) if has_sink else None
    o_ref = next(it)
    lse_ref = next(it) if want_lse else None
    step = pl.program_id(2)
    blk = ATTN_BLOCK
    kb = blk + 2 * halo
    qw = n_pairs * LANES
    kv = jnp.concatenate([kv_prev[...], kv_cur[...], kv_next[...]], axis=0)
    col = lax.broadcasted_iota(I32, (1, kb), 1)
    for b in range(nblk):
        lo = jnp.where(step == 0, halo, 0) if b == 0 else 0
        hi = jnp.where(step == n_steps - 1, kb - halo, kb) if b == nblk - 1 else kb
        pen = None
        if b == 0 or b == nblk - 1:
            pen1 = jnp.where((col >= lo) & (col < hi), 0.0, NEG_INF)
            pen = jnp.concatenate([pen1, pen1], axis=1)
        rows = slice(b * blk, (b + 1) * blk)
        krows = slice(b * blk, b * blk + kb)
        lane_col = lambda j: kv[krows, j * LANES:(j + 1) * LANES]
        if shared_kv:
            k, k_sw, v, v_sw = (lane_col(j) for j in range(4))
            rhs = [_pair_rhs(k, k_sw, v, v_sw), _pair_rhs(k_sw, k, v_sw, v)]
        for s in range(n_seqs):
            for c in range(n_pairs):
                lanes = slice(s * qw + c * LANES, s * qw + (c + 1) * LANES)
                if shared_kv:
                    rhs_k, rhs_v = rhs[c // (n_pairs // 2)]
                else:
                    k, v = lane_col(2 * s * n_pairs + c), lane_col((2 * s + 1) * n_pairs + c)
                    rhs_k, rhs_v = _pair_rhs(k, k, v, v)
                o, lse = _pair_attention(q_ref[rows, lanes], rhs_k, rhs_v, bias_ref[c], pen,
                                         sink_ref[c] if has_sink else None, want_lse)
                o_ref[rows, lanes] = o.astype(o_ref.dtype)
                if want_lse:
                    lse_ref[rows, lanes] = lse


def _band_attention(q, kv, bias, sink, *, nbatch, seq, dil, halo, step_rows, shared_kv, want_lse):
    n = nbatch * seq
    sub_len = seq // dil
    qw = q.shape[1] // dil
    kv_width = kv.shape[1] // dil
    rows = min(step_rows, sub_len)
    n_seqs = min(dil, step_rows // rows)
    assert sub_len % rows == 0 and rows % ATTN_BLOCK == 0 and rows % halo == 0 and dil % n_seqs == 0
    nq = sub_len // rows
    per = rows // halo
    total_halos = n // dil // halo
    cur = lambda b, r, i: (b * nq + i, r)
    prev = lambda b, r, i: (jnp.maximum((b * nq + i) * per - 1, 0), r)
    nxt = lambda b, r, i: (jnp.minimum((b * nq + i + 1) * per, total_halos - 1), r)
    const = lambda a: pl.BlockSpec(a.shape, lambda b, r, i: (0,) * a.ndim)
    qw_step, kvw_step = qw * n_seqs, kv_width * n_seqs

    in_specs = [pl.BlockSpec((rows, qw_step), cur), pl.BlockSpec((halo, kvw_step), prev),
                pl.BlockSpec((rows, kvw_step), cur), pl.BlockSpec((halo, kvw_step), nxt),
                const(bias)]
    args = [q, kv, kv, kv, bias]
    if sink is not None:
        in_specs.append(const(sink))
        args.append(sink)
    out_shape = [jax.ShapeDtypeStruct(q.shape, BF16)]
    out_specs = [pl.BlockSpec((rows, qw_step), cur)]
    if want_lse:
        out_shape.append(jax.ShapeDtypeStruct(q.shape, F32))
        out_specs.append(pl.BlockSpec((rows, qw_step), cur))
    return pl.pallas_call(
        functools.partial(_attn_kernel, halo=halo, nblk=rows // ATTN_BLOCK, n_pairs=qw // LANES,
                          n_seqs=n_seqs, shared_kv=shared_kv, has_sink=sink is not None,
                          want_lse=want_lse, n_steps=nq),
        out_shape=out_shape,
        grid=(nbatch, dil // n_seqs, nq),
        in_specs=in_specs,
        out_specs=out_specs,
        compiler_params=_cparams(3),
        name=f"band_attn_d{dil}",
    )(*args)


def _pack_rows(y):
    return pltpu.pack_elementwise([y[:, :HALF_D], y[:, HALF_D:]], packed_dtype=BF16)


def _store_slabs(ref, packed):
    t = packed.shape[0]
    for c in range(ROW_SLABS):
        ref[pl.ds(c, t, stride=ROW_SLABS), :] = packed[:, c * LANES:(c + 1) * LANES]


def _load_slabs(ref):
    t = ref.shape[0] // ROW_SLABS
    return jnp.concatenate([ref[pl.ds(c, t, stride=ROW_SLABS), :] for c in range(ROW_SLABS)], axis=1)


def _unpack_rows(p):
    return tuple(pltpu.unpack_elementwise(p, index=i, packed_dtype=BF16, unpacked_dtype=F32)
                 for i in range(2))


def _route(sel, scores):
    t = sel.shape[-1]
    per = N_EXPERTS // N_EXPERT_GROUPS
    shape3 = (N_EXPERT_GROUPS, per, t)
    sel3 = sel.reshape(shape3)
    sc3 = scores.reshape(shape3)
    iota_g = lax.broadcasted_iota(I32, shape3, 0)
    iota_m = lax.broadcasted_iota(I32, shape3, 1)
    iota_e = iota_g * per + iota_m
    m1 = jnp.max(sel3, axis=1, keepdims=True)
    i1 = jnp.min(jnp.where(sel3 == m1, iota_m, per), axis=1, keepdims=True)
    m2 = jnp.max(jnp.where(iota_m == i1, REMOVED, sel3), axis=1, keepdims=True)
    gscore = m1 + m2
    iota_g1 = lax.broadcasted_iota(I32, gscore.shape, 0)
    gmask = jnp.zeros(gscore.shape, jnp.bool_)
    for _ in range(TOPK_GROUPS):
        mx = jnp.max(gscore, axis=0, keepdims=True)
        ix = jnp.min(jnp.where(gscore == mx, iota_g1, N_EXPERT_GROUPS), axis=0, keepdims=True)
        hit = iota_g1 == ix
        gmask = gmask | hit
        gscore = jnp.where(hit, REMOVED, gscore)
    cur = jnp.where(gmask, sel3, NEG_INF)
    sum_all = lambda a: jnp.sum(jnp.sum(a, axis=1, keepdims=True), axis=0, keepdims=True)
    idxs, wts, hits = [], [], []
    for _ in range(TOP_K):
        mx = jnp.max(jnp.max(cur, axis=1, keepdims=True), axis=0, keepdims=True)
        cand = jnp.where(cur == mx, iota_e, N_EXPERTS)
        ix = jnp.min(jnp.min(cand, axis=1, keepdims=True), axis=0, keepdims=True)
        hit = iota_e == ix
        wts.append(sum_all(jnp.where(hit, sc3, 0.0)).reshape(1, t))
        cur = jnp.where(hit, REMOVED, cur)
        idxs.append(ix.reshape(1, t))
        hits.append(hit)
    wsum = wts[0]
    for w in wts[1:]:
        wsum = wsum + w
    wts = [w / wsum * ROUTED_SCALE for w in wts]
    onehot = hits[0].astype(F32)
    for hit in hits[1:]:
        onehot = onehot + hit.astype(F32)
    tri = (lax.broadcasted_iota(I32, (t, t), 0) <= lax.broadcasted_iota(I32, (t, t), 1))
    cum = jnp.dot(onehot.reshape(N_EXPERTS, t).astype(BF16), tri.astype(F32).astype(BF16),
                  preferred_element_type=F32)
    cum3 = cum.reshape(shape3) - 1.0
    ranks = [sum_all(jnp.where(hit, cum3, 0.0)).reshape(1, t).astype(I32) for hit in hits]
    counts = cum[:, t - 1:t].astype(I32)
    return idxs, wts, ranks, counts


def _merge_kernel(x_ref, ya_ref, o1, o2, o3, l1, l2, l3, sga_ref, sgb_ref, mod_ref, n2_ref,
                  wpa, wpb, wo, wrt, rbias, wsg, wsu, wsd,
                  base_ref, hp_ref, ridx_ref, rw_ref, rank_ref, cnt_ref, scr):
    mod = mod_ref[0]
    tm = x_ref.shape[0]

    def token_major(ref, slot, d):
        if d == 1:
            return ref[...].astype(F32)
        nc = B_OUT_W // LANES
        for c in range(nc):
            for res in range(d):
                col = res * B_OUT_W + c * LANES
                scr[slot * nc + c, pl.ds(res, tm // d, stride=d), :] = ref[
                    :, col:col + LANES].astype(F32)
        return jnp.concatenate([scr[slot * nc + c] for c in range(nc)], axis=1)

    dils = [d for _, d in B_GROUPS]
    os_ = [token_major(r, i, d) for i, (r, d) in enumerate(zip((o1, o2, o3), dils))]
    ls = [token_major(r, 3 + i, d) for i, (r, d) in enumerate(zip((l1, l2, l3), dils))]
    mx = jnp.maximum(jnp.maximum(ls[0], ls[1]), ls[2])
    es = [jnp.exp(l - mx) for l in ls]
    den = es[0] + es[1] + es[2]
    ob = (es[0] / den) * os_[0] + (es[1] / den) * os_[1] + (es[2] / den) * os_[2]
    pa = jnp.dot(ya_ref[...], wpa[...], preferred_element_type=F32)
    pb = jnp.dot(ob.astype(BF16), wpb[...], preferred_element_type=F32)
    merged = sga_ref[...].astype(F32) * pa + sgb_ref[...].astype(F32) * pb
    mix = jnp.dot(merged.astype(BF16), wo[...], preferred_element_type=F32)
    x1 = x_ref[...] + mod[G1:G1 + 1] * mix
    h2 = _rms(x1, n2_ref[...]) * (1.0 + mod[SC2:SC2 + 1]) + mod[SH2:SH2 + 1]
    logits = lax.dot_general(wrt[...], h2, (((1,), (1,)), ((), ())), preferred_element_type=F32,
                             precision=lax.Precision.HIGHEST)
    scores = jax.nn.sigmoid(logits)
    idxs, wts, ranks, counts = _route(scores + rbias[...], scores)
    for k in range(TOP_K):
        ridx_ref[k:k + 1, :] = idxs[k]
        rw_ref[k:k + 1, :] = wts[k]
        rank_ref[k:k + 1, :] = ranks[k]
    ridx_ref[TOP_K:, :] = jnp.zeros((SUBLANES - TOP_K, tm), I32)
    rw_ref[TOP_K:, :] = jnp.zeros((SUBLANES - TOP_K, tm), F32)
    rank_ref[TOP_K:, :] = jnp.zeros((SUBLANES - TOP_K, tm), I32)
    cnt_ref[0] = counts
    hb = h2.astype(BF16)
    g = jnp.dot(hb, wsg[...], preferred_element_type=F32)
    u = jnp.dot(hb, wsu[...], preferred_element_type=F32)
    act = (g * jax.nn.sigmoid(g) * u).astype(BF16)
    shared = jnp.dot(act, wsd[...], preferred_element_type=F32)
    base_ref[...] = x1 + mod[G2:G2 + 1] * shared
    _store_slabs(hp_ref, _pack_rows(h2))


def _merge(x2, ya, outs, lses, sga, sgb, mod3, norm2, wpa, wpb, wo, wrt, rbias, wsg, wsu, wsd, seq):
    n = x2.shape[0]
    tm = TM_MERGE
    assert seq % tm == 0
    row = lambda i: (i, 0)
    full = lambda a: pl.BlockSpec(a.shape, lambda i: (0,) * a.ndim)
    weights = [wpa, wpb, wo, wrt, rbias, wsg, wsu, wsd]
    group_specs = [pl.BlockSpec((tm // d, d * B_OUT_W), row) for _, d in B_GROUPS]
    lanes = lambda i: (0, i)
    return pl.pallas_call(
        _merge_kernel,
        out_shape=[jax.ShapeDtypeStruct((n, D_MODEL), F32),
                   jax.ShapeDtypeStruct((n * ROW_SLABS, LANES), U32),
                   jax.ShapeDtypeStruct((SUBLANES, n), I32),
                   jax.ShapeDtypeStruct((SUBLANES, n), F32),
                   jax.ShapeDtypeStruct((SUBLANES, n), I32),
                   jax.ShapeDtypeStruct((n // tm, N_EXPERTS, 1), I32)],
        grid=(n // tm,),
        in_specs=[pl.BlockSpec((tm, D_MODEL), row), pl.BlockSpec((tm, A_Q_W), row)]
                 + group_specs * 2
                 + [pl.BlockSpec((tm, D_MODEL), row)] * 2
                 + [pl.BlockSpec((1, N_MOD, D_MODEL), lambda i: (i * tm // seq, 0, 0)),
                    pl.BlockSpec((1, D_MODEL), lambda i: (0, 0))]
                 + [full(w) for w in weights],
        out_specs=[pl.BlockSpec((tm, D_MODEL), row),
                   pl.BlockSpec((tm * ROW_SLABS, LANES), row),
                   pl.BlockSpec((SUBLANES, tm), lanes), pl.BlockSpec((SUBLANES, tm), lanes),
                   pl.BlockSpec((SUBLANES, tm), lanes),
                   pl.BlockSpec((1, N_EXPERTS, 1), lambda i: (i, 0, 0))],
        scratch_shapes=[pltpu.VMEM((6 * B_OUT_W // LANES, tm, LANES), F32)],
        compiler_params=_cparams(1),
        name="merge_route",
    )(x2, ya, *outs, *lses, sga, sgb, mod3, norm2.reshape(1, D_MODEL), *weights)


def _plan_kernel(ridx_ref, rank_ref, tb_ref, dest_ref, *, tm):
    iota_e = lax.broadcasted_iota(I32, (N_EXPERTS, tm), 0)
    for j in range(dest_ref.shape[0]):
        tb = tb_ref[j]
        cols = slice(j * tm, (j + 1) * tm)
        for k in range(TOP_K):
            base = jnp.sum(jnp.where(iota_e == ridx_ref[k:k + 1, cols], tb, 0), axis=0, keepdims=True)
            dest_ref[j, :, k * tm:(k + 1) * tm] = base + rank_ref[k:k + 1, cols]


def _plan(ridx, rank, tile_base, tm):
    n = ridx.shape[1]
    tiles = math.gcd(PLAN_TILES, n // tm)
    lanes = lambda i: (0, i)
    return pl.pallas_call(
        functools.partial(_plan_kernel, tm=tm),
        out_shape=jax.ShapeDtypeStruct((n // tm, 1, TOP_K * tm), I32),
        grid=(n // tm // tiles,),
        in_specs=[pl.BlockSpec((SUBLANES, tiles * tm), lanes), pl.BlockSpec((SUBLANES, tiles * tm), lanes),
                  pl.BlockSpec((tiles, N_EXPERTS, 1), lambda i: (i, 0, 0))],
        out_specs=pl.BlockSpec((tiles, 1, TOP_K * tm), lambda i: (i, 0, 0)),
        compiler_params=_cparams(1),
        name="moe_plan",
    )(ridx, rank, tile_base)


def _sc_mesh():
    return plsc.VectorSubcoreMesh(core_axis_name="core", subcore_axis_name="subcore",
                                  num_cores=SC_CORES, num_subcores=SC_SUBCORES)


def _sc_worker():
    return lax.axis_index("subcore") * SC_CORES + lax.axis_index("core")


def _sc_dispatch(hp, dest_rows, n_rows):
    n = hp.shape[0]
    halves = TM_MERGE // SC_CHUNK
    workers = SC_CORES * SC_SUBCORES
    tiles_per_w = n // TM_MERGE // workers
    idx_per_w = tiles_per_w * TOP_K * halves
    assert tiles_per_w * workers * TM_MERGE == n

    @functools.partial(
        pl.kernel, mesh=_sc_mesh(),
        out_type=jax.ShapeDtypeStruct((n_rows, ROW_SLABS, LANES), U32),
        scratch_types=[pltpu.VMEM((idx_per_w, SC_CHUNK), I32),
                       pltpu.VMEM((SC_CHUNK, ROW_SLABS, LANES), U32),
                       pltpu.SemaphoreType.DMA],
        name="moe_dispatch_sc")
    def scatter(hp_hbm, idx_hbm, xs_hbm, idx_v, rows_v, sem):
        wid = _sc_worker()
        pltpu.sync_copy(idx_hbm.at[pl.ds(wid * idx_per_w, idx_per_w)], idx_v)

        @pl.loop(0, tiles_per_w * halves)
        def _(j):
            tile = j // halves
            h = j - tile * halves
            tok = (wid * tiles_per_w + tile) * TM_MERGE + h * SC_CHUNK
            pltpu.sync_copy(hp_hbm.at[pl.ds(tok, SC_CHUNK)], rows_v)
            for k in range(TOP_K):
                row = (tile * TOP_K + k) * halves + h
                pltpu.async_copy(rows_v, xs_hbm.at[idx_v.at[row]], sem).wait()

    return scatter(hp, dest_rows)


def _sc_collect(ys, dest_rows):
    n_chunks = dest_rows.shape[0]
    workers = SC_CORES * SC_SUBCORES
    per_w = n_chunks // workers
    assert per_w * workers == n_chunks

    @functools.partial(
        pl.kernel, mesh=_sc_mesh(),
        out_type=jax.ShapeDtypeStruct((n_chunks * SC_CHUNK, ROW_SLABS, LANES), U32),
        scratch_types=[pltpu.VMEM((per_w, SC_CHUNK), I32),
                       pltpu.VMEM((SC_CHUNK, ROW_SLABS, LANES), U32),
                       pltpu.SemaphoreType.DMA],
        name="moe_collect_sc")
    def gather(ys_hbm, idx_hbm, out_hbm, idx_v, rows_v, sem):
        wid = _sc_worker()
        base = wid * per_w
        pltpu.sync_copy(idx_hbm.at[pl.ds(base, per_w)], idx_v)

        @pl.loop(0, per_w)
        def _(j):
            pltpu.async_copy(ys_hbm.at[idx_v.at[j]], rows_v, sem).wait()
            pltpu.sync_copy(rows_v, out_hbm.at[pl.ds((base + j) * SC_CHUNK, SC_CHUNK)])

    return gather(ys, dest_rows)


def _expert_kernel(blk_e_ref, valid_ref, src_ref, xs_ref, wg_ref, wu_ref, wd_ref, ys_ref,
                   wg_bf, wu_bf, wd_bf):
    del src_ref
    step = pl.program_id(0)
    valid = valid_ref[step]

    @pl.when((step == 0) | (blk_e_ref[step] != blk_e_ref[jnp.maximum(step - 1, 0)]))
    def _():
        wg_bf[...] = wg_ref[0].astype(BF16)
        wu_bf[...] = wu_ref[0].astype(BF16)
        wd_bf[...] = wd_ref[0].astype(BF16)

    @pl.when(valid > 0)
    def _():
        rows = lax.broadcasted_iota(I32, (xs_ref.shape[0] // ROW_SLABS, 1), 0)
        lo, hi = _unpack_rows(jnp.where(rows < valid, _load_slabs(xs_ref), jnp.uint32(0)))
        lo, hi = lo.astype(BF16), hi.astype(BF16)

        def up(w_bf):
            return jnp.dot(lo, w_bf[:HALF_D], preferred_element_type=F32) \
                + jnp.dot(hi, w_bf[HALF_D:], preferred_element_type=F32)

        g = up(wg_bf)
        act = (g * jax.nn.sigmoid(g) * up(wu_bf)).astype(BF16)
        _store_slabs(ys_ref, _pack_rows(jnp.dot(act, wd_bf[...], preferred_element_type=F32)))


def _experts(xs, blk_e, blk_valid, blk_src, wg, wu, wd):
    n_rows = xs.shape[0] // ROW_SLABS
    bm = MOE_BM
    nblk = n_rows // bm
    return pl.pallas_call(
        _expert_kernel,
        out_shape=jax.ShapeDtypeStruct((n_rows * ROW_SLABS, LANES), U32),
        grid_spec=pltpu.PrefetchScalarGridSpec(
            num_scalar_prefetch=3,
            grid=(nblk,),
            in_specs=[pl.BlockSpec((bm * ROW_SLABS, LANES), lambda i, be, bv, bs: (bs[i], 0)),
                      pl.BlockSpec((1, D_MODEL, D_EXPERT), lambda i, be, bv, bs: (be[i], 0, 0)),
                      pl.BlockSpec((1, D_MODEL, D_EXPERT), lambda i, be, bv, bs: (be[i], 0, 0)),
                      pl.BlockSpec((1, D_EXPERT, D_MODEL), lambda i, be, bv, bs: (be[i], 0, 0))],
            out_specs=pl.BlockSpec((bm * ROW_SLABS, LANES), lambda i, be, bv, bs: (bs[i], 0)),
            scratch_shapes=[pltpu.VMEM((D_MODEL, D_EXPERT), BF16), pltpu.VMEM((D_MODEL, D_EXPERT), BF16),
                            pltpu.VMEM((D_EXPERT, D_MODEL), BF16)]),
        compiler_params=_cparams(1),
        name="moe_experts",
    )(blk_e, blk_valid, blk_src, xs, wg, wu, wd)


def _combine_kernel(g_ref, base_ref, rw_ref, mod_ref, fn_ref, *rest):
    y_ref = rest[-1]
    tiles = g_ref.shape[0]
    tm = base_ref.shape[0] // tiles
    for j in range(tiles):
        rows = slice(j * tm, (j + 1) * tm)
        rw = rw_ref[rows, :]
        acc_lo = jnp.zeros((tm, HALF_D), F32)
        acc_hi = jnp.zeros((tm, HALF_D), F32)
        for k in range(TOP_K):
            lo, hi = _unpack_rows(_load_slabs(g_ref.at[j, k]))
            w = rw[:, k:k + 1]
            acc_lo = acc_lo + w * lo
            acc_hi = acc_hi + w * hi
        routed = jnp.concatenate([acc_lo, acc_hi], axis=1)
        x2 = base_ref[rows, :] + mod_ref[0][G2:G2 + 1] * routed
        y_ref[rows, :] = _rms(x2, fn_ref[...])


def _combine(g, base, rw_t, mod3, final_norm, seq, part, y_prev):
    n = base.shape[0]
    per = COMBINE_TILES
    tm = per * TM_MERGE
    assert seq % tm == 0
    g4 = g.reshape(-1, TOP_K, TM_MERGE * ROW_SLABS, LANES)
    assert g4.shape[0] % per == 0
    tiles = g4.shape[0] // per
    off = part * tiles
    row = lambda i: (i + off, 0)
    in_specs = [pl.BlockSpec((per, TOP_K, TM_MERGE * ROW_SLABS, LANES), lambda i: (i, 0, 0, 0)),
                pl.BlockSpec((tm, D_MODEL), row),
                pl.BlockSpec((tm, SUBLANES), row),
                pl.BlockSpec((1, N_MOD, D_MODEL), lambda i: ((i + off) * tm // seq, 0, 0)),
                pl.BlockSpec((1, D_MODEL), lambda i: (0, 0))]
    args = [g4, base, rw_t, mod3, final_norm.reshape(1, D_MODEL)]
    aliases = {}
    if y_prev is not None:
        in_specs.append(pl.BlockSpec(memory_space=pl.ANY))
        args.append(y_prev)
        aliases = {len(args) - 1: 0}
    return pl.pallas_call(
        _combine_kernel,
        out_shape=jax.ShapeDtypeStruct((n, D_MODEL), F32),
        grid=(tiles,),
        in_specs=in_specs,
        out_specs=pl.BlockSpec((tm, D_MODEL), row),
        input_output_aliases=aliases,
        compiler_params=_cparams(1),
        name="moe_combine",
    )(*args)


def _block_layout(tile_counts, n):
    bm = MOE_BM
    c = tile_counts[:, :, 0]
    ntiles = c.shape[0]
    counts = jnp.sum(c, axis=0)
    padded = (counts + bm - 1) // bm * bm
    earlier_e = np.tri(N_EXPERTS, k=-1, dtype=bool)
    pstart = jnp.sum(jnp.where(earlier_e, padded[None, :], 0), axis=1)
    pend = pstart + padded
    earlier_t = np.tri(ntiles, k=-1, dtype=bool)
    tile_base = pstart[None, :] + jnp.sum(jnp.where(earlier_t[:, :, None], c[None], 0), axis=1)
    nblk = -(-n * TOP_K // bm) + N_EXPERTS
    blk = jnp.arange(nblk, dtype=I32)
    blk_e = jnp.minimum(jnp.sum((pend[None, :] <= blk[:, None] * bm).astype(I32), axis=1),
                        N_EXPERTS - 1)
    onehot_e = blk_e[:, None] == jnp.arange(N_EXPERTS, dtype=I32)[None, :]
    end_e = jnp.sum(jnp.where(onehot_e, (pstart + counts)[None, :], 0), axis=1)
    blk_valid = jnp.clip(end_e - blk * bm, 0, bm).astype(I32)
    blk_src = jnp.minimum(blk, pend[N_EXPERTS - 1] // bm - 1).astype(I32)
    last_e = jnp.max(jnp.where(counts > 0, jnp.arange(N_EXPERTS, dtype=I32), 0))
    blk_e = jnp.where(blk == blk_src, blk_e, last_e).astype(I32)
    return tile_base.astype(I32)[:, :, None], blk_e, blk_valid, blk_src, nblk * bm


def _front(x, mod3, p, after=None):
    nbatch, seq, _ = x.shape
    n = nbatch * seq
    x2 = x.reshape(n, D_MODEL)
    proj = _inproj(x2, mod3, p["norm1"], p["w_in"], seq)
    qa, kva = proj[:2]
    sga, sgb = proj[8:]
    (ya,) = _band_attention(qa, kva, p["bias_a"], p["sink_a"], nbatch=nbatch, seq=seq, dil=1,
                            halo=A_HALF_WINDOW, step_rows=ATTN_ROWS_A, shared_kv=True, want_lse=False)
    outs, lses = [], []
    for gi, (w, d) in enumerate(B_GROUPS):
        q, kv = proj[2 + 2 * gi:4 + 2 * gi]
        o, lse = _band_attention(q, kv, p["bias_b"][gi], None, nbatch=nbatch, seq=seq, dil=d,
                                 halo=w // (2 * d), step_rows=ATTN_ROWS_B, shared_kv=False,
                                 want_lse=True)
        outs.append(o)
        lses.append(lse)
    if after is not None:
        ya, after = lax.optimization_barrier((ya, after))
    base, hp, ridx, rw, rank, tile_counts = _merge(
        x2, ya, outs, lses, sga, sgb, mod3, p["norm2"], p["w_pa"], p["w_pb"], p["w_o"], p["w_rt"],
        p["rbias"], p["ws_gate"], p["ws_up"], p["ws_down"], seq)
    tile_base, blk_e, blk_valid, blk_src, n_rows = _block_layout(tile_counts, n)
    dest_rows = _plan(ridx, rank, tile_base, TM_MERGE).reshape(-1, SC_CHUNK)
    return dict(base=base, hp=hp, rw=rw, dest_rows=dest_rows, blk_e=blk_e, blk_valid=blk_valid,
                blk_src=blk_src, n_rows=n_rows, shape=x.shape, after=after)


def _slabs(a):
    return a.reshape(-1, ROW_SLABS, LANES)


def _moe(st, xs, mod3, p):
    nbatch, seq, _ = st["shape"]
    dest_rows = st["dest_rows"]
    ys = _experts(xs.reshape(-1, LANES), st["blk_e"], st["blk_valid"], st["blk_src"], p["w_gate"], p["w_up"],
                  p["w_down"])
    rows_per_part = dest_rows.shape[0] // COLLECT_PARTS
    rw_t = st["rw"].T
    y = None
    for part in range(COLLECT_PARTS):
        g = _sc_collect(_slabs(ys), dest_rows[part * rows_per_part:(part + 1) * rows_per_part])
        y = _combine(g, st["base"], rw_t, mod3, p["final_norm"], seq, part, y)
    return y.reshape(nbatch, seq, D_MODEL)


def kernel(x_prompt, x_sample, c_prompt, c_sample, rel_bias, w_ada, b_ada, norm1, w_in, sink, w_pa, w_pb, w_o, norm2, w_router, router_bias, w_gate, w_up, w_down, ws_gate, ws_up, ws_down, final_norm):
    assert w_ada.shape[0] == 1
    nbp = x_prompt.shape[0]
    mod = _ada(jnp.concatenate([c_prompt, c_sample], axis=0), w_ada[0], b_ada[0])
    mod3 = mod.reshape(-1, N_MOD, D_MODEL)

    def pair_bias(heads, halo, dist_scale):
        kb = ATTN_BLOCK + 2 * halo
        tab = _bias_table(heads, ATTN_BLOCK, kb, halo, halo, dist_scale)
        return tab.reshape(-1, 2, ATTN_BLOCK, kb).transpose(0, 2, 1, 3).reshape(-1, ATTN_BLOCK, 2 * kb)

    bias_a = pair_bias(rel_bias[:, :A_Q_HEADS], A_HALF_WINDOW, 1)
    bias_b = []
    for gi, (w, d) in enumerate(B_GROUPS):
        h0 = A_Q_HEADS + gi * B_HEADS_PER_GROUP
        bias_b.append(pair_bias(rel_bias[:, h0:h0 + B_HEADS_PER_GROUP], w // (2 * d), d))
    p = {
        "norm1": norm1[0], "norm2": norm2[0], "final_norm": final_norm,
        "w_in": w_in[0].astype(BF16),
        "bias_a": bias_a,
        "sink_a": jnp.repeat(sink[0].astype(F32), HEAD_DIM).reshape(A_Q_HEADS // 2, 1, LANES),
        "bias_b": bias_b,
        "w_pa": w_pa[0].astype(BF16), "w_pb": w_pb[0].astype(BF16), "w_o": w_o[0].astype(BF16),
        "w_rt": w_router[0].T, "rbias": router_bias[0].reshape(N_EXPERTS, 1),
        "ws_gate": ws_gate[0].astype(BF16), "ws_up": ws_up[0].astype(BF16),
        "ws_down": ws_down[0].astype(BF16),
        "w_gate": w_gate[0], "w_up": w_up[0], "w_down": w_down[0],
    }
    dispatch = lambda st: _sc_dispatch(_slabs(st["hp"]), st["dest_rows"], st["n_rows"])
    st_p = _front(x_prompt, mod3[:nbp], p)
    st_s = _front(x_sample, mod3[nbp:], p, after=dispatch(st_p))
    return (_moe(st_p, st_s["after"], mod3[:nbp], p), _moe(st_s, dispatch(st_s), mod3[nbp:], p))
```

```python
import functools
import math

import jax
import jax.numpy as jnp
import numpy as np
from jax import lax
from jax.experimental import pallas as pl
from jax.experimental.pallas import tpu as pltpu
from jax.experimental.pallas import tpu_sc as plsc

F32 = jnp.float32
BF16 = jnp.bfloat16
U32 = jnp.uint32
I32 = jnp.int32

D_MODEL = 1024
HEAD_DIM = 64
A_Q_HEADS = 8
A_KV_HEADS = 2
A_HALF_WINDOW = 128
B_GROUPS = ((128, 1), (512, 4), (2048, 16))
B_HEADS_PER_GROUP = 4
N_BUCKETS = 32
MAX_DISTANCE = 1024
N_EXPERTS = 64
TOP_K = 6
N_EXPERT_GROUPS = 8
TOPK_GROUPS = 4
D_EXPERT = 256
ROUTED_SCALE = 2.5
RMS_EPS = 1e-6
NEG_INF = -1e30
N_MOD = 6
SH1, SC1, G1, SH2, SC2, G2 = range(N_MOD)
REMOVED = -3e38

A_Q_W = A_Q_HEADS * HEAD_DIM
A_KV_W = A_KV_HEADS * HEAD_DIM
B_W = len(B_GROUPS) * B_HEADS_PER_GROUP * HEAD_DIM
B_OUT_W = B_HEADS_PER_GROUP * HEAD_DIM
D_IN = A_Q_W + 2 * A_KV_W + 3 * B_W + 2 * D_MODEL
HALF_D = D_MODEL // 2
LANES = 128
SUBLANES = 8
ROW_SLABS = HALF_D // LANES
SC_CORES = 2
SC_SUBCORES = 16
SC_CHUNK = 128

TM_INPROJ = 1024
TM_MERGE = 512
ATTN_BLOCK = 128
ATTN_ROWS_A = 512
ATTN_ROWS_B = 2048
MOE_BM = 1024
EXPERT_RING = 3
TM_COMBINE = 512
COLLECT_PARTS = 2
PLAN_TILES = 8
VMEM_LIMIT = 48 * 1024 * 1024
VMEM_LIMIT_INPROJ = 60 * 1024 * 1024


def _cparams(n_axes, vmem_limit=VMEM_LIMIT):
    return pltpu.CompilerParams(
        dimension_semantics=("arbitrary",) * n_axes, vmem_limit_bytes=vmem_limit)


def _ada_kernel(c_ref, w_ref, b_ref, o_ref):
    c = c_ref[...]
    s = c * jax.nn.sigmoid(c)
    o_ref[...] = jnp.dot(s, w_ref[...], preferred_element_type=F32,
                         precision=lax.Precision.HIGHEST) + b_ref[...]


def _ada(c_all, w_ada, b_ada):
    nb = c_all.shape[0]
    return pl.pallas_call(
        _ada_kernel,
        out_shape=jax.ShapeDtypeStruct((nb, N_MOD * D_MODEL), F32),
        grid=(N_MOD,),
        in_specs=[pl.BlockSpec((nb, D_MODEL), lambda j: (0, 0)),
                  pl.BlockSpec((D_MODEL, D_MODEL), lambda j: (0, j)),
                  pl.BlockSpec((1, D_MODEL), lambda j: (0, j))],
        out_specs=pl.BlockSpec((nb, D_MODEL), lambda j: (0, j)),
        compiler_params=_cparams(1),
        name="ada",
    )(c_all, w_ada, b_ada.reshape(1, N_MOD * D_MODEL))


def _rms(x, g):
    return x * lax.rsqrt(jnp.mean(x * x, axis=-1, keepdims=True) + RMS_EPS) * g


def _inproj_kernel(x_ref, mod_ref, n1_ref, w_ref, qa, kva, *rest):
    q_b = rest[0:6:2]
    kv_b = rest[1:6:2]
    sga, sgb, scr = rest[6:]
    mod = mod_ref[0]
    h = _rms(x_ref[...], n1_ref[...]) * (1.0 + mod[SC1:SC1 + 1]) + mod[SH1:SH1 + 1]
    hb = h.astype(BF16)
    tm = hb.shape[0]
    scale = HEAD_DIM ** -0.5

    def proj(off, width):
        return jnp.dot(hb, w_ref[:, off:off + width], preferred_element_type=F32)

    qa[...] = (proj(0, A_Q_W) * scale).astype(BF16)
    for j, off in enumerate((A_Q_W, A_Q_W + A_KV_W)):
        r = proj(off, A_KV_W)
        kva[:, 2 * j * A_KV_W:(2 * j + 1) * A_KV_W] = r.astype(BF16)
        kva[:, (2 * j + 1) * A_KV_W:(2 * j + 2) * A_KV_W] = pltpu.roll(r, HEAD_DIM, 1).astype(BF16)
    off = A_Q_W + 2 * A_KV_W
    for t in range(3):
        for gi, (_, d) in enumerate(B_GROUPS):
            r = proj(off + t * B_W + gi * B_OUT_W, B_OUT_W)
            if t == 0:
                r = r * scale
            ref, width, base = (q_b[gi], B_OUT_W, 0) if t == 0 else (kv_b[gi], 2 * B_OUT_W,
                                                                     (t - 1) * B_OUT_W)
            if d == 1:
                ref[:, base:base + B_OUT_W] = r.astype(BF16)
            else:
                for c in range(B_OUT_W // LANES):
                    slot = (t * 2 + gi - 1) * (B_OUT_W // LANES) + c
                    scr[slot] = r[:, c * LANES:(c + 1) * LANES]
                    for res in range(d):
                        col = res * width + base + c * LANES
                        ref[:, col:col + LANES] = scr[
                            slot, pl.ds(res, tm // d, stride=d), :].astype(BF16)
    off += 3 * B_W
    sga[...] = jax.nn.sigmoid(proj(off, D_MODEL)).astype(BF16)
    sgb[...] = jax.nn.sigmoid(proj(off + D_MODEL, D_MODEL)).astype(BF16)


def _inproj(x2, mod3, norm1, w_in_bf, seq):
    n = x2.shape[0]
    tm = TM_INPROJ
    assert seq % tm == 0 and n % tm == 0
    row = lambda i: (i, 0)
    shapes = [(n, A_Q_W, tm), (n, 4 * A_KV_W, tm)]
    for _, d in B_GROUPS:
        shapes += [(n // d, d * B_OUT_W, tm // d), (n // d, d * 2 * B_OUT_W, tm // d)]
    shapes += [(n, D_MODEL, tm)] * 2
    return pl.pallas_call(
        _inproj_kernel,
        out_shape=[jax.ShapeDtypeStruct((r, c), BF16) for r, c, _ in shapes],
        grid=(n // tm,),
        in_specs=[pl.BlockSpec((tm, D_MODEL), row),
                  pl.BlockSpec((1, N_MOD, D_MODEL), lambda i: (i * tm // seq, 0, 0)),
                  pl.BlockSpec((1, D_MODEL), lambda i: (0, 0)),
                  pl.BlockSpec((D_MODEL, D_IN), lambda i: (0, 0), pipeline_mode=pl.Buffered(1))],
        out_specs=[pl.BlockSpec((b, c), row) for _, c, b in shapes],
        scratch_shapes=[pltpu.VMEM((6 * B_OUT_W // LANES, tm, LANES), F32)],
        compiler_params=_cparams(1, VMEM_LIMIT_INPROJ),
        name="inproj",
    )(x2, mod3, norm1.reshape(1, D_MODEL), w_in_bf)


def _rel_bucket_np(rel):
    half = N_BUCKETS // 2
    max_exact = half // 2
    n = np.abs(rel)
    large = max_exact + (np.log(np.maximum(n, 1) / max_exact) / math.log(MAX_DISTANCE / max_exact)
                         * (half - max_exact)).astype(np.int32)
    large = np.minimum(large, half - 1)
    return ((rel > 0).astype(np.int32) * half + np.where(n < max_exact, n, large)).astype(np.int32)


def _bias_table(rel_bias_heads, n_q, n_k, key_off, band, dist_scale):
    p = n_q + n_k
    rel = np.arange(p) - (n_q - 1) - key_off
    bucket = _rel_bucket_np(rel * dist_scale)
    t = jnp.where((np.abs(rel) <= band)[None], rel_bias_heads.astype(F32)[bucket].T, NEG_INF)
    big = jnp.tile(t, (1, n_q + 1))
    tab = big[:, n_q - 1:n_q - 1 + n_q * (p - 1)].reshape(-1, n_q, p - 1)
    return tab[:, :, :n_k]


def _pair_rhs(k_top, k_bot, v_top, v_bot):
    kb = k_top.shape[0]
    low = jnp.where(lax.broadcasted_iota(I32, (kb, LANES), 1) < HEAD_DIM, 1.0, 0.0).astype(BF16)
    high = jnp.where(lax.broadcasted_iota(I32, (kb, LANES), 1) < HEAD_DIM, 0.0, 1.0).astype(BF16)
    rhs_k = jnp.concatenate([k_top * low, k_bot * high], axis=0)
    rhs_v = jnp.concatenate([jnp.concatenate([v_top * low, low], axis=1),
                             jnp.concatenate([v_bot * high, high], axis=1)], axis=0)
    return rhs_k, rhs_v


def _pair_attention(q_pair, rhs_k, rhs_v, bias_pair, pen, sink_pair, want_lse):
    kb = rhs_k.shape[0] // 2
    s = lax.dot_general(q_pair, rhs_k, (((1,), (1,)), ((), ())), preferred_element_type=F32)
    s = s + bias_pair
    if pen is not None:
        s = s + pen
    s0, s1 = s[:, :kb], s[:, kb:]
    m0 = jnp.max(s0, axis=-1, keepdims=True)
    m1 = jnp.max(s1, axis=-1, keepdims=True)
    if sink_pair is not None:
        m0 = jnp.maximum(m0, sink_pair[:, 0:1])
        m1 = jnp.maximum(m1, sink_pair[:, HEAD_DIM:HEAD_DIM + 1])
    p = jnp.concatenate([jnp.exp(s0 - m0), jnp.exp(s1 - m1)], axis=1).astype(BF16)
    od = jnp.dot(p, rhs_v, preferred_element_type=F32)
    o, den = od[:, :LANES], od[:, LANES:]
    low = lax.broadcasted_iota(I32, o.shape, 1) < HEAD_DIM
    m_full = jnp.where(low, m0, m1)
    if sink_pair is not None:
        den = den + jnp.exp(sink_pair - m_full)
    return o / den, (m_full + jnp.log(den)) if want_lse else None


def _attn_kernel(*refs, halo, nblk, n_pairs, n_seqs, shared_kv, has_sink, want_lse, n_steps):
    it = iter(refs)
    q_ref = next(it)
    kv_prev, kv_cur, kv_next = next(it), next(it), next(it)
    bias_ref = next(it)
    sink_ref = next(it) if has_sink else None
    o_ref = next(it)
    lse_ref = next(it) if want_lse else None
    step = pl.program_id(2)
    blk = ATTN_BLOCK
    kb = blk + 2 * halo
    qw = n_pairs * LANES
    kv = jnp.concatenate([kv_prev[...], kv_cur[...], kv_next[...]], axis=0)
    col = lax.broadcasted_iota(I32, (1, kb), 1)
    for b in range(nblk):
        lo = jnp.where(step == 0, halo, 0) if b == 0 else 0
        hi = jnp.where(step == n_steps - 1, kb - halo, kb) if b == nblk - 1 else kb
        pen = None
        if b == 0 or b == nblk - 1:
            pen1 = jnp.where((col >= lo) & (col < hi), 0.0, NEG_INF)
            pen = jnp.concatenate([pen1, pen1], axis=1)
        rows = slice(b * blk, (b + 1) * blk)
        krows = slice(b * blk, b * blk + kb)
        lane_col = lambda j: kv[krows, j * LANES:(j + 1) * LANES]
        if shared_kv:
            k, k_sw, v, v_sw = (lane_col(j) for j in range(4))
            rhs = [_pair_rhs(k, k_sw, v, v_sw), _pair_rhs(k_sw, k, v_sw, v)]
        for s in range(n_seqs):
            for c in range(n_pairs):
                lanes = slice(s * qw + c * LANES, s * qw + (c + 1) * LANES)
                if shared_kv:
                    rhs_k, rhs_v = rhs[c // (n_pairs // 2)]
                else:
                    k, v = lane_col(2 * s * n_pairs + c), lane_col((2 * s + 1) * n_pairs + c)
                    rhs_k, rhs_v = _pair_rhs(k, k, v, v)
                o, lse = _pair_attention(q_ref[rows, lanes], rhs_k, rhs_v, bias_ref[c], pen,
                                         sink_ref[c] if has_sink else None, want_lse)
                o_ref[rows, lanes] = o.astype(o_ref.dtype)
                if want_lse:
                    lse_ref[rows, lanes] = lse


def _band_attention(q, kv, bias, sink, *, nbatch, seq, dil, halo, step_rows, shared_kv, want_lse):
    n = nbatch * seq
    sub_len = seq // dil
    qw = q.shape[1] // dil
    kv_width = kv.shape[1] // dil
    rows = min(step_rows, sub_len)
    n_seqs = min(dil, step_rows // rows)
    assert sub_len % rows == 0 and rows % ATTN_BLOCK == 0 and rows % halo == 0 and dil % n_seqs == 0
    nq = sub_len // rows
    per = rows // halo
    total_halos = n // dil // halo
    cur = lambda b, r, i: (b * nq + i, r)
    prev = lambda b, r, i: (jnp.maximum((b * nq + i) * per - 1, 0), r)
    nxt = lambda b, r, i: (jnp.minimum((b * nq + i + 1) * per, total_halos - 1), r)
    const = lambda a: pl.BlockSpec(a.shape, lambda b, r, i: (0,) * a.ndim)
    qw_step, kvw_step = qw * n_seqs, kv_width * n_seqs

    in_specs = [pl.BlockSpec((rows, qw_step), cur), pl.BlockSpec((halo, kvw_step), prev),
                pl.BlockSpec((rows, kvw_step), cur), pl.BlockSpec((halo, kvw_step), nxt),
                const(bias)]
    args = [q, kv, kv, kv, bias]
    if sink is not None:
        in_specs.append(const(sink))
        args.append(sink)
    out_shape = [jax.ShapeDtypeStruct(q.shape, BF16)]
    out_specs = [pl.BlockSpec((rows, qw_step), cur)]
    if want_lse:
        out_shape.append(jax.ShapeDtypeStruct(q.shape, F32))
        out_specs.append(pl.BlockSpec((rows, qw_step), cur))
    return pl.pallas_call(
        functools.partial(_attn_kernel, halo=halo, nblk=rows // ATTN_BLOCK, n_pairs=qw // LANES,
                          n_seqs=n_seqs, shared_kv=shared_kv, has_sink=sink is not None,
                          want_lse=want_lse, n_steps=nq),
        out_shape=out_shape,
        grid=(nbatch, dil // n_seqs, nq),
        in_specs=in_specs,
        out_specs=out_specs,
        compiler_params=_cparams(3),
        name=f"band_attn_d{dil}",
    )(*args)


def _pack_rows(y):
    return pltpu.pack_elementwise([y[:, :HALF_D], y[:, HALF_D:]], packed_dtype=BF16)


def _store_slabs(ref, packed):
    t = packed.shape[0]
    for c in range(ROW_SLABS):
        ref[pl.ds(c, t, stride=ROW_SLABS), :] = packed[:, c * LANES:(c + 1) * LANES]


def _load_slabs(ref):
    t = ref.shape[0] // ROW_SLABS
    return jnp.concatenate([ref[pl.ds(c, t, stride=ROW_SLABS), :] for c in range(ROW_SLABS)], axis=1)


def _unpack_rows(p):
    return tuple(pltpu.unpack_elementwise(p, index=i, packed_dtype=BF16, unpacked_dtype=F32)
                 for i in range(2))


def _route(sel, scores):
    t = sel.shape[-1]
    per = N_EXPERTS // N_EXPERT_GROUPS
    shape3 = (N_EXPERT_GROUPS, per, t)
    sel3 = sel.reshape(shape3)
    sc3 = scores.reshape(shape3)
    iota_g = lax.broadcasted_iota(I32, shape3, 0)
    iota_m = lax.broadcasted_iota(I32, shape3, 1)
    iota_e = iota_g * per + iota_m
    m1 = jnp.max(sel3, axis=1, keepdims=True)
    i1 = jnp.min(jnp.where(sel3 == m1, iota_m, per), axis=1, keepdims=True)
    m2 = jnp.max(jnp.where(iota_m == i1, REMOVED, sel3), axis=1, keepdims=True)
    gscore = m1 + m2
    iota_g1 = lax.broadcasted_iota(I32, gscore.shape, 0)
    gmask = jnp.zeros(gscore.shape, jnp.bool_)
    for _ in range(TOPK_GROUPS):
        mx = jnp.max(gscore, axis=0, keepdims=True)
        ix = jnp.min(jnp.where(gscore == mx, iota_g1, N_EXPERT_GROUPS), axis=0, keepdims=True)
        hit = iota_g1 == ix
        gmask = gmask | hit
        gscore = jnp.where(hit, REMOVED, gscore)
    cur = jnp.where(gmask, sel3, NEG_INF)
    sum_all = lambda a: jnp.sum(jnp.sum(a, axis=1, keepdims=True), axis=0, keepdims=True)
    idxs, wts, hits = [], [], []
    for _ in range(TOP_K):
        mx = jnp.max(jnp.max(cur, axis=1, keepdims=True), axis=0, keepdims=True)
        cand = jnp.where(cur == mx, iota_e, N_EXPERTS)
        ix = jnp.min(jnp.min(cand, axis=1, keepdims=True), axis=0, keepdims=True)
        hit = iota_e == ix
        wts.append(sum_all(jnp.where(hit, sc3, 0.0)).reshape(1, t))
        cur = jnp.where(hit, REMOVED, cur)
        idxs.append(ix.reshape(1, t))
        hits.append(hit)
    wsum = wts[0]
    for w in wts[1:]:
        wsum = wsum + w
    wts = [w / wsum * ROUTED_SCALE for w in wts]
    onehot = hits[0].astype(F32)
    for hit in hits[1:]:
        onehot = onehot + hit.astype(F32)
    tri = (lax.broadcasted_iota(I32, (t, t), 0) <= lax.broadcasted_iota(I32, (t, t), 1))
    cum = jnp.dot(onehot.reshape(N_EXPERTS, t).astype(BF16), tri.astype(F32).astype(BF16),
                  preferred_element_type=F32)
    cum3 = cum.reshape(shape3) - 1.0
    ranks = [sum_all(jnp.where(hit, cum3, 0.0)).reshape(1, t).astype(I32) for hit in hits]
    counts = cum[:, t - 1:t].astype(I32)
    return idxs, wts, ranks, counts


def _merge_kernel(x_ref, ya_ref, o1, o2, o3, l1, l2, l3, sga_ref, sgb_ref, mod_ref, n2_ref,
                  wpa, wpb, wo, wrt, rbias, wsg, wsu, wsd,
                  base_ref, hp_ref, ridx_ref, rw_ref, rank_ref, cnt_ref, scr):
    mod = mod_ref[0]
    tm = x_ref.shape[0]

    def token_major(ref, slot, d):
        if d == 1:
            return ref[...].astype(F32)
        nc = B_OUT_W // LANES
        for c in range(nc):
            for res in range(d):
                col = res * B_OUT_W + c * LANES
                scr[slot * nc + c, pl.ds(res, tm // d, stride=d), :] = ref[
                    :, col:col + LANES].astype(F32)
        return jnp.concatenate([scr[slot * nc + c] for c in range(nc)], axis=1)

    dils = [d for _, d in B_GROUPS]
    os_ = [token_major(r, i, d) for i, (r, d) in enumerate(zip((o1, o2, o3), dils))]
    ls = [token_major(r, 3 + i, d) for i, (r, d) in enumerate(zip((l1, l2, l3), dils))]
    mx = jnp.maximum(jnp.maximum(ls[0], ls[1]), ls[2])
    es = [jnp.exp(l - mx) for l in ls]
    den = es[0] + es[1] + es[2]
    ob = (es[0] / den) * os_[0] + (es[1] / den) * os_[1] + (es[2] / den) * os_[2]
    pa = jnp.dot(ya_ref[...], wpa[...], preferred_element_type=F32)
    pb = jnp.dot(ob.astype(BF16), wpb[...], preferred_element_type=F32)
    merged = sga_ref[...].astype(F32) * pa + sgb_ref[...].astype(F32) * pb
    mix = jnp.dot(merged.astype(BF16), wo[...], preferred_element_type=F32)
    x1 = x_ref[...] + mod[G1:G1 + 1] * mix
    h2 = _rms(x1, n2_ref[...]) * (1.0 + mod[SC2:SC2 + 1]) + mod[SH2:SH2 + 1]
    logits = lax.dot_general(wrt[...], h2, (((1,), (1,)), ((), ())), preferred_element_type=F32,
                             precision=lax.Precision.HIGHEST)
    scores = jax.nn.sigmoid(logits)
    idxs, wts, ranks, counts = _route(scores + rbias[...], scores)
    for k in range(TOP_K):
        ridx_ref[k:k + 1, :] = idxs[k]
        rw_ref[k:k + 1, :] = wts[k]
        rank_ref[k:k + 1, :] = ranks[k]
    ridx_ref[TOP_K:, :] = jnp.zeros((SUBLANES - TOP_K, tm), I32)
    rw_ref[TOP_K:, :] = jnp.zeros((SUBLANES - TOP_K, tm), F32)
    rank_ref[TOP_K:, :] = jnp.zeros((SUBLANES - TOP_K, tm), I32)
    cnt_ref[0] = counts
    hb = h2.astype(BF16)
    g = jnp.dot(hb, wsg[...], preferred_element_type=F32)
    u = jnp.dot(hb, wsu[...], preferred_element_type=F32)
    act = (g * jax.nn.sigmoid(g) * u).astype(BF16)
    shared = jnp.dot(act, wsd[...], preferred_element_type=F32)
    base_ref[...] = x1 + mod[G2:G2 + 1] * shared
    _store_slabs(hp_ref, _pack_rows(h2))


def _merge(x2, ya, outs, lses, sga, sgb, mod3, norm2, wpa, wpb, wo, wrt, rbias, wsg, wsu, wsd, seq):
    n = x2.shape[0]
    tm = TM_MERGE
    assert seq % tm == 0
    row = lambda i: (i, 0)
    full = lambda a: pl.BlockSpec(a.shape, lambda i: (0,) * a.ndim)
    weights = [wpa, wpb, wo, wrt, rbias, wsg, wsu, wsd]
    group_specs = [pl.BlockSpec((tm // d, d * B_OUT_W), row) for _, d in B_GROUPS]
    lanes = lambda i: (0, i)
    return pl.pallas_call(
        _merge_kernel,
        out_shape=[jax.ShapeDtypeStruct((n, D_MODEL), F32),
                   jax.ShapeDtypeStruct((n * ROW_SLABS, LANES), U32),
                   jax.ShapeDtypeStruct((SUBLANES, n), I32),
                   jax.ShapeDtypeStruct((SUBLANES, n), F32),
                   jax.ShapeDtypeStruct((SUBLANES, n), I32),
                   jax.ShapeDtypeStruct((n // tm, N_EXPERTS, 1), I32)],
        grid=(n // tm,),
        in_specs=[pl.BlockSpec((tm, D_MODEL), row), pl.BlockSpec((tm, A_Q_W), row)]
                 + group_specs * 2
                 + [pl.BlockSpec((tm, D_MODEL), row)] * 2
                 + [pl.BlockSpec((1, N_MOD, D_MODEL), lambda i: (i * tm // seq, 0, 0)),
                    pl.BlockSpec((1, D_MODEL), lambda i: (0, 0))]
                 + [full(w) for w in weights],
        out_specs=[pl.BlockSpec((tm, D_MODEL), row),
                   pl.BlockSpec((tm * ROW_SLABS, LANES), row),
                   pl.BlockSpec((SUBLANES, tm), lanes), pl.BlockSpec((SUBLANES, tm), lanes),
                   pl.BlockSpec((SUBLANES, tm), lanes),
                   pl.BlockSpec((1, N_EXPERTS, 1), lambda i: (i, 0, 0))],
        scratch_shapes=[pltpu.VMEM((6 * B_OUT_W // LANES, tm, LANES), F32)],
        compiler_params=_cparams(1),
        name="merge_route",
    )(x2, ya, *outs, *lses, sga, sgb, mod3, norm2.reshape(1, D_MODEL), *weights)


def _plan_kernel(ridx_ref, rank_ref, tb_ref, dest_ref, *, tm):
    iota_e = lax.broadcasted_iota(I32, (N_EXPERTS, tm), 0)
    for j in range(dest_ref.shape[0]):
        tb = tb_ref[j]
        cols = slice(j * tm, (j + 1) * tm)
        for k in range(TOP_K):
            base = jnp.sum(jnp.where(iota_e == ridx_ref[k:k + 1, cols], tb, 0), axis=0, keepdims=True)
            dest_ref[j, :, k * tm:(k + 1) * tm] = base + rank_ref[k:k + 1, cols]


def _plan(ridx, rank, tile_base, tm):
    n = ridx.shape[1]
    tiles = math.gcd(PLAN_TILES, n // tm)
    lanes = lambda i: (0, i)
    return pl.pallas_call(
        functools.partial(_plan_kernel, tm=tm),
        out_shape=jax.ShapeDtypeStruct((n // tm, 1, TOP_K * tm), I32),
        grid=(n // tm // tiles,),
        in_specs=[pl.BlockSpec((SUBLANES, tiles * tm), lanes), pl.BlockSpec((SUBLANES, tiles * tm), lanes),
                  pl.BlockSpec((tiles, N_EXPERTS, 1), lambda i: (i, 0, 0))],
        out_specs=pl.BlockSpec((tiles, 1, TOP_K * tm), lambda i: (i, 0, 0)),
        compiler_params=_cparams(1),
        name="moe_plan",
    )(ridx, rank, tile_base)


def _sc_mesh():
    return plsc.VectorSubcoreMesh(core_axis_name="core", subcore_axis_name="subcore",
                                  num_cores=SC_CORES, num_subcores=SC_SUBCORES)


def _sc_worker():
    return lax.axis_index("subcore") * SC_CORES + lax.axis_index("core")


def _sc_dispatch(hp, dest_rows, n_rows):
    n = hp.shape[0]
    halves = TM_MERGE // SC_CHUNK
    workers = SC_CORES * SC_SUBCORES
    tiles_per_w = n // TM_MERGE // workers
    idx_per_w = tiles_per_w * TOP_K * halves
    assert tiles_per_w * workers * TM_MERGE == n

    @functools.partial(
        pl.kernel, mesh=_sc_mesh(),
        out_type=jax.ShapeDtypeStruct((n_rows, ROW_SLABS, LANES), U32),
        scratch_types=[pltpu.VMEM((idx_per_w, SC_CHUNK), I32),
                       pltpu.VMEM((SC_CHUNK, ROW_SLABS, LANES), U32),
                       pltpu.SemaphoreType.DMA],
        name="moe_dispatch_sc")
    def scatter(hp_hbm, idx_hbm, xs_hbm, idx_v, rows_v, sem):
        wid = _sc_worker()
        pltpu.sync_copy(idx_hbm.at[pl.ds(wid * idx_per_w, idx_per_w)], idx_v)

        @pl.loop(0, tiles_per_w * halves)
        def _(j):
            tile = j // halves
            h = j - tile * halves
            tok = (wid * tiles_per_w + tile) * TM_MERGE + h * SC_CHUNK
            pltpu.sync_copy(hp_hbm.at[pl.ds(tok, SC_CHUNK)], rows_v)
            for k in range(TOP_K):
                row = (tile * TOP_K + k) * halves + h
                pltpu.async_copy(rows_v, xs_hbm.at[idx_v.at[row]], sem).wait()

    return scatter(hp, dest_rows)


def _sc_collect(ys, dest_rows):
    n_chunks = dest_rows.shape[0]
    workers = SC_CORES * SC_SUBCORES
    per_w = n_chunks // workers
    assert per_w * workers == n_chunks

    @functools.partial(
        pl.kernel, mesh=_sc_mesh(),
        out_type=jax.ShapeDtypeStruct((n_chunks * SC_CHUNK, ROW_SLABS, LANES), U32),
        scratch_types=[pltpu.VMEM((per_w, SC_CHUNK), I32),
                       pltpu.VMEM((SC_CHUNK, ROW_SLABS, LANES), U32),
                       pltpu.SemaphoreType.DMA],
        name="moe_collect_sc")
    def gather(ys_hbm, idx_hbm, out_hbm, idx_v, rows_v, sem):
        wid = _sc_worker()
        base = wid * per_w
        pltpu.sync_copy(idx_hbm.at[pl.ds(base, per_w)], idx_v)

        @pl.loop(0, per_w)
        def _(j):
            pltpu.async_copy(ys_hbm.at[idx_v.at[j]], rows_v, sem).wait()
            pltpu.sync_copy(rows_v, out_hbm.at[pl.ds((base + j) * SC_CHUNK, SC_CHUNK)])

    return gather(ys, dest_rows)


def _expert_kernel(blk_e_ref, valid_ref, src_ref, xs_hbm, wg_ref, wu_ref, wd_ref, ys_ref,
                   wg_bf, wu_bf, wd_bf, xbuf, sem, *, nblk):
    del src_ref
    step = pl.program_id(0)
    valid = valid_ref[step]
    rows = xbuf.shape[1]

    def fetch(t):
        slot = lax.rem(t, EXPERT_RING)
        return pltpu.make_async_copy(xs_hbm.at[pl.ds(pl.multiple_of(t * rows, rows), rows)],
                                     xbuf.at[slot], sem.at[slot])

    def request(t):
        @pl.when((t < nblk) & (valid_ref[jnp.minimum(t, nblk - 1)] > 0))
        def _():
            fetch(t).start()

    @pl.when(step == 0)
    def _():
        for t in range(EXPERT_RING - 1):
            request(jnp.int32(t))

    request(step + EXPERT_RING - 1)

    @pl.when((step == 0) | (blk_e_ref[step] != blk_e_ref[jnp.maximum(step - 1, 0)]))
    def _():
        wg_bf[...] = wg_ref[0].astype(BF16)
        wu_bf[...] = wu_ref[0].astype(BF16)
        wd_bf[...] = wd_ref[0].astype(BF16)

    @pl.when(valid > 0)
    def _():
        fetch(step).wait()
        xs_ref = xbuf.at[lax.rem(step, EXPERT_RING)]
        row_id = lax.broadcasted_iota(I32, (rows // ROW_SLABS, 1), 0)
        lo, hi = _unpack_rows(jnp.where(row_id < valid, _load_slabs(xs_ref), jnp.uint32(0)))
        lo, hi = lo.astype(BF16), hi.astype(BF16)

        def up(w_bf):
            return jnp.dot(lo, w_bf[:HALF_D], preferred_element_type=F32) \
                + jnp.dot(hi, w_bf[HALF_D:], preferred_element_type=F32)

        g = up(wg_bf)
        act = (g * jax.nn.sigmoid(g) * up(wu_bf)).astype(BF16)
        _store_slabs(ys_ref, _pack_rows(jnp.dot(act, wd_bf[...], preferred_element_type=F32)))


def _experts(xs, blk_e, blk_valid, blk_src, wg, wu, wd):
    n_rows = xs.shape[0] // ROW_SLABS
    bm = MOE_BM
    nblk = n_rows // bm
    return pl.pallas_call(
        functools.partial(_expert_kernel, nblk=nblk),
        out_shape=jax.ShapeDtypeStruct((n_rows * ROW_SLABS, LANES), U32),
        grid_spec=pltpu.PrefetchScalarGridSpec(
            num_scalar_prefetch=3,
            grid=(nblk,),
            in_specs=[pl.BlockSpec(memory_space=pl.ANY),
                      pl.BlockSpec((1, D_MODEL, D_EXPERT), lambda i, be, bv, bs: (be[i], 0, 0)),
                      pl.BlockSpec((1, D_MODEL, D_EXPERT), lambda i, be, bv, bs: (be[i], 0, 0)),
                      pl.BlockSpec((1, D_EXPERT, D_MODEL), lambda i, be, bv, bs: (be[i], 0, 0))],
            out_specs=pl.BlockSpec((bm * ROW_SLABS, LANES), lambda i, be, bv, bs: (bs[i], 0)),
            scratch_shapes=[pltpu.VMEM((D_MODEL, D_EXPERT), BF16), pltpu.VMEM((D_MODEL, D_EXPERT), BF16),
                            pltpu.VMEM((D_EXPERT, D_MODEL), BF16),
                            pltpu.VMEM((EXPERT_RING, bm * ROW_SLABS, LANES), U32),
                            pltpu.SemaphoreType.DMA((EXPERT_RING,))]),
        compiler_params=_cparams(1),
        name="moe_experts",
    )(blk_e, blk_valid, blk_src, xs, wg, wu, wd)


def _combine_kernel(g_ref, base_ref, rw_ref, mod_ref, fn_ref, *rest):
    y_ref = rest[-1]
    tm = base_ref.shape[0]
    rw = rw_ref[...]
    acc_lo = jnp.zeros((tm, HALF_D), F32)
    acc_hi = jnp.zeros((tm, HALF_D), F32)
    for k in range(TOP_K):
        lo, hi = _unpack_rows(_load_slabs(g_ref.at[0, k]))
        w = rw[:, k:k + 1]
        acc_lo = acc_lo + w * lo
        acc_hi = acc_hi + w * hi
    routed = jnp.concatenate([acc_lo, acc_hi], axis=1)
    x2 = base_ref[...] + mod_ref[0][G2:G2 + 1] * routed
    y_ref[...] = _rms(x2, fn_ref[...])


def _combine(g, base, rw_t, mod3, final_norm, seq, part, y_prev):
    n = base.shape[0]
    tm = TM_COMBINE
    assert seq % tm == 0
    g4 = g.reshape(-1, TOP_K, tm * ROW_SLABS, LANES)
    tiles = g4.shape[0]
    off = part * tiles
    row = lambda i: (i + off, 0)
    in_specs = [pl.BlockSpec((1, TOP_K, tm * ROW_SLABS, LANES), lambda i: (i, 0, 0, 0)),
                pl.BlockSpec((tm, D_MODEL), row),
                pl.BlockSpec((tm, SUBLANES), row),
                pl.BlockSpec((1, N_MOD, D_MODEL), lambda i: ((i + off) * tm // seq, 0, 0)),
                pl.BlockSpec((1, D_MODEL), lambda i: (0, 0))]
    args = [g4, base, rw_t, mod3, final_norm.reshape(1, D_MODEL)]
    aliases = {}
    if y_prev is not None:
        in_specs.append(pl.BlockSpec(memory_space=pl.ANY))
        args.append(y_prev)
        aliases = {len(args) - 1: 0}
    return pl.pallas_call(
        _combine_kernel,
        out_shape=jax.ShapeDtypeStruct((n, D_MODEL), F32),
        grid=(tiles,),
        in_specs=in_specs,
        out_specs=pl.BlockSpec((tm, D_MODEL), row),
        input_output_aliases=aliases,
        compiler_params=_cparams(1),
        name="moe_combine",
    )(*args)


def _block_layout(tile_counts, n):
    bm = MOE_BM
    c = tile_counts[:, :, 0]
    ntiles = c.shape[0]
    counts = jnp.sum(c, axis=0)
    padded = (counts + bm - 1) // bm * bm
    earlier_e = np.tri(N_EXPERTS, k=-1, dtype=bool)
    pstart = jnp.sum(jnp.where(earlier_e, padded[None, :], 0), axis=1)
    pend = pstart + padded
    earlier_t = np.tri(ntiles, k=-1, dtype=bool)
    tile_base = pstart[None, :] + jnp.sum(jnp.where(earlier_t[:, :, None], c[None], 0), axis=1)
    nblk = -(-n * TOP_K // bm) + N_EXPERTS
    blk = jnp.arange(nblk, dtype=I32)
    blk_e = jnp.minimum(jnp.sum((pend[None, :] <= blk[:, None] * bm).astype(I32), axis=1),
                        N_EXPERTS - 1)
    onehot_e = blk_e[:, None] == jnp.arange(N_EXPERTS, dtype=I32)[None, :]
    end_e = jnp.sum(jnp.where(onehot_e, (pstart + counts)[None, :], 0), axis=1)
    blk_valid = jnp.clip(end_e - blk * bm, 0, bm).astype(I32)
    blk_src = jnp.minimum(blk, pend[N_EXPERTS - 1] // bm - 1).astype(I32)
    last_e = jnp.max(jnp.where(counts > 0, jnp.arange(N_EXPERTS, dtype=I32), 0))
    blk_e = jnp.where(blk == blk_src, blk_e, last_e).astype(I32)
    return tile_base.astype(I32)[:, :, None], blk_e, blk_valid, blk_src, nblk * bm


def _front(x, mod3, p, after=None):
    nbatch, seq, _ = x.shape
    n = nbatch * seq
    x2 = x.reshape(n, D_MODEL)
    proj = _inproj(x2, mod3, p["norm1"], p["w_in"], seq)
    qa, kva = proj[:2]
    sga, sgb = proj[8:]
    (ya,) = _band_attention(qa, kva, p["bias_a"], p["sink_a"], nbatch=nbatch, seq=seq, dil=1,
                            halo=A_HALF_WINDOW, step_rows=ATTN_ROWS_A, shared_kv=True, want_lse=False)
    outs, lses = [], []
    for gi, (w, d) in enumerate(B_GROUPS):
        q, kv = proj[2 + 2 * gi:4 + 2 * gi]
        o, lse = _band_attention(q, kv, p["bias_b"][gi], None, nbatch=nbatch, seq=seq, dil=d,
                                 halo=w // (2 * d), step_rows=ATTN_ROWS_B, shared_kv=False,
                                 want_lse=True)
        outs.append(o)
        lses.append(lse)
    if after is not None:
        ya, after = lax.optimization_barrier((ya, after))
    base, hp, ridx, rw, rank, tile_counts = _merge(
        x2, ya, outs, lses, sga, sgb, mod3, p["norm2"], p["w_pa"], p["w_pb"], p["w_o"], p["w_rt"],
        p["rbias"], p["ws_gate"], p["ws_up"], p["ws_down"], seq)
    assert TM_MERGE == TM_COMBINE
    tile_base, blk_e, blk_valid, blk_src, n_rows = _block_layout(tile_counts, n)
    dest_rows = _plan(ridx, rank, tile_base, TM_MERGE).reshape(-1, SC_CHUNK)
    return dict(base=base, hp=hp, rw=rw, dest_rows=dest_rows, blk_e=blk_e, blk_valid=blk_valid,
                blk_src=blk_src, n_rows=n_rows, shape=x.shape, after=after)


def _slabs(a):
    return a.reshape(-1, ROW_SLABS, LANES)


def _moe(st, xs, mod3, p):
    nbatch, seq, _ = st["shape"]
    dest_rows = st["dest_rows"]
    ys = _experts(xs.reshape(-1, LANES), st["blk_e"], st["blk_valid"], st["blk_src"], p["w_gate"], p["w_up"],
                  p["w_down"])
    rows_per_part = dest_rows.shape[0] // COLLECT_PARTS
    rw_t = st["rw"].T
    y = None
    for part in range(COLLECT_PARTS):
        g = _sc_collect(_slabs(ys), dest_rows[part * rows_per_part:(part + 1) * rows_per_part])
        y = _combine(g, st["base"], rw_t, mod3, p["final_norm"], seq, part, y)
    return y.reshape(nbatch, seq, D_MODEL)


def kernel(x_prompt, x_sample, c_prompt, c_sample, rel_bias, w_ada, b_ada, norm1, w_in, sink, w_pa, w_pb, w_o, norm2, w_router, router_bias, w_gate, w_up, w_down, ws_gate, ws_up, ws_down, final_norm):
    assert w_ada.shape[0] == 1
    nbp = x_prompt.shape[0]
    mod = _ada(jnp.concatenate([c_prompt, c_sample], axis=0), w_ada[0], b_ada[0])
    mod3 = mod.reshape(-1, N_MOD, D_MODEL)

    def pair_bias(heads, halo, dist_scale):
        kb = ATTN_BLOCK + 2 * halo
        tab = _bias_table(heads, ATTN_BLOCK, kb, halo, halo, dist_scale)
        return tab.reshape(-1, 2, ATTN_BLOCK, kb).transpose(0, 2, 1, 3).reshape(-1, ATTN_BLOCK, 2 * kb)

    bias_a = pair_bias(rel_bias[:, :A_Q_HEADS], A_HALF_WINDOW, 1)
    bias_b = []
    for gi, (w, d) in enumerate(B_GROUPS):
        h0 = A_Q_HEADS + gi * B_HEADS_PER_GROUP
        bias_b.append(pair_bias(rel_bias[:, h0:h0 + B_HEADS_PER_GROUP], w // (2 * d), d))
    p = {
        "norm1": norm1[0], "norm2": norm2[0], "final_norm": final_norm,
        "w_in": w_in[0].astype(BF16),
        "bias_a": bias_a,
        "sink_a": jnp.repeat(sink[0].astype(F32), HEAD_DIM).reshape(A_Q_HEADS // 2, 1, LANES),
        "bias_b": bias_b,
        "w_pa": w_pa[0].astype(BF16), "w_pb": w_pb[0].astype(BF16), "w_o": w_o[0].astype(BF16),
        "w_rt": w_router[0].T, "rbias": router_bias[0].reshape(N_EXPERTS, 1),
        "ws_gate": ws_gate[0].astype(BF16), "ws_up": ws_up[0].astype(BF16),
        "ws_down": ws_down[0].astype(BF16),
        "w_gate": w_gate[0], "w_up": w_up[0], "w_down": w_down[0],
    }
    dispatch = lambda st: _sc_dispatch(_slabs(st["hp"]), st["dest_rows"], st["n_rows"])
    st_p = _front(x_prompt, mod3[:nbp], p)
    st_s = _front(x_sample, mod3[nbp:], p, after=dispatch(st_p))
    return (_moe(st_p, st_s["after"], mod3[:nbp], p), _moe(st_s, dispatch(st_s), mod3[nbp:], p))
```

```python
import functools
import math

import jax
import jax.numpy as jnp
import numpy as np
from jax import lax
from jax.experimental import pallas as pl
from jax.experimental.pallas import tpu as pltpu
from jax.experimental.pallas import tpu_sc as plsc

F32 = jnp.float32
BF16 = jnp.bfloat16
U32 = jnp.uint32
I32 = jnp.int32

D_MODEL = 1024
HEAD_DIM = 64
A_Q_HEADS = 8
A_KV_HEADS = 2
A_HALF_WINDOW = 128
B_GROUPS = ((128, 1), (512, 4), (2048, 16))
B_HEADS_PER_GROUP = 4
N_BUCKETS = 32
MAX_DISTANCE = 1024
N_EXPERTS = 64
TOP_K = 6
N_EXPERT_GROUPS = 8
TOPK_GROUPS = 4
D_EXPERT = 256
ROUTED_SCALE = 2.5
RMS_EPS = 1e-6
NEG_INF = -1e30
N_MOD = 6
SH1, SC1, G1, SH2, SC2, G2 = range(N_MOD)
REMOVED = -3e38

A_Q_W = A_Q_HEADS * HEAD_DIM
A_KV_W = A_KV_HEADS * HEAD_DIM
B_W = len(B_GROUPS) * B_HEADS_PER_GROUP * HEAD_DIM
B_OUT_W = B_HEADS_PER_GROUP * HEAD_DIM
D_IN = A_Q_W + 2 * A_KV_W + 3 * B_W + 2 * D_MODEL
HALF_D = D_MODEL // 2
LANES = 128
SUBLANES = 8
ROW_SLABS = HALF_D // LANES
SC_CORES = 2
SC_SUBCORES = 16
SC_CHUNK = 128

TM_INPROJ = 1024
TM_MERGE = 512
ATTN_BLOCK = 128
ATTN_ROWS_A = 512
ATTN_ROWS_B = 2048
MOE_BM = 1024
EXPERT_RING = 3
TM_COMBINE = 512
COLLECT_PARTS = 2
PLAN_TILES = 8
VMEM_LIMIT = 48 * 1024 * 1024
VMEM_LIMIT_INPROJ = 60 * 1024 * 1024


def _cparams(n_axes, vmem_limit=VMEM_LIMIT):
    return pltpu.CompilerParams(
        dimension_semantics=("arbitrary",) * n_axes, vmem_limit_bytes=vmem_limit)


def _ada_kernel(c_ref, w_ref, b_ref, o_ref):
    c = c_ref[...]
    s = c * jax.nn.sigmoid(c)
    o_ref[...] = jnp.dot(s, w_ref[...], preferred_element_type=F32,
                         precision=lax.Precision.HIGHEST) + b_ref[...]


def _ada(c_all, w_ada, b_ada):
    nb = c_all.shape[0]
    return pl.pallas_call(
        _ada_kernel,
        out_shape=jax.ShapeDtypeStruct((nb, N_MOD * D_MODEL), F32),
        grid=(N_MOD,),
        in_specs=[pl.BlockSpec((nb, D_MODEL), lambda j: (0, 0)),
                  pl.BlockSpec((D_MODEL, D_MODEL), lambda j: (0, j)),
                  pl.BlockSpec((1, D_MODEL), lambda j: (0, j))],
        out_specs=pl.BlockSpec((nb, D_MODEL), lambda j: (0, j)),
        compiler_params=_cparams(1),
        name="ada",
    )(c_all, w_ada, b_ada.reshape(1, N_MOD * D_MODEL))


def _rms(x, g):
    return x * lax.rsqrt(jnp.mean(x * x, axis=-1, keepdims=True) + RMS_EPS) * g


def _inproj_kernel(x_ref, mod_ref, n1_ref, w_ref, qa, kva, *rest):
    q_b = rest[0:6:2]
    kv_b = rest[1:6:2]
    sga, sgb, scr = rest[6:]
    mod = mod_ref[0]
    h = _rms(x_ref[...], n1_ref[...]) * (1.0 + mod[SC1:SC1 + 1]) + mod[SH1:SH1 + 1]
    hb = h.astype(BF16)
    tm = hb.shape[0]
    scale = HEAD_DIM ** -0.5

    def proj(off, width):
        return jnp.dot(hb, w_ref[:, off:off + width], preferred_element_type=F32)

    qa[...] = (proj(0, A_Q_W) * scale).astype(BF16)
    for j, off in enumerate((A_Q_W, A_Q_W + A_KV_W)):
        r = proj(off, A_KV_W)
        kva[:, 2 * j * A_KV_W:(2 * j + 1) * A_KV_W] = r.astype(BF16)
        kva[:, (2 * j + 1) * A_KV_W:(2 * j + 2) * A_KV_W] = pltpu.roll(r, HEAD_DIM, 1).astype(BF16)
    off = A_Q_W + 2 * A_KV_W
    for t in range(3):
        for gi, (_, d) in enumerate(B_GROUPS):
            r = proj(off + t * B_W + gi * B_OUT_W, B_OUT_W)
            if t == 0:
                r = r * scale
            ref, width, base = (q_b[gi], B_OUT_W, 0) if t == 0 else (kv_b[gi], 2 * B_OUT_W,
                                                                     (t - 1) * B_OUT_W)
            if d == 1:
                ref[:, base:base + B_OUT_W] = r.astype(BF16)
            else:
                for c in range(B_OUT_W // LANES):
                    slot = (t * 2 + gi - 1) * (B_OUT_W // LANES) + c
                    scr[slot] = r[:, c * LANES:(c + 1) * LANES]
                    for res in range(d):
                        col = res * width + base + c * LANES
                        ref[:, col:col + LANES] = scr[
                            slot, pl.ds(res, tm // d, stride=d), :].astype(BF16)
    off += 3 * B_W
    sga[...] = jax.nn.sigmoid(proj(off, D_MODEL)).astype(BF16)
    sgb[...] = jax.nn.sigmoid(proj(off + D_MODEL, D_MODEL)).astype(BF16)


def _inproj(x2, mod3, norm1, w_in_bf, seq):
    n = x2.shape[0]
    tm = TM_INPROJ
    assert seq % tm == 0 and n % tm == 0
    row = lambda i: (i, 0)
    shapes = [(n, A_Q_W, tm), (n, 4 * A_KV_W, tm)]
    for _, d in B_GROUPS:
        shapes += [(n // d, d * B_OUT_W, tm // d), (n // d, d * 2 * B_OUT_W, tm // d)]
    shapes += [(n, D_MODEL, tm)] * 2
    return pl.pallas_call(
        _inproj_kernel,
        out_shape=[jax.ShapeDtypeStruct((r, c), BF16) for r, c, _ in shapes],
        grid=(n // tm,),
        in_specs=[pl.BlockSpec((tm, D_MODEL), row),
                  pl.BlockSpec((1, N_MOD, D_MODEL), lambda i: (i * tm // seq, 0, 0)),
                  pl.BlockSpec((1, D_MODEL), lambda i: (0, 0)),
                  pl.BlockSpec((D_MODEL, D_IN), lambda i: (0, 0), pipeline_mode=pl.Buffered(1))],
        out_specs=[pl.BlockSpec((b, c), row) for _, c, b in shapes],
        scratch_shapes=[pltpu.VMEM((6 * B_OUT_W // LANES, tm, LANES), F32)],
        compiler_params=_cparams(1, VMEM_LIMIT_INPROJ),
        name="inproj",
    )(x2, mod3, norm1.reshape(1, D_MODEL), w_in_bf)


def _rel_bucket_np(rel):
    half = N_BUCKETS // 2
    max_exact = half // 2
    n = np.abs(rel)
    large = max_exact + (np.log(np.maximum(n, 1) / max_exact) / math.log(MAX_DISTANCE / max_exact)
                         * (half - max_exact)).astype(np.int32)
    large = np.minimum(large, half - 1)
    return ((rel > 0).astype(np.int32) * half + np.where(n < max_exact, n, large)).astype(np.int32)


def _bias_table(rel_bias_heads, n_q, n_k, key_off, band, dist_scale):
    p = n_q + n_k
    rel = np.arange(p) - (n_q - 1) - key_off
    bucket = _rel_bucket_np(rel * dist_scale)
    t = jnp.where((np.abs(rel) <= band)[None], rel_bias_heads.astype(F32)[bucket].T, NEG_INF)
    big = jnp.tile(t, (1, n_q + 1))
    tab = big[:, n_q - 1:n_q - 1 + n_q * (p - 1)].reshape(-1, n_q, p - 1)
    return tab[:, :, :n_k]


def _pair_rhs(k_top, k_bot, v_top, v_bot):
    kb = k_top.shape[0]
    low = jnp.where(lax.broadcasted_iota(I32, (kb, LANES), 1) < HEAD_DIM, 1.0, 0.0).astype(BF16)
    high = jnp.where(lax.broadcasted_iota(I32, (kb, LANES), 1) < HEAD_DIM, 0.0, 1.0).astype(BF16)
    rhs_k = jnp.concatenate([k_top * low, k_bot * high], axis=0)
    rhs_v = jnp.concatenate([jnp.concatenate([v_top * low, low], axis=1),
                             jnp.concatenate([v_bot * high, high], axis=1)], axis=0)
    return rhs_k, rhs_v


def _pair_attention(q_pair, rhs_k, rhs_v, bias_pair, pen, sink_pair, want_lse):
    kb = rhs_k.shape[0] // 2
    s = lax.dot_general(q_pair, rhs_k, (((1,), (1,)), ((), ())), preferred_element_type=F32)
    s = s + bias_pair
    if pen is not None:
        s = s + pen
    s0, s1 = s[:, :kb], s[:, kb:]
    m0 = jnp.max(s0, axis=-1, keepdims=True)
    m1 = jnp.max(s1, axis=-1, keepdims=True)
    if sink_pair is not None:
        m0 = jnp.maximum(m0, sink_pair[:, 0:1])
        m1 = jnp.maximum(m1, sink_pair[:, HEAD_DIM:HEAD_DIM + 1])
    p = jnp.concatenate([jnp.exp(s0 - m0), jnp.exp(s1 - m1)], axis=1).astype(BF16)
    od = jnp.dot(p, rhs_v, preferred_element_type=F32)
    o, den = od[:, :LANES], od[:, LANES:]
    low = lax.broadcasted_iota(I32, o.shape, 1) < HEAD_DIM
    m_full = jnp.where(low, m0, m1)
    if sink_pair is not None:
        den = den + jnp.exp(sink_pair - m_full)
    return o / den, (m_full + jnp.log(den)) if want_lse else None


def _attn_kernel(*refs, halo, nblk, n_pairs, n_seqs, shared_kv, has_sink, want_lse, n_steps):
    it = iter(refs)
    q_ref = next(it)
    kv_prev, kv_cur, kv_next = next(it), next(it), next(it)
    bias_ref = next(it)
    sink_ref = next(it) if has_sink else None
    o_ref = next(it)
    lse_ref = next(it) if want_lse else None
    step = pl.program_id(2)
    blk = ATTN_BLOCK
    kb = blk + 2 * halo
    qw = n_pairs * LANES
    kv = jnp.concatenate([kv_prev[...], kv_cur[...], kv_next[...]], axis=0)
    col = lax.broadcasted_iota(I32, (1, kb), 1)
    for b in range(nblk):
        lo = jnp.where(step == 0, halo, 0) if b == 0 else 0
        hi = jnp.where(step == n_steps - 1, kb - halo, kb) if b == nblk - 1 else kb
        pen = None
        if b == 0 or b == nblk - 1:
            pen1 = jnp.where((col >= lo) & (col < hi), 0.0, NEG_INF)
            pen = jnp.concatenate([pen1, pen1], axis=1)
        rows = slice(b * blk, (b + 1) * blk)
        krows = slice(b * blk, b * blk + kb)
        lane_col = lambda j: kv[krows, j * LANES:(j + 1) * LANES]
        if shared_kv:
            k, k_sw, v, v_sw = (lane_col(j) for j in range(4))
            rhs = [_pair_rhs(k, k_sw, v, v_sw), _pair_rhs(k_sw, k, v_sw, v)]
        for s in range(n_seqs):
            for c in range(n_pairs):
                lanes = slice(s * qw + c * LANES, s * qw + (c + 1) * LANES)
                if shared_kv:
                    rhs_k, rhs_v = rhs[c // (n_pairs // 2)]
                else:
                    k, v = lane_col(2 * s * n_pairs + c), lane_col((2 * s + 1) * n_pairs + c)
                    rhs_k, rhs_v = _pair_rhs(k, k, v, v)
                o, lse = _pair_attention(q_ref[rows, lanes], rhs_k, rhs_v, bias_ref[c], pen,
                                         sink_ref[c] if has_sink else None, want_lse)
                o_ref[rows, lanes] = o.astype(o_ref.dtype)
                if want_lse:
                    lse_ref[rows, lanes] = lse


def _band_attention(q, kv, bias, sink, *, nbatch, seq, dil, halo, step_rows, shared_kv, want_lse):
    n = nbatch * seq
    sub_len = seq // dil
    qw = q.shape[1] // dil
    kv_width = kv.shape[1] // dil
    rows = min(step_rows, sub_len)
    n_seqs = min(dil, step_rows // rows)
    assert sub_len % rows == 0 and rows % ATTN_BLOCK == 0 and rows % halo == 0 and dil % n_seqs == 0
    nq = sub_len // rows
    per = rows // halo
    total_halos = n // dil // halo
    cur = lambda b, r, i: (b * nq + i, r)
    prev = lambda b, r, i: (jnp.maximum((b * nq + i) * per - 1, 0), r)
    nxt = lambda b, r, i: (jnp.minimum((b * nq + i + 1) * per, total_halos - 1), r)
    const = lambda a: pl.BlockSpec(a.shape, lambda b, r, i: (0,) * a.ndim)
    qw_step, kvw_step = qw * n_seqs, kv_width * n_seqs

    in_specs = [pl.BlockSpec((rows, qw_step), cur), pl.BlockSpec((halo, kvw_step), prev),
                pl.BlockSpec((rows, kvw_step), cur), pl.BlockSpec((halo, kvw_step), nxt),
                const(bias)]
    args = [q, kv, kv, kv, bias]
    if sink is not None:
        in_specs.append(const(sink))
        args.append(sink)
    out_shape = [jax.ShapeDtypeStruct(q.shape, BF16)]
    out_specs = [pl.BlockSpec((rows, qw_step), cur)]
    if want_lse:
        out_shape.append(jax.ShapeDtypeStruct(q.shape, F32))
        out_specs.append(pl.BlockSpec((rows, qw_step), cur))
    return pl.pallas_call(
        functools.partial(_attn_kernel, halo=halo, nblk=rows // ATTN_BLOCK, n_pairs=qw // LANES,
                          n_seqs=n_seqs, shared_kv=shared_kv, has_sink=sink is not None,
                          want_lse=want_lse, n_steps=nq),
        out_shape=out_shape,
        grid=(nbatch, dil // n_seqs, nq),
        in_specs=in_specs,
        out_specs=out_specs,
        compiler_params=_cparams(3),
        name=f"band_attn_d{dil}",
    )(*args)


def _pack_rows(y):
    return pltpu.pack_elementwise([y[:, :HALF_D], y[:, HALF_D:]], packed_dtype=BF16)


def _store_slabs(ref, packed):
    t = packed.shape[0]
    for c in range(ROW_SLABS):
        ref[pl.ds(c, t, stride=ROW_SLABS), :] = packed[:, c * LANES:(c + 1) * LANES]


def _load_slabs(ref):
    t = ref.shape[0] // ROW_SLABS
    return jnp.concatenate([ref[pl.ds(c, t, stride=ROW_SLABS), :] for c in range(ROW_SLABS)], axis=1)


def _unpack_rows(p):
    return tuple(pltpu.unpack_elementwise(p, index=i, packed_dtype=BF16, unpacked_dtype=F32)
                 for i in range(2))


def _route(sel, scores):
    t = sel.shape[-1]
    per = N_EXPERTS // N_EXPERT_GROUPS
    shape3 = (N_EXPERT_GROUPS, per, t)
    sel3 = sel.reshape(shape3)
    sc3 = scores.reshape(shape3)
    iota_g = lax.broadcasted_iota(I32, shape3, 0)
    iota_m = lax.broadcasted_iota(I32, shape3, 1)
    iota_e = iota_g * per + iota_m
    m1 = jnp.max(sel3, axis=1, keepdims=True)
    i1 = jnp.min(jnp.where(sel3 == m1, iota_m, per), axis=1, keepdims=True)
    m2 = jnp.max(jnp.where(iota_m == i1, REMOVED, sel3), axis=1, keepdims=True)
    gscore = m1 + m2
    iota_g1 = lax.broadcasted_iota(I32, gscore.shape, 0)
    gmask = jnp.zeros(gscore.shape, jnp.bool_)
    for _ in range(TOPK_GROUPS):
        mx = jnp.max(gscore, axis=0, keepdims=True)
        ix = jnp.min(jnp.where(gscore == mx, iota_g1, N_EXPERT_GROUPS), axis=0, keepdims=True)
        hit = iota_g1 == ix
        gmask = gmask | hit
        gscore = jnp.where(hit, REMOVED, gscore)
    cur = jnp.where(gmask, sel3, NEG_INF)
    sum_all = lambda a: jnp.sum(jnp.sum(a, axis=1, keepdims=True), axis=0, keepdims=True)
    idxs, wts, hits = [], [], []
    for _ in range(TOP_K):
        mx = jnp.max(jnp.max(cur, axis=1, keepdims=True), axis=0, keepdims=True)
        cand = jnp.where(cur == mx, iota_e, N_EXPERTS)
        ix = jnp.min(jnp.min(cand, axis=1, keepdims=True), axis=0, keepdims=True)
        hit = iota_e == ix
        wts.append(sum_all(jnp.where(hit, sc3, 0.0)).reshape(1, t))
        cur = jnp.where(hit, REMOVED, cur)
        idxs.append(ix.reshape(1, t))
        hits.append(hit)
    wsum = wts[0]
    for w in wts[1:]:
        wsum = wsum + w
    wts = [w / wsum * ROUTED_SCALE for w in wts]
    onehot = hits[0].astype(F32)
    for hit in hits[1:]:
        onehot = onehot + hit.astype(F32)
    tri = (lax.broadcasted_iota(I32, (t, t), 0) <= lax.broadcasted_iota(I32, (t, t), 1))
    cum = jnp.dot(onehot.reshape(N_EXPERTS, t).astype(BF16), tri.astype(F32).astype(BF16),
                  preferred_element_type=F32)
    cum3 = cum.reshape(shape3) - 1.0
    ranks = [sum_all(jnp.where(hit, cum3, 0.0)).reshape(1, t).astype(I32) for hit in hits]
    counts = cum[:, t - 1:t].astype(I32)
    return idxs, wts, ranks, counts


def _merge_kernel(x_ref, ya_ref, o1, o2, o3, l1, l2, l3, sga_ref, sgb_ref, mod_ref, n2_ref,
                  wpa, wpb, wo, wrt, rbias, wsg, wsu, wsd,
                  base_ref, hp_ref, ridx_ref, rw_ref, rank_ref, cnt_ref, scr):
    mod = mod_ref[0]
    tm = x_ref.shape[0]

    def token_major(ref, slot, d):
        if d == 1:
            return ref[...].astype(F32)
        nc = B_OUT_W // LANES
        for c in range(nc):
            for res in range(d):
                col = res * B_OUT_W + c * LANES
                scr[slot * nc + c, pl.ds(res, tm // d, stride=d), :] = ref[
                    :, col:col + LANES].astype(F32)
        return jnp.concatenate([scr[slot * nc + c] for c in range(nc)], axis=1)

    dils = [d for _, d in B_GROUPS]
    os_ = [token_major(r, i, d) for i, (r, d) in enumerate(zip((o1, o2, o3), dils))]
    ls = [token_major(r, 3 + i, d) for i, (r, d) in enumerate(zip((l1, l2, l3), dils))]
    mx = jnp.maximum(jnp.maximum(ls[0], ls[1]), ls[2])
    es = [jnp.exp(l - mx) for l in ls]
    den = es[0] + es[1] + es[2]
    ob = (es[0] / den) * os_[0] + (es[1] / den) * os_[1] + (es[2] / den) * os_[2]
    pa = jnp.dot(ya_ref[...], wpa[...], preferred_element_type=F32)
    pb = jnp.dot(ob.astype(BF16), wpb[...], preferred_element_type=F32)
    merged = sga_ref[...].astype(F32) * pa + sgb_ref[...].astype(F32) * pb
    mix = jnp.dot(merged.astype(BF16), wo[...], preferred_element_type=F32)
    x1 = x_ref[...] + mod[G1:G1 + 1] * mix
    h2 = _rms(x1, n2_ref[...]) * (1.0 + mod[SC2:SC2 + 1]) + mod[SH2:SH2 + 1]
    logits = lax.dot_general(wrt[...], h2, (((1,), (1,)), ((), ())), preferred_element_type=F32,
                             precision=lax.Precision.HIGHEST)
    scores = jax.nn.sigmoid(logits)
    idxs, wts, ranks, counts = _route(scores + rbias[...], scores)
    for k in range(TOP_K):
        ridx_ref[k:k + 1, :] = idxs[k]
        rw_ref[k:k + 1, :] = wts[k]
        rank_ref[k:k + 1, :] = ranks[k]
    ridx_ref[TOP_K:, :] = jnp.zeros((SUBLANES - TOP_K, tm), I32)
    rw_ref[TOP_K:, :] = jnp.zeros((SUBLANES - TOP_K, tm), F32)
    rank_ref[TOP_K:, :] = jnp.zeros((SUBLANES - TOP_K, tm), I32)
    cnt_ref[0] = counts
    hb = h2.astype(BF16)
    g = jnp.dot(hb, wsg[...], preferred_element_type=F32)
    u = jnp.dot(hb, wsu[...], preferred_element_type=F32)
    act = (g * jax.nn.sigmoid(g) * u).astype(BF16)
    shared = jnp.dot(act, wsd[...], preferred_element_type=F32)
    base_ref[...] = x1 + mod[G2:G2 + 1] * shared
    _store_slabs(hp_ref, _pack_rows(h2))


def _merge(x2, ya, outs, lses, sga, sgb, mod3, norm2, wpa, wpb, wo, wrt, rbias, wsg, wsu, wsd, seq):
    n = x2.shape[0]
    tm = TM_MERGE
    assert seq % tm == 0
    row = lambda i: (i, 0)
    full = lambda a: pl.BlockSpec(a.shape, lambda i: (0,) * a.ndim)
    weights = [wpa, wpb, wo, wrt, rbias, wsg, wsu, wsd]
    group_specs = [pl.BlockSpec((tm // d, d * B_OUT_W), row) for _, d in B_GROUPS]
    lanes = lambda i: (0, i)
    return pl.pallas_call(
        _merge_kernel,
        out_shape=[jax.ShapeDtypeStruct((n, D_MODEL), F32),
                   jax.ShapeDtypeStruct((n * ROW_SLABS, LANES), U32),
                   jax.ShapeDtypeStruct((SUBLANES, n), I32),
                   jax.ShapeDtypeStruct((SUBLANES, n), F32),
                   jax.ShapeDtypeStruct((SUBLANES, n), I32),
                   jax.ShapeDtypeStruct((n // tm, N_EXPERTS, 1), I32)],
        grid=(n // tm,),
        in_specs=[pl.BlockSpec((tm, D_MODEL), row), pl.BlockSpec((tm, A_Q_W), row)]
                 + group_specs * 2
                 + [pl.BlockSpec((tm, D_MODEL), row)] * 2
                 + [pl.BlockSpec((1, N_MOD, D_MODEL), lambda i: (i * tm // seq, 0, 0)),
                    pl.BlockSpec((1, D_MODEL), lambda i: (0, 0))]
                 + [full(w) for w in weights],
        out_specs=[pl.BlockSpec((tm, D_MODEL), row),
                   pl.BlockSpec((tm * ROW_SLABS, LANES), row),
                   pl.BlockSpec((SUBLANES, tm), lanes), pl.BlockSpec((SUBLANES, tm), lanes),
                   pl.BlockSpec((SUBLANES, tm), lanes),
                   pl.BlockSpec((1, N_EXPERTS, 1), lambda i: (i, 0, 0))],
        scratch_shapes=[pltpu.VMEM((6 * B_OUT_W // LANES, tm, LANES), F32)],
        compiler_params=_cparams(1),
        name="merge_route",
    )(x2, ya, *outs, *lses, sga, sgb, mod3, norm2.reshape(1, D_MODEL), *weights)


def _plan_kernel(ridx_ref, rank_ref, tb_ref, dest_ref, *, tm):
    iota_e = lax.broadcasted_iota(I32, (N_EXPERTS, tm), 0)
    for j in range(dest_ref.shape[0]):
        tb = tb_ref[j]
        cols = slice(j * tm, (j + 1) * tm)
        for k in range(TOP_K):
            base = jnp.sum(jnp.where(iota_e == ridx_ref[k:k + 1, cols], tb, 0), axis=0, keepdims=True)
            dest_ref[j, :, k * tm:(k + 1) * tm] = base + rank_ref[k:k + 1, cols]


def _plan(ridx, rank, tile_base, tm):
    n = ridx.shape[1]
    tiles = math.gcd(PLAN_TILES, n // tm)
    lanes = lambda i: (0, i)
    return pl.pallas_call(
        functools.partial(_plan_kernel, tm=tm),
        out_shape=jax.ShapeDtypeStruct((n // tm, 1, TOP_K * tm), I32),
        grid=(n // tm // tiles,),
        in_specs=[pl.BlockSpec((SUBLANES, tiles * tm), lanes), pl.BlockSpec((SUBLANES, tiles * tm), lanes),
                  pl.BlockSpec((tiles, N_EXPERTS, 1), lambda i: (i, 0, 0))],
        out_specs=pl.BlockSpec((tiles, 1, TOP_K * tm), lambda i: (i, 0, 0)),
        compiler_params=_cparams(1),
        name="moe_plan",
    )(ridx, rank, tile_base)


def _sc_mesh():
    return plsc.VectorSubcoreMesh(core_axis_name="core", subcore_axis_name="subcore",
                                  num_cores=SC_CORES, num_subcores=SC_SUBCORES)


def _sc_worker():
    return lax.axis_index("subcore") * SC_CORES + lax.axis_index("core")


def _sc_dispatch(hp, dest_rows, n_rows):
    n = hp.shape[0]
    halves = TM_MERGE // SC_CHUNK
    workers = SC_CORES * SC_SUBCORES
    tiles_per_w = n // TM_MERGE // workers
    idx_per_w = tiles_per_w * TOP_K * halves
    assert tiles_per_w * workers * TM_MERGE == n

    @functools.partial(
        pl.kernel, mesh=_sc_mesh(),
        out_type=jax.ShapeDtypeStruct((n_rows, ROW_SLABS, LANES), U32),
        scratch_types=[pltpu.VMEM((idx_per_w, SC_CHUNK), I32),
                       pltpu.VMEM((SC_CHUNK, ROW_SLABS, LANES), U32),
                       pltpu.SemaphoreType.DMA],
        name="moe_dispatch_sc")
    def scatter(hp_hbm, idx_hbm, xs_hbm, idx_v, rows_v, sem):
        wid = _sc_worker()
        pltpu.sync_copy(idx_hbm.at[pl.ds(wid * idx_per_w, idx_per_w)], idx_v)

        @pl.loop(0, tiles_per_w * halves)
        def _(j):
            tile = j // halves
            h = j - tile * halves
            tok = (wid * tiles_per_w + tile) * TM_MERGE + h * SC_CHUNK
            pltpu.sync_copy(hp_hbm.at[pl.ds(tok, SC_CHUNK)], rows_v)
            for k in range(TOP_K):
                row = (tile * TOP_K + k) * halves + h
                pltpu.async_copy(rows_v, xs_hbm.at[idx_v.at[row]], sem).wait()

    return scatter(hp, dest_rows)


def _sc_collect(ys, dest_rows):
    n_chunks = dest_rows.shape[0]
    workers = SC_CORES * SC_SUBCORES
    per_w = n_chunks // workers
    assert per_w * workers == n_chunks

    @functools.partial(
        pl.kernel, mesh=_sc_mesh(),
        out_type=jax.ShapeDtypeStruct((n_chunks * SC_CHUNK, ROW_SLABS, LANES), U32),
        scratch_types=[pltpu.VMEM((per_w, SC_CHUNK), I32),
                       pltpu.VMEM((SC_CHUNK, ROW_SLABS, LANES), U32),
                       pltpu.SemaphoreType.DMA],
        name="moe_collect_sc")
    def gather(ys_hbm, idx_hbm, out_hbm, idx_v, rows_v, sem):
        wid = _sc_worker()
        base = wid * per_w
        pltpu.sync_copy(idx_hbm.at[pl.ds(base, per_w)], idx_v)

        @pl.loop(0, per_w)
        def _(j):
            pltpu.async_copy(ys_hbm.at[idx_v.at[j]], rows_v, sem).wait()
            pltpu.sync_copy(rows_v, out_hbm.at[pl.ds((base + j) * SC_CHUNK, SC_CHUNK)])

    return gather(ys, dest_rows)


def _expert_kernel(blk_e_ref, valid_ref, src_ref, xs_hbm, wg_ref, wu_ref, wd_ref, ys_hbm,
                   wg_bf, wu_bf, wd_bf, xbuf, sem, obuf, osem, *, nblk):
    step = pl.program_id(0)
    valid = valid_ref[step]
    rows = xbuf.shape[1]

    def fetch(t):
        slot = lax.rem(t, EXPERT_RING)
        return pltpu.make_async_copy(xs_hbm.at[pl.ds(pl.multiple_of(t * rows, rows), rows)],
                                     xbuf.at[slot], sem.at[slot])

    def request(t):
        @pl.when((t < nblk) & (valid_ref[jnp.minimum(t, nblk - 1)] > 0))
        def _():
            fetch(t).start()

    @pl.when(step == 0)
    def _():
        for t in range(EXPERT_RING - 1):
            request(jnp.int32(t))

    request(step + EXPERT_RING - 1)

    def store(t):
        slot = lax.rem(t, EXPERT_RING)
        return pltpu.make_async_copy(obuf.at[slot],
                                     ys_hbm.at[pl.ds(pl.multiple_of(t * rows, rows), rows)],
                                     osem.at[slot])

    @pl.when((step == 0) | (blk_e_ref[step] != blk_e_ref[jnp.maximum(step - 1, 0)]))
    def _():
        wg_bf[...] = wg_ref[0].astype(BF16)
        wu_bf[...] = wu_ref[0].astype(BF16)
        wd_bf[...] = wd_ref[0].astype(BF16)

    @pl.when(valid > 0)
    def _():
        fetch(step).wait()

        @pl.when(step >= EXPERT_RING)
        def _():
            store(step - EXPERT_RING).wait()

        xs_ref = xbuf.at[lax.rem(step, EXPERT_RING)]
        ys_ref = obuf.at[lax.rem(step, EXPERT_RING)]
        row_id = lax.broadcasted_iota(I32, (rows // ROW_SLABS, 1), 0)
        lo, hi = _unpack_rows(jnp.where(row_id < valid, _load_slabs(xs_ref), jnp.uint32(0)))
        lo, hi = lo.astype(BF16), hi.astype(BF16)

        def up(w_bf):
            return jnp.dot(lo, w_bf[:HALF_D], preferred_element_type=F32) \
                + jnp.dot(hi, w_bf[HALF_D:], preferred_element_type=F32)

        g = up(wg_bf)
        act = (g * jax.nn.sigmoid(g) * up(wu_bf)).astype(BF16)
        _store_slabs(ys_ref, _pack_rows(jnp.dot(act, wd_bf[...], preferred_element_type=F32)))
        store(step).start()

    @pl.when(step == nblk - 1)
    def _():
        n_used = src_ref[nblk - 1] + 1
        for j in range(EXPERT_RING):
            @pl.when(n_used - 1 - j >= 0)
            def _(j=j):
                store(n_used - 1 - j).wait()


def _experts(xs, blk_e, blk_valid, blk_src, wg, wu, wd):
    n_rows = xs.shape[0] // ROW_SLABS
    bm = MOE_BM
    nblk = n_rows // bm
    return pl.pallas_call(
        functools.partial(_expert_kernel, nblk=nblk),
        out_shape=jax.ShapeDtypeStruct((n_rows * ROW_SLABS, LANES), U32),
        grid_spec=pltpu.PrefetchScalarGridSpec(
            num_scalar_prefetch=3,
            grid=(nblk,),
            in_specs=[pl.BlockSpec(memory_space=pl.ANY),
                      pl.BlockSpec((1, D_MODEL, D_EXPERT), lambda i, be, bv, bs: (be[i], 0, 0)),
                      pl.BlockSpec((1, D_MODEL, D_EXPERT), lambda i, be, bv, bs: (be[i], 0, 0)),
                      pl.BlockSpec((1, D_EXPERT, D_MODEL), lambda i, be, bv, bs: (be[i], 0, 0))],
            out_specs=pl.BlockSpec(memory_space=pl.ANY),
            scratch_shapes=[pltpu.VMEM((D_MODEL, D_EXPERT), BF16), pltpu.VMEM((D_MODEL, D_EXPERT), BF16),
                            pltpu.VMEM((D_EXPERT, D_MODEL), BF16),
                            pltpu.VMEM((EXPERT_RING, bm * ROW_SLABS, LANES), U32),
                            pltpu.SemaphoreType.DMA((EXPERT_RING,)),
                            pltpu.VMEM((EXPERT_RING, bm * ROW_SLABS, LANES), U32),
                            pltpu.SemaphoreType.DMA((EXPERT_RING,))]),
        compiler_params=_cparams(1),
        name="moe_experts",
    )(blk_e, blk_valid, blk_src, xs, wg, wu, wd)


def _combine_kernel(g_ref, base_ref, rw_ref, mod_ref, fn_ref, *rest):
    y_ref = rest[-1]
    tm = base_ref.shape[0]
    rw = rw_ref[...]
    acc_lo = jnp.zeros((tm, HALF_D), F32)
    acc_hi = jnp.zeros((tm, HALF_D), F32)
    for k in range(TOP_K):
        lo, hi = _unpack_rows(_load_slabs(g_ref.at[0, k]))
        w = rw[:, k:k + 1]
        acc_lo = acc_lo + w * lo
        acc_hi = acc_hi + w * hi
    routed = jnp.concatenate([acc_lo, acc_hi], axis=1)
    x2 = base_ref[...] + mod_ref[0][G2:G2 + 1] * routed
    y_ref[...] = _rms(x2, fn_ref[...])


def _combine(g, base, rw_t, mod3, final_norm, seq, part, y_prev):
    n = base.shape[0]
    tm = TM_COMBINE
    assert seq % tm == 0
    g4 = g.reshape(-1, TOP_K, tm * ROW_SLABS, LANES)
    tiles = g4.shape[0]
    off = part * tiles
    row = lambda i: (i + off, 0)
    in_specs = [pl.BlockSpec((1, TOP_K, tm * ROW_SLABS, LANES), lambda i: (i, 0, 0, 0)),
                pl.BlockSpec((tm, D_MODEL), row),
                pl.BlockSpec((tm, SUBLANES), row),
                pl.BlockSpec((1, N_MOD, D_MODEL), lambda i: ((i + off) * tm // seq, 0, 0)),
                pl.BlockSpec((1, D_MODEL), lambda i: (0, 0))]
    args = [g4, base, rw_t, mod3, final_norm.reshape(1, D_MODEL)]
    aliases = {}
    if y_prev is not None:
        in_specs.append(pl.BlockSpec(memory_space=pl.ANY))
        args.append(y_prev)
        aliases = {len(args) - 1: 0}
    return pl.pallas_call(
        _combine_kernel,
        out_shape=jax.ShapeDtypeStruct((n, D_MODEL), F32),
        grid=(tiles,),
        in_specs=in_specs,
        out_specs=pl.BlockSpec((tm, D_MODEL), row),
        input_output_aliases=aliases,
        compiler_params=_cparams(1),
        name="moe_combine",
    )(*args)


def _block_layout(tile_counts, n):
    bm = MOE_BM
    c = tile_counts[:, :, 0]
    ntiles = c.shape[0]
    counts = jnp.sum(c, axis=0)
    padded = (counts + bm - 1) // bm * bm
    earlier_e = np.tri(N_EXPERTS, k=-1, dtype=bool)
    pstart = jnp.sum(jnp.where(earlier_e, padded[None, :], 0), axis=1)
    pend = pstart + padded
    earlier_t = np.tri(ntiles, k=-1, dtype=bool)
    tile_base = pstart[None, :] + jnp.sum(jnp.where(earlier_t[:, :, None], c[None], 0), axis=1)
    nblk = -(-n * TOP_K // bm) + N_EXPERTS
    blk = jnp.arange(nblk, dtype=I32)
    blk_e = jnp.minimum(jnp.sum((pend[None, :] <= blk[:, None] * bm).astype(I32), axis=1),
                        N_EXPERTS - 1)
    onehot_e = blk_e[:, None] == jnp.arange(N_EXPERTS, dtype=I32)[None, :]
    end_e = jnp.sum(jnp.where(onehot_e, (pstart + counts)[None, :], 0), axis=1)
    blk_valid = jnp.clip(end_e - blk * bm, 0, bm).astype(I32)
    blk_src = jnp.minimum(blk, pend[N_EXPERTS - 1] // bm - 1).astype(I32)
    last_e = jnp.max(jnp.where(counts > 0, jnp.arange(N_EXPERTS, dtype=I32), 0))
    blk_e = jnp.where(blk == blk_src, blk_e, last_e).astype(I32)
    return tile_base.astype(I32)[:, :, None], blk_e, blk_valid, blk_src, nblk * bm


def _front(x, mod3, p, after=None):
    nbatch, seq, _ = x.shape
    n = nbatch * seq
    x2 = x.reshape(n, D_MODEL)
    proj = _inproj(x2, mod3, p["norm1"], p["w_in"], seq)
    qa, kva = proj[:2]
    sga, sgb = proj[8:]
    (ya,) = _band_attention(qa, kva, p["bias_a"], p["sink_a"], nbatch=nbatch, seq=seq, dil=1,
                            halo=A_HALF_WINDOW, step_rows=ATTN_ROWS_A, shared_kv=True, want_lse=False)
    outs, lses = [], []
    for gi, (w, d) in enumerate(B_GROUPS):
        q, kv = proj[2 + 2 * gi:4 + 2 * gi]
        o, lse = _band_attention(q, kv, p["bias_b"][gi], None, nbatch=nbatch, seq=seq, dil=d,
                                 halo=w // (2 * d), step_rows=ATTN_ROWS_B, shared_kv=False,
                                 want_lse=True)
        outs.append(o)
        lses.append(lse)
    if after is not None:
        ya, after = lax.optimization_barrier((ya, after))
    base, hp, ridx, rw, rank, tile_counts = _merge(
        x2, ya, outs, lses, sga, sgb, mod3, p["norm2"], p["w_pa"], p["w_pb"], p["w_o"], p["w_rt"],
        p["rbias"], p["ws_gate"], p["ws_up"], p["ws_down"], seq)
    assert TM_MERGE == TM_COMBINE
    tile_base, blk_e, blk_valid, blk_src, n_rows = _block_layout(tile_counts, n)
    dest_rows = _plan(ridx, rank, tile_base, TM_MERGE).reshape(-1, SC_CHUNK)
    return dict(base=base, hp=hp, rw=rw, dest_rows=dest_rows, blk_e=blk_e, blk_valid=blk_valid,
                blk_src=blk_src, n_rows=n_rows, shape=x.shape, after=after)


def _slabs(a):
    return a.reshape(-1, ROW_SLABS, LANES)


def _moe(st, xs, mod3, p):
    nbatch, seq, _ = st["shape"]
    dest_rows = st["dest_rows"]
    ys = _experts(xs.reshape(-1, LANES), st["blk_e"], st["blk_valid"], st["blk_src"], p["w_gate"], p["w_up"],
                  p["w_down"])
    rows_per_part = dest_rows.shape[0] // COLLECT_PARTS
    rw_t = st["rw"].T
    y = None
    for part in range(COLLECT_PARTS):
        g = _sc_collect(_slabs(ys), dest_rows[part * rows_per_part:(part + 1) * rows_per_part])
        y = _combine(g, st["base"], rw_t, mod3, p["final_norm"], seq, part, y)
    return y.reshape(nbatch, seq, D_MODEL)


def kernel(x_prompt, x_sample, c_prompt, c_sample, rel_bias, w_ada, b_ada, norm1, w_in, sink, w_pa, w_pb, w_o, norm2, w_router, router_bias, w_gate, w_up, w_down, ws_gate, ws_up, ws_down, final_norm):
    assert w_ada.shape[0] == 1
    nbp = x_prompt.shape[0]
    mod = _ada(jnp.concatenate([c_prompt, c_sample], axis=0), w_ada[0], b_ada[0])
    mod3 = mod.reshape(-1, N_MOD, D_MODEL)

    def pair_bias(heads, halo, dist_scale):
        kb = ATTN_BLOCK + 2 * halo
        tab = _bias_table(heads, ATTN_BLOCK, kb, halo, halo, dist_scale)
        return tab.reshape(-1, 2, ATTN_BLOCK, kb).transpose(0, 2, 1, 3).reshape(-1, ATTN_BLOCK, 2 * kb)

    bias_a = pair_bias(rel_bias[:, :A_Q_HEADS], A_HALF_WINDOW, 1)
    bias_b = []
    for gi, (w, d) in enumerate(B_GROUPS):
        h0 = A_Q_HEADS + gi * B_HEADS_PER_GROUP
        bias_b.append(pair_bias(rel_bias[:, h0:h0 + B_HEADS_PER_GROUP], w // (2 * d), d))
    p = {
        "norm1": norm1[0], "norm2": norm2[0], "final_norm": final_norm,
        "w_in": w_in[0].astype(BF16),
        "bias_a": bias_a,
        "sink_a": jnp.repeat(sink[0].astype(F32), HEAD_DIM).reshape(A_Q_HEADS // 2, 1, LANES),
        "bias_b": bias_b,
        "w_pa": w_pa[0].astype(BF16), "w_pb": w_pb[0].astype(BF16), "w_o": w_o[0].astype(BF16),
        "w_rt": w_router[0].T, "rbias": router_bias[0].reshape(N_EXPERTS, 1),
        "ws_gate": ws_gate[0].astype(BF16), "ws_up": ws_up[0].astype(BF16),
        "ws_down": ws_down[0].astype(BF16),
        "w_gate": w_gate[0], "w_up": w_up[0], "w_down": w_down[0],
    }
    dispatch = lambda st: _sc_dispatch(_slabs(st["hp"]), st["dest_rows"], st["n_rows"])
    st_p = _front(x_prompt, mod3[:nbp], p)
    st_s = _front(x_sample, mod3[nbp:], p, after=dispatch(st_p))
    return (_moe(st_p, st_s["after"], mod3[:nbp], p), _moe(st_s, dispatch(st_s), mod3[nbp:], p))
```

```python
import functools
import math

import jax
import jax.numpy as jnp
import numpy as np
from jax import lax
from jax.experimental import pallas as pl
from jax.experimental.pallas import tpu as pltpu
from jax.experimental.pallas import tpu_sc as plsc

F32 = jnp.float32
BF16 = jnp.bfloat16
U32 = jnp.uint32
I32 = jnp.int32

D_MODEL = 1024
HEAD_DIM = 64
A_Q_HEADS = 8
A_KV_HEADS = 2
A_HALF_WINDOW = 128
B_GROUPS = ((128, 1), (512, 4), (2048, 16))
B_HEADS_PER_GROUP = 4
N_BUCKETS = 32
MAX_DISTANCE = 1024
N_EXPERTS = 64
TOP_K = 6
N_EXPERT_GROUPS = 8
TOPK_GROUPS = 4
D_EXPERT = 256
ROUTED_SCALE = 2.5
RMS_EPS = 1e-6
NEG_INF = -1e30
N_MOD = 6
SH1, SC1, G1, SH2, SC2, G2 = range(N_MOD)
REMOVED = -3e38

A_Q_W = A_Q_HEADS * HEAD_DIM
A_KV_W = A_KV_HEADS * HEAD_DIM
B_W = len(B_GROUPS) * B_HEADS_PER_GROUP * HEAD_DIM
B_OUT_W = B_HEADS_PER_GROUP * HEAD_DIM
D_IN = A_Q_W + 2 * A_KV_W + 3 * B_W + 2 * D_MODEL
HALF_D = D_MODEL // 2
LANES = 128
SUBLANES = 8
ROW_SLABS = HALF_D // LANES
SC_CORES = 2
SC_SUBCORES = 16
SC_CHUNK = 128

TM_INPROJ = 1024
TM_MERGE = 512
ATTN_BLOCK = 128
ATTN_ROWS_A = 512
ATTN_ROWS_B = 2048
MOE_BM = 1024
EXPERT_RING = 4
TM_COMBINE = 512
COLLECT_PARTS = 2
PLAN_TILES = 8
VMEM_LIMIT = 48 * 1024 * 1024
VMEM_LIMIT_INPROJ = 60 * 1024 * 1024


def _cparams(n_axes, vmem_limit=VMEM_LIMIT):
    return pltpu.CompilerParams(
        dimension_semantics=("arbitrary",) * n_axes, vmem_limit_bytes=vmem_limit)


def _ada_kernel(c_ref, w_ref, b_ref, o_ref):
    c = c_ref[...]
    s = c * jax.nn.sigmoid(c)
    o_ref[...] = jnp.dot(s, w_ref[...], preferred_element_type=F32,
                         precision=lax.Precision.HIGHEST) + b_ref[...]


def _ada(c_all, w_ada, b_ada):
    nb = c_all.shape[0]
    return pl.pallas_call(
        _ada_kernel,
        out_shape=jax.ShapeDtypeStruct((nb, N_MOD * D_MODEL), F32),
        grid=(N_MOD,),
        in_specs=[pl.BlockSpec((nb, D_MODEL), lambda j: (0, 0)),
                  pl.BlockSpec((D_MODEL, D_MODEL), lambda j: (0, j)),
                  pl.BlockSpec((1, D_MODEL), lambda j: (0, j))],
        out_specs=pl.BlockSpec((nb, D_MODEL), lambda j: (0, j)),
        compiler_params=_cparams(1),
        name="ada",
    )(c_all, w_ada, b_ada.reshape(1, N_MOD * D_MODEL))


def _rms(x, g):
    return x * lax.rsqrt(jnp.mean(x * x, axis=-1, keepdims=True) + RMS_EPS) * g


def _inproj_kernel(x_ref, mod_ref, n1_ref, w_ref, qa, kva, *rest):
    q_b = rest[0:6:2]
    kv_b = rest[1:6:2]
    sga, sgb, scr = rest[6:]
    mod = mod_ref[0]
    h = _rms(x_ref[...], n1_ref[...]) * (1.0 + mod[SC1:SC1 + 1]) + mod[SH1:SH1 + 1]
    hb = h.astype(BF16)
    tm = hb.shape[0]
    scale = HEAD_DIM ** -0.5

    def proj(off, width):
        return jnp.dot(hb, w_ref[:, off:off + width], preferred_element_type=F32)

    qa[...] = (proj(0, A_Q_W) * scale).astype(BF16)
    for j, off in enumerate((A_Q_W, A_Q_W + A_KV_W)):
        r = proj(off, A_KV_W)
        kva[:, 2 * j * A_KV_W:(2 * j + 1) * A_KV_W] = r.astype(BF16)
        kva[:, (2 * j + 1) * A_KV_W:(2 * j + 2) * A_KV_W] = pltpu.roll(r, HEAD_DIM, 1).astype(BF16)
    off = A_Q_W + 2 * A_KV_W
    for t in range(3):
        for gi, (_, d) in enumerate(B_GROUPS):
            r = proj(off + t * B_W + gi * B_OUT_W, B_OUT_W)
            if t == 0:
                r = r * scale
            ref, width, base = (q_b[gi], B_OUT_W, 0) if t == 0 else (kv_b[gi], 2 * B_OUT_W,
                                                                     (t - 1) * B_OUT_W)
            if d == 1:
                ref[:, base:base + B_OUT_W] = r.astype(BF16)
            else:
                for c in range(B_OUT_W // LANES):
                    slot = (t * 2 + gi - 1) * (B_OUT_W // LANES) + c
                    scr[slot] = r[:, c * LANES:(c + 1) * LANES]
                    for res in range(d):
                        col = res * width + base + c * LANES
                        ref[:, col:col + LANES] = scr[
                            slot, pl.ds(res, tm // d, stride=d), :].astype(BF16)
    off += 3 * B_W
    sga[...] = jax.nn.sigmoid(proj(off, D_MODEL)).astype(BF16)
    sgb[...] = jax.nn.sigmoid(proj(off + D_MODEL, D_MODEL)).astype(BF16)


def _inproj(x2, mod3, norm1, w_in_bf, seq):
    n = x2.shape[0]
    tm = TM_INPROJ
    assert seq % tm == 0 and n % tm == 0
    row = lambda i: (i, 0)
    shapes = [(n, A_Q_W, tm), (n, 4 * A_KV_W, tm)]
    for _, d in B_GROUPS:
        shapes += [(n // d, d * B_OUT_W, tm // d), (n // d, d * 2 * B_OUT_W, tm // d)]
    shapes += [(n, D_MODEL, tm)] * 2
    return pl.pallas_call(
        _inproj_kernel,
        out_shape=[jax.ShapeDtypeStruct((r, c), BF16) for r, c, _ in shapes],
        grid=(n // tm,),
        in_specs=[pl.BlockSpec((tm, D_MODEL), row),
                  pl.BlockSpec((1, N_MOD, D_MODEL), lambda i: (i * tm // seq, 0, 0)),
                  pl.BlockSpec((1, D_MODEL), lambda i: (0, 0)),
                  pl.BlockSpec((D_MODEL, D_IN), lambda i: (0, 0), pipeline_mode=pl.Buffered(1))],
        out_specs=[pl.BlockSpec((b, c), row) for _, c, b in shapes],
        scratch_shapes=[pltpu.VMEM((6 * B_OUT_W // LANES, tm, LANES), F32)],
        compiler_params=_cparams(1, VMEM_LIMIT_INPROJ),
        name="inproj",
    )(x2, mod3, norm1.reshape(1, D_MODEL), w_in_bf)


def _rel_bucket_np(rel):
    half = N_BUCKETS // 2
    max_exact = half // 2
    n = np.abs(rel)
    large = max_exact + (np.log(np.maximum(n, 1) / max_exact) / math.log(MAX_DISTANCE / max_exact)
                         * (half - max_exact)).astype(np.int32)
    large = np.minimum(large, half - 1)
    return ((rel > 0).astype(np.int32) * half + np.where(n < max_exact, n, large)).astype(np.int32)


def _bias_table(rel_bias_heads, n_q, n_k, key_off, band, dist_scale):
    p = n_q + n_k
    rel = np.arange(p) - (n_q - 1) - key_off
    bucket = _rel_bucket_np(rel * dist_scale)
    t = jnp.where((np.abs(rel) <= band)[None], rel_bias_heads.astype(F32)[bucket].T, NEG_INF)
    big = jnp.tile(t, (1, n_q + 1))
    tab = big[:, n_q - 1:n_q - 1 + n_q * (p - 1)].reshape(-1, n_q, p - 1)
    return tab[:, :, :n_k]


def _pair_rhs(k_top, k_bot, v_top, v_bot):
    kb = k_top.shape[0]
    low = jnp.where(lax.broadcasted_iota(I32, (kb, LANES), 1) < HEAD_DIM, 1.0, 0.0).astype(BF16)
    high = jnp.where(lax.broadcasted_iota(I32, (kb, LANES), 1) < HEAD_DIM, 0.0, 1.0).astype(BF16)
    rhs_k = jnp.concatenate([k_top * low, k_bot * high], axis=0)
    rhs_v = jnp.concatenate([jnp.concatenate([v_top * low, low], axis=1),
                             jnp.concatenate([v_bot * high, high], axis=1)], axis=0)
    return rhs_k, rhs_v


def _pair_attention(q_pair, rhs_k, rhs_v, bias_pair, pen, sink_pair, want_lse):
    kb = rhs_k.shape[0] // 2
    s = lax.dot_general(q_pair, rhs_k, (((1,), (1,)), ((), ())), preferred_element_type=F32)
    s = s + bias_pair
    if pen is not None:
        s = s + pen
    s0, s1 = s[:, :kb], s[:, kb:]
    m0 = jnp.max(s0, axis=-1, keepdims=True)
    m1 = jnp.max(s1, axis=-1, keepdims=True)
    if sink_pair is not None:
        m0 = jnp.maximum(m0, sink_pair[:, 0:1])
        m1 = jnp.maximum(m1, sink_pair[:, HEAD_DIM:HEAD_DIM + 1])
    p = jnp.concatenate([jnp.exp(s0 - m0), jnp.exp(s1 - m1)], axis=1).astype(BF16)
    od = jnp.dot(p, rhs_v, preferred_element_type=F32)
    o, den = od[:, :LANES], od[:, LANES:]
    low = lax.broadcasted_iota(I32, o.shape, 1) < HEAD_DIM
    m_full = jnp.where(low, m0, m1)
    if sink_pair is not None:
        den = den + jnp.exp(sink_pair - m_full)
    return o / den, (m_full + jnp.log(den)) if want_lse else None


def _attn_kernel(*refs, halo, nblk, n_pairs, n_seqs, shared_kv, has_sink, want_lse, n_steps):
    it = iter(refs)
    q_ref = next(it)
    kv_prev, kv_cur, kv_next = next(it), next(it), next(it)
    bias_ref = next(it)
    sink_ref = next(it) if has_sink else None
    o_ref = next(it)
    lse_ref = next(it) if want_lse else None
    step = pl.program_id(2)
    blk = ATTN_BLOCK
    kb = blk + 2 * halo
    qw = n_pairs * LANES
    kv = jnp.concatenate([kv_prev[...], kv_cur[...], kv_next[...]], axis=0)
    col = lax.broadcasted_iota(I32, (1, kb), 1)
    for b in range(nblk):
        lo = jnp.where(step == 0, halo, 0) if b == 0 else 0
        hi = jnp.where(step == n_steps - 1, kb - halo, kb) if b == nblk - 1 else kb
        pen = None
        if b == 0 or b == nblk - 1:
            pen1 = jnp.where((col >= lo) & (col < hi), 0.0, NEG_INF)
            pen = jnp.concatenate([pen1, pen1], axis=1)
        rows = slice(b * blk, (b + 1) * blk)
        krows = slice(b * blk, b * blk + kb)
        lane_col = lambda j: kv[krows, j * LANES:(j + 1) * LANES]
        if shared_kv:
            k, k_sw, v, v_sw = (lane_col(j) for j in range(4))
            rhs = [_pair_rhs(k, k_sw, v, v_sw), _pair_rhs(k_sw, k, v_sw, v)]
        for s in range(n_seqs):
            for c in range(n_pairs):
                lanes = slice(s * qw + c * LANES, s * qw + (c + 1) * LANES)
                if shared_kv:
                    rhs_k, rhs_v = rhs[c // (n_pairs // 2)]
                else:
                    k, v = lane_col(2 * s * n_pairs + c), lane_col((2 * s + 1) * n_pairs + c)
                    rhs_k, rhs_v = _pair_rhs(k, k, v, v)
                o, lse = _pair_attention(q_ref[rows, lanes], rhs_k, rhs_v, bias_ref[c], pen,
                                         sink_ref[c] if has_sink else None, want_lse)
                o_ref[rows, lanes] = o.astype(o_ref.dtype)
                if want_lse:
                    lse_ref[rows, lanes] = lse


def _band_attention(q, kv, bias, sink, *, nbatch, seq, dil, halo, step_rows, shared_kv, want_lse):
    n = nbatch * seq
    sub_len = seq // dil
    qw = q.shape[1] // dil
    kv_width = kv.shape[1] // dil
    rows = min(step_rows, sub_len)
    n_seqs = min(dil, step_rows // rows)
    assert sub_len % rows == 0 and rows % ATTN_BLOCK == 0 and rows % halo == 0 and dil % n_seqs == 0
    nq = sub_len // rows
    per = rows // halo
    total_halos = n // dil // halo
    cur = lambda b, r, i: (b * nq + i, r)
    prev = lambda b, r, i: (jnp.maximum((b * nq + i) * per - 1, 0), r)
    nxt = lambda b, r, i: (jnp.minimum((b * nq + i + 1) * per, total_halos - 1), r)
    const = lambda a: pl.BlockSpec(a.shape, lambda b, r, i: (0,) * a.ndim)
    qw_step, kvw_step = qw * n_seqs, kv_width * n_seqs

    in_specs = [pl.BlockSpec((rows, qw_step), cur), pl.BlockSpec((halo, kvw_step), prev),
                pl.BlockSpec((rows, kvw_step), cur), pl.BlockSpec((halo, kvw_step), nxt),
                const(bias)]
    args = [q, kv, kv, kv, bias]
    if sink is not None:
        in_specs.append(const(sink))
        args.append(sink)
    out_shape = [jax.ShapeDtypeStruct(q.shape, BF16)]
    out_specs = [pl.BlockSpec((rows, qw_step), cur)]
    if want_lse:
        out_shape.append(jax.ShapeDtypeStruct(q.shape, F32))
        out_specs.append(pl.BlockSpec((rows, qw_step), cur))
    return pl.pallas_call(
        functools.partial(_attn_kernel, halo=halo, nblk=rows // ATTN_BLOCK, n_pairs=qw // LANES,
                          n_seqs=n_seqs, shared_kv=shared_kv, has_sink=sink is not None,
                          want_lse=want_lse, n_steps=nq),
        out_shape=out_shape,
        grid=(nbatch, dil // n_seqs, nq),
        in_specs=in_specs,
        out_specs=out_specs,
        compiler_params=_cparams(3),
        name=f"band_attn_d{dil}",
    )(*args)


def _pack_rows(y):
    return pltpu.pack_elementwise([y[:, :HALF_D], y[:, HALF_D:]], packed_dtype=BF16)


def _store_slabs(ref, packed):
    t = packed.shape[0]
    for c in range(ROW_SLABS):
        ref[pl.ds(c, t, stride=ROW_SLABS), :] = packed[:, c * LANES:(c + 1) * LANES]


def _load_slabs(ref):
    t = ref.shape[0] // ROW_SLABS
    return jnp.concatenate([ref[pl.ds(c, t, stride=ROW_SLABS), :] for c in range(ROW_SLABS)], axis=1)


def _unpack_rows(p):
    return tuple(pltpu.unpack_elementwise(p, index=i, packed_dtype=BF16, unpacked_dtype=F32)
                 for i in range(2))


def _route(sel, scores):
    t = sel.shape[-1]
    per = N_EXPERTS // N_EXPERT_GROUPS
    shape3 = (N_EXPERT_GROUPS, per, t)
    sel3 = sel.reshape(shape3)
    sc3 = scores.reshape(shape3)
    iota_g = lax.broadcasted_iota(I32, shape3, 0)
    iota_m = lax.broadcasted_iota(I32, shape3, 1)
    iota_e = iota_g * per + iota_m
    m1 = jnp.max(sel3, axis=1, keepdims=True)
    i1 = jnp.min(jnp.where(sel3 == m1, iota_m, per), axis=1, keepdims=True)
    m2 = jnp.max(jnp.where(iota_m == i1, REMOVED, sel3), axis=1, keepdims=True)
    gscore = m1 + m2
    iota_g1 = lax.broadcasted_iota(I32, gscore.shape, 0)
    gmask = jnp.zeros(gscore.shape, jnp.bool_)
    for _ in range(TOPK_GROUPS):
        mx = jnp.max(gscore, axis=0, keepdims=True)
        ix = jnp.min(jnp.where(gscore == mx, iota_g1, N_EXPERT_GROUPS), axis=0, keepdims=True)
        hit = iota_g1 == ix
        gmask = gmask | hit
        gscore = jnp.where(hit, REMOVED, gscore)
    cur = jnp.where(gmask, sel3, NEG_INF)
    sum_all = lambda a: jnp.sum(jnp.sum(a, axis=1, keepdims=True), axis=0, keepdims=True)
    idxs, wts, hits = [], [], []
    for _ in range(TOP_K):
        mx = jnp.max(jnp.max(cur, axis=1, keepdims=True), axis=0, keepdims=True)
        cand = jnp.where(cur == mx, iota_e, N_EXPERTS)
        ix = jnp.min(jnp.min(cand, axis=1, keepdims=True), axis=0, keepdims=True)
        hit = iota_e == ix
        wts.append(sum_all(jnp.where(hit, sc3, 0.0)).reshape(1, t))
        cur = jnp.where(hit, REMOVED, cur)
        idxs.append(ix.reshape(1, t))
        hits.append(hit)
    wsum = wts[0]
    for w in wts[1:]:
        wsum = wsum + w
    wts = [w / wsum * ROUTED_SCALE for w in wts]
    onehot = hits[0].astype(F32)
    for hit in hits[1:]:
        onehot = onehot + hit.astype(F32)
    tri = (lax.broadcasted_iota(I32, (t, t), 0) <= lax.broadcasted_iota(I32, (t, t), 1))
    cum = jnp.dot(onehot.reshape(N_EXPERTS, t).astype(BF16), tri.astype(F32).astype(BF16),
                  preferred_element_type=F32)
    cum3 = cum.reshape(shape3) - 1.0
    ranks = [sum_all(jnp.where(hit, cum3, 0.0)).reshape(1, t).astype(I32) for hit in hits]
    counts = cum[:, t - 1:t].astype(I32)
    return idxs, wts, ranks, counts


def _merge_kernel(x_ref, ya_ref, o1, o2, o3, l1, l2, l3, sga_ref, sgb_ref, mod_ref, n2_ref,
                  wpa, wpb, wo, wrt, rbias, wsg, wsu, wsd,
                  base_ref, hp_ref, ridx_ref, rw_ref, rank_ref, cnt_ref, scr):
    mod = mod_ref[0]
    tm = x_ref.shape[0]

    def token_major(ref, slot, d):
        if d == 1:
            return ref[...].astype(F32)
        nc = B_OUT_W // LANES
        for c in range(nc):
            for res in range(d):
                col = res * B_OUT_W + c * LANES
                scr[slot * nc + c, pl.ds(res, tm // d, stride=d), :] = ref[
                    :, col:col + LANES].astype(F32)
        return jnp.concatenate([scr[slot * nc + c] for c in range(nc)], axis=1)

    dils = [d for _, d in B_GROUPS]
    os_ = [token_major(r, i, d) for i, (r, d) in enumerate(zip((o1, o2, o3), dils))]
    ls = [token_major(r, 3 + i, d) for i, (r, d) in enumerate(zip((l1, l2, l3), dils))]
    mx = jnp.maximum(jnp.maximum(ls[0], ls[1]), ls[2])
    es = [jnp.exp(l - mx) for l in ls]
    den = es[0] + es[1] + es[2]
    ob = (es[0] / den) * os_[0] + (es[1] / den) * os_[1] + (es[2] / den) * os_[2]
    pa = jnp.dot(ya_ref[...], wpa[...], preferred_element_type=F32)
    pb = jnp.dot(ob.astype(BF16), wpb[...], preferred_element_type=F32)
    merged = sga_ref[...].astype(F32) * pa + sgb_ref[...].astype(F32) * pb
    mix = jnp.dot(merged.astype(BF16), wo[...], preferred_element_type=F32)
    x1 = x_ref[...] + mod[G1:G1 + 1] * mix
    h2 = _rms(x1, n2_ref[...]) * (1.0 + mod[SC2:SC2 + 1]) + mod[SH2:SH2 + 1]
    logits = lax.dot_general(wrt[...], h2, (((1,), (1,)), ((), ())), preferred_element_type=F32,
                             precision=lax.Precision.HIGHEST)
    scores = jax.nn.sigmoid(logits)
    idxs, wts, ranks, counts = _route(scores + rbias[...], scores)
    for k in range(TOP_K):
        ridx_ref[k:k + 1, :] = idxs[k]
        rw_ref[k:k + 1, :] = wts[k]
        rank_ref[k:k + 1, :] = ranks[k]
    ridx_ref[TOP_K:, :] = jnp.zeros((SUBLANES - TOP_K, tm), I32)
    rw_ref[TOP_K:, :] = jnp.zeros((SUBLANES - TOP_K, tm), F32)
    rank_ref[TOP_K:, :] = jnp.zeros((SUBLANES - TOP_K, tm), I32)
    cnt_ref[0] = counts
    hb = h2.astype(BF16)
    g = jnp.dot(hb, wsg[...], preferred_element_type=F32)
    u = jnp.dot(hb, wsu[...], preferred_element_type=F32)
    act = (g * jax.nn.sigmoid(g) * u).astype(BF16)
    shared = jnp.dot(act, wsd[...], preferred_element_type=F32)
    base_ref[...] = x1 + mod[G2:G2 + 1] * shared
    _store_slabs(hp_ref, _pack_rows(h2))


def _merge(x2, ya, outs, lses, sga, sgb, mod3, norm2, wpa, wpb, wo, wrt, rbias, wsg, wsu, wsd, seq):
    n = x2.shape[0]
    tm = TM_MERGE
    assert seq % tm == 0
    row = lambda i: (i, 0)
    full = lambda a: pl.BlockSpec(a.shape, lambda i: (0,) * a.ndim)
    weights = [wpa, wpb, wo, wrt, rbias, wsg, wsu, wsd]
    group_specs = [pl.BlockSpec((tm // d, d * B_OUT_W), row) for _, d in B_GROUPS]
    lanes = lambda i: (0, i)
    return pl.pallas_call(
        _merge_kernel,
        out_shape=[jax.ShapeDtypeStruct((n, D_MODEL), F32),
                   jax.ShapeDtypeStruct((n * ROW_SLABS, LANES), U32),
                   jax.ShapeDtypeStruct((SUBLANES, n), I32),
                   jax.ShapeDtypeStruct((SUBLANES, n), F32),
                   jax.ShapeDtypeStruct((SUBLANES, n), I32),
                   jax.ShapeDtypeStruct((n // tm, N_EXPERTS, 1), I32)],
        grid=(n // tm,),
        in_specs=[pl.BlockSpec((tm, D_MODEL), row), pl.BlockSpec((tm, A_Q_W), row)]
                 + group_specs * 2
                 + [pl.BlockSpec((tm, D_MODEL), row)] * 2
                 + [pl.BlockSpec((1, N_MOD, D_MODEL), lambda i: (i * tm // seq, 0, 0)),
                    pl.BlockSpec((1, D_MODEL), lambda i: (0, 0))]
                 + [full(w) for w in weights],
        out_specs=[pl.BlockSpec((tm, D_MODEL), row),
                   pl.BlockSpec((tm * ROW_SLABS, LANES), row),
                   pl.BlockSpec((SUBLANES, tm), lanes), pl.BlockSpec((SUBLANES, tm), lanes),
                   pl.BlockSpec((SUBLANES, tm), lanes),
                   pl.BlockSpec((1, N_EXPERTS, 1), lambda i: (i, 0, 0))],
        scratch_shapes=[pltpu.VMEM((6 * B_OUT_W // LANES, tm, LANES), F32)],
        compiler_params=_cparams(1),
        name="merge_route",
    )(x2, ya, *outs, *lses, sga, sgb, mod3, norm2.reshape(1, D_MODEL), *weights)


def _plan_kernel(ridx_ref, rank_ref, tb_ref, dest_ref, *, tm):
    iota_e = lax.broadcasted_iota(I32, (N_EXPERTS, tm), 0)
    for j in range(dest_ref.shape[0]):
        tb = tb_ref[j]
        cols = slice(j * tm, (j + 1) * tm)
        for k in range(TOP_K):
            base = jnp.sum(jnp.where(iota_e == ridx_ref[k:k + 1, cols], tb, 0), axis=0, keepdims=True)
            dest_ref[j, :, k * tm:(k + 1) * tm] = base + rank_ref[k:k + 1, cols]


def _plan(ridx, rank, tile_base, tm):
    n = ridx.shape[1]
    tiles = math.gcd(PLAN_TILES, n // tm)
    lanes = lambda i: (0, i)
    return pl.pallas_call(
        functools.partial(_plan_kernel, tm=tm),
        out_shape=jax.ShapeDtypeStruct((n // tm, 1, TOP_K * tm), I32),
        grid=(n // tm // tiles,),
        in_specs=[pl.BlockSpec((SUBLANES, tiles * tm), lanes), pl.BlockSpec((SUBLANES, tiles * tm), lanes),
                  pl.BlockSpec((tiles, N_EXPERTS, 1), lambda i: (i, 0, 0))],
        out_specs=pl.BlockSpec((tiles, 1, TOP_K * tm), lambda i: (i, 0, 0)),
        compiler_params=_cparams(1),
        name="moe_plan",
    )(ridx, rank, tile_base)


def _sc_mesh():
    return plsc.VectorSubcoreMesh(core_axis_name="core", subcore_axis_name="subcore",
                                  num_cores=SC_CORES, num_subcores=SC_SUBCORES)


def _sc_worker():
    return lax.axis_index("subcore") * SC_CORES + lax.axis_index("core")


def _sc_dispatch(hp, dest_rows, n_rows):
    n = hp.shape[0]
    halves = TM_MERGE // SC_CHUNK
    workers = SC_CORES * SC_SUBCORES
    tiles_per_w = n // TM_MERGE // workers
    idx_per_w = tiles_per_w * TOP_K * halves
    assert tiles_per_w * workers * TM_MERGE == n

    @functools.partial(
        pl.kernel, mesh=_sc_mesh(),
        out_type=jax.ShapeDtypeStruct((n_rows, ROW_SLABS, LANES), U32),
        scratch_types=[pltpu.VMEM((idx_per_w, SC_CHUNK), I32),
                       pltpu.VMEM((SC_CHUNK, ROW_SLABS, LANES), U32),
                       pltpu.SemaphoreType.DMA],
        name="moe_dispatch_sc")
    def scatter(hp_hbm, idx_hbm, xs_hbm, idx_v, rows_v, sem):
        wid = _sc_worker()
        pltpu.sync_copy(idx_hbm.at[pl.ds(wid * idx_per_w, idx_per_w)], idx_v)

        @pl.loop(0, tiles_per_w * halves)
        def _(j):
            tile = j // halves
            h = j - tile * halves
            tok = (wid * tiles_per_w + tile) * TM_MERGE + h * SC_CHUNK
            pltpu.sync_copy(hp_hbm.at[pl.ds(tok, SC_CHUNK)], rows_v)
            for k in range(TOP_K):
                row = (tile * TOP_K + k) * halves + h
                pltpu.async_copy(rows_v, xs_hbm.at[idx_v.at[row]], sem).wait()

    return scatter(hp, dest_rows)


def _sc_collect(ys, dest_rows):
    n_chunks = dest_rows.shape[0]
    workers = SC_CORES * SC_SUBCORES
    per_w = n_chunks // workers
    assert per_w * workers == n_chunks

    @functools.partial(
        pl.kernel, mesh=_sc_mesh(),
        out_type=jax.ShapeDtypeStruct((n_chunks * SC_CHUNK, ROW_SLABS, LANES), U32),
        scratch_types=[pltpu.VMEM((per_w, SC_CHUNK), I32),
                       pltpu.VMEM((SC_CHUNK, ROW_SLABS, LANES), U32),
                       pltpu.SemaphoreType.DMA],
        name="moe_collect_sc")
    def gather(ys_hbm, idx_hbm, out_hbm, idx_v, rows_v, sem):
        wid = _sc_worker()
        base = wid * per_w
        pltpu.sync_copy(idx_hbm.at[pl.ds(base, per_w)], idx_v)

        @pl.loop(0, per_w)
        def _(j):
            pltpu.async_copy(ys_hbm.at[idx_v.at[j]], rows_v, sem).wait()
            pltpu.sync_copy(rows_v, out_hbm.at[pl.ds((base + j) * SC_CHUNK, SC_CHUNK)])

    return gather(ys, dest_rows)


def _expert_kernel(blk_e_ref, valid_ref, src_ref, xs_hbm, wg_ref, wu_ref, wd_ref, ys_hbm,
                   wg_bf, wu_bf, wd_bf, xbuf, sem, obuf, osem, *, nblk):
    step = pl.program_id(0)
    valid = valid_ref[step]
    rows = xbuf.shape[1]

    def fetch(t):
        slot = lax.rem(t, EXPERT_RING)
        return pltpu.make_async_copy(xs_hbm.at[pl.ds(pl.multiple_of(t * rows, rows), rows)],
                                     xbuf.at[slot], sem.at[slot])

    def request(t):
        @pl.when((t < nblk) & (valid_ref[jnp.minimum(t, nblk - 1)] > 0))
        def _():
            fetch(t).start()

    @pl.when(step == 0)
    def _():
        for t in range(EXPERT_RING - 1):
            request(jnp.int32(t))

    request(step + EXPERT_RING - 1)

    def store(t):
        slot = lax.rem(t, EXPERT_RING)
        return pltpu.make_async_copy(obuf.at[slot],
                                     ys_hbm.at[pl.ds(pl.multiple_of(t * rows, rows), rows)],
                                     osem.at[slot])

    @pl.when((step == 0) | (blk_e_ref[step] != blk_e_ref[jnp.maximum(step - 1, 0)]))
    def _():
        wg_bf[...] = wg_ref[0].astype(BF16)
        wu_bf[...] = wu_ref[0].astype(BF16)
        wd_bf[...] = wd_ref[0].astype(BF16)

    @pl.when(valid > 0)
    def _():
        fetch(step).wait()

        @pl.when(step >= EXPERT_RING)
        def _():
            store(step - EXPERT_RING).wait()

        xs_ref = xbuf.at[lax.rem(step, EXPERT_RING)]
        ys_ref = obuf.at[lax.rem(step, EXPERT_RING)]
        row_id = lax.broadcasted_iota(I32, (rows // ROW_SLABS, 1), 0)
        lo, hi = _unpack_rows(jnp.where(row_id < valid, _load_slabs(xs_ref), jnp.uint32(0)))
        lo, hi = lo.astype(BF16), hi.astype(BF16)

        def up(w_bf):
            return jnp.dot(lo, w_bf[:HALF_D], preferred_element_type=F32) \
                + jnp.dot(hi, w_bf[HALF_D:], preferred_element_type=F32)

        g = up(wg_bf)
        act = (g * jax.nn.sigmoid(g) * up(wu_bf)).astype(BF16)
        _store_slabs(ys_ref, _pack_rows(jnp.dot(act, wd_bf[...], preferred_element_type=F32)))
        store(step).start()

    @pl.when(step == nblk - 1)
    def _():
        n_used = src_ref[nblk - 1] + 1
        for j in range(EXPERT_RING):
            @pl.when(n_used - 1 - j >= 0)
            def _(j=j):
                store(n_used - 1 - j).wait()


def _experts(xs, blk_e, blk_valid, blk_src, wg, wu, wd):
    n_rows = xs.shape[0] // ROW_SLABS
    bm = MOE_BM
    nblk = n_rows // bm
    return pl.pallas_call(
        functools.partial(_expert_kernel, nblk=nblk),
        out_shape=jax.ShapeDtypeStruct((n_rows * ROW_SLABS, LANES), U32),
        grid_spec=pltpu.PrefetchScalarGridSpec(
            num_scalar_prefetch=3,
            grid=(nblk,),
            in_specs=[pl.BlockSpec(memory_space=pl.ANY),
                      pl.BlockSpec((1, D_MODEL, D_EXPERT), lambda i, be, bv, bs: (be[i], 0, 0)),
                      pl.BlockSpec((1, D_MODEL, D_EXPERT), lambda i, be, bv, bs: (be[i], 0, 0)),
                      pl.BlockSpec((1, D_EXPERT, D_MODEL), lambda i, be, bv, bs: (be[i], 0, 0))],
            out_specs=pl.BlockSpec(memory_space=pl.ANY),
            scratch_shapes=[pltpu.VMEM((D_MODEL, D_EXPERT), BF16), pltpu.VMEM((D_MODEL, D_EXPERT), BF16),
                            pltpu.VMEM((D_EXPERT, D_MODEL), BF16),
                            pltpu.VMEM((EXPERT_RING, bm * ROW_SLABS, LANES), U32),
                            pltpu.SemaphoreType.DMA((EXPERT_RING,)),
                            pltpu.VMEM((EXPERT_RING, bm * ROW_SLABS, LANES), U32),
                            pltpu.SemaphoreType.DMA((EXPERT_RING,))]),
        compiler_params=_cparams(1),
        name="moe_experts",
    )(blk_e, blk_valid, blk_src, xs, wg, wu, wd)


def _combine_kernel(g_ref, base_ref, rw_ref, mod_ref, fn_ref, *rest):
    y_ref = rest[-1]
    tm = base_ref.shape[0]
    rw = rw_ref[...]
    acc_lo = jnp.zeros((tm, HALF_D), F32)
    acc_hi = jnp.zeros((tm, HALF_D), F32)
    for k in range(TOP_K):
        lo, hi = _unpack_rows(_load_slabs(g_ref.at[0, k]))
        w = rw[:, k:k + 1]
        acc_lo = acc_lo + w * lo
        acc_hi = acc_hi + w * hi
    routed = jnp.concatenate([acc_lo, acc_hi], axis=1)
    x2 = base_ref[...] + mod_ref[0][G2:G2 + 1] * routed
    y_ref[...] = _rms(x2, fn_ref[...])


def _combine(g, base, rw_t, mod3, final_norm, seq, part, y_prev):
    n = base.shape[0]
    tm = TM_COMBINE
    assert seq % tm == 0
    g4 = g.reshape(-1, TOP_K, tm * ROW_SLABS, LANES)
    tiles = g4.shape[0]
    off = part * tiles
    row = lambda i: (i + off, 0)
    in_specs = [pl.BlockSpec((1, TOP_K, tm * ROW_SLABS, LANES), lambda i: (i, 0, 0, 0)),
                pl.BlockSpec((tm, D_MODEL), row),
                pl.BlockSpec((tm, SUBLANES), row),
                pl.BlockSpec((1, N_MOD, D_MODEL), lambda i: ((i + off) * tm // seq, 0, 0)),
                pl.BlockSpec((1, D_MODEL), lambda i: (0, 0))]
    args = [g4, base, rw_t, mod3, final_norm.reshape(1, D_MODEL)]
    aliases = {}
    if y_prev is not None:
        in_specs.append(pl.BlockSpec(memory_space=pl.ANY))
        args.append(y_prev)
        aliases = {len(args) - 1: 0}
    return pl.pallas_call(
        _combine_kernel,
        out_shape=jax.ShapeDtypeStruct((n, D_MODEL), F32),
        grid=(tiles,),
        in_specs=in_specs,
        out_specs=pl.BlockSpec((tm, D_MODEL), row),
        input_output_aliases=aliases,
        compiler_params=_cparams(1),
        name="moe_combine",
    )(*args)


def _block_layout(tile_counts, n):
    bm = MOE_BM
    c = tile_counts[:, :, 0]
    ntiles = c.shape[0]
    counts = jnp.sum(c, axis=0)
    padded = (counts + bm - 1) // bm * bm
    earlier_e = np.tri(N_EXPERTS, k=-1, dtype=bool)
    pstart = jnp.sum(jnp.where(earlier_e, padded[None, :], 0), axis=1)
    pend = pstart + padded
    earlier_t = np.tri(ntiles, k=-1, dtype=bool)
    tile_base = pstart[None, :] + jnp.sum(jnp.where(earlier_t[:, :, None], c[None], 0), axis=1)
    nblk = -(-n * TOP_K // bm) + N_EXPERTS
    blk = jnp.arange(nblk, dtype=I32)
    blk_e = jnp.minimum(jnp.sum((pend[None, :] <= blk[:, None] * bm).astype(I32), axis=1),
                        N_EXPERTS - 1)
    onehot_e = blk_e[:, None] == jnp.arange(N_EXPERTS, dtype=I32)[None, :]
    end_e = jnp.sum(jnp.where(onehot_e, (pstart + counts)[None, :], 0), axis=1)
    blk_valid = jnp.clip(end_e - blk * bm, 0, bm).astype(I32)
    blk_src = jnp.minimum(blk, pend[N_EXPERTS - 1] // bm - 1).astype(I32)
    last_e = jnp.max(jnp.where(counts > 0, jnp.arange(N_EXPERTS, dtype=I32), 0))
    blk_e = jnp.where(blk == blk_src, blk_e, last_e).astype(I32)
    return tile_base.astype(I32)[:, :, None], blk_e, blk_valid, blk_src, nblk * bm


def _front(x, mod3, p, after=None):
    nbatch, seq, _ = x.shape
    n = nbatch * seq
    x2 = x.reshape(n, D_MODEL)
    proj = _inproj(x2, mod3, p["norm1"], p["w_in"], seq)
    qa, kva = proj[:2]
    sga, sgb = proj[8:]
    (ya,) = _band_attention(qa, kva, p["bias_a"], p["sink_a"], nbatch=nbatch, seq=seq, dil=1,
                            halo=A_HALF_WINDOW, step_rows=ATTN_ROWS_A, shared_kv=True, want_lse=False)
    outs, lses = [], []
    for gi, (w, d) in enumerate(B_GROUPS):
        q, kv = proj[2 + 2 * gi:4 + 2 * gi]
        o, lse = _band_attention(q, kv, p["bias_b"][gi], None, nbatch=nbatch, seq=seq, dil=d,
                                 halo=w // (2 * d), step_rows=ATTN_ROWS_B, shared_kv=False,
                                 want_lse=True)
        outs.append(o)
        lses.append(lse)
    if after is not None:
        ya, after = lax.optimization_barrier((ya, after))
    base, hp, ridx, rw, rank, tile_counts = _merge(
        x2, ya, outs, lses, sga, sgb, mod3, p["norm2"], p["w_pa"], p["w_pb"], p["w_o"], p["w_rt"],
        p["rbias"], p["ws_gate"], p["ws_up"], p["ws_down"], seq)
    assert TM_MERGE == TM_COMBINE
    tile_base, blk_e, blk_valid, blk_src, n_rows = _block_layout(tile_counts, n)
    dest_rows = _plan(ridx, rank, tile_base, TM_MERGE).reshape(-1, SC_CHUNK)
    return dict(base=base, hp=hp, rw=rw, dest_rows=dest_rows, blk_e=blk_e, blk_valid=blk_valid,
                blk_src=blk_src, n_rows=n_rows, shape=x.shape, after=after)


def _slabs(a):
    return a.reshape(-1, ROW_SLABS, LANES)


def _moe(st, xs, mod3, p):
    nbatch, seq, _ = st["shape"]
    dest_rows = st["dest_rows"]
    ys = _experts(xs.reshape(-1, LANES), st["blk_e"], st["blk_valid"], st["blk_src"], p["w_gate"], p["w_up"],
                  p["w_down"])
    rows_per_part = dest_rows.shape[0] // COLLECT_PARTS
    rw_t = st["rw"].T
    y = None
    for part in range(COLLECT_PARTS):
        g = _sc_collect(_slabs(ys), dest_rows[part * rows_per_part:(part + 1) * rows_per_part])
        y = _combine(g, st["base"], rw_t, mod3, p["final_norm"], seq, part, y)
    return y.reshape(nbatch, seq, D_MODEL)


def kernel(x_prompt, x_sample, c_prompt, c_sample, rel_bias, w_ada, b_ada, norm1, w_in, sink, w_pa, w_pb, w_o, norm2, w_router, router_bias, w_gate, w_up, w_down, ws_gate, ws_up, ws_down, final_norm):
    assert w_ada.shape[0] == 1
    nbp = x_prompt.shape[0]
    mod = _ada(jnp.concatenate([c_prompt, c_sample], axis=0), w_ada[0], b_ada[0])
    mod3 = mod.reshape(-1, N_MOD, D_MODEL)

    def pair_bias(heads, halo, dist_scale):
        kb = ATTN_BLOCK + 2 * halo
        tab = _bias_table(heads, ATTN_BLOCK, kb, halo, halo, dist_scale)
        return tab.reshape(-1, 2, ATTN_BLOCK, kb).transpose(0, 2, 1, 3).reshape(-1, ATTN_BLOCK, 2 * kb)

    bias_a = pair_bias(rel_bias[:, :A_Q_HEADS], A_HALF_WINDOW, 1)
    bias_b = []
    for gi, (w, d) in enumerate(B_GROUPS):
        h0 = A_Q_HEADS + gi * B_HEADS_PER_GROUP
        bias_b.append(pair_bias(rel_bias[:, h0:h0 + B_HEADS_PER_GROUP], w // (2 * d), d))
    p = {
        "norm1": norm1[0], "norm2": norm2[0], "final_norm": final_norm,
        "w_in": w_in[0].astype(BF16),
        "bias_a": bias_a,
        "sink_a": jnp.repeat(sink[0].astype(F32), HEAD_DIM).reshape(A_Q_HEADS // 2, 1, LANES),
        "bias_b": bias_b,
        "w_pa": w_pa[0].astype(BF16), "w_pb": w_pb[0].astype(BF16), "w_o": w_o[0].astype(BF16),
        "w_rt": w_router[0].T, "rbias": router_bias[0].reshape(N_EXPERTS, 1),
        "ws_gate": ws_gate[0].astype(BF16), "ws_up": ws_up[0].astype(BF16),
        "ws_down": ws_down[0].astype(BF16),
        "w_gate": w_gate[0], "w_up": w_up[0], "w_down": w_down[0],
    }
    dispatch = lambda st: _sc_dispatch(_slabs(st["hp"]), st["dest_rows"], st["n_rows"])
    st_p = _front(x_prompt, mod3[:nbp], p)
    st_s = _front(x_sample, mod3[nbp:], p, after=dispatch(st_p))
    return (_moe(st_p, st_s["after"], mod3[:nbp], p), _moe(st_s, dispatch(st_s), mod3[nbp:], p))
```
